```python
import jax
import jax.numpy as jnp
from jax import lax
import numpy as np

D_MODEL = 1024
BATCH = 8
SEQ = 4096
DEPTH = 1

GRID_W = 64
CTX_LEN = 256

CONV_CH = 512
CONV_GROUPS = 8
GLA_HEADS = 4
GLA_DK = 64
GLA_DV = 128
GLA_KEY = GLA_HEADS * GLA_DK
GLA_VAL = GLA_HEADS * GLA_DV
GLA_GATE_RANK = 16
GLA_TAU = 16.0
GLA_CHUNK = 16
MIX_WIDTH = CONV_CH + GLA_VAL

OFF_AB = 0
OFF_AC = OFF_AB + CONV_CH
OFF_AX = OFF_AC + CONV_CH
OFF_Q = OFF_AX + CONV_CH
OFF_K = OFF_Q + GLA_KEY
OFF_V = OFF_K + GLA_KEY
OFF_R = OFF_V + GLA_VAL
OFF_GF = OFF_R + GLA_VAL
OFF_GB = OFF_GF + GLA_GATE_RANK
D_PROJ = OFF_GB + GLA_GATE_RANK

N_GROUPS = 4
EXPERTS_PER_GROUP = 4
N_EXPERTS = N_GROUPS * EXPERTS_PER_GROUP
TOP_K_IN_GROUP = 2
D_EXPERT = 512

LN_EPS = 1e-5
RMS_EPS = 1e-6
DEEPNORM_ALPHA = (2.0 * DEPTH) ** 0.25
DEEPNORM_BETA = (8.0 * DEPTH) ** -0.25

kernel_name = "hymba_conv_gla_hmoe_diffusion_layer"


def layer_norm(x, g, b):
    xf = x.astype(jnp.float32)
    mu = jnp.mean(xf, axis=-1, keepdims=True)
    var = jnp.mean(jnp.square(xf - mu), axis=-1, keepdims=True)
    y = (xf - mu) * lax.rsqrt(var + LN_EPS)
    return (y * g.astype(jnp.float32) + b.astype(jnp.float32)).astype(x.dtype)


def rms_norm(x, g):
    xf = x.astype(jnp.float32)
    y = xf * lax.rsqrt(jnp.mean(jnp.square(xf), axis=-1, keepdims=True) + RMS_EPS)
    return y * g.astype(jnp.float32)


def ada_params(cond, w_ada, b_ada):
    return jnp.split(jax.nn.silu(cond) @ w_ada + b_ada, 6, axis=-1)


def modulate(x, shift, scale):
    return x * (1.0 + scale) + shift


def post_norm_residual(x, y, gate, g, b):
    return layer_norm(DEEPNORM_ALPHA * x + gate * y, g, b)


def dwconv3(u, w, b):
    n = u.shape[-2]
    pad = [(0, 0)] * (u.ndim - 2) + [(1, 1), (0, 0)]
    up = jnp.pad(u, pad)
    return up[..., 0:n, :] * w[0] + up[..., 1:n + 1, :] * w[1] + up[..., 2:n + 2, :] * w[2] + b


def to_heads(a, d):
    bsz, t, _ = a.shape
    return a.reshape(bsz, t, GLA_HEADS, d).transpose(0, 2, 1, 3).astype(jnp.float32)


def log_decay(low, w2, b2):
    return jax.nn.log_sigmoid((low @ w2 + b2).astype(jnp.float32)) / GLA_TAU


def gla_chunked(q, k, v, g, s0):
    bsz, h, t, dk = q.shape
    dv = v.shape[-1]
    n = t // GLA_CHUNK
    q, k, g = (a.reshape(bsz, h, n, GLA_CHUNK, dk) for a in (q, k, g))
    v = v.reshape(bsz, h, n, GLA_CHUNK, dv)
    gc = jnp.cumsum(g, axis=3)
    g_last = gc[:, :, :, -1:, :]
    q_dec = q * jnp.exp(gc)
    k_inv = k * jnp.exp(-gc)
    k_end = k * jnp.exp(g_last - gc)
    lower = jnp.tril(jnp.ones((GLA_CHUNK, GLA_CHUNK), dtype=bool))
    scores = jnp.where(lower, jnp.einsum("bhncd,bhnsd->bhncs", q_dec, k_inv), 0.0)
    o_intra = jnp.einsum("bhncs,bhnse->bhnce", scores, v)
    u = jnp.einsum("bhncd,bhnce->bhnde", k_end, v)
    decay = jnp.exp(g_last[:, :, :, 0, :])

    def step(s, inp):
        dec_n, u_n = inp
        return dec_n[..., None] * s + u_n, s

    s_final, s_start = lax.scan(step, s0, (jnp.moveaxis(decay, 2, 0), jnp.moveaxis(u, 2, 0)))
    s_start = jnp.moveaxis(s_start, 0, 2)
    o_inter = jnp.einsum("bhncd,bhnde->bhnce", q_dec, s_start)
    return (o_intra + o_inter).reshape(bsz, h, t, dv), s_final


def flip_t(a):
    return jnp.flip(a, axis=2)


def gla_bidirectional(q, k, v, g_fwd, g_bwd, s0_fwd, s0_bwd):
    o_f, s_f = gla_chunked(q, k, v, g_fwd, s0_fwd)
    o_b, s_b = gla_chunked(flip_t(q), flip_t(k), flip_t(v), flip_t(g_bwd), s0_bwd)
    return o_f + flip_t(o_b), s_f, s_b


def gla_final_state(k, v, g):
    gc = jnp.cumsum(g, axis=2)
    w = jnp.exp(gc[:, :, -1:, :] - gc)
    return jnp.einsum("bhtd,bhte->bhde", k * w, v)


def hybrid_mixer(h, w_in, conv_w, conv_b, gw_f, gb_f, gw_b, gb_b, gla_norm_g, w_out,
                 s0_fwd, s0_bwd, on_grid):
    bsz, t, _ = h.shape
    p = h @ w_in
    a_b, a_c, a_x, q, k, v, r, low_f, low_b = jnp.split(
        p, [OFF_AC, OFF_AX, OFF_Q, OFF_K, OFF_V, OFF_R, OFF_GF, OFF_GB], axis=-1)
    u = a_c * a_x
    if on_grid:
        rows = t // GRID_W
        u_conv = dwconv3(u.reshape(bsz, rows, GRID_W, CONV_CH), conv_w, conv_b).reshape(bsz, t, CONV_CH)
    else:
        u_conv = dwconv3(u, conv_w, conv_b)
    y_a = a_b * u_conv
    qh = to_heads(q, GLA_DK) * (GLA_DK ** -0.5)
    kh = to_heads(k, GLA_DK)
    vh = to_heads(v, GLA_DV)
    gf = to_heads(log_decay(low_f, gw_f, gb_f), GLA_DK)
    gb = to_heads(log_decay(low_b, gw_b, gb_b), GLA_DK)
    o, s_f, s_b = gla_bidirectional(qh, kh, vh, gf, gb, s0_fwd, s0_bwd)
    o = rms_norm(o, gla_norm_g).transpose(0, 2, 1, 3).reshape(bsz, t, GLA_VAL)
    y_b = o.astype(h.dtype) * jax.nn.silu(r)
    y = jnp.concatenate([y_a, y_b], axis=-1) @ w_out
    return y, s_f, s_b


def context_gla_states(hc, w_in, gw_f, gb_f, gw_b, gb_b):
    kv = hc @ w_in[:, OFF_K:OFF_R]
    low = hc @ w_in[:, OFF_GF:D_PROJ]
    kh = to_heads(kv[..., :GLA_KEY], GLA_DK)
    vh = to_heads(kv[..., GLA_KEY:], GLA_DV)
    gf = to_heads(log_decay(low[..., :GLA_GATE_RANK], gw_f, gb_f), GLA_DK)
    gb = to_heads(log_decay(low[..., GLA_GATE_RANK:], gw_b, gb_b), GLA_DK)
    return gla_final_state(kh, vh, gf), gla_final_state(flip_t(kh), flip_t(vh), flip_t(gb))


def hier_moe(h, wgr, bgr, wer, ber, w1, w3, w2):
    bsz, t, d = h.shape
    tok = h.reshape(bsz * t, d)
    group_prob = jax.nn.softmax((tok @ wgr + bgr).astype(jnp.float32), axis=-1)
    p_g, g_idx = lax.top_k(group_prob, 1)
    exp_logits = (tok @ wer + ber).astype(jnp.float32).reshape(-1, N_GROUPS, EXPERTS_PER_GROUP)
    sel = jnp.take_along_axis(exp_logits, g_idx[:, :, None], axis=1)[:, 0]
    top_w, top_i = lax.top_k(jax.nn.softmax(sel, axis=-1), TOP_K_IN_GROUP)
    top_w = top_w / jnp.sum(top_w, axis=-1, keepdims=True)
    within = jnp.sum(jax.nn.one_hot(top_i, EXPERTS_PER_GROUP, dtype=jnp.float32) * top_w[..., None], axis=1)
    combine = jax.nn.one_hot(g_idx[:, 0], N_GROUPS, dtype=jnp.float32)[:, :, None] * (
        p_g[:, :, None] * within[:, None, :])
    combine = combine.reshape(-1, N_EXPERTS).astype(h.dtype)
    out = jnp.zeros_like(tok)
    for e in range(N_EXPERTS):
        act = jax.nn.silu(tok @ w1[e]) * (tok @ w3[e])
        out = out + combine[:, e:e + 1] * (act @ w2[e])
    return out.reshape(bsz, t, d)


def setup_inputs(seed: int = 0) -> dict:
    key = jax.random.key(seed)
    ks = jax.random.split(key, 28)
    f32 = jnp.float32
    L, D = DEPTH, D_MODEL

    def nrm(k, shape, s):
        return jax.random.normal(k, shape, f32) * s

    return {
        "x": nrm(ks[0], (BATCH, SEQ, D), 1.0),
        "c": nrm(ks[1], (BATCH, D), 1.0),
        "ctx": nrm(ks[2], (BATCH, CTX_LEN, D), 1.0),
        "c_ctx": nrm(ks[3], (D,), 1.0),
        "ln_in_g": 1.0 + nrm(ks[4], (D,), 0.02),
        "ln_in_b": nrm(ks[5], (D,), 0.02),
        "w_ada": nrm(ks[6], (L, D, 6 * D), D ** -0.5),
        "b_ada": nrm(ks[7], (L, 6 * D), 0.02),
        "w_in": nrm(ks[8], (L, D, D_PROJ), D ** -0.5),
        "conv_w": nrm(ks[9], (L, 3, CONV_CH), 3 ** -0.5),
        "conv_b": nrm(ks[10], (L, CONV_CH), 0.02),
        "gate_w2_fwd": nrm(ks[11], (L, GLA_GATE_RANK, GLA_KEY), GLA_GATE_RANK ** -0.5),
        "gate_b_fwd": nrm(ks[12], (L, GLA_KEY), 0.02),
        "gate_w2_bwd": nrm(ks[13], (L, GLA_GATE_RANK, GLA_KEY), GLA_GATE_RANK ** -0.5),
        "gate_b_bwd": nrm(ks[14], (L, GLA_KEY), 0.02),
        "gla_norm_g": 1.0 + nrm(ks[15], (L, GLA_DV), 0.02),
        "w_out": nrm(ks[16], (L, MIX_WIDTH, D), MIX_WIDTH ** -0.5 * DEEPNORM_BETA),
        "ln1_g": 1.0 + nrm(ks[17], (L, D), 0.02),
        "ln1_b": nrm(ks[18], (L, D), 0.02),
        "router_group_w": nrm(ks[19], (L, D, N_GROUPS), D ** -0.5),
        "router_group_b": nrm(ks[20], (L, N_GROUPS), 0.01),
        "router_expert_w": nrm(ks[21], (L, D, N_EXPERTS), D ** -0.5),
        "router_expert_b": nrm(ks[22], (L, N_EXPERTS), 0.01),
        "expert_w1": nrm(ks[23], (L, N_EXPERTS, D, D_EXPERT), D ** -0.5),
        "expert_w3": nrm(ks[24], (L, N_EXPERTS, D, D_EXPERT), D ** -0.5),
        "expert_w2": nrm(ks[25], (L, N_EXPERTS, D_EXPERT, D), D_EXPERT ** -0.5 * DEEPNORM_BETA),
        "ln2_g": 1.0 + nrm(ks[26], (L, D), 0.02),
        "ln2_b": nrm(ks[27], (L, D), 0.02),
    }


def reference(x, c, ctx, c_ctx, ln_in_g, ln_in_b, w_ada, b_ada, w_in, conv_w, conv_b,
              gate_w2_fwd, gate_b_fwd, gate_w2_bwd, gate_b_bwd, gla_norm_g, w_out,
              ln1_g, ln1_b, router_group_w, router_group_b, router_expert_w, router_expert_b,
              expert_w1, expert_w3, expert_w2, ln2_g, ln2_b):
    bsz = x.shape[0]
    x = layer_norm(x, ln_in_g, ln_in_b)
    xc = layer_norm(ctx, ln_in_g, ln_in_b)
    for l in range(DEPTH):
        last = l == DEPTH - 1
        sh1, sc1, g1, sh2, sc2, g2 = (m[:, None, :] for m in ada_params(c, w_ada[l], b_ada[l]))
        csh1, csc1, cg1, csh2, csc2, cg2 = ada_params(c_ctx, w_ada[l], b_ada[l])
        mix_w = (w_in[l], conv_w[l], conv_b[l], gate_w2_fwd[l], gate_b_fwd[l],
                 gate_w2_bwd[l], gate_b_bwd[l], gla_norm_g[l], w_out[l])
        moe_w = (router_group_w[l], router_group_b[l], router_expert_w[l], router_expert_b[l],
                 expert_w1[l], expert_w3[l], expert_w2[l])
        hc = modulate(xc, csh1, csc1)
        if last:
            s_f, s_b = context_gla_states(hc, w_in[l], gate_w2_fwd[l], gate_b_fwd[l],
                                          gate_w2_bwd[l], gate_b_bwd[l])
        else:
            zeros = jnp.zeros((bsz, GLA_HEADS, GLA_DK, GLA_DV), jnp.float32)
            yc, s_f, s_b = hybrid_mixer(hc, *mix_w, zeros, zeros, False)
            xc = post_norm_residual(xc, yc, cg1, ln1_g[l], ln1_b[l])
            xc = post_norm_residual(xc, hier_moe(modulate(xc, csh2, csc2), *moe_w), cg2, ln2_g[l], ln2_b[l])
        h = modulate(x, sh1, sc1)
        y, _, _ = hybrid_mixer(h, *mix_w, s_f, s_b, True)
        x = post_norm_residual(x, y, g1, ln1_g[l], ln1_b[l])
        x = post_norm_residual(x, hier_moe(modulate(x, sh2, sc2), *moe_w), g2, ln2_g[l], ln2_b[l])
    return x
```

```python
import functools

import jax
import jax.numpy as jnp
from jax import lax
from jax.experimental import pallas as pl
from jax.experimental.pallas import tpu as pltpu

F32 = jnp.float32
BF16 = jnp.bfloat16

D_MODEL = 1024
GRID_W = 64
CONV_CH = 512
GLA_HEADS = 4
GLA_DK = 64
GLA_DV = 128
GLA_KEY = GLA_HEADS * GLA_DK
GLA_VAL = GLA_HEADS * GLA_DV
GLA_GATE_RANK = 16
GLA_TAU = 16.0
OFF_AB = 0
OFF_AC = OFF_AB + CONV_CH
OFF_AX = OFF_AC + CONV_CH
OFF_Q = OFF_AX + CONV_CH
OFF_K = OFF_Q + GLA_KEY
OFF_V = OFF_K + GLA_KEY
OFF_R = OFF_V + GLA_VAL
OFF_GF = OFF_R + GLA_VAL
D_PROJ = OFF_GF + 2 * GLA_GATE_RANK
N_GROUPS = 4
EXPERTS_PER_GROUP = 4
N_EXPERTS = N_GROUPS * EXPERTS_PER_GROUP
D_EXPERT = 512
PAIRS_PER_GROUP = 6
N_CLASSES = N_GROUPS * PAIRS_PER_GROUP
LN_EPS = 1e-5
RMS_EPS = 1e-6
DEPTH = 1
DEEPNORM_ALPHA = (2.0 * DEPTH) ** 0.25

LANES = 128
GLA_CHUNK = 64
GLA_SUB = 16
N_SUB = GLA_CHUNK // GLA_SUB
ROUTE_W = LANES
MOE_ROW_W = D_MODEL + ROUTE_W
VMEM_LIMIT = 56 * 1024 * 1024


def _params(n_axes, vmem=VMEM_LIMIT):
    return pltpu.CompilerParams(dimension_semantics=("arbitrary",) * n_axes, vmem_limit_bytes=vmem)


def _dot(a, b):
    return jnp.dot(a, b, preferred_element_type=F32)


def _div_pow2(x, d):
    assert d & (d - 1) == 0
    return lax.shift_right_logical(x, jnp.int32(d.bit_length() - 1))


def _mod_pow2(x, d):
    assert d & (d - 1) == 0
    return lax.bitwise_and(x, jnp.int32(d - 1))


def _split2(x):
    hi = x.astype(BF16)
    lo = (x - hi.astype(F32)).astype(BF16)
    return hi, lo


def _split3(x):
    hi = x.astype(BF16)
    r = x - hi.astype(F32)
    mid = r.astype(BF16)
    lo = (r - mid.astype(F32)).astype(BF16)
    return hi, mid, lo


def _dot3(a, b):
    ah, al = _split2(a)
    bh, bl = _split2(b)
    return _dot(ah, bh) + _dot(ah, bl) + _dot(al, bh)


def _silu(x):
    return x / (1.0 + jnp.exp(-x))


def _layer_norm(x, g, b):
    mu = jnp.mean(x, axis=-1, keepdims=True)
    xc = x - mu
    var = jnp.mean(xc * xc, axis=-1, keepdims=True)
    return xc * lax.rsqrt(var + LN_EPS) * g + b


def _ada_kernel(c_ref, w_ref, b_ref, o_ref):
    o_ref[...] = _dot3(_silu(c_ref[...]), w_ref[...]) + b_ref[...]


def _ada(cond, w_ada, b_ada):
    rows = cond.shape[0]
    n_out = w_ada.shape[1]
    tn = 1024
    return pl.pallas_call(
        _ada_kernel,
        grid=(n_out // tn,),
        in_specs=[
            pl.BlockSpec((rows, D_MODEL), lambda j: (0, 0)),
            pl.BlockSpec((D_MODEL, tn), lambda j: (0, j)),
            pl.BlockSpec((1, tn), lambda j: (0, j)),
        ],
        out_specs=pl.BlockSpec((rows, tn), lambda j: (0, j)),
        out_shape=jax.ShapeDtypeStruct((rows, n_out), F32),
        compiler_params=_params(1),
        name="ada",
    )(cond, w_ada, b_ada)


def _log_sigmoid(z):
    return jnp.minimum(z, 0.0) - jnp.log(1.0 + jnp.exp(-jnp.abs(z)))


def _proj_kernel(latent, tm, x_ref, mod_ref, lnp_ref, w_ref, cw_ref, cb_ref, w2_ref, gbias_ref, *out_refs):
    x = x_ref[...]
    xn = _layer_norm(x, lnp_ref[0:1, :], lnp_ref[1:2, :])
    h = xn * mod_ref[0:1, :] + mod_ref[1:2, :]
    hb = h.astype(BF16)
    if latent:
        ya_ref, q_ref, k_ref, v_ref, sr_ref, g_ref = out_refs
        p = _dot(hb, w_ref[:, OFF_AB:OFF_Q])
        a_b = p[:, 0:CONV_CH]
        u = p[:, CONV_CH:2 * CONV_CH] * p[:, 2 * CONV_CH:3 * CONV_CH]
        pos = _mod_pow2(lax.broadcasted_iota(jnp.int32, (tm, 1), 0), GRID_W)
        u_prev = jnp.where(pos == 0, 0.0, pltpu.roll(u, 1, 0))
        u_next = jnp.where(pos == GRID_W - 1, 0.0, pltpu.roll(u, tm - 1, 0))
        conv = u_prev * cw_ref[0:1, :] + u * cw_ref[1:2, :] + u_next * cw_ref[2:3, :] + cb_ref[...]
        ya_ref[...] = (a_b * conv).astype(BF16)
        qk = _dot(hb, w_ref[:, OFF_Q:OFF_V])
        q_ref[...] = (qk[:, 0:GLA_KEY] * (GLA_DK ** -0.5)).astype(BF16)
        k_ref[...] = qk[:, GLA_KEY:].astype(BF16)
        r = _dot(hb, w_ref[:, OFF_R:OFF_GF])
        sr_ref[...] = _silu(r).astype(BF16)
    else:
        k_ref, v_ref, g_ref = out_refs
        k_ref[...] = _dot(hb, w_ref[:, OFF_K:OFF_V]).astype(BF16)
    v_ref[...] = _dot(hb, w_ref[:, OFF_V:OFF_R]).astype(BF16)
    low = _dot(hb, w_ref[:, OFF_GF:D_PROJ])
    z = _dot(low.astype(BF16), w2_ref[...]) + gbias_ref[...]
    g_ref[...] = _log_sigmoid(z) * (1.0 / GLA_TAU)


def _proj(x, mod, lnp, w_in_b, conv_w, conv_b, w2cat, gbias, latent):
    bsz, t, _ = x.shape
    tm = min(512, t)
    assert t % tm == 0 and tm % GRID_W == 0
    tok = lambda w: pl.BlockSpec((None, tm, w), lambda b, i: (b, i, 0))
    full = lambda a: pl.BlockSpec(a.shape, lambda b, i: (0,) * a.ndim)
    widths = ([(CONV_CH, BF16), (GLA_KEY, BF16)] if latent else []) + [(GLA_KEY, BF16), (GLA_VAL, BF16)]
    widths += ([(GLA_VAL, BF16)] if latent else []) + [(2 * GLA_KEY, F32)]
    return pl.pallas_call(
        functools.partial(_proj_kernel, latent, tm),
        grid=(bsz, t // tm),
        in_specs=[
            tok(D_MODEL),
            pl.BlockSpec((None, 2, D_MODEL), lambda b, i: (b, 0, 0)),
            full(lnp), full(w_in_b), full(conv_w), full(conv_b), full(w2cat), full(gbias),
        ],
        out_specs=[tok(w) for w, _ in widths],
        out_shape=[jax.ShapeDtypeStruct((bsz, t, w), dt) for w, dt in widths],
        compiler_params=_params(2),
        name="proj_latent" if latent else "proj_ctx",
    )(x, mod, lnp, w_in_b, conv_w, conv_b, w2cat, gbias)


def _tri(n, reverse, strict=False):
    i = lax.broadcasted_iota(jnp.int32, (n, n), 0)
    j = lax.broadcasted_iota(jnp.int32, (n, n), 1)
    if strict:
        m = (j > i) if reverse else (j < i)
    else:
        m = (j >= i) if reverse else (j <= i)
    return jnp.where(m, 1.0, 0.0).astype(BF16)


def _chunk_cumsum(g, reverse):
    tri = _tri(GLA_CHUNK, reverse)
    g1, g2, g3 = _split3(g)
    return _dot(tri, g1) + _dot(tri, g2) + _dot(tri, g3)


def _chunk_total_col(g):
    ones = jnp.ones((GLA_CHUNK, LANES), BF16)
    dn = (((0,), (0,)), ((), ()))
    g1, g2, g3 = _split3(g)
    tot = lambda a: lax.dot_general(a, ones, dn, preferred_element_type=F32)
    return tot(g1) + tot(g2) + tot(g3)


def _sub_anchors(gc, reverse):
    zero = jnp.zeros((1, GLA_KEY), F32)
    if reverse:
        return [gc[GLA_SUB * (a + 1):GLA_SUB * (a + 1) + 1] for a in range(N_SUB - 1)] + [zero]
    return [zero] + [gc[GLA_SUB * a - 1:GLA_SUB * a] for a in range(1, N_SUB)]


def _intra_scores(q, k, gc, reverse):
    r = _sub_anchors(gc, reverse)
    anchor = jnp.concatenate([jnp.broadcast_to(ra, (GLA_SUB, GLA_KEY)) for ra in r], axis=0)
    gcb = gc - anchor
    qt = q * jnp.exp(gcb)
    kt = k * jnp.exp(-gcb)
    pairs = [(a, b) for a in range(N_SUB) for b in range(N_SUB) if (b >= a if reverse else b <= a)]
    rows = []
    for a, b in pairs:
        qa = qt[GLA_SUB * a:GLA_SUB * (a + 1)]
        if a != b:
            qa = qa * jnp.exp(r[a] - r[b])
        rows.append(qa)
    qp = jnp.concatenate(rows, axis=0).astype(BF16)
    width = GLA_HEADS * GLA_CHUNK
    rr = lax.broadcasted_iota(jnp.int32, (width, GLA_KEY), 0)
    cc = lax.broadcasted_iota(jnp.int32, (width, GLA_KEY), 1)
    kbd = jnp.where(_div_pow2(rr, GLA_CHUNK) == _div_pow2(cc, GLA_DK),
                    jnp.concatenate([kt] * GLA_HEADS, axis=0), 0.0)
    res = lax.dot_general(qp, kbd.astype(BF16), (((1,), (1,)), ((), ())), preferred_element_type=F32)
    col = _mod_pow2(lax.broadcasted_iota(jnp.int32, (GLA_SUB, width), 1), GLA_CHUNK)
    col_blk = _div_pow2(col, GLA_SUB)
    col_pos = _mod_pow2(col, GLA_SUB)
    row_pos = lax.broadcasted_iota(jnp.int32, (GLA_SUB, width), 0)
    causal = (col_pos >= row_pos) if reverse else (col_pos <= row_pos)
    blocks = []
    for a in range(N_SUB):
        acc = jnp.zeros((GLA_SUB, width), F32)
        for idx, (pa, pb) in enumerate(pairs):
            if pa != a:
                continue
            keep = col_blk == pb
            if pa == pb:
                keep = keep & causal
            acc = acc + jnp.where(keep, res[GLA_SUB * idx:GLA_SUB * (idx + 1)], 0.0)
        blocks.append(acc)
    return jnp.concatenate(blocks, axis=0)


def _head_mask(rows_per_head, cols_per_head):
    shape = (GLA_HEADS * rows_per_head, GLA_HEADS * cols_per_head)
    rr = lax.broadcasted_iota(jnp.int32, shape, 0)
    cc = lax.broadcasted_iota(jnp.int32, shape, 1)
    return _div_pow2(rr, rows_per_head) == _div_pow2(cc, cols_per_head)


def _state_step(q, k, v_b, g, gc, state, reverse):
    total = gc[0:1] if reverse else gc[GLA_CHUNK - 1:GLA_CHUNK]
    o_inter = None
    if q is not None:
        o_inter = _dot((q * jnp.exp(gc)).astype(BF16), state.astype(BF16))
    k_end = (k * jnp.exp(total - gc)).astype(BF16)
    upd = lax.dot_general(k_end, v_b, (((0,), (0,)), ((), ())), preferred_element_type=F32)
    decay = jnp.exp(_chunk_total_col(g))
    decay = jnp.concatenate([decay] * (GLA_VAL // LANES), axis=1)
    new_state = state * decay + jnp.where(_head_mask(GLA_DK, GLA_DV), upd, 0.0)
    return o_inter, new_state


def _gla_kernel(tt, nt, with_out, *refs):
    if with_out:
        (qf_ref, kf_ref, vf_ref, gfw_ref, qb_ref, kb_ref, vb_ref, gbw_ref, s0f_ref, s0b_ref,
         of_ref, ob_ref, sf_ref, sb_ref) = refs
    else:
        kf_ref, vf_ref, gfw_ref, kb_ref, vb_ref, gbw_ref, s0f_ref, s0b_ref, sf_ref, sb_ref = refs
    j = pl.program_id(1)

    @pl.when(j == 0)
    def _():
        sf_ref[...] = s0f_ref[...]
        sb_ref[...] = s0b_ref[...]

    n_chunks = tt // GLA_CHUNK
    state = sf_ref[...]
    for c in range(n_chunks):
        sl = slice(c * GLA_CHUNK, (c + 1) * GLA_CHUNK)
        k = kf_ref[sl, :].astype(F32)
        v_b = vf_ref[sl, :]
        g = gfw_ref[sl, :]
        g_f = g[:, 0:GLA_KEY]
        gc_f = _chunk_cumsum(g_f, False)
        q = None
        if with_out:
            q = qf_ref[sl, :].astype(F32)
            gc_b = _chunk_cumsum(g[:, GLA_KEY:], True)
            scores = _intra_scores(q, k, gc_f, False) + _intra_scores(q, k, gc_b, True)
            vbd = jnp.where(_head_mask(GLA_CHUNK, GLA_DV), jnp.concatenate([v_b] * GLA_HEADS, axis=0),
                            jnp.zeros((), BF16))
            o_intra = _dot(scores.astype(BF16), vbd)
        o_inter, state = _state_step(q, k, v_b, g_f, gc_f, state, False)
        if with_out:
            of_ref[sl, :] = (o_intra + o_inter).astype(BF16)
    sf_ref[...] = state

    state = sb_ref[...]
    for c in reversed(range(n_chunks)):
        sl = slice(c * GLA_CHUNK, (c + 1) * GLA_CHUNK)
        k = kb_ref[sl, :].astype(F32)
        v_b = vb_ref[sl, :]
        g_b = gbw_ref[sl, :][:, GLA_KEY:]
        gc_b = _chunk_cumsum(g_b, True)
        q = qb_ref[sl, :].astype(F32) if with_out else None
        o_inter, state = _state_step(q, k, v_b, g_b, gc_b, state, True)
        if with_out:
            ob_ref[sl, :] = o_inter.astype(BF16)
    sb_ref[...] = state


def _gla(q, k, v, g, s0f, s0b):
    with_out = q is not None
    bsz, t, _ = k.shape
    tt = min(256, t)
    assert t % tt == 0 and tt % GLA_CHUNK == 0
    nt = t // tt
    fwd = lambda w: pl.BlockSpec((None, tt, w), lambda b, j: (b, j, 0))
    bwd = lambda w: pl.BlockSpec((None, tt, w), lambda b, j: (b, nt - 1 - j, 0))
    st = pl.BlockSpec((None, GLA_KEY, GLA_VAL), lambda b, j: (b, 0, 0))
    st_shape = jax.ShapeDtypeStruct((bsz, GLA_KEY, GLA_VAL), F32)
    if with_out:
        ins = [q, k, v, g, q, k, v, g, s0f, s0b]
        in_specs = [fwd(GLA_KEY), fwd(GLA_KEY), fwd(GLA_VAL), fwd(2 * GLA_KEY),
                    bwd(GLA_KEY), bwd(GLA_KEY), bwd(GLA_VAL), bwd(2 * GLA_KEY), st, st]
        out_specs = [fwd(GLA_VAL), bwd(GLA_VAL), st, st]
        o_shape = jax.ShapeDtypeStruct((bsz, t, GLA_VAL), BF16)
        out_shape = [o_shape, o_shape, st_shape, st_shape]
    else:
        ins = [k, v, g, k, v, g, s0f, s0b]
        in_specs = [fwd(GLA_KEY), fwd(GLA_VAL), fwd(2 * GLA_KEY),
                    bwd(GLA_KEY), bwd(GLA_VAL), bwd(2 * GLA_KEY), st, st]
        out_specs = [st, st]
        out_shape = [st_shape, st_shape]
    return pl.pallas_call(
        functools.partial(_gla_kernel, tt, nt, with_out),
        grid=(bsz, nt),
        in_specs=in_specs,
        out_specs=out_specs,
        out_shape=out_shape,
        compiler_params=_params(2),
        name="gla_latent" if with_out else "gla_ctx",
    )(*ins)


def _row_max(x):
    return jnp.max(x, axis=-1, keepdims=True)


def _row_min(x):
    return jnp.min(x, axis=-1, keepdims=True)


def _row_sum(x):
    return jnp.sum(x, axis=-1, keepdims=True)


def _route(logit, tm, cnt_ref):
    lane = lax.broadcasted_iota(jnp.int32, (tm, ROUTE_W), 1).astype(F32)
    none = float(ROUTE_W - 1)
    gmask = lane < N_GROUPS
    eg = jnp.where(gmask, jnp.exp(logit - _row_max(jnp.where(gmask, logit, -jnp.inf))), 0.0)
    pg = eg / _row_sum(eg)
    p_g = _row_max(pg)
    g_idx = _row_min(jnp.where(gmask & (pg >= p_g), lane, none))
    first = N_GROUPS + EXPERTS_PER_GROUP * g_idx
    emask = (lane >= first) & (lane < first + EXPERTS_PER_GROUP)
    ee = jnp.where(emask, jnp.exp(logit - _row_max(jnp.where(emask, logit, -jnp.inf))), 0.0)
    pe = jnp.where(emask, ee / _row_sum(ee), -1.0)
    p1 = _row_max(pe)
    i1 = _row_min(jnp.where(emask & (pe >= p1), lane, none))
    rest = emask & (lane != i1)
    pe2 = jnp.where(rest, pe, -1.0)
    p2 = _row_max(pe2)
    i2 = _row_min(jnp.where(rest & (pe2 >= p2), lane, none))
    den = p1 + p2
    w1 = p1 / den * p_g
    w2 = p2 / den * p_g
    l1 = i1 - first
    l2 = i2 - first
    lo = jnp.minimum(l1, l2)
    hi = jnp.maximum(l1, l2)
    pair = lo * (7.0 - lo) * 0.5 + (hi - lo - 1.0)
    cls = g_idx * PAIRS_PER_GROUP + pair
    w_lo = jnp.where(l1 < l2, w1, w2)
    w_hi = jnp.where(l1 < l2, w2, w1)
    onehot = jnp.where(lane == cls, 1.0, 0.0)
    before = _dot(_tri(tm, False, strict=True), onehot.astype(BF16))
    base = cnt_ref[0:1, :]
    rank = _row_sum(onehot * (before + base))
    cnt_ref[...] = jnp.broadcast_to(base + jnp.sum(onehot, axis=0, keepdims=True), cnt_ref.shape)
    rec = jnp.where(lane == 0.0, cls, 0.0)
    rec = jnp.where(lane == 1.0, w_lo, rec)
    rec = jnp.where(lane == 2.0, w_hi, rec)
    rec = jnp.where(lane == 3.0, rank, rec)
    return rec


def _mix_out_kernel(tm, x_ref, of_ref, ob_ref, sr_ref, ya_ref, mod_ref, lnp_ref, gn_ref, wo_ref, wr_ref, br_ref,
                    x1_ref, hx_ref, cnt_ref):
    @pl.when((pl.program_id(0) == 0) & (pl.program_id(1) == 0))
    def _():
        cnt_ref[...] = jnp.zeros(cnt_ref.shape, F32)

    o = of_ref[...].astype(F32) + ob_ref[...].astype(F32)
    sr = sr_ref[...].astype(F32)
    y = _dot(ya_ref[...], wo_ref[0:CONV_CH, :])
    for h in range(GLA_HEADS):
        sl = slice(h * GLA_DV, (h + 1) * GLA_DV)
        oh = o[:, sl]
        ms = jnp.mean(oh * oh, axis=-1, keepdims=True)
        yb = oh * lax.rsqrt(ms + RMS_EPS) * gn_ref[...] * sr[:, sl]
        y = y + _dot(yb.astype(BF16), wo_ref[CONV_CH + h * GLA_DV:CONV_CH + (h + 1) * GLA_DV, :])
    xn = _layer_norm(x_ref[...], lnp_ref[0:1, :], lnp_ref[1:2, :])
    x1 = _layer_norm(DEEPNORM_ALPHA * xn + mod_ref[0:1, :] * y, lnp_ref[2:3, :], lnp_ref[3:4, :])
    x1_ref[...] = x1
    h2 = x1 * mod_ref[1:2, :] + mod_ref[2:3, :]
    hx_ref[:, 0:D_MODEL] = h2
    logit = _dot3(h2, wr_ref[...]) + br_ref[...]
    hx_ref[:, D_MODEL:MOE_ROW_W] = _route(logit, tm, cnt_ref)


def _mix_out(x, o_f, o_b, sr, ya, mod, lnp, gn, w_out_b, wr, br):
    bsz, t, _ = x.shape
    tm = min(256, t)
    assert t % tm == 0
    n_t = t // tm
    tok = lambda w: pl.BlockSpec((None, tm, w), lambda b, i: (b, i, 0))
    full = lambda a: pl.BlockSpec(a.shape, lambda b, i: (0,) * a.ndim)
    return pl.pallas_call(
        functools.partial(_mix_out_kernel, tm),
        grid=(bsz, n_t),
        in_specs=[
            tok(D_MODEL), tok(GLA_VAL), tok(GLA_VAL), tok(GLA_VAL), tok(CONV_CH),
            pl.BlockSpec((None, 3, D_MODEL), lambda b, i: (b, 0, 0)),
            full(lnp), full(gn), full(w_out_b), full(wr), full(br),
        ],
        out_specs=[
            tok(D_MODEL),
            pl.BlockSpec((tm, MOE_ROW_W), lambda b, i: (b * n_t + i, 0)),
            pl.BlockSpec((8, ROUTE_W), lambda b, i: (0, 0)),
        ],
        out_shape=[
            jax.ShapeDtypeStruct((bsz, t, D_MODEL), F32),
            jax.ShapeDtypeStruct((bsz * t, MOE_ROW_W), F32),
            jax.ShapeDtypeStruct((8, ROUTE_W), F32),
        ],
        compiler_params=_params(2),
        name="mix_out",
    )(x, o_f, o_b, sr, ya, mod, lnp, gn, w_out_b, wr, br)


def _moe_kernel(tm, n_tok, n_steps, ntiles_ref, lo_ref, hi_ref, src_ref,
                hx_hbm, w13l_ref, w2l_ref, w13h_ref, w2h_ref, out_hbm, gbuf, obuf, gsem, ssem):
    i = pl.program_id(0)
    n_used = ntiles_ref[0]
    slot = lax.bitwise_and(i, 1)

    def gather_copy(tile, buf_slot, j):
        idx = jnp.minimum(src_ref[tile * tm + j], n_tok - 1)
        return pltpu.make_async_copy(hx_hbm.at[pl.ds(idx, 1)], gbuf.at[buf_slot, pl.ds(j, 1)], gsem.at[buf_slot])

    def scatter_copy(tile, buf_slot, j):
        idx = src_ref[tile * tm + j]
        idx = jnp.where(idx >= n_tok, n_tok + buf_slot * tm + j, idx)
        return pltpu.make_async_copy(obuf.at[buf_slot, pl.ds(j, 1)], out_hbm.at[pl.ds(idx, 1)], ssem.at[buf_slot])

    def start_gather(tile, buf_slot):
        for j in range(tm):
            gather_copy(tile, buf_slot, j).start()

    def wait_gather(buf_slot):
        pltpu.make_async_copy(hx_hbm.at[pl.ds(0, tm)], gbuf.at[buf_slot], gsem.at[buf_slot]).wait()

    def wait_scatter(buf_slot):
        pltpu.make_async_copy(obuf.at[buf_slot], out_hbm.at[pl.ds(0, tm)], ssem.at[buf_slot]).wait()

    @pl.when(i == 0)
    def _():
        start_gather(0, 0)
        obuf[...] = jnp.zeros(obuf.shape, F32)
        for s in range(2):
            fill = pltpu.make_async_copy(obuf.at[s], out_hbm.at[pl.ds(n_tok + s * tm, tm)], ssem.at[s])
            fill.start()
            fill.wait()

    @pl.when(i + 1 < n_used)
    def _():
        start_gather(i + 1, 1 - slot)

    @pl.when(i < n_used)
    def _():
        wait_gather(slot)

        @pl.when(i >= 2)
        def _():
            wait_scatter(slot)

        xg = gbuf[slot]
        xb = xg[:, 0:D_MODEL].astype(BF16)
        w_lo = xg[:, D_MODEL + 1:D_MODEL + 2]
        w_hi = xg[:, D_MODEL + 2:D_MODEL + 3]

        def expert(w13_ref, w2_ref):
            h13 = _dot(xb, w13_ref[...])
            act = _silu(h13[:, 0:D_EXPERT]) * h13[:, D_EXPERT:]
            return _dot(act.astype(BF16), w2_ref[...])

        obuf[slot] = w_lo * expert(w13l_ref, w2l_ref) + w_hi * expert(w13h_ref, w2h_ref)
        for j in range(tm):
            scatter_copy(i, slot, j).start()

        @pl.when(i == n_used - 1)
        def _():
            wait_scatter(slot)

            @pl.when(i >= 1)
            def _():
                wait_scatter(1 - slot)


def _moe(hx, src, n_tiles, tile_lo, tile_hi, w13_b, w2_b, tm):
    n_tok = hx.shape[0]
    n_steps = src.shape[0] // tm
    wspec = lambda which, shape: pl.BlockSpec(
        (None,) + shape, (lambda i, nt, lo, hi, s: (lo[i], 0, 0)) if which == 0 else
        (lambda i, nt, lo, hi, s: (hi[i], 0, 0)))
    grid_spec = pltpu.PrefetchScalarGridSpec(
        num_scalar_prefetch=4,
        grid=(n_steps,),
        in_specs=[
            pl.BlockSpec(memory_space=pl.ANY),
            wspec(0, (D_MODEL, 2 * D_EXPERT)), wspec(0, (D_EXPERT, D_MODEL)),
            wspec(1, (D_MODEL, 2 * D_EXPERT)), wspec(1, (D_EXPERT, D_MODEL)),
        ],
        out_specs=pl.BlockSpec(memory_space=pl.ANY),
        scratch_shapes=[
            pltpu.VMEM((2, tm, MOE_ROW_W), F32),
            pltpu.VMEM((2, tm, D_MODEL), F32),
            pltpu.SemaphoreType.DMA((2,)),
            pltpu.SemaphoreType.DMA((2,)),
        ],
    )
    return pl.pallas_call(
        functools.partial(_moe_kernel, tm, n_tok, n_steps),
        grid_spec=grid_spec,
        out_shape=jax.ShapeDtypeStruct((n_tok + 2 * tm, D_MODEL), F32),
        compiler_params=_params(1),
        name="moe",
    )(n_tiles, tile_lo, tile_hi, src, hx, w13_b, w2_b, w13_b, w2_b)


def _final_kernel(x1_ref, moe_ref, mod_ref, lnp_ref, o_ref):
    o_ref[...] = _layer_norm(DEEPNORM_ALPHA * x1_ref[...] + mod_ref[...] * moe_ref[...],
                             lnp_ref[0:1, :], lnp_ref[1:2, :])


def _final(x1, moe, g2, lnp):
    bsz, t, _ = x1.shape
    tm = min(512, t)
    n_t = t // tm
    return pl.pallas_call(
        _final_kernel,
        grid=(bsz, n_t),
        in_specs=[
            pl.BlockSpec((None, tm, D_MODEL), lambda b, i: (b, i, 0)),
            pl.BlockSpec((tm, D_MODEL), lambda b, i: (b * n_t + i, 0)),
            pl.BlockSpec((None, 1, D_MODEL), lambda b, i: (b, 0, 0)),
            pl.BlockSpec(lnp.shape, lambda b, i: (0, 0)),
        ],
        out_specs=pl.BlockSpec((None, tm, D_MODEL), lambda b, i: (b, i, 0)),
        out_shape=jax.ShapeDtypeStruct((bsz, t, D_MODEL), F32),
        compiler_params=_params(2),
        name="final",
    )(x1, moe, g2, lnp)


def _pair_tables():
    lo, hi = [], []
    for g in range(N_GROUPS):
        for a in range(EXPERTS_PER_GROUP):
            for b in range(a + 1, EXPERTS_PER_GROUP):
                lo.append(g * EXPERTS_PER_GROUP + a)
                hi.append(g * EXPERTS_PER_GROUP + b)
    return jnp.array(lo, jnp.int32), jnp.array(hi, jnp.int32)


def kernel(x, c, ctx, c_ctx, ln_in_g, ln_in_b, w_ada, b_ada, w_in, conv_w, conv_b, gate_w2_fwd, gate_b_fwd,
           gate_w2_bwd, gate_b_bwd, gla_norm_g, w_out, ln1_g, ln1_b, router_group_w, router_group_b,
           router_expert_w, router_expert_b, expert_w1, expert_w3, expert_w2, ln2_g, ln2_b):
    bsz, t, _ = x.shape
    n_tok = bsz * t
    l = 0
    rows = -(-(bsz + 1) // 8) * 8
    cond = jnp.zeros((rows, D_MODEL), F32).at[:bsz].set(c).at[bsz].set(c_ctx)
    ada = _ada(cond, w_ada[l], b_ada[l][None, :])
    sh1, sc1, g1, sh2, sc2, g2 = [ada[:, i * D_MODEL:(i + 1) * D_MODEL] for i in range(6)]

    w_in_b = w_in[l].astype(BF16)
    lnp_in = jnp.stack([ln_in_g, ln_in_b])
    zero = jnp.zeros((GLA_GATE_RANK, GLA_KEY), F32)
    w2cat = jnp.concatenate([jnp.concatenate([gate_w2_fwd[l], zero], axis=1),
                             jnp.concatenate([zero, gate_w2_bwd[l]], axis=1)], axis=0).astype(BF16)
    gbias = jnp.concatenate([gate_b_fwd[l], gate_b_bwd[l]])[None, :]

    mod_ctx = jnp.broadcast_to(jnp.stack([1.0 + sc1[bsz], sh1[bsz]])[None], (bsz, 2, D_MODEL))
    k_c, v_c, g_c = _proj(ctx, mod_ctx, lnp_in, w_in_b, conv_w[l], conv_b[l][None, :], w2cat, gbias, False)
    zero_state = jnp.zeros((bsz, GLA_KEY, GLA_VAL), F32)
    s_f, s_b = _gla(None, k_c, v_c, g_c, zero_state, zero_state)

    mod1 = jnp.stack([1.0 + sc1[:bsz], sh1[:bsz]], axis=1)
    ya, q, k, v, sr, g = _proj(x, mod1, lnp_in, w_in_b, conv_w[l], conv_b[l][None, :], w2cat, gbias, True)
    o_f, o_b, _, _ = _gla(q, k, v, g, s_f, s_b)

    mod2 = jnp.stack([g1[:bsz], 1.0 + sc2[:bsz], sh2[:bsz]], axis=1)
    lnp1 = jnp.stack([ln_in_g, ln_in_b, ln1_g[l], ln1_b[l]])
    wr = jnp.zeros((D_MODEL, ROUTE_W), F32)
    wr = wr.at[:, :N_GROUPS].set(router_group_w[l]).at[:, N_GROUPS:N_GROUPS + N_EXPERTS].set(router_expert_w[l])
    br = jnp.zeros((1, ROUTE_W), F32)
    br = br.at[0, :N_GROUPS].set(router_group_b[l]).at[0, N_GROUPS:N_GROUPS + N_EXPERTS].set(router_expert_b[l])
    x1, hx, cnt = _mix_out(x, o_f, o_b, sr, ya, mod2, lnp1, gla_norm_g[l][None, :], w_out[l].astype(BF16),
                           wr, br)

    tm = 256
    n_steps = n_tok // tm + N_CLASSES
    cls = hx[:, D_MODEL].astype(jnp.int32)
    rank = hx[:, D_MODEL + 3].astype(jnp.int32)
    counts = cnt[0, :N_CLASSES].astype(jnp.int32)
    tiles_per_class = (counts + tm - 1) // tm
    tile_end = jnp.cumsum(tiles_per_class)
    tile_start = tile_end - tiles_per_class
    dest = tile_start[cls] * tm + rank
    src = jnp.full((n_steps * tm,), n_tok, jnp.int32).at[dest].set(jnp.arange(n_tok, dtype=jnp.int32))
    n_tiles = tile_end[-1:]
    tile_cls = jnp.searchsorted(tile_end, jnp.minimum(jnp.arange(n_steps), n_tiles[0] - 1), side="right")
    pair_lo, pair_hi = _pair_tables()
    w13_b = jnp.concatenate([expert_w1[l], expert_w3[l]], axis=-1).astype(BF16)
    moe = _moe(hx, src, n_tiles.astype(jnp.int32), pair_lo[tile_cls], pair_hi[tile_cls], w13_b,
               expert_w2[l].astype(BF16), tm)

    return _final(x1, moe, g2[:bsz][:, None, :], jnp.stack([ln2_g[l], ln2_b[l]]))
```

```python
import functools

import jax
import jax.numpy as jnp
from jax import lax
from jax.experimental import pallas as pl
from jax.experimental.pallas import tpu as pltpu

F32 = jnp.float32
BF16 = jnp.bfloat16

D_MODEL = 1024
GRID_W = 64
CONV_CH = 512
GLA_HEADS = 4
GLA_DK = 64
GLA_DV = 128
GLA_KEY = GLA_HEADS * GLA_DK
GLA_VAL = GLA_HEADS * GLA_DV
GLA_GATE_RANK = 16
GLA_TAU = 16.0
OFF_AB = 0
OFF_AC = OFF_AB + CONV_CH
OFF_AX = OFF_AC + CONV_CH
OFF_Q = OFF_AX + CONV_CH
OFF_K = OFF_Q + GLA_KEY
OFF_V = OFF_K + GLA_KEY
OFF_R = OFF_V + GLA_VAL
OFF_GF = OFF_R + GLA_VAL
D_PROJ = OFF_GF + 2 * GLA_GATE_RANK
N_GROUPS = 4
EXPERTS_PER_GROUP = 4
N_EXPERTS = N_GROUPS * EXPERTS_PER_GROUP
D_EXPERT = 512
PAIRS_PER_GROUP = 6
N_CLASSES = N_GROUPS * PAIRS_PER_GROUP
LN_EPS = 1e-5
RMS_EPS = 1e-6
DEPTH = 1
DEEPNORM_ALPHA = (2.0 * DEPTH) ** 0.25

LANES = 128
GLA_CHUNK = 64
GLA_SUB = 16
N_SUB = GLA_CHUNK // GLA_SUB
ROUTE_W = LANES
MOE_ROW_W = D_MODEL + ROUTE_W
SORT_TILE = 256
MOE_TILE = 256
CHUNK_ROWS = 4
LOCAL_CHUNKS = -(-(SORT_TILE + N_CLASSES * (CHUNK_ROWS - 1)) // (8 * CHUNK_ROWS)) * 8
CHUNKS_PER_TILE = MOE_TILE // CHUNK_ROWS
VMEM_LIMIT = 56 * 1024 * 1024


def _params(n_axes, vmem=VMEM_LIMIT):
    return pltpu.CompilerParams(dimension_semantics=("arbitrary",) * n_axes, vmem_limit_bytes=vmem)


def _dot(a, b):
    return jnp.dot(a, b, preferred_element_type=F32)


def _div_pow2(x, d):
    assert d & (d - 1) == 0
    return lax.shift_right_logical(x, jnp.int32(d.bit_length() - 1))


def _mod_pow2(x, d):
    assert d & (d - 1) == 0
    return lax.bitwise_and(x, jnp.int32(d - 1))


def _split2(x):
    hi = x.astype(BF16)
    lo = (x - hi.astype(F32)).astype(BF16)
    return hi, lo


def _split3(x):
    hi = x.astype(BF16)
    r = x - hi.astype(F32)
    mid = r.astype(BF16)
    lo = (r - mid.astype(F32)).astype(BF16)
    return hi, mid, lo


def _dot3(a, b):
    ah, al = _split2(a)
    bh, bl = _split2(b)
    return _dot(ah, bh) + _dot(ah, bl) + _dot(al, bh)


def _silu(x):
    return x / (1.0 + jnp.exp(-x))


def _layer_norm(x, g, b):
    mu = jnp.mean(x, axis=-1, keepdims=True)
    xc = x - mu
    var = jnp.mean(xc * xc, axis=-1, keepdims=True)
    return xc * lax.rsqrt(var + LN_EPS) * g + b


def _ada_kernel(c_ref, w_ref, b_ref, o_ref):
    o_ref[...] = _dot3(_silu(c_ref[...]), w_ref[...]) + b_ref[...]


def _ada(cond, w_ada, b_ada):
    rows = cond.shape[0]
    n_out = w_ada.shape[1]
    tn = 1024
    return pl.pallas_call(
        _ada_kernel,
        grid=(n_out // tn,),
        in_specs=[
            pl.BlockSpec((rows, D_MODEL), lambda j: (0, 0)),
            pl.BlockSpec((D_MODEL, tn), lambda j: (0, j)),
            pl.BlockSpec((1, tn), lambda j: (0, j)),
        ],
        out_specs=pl.BlockSpec((rows, tn), lambda j: (0, j)),
        out_shape=jax.ShapeDtypeStruct((rows, n_out), F32),
        compiler_params=_params(1),
        name="ada",
    )(cond, w_ada, b_ada)


def _log_sigmoid(z):
    return jnp.minimum(z, 0.0) - jnp.log(1.0 + jnp.exp(-jnp.abs(z)))


def _proj_kernel(latent, tm, x_ref, mod_ref, lnp_ref, w_ref, cw_ref, cb_ref, w2_ref, gbias_ref, *out_refs):
    x = x_ref[...]
    xn = _layer_norm(x, lnp_ref[0:1, :], lnp_ref[1:2, :])
    h = xn * mod_ref[0:1, :] + mod_ref[1:2, :]
    hb = h.astype(BF16)
    if latent:
        ya_ref, q_ref, k_ref, v_ref, sr_ref, g_ref = out_refs
        p = _dot(hb, w_ref[:, OFF_AB:OFF_Q])
        a_b = p[:, 0:CONV_CH]
        u = p[:, CONV_CH:2 * CONV_CH] * p[:, 2 * CONV_CH:3 * CONV_CH]
        pos = _mod_pow2(lax.broadcasted_iota(jnp.int32, (tm, 1), 0), GRID_W)
        u_prev = jnp.where(pos == 0, 0.0, pltpu.roll(u, 1, 0))
        u_next = jnp.where(pos == GRID_W - 1, 0.0, pltpu.roll(u, tm - 1, 0))
        conv = u_prev * cw_ref[0:1, :] + u * cw_ref[1:2, :] + u_next * cw_ref[2:3, :] + cb_ref[...]
        ya_ref[...] = (a_b * conv).astype(BF16)
        qk = _dot(hb, w_ref[:, OFF_Q:OFF_V])
        q_ref[...] = (qk[:, 0:GLA_KEY] * (GLA_DK ** -0.5)).astype(BF16)
        k_ref[...] = qk[:, GLA_KEY:].astype(BF16)
        r = _dot(hb, w_ref[:, OFF_R:OFF_GF])
        sr_ref[...] = _silu(r).astype(BF16)
    else:
        k_ref, v_ref, g_ref = out_refs
        k_ref[...] = _dot(hb, w_ref[:, OFF_K:OFF_V]).astype(BF16)
    v_ref[...] = _dot(hb, w_ref[:, OFF_V:OFF_R]).astype(BF16)
    low = _dot(hb, w_ref[:, OFF_GF:D_PROJ])
    z = _dot(low.astype(BF16), w2_ref[...]) + gbias_ref[...]
    g_ref[...] = _log_sigmoid(z) * (1.0 / GLA_TAU)


def _proj(x, mod, lnp, w_in_b, conv_w, conv_b, w2cat, gbias, latent):
    bsz, t, _ = x.shape
    tm = min(512, t)
    assert t % tm == 0 and tm % GRID_W == 0
    tok = lambda w: pl.BlockSpec((None, tm, w), lambda b, i: (b, i, 0))
    full = lambda a: pl.BlockSpec(a.shape, lambda b, i: (0,) * a.ndim)
    widths = ([(CONV_CH, BF16), (GLA_KEY, BF16)] if latent else []) + [(GLA_KEY, BF16), (GLA_VAL, BF16)]
    widths += ([(GLA_VAL, BF16)] if latent else []) + [(2 * GLA_KEY, F32)]
    return pl.pallas_call(
        functools.partial(_proj_kernel, latent, tm),
        grid=(bsz, t // tm),
        in_specs=[
            tok(D_MODEL),
            pl.BlockSpec((None, 2, D_MODEL), lambda b, i: (b, 0, 0)),
            full(lnp), full(w_in_b), full(conv_w), full(conv_b), full(w2cat), full(gbias),
        ],
        out_specs=[tok(w) for w, _ in widths],
        out_shape=[jax.ShapeDtypeStruct((bsz, t, w), dt) for w, dt in widths],
        compiler_params=_params(2),
        name="proj_latent" if latent else "proj_ctx",
    )(x, mod, lnp, w_in_b, conv_w, conv_b, w2cat, gbias)


def _tri(n, reverse, strict=False):
    i = lax.broadcasted_iota(jnp.int32, (n, n), 0)
    j = lax.broadcasted_iota(jnp.int32, (n, n), 1)
    if strict:
        m = (j > i) if reverse else (j < i)
    else:
        m = (j >= i) if reverse else (j <= i)
    return jnp.where(m, 1.0, 0.0).astype(BF16)


def _chunk_cumsum(g, reverse):
    tri = _tri(GLA_CHUNK, reverse)
    g1, g2, g3 = _split3(g)
    return _dot(tri, g1) + _dot(tri, g2) + _dot(tri, g3)


def _chunk_total_col(g):
    ones = jnp.ones((GLA_CHUNK, LANES), BF16)
    dn = (((0,), (0,)), ((), ()))
    g1, g2, g3 = _split3(g)
    tot = lambda a: lax.dot_general(a, ones, dn, preferred_element_type=F32)
    return tot(g1) + tot(g2) + tot(g3)


def _sub_anchors(gc, reverse):
    zero = jnp.zeros((1, GLA_KEY), F32)
    if reverse:
        return [gc[GLA_SUB * (a + 1):GLA_SUB * (a + 1) + 1] for a in range(N_SUB - 1)] + [zero]
    return [zero] + [gc[GLA_SUB * a - 1:GLA_SUB * a] for a in range(1, N_SUB)]


def _intra_scores(q, k, gc, reverse):
    r = _sub_anchors(gc, reverse)
    anchor = jnp.concatenate([jnp.broadcast_to(ra, (GLA_SUB, GLA_KEY)) for ra in r], axis=0)
    gcb = gc - anchor
    qt = q * jnp.exp(gcb)
    kt = k * jnp.exp(-gcb)
    pairs = [(a, b) for a in range(N_SUB) for b in range(N_SUB) if (b >= a if reverse else b <= a)]
    rows = []
    for a, b in pairs:
        qa = qt[GLA_SUB * a:GLA_SUB * (a + 1)]
        if a != b:
            qa = qa * jnp.exp(r[a] - r[b])
        rows.append(qa)
    qp = jnp.concatenate(rows, axis=0).astype(BF16)
    width = GLA_HEADS * GLA_CHUNK
    rr = lax.broadcasted_iota(jnp.int32, (width, GLA_KEY), 0)
    cc = lax.broadcasted_iota(jnp.int32, (width, GLA_KEY), 1)
    kbd = jnp.where(_div_pow2(rr, GLA_CHUNK) == _div_pow2(cc, GLA_DK),
                    jnp.concatenate([kt] * GLA_HEADS, axis=0), 0.0)
    res = lax.dot_general(qp, kbd.astype(BF16), (((1,), (1,)), ((), ())), preferred_element_type=F32)
    col = _mod_pow2(lax.broadcasted_iota(jnp.int32, (GLA_SUB, width), 1), GLA_CHUNK)
    col_blk = _div_pow2(col, GLA_SUB)
    col_pos = _mod_pow2(col, GLA_SUB)
    row_pos = lax.broadcasted_iota(jnp.int32, (GLA_SUB, width), 0)
    causal = (col_pos >= row_pos) if reverse else (col_pos <= row_pos)
    blocks = []
    for a in range(N_SUB):
        acc = jnp.zeros((GLA_SUB, width), F32)
        for idx, (pa, pb) in enumerate(pairs):
            if pa != a:
                continue
            keep = col_blk == pb
            if pa == pb:
                keep = keep & causal
            acc = acc + jnp.where(keep, res[GLA_SUB * idx:GLA_SUB * (idx + 1)], 0.0)
        blocks.append(acc)
    return jnp.concatenate(blocks, axis=0)


def _head_mask(rows_per_head, cols_per_head):
    shape = (GLA_HEADS * rows_per_head, GLA_HEADS * cols_per_head)
    rr = lax.broadcasted_iota(jnp.int32, shape, 0)
    cc = lax.broadcasted_iota(jnp.int32, shape, 1)
    return _div_pow2(rr, rows_per_head) == _div_pow2(cc, cols_per_head)


def _state_step(q, k, v_b, g, gc, state, reverse):
    total = gc[0:1] if reverse else gc[GLA_CHUNK - 1:GLA_CHUNK]
    o_inter = None
    if q is not None:
        o_inter = _dot((q * jnp.exp(gc)).astype(BF16), state.astype(BF16))
    k_end = (k * jnp.exp(total - gc)).astype(BF16)
    upd = lax.dot_general(k_end, v_b, (((0,), (0,)), ((), ())), preferred_element_type=F32)
    decay = jnp.exp(_chunk_total_col(g))
    decay = jnp.concatenate([decay] * (GLA_VAL // LANES), axis=1)
    new_state = state * decay + jnp.where(_head_mask(GLA_DK, GLA_DV), upd, 0.0)
    return o_inter, new_state


def _gla_kernel(tt, nt, with_out, *refs):
    if with_out:
        (qf_ref, kf_ref, vf_ref, gfw_ref, qb_ref, kb_ref, vb_ref, gbw_ref, s0f_ref, s0b_ref,
         of_ref, ob_ref, sf_ref, sb_ref) = refs
    else:
        kf_ref, vf_ref, gfw_ref, kb_ref, vb_ref, gbw_ref, s0f_ref, s0b_ref, sf_ref, sb_ref = refs
    j = pl.program_id(1)

    @pl.when(j == 0)
    def _():
        sf_ref[...] = s0f_ref[...]
        sb_ref[...] = s0b_ref[...]

    n_chunks = tt // GLA_CHUNK
    state = sf_ref[...]
    for c in range(n_chunks):
        sl = slice(c * GLA_CHUNK, (c + 1) * GLA_CHUNK)
        k = kf_ref[sl, :].astype(F32)
        v_b = vf_ref[sl, :]
        g = gfw_ref[sl, :]
        g_f = g[:, 0:GLA_KEY]
        gc_f = _chunk_cumsum(g_f, False)
        q = None
        if with_out:
            q = qf_ref[sl, :].astype(F32)
            gc_b = _chunk_cumsum(g[:, GLA_KEY:], True)
            scores = _intra_scores(q, k, gc_f, False) + _intra_scores(q, k, gc_b, True)
            vbd = jnp.where(_head_mask(GLA_CHUNK, GLA_DV), jnp.concatenate([v_b] * GLA_HEADS, axis=0),
                            jnp.zeros((), BF16))
            o_intra = _dot(scores.astype(BF16), vbd)
        o_inter, state = _state_step(q, k, v_b, g_f, gc_f, state, False)
        if with_out:
            of_ref[sl, :] = (o_intra + o_inter).astype(BF16)
    sf_ref[...] = state

    state = sb_ref[...]
    for c in reversed(range(n_chunks)):
        sl = slice(c * GLA_CHUNK, (c + 1) * GLA_CHUNK)
        k = kb_ref[sl, :].astype(F32)
        v_b = vb_ref[sl, :]
        g_b = gbw_ref[sl, :][:, GLA_KEY:]
        gc_b = _chunk_cumsum(g_b, True)
        q = qb_ref[sl, :].astype(F32) if with_out else None
        o_inter, state = _state_step(q, k, v_b, g_b, gc_b, state, True)
        if with_out:
            ob_ref[sl, :] = o_inter.astype(BF16)
    sb_ref[...] = state


def _gla(q, k, v, g, s0f, s0b):
    with_out = q is not None
    bsz, t, _ = k.shape
    tt = min(256, t)
    assert t % tt == 0 and tt % GLA_CHUNK == 0
    nt = t // tt
    fwd = lambda w: pl.BlockSpec((None, tt, w), lambda b, j: (b, j, 0))
    bwd = lambda w: pl.BlockSpec((None, tt, w), lambda b, j: (b, nt - 1 - j, 0))
    st = pl.BlockSpec((None, GLA_KEY, GLA_VAL), lambda b, j: (b, 0, 0))
    st_shape = jax.ShapeDtypeStruct((bsz, GLA_KEY, GLA_VAL), F32)
    if with_out:
        ins = [q, k, v, g, q, k, v, g, s0f, s0b]
        in_specs = [fwd(GLA_KEY), fwd(GLA_KEY), fwd(GLA_VAL), fwd(2 * GLA_KEY),
                    bwd(GLA_KEY), bwd(GLA_KEY), bwd(GLA_VAL), bwd(2 * GLA_KEY), st, st]
        out_specs = [fwd(GLA_VAL), bwd(GLA_VAL), st, st]
        o_shape = jax.ShapeDtypeStruct((bsz, t, GLA_VAL), BF16)
        out_shape = [o_shape, o_shape, st_shape, st_shape]
    else:
        ins = [k, v, g, k, v, g, s0f, s0b]
        in_specs = [fwd(GLA_KEY), fwd(GLA_VAL), fwd(2 * GLA_KEY),
                    bwd(GLA_KEY), bwd(GLA_VAL), bwd(2 * GLA_KEY), st, st]
        out_specs = [st, st]
        out_shape = [st_shape, st_shape]
    return pl.pallas_call(
        functools.partial(_gla_kernel, tt, nt, with_out),
        grid=(bsz, nt),
        in_specs=in_specs,
        out_specs=out_specs,
        out_shape=out_shape,
        compiler_params=_params(2),
        name="gla_latent" if with_out else "gla_ctx",
    )(*ins)


def _row_max(x):
    return jnp.max(x, axis=-1, keepdims=True)


def _row_min(x):
    return jnp.min(x, axis=-1, keepdims=True)


def _row_sum(x):
    return jnp.sum(x, axis=-1, keepdims=True)


def _exact_bf16_parts(x):
    hi = x.astype(BF16).astype(F32)
    r = x - hi
    mid = r.astype(BF16).astype(F32)
    lo = (r - mid).astype(BF16).astype(F32)
    return hi, mid, lo


def _route(logit, tm):
    lane = lax.broadcasted_iota(jnp.int32, (tm, ROUTE_W), 1).astype(F32)
    none = float(ROUTE_W - 1)
    gmask = lane < N_GROUPS
    eg = jnp.where(gmask, jnp.exp(logit - _row_max(jnp.where(gmask, logit, -jnp.inf))), 0.0)
    pg = eg / _row_sum(eg)
    p_g = _row_max(pg)
    g_idx = _row_min(jnp.where(gmask & (pg >= p_g), lane, none))
    first = N_GROUPS + EXPERTS_PER_GROUP * g_idx
    emask = (lane >= first) & (lane < first + EXPERTS_PER_GROUP)
    ee = jnp.where(emask, jnp.exp(logit - _row_max(jnp.where(emask, logit, -jnp.inf))), 0.0)
    pe = jnp.where(emask, ee / _row_sum(ee), -1.0)
    p1 = _row_max(pe)
    i1 = _row_min(jnp.where(emask & (pe >= p1), lane, none))
    rest = emask & (lane != i1)
    pe2 = jnp.where(rest, pe, -1.0)
    p2 = _row_max(pe2)
    i2 = _row_min(jnp.where(rest & (pe2 >= p2), lane, none))
    den = p1 + p2
    w1 = p1 / den * p_g
    w2 = p2 / den * p_g
    l1 = i1 - first
    l2 = i2 - first
    lo = jnp.minimum(l1, l2)
    hi = jnp.maximum(l1, l2)
    pair = lo * (7.0 - lo) * 0.5 + (hi - lo - 1.0)
    cls = g_idx * PAIRS_PER_GROUP + pair
    w_lo = jnp.where(l1 < l2, w1, w2)
    w_hi = jnp.where(l1 < l2, w2, w1)
    onehot = jnp.where(lane == cls, 1.0, 0.0)
    before = _dot(_tri(tm, False, strict=True), onehot.astype(BF16))
    count = jnp.sum(onehot, axis=0, keepdims=True)
    chunks = jnp.floor((count + (CHUNK_ROWS - 1.0)) * (1.0 / CHUNK_ROWS))
    chunks = jnp.where(lane[0:1] == float(N_CLASSES), LOCAL_CHUNKS - _row_sum(chunks), chunks)
    first_chunk = _dot(jnp.broadcast_to(chunks, (8, ROUTE_W)).astype(BF16), _tri(ROUTE_W, True, strict=True))[0:1]
    pos = _row_sum(onehot * (CHUNK_ROWS * first_chunk + before))
    return pos, w_lo, w_hi, chunks


def _sort_matrices(pos, tm):
    chunk_row0 = CHUNK_ROWS * lax.broadcasted_iota(jnp.int32, (tm, LOCAL_CHUNKS), 1).astype(F32)
    return [jnp.where(chunk_row0 + float(i) == pos, 1.0, 0.0).astype(BF16) for i in range(CHUNK_ROWS)]


def _mix_out_kernel(tm, x_ref, of_ref, ob_ref, sr_ref, ya_ref, mod_ref, lnp_ref, gn_ref, wo_ref, wr_ref, br_ref,
                    x1_ref, hxs_ref, pos_ref, chunks_ref):
    o = of_ref[...].astype(F32) + ob_ref[...].astype(F32)
    sr = sr_ref[...].astype(F32)
    y = _dot(ya_ref[...], wo_ref[0:CONV_CH, :])
    for h in range(GLA_HEADS):
        sl = slice(h * GLA_DV, (h + 1) * GLA_DV)
        oh = o[:, sl]
        ms = jnp.mean(oh * oh, axis=-1, keepdims=True)
        yb = oh * lax.rsqrt(ms + RMS_EPS) * gn_ref[...] * sr[:, sl]
        y = y + _dot(yb.astype(BF16), wo_ref[CONV_CH + h * GLA_DV:CONV_CH + (h + 1) * GLA_DV, :])
    xn = _layer_norm(x_ref[...], lnp_ref[0:1, :], lnp_ref[1:2, :])
    x1 = _layer_norm(DEEPNORM_ALPHA * xn + mod_ref[0:1, :] * y, lnp_ref[2:3, :], lnp_ref[3:4, :])
    x1_ref[...] = x1
    h2 = x1 * mod_ref[1:2, :] + mod_ref[2:3, :]
    logit = _dot3(h2, wr_ref[...]) + br_ref[...]
    pos, w_lo, w_hi, chunks = _route(logit, tm)
    lane = lax.broadcasted_iota(jnp.int32, (tm, ROUTE_W), 1)
    rec = jnp.zeros((tm, ROUTE_W), F32)
    for i, part in enumerate(_exact_bf16_parts(w_lo) + _exact_bf16_parts(w_hi)):
        rec = jnp.where(lane == i, part, rec)
    h2_b = h2.astype(BF16)
    rec_b = rec.astype(BF16)
    dn = (((0,), (0,)), ((), ()))
    for i, sort_t in enumerate(_sort_matrices(pos, tm)):
        base = i * MOE_ROW_W
        hxs_ref[:, base:base + D_MODEL] = lax.dot_general(sort_t, h2_b, dn, preferred_element_type=F32)
        hxs_ref[:, base + D_MODEL:base + MOE_ROW_W] = lax.dot_general(sort_t, rec_b, dn, preferred_element_type=F32)
    pos_ref[...] = jnp.broadcast_to(pos, (tm, ROUTE_W))
    chunks_ref[...] = jnp.broadcast_to(chunks, chunks_ref.shape)


def _mix_out(x, o_f, o_b, sr, ya, mod, lnp, gn, w_out_b, wr, br):
    bsz, t, _ = x.shape
    tm = SORT_TILE
    assert t % tm == 0
    n_t = t // tm
    tok = lambda w: pl.BlockSpec((None, tm, w), lambda b, i: (b, i, 0))
    full = lambda a: pl.BlockSpec(a.shape, lambda b, i: (0,) * a.ndim)
    flat = lambda rows, w: pl.BlockSpec((rows, w), lambda b, i: (b * n_t + i, 0))
    return pl.pallas_call(
        functools.partial(_mix_out_kernel, tm),
        grid=(bsz, n_t),
        in_specs=[
            tok(D_MODEL), tok(GLA_VAL), tok(GLA_VAL), tok(GLA_VAL), tok(CONV_CH),
            pl.BlockSpec((None, 3, D_MODEL), lambda b, i: (b, 0, 0)),
            full(lnp), full(gn), full(w_out_b), full(wr), full(br),
        ],
        out_specs=[tok(D_MODEL), flat(LOCAL_CHUNKS, CHUNK_ROWS * MOE_ROW_W), flat(tm, ROUTE_W), flat(8, ROUTE_W)],
        out_shape=[
            jax.ShapeDtypeStruct((bsz, t, D_MODEL), F32),
            jax.ShapeDtypeStruct((bsz * n_t * LOCAL_CHUNKS, CHUNK_ROWS * MOE_ROW_W), F32),
            jax.ShapeDtypeStruct((bsz * t, ROUTE_W), F32),
            jax.ShapeDtypeStruct((bsz * n_t * 8, ROUTE_W), F32),
        ],
        compiler_params=_params(2),
        name="mix_out",
    )(x, o_f, o_b, sr, ya, mod, lnp, gn, w_out_b, wr, br)


def _moe_kernel(n_rows, nused_ref, lo_ref, hi_ref, live_ref, src_ref,
                hxs_hbm, w13l_ref, w2l_ref, w13h_ref, w2h_ref, out_hbm, gbuf, obuf, gsem, ssem):
    tm = MOE_TILE
    i = pl.program_id(0)
    n_used = nused_ref[0]
    slot = lax.bitwise_and(i, 1)

    def gather_copy(tile, buf_slot, j):
        row = jnp.maximum(src_ref[tile * CHUNKS_PER_TILE + j], 0)
        return pltpu.make_async_copy(hxs_hbm.at[pl.ds(row, 1)], gbuf.at[buf_slot, pl.ds(j, 1)], gsem.at[buf_slot])

    def scatter_copy(tile, buf_slot, j):
        row = src_ref[tile * CHUNKS_PER_TILE + j]
        row = jnp.where(row < 0, n_rows + buf_slot * CHUNKS_PER_TILE + j, row)
        return pltpu.make_async_copy(obuf.at[buf_slot, pl.ds(j, 1)], out_hbm.at[pl.ds(row, 1)], ssem.at[buf_slot])

    def start_gather(tile, buf_slot):
        for j in range(CHUNKS_PER_TILE):
            gather_copy(tile, buf_slot, j).start()

    def wait_gather(buf_slot):
        pltpu.make_async_copy(hxs_hbm.at[pl.ds(0, CHUNKS_PER_TILE)], gbuf.at[buf_slot], gsem.at[buf_slot]).wait()

    def wait_scatter(buf_slot):
        pltpu.make_async_copy(obuf.at[buf_slot], out_hbm.at[pl.ds(0, CHUNKS_PER_TILE)], ssem.at[buf_slot]).wait()

    @pl.when(i == 0)
    def _():
        start_gather(0, 0)
        obuf[...] = jnp.zeros(obuf.shape, F32)
        for s in range(2):
            fill = pltpu.make_async_copy(obuf.at[s], out_hbm.at[pl.ds(n_rows + s * CHUNKS_PER_TILE, CHUNKS_PER_TILE)],
                                         ssem.at[s])
            fill.start()
            fill.wait()

    @pl.when(i + 1 < n_used)
    def _():
        start_gather(i + 1, 1 - slot)

    @pl.when(i < n_used)
    def _():
        wait_gather(slot)

        @pl.when(i >= 2)
        def _():
            wait_scatter(slot)

        @pl.when(live_ref[i] != 0)
        def _():
            xg = jnp.concatenate([gbuf[slot, :, r * MOE_ROW_W:(r + 1) * MOE_ROW_W] for r in range(CHUNK_ROWS)],
                                 axis=0)
            xb = xg[:, 0:D_MODEL].astype(BF16)
            w_lo = xg[:, D_MODEL:D_MODEL + 1] + xg[:, D_MODEL + 1:D_MODEL + 2] + xg[:, D_MODEL + 2:D_MODEL + 3]
            w_hi = xg[:, D_MODEL + 3:D_MODEL + 4] + xg[:, D_MODEL + 4:D_MODEL + 5] + xg[:, D_MODEL + 5:D_MODEL + 6]

            def expert(w13_ref, w2_ref):
                h13 = _dot(xb, w13_ref[...])
                act = _silu(h13[:, 0:D_EXPERT]) * h13[:, D_EXPERT:]
                return _dot(act.astype(BF16), w2_ref[...])

            y = w_lo * expert(w13l_ref, w2l_ref) + w_hi * expert(w13h_ref, w2h_ref)
            y = y.astype(BF16).astype(F32)
            for r in range(CHUNK_ROWS):
                obuf[slot, :, r * D_MODEL:(r + 1) * D_MODEL] = y[r * CHUNKS_PER_TILE:(r + 1) * CHUNKS_PER_TILE]

        @pl.when(live_ref[i] == 0)
        def _():
            obuf[slot] = jnp.zeros(obuf.shape[1:], F32)

        for j in range(CHUNKS_PER_TILE):
            scatter_copy(i, slot, j).start()

        @pl.when(i == n_used - 1)
        def _():
            wait_scatter(slot)

            @pl.when(i >= 1)
            def _():
                wait_scatter(1 - slot)


def _moe(hxs, src, n_used, tile_lo, tile_hi, tile_live, w13_b, w2_b):
    n_rows = hxs.shape[0]
    tm = MOE_TILE
    n_steps = src.shape[0] // CHUNKS_PER_TILE
    wspec = lambda which, shape: pl.BlockSpec(
        (None,) + shape, (lambda i, nu, lo, hi, lv, s: (lo[i], 0, 0)) if which == 0 else
        (lambda i, nu, lo, hi, lv, s: (hi[i], 0, 0)))
    grid_spec = pltpu.PrefetchScalarGridSpec(
        num_scalar_prefetch=5,
        grid=(n_steps,),
        in_specs=[
            pl.BlockSpec(memory_space=pl.ANY),
            wspec(0, (D_MODEL, 2 * D_EXPERT)), wspec(0, (D_EXPERT, D_MODEL)),
            wspec(1, (D_MODEL, 2 * D_EXPERT)), wspec(1, (D_EXPERT, D_MODEL)),
        ],
        out_specs=pl.BlockSpec(memory_space=pl.ANY),
        scratch_shapes=[
            pltpu.VMEM((2, CHUNKS_PER_TILE, CHUNK_ROWS * MOE_ROW_W), F32),
            pltpu.VMEM((2, CHUNKS_PER_TILE, CHUNK_ROWS * D_MODEL), F32),
            pltpu.SemaphoreType.DMA((2,)),
            pltpu.SemaphoreType.DMA((2,)),
        ],
    )
    return pl.pallas_call(
        functools.partial(_moe_kernel, n_rows),
        grid_spec=grid_spec,
        out_shape=jax.ShapeDtypeStruct((n_rows + 2 * CHUNKS_PER_TILE, CHUNK_ROWS * D_MODEL), F32),
        compiler_params=_params(1),
        name="moe",
    )(n_used, tile_lo, tile_hi, tile_live, src, hxs, w13_b, w2_b, w13_b, w2_b)


def _final_kernel(tm, x1_ref, moe_ref, pos_ref, mod_ref, lnp_ref, o_ref):
    moe = jnp.zeros((tm, D_MODEL), F32)
    for r, sort_t in enumerate(_sort_matrices(pos_ref[:, 0:1], tm)):
        moe = moe + _dot(sort_t, moe_ref[:, r * D_MODEL:(r + 1) * D_MODEL].astype(BF16))
    o_ref[...] = _layer_norm(DEEPNORM_ALPHA * x1_ref[...] + mod_ref[...] * moe, lnp_ref[0:1, :], lnp_ref[1:2, :])


def _final(x1, moe, pos, g2, lnp):
    bsz, t, _ = x1.shape
    tm = SORT_TILE
    n_t = t // tm
    flat = lambda rows, w: pl.BlockSpec((rows, w), lambda b, i: (b * n_t + i, 0))
    return pl.pallas_call(
        functools.partial(_final_kernel, tm),
        grid=(bsz, n_t),
        in_specs=[
            pl.BlockSpec((None, tm, D_MODEL), lambda b, i: (b, i, 0)),
            flat(LOCAL_CHUNKS, CHUNK_ROWS * D_MODEL), flat(tm, ROUTE_W),
            pl.BlockSpec((None, 1, D_MODEL), lambda b, i: (b, 0, 0)),
            pl.BlockSpec(lnp.shape, lambda b, i: (0, 0)),
        ],
        out_specs=pl.BlockSpec((None, tm, D_MODEL), lambda b, i: (b, i, 0)),
        out_shape=jax.ShapeDtypeStruct((bsz, t, D_MODEL), F32),
        compiler_params=_params(2),
        name="final",
    )(x1, moe, pos, g2, lnp)


def _pair_tables():
    lo, hi = [], []
    for g in range(N_GROUPS):
        for a in range(EXPERTS_PER_GROUP):
            for b in range(a + 1, EXPERTS_PER_GROUP):
                lo.append(g * EXPERTS_PER_GROUP + a)
                hi.append(g * EXPERTS_PER_GROUP + b)
    return jnp.array(lo, jnp.int32), jnp.array(hi, jnp.int32)


def _moe_plan(chunks, n_sort_tiles):
    n_cls = N_CLASSES + 1
    hp = lax.Precision.HIGHEST
    m = chunks.reshape(n_sort_tiles, 8, ROUTE_W)[:, 0, :n_cls].astype(jnp.int32)
    a_end = jnp.cumsum(m, axis=0)
    a_start = a_end - m
    per_cls = a_end[-1]
    padded = (per_cls + CHUNKS_PER_TILE - 1) // CHUNKS_PER_TILE * CHUNKS_PER_TILE
    g_end = jnp.cumsum(padded)
    g_start = g_end - padded
    local_off = jnp.cumsum(m, axis=1) - m
    seg = jnp.arange(n_sort_tiles, dtype=jnp.int32)[:, None] * LOCAL_CHUNKS + local_off - a_start
    n_steps = -(-(n_sort_tiles * LOCAL_CHUNKS) // CHUNKS_PER_TILE) + n_cls
    p = jnp.arange(n_steps * CHUNKS_PER_TILE, dtype=jnp.int32)
    cls_p = jnp.minimum(jnp.sum((g_end[None, :] <= p[:, None]).astype(jnp.int32), axis=1), n_cls - 1)
    onehot = (cls_p[:, None] == jnp.arange(n_cls, dtype=jnp.int32)[None, :]).astype(F32)
    pick = lambda tab: jnp.dot(onehot, tab.astype(F32), precision=hp)
    u = p - pick(g_start[:, None])[:, 0].astype(jnp.int32)
    valid = u < pick(per_cls[:, None])[:, 0].astype(jnp.int32)
    a_end_p = pick(a_end.T).astype(jnp.int32)
    seg_p = pick(seg.T).astype(jnp.int32)
    tile_p = jnp.sum((a_end_p <= u[:, None]).astype(jnp.int32), axis=1)
    hit = jnp.arange(n_sort_tiles, dtype=jnp.int32)[None, :] == tile_p[:, None]
    src = jnp.sum(jnp.where(hit, seg_p, 0), axis=1) + u
    src = jnp.where(valid, src, -1).astype(jnp.int32)
    n_used = g_end[-1:] // CHUNKS_PER_TILE
    step = jnp.arange(n_steps, dtype=jnp.int32)
    tile_cls = jnp.sum((g_end[None, :] // CHUNKS_PER_TILE <= step[:, None]).astype(jnp.int32), axis=1)
    live = ((tile_cls < N_CLASSES) & (step < n_used[0])).astype(jnp.int32)
    pair_lo, pair_hi = _pair_tables()
    pair_oh = (jnp.minimum(tile_cls, N_CLASSES - 1)[:, None] == jnp.arange(N_CLASSES)[None, :]).astype(jnp.int32)
    tile_lo = jnp.sum(pair_oh * pair_lo[None, :], axis=1).astype(jnp.int32)
    tile_hi = jnp.sum(pair_oh * pair_hi[None, :], axis=1).astype(jnp.int32)
    return src, n_used.astype(jnp.int32), tile_lo, tile_hi, live


def kernel(x, c, ctx, c_ctx, ln_in_g, ln_in_b, w_ada, b_ada, w_in, conv_w, conv_b, gate_w2_fwd, gate_b_fwd,
           gate_w2_bwd, gate_b_bwd, gla_norm_g, w_out, ln1_g, ln1_b, router_group_w, router_group_b,
           router_expert_w, router_expert_b, expert_w1, expert_w3, expert_w2, ln2_g, ln2_b):
    bsz, t, _ = x.shape
    n_tok = bsz * t
    l = 0
    rows = -(-(bsz + 1) // 8) * 8
    cond = jnp.zeros((rows, D_MODEL), F32).at[:bsz].set(c).at[bsz].set(c_ctx)
    ada = _ada(cond, w_ada[l], b_ada[l][None, :])
    sh1, sc1, g1, sh2, sc2, g2 = [ada[:, i * D_MODEL:(i + 1) * D_MODEL] for i in range(6)]

    w_in_b = w_in[l].astype(BF16)
    lnp_in = jnp.stack([ln_in_g, ln_in_b])
    zero = jnp.zeros((GLA_GATE_RANK, GLA_KEY), F32)
    w2cat = jnp.concatenate([jnp.concatenate([gate_w2_fwd[l], zero], axis=1),
                             jnp.concatenate([zero, gate_w2_bwd[l]], axis=1)], axis=0).astype(BF16)
    gbias = jnp.concatenate([gate_b_fwd[l], gate_b_bwd[l]])[None, :]

    mod_ctx = jnp.broadcast_to(jnp.stack([1.0 + sc1[bsz], sh1[bsz]])[None], (bsz, 2, D_MODEL))
    k_c, v_c, g_c = _proj(ctx, mod_ctx, lnp_in, w_in_b, conv_w[l], conv_b[l][None, :], w2cat, gbias, False)
    zero_state = jnp.zeros((bsz, GLA_KEY, GLA_VAL), F32)
    s_f, s_b = _gla(None, k_c, v_c, g_c, zero_state, zero_state)

    mod1 = jnp.stack([1.0 + sc1[:bsz], sh1[:bsz]], axis=1)
    ya, q, k, v, sr, g = _proj(x, mod1, lnp_in, w_in_b, conv_w[l], conv_b[l][None, :], w2cat, gbias, True)
    o_f, o_b, _, _ = _gla(q, k, v, g, s_f, s_b)

    mod2 = jnp.stack([g1[:bsz], 1.0 + sc2[:bsz], sh2[:bsz]], axis=1)
    lnp1 = jnp.stack([ln_in_g, ln_in_b, ln1_g[l], ln1_b[l]])
    wr = jnp.zeros((D_MODEL, ROUTE_W), F32)
    wr = wr.at[:, :N_GROUPS].set(router_group_w[l]).at[:, N_GROUPS:N_GROUPS + N_EXPERTS].set(router_expert_w[l])
    br = jnp.zeros((1, ROUTE_W), F32)
    br = br.at[0, :N_GROUPS].set(router_group_b[l]).at[0, N_GROUPS:N_GROUPS + N_EXPERTS].set(router_expert_b[l])
    x1, hxs, pos, chunks = _mix_out(x, o_f, o_b, sr, ya, mod2, lnp1, gla_norm_g[l][None, :],
                                    w_out[l].astype(BF16), wr, br)

    src, n_used, tile_lo, tile_hi, live = _moe_plan(chunks, n_tok // SORT_TILE)
    w13_b = jnp.concatenate([expert_w1[l], expert_w3[l]], axis=-1).astype(BF16)
    moe = _moe(hxs, src, n_used, tile_lo, tile_hi, live, w13_b, expert_w2[l].astype(BF16))

    return _final(x1, moe, pos, g2[:bsz][:, None, :], jnp.stack([ln2_g[l], ln2_b[l]]))
```

```python
import functools

import jax
import jax.numpy as jnp
from jax import lax
from jax.experimental import pallas as pl
from jax.experimental.pallas import tpu as pltpu

F32 = jnp.float32
BF16 = jnp.bfloat16

D_MODEL = 1024
GRID_W = 64
CONV_CH = 512
GLA_HEADS = 4
GLA_DK = 64
GLA_DV = 128
GLA_KEY = GLA_HEADS * GLA_DK
GLA_VAL = GLA_HEADS * GLA_DV
GLA_GATE_RANK = 16
GLA_TAU = 16.0
OFF_AB = 0
OFF_AC = OFF_AB + CONV_CH
OFF_AX = OFF_AC + CONV_CH
OFF_Q = OFF_AX + CONV_CH
OFF_K = OFF_Q + GLA_KEY
OFF_V = OFF_K + GLA_KEY
OFF_R = OFF_V + GLA_VAL
OFF_GF = OFF_R + GLA_VAL
D_PROJ = OFF_GF + 2 * GLA_GATE_RANK
N_GROUPS = 4
EXPERTS_PER_GROUP = 4
N_EXPERTS = N_GROUPS * EXPERTS_PER_GROUP
D_EXPERT = 512
PAIRS_PER_GROUP = 6
N_CLASSES = N_GROUPS * PAIRS_PER_GROUP
LN_EPS = 1e-5
RMS_EPS = 1e-6
DEPTH = 1
DEEPNORM_ALPHA = (2.0 * DEPTH) ** 0.25

LANES = 128
GLA_CHUNK = 64
GLA_SUB = 16
N_SUB = GLA_CHUNK // GLA_SUB
ROUTE_W = LANES
MOE_IN_WORDS = D_MODEL + ROUTE_W
MOE_OUT_WORDS = D_MODEL
SORT_TILE = 256
MOE_TILE = 256
CHUNK_ROWS = 4
LOCAL_CHUNKS = -(-(SORT_TILE + N_CLASSES * (CHUNK_ROWS - 1)) // (8 * CHUNK_ROWS)) * 8
CHUNKS_PER_TILE = MOE_TILE // CHUNK_ROWS
VMEM_LIMIT = 56 * 1024 * 1024


def _params(n_axes, vmem=VMEM_LIMIT):
    return pltpu.CompilerParams(dimension_semantics=("arbitrary",) * n_axes, vmem_limit_bytes=vmem)


def _dot(a, b):
    return jnp.dot(a, b, preferred_element_type=F32)


def _div_pow2(x, d):
    assert d & (d - 1) == 0
    return lax.shift_right_logical(x, jnp.int32(d.bit_length() - 1))


def _mod_pow2(x, d):
    assert d & (d - 1) == 0
    return lax.bitwise_and(x, jnp.int32(d - 1))


def _split2(x):
    hi = x.astype(BF16)
    lo = (x - hi.astype(F32)).astype(BF16)
    return hi, lo


def _split3(x):
    hi = x.astype(BF16)
    r = x - hi.astype(F32)
    mid = r.astype(BF16)
    lo = (r - mid.astype(F32)).astype(BF16)
    return hi, mid, lo


def _dot3(a, b):
    ah, al = _split2(a)
    bh, bl = _split2(b)
    return _dot(ah, bh) + _dot(ah, bl) + _dot(al, bh)


def _silu(x):
    return x / (1.0 + jnp.exp(-x))


def _layer_norm(x, g, b):
    mu = jnp.mean(x, axis=-1, keepdims=True)
    xc = x - mu
    var = jnp.mean(xc * xc, axis=-1, keepdims=True)
    return xc * lax.rsqrt(var + LN_EPS) * g + b


def _ada_kernel(c_ref, w_ref, b_ref, o_ref):
    o_ref[...] = _dot3(_silu(c_ref[...]), w_ref[...]) + b_ref[...]


def _ada(cond, w_ada, b_ada):
    rows = cond.shape[0]
    n_out = w_ada.shape[1]
    tn = 1024
    return pl.pallas_call(
        _ada_kernel,
        grid=(n_out // tn,),
        in_specs=[
            pl.BlockSpec((rows, D_MODEL), lambda j: (0, 0)),
            pl.BlockSpec((D_MODEL, tn), lambda j: (0, j)),
            pl.BlockSpec((1, tn), lambda j: (0, j)),
        ],
        out_specs=pl.BlockSpec((rows, tn), lambda j: (0, j)),
        out_shape=jax.ShapeDtypeStruct((rows, n_out), F32),
        compiler_params=_params(1),
        name="ada",
    )(cond, w_ada, b_ada)


def _log_sigmoid(z):
    return jnp.minimum(z, 0.0) - jnp.log(1.0 + jnp.exp(-jnp.abs(z)))


def _proj_kernel(latent, tm, x_ref, mod_ref, lnp_ref, w_ref, cw_ref, cb_ref, w2_ref, gbias_ref, *out_refs):
    x = x_ref[...]
    xn = _layer_norm(x, lnp_ref[0:1, :], lnp_ref[1:2, :])
    h = xn * mod_ref[0:1, :] + mod_ref[1:2, :]
    hb = h.astype(BF16)
    if latent:
        ya_ref, q_ref, k_ref, v_ref, sr_ref, g_ref = out_refs
        p = _dot(hb, w_ref[:, OFF_AB:OFF_Q])
        a_b = p[:, 0:CONV_CH]
        u = p[:, CONV_CH:2 * CONV_CH] * p[:, 2 * CONV_CH:3 * CONV_CH]
        pos = _mod_pow2(lax.broadcasted_iota(jnp.int32, (tm, 1), 0), GRID_W)
        u_prev = jnp.where(pos == 0, 0.0, pltpu.roll(u, 1, 0))
        u_next = jnp.where(pos == GRID_W - 1, 0.0, pltpu.roll(u, tm - 1, 0))
        conv = u_prev * cw_ref[0:1, :] + u * cw_ref[1:2, :] + u_next * cw_ref[2:3, :] + cb_ref[...]
        ya_ref[...] = (a_b * conv).astype(BF16)
        qk = _dot(hb, w_ref[:, OFF_Q:OFF_V])
        q_ref[...] = (qk[:, 0:GLA_KEY] * (GLA_DK ** -0.5)).astype(BF16)
        k_ref[...] = qk[:, GLA_KEY:].astype(BF16)
        r = _dot(hb, w_ref[:, OFF_R:OFF_GF])
        sr_ref[...] = _silu(r).astype(BF16)
    else:
        k_ref, v_ref, g_ref = out_refs
        k_ref[...] = _dot(hb, w_ref[:, OFF_K:OFF_V]).astype(BF16)
    v_ref[...] = _dot(hb, w_ref[:, OFF_V:OFF_R]).astype(BF16)
    low = _dot(hb, w_ref[:, OFF_GF:D_PROJ])
    z = _dot(low.astype(BF16), w2_ref[...]) + gbias_ref[...]
    g_ref[...] = _log_sigmoid(z) * (1.0 / GLA_TAU)


def _proj(x, mod, lnp, w_in_b, conv_w, conv_b, w2cat, gbias, latent):
    bsz, t, _ = x.shape
    tm = min(512, t)
    assert t % tm == 0 and tm % GRID_W == 0
    tok = lambda w: pl.BlockSpec((None, tm, w), lambda b, i: (b, i, 0))
    full = lambda a: pl.BlockSpec(a.shape, lambda b, i: (0,) * a.ndim)
    widths = ([(CONV_CH, BF16), (GLA_KEY, BF16)] if latent else []) + [(GLA_KEY, BF16), (GLA_VAL, BF16)]
    widths += ([(GLA_VAL, BF16)] if latent else []) + [(2 * GLA_KEY, F32)]
    return pl.pallas_call(
        functools.partial(_proj_kernel, latent, tm),
        grid=(bsz, t // tm),
        in_specs=[
            tok(D_MODEL),
            pl.BlockSpec((None, 2, D_MODEL), lambda b, i: (b, 0, 0)),
            full(lnp), full(w_in_b), full(conv_w), full(conv_b), full(w2cat), full(gbias),
        ],
        out_specs=[tok(w) for w, _ in widths],
        out_shape=[jax.ShapeDtypeStruct((bsz, t, w), dt) for w, dt in widths],
        compiler_params=_params(2),
        name="proj_latent" if latent else "proj_ctx",
    )(x, mod, lnp, w_in_b, conv_w, conv_b, w2cat, gbias)


def _tri(n, reverse, strict=False):
    i = lax.broadcasted_iota(jnp.int32, (n, n), 0)
    j = lax.broadcasted_iota(jnp.int32, (n, n), 1)
    if strict:
        m = (j > i) if reverse else (j < i)
    else:
        m = (j >= i) if reverse else (j <= i)
    return jnp.where(m, 1.0, 0.0).astype(BF16)


def _chunk_cumsum(g, reverse):
    tri = _tri(GLA_CHUNK, reverse)
    g1, g2, g3 = _split3(g)
    return _dot(tri, g1) + _dot(tri, g2) + _dot(tri, g3)


def _chunk_total_col(g):
    ones = jnp.ones((GLA_CHUNK, LANES), BF16)
    dn = (((0,), (0,)), ((), ()))
    g1, g2, g3 = _split3(g)
    tot = lambda a: lax.dot_general(a, ones, dn, preferred_element_type=F32)
    return tot(g1) + tot(g2) + tot(g3)


def _sub_anchors(gc, reverse):
    zero = jnp.zeros((1, GLA_KEY), F32)
    if reverse:
        return [gc[GLA_SUB * (a + 1):GLA_SUB * (a + 1) + 1] for a in range(N_SUB - 1)] + [zero]
    return [zero] + [gc[GLA_SUB * a - 1:GLA_SUB * a] for a in range(1, N_SUB)]


def _intra_scores(q, k, gc, reverse):
    r = _sub_anchors(gc, reverse)
    anchor = jnp.concatenate([jnp.broadcast_to(ra, (GLA_SUB, GLA_KEY)) for ra in r], axis=0)
    gcb = gc - anchor
    qt = q * jnp.exp(gcb)
    kt = k * jnp.exp(-gcb)
    pairs = [(a, b) for a in range(N_SUB) for b in range(N_SUB) if (b >= a if reverse else b <= a)]
    rows = []
    for a, b in pairs:
        qa = qt[GLA_SUB * a:GLA_SUB * (a + 1)]
        if a != b:
            qa = qa * jnp.exp(r[a] - r[b])
        rows.append(qa)
    qp = jnp.concatenate(rows, axis=0).astype(BF16)
    width = GLA_HEADS * GLA_CHUNK
    rr = lax.broadcasted_iota(jnp.int32, (width, GLA_KEY), 0)
    cc = lax.broadcasted_iota(jnp.int32, (width, GLA_KEY), 1)
    kbd = jnp.where(_div_pow2(rr, GLA_CHUNK) == _div_pow2(cc, GLA_DK),
                    jnp.concatenate([kt] * GLA_HEADS, axis=0), 0.0)
    res = lax.dot_general(qp, kbd.astype(BF16), (((1,), (1,)), ((), ())), preferred_element_type=F32)
    col = _mod_pow2(lax.broadcasted_iota(jnp.int32, (GLA_SUB, width), 1), GLA_CHUNK)
    col_blk = _div_pow2(col, GLA_SUB)
    col_pos = _mod_pow2(col, GLA_SUB)
    row_pos = lax.broadcasted_iota(jnp.int32, (GLA_SUB, width), 0)
    causal = (col_pos >= row_pos) if reverse else (col_pos <= row_pos)
    blocks = []
    for a in range(N_SUB):
        acc = jnp.zeros((GLA_SUB, width), F32)
        for idx, (pa, pb) in enumerate(pairs):
            if pa != a:
                continue
            keep = col_blk == pb
            if pa == pb:
                keep = keep & causal
            acc = acc + jnp.where(keep, res[GLA_SUB * idx:GLA_SUB * (idx + 1)], 0.0)
        blocks.append(acc)
    return jnp.concatenate(blocks, axis=0)


def _head_mask(rows_per_head, cols_per_head):
    shape = (GLA_HEADS * rows_per_head, GLA_HEADS * cols_per_head)
    rr = lax.broadcasted_iota(jnp.int32, shape, 0)
    cc = lax.broadcasted_iota(jnp.int32, shape, 1)
    return _div_pow2(rr, rows_per_head) == _div_pow2(cc, cols_per_head)


def _state_step(q, k, v_b, g, gc, state, reverse):
    total = gc[0:1] if reverse else gc[GLA_CHUNK - 1:GLA_CHUNK]
    o_inter = None
    if q is not None:
        o_inter = _dot((q * jnp.exp(gc)).astype(BF16), state.astype(BF16))
    k_end = (k * jnp.exp(total - gc)).astype(BF16)
    upd = lax.dot_general(k_end, v_b, (((0,), (0,)), ((), ())), preferred_element_type=F32)
    decay = jnp.exp(_chunk_total_col(g))
    decay = jnp.concatenate([decay] * (GLA_VAL // LANES), axis=1)
    new_state = state * decay + jnp.where(_head_mask(GLA_DK, GLA_DV), upd, 0.0)
    return o_inter, new_state


def _gla_kernel(tt, nt, with_out, *refs):
    if with_out:
        (qf_ref, kf_ref, vf_ref, gfw_ref, qb_ref, kb_ref, vb_ref, gbw_ref, s0f_ref, s0b_ref,
         of_ref, ob_ref, sf_ref, sb_ref) = refs
    else:
        kf_ref, vf_ref, gfw_ref, kb_ref, vb_ref, gbw_ref, s0f_ref, s0b_ref, sf_ref, sb_ref = refs
    j = pl.program_id(1)

    @pl.when(j == 0)
    def _():
        sf_ref[...] = s0f_ref[...]
        sb_ref[...] = s0b_ref[...]

    n_chunks = tt // GLA_CHUNK
    state = sf_ref[...]
    for c in range(n_chunks):
        sl = slice(c * GLA_CHUNK, (c + 1) * GLA_CHUNK)
        k = kf_ref[sl, :].astype(F32)
        v_b = vf_ref[sl, :]
        g = gfw_ref[sl, :]
        g_f = g[:, 0:GLA_KEY]
        gc_f = _chunk_cumsum(g_f, False)
        q = None
        if with_out:
            q = qf_ref[sl, :].astype(F32)
            gc_b = _chunk_cumsum(g[:, GLA_KEY:], True)
            scores = _intra_scores(q, k, gc_f, False) + _intra_scores(q, k, gc_b, True)
            vbd = jnp.where(_head_mask(GLA_CHUNK, GLA_DV), jnp.concatenate([v_b] * GLA_HEADS, axis=0),
                            jnp.zeros((), BF16))
            o_intra = _dot(scores.astype(BF16), vbd)
        o_inter, state = _state_step(q, k, v_b, g_f, gc_f, state, False)
        if with_out:
            of_ref[sl, :] = (o_intra + o_inter).astype(BF16)
    sf_ref[...] = state

    state = sb_ref[...]
    for c in reversed(range(n_chunks)):
        sl = slice(c * GLA_CHUNK, (c + 1) * GLA_CHUNK)
        k = kb_ref[sl, :].astype(F32)
        v_b = vb_ref[sl, :]
        g_b = gbw_ref[sl, :][:, GLA_KEY:]
        gc_b = _chunk_cumsum(g_b, True)
        q = qb_ref[sl, :].astype(F32) if with_out else None
        o_inter, state = _state_step(q, k, v_b, g_b, gc_b, state, True)
        if with_out:
            ob_ref[sl, :] = o_inter.astype(BF16)
    sb_ref[...] = state


def _gla(q, k, v, g, s0f, s0b):
    with_out = q is not None
    bsz, t, _ = k.shape
    tt = min(256, t)
    assert t % tt == 0 and tt % GLA_CHUNK == 0
    nt = t // tt
    fwd = lambda w: pl.BlockSpec((None, tt, w), lambda b, j: (b, j, 0))
    bwd = lambda w: pl.BlockSpec((None, tt, w), lambda b, j: (b, nt - 1 - j, 0))
    st = pl.BlockSpec((None, GLA_KEY, GLA_VAL), lambda b, j: (b, 0, 0))
    st_shape = jax.ShapeDtypeStruct((bsz, GLA_KEY, GLA_VAL), F32)
    if with_out:
        ins = [q, k, v, g, q, k, v, g, s0f, s0b]
        in_specs = [fwd(GLA_KEY), fwd(GLA_KEY), fwd(GLA_VAL), fwd(2 * GLA_KEY),
                    bwd(GLA_KEY), bwd(GLA_KEY), bwd(GLA_VAL), bwd(2 * GLA_KEY), st, st]
        out_specs = [fwd(GLA_VAL), bwd(GLA_VAL), st, st]
        o_shape = jax.ShapeDtypeStruct((bsz, t, GLA_VAL), BF16)
        out_shape = [o_shape, o_shape, st_shape, st_shape]
    else:
        ins = [k, v, g, k, v, g, s0f, s0b]
        in_specs = [fwd(GLA_KEY), fwd(GLA_VAL), fwd(2 * GLA_KEY),
                    bwd(GLA_KEY), bwd(GLA_VAL), bwd(2 * GLA_KEY), st, st]
        out_specs = [st, st]
        out_shape = [st_shape, st_shape]
    return pl.pallas_call(
        functools.partial(_gla_kernel, tt, nt, with_out),
        grid=(bsz, nt),
        in_specs=in_specs,
        out_specs=out_specs,
        out_shape=out_shape,
        compiler_params=_params(2),
        name="gla_latent" if with_out else "gla_ctx",
    )(*ins)


def _row_max(x):
    return jnp.max(x, axis=-1, keepdims=True)


def _row_min(x):
    return jnp.min(x, axis=-1, keepdims=True)


def _row_sum(x):
    return jnp.sum(x, axis=-1, keepdims=True)


def _exact_bf16_parts(x):
    hi = x.astype(BF16).astype(F32)
    r = x - hi
    mid = r.astype(BF16).astype(F32)
    lo = (r - mid).astype(BF16).astype(F32)
    return hi, mid, lo


def _route(logit, tm):
    lane = lax.broadcasted_iota(jnp.int32, (tm, ROUTE_W), 1).astype(F32)
    none = float(ROUTE_W - 1)
    gmask = lane < N_GROUPS
    eg = jnp.where(gmask, jnp.exp(logit - _row_max(jnp.where(gmask, logit, -jnp.inf))), 0.0)
    pg = eg / _row_sum(eg)
    p_g = _row_max(pg)
    g_idx = _row_min(jnp.where(gmask & (pg >= p_g), lane, none))
    first = N_GROUPS + EXPERTS_PER_GROUP * g_idx
    emask = (lane >= first) & (lane < first + EXPERTS_PER_GROUP)
    ee = jnp.where(emask, jnp.exp(logit - _row_max(jnp.where(emask, logit, -jnp.inf))), 0.0)
    pe = jnp.where(emask, ee / _row_sum(ee), -1.0)
    p1 = _row_max(pe)
    i1 = _row_min(jnp.where(emask & (pe >= p1), lane, none))
    rest = emask & (lane != i1)
    pe2 = jnp.where(rest, pe, -1.0)
    p2 = _row_max(pe2)
    i2 = _row_min(jnp.where(rest & (pe2 >= p2), lane, none))
    den = p1 + p2
    w1 = p1 / den * p_g
    w2 = p2 / den * p_g
    l1 = i1 - first
    l2 = i2 - first
    lo = jnp.minimum(l1, l2)
    hi = jnp.maximum(l1, l2)
    pair = lo * (7.0 - lo) * 0.5 + (hi - lo - 1.0)
    cls = g_idx * PAIRS_PER_GROUP + pair
    w_lo = jnp.where(l1 < l2, w1, w2)
    w_hi = jnp.where(l1 < l2, w2, w1)
    onehot = jnp.where(lane == cls, 1.0, 0.0)
    before = _dot(_tri(tm, False, strict=True), onehot.astype(BF16))
    count = jnp.sum(onehot, axis=0, keepdims=True)
    chunks = jnp.floor((count + (CHUNK_ROWS - 1.0)) * (1.0 / CHUNK_ROWS))
    chunks = jnp.where(lane[0:1] == float(N_CLASSES), LOCAL_CHUNKS - _row_sum(chunks), chunks)
    first_chunk = _dot(jnp.broadcast_to(chunks, (8, ROUTE_W)).astype(BF16), _tri(ROUTE_W, True, strict=True))[0:1]
    val = onehot * (CHUNK_ROWS * first_chunk + before)
    pos = _row_sum(val)
    digit_hi = jnp.floor(val * (1.0 / 16.0))
    digit_lo = val - 16.0 * digit_hi
    ones = jnp.ones((8, ROUTE_W), BF16)
    nt = (((1,), (1,)), ((), ()))
    pos_row = (16.0 * lax.dot_general(ones, digit_hi.astype(BF16), nt, preferred_element_type=F32)
               + lax.dot_general(ones, digit_lo.astype(BF16), nt, preferred_element_type=F32))[0:1]
    return pos, pos_row, w_lo, w_hi, chunks


def _sort_matrices(pos, tm, chunk_axis):
    shape = (tm, LOCAL_CHUNKS) if chunk_axis == 1 else (LOCAL_CHUNKS, tm)
    chunk_row0 = CHUNK_ROWS * lax.broadcasted_iota(jnp.int32, shape, chunk_axis).astype(F32)
    return [jnp.where(chunk_row0 + float(i) == pos, 1.0, 0.0).astype(BF16) for i in range(CHUNK_ROWS)]


def _mix_out_kernel(tm, x_ref, of_ref, ob_ref, sr_ref, ya_ref, mod_ref, lnp_ref, gn_ref, wo_ref, wr_ref, br_ref,
                    x1_ref, hxs_ref, pos_ref, chunks_ref):
    o = of_ref[...].astype(F32) + ob_ref[...].astype(F32)
    sr = sr_ref[...].astype(F32)
    y = _dot(ya_ref[...], wo_ref[0:CONV_CH, :])
    for h in range(GLA_HEADS):
        sl = slice(h * GLA_DV, (h + 1) * GLA_DV)
        oh = o[:, sl]
        ms = jnp.mean(oh * oh, axis=-1, keepdims=True)
        yb = oh * lax.rsqrt(ms + RMS_EPS) * gn_ref[...] * sr[:, sl]
        y = y + _dot(yb.astype(BF16), wo_ref[CONV_CH + h * GLA_DV:CONV_CH + (h + 1) * GLA_DV, :])
    xn = _layer_norm(x_ref[...], lnp_ref[0:1, :], lnp_ref[1:2, :])
    x1 = _layer_norm(DEEPNORM_ALPHA * xn + mod_ref[0:1, :] * y, lnp_ref[2:3, :], lnp_ref[3:4, :])
    x1_ref[...] = x1
    h2 = x1 * mod_ref[1:2, :] + mod_ref[2:3, :]
    logit = _dot3(h2, wr_ref[...]) + br_ref[...]
    pos, pos_row, w_lo, w_hi, chunks = _route(logit, tm)
    lane = lax.broadcasted_iota(jnp.int32, (tm, ROUTE_W), 1)
    rec = jnp.zeros((tm, ROUTE_W), F32)
    for i, part in enumerate(_exact_bf16_parts(w_lo) + _exact_bf16_parts(w_hi)):
        rec = jnp.where(lane == i, part, rec)
    h2_b = h2.astype(BF16)
    rec_b = rec.astype(BF16)
    for i, sort_m in enumerate(_sort_matrices(pos_row, tm, 0)):
        base = i * MOE_IN_WORDS
        hxs_ref[:, base:base + D_MODEL] = _dot(sort_m, h2_b)
        hxs_ref[:, base + D_MODEL:base + MOE_IN_WORDS] = _dot(sort_m, rec_b)
    pos_ref[...] = jnp.broadcast_to(pos, (tm, ROUTE_W))
    chunks_ref[...] = jnp.broadcast_to(chunks, chunks_ref.shape)


def _mix_out(x, o_f, o_b, sr, ya, mod, lnp, gn, w_out_b, wr, br):
    bsz, t, _ = x.shape
    tm = SORT_TILE
    assert t % tm == 0
    n_t = t // tm
    tok = lambda w: pl.BlockSpec((None, tm, w), lambda b, i: (b, i, 0))
    full = lambda a: pl.BlockSpec(a.shape, lambda b, i: (0,) * a.ndim)
    flat = lambda rows, w: pl.BlockSpec((rows, w), lambda b, i: (b * n_t + i, 0))
    return pl.pallas_call(
        functools.partial(_mix_out_kernel, tm),
        grid=(bsz, n_t),
        in_specs=[
            tok(D_MODEL), tok(GLA_VAL), tok(GLA_VAL), tok(GLA_VAL), tok(CONV_CH),
            pl.BlockSpec((None, 3, D_MODEL), lambda b, i: (b, 0, 0)),
            full(lnp), full(gn), full(w_out_b), full(wr), full(br),
        ],
        out_specs=[tok(D_MODEL), flat(LOCAL_CHUNKS, CHUNK_ROWS * MOE_IN_WORDS), flat(tm, ROUTE_W), flat(8, ROUTE_W)],
        out_shape=[
            jax.ShapeDtypeStruct((bsz, t, D_MODEL), F32),
            jax.ShapeDtypeStruct((bsz * n_t * LOCAL_CHUNKS, CHUNK_ROWS * MOE_IN_WORDS), F32),
            jax.ShapeDtypeStruct((bsz * t, ROUTE_W), F32),
            jax.ShapeDtypeStruct((bsz * n_t * 8, ROUTE_W), F32),
        ],
        compiler_params=_params(2),
        name="mix_out",
    )(x, o_f, o_b, sr, ya, mod, lnp, gn, w_out_b, wr, br)


def _moe_kernel(n_rows, lo_ref, hi_ref, live_ref, src_ref,
                hxs_hbm, w13l_ref, w2l_ref, w13h_ref, w2h_ref, out_hbm, gbuf, obuf, gsem, ssem):
    i = pl.program_id(0)
    last = pl.num_programs(0) - 1
    slot = lax.bitwise_and(i, 1)
    other = 1 - slot

    def gather_copy(tile, buf_slot, j):
        row = jnp.maximum(src_ref[(tile + 1) * CHUNKS_PER_TILE + j], 0)
        return pltpu.make_async_copy(hxs_hbm.at[pl.ds(row, 1)], gbuf.at[buf_slot, pl.ds(j, 1)], gsem.at[buf_slot])

    def scatter_copy(tile, buf_slot, j):
        row = src_ref[(tile + 1) * CHUNKS_PER_TILE + j]
        row = jnp.where(row < 0, n_rows + buf_slot * CHUNKS_PER_TILE + j, row)
        return pltpu.make_async_copy(obuf.at[buf_slot, pl.ds(j, 1)], out_hbm.at[pl.ds(row, 1)], ssem.at[buf_slot])

    def wait_gather(buf_slot):
        pltpu.make_async_copy(hxs_hbm.at[pl.ds(0, CHUNKS_PER_TILE)], gbuf.at[buf_slot], gsem.at[buf_slot]).wait()

    def wait_scatter(buf_slot):
        pltpu.make_async_copy(obuf.at[buf_slot], out_hbm.at[pl.ds(0, CHUNKS_PER_TILE)], ssem.at[buf_slot]).wait()

    @pl.when(i == 0)
    def _():
        for j in range(CHUNKS_PER_TILE):
            gather_copy(0, 0, j).start()
        obuf[...] = jnp.zeros(obuf.shape, F32)
        for s in range(2):
            fill = pltpu.make_async_copy(obuf.at[s], out_hbm.at[pl.ds(n_rows + s * CHUNKS_PER_TILE, CHUNKS_PER_TILE)],
                                         ssem.at[s])
            fill.start()
            fill.wait()

    wait_gather(slot)

    @pl.when(i >= 1)
    def _():
        wait_scatter(slot)

    def neighbour_copies():
        for j in range(CHUNKS_PER_TILE):
            yield gather_copy(i + 1, other, j)
            yield scatter_copy(i - 1, other, j)

    @pl.when(live_ref[i] != 0)
    def _():
        copies = neighbour_copies()

        def issue(n):
            for _ in range(n):
                next(copies).start()

        per_stage = 2 * CHUNKS_PER_TILE // 8
        issue(per_stage)
        xw = jnp.concatenate([gbuf[slot, :, r * MOE_IN_WORDS:(r + 1) * MOE_IN_WORDS] for r in range(CHUNK_ROWS)],
                             axis=0)
        xb = xw[:, 0:D_MODEL].astype(BF16)
        rec = xw[:, D_MODEL:MOE_IN_WORDS]
        y = None
        for w13_ref, w2_ref, first in ((w13l_ref, w2l_ref, 0), (w13h_ref, w2h_ref, 3)):
            gate = _dot(xb, w13_ref[:, 0:D_EXPERT])
            issue(per_stage)
            up = _dot(xb, w13_ref[:, D_EXPERT:])
            issue(per_stage)
            e = _dot((_silu(gate) * up).astype(BF16), w2_ref[...])
            issue(per_stage)
            e = (rec[:, first:first + 1] + rec[:, first + 1:first + 2] + rec[:, first + 2:first + 3]) * e
            y = e if y is None else y + e
        issue(per_stage)
        yw = y.astype(BF16).astype(F32)
        for r in range(CHUNK_ROWS):
            obuf[slot, :, r * MOE_OUT_WORDS:(r + 1) * MOE_OUT_WORDS] = yw[r * CHUNKS_PER_TILE:(r + 1) * CHUNKS_PER_TILE]

    @pl.when(live_ref[i] == 0)
    def _():
        obuf[slot] = jnp.zeros(obuf.shape[1:], F32)
        for copy in neighbour_copies():
            copy.start()

    @pl.when(i == last)
    def _():
        for j in range(CHUNKS_PER_TILE):
            scatter_copy(i, slot, j).start()
        wait_scatter(slot)
        wait_scatter(other)
        wait_gather(other)


def _moe(hxs, src, tile_lo, tile_hi, tile_live, w13_b, w2_b):
    n_rows = hxs.shape[0]
    n_steps = src.shape[0] // CHUNKS_PER_TILE - 2
    wspec = lambda which, shape: pl.BlockSpec(
        (None,) + shape, (lambda i, lo, hi, lv, s: (lo[i], 0, 0)) if which == 0 else
        (lambda i, lo, hi, lv, s: (hi[i], 0, 0)))
    grid_spec = pltpu.PrefetchScalarGridSpec(
        num_scalar_prefetch=4,
        grid=(n_steps,),
        in_specs=[
            pl.BlockSpec(memory_space=pl.ANY),
            wspec(0, (D_MODEL, 2 * D_EXPERT)), wspec(0, (D_EXPERT, D_MODEL)),
            wspec(1, (D_MODEL, 2 * D_EXPERT)), wspec(1, (D_EXPERT, D_MODEL)),
        ],
        out_specs=pl.BlockSpec(memory_space=pl.ANY),
        scratch_shapes=[
            pltpu.VMEM((2, CHUNKS_PER_TILE, CHUNK_ROWS * MOE_IN_WORDS), F32),
            pltpu.VMEM((2, CHUNKS_PER_TILE, CHUNK_ROWS * MOE_OUT_WORDS), F32),
            pltpu.SemaphoreType.DMA((2,)),
            pltpu.SemaphoreType.DMA((2,)),
        ],
    )
    return pl.pallas_call(
        functools.partial(_moe_kernel, n_rows),
        grid_spec=grid_spec,
        out_shape=jax.ShapeDtypeStruct((n_rows + 2 * CHUNKS_PER_TILE, CHUNK_ROWS * MOE_OUT_WORDS), F32),
        compiler_params=_params(1),
        name="moe",
    )(tile_lo, tile_hi, tile_live, src, hxs, w13_b, w2_b, w13_b, w2_b)


def _final_kernel(tm, x1_ref, moe_ref, pos_ref, mod_ref, lnp_ref, o_ref):
    moe = jnp.zeros((tm, D_MODEL), F32)
    for r, sort_t in enumerate(_sort_matrices(pos_ref[:, 0:1], tm, 1)):
        moe = moe + _dot(sort_t, moe_ref[:, r * MOE_OUT_WORDS:(r + 1) * MOE_OUT_WORDS].astype(BF16))
    o_ref[...] = _layer_norm(DEEPNORM_ALPHA * x1_ref[...] + mod_ref[...] * moe, lnp_ref[0:1, :], lnp_ref[1:2, :])


def _final(x1, moe, pos, g2, lnp):
    bsz, t, _ = x1.shape
    tm = SORT_TILE
    n_t = t // tm
    flat = lambda rows, w: pl.BlockSpec((rows, w), lambda b, i: (b * n_t + i, 0))
    return pl.pallas_call(
        functools.partial(_final_kernel, tm),
        grid=(bsz, n_t),
        in_specs=[
            pl.BlockSpec((None, tm, D_MODEL), lambda b, i: (b, i, 0)),
            flat(LOCAL_CHUNKS, CHUNK_ROWS * MOE_OUT_WORDS), flat(tm, ROUTE_W),
            pl.BlockSpec((None, 1, D_MODEL), lambda b, i: (b, 0, 0)),
            pl.BlockSpec(lnp.shape, lambda b, i: (0, 0)),
        ],
        out_specs=pl.BlockSpec((None, tm, D_MODEL), lambda b, i: (b, i, 0)),
        out_shape=jax.ShapeDtypeStruct((bsz, t, D_MODEL), F32),
        compiler_params=_params(2),
        name="final",
    )(x1, moe, pos, g2, lnp)


def _pair_tables():
    lo, hi = [], []
    for g in range(N_GROUPS):
        for a in range(EXPERTS_PER_GROUP):
            for b in range(a + 1, EXPERTS_PER_GROUP):
                lo.append(g * EXPERTS_PER_GROUP + a)
                hi.append(g * EXPERTS_PER_GROUP + b)
    return jnp.array(lo, jnp.int32), jnp.array(hi, jnp.int32)


def _moe_plan(chunks, n_sort_tiles):
    n_cls = N_CLASSES + 1
    hp = lax.Precision.HIGHEST
    m = chunks.reshape(n_sort_tiles, 8, ROUTE_W)[:, 0, :n_cls].astype(jnp.int32)
    a_end = jnp.cumsum(m, axis=0)
    a_start = a_end - m
    per_cls = a_end[-1]
    padded = (per_cls + CHUNKS_PER_TILE - 1) // CHUNKS_PER_TILE * CHUNKS_PER_TILE
    g_end = jnp.cumsum(padded)
    g_start = g_end - padded
    local_off = jnp.cumsum(m, axis=1) - m
    seg = jnp.arange(n_sort_tiles, dtype=jnp.int32)[:, None] * LOCAL_CHUNKS + local_off - a_start
    n_steps = -(-(n_sort_tiles * LOCAL_CHUNKS) // CHUNKS_PER_TILE) + n_cls
    p = jnp.arange(n_steps * CHUNKS_PER_TILE, dtype=jnp.int32)
    cls_p = jnp.minimum(jnp.sum((g_end[None, :] <= p[:, None]).astype(jnp.int32), axis=1), n_cls - 1)
    onehot = (cls_p[:, None] == jnp.arange(n_cls, dtype=jnp.int32)[None, :]).astype(F32)
    pick = lambda tab: jnp.dot(onehot, tab.astype(F32), precision=hp)
    u = p - pick(g_start[:, None])[:, 0].astype(jnp.int32)
    valid = u < pick(per_cls[:, None])[:, 0].astype(jnp.int32)
    a_end_p = pick(a_end.T).astype(jnp.int32)
    seg_p = pick(seg.T).astype(jnp.int32)
    tile_p = jnp.sum((a_end_p <= u[:, None]).astype(jnp.int32), axis=1)
    hit = jnp.arange(n_sort_tiles, dtype=jnp.int32)[None, :] == tile_p[:, None]
    src = jnp.sum(jnp.where(hit, seg_p, 0), axis=1) + u
    src = jnp.where(valid, src, -1).astype(jnp.int32)
    pad_tile = jnp.full((CHUNKS_PER_TILE,), -1, jnp.int32)
    src = jnp.concatenate([pad_tile, src, pad_tile])
    n_used = g_end[-1] // CHUNKS_PER_TILE
    step = jnp.arange(n_steps, dtype=jnp.int32)
    tile_cls = jnp.sum((g_end[None, :] // CHUNKS_PER_TILE <= step[:, None]).astype(jnp.int32), axis=1)
    live = ((tile_cls < N_CLASSES) & (step < n_used)).astype(jnp.int32)
    pair_lo, pair_hi = _pair_tables()
    pair_oh = (jnp.minimum(tile_cls, N_CLASSES - 1)[:, None] == jnp.arange(N_CLASSES)[None, :]).astype(jnp.int32)
    tile_lo = jnp.sum(pair_oh * pair_lo[None, :], axis=1).astype(jnp.int32)
    tile_hi = jnp.sum(pair_oh * pair_hi[None, :], axis=1).astype(jnp.int32)
    return src, tile_lo, tile_hi, live


def kernel(x, c, ctx, c_ctx, ln_in_g, ln_in_b, w_ada, b_ada, w_in, conv_w, conv_b, gate_w2_fwd, gate_b_fwd,
           gate_w2_bwd, gate_b_bwd, gla_norm_g, w_out, ln1_g, ln1_b, router_group_w, router_group_b,
           router_expert_w, router_expert_b, expert_w1, expert_w3, expert_w2, ln2_g, ln2_b):
    bsz, t, _ = x.shape
    n_tok = bsz * t
    l = 0
    rows = -(-(bsz + 1) // 8) * 8
    cond = jnp.zeros((rows, D_MODEL), F32).at[:bsz].set(c).at[bsz].set(c_ctx)
    ada = _ada(cond, w_ada[l], b_ada[l][None, :])
    sh1, sc1, g1, sh2, sc2, g2 = [ada[:, i * D_MODEL:(i + 1) * D_MODEL] for i in range(6)]

    w_in_b = w_in[l].astype(BF16)
    lnp_in = jnp.stack([ln_in_g, ln_in_b])
    zero = jnp.zeros((GLA_GATE_RANK, GLA_KEY), F32)
    w2cat = jnp.concatenate([jnp.concatenate([gate_w2_fwd[l], zero], axis=1),
                             jnp.concatenate([zero, gate_w2_bwd[l]], axis=1)], axis=0).astype(BF16)
    gbias = jnp.concatenate([gate_b_fwd[l], gate_b_bwd[l]])[None, :]

    mod_ctx = jnp.broadcast_to(jnp.stack([1.0 + sc1[bsz], sh1[bsz]])[None], (bsz, 2, D_MODEL))
    k_c, v_c, g_c = _proj(ctx, mod_ctx, lnp_in, w_in_b, conv_w[l], conv_b[l][None, :], w2cat, gbias, False)
    zero_state = jnp.zeros((bsz, GLA_KEY, GLA_VAL), F32)
    s_f, s_b = _gla(None, k_c, v_c, g_c, zero_state, zero_state)

    mod1 = jnp.stack([1.0 + sc1[:bsz], sh1[:bsz]], axis=1)
    ya, q, k, v, sr, g = _proj(x, mod1, lnp_in, w_in_b, conv_w[l], conv_b[l][None, :], w2cat, gbias, True)
    o_f, o_b, _, _ = _gla(q, k, v, g, s_f, s_b)

    mod2 = jnp.stack([g1[:bsz], 1.0 + sc2[:bsz], sh2[:bsz]], axis=1)
    lnp1 = jnp.stack([ln_in_g, ln_in_b, ln1_g[l], ln1_b[l]])
    wr = jnp.zeros((D_MODEL, ROUTE_W), F32)
    wr = wr.at[:, :N_GROUPS].set(router_group_w[l]).at[:, N_GROUPS:N_GROUPS + N_EXPERTS].set(router_expert_w[l])
    br = jnp.zeros((1, ROUTE_W), F32)
    br = br.at[0, :N_GROUPS].set(router_group_b[l]).at[0, N_GROUPS:N_GROUPS + N_EXPERTS].set(router_expert_b[l])
    x1, hxs, pos, chunks = _mix_out(x, o_f, o_b, sr, ya, mod2, lnp1, gla_norm_g[l][None, :],
                                    w_out[l].astype(BF16), wr, br)

    src, tile_lo, tile_hi, live = _moe_plan(chunks, n_tok // SORT_TILE)
    w13_b = jnp.concatenate([expert_w1[l], expert_w3[l]], axis=-1).astype(BF16)
    moe = _moe(hxs, src, tile_lo, tile_hi, live, w13_b, expert_w2[l].astype(BF16))

    return _final(x1, moe, pos, g2[:bsz][:, None, :], jnp.stack([ln2_g[l], ln2_b[l]]))
```

```python
import functools

import jax
import jax.numpy as jnp
from jax import lax
from jax.experimental import pallas as pl
from jax.experimental.pallas import tpu as pltpu

F32 = jnp.float32
BF16 = jnp.bfloat16

D_MODEL = 1024
GRID_W = 64
CONV_CH = 512
GLA_HEADS = 4
GLA_DK = 64
GLA_DV = 128
GLA_KEY = GLA_HEADS * GLA_DK
GLA_VAL = GLA_HEADS * GLA_DV
GLA_GATE_RANK = 16
GLA_TAU = 16.0
OFF_AB = 0
OFF_AC = OFF_AB + CONV_CH
OFF_AX = OFF_AC + CONV_CH
OFF_Q = OFF_AX + CONV_CH
OFF_K = OFF_Q + GLA_KEY
OFF_V = OFF_K + GLA_KEY
OFF_R = OFF_V + GLA_VAL
OFF_GF = OFF_R + GLA_VAL
D_PROJ = OFF_GF + 2 * GLA_GATE_RANK
N_GROUPS = 4
EXPERTS_PER_GROUP = 4
N_EXPERTS = N_GROUPS * EXPERTS_PER_GROUP
D_EXPERT = 512
PAIRS_PER_GROUP = 6
N_CLASSES = N_GROUPS * PAIRS_PER_GROUP
LN_EPS = 1e-5
RMS_EPS = 1e-6
DEPTH = 1
DEEPNORM_ALPHA = (2.0 * DEPTH) ** 0.25

LANES = 128
GLA_CHUNK = 64
GLA_SUB = 16
N_SUB = GLA_CHUNK // GLA_SUB
ROUTE_W = LANES
HALF_W = D_MODEL // 2
SLAB_IN_W = HALF_W + ROUTE_W
SLAB_OUT_W = HALF_W
SORT_TILE = 256
MOE_TILE = 256
CHUNK_ROWS = 4
SLAB_ROWS = 2 * CHUNK_ROWS
LOCAL_CHUNKS = -(-(SORT_TILE + N_CLASSES * (CHUNK_ROWS - 1)) // (8 * CHUNK_ROWS)) * 8
LOCAL_SLAB_ROWS = LOCAL_CHUNKS * SLAB_ROWS
CHUNKS_PER_TILE = MOE_TILE // CHUNK_ROWS
VMEM_LIMIT = 56 * 1024 * 1024


def _params(n_axes, vmem=VMEM_LIMIT):
    return pltpu.CompilerParams(dimension_semantics=("arbitrary",) * n_axes, vmem_limit_bytes=vmem)


def _dot(a, b):
    return jnp.dot(a, b, preferred_element_type=F32)


def _div_pow2(x, d):
    assert d & (d - 1) == 0
    return lax.shift_right_logical(x, jnp.int32(d.bit_length() - 1))


def _mod_pow2(x, d):
    assert d & (d - 1) == 0
    return lax.bitwise_and(x, jnp.int32(d - 1))


def _split2(x):
    hi = x.astype(BF16)
    lo = (x - hi.astype(F32)).astype(BF16)
    return hi, lo


def _split3(x):
    hi = x.astype(BF16)
    r = x - hi.astype(F32)
    mid = r.astype(BF16)
    lo = (r - mid.astype(F32)).astype(BF16)
    return hi, mid, lo


def _dot3(a, b):
    ah, al = _split2(a)
    bh, bl = _split2(b)
    return _dot(ah, bh) + _dot(ah, bl) + _dot(al, bh)


def _silu(x):
    return x / (1.0 + jnp.exp(-x))


def _layer_norm(x, g, b):
    mu = jnp.mean(x, axis=-1, keepdims=True)
    xc = x - mu
    var = jnp.mean(xc * xc, axis=-1, keepdims=True)
    return xc * lax.rsqrt(var + LN_EPS) * g + b


def _ada_kernel(c_ref, w_ref, b_ref, o_ref):
    o_ref[...] = _dot3(_silu(c_ref[...]), w_ref[...]) + b_ref[...]


def _ada(cond, w_ada, b_ada):
    rows = cond.shape[0]
    n_out = w_ada.shape[1]
    tn = 1024
    return pl.pallas_call(
        _ada_kernel,
        grid=(n_out // tn,),
        in_specs=[
            pl.BlockSpec((rows, D_MODEL), lambda j: (0, 0)),
            pl.BlockSpec((D_MODEL, tn), lambda j: (0, j)),
            pl.BlockSpec((1, tn), lambda j: (0, j)),
        ],
        out_specs=pl.BlockSpec((rows, tn), lambda j: (0, j)),
        out_shape=jax.ShapeDtypeStruct((rows, n_out), F32),
        compiler_params=_params(1),
        name="ada",
    )(cond, w_ada, b_ada)


def _log_sigmoid(z):
    return jnp.minimum(z, 0.0) - jnp.log(1.0 + jnp.exp(-jnp.abs(z)))


def _proj_kernel(latent, tm, x_ref, mod_ref, lnp_ref, w_ref, cw_ref, cb_ref, w2_ref, gbias_ref, *out_refs):
    x = x_ref[...]
    xn = _layer_norm(x, lnp_ref[0:1, :], lnp_ref[1:2, :])
    h = xn * mod_ref[0:1, :] + mod_ref[1:2, :]
    hb = h.astype(BF16)
    if latent:
        ya_ref, q_ref, k_ref, v_ref, sr_ref, g_ref = out_refs
        p = _dot(hb, w_ref[:, OFF_AB:OFF_Q])
        a_b = p[:, 0:CONV_CH]
        u = p[:, CONV_CH:2 * CONV_CH] * p[:, 2 * CONV_CH:3 * CONV_CH]
        pos = _mod_pow2(lax.broadcasted_iota(jnp.int32, (tm, 1), 0), GRID_W)
        u_prev = jnp.where(pos == 0, 0.0, pltpu.roll(u, 1, 0))
        u_next = jnp.where(pos == GRID_W - 1, 0.0, pltpu.roll(u, tm - 1, 0))
        conv = u_prev * cw_ref[0:1, :] + u * cw_ref[1:2, :] + u_next * cw_ref[2:3, :] + cb_ref[...]
        ya_ref[...] = (a_b * conv).astype(BF16)
        qk = _dot(hb, w_ref[:, OFF_Q:OFF_V])
        q_ref[...] = (qk[:, 0:GLA_KEY] * (GLA_DK ** -0.5)).astype(BF16)
        k_ref[...] = qk[:, GLA_KEY:].astype(BF16)
        r = _dot(hb, w_ref[:, OFF_R:OFF_GF])
        sr_ref[...] = _silu(r).astype(BF16)
    else:
        k_ref, v_ref, g_ref = out_refs
        k_ref[...] = _dot(hb, w_ref[:, OFF_K:OFF_V]).astype(BF16)
    v_ref[...] = _dot(hb, w_ref[:, OFF_V:OFF_R]).astype(BF16)
    low = _dot(hb, w_ref[:, OFF_GF:D_PROJ])
    z = _dot(low.astype(BF16), w2_ref[...]) + gbias_ref[...]
    g_ref[...] = _log_sigmoid(z) * (1.0 / GLA_TAU)


def _proj(x, mod, lnp, w_in_b, conv_w, conv_b, w2cat, gbias, latent):
    bsz, t, _ = x.shape
    tm = min(512, t)
    assert t % tm == 0 and tm % GRID_W == 0
    tok = lambda w: pl.BlockSpec((None, tm, w), lambda b, i: (b, i, 0))
    full = lambda a: pl.BlockSpec(a.shape, lambda b, i: (0,) * a.ndim)
    widths = ([(CONV_CH, BF16), (GLA_KEY, BF16)] if latent else []) + [(GLA_KEY, BF16), (GLA_VAL, BF16)]
    widths += ([(GLA_VAL, BF16)] if latent else []) + [(2 * GLA_KEY, F32)]
    return pl.pallas_call(
        functools.partial(_proj_kernel, latent, tm),
        grid=(bsz, t // tm),
        in_specs=[
            tok(D_MODEL),
            pl.BlockSpec((None, 2, D_MODEL), lambda b, i: (b, 0, 0)),
            full(lnp), full(w_in_b), full(conv_w), full(conv_b), full(w2cat), full(gbias),
        ],
        out_specs=[tok(w) for w, _ in widths],
        out_shape=[jax.ShapeDtypeStruct((bsz, t, w), dt) for w, dt in widths],
        compiler_params=_params(2),
        name="proj_latent" if latent else "proj_ctx",
    )(x, mod, lnp, w_in_b, conv_w, conv_b, w2cat, gbias)


def _tri(n, reverse, strict=False):
    i = lax.broadcasted_iota(jnp.int32, (n, n), 0)
    j = lax.broadcasted_iota(jnp.int32, (n, n), 1)
    if strict:
        m = (j > i) if reverse else (j < i)
    else:
        m = (j >= i) if reverse else (j <= i)
    return jnp.where(m, 1.0, 0.0).astype(BF16)


def _chunk_cumsum(g, reverse):
    tri = _tri(GLA_CHUNK, reverse)
    g1, g2, g3 = _split3(g)
    return _dot(tri, g1) + _dot(tri, g2) + _dot(tri, g3)


def _chunk_total_col(g):
    ones = jnp.ones((GLA_CHUNK, LANES), BF16)
    dn = (((0,), (0,)), ((), ()))
    g1, g2, g3 = _split3(g)
    tot = lambda a: lax.dot_general(a, ones, dn, preferred_element_type=F32)
    return tot(g1) + tot(g2) + tot(g3)


def _sub_anchors(gc, reverse):
    zero = jnp.zeros((1, GLA_KEY), F32)
    if reverse:
        return [gc[GLA_SUB * (a + 1):GLA_SUB * (a + 1) + 1] for a in range(N_SUB - 1)] + [zero]
    return [zero] + [gc[GLA_SUB * a - 1:GLA_SUB * a] for a in range(1, N_SUB)]


def _intra_scores(q, k, gc, reverse):
    r = _sub_anchors(gc, reverse)
    anchor = jnp.concatenate([jnp.broadcast_to(ra, (GLA_SUB, GLA_KEY)) for ra in r], axis=0)
    gcb = gc - anchor
    qt = q * jnp.exp(gcb)
    kt = k * jnp.exp(-gcb)
    pairs = [(a, b) for a in range(N_SUB) for b in range(N_SUB) if (b >= a if reverse else b <= a)]
    rows = []
    for a, b in pairs:
        qa = qt[GLA_SUB * a:GLA_SUB * (a + 1)]
        if a != b:
            qa = qa * jnp.exp(r[a] - r[b])
        rows.append(qa)
    qp = jnp.concatenate(rows, axis=0).astype(BF16)
    width = GLA_HEADS * GLA_CHUNK
    rr = lax.broadcasted_iota(jnp.int32, (width, GLA_KEY), 0)
    cc = lax.broadcasted_iota(jnp.int32, (width, GLA_KEY), 1)
    kbd = jnp.where(_div_pow2(rr, GLA_CHUNK) == _div_pow2(cc, GLA_DK),
                    jnp.concatenate([kt] * GLA_HEADS, axis=0), 0.0)
    res = lax.dot_general(qp, kbd.astype(BF16), (((1,), (1,)), ((), ())), preferred_element_type=F32)
    col = _mod_pow2(lax.broadcasted_iota(jnp.int32, (GLA_SUB, width), 1), GLA_CHUNK)
    col_blk = _div_pow2(col, GLA_SUB)
    col_pos = _mod_pow2(col, GLA_SUB)
    row_pos = lax.broadcasted_iota(jnp.int32, (GLA_SUB, width), 0)
    causal = (col_pos >= row_pos) if reverse else (col_pos <= row_pos)
    blocks = []
    for a in range(N_SUB):
        acc = jnp.zeros((GLA_SUB, width), F32)
        for idx, (pa, pb) in enumerate(pairs):
            if pa != a:
                continue
            keep = col_blk == pb
            if pa == pb:
                keep = keep & causal
            acc = acc + jnp.where(keep, res[GLA_SUB * idx:GLA_SUB * (idx + 1)], 0.0)
        blocks.append(acc)
    return jnp.concatenate(blocks, axis=0)


def _head_mask(rows_per_head, cols_per_head):
    shape = (GLA_HEADS * rows_per_head, GLA_HEADS * cols_per_head)
    rr = lax.broadcasted_iota(jnp.int32, shape, 0)
    cc = lax.broadcasted_iota(jnp.int32, shape, 1)
    return _div_pow2(rr, rows_per_head) == _div_pow2(cc, cols_per_head)


def _state_step(q, k, v_b, g, gc, state, reverse):
    total = gc[0:1] if reverse else gc[GLA_CHUNK - 1:GLA_CHUNK]
    o_inter = None
    if q is not None:
        o_inter = _dot((q * jnp.exp(gc)).astype(BF16), state.astype(BF16))
    k_end = (k * jnp.exp(total - gc)).astype(BF16)
    upd = lax.dot_general(k_end, v_b, (((0,), (0,)), ((), ())), preferred_element_type=F32)
    decay = jnp.exp(_chunk_total_col(g))
    decay = jnp.concatenate([decay] * (GLA_VAL // LANES), axis=1)
    new_state = state * decay + jnp.where(_head_mask(GLA_DK, GLA_DV), upd, 0.0)
    return o_inter, new_state


def _gla_kernel(tt, nt, with_out, *refs):
    if with_out:
        (qf_ref, kf_ref, vf_ref, gfw_ref, qb_ref, kb_ref, vb_ref, gbw_ref, s0f_ref, s0b_ref,
         of_ref, ob_ref, sf_ref, sb_ref) = refs
    else:
        kf_ref, vf_ref, gfw_ref, kb_ref, vb_ref, gbw_ref, s0f_ref, s0b_ref, sf_ref, sb_ref = refs
    j = pl.program_id(1)

    @pl.when(j == 0)
    def _():
        sf_ref[...] = s0f_ref[...]
        sb_ref[...] = s0b_ref[...]

    n_chunks = tt // GLA_CHUNK
    state = sf_ref[...]
    for c in range(n_chunks):
        sl = slice(c * GLA_CHUNK, (c + 1) * GLA_CHUNK)
        k = kf_ref[sl, :].astype(F32)
        v_b = vf_ref[sl, :]
        g = gfw_ref[sl, :]
        g_f = g[:, 0:GLA_KEY]
        gc_f = _chunk_cumsum(g_f, False)
        q = None
        if with_out:
            q = qf_ref[sl, :].astype(F32)
            gc_b = _chunk_cumsum(g[:, GLA_KEY:], True)
            scores = _intra_scores(q, k, gc_f, False) + _intra_scores(q, k, gc_b, True)
            vbd = jnp.where(_head_mask(GLA_CHUNK, GLA_DV), jnp.concatenate([v_b] * GLA_HEADS, axis=0),
                            jnp.zeros((), BF16))
            o_intra = _dot(scores.astype(BF16), vbd)
        o_inter, state = _state_step(q, k, v_b, g_f, gc_f, state, False)
        if with_out:
            of_ref[sl, :] = (o_intra + o_inter).astype(BF16)
    sf_ref[...] = state

    state = sb_ref[...]
    for c in reversed(range(n_chunks)):
        sl = slice(c * GLA_CHUNK, (c + 1) * GLA_CHUNK)
        k = kb_ref[sl, :].astype(F32)
        v_b = vb_ref[sl, :]
        g_b = gbw_ref[sl, :][:, GLA_KEY:]
        gc_b = _chunk_cumsum(g_b, True)
        q = qb_ref[sl, :].astype(F32) if with_out else None
        o_inter, state = _state_step(q, k, v_b, g_b, gc_b, state, True)
        if with_out:
            ob_ref[sl, :] = o_inter.astype(BF16)
    sb_ref[...] = state


def _gla(q, k, v, g, s0f, s0b):
    with_out = q is not None
    bsz, t, _ = k.shape
    tt = min(256, t)
    assert t % tt == 0 and tt % GLA_CHUNK == 0
    nt = t // tt
    fwd = lambda w: pl.BlockSpec((None, tt, w), lambda b, j: (b, j, 0))
    bwd = lambda w: pl.BlockSpec((None, tt, w), lambda b, j: (b, nt - 1 - j, 0))
    st = pl.BlockSpec((None, GLA_KEY, GLA_VAL), lambda b, j: (b, 0, 0))
    st_shape = jax.ShapeDtypeStruct((bsz, GLA_KEY, GLA_VAL), F32)
    if with_out:
        ins = [q, k, v, g, q, k, v, g, s0f, s0b]
        in_specs = [fwd(GLA_KEY), fwd(GLA_KEY), fwd(GLA_VAL), fwd(2 * GLA_KEY),
                    bwd(GLA_KEY), bwd(GLA_KEY), bwd(GLA_VAL), bwd(2 * GLA_KEY), st, st]
        out_specs = [fwd(GLA_VAL), bwd(GLA_VAL), st, st]
        o_shape = jax.ShapeDtypeStruct((bsz, t, GLA_VAL), BF16)
        out_shape = [o_shape, o_shape, st_shape, st_shape]
    else:
        ins = [k, v, g, k, v, g, s0f, s0b]
        in_specs = [fwd(GLA_KEY), fwd(GLA_VAL), fwd(2 * GLA_KEY),
                    bwd(GLA_KEY), bwd(GLA_VAL), bwd(2 * GLA_KEY), st, st]
        out_specs = [st, st]
        out_shape = [st_shape, st_shape]
    return pl.pallas_call(
        functools.partial(_gla_kernel, tt, nt, with_out),
        grid=(bsz, nt),
        in_specs=in_specs,
        out_specs=out_specs,
        out_shape=out_shape,
        compiler_params=_params(2),
        name="gla_latent" if with_out else "gla_ctx",
    )(*ins)


def _row_max(x):
    return jnp.max(x, axis=-1, keepdims=True)


def _row_min(x):
    return jnp.min(x, axis=-1, keepdims=True)


def _row_sum(x):
    return jnp.sum(x, axis=-1, keepdims=True)


def _exact_bf16_parts(x):
    hi = x.astype(BF16).astype(F32)
    r = x - hi
    mid = r.astype(BF16).astype(F32)
    lo = (r - mid).astype(BF16).astype(F32)
    return hi, mid, lo


def _route(logit, tm):
    lane = lax.broadcasted_iota(jnp.int32, (tm, ROUTE_W), 1).astype(F32)
    none = float(ROUTE_W - 1)
    gmask = lane < N_GROUPS
    eg = jnp.where(gmask, jnp.exp(logit - _row_max(jnp.where(gmask, logit, -jnp.inf))), 0.0)
    pg = eg / _row_sum(eg)
    p_g = _row_max(pg)
    g_idx = _row_min(jnp.where(gmask & (pg >= p_g), lane, none))
    first = N_GROUPS + EXPERTS_PER_GROUP * g_idx
    emask = (lane >= first) & (lane < first + EXPERTS_PER_GROUP)
    ee = jnp.where(emask, jnp.exp(logit - _row_max(jnp.where(emask, logit, -jnp.inf))), 0.0)
    pe = jnp.where(emask, ee / _row_sum(ee), -1.0)
    p1 = _row_max(pe)
    i1 = _row_min(jnp.where(emask & (pe >= p1), lane, none))
    rest = emask & (lane != i1)
    pe2 = jnp.where(rest, pe, -1.0)
    p2 = _row_max(pe2)
    i2 = _row_min(jnp.where(rest & (pe2 >= p2), lane, none))
    den = p1 + p2
    w1 = p1 / den * p_g
    w2 = p2 / den * p_g
    l1 = i1 - first
    l2 = i2 - first
    lo = jnp.minimum(l1, l2)
    hi = jnp.maximum(l1, l2)
    pair = lo * (7.0 - lo) * 0.5 + (hi - lo - 1.0)
    cls = g_idx * PAIRS_PER_GROUP + pair
    w_lo = jnp.where(l1 < l2, w1, w2)
    w_hi = jnp.where(l1 < l2, w2, w1)
    onehot = jnp.where(lane == cls, 1.0, 0.0)
    before = _dot(_tri(tm, False, strict=True), onehot.astype(BF16))
    count = jnp.sum(onehot, axis=0, keepdims=True)
    chunks = jnp.floor((count + (CHUNK_ROWS - 1.0)) * (1.0 / CHUNK_ROWS))
    chunks = jnp.where(lane[0:1] == float(N_CLASSES), LOCAL_CHUNKS - _row_sum(chunks), chunks)
    first_chunk = _dot(jnp.broadcast_to(chunks, (8, ROUTE_W)).astype(BF16), _tri(ROUTE_W, True, strict=True))[0:1]
    val = onehot * (CHUNK_ROWS * first_chunk + before)
    pos = _row_sum(val)
    digit_hi = jnp.floor(val * (1.0 / 16.0))
    digit_lo = val - 16.0 * digit_hi
    ones = jnp.ones((8, ROUTE_W), BF16)
    nt = (((1,), (1,)), ((), ()))
    pos_row = (16.0 * lax.dot_general(ones, digit_hi.astype(BF16), nt, preferred_element_type=F32)
               + lax.dot_general(ones, digit_lo.astype(BF16), nt, preferred_element_type=F32))[0:1]
    return pos, pos_row, w_lo, w_hi, chunks


def _slab_sort_matrices(pos, tm, slab_axis):
    shape = (LOCAL_SLAB_ROWS, tm) if slab_axis == 0 else (tm, LOCAL_SLAB_ROWS)
    slab_row = lax.broadcasted_iota(jnp.int32, shape, slab_axis)
    sub = _mod_pow2(slab_row, SLAB_ROWS)
    token_row = CHUNK_ROWS * _div_pow2(slab_row, SLAB_ROWS) + _div_pow2(sub, 2)
    hit = token_row.astype(F32) == pos
    half = _mod_pow2(sub, 2)
    return [jnp.where(hit & (half == h), 1.0, 0.0).astype(BF16) for h in range(2)]


def _mix_out_kernel(tm, x_ref, of_ref, ob_ref, sr_ref, ya_ref, mod_ref, lnp_ref, gn_ref, wo_ref, wr_ref, br_ref,
                    x1_ref, hxs_ref, pos_ref, chunks_ref):
    o = of_ref[...].astype(F32) + ob_ref[...].astype(F32)
    sr = sr_ref[...].astype(F32)
    y = _dot(ya_ref[...], wo_ref[0:CONV_CH, :])
    for h in range(GLA_HEADS):
        sl = slice(h * GLA_DV, (h + 1) * GLA_DV)
        oh = o[:, sl]
        ms = jnp.mean(oh * oh, axis=-1, keepdims=True)
        yb = oh * lax.rsqrt(ms + RMS_EPS) * gn_ref[...] * sr[:, sl]
        y = y + _dot(yb.astype(BF16), wo_ref[CONV_CH + h * GLA_DV:CONV_CH + (h + 1) * GLA_DV, :])
    xn = _layer_norm(x_ref[...], lnp_ref[0:1, :], lnp_ref[1:2, :])
    x1 = _layer_norm(DEEPNORM_ALPHA * xn + mod_ref[0:1, :] * y, lnp_ref[2:3, :], lnp_ref[3:4, :])
    x1_ref[...] = x1
    h2 = x1 * mod_ref[1:2, :] + mod_ref[2:3, :]
    logit = _dot3(h2, wr_ref[...]) + br_ref[...]
    pos, pos_row, w_lo, w_hi, chunks = _route(logit, tm)
    lane = lax.broadcasted_iota(jnp.int32, (tm, ROUTE_W), 1)
    rec = jnp.zeros((tm, ROUTE_W), F32)
    for i, part in enumerate(_exact_bf16_parts(w_lo) + _exact_bf16_parts(w_hi)):
        rec = jnp.where(lane == i, part, rec)
    h2_b = h2.astype(BF16)
    rec_b = rec.astype(BF16)
    sort_lo, sort_hi = _slab_sort_matrices(pos_row, tm, 0)
    pay_lo = jnp.concatenate([h2_b[:, 0:HALF_W], rec_b], axis=1)
    pay_hi = jnp.concatenate([h2_b[:, HALF_W:], jnp.zeros((tm, ROUTE_W), BF16)], axis=1)
    slabs = _dot(sort_lo, pay_lo) + _dot(sort_hi, pay_hi)
    for c in range(SLAB_IN_W // LANES):
        hxs_ref[c] = slabs[:, c * LANES:(c + 1) * LANES]
    pos_ref[...] = jnp.broadcast_to(pos, (tm, ROUTE_W))
    chunks_ref[...] = jnp.broadcast_to(chunks, chunks_ref.shape)


def _mix_out(x, o_f, o_b, sr, ya, mod, lnp, gn, w_out_b, wr, br):
    bsz, t, _ = x.shape
    tm = SORT_TILE
    assert t % tm == 0
    n_t = t // tm
    tok = lambda w: pl.BlockSpec((None, tm, w), lambda b, i: (b, i, 0))
    full = lambda a: pl.BlockSpec(a.shape, lambda b, i: (0,) * a.ndim)
    flat = lambda rows, w: pl.BlockSpec((rows, w), lambda b, i: (b * n_t + i, 0))
    return pl.pallas_call(
        functools.partial(_mix_out_kernel, tm),
        grid=(bsz, n_t),
        in_specs=[
            tok(D_MODEL), tok(GLA_VAL), tok(GLA_VAL), tok(GLA_VAL), tok(CONV_CH),
            pl.BlockSpec((None, 3, D_MODEL), lambda b, i: (b, 0, 0)),
            full(lnp), full(gn), full(w_out_b), full(wr), full(br),
        ],
        out_specs=[tok(D_MODEL),
                   pl.BlockSpec((SLAB_IN_W // LANES, LOCAL_SLAB_ROWS, LANES), lambda b, i: (0, b * n_t + i, 0)),
                   flat(tm, ROUTE_W), flat(8, ROUTE_W)],
        out_shape=[
            jax.ShapeDtypeStruct((bsz, t, D_MODEL), F32),
            jax.ShapeDtypeStruct((SLAB_IN_W // LANES, bsz * n_t * LOCAL_SLAB_ROWS, LANES), F32),
            jax.ShapeDtypeStruct((bsz * t, ROUTE_W), F32),
            jax.ShapeDtypeStruct((bsz * n_t * 8, ROUTE_W), F32),
        ],
        compiler_params=_params(2),
        name="mix_out",
    )(x, o_f, o_b, sr, ya, mod, lnp, gn, w_out_b, wr, br)


def _moe_kernel(n_chunks, nused_ref, lo_ref, hi_ref, live_ref, src_ref,
                hxs_hbm, w13l_ref, w2l_ref, w13h_ref, w2h_ref, out_hbm, gbuf, obuf, gsem, ssem):
    tile_rows = CHUNKS_PER_TILE * SLAB_ROWS
    i = pl.program_id(0)
    n_used = nused_ref[0]
    slot = lax.bitwise_and(i, 1)

    def slab(chunk):
        return pl.ds(pl.multiple_of(chunk * SLAB_ROWS, SLAB_ROWS), SLAB_ROWS)

    def gather_copy(tile, buf_slot, j):
        chunk = jnp.maximum(src_ref[tile * CHUNKS_PER_TILE + j], 0)
        return pltpu.make_async_copy(hxs_hbm.at[:, slab(chunk), :], gbuf.at[buf_slot, :, slab(j), :],
                                     gsem.at[buf_slot])

    def scatter_copy(tile, buf_slot, j):
        chunk = src_ref[tile * CHUNKS_PER_TILE + j]
        chunk = jnp.where(chunk < 0, n_chunks + buf_slot * CHUNKS_PER_TILE + j, chunk)
        return pltpu.make_async_copy(obuf.at[buf_slot, :, slab(j), :], out_hbm.at[:, slab(chunk), :],
                                     ssem.at[buf_slot])

    def start_gather(tile, buf_slot):
        for j in range(CHUNKS_PER_TILE):
            gather_copy(tile, buf_slot, j).start(priority=j % 2)

    def wait_gather(buf_slot):
        pltpu.make_async_copy(hxs_hbm.at[:, pl.ds(0, tile_rows), :], gbuf.at[buf_slot], gsem.at[buf_slot]).wait()

    def wait_scatter(buf_slot):
        pltpu.make_async_copy(obuf.at[buf_slot], out_hbm.at[:, pl.ds(0, tile_rows), :], ssem.at[buf_slot]).wait()

    @pl.when(i == 0)
    def _():
        start_gather(0, 0)
        obuf[...] = jnp.zeros(obuf.shape, F32)
        for s in range(2):
            fill = pltpu.make_async_copy(
                obuf.at[s], out_hbm.at[:, pl.ds((n_chunks + s * CHUNKS_PER_TILE) * SLAB_ROWS, tile_rows), :],
                ssem.at[s])
            fill.start()
            fill.wait()

    @pl.when(i + 1 < n_used)
    def _():
        start_gather(i + 1, 1 - slot)

    @pl.when(i < n_used)
    def _():
        wait_gather(slot)

        @pl.when(i >= 2)
        def _():
            wait_scatter(slot)

        @pl.when(live_ref[i] != 0)
        def _():
            def lane_block(c, half):
                return jnp.concatenate(
                    [gbuf[slot, c, pl.ds(2 * r + half, CHUNKS_PER_TILE, stride=SLAB_ROWS), :]
                     for r in range(CHUNK_ROWS)], axis=0)

            n_blk = HALF_W // LANES
            xb = jnp.concatenate([lane_block(c, 0) for c in range(n_blk)]
                                 + [lane_block(c, 1) for c in range(n_blk)], axis=1).astype(BF16)
            rec = lane_block(n_blk, 0)
            w_lo = rec[:, 0:1] + rec[:, 1:2] + rec[:, 2:3]
            w_hi = rec[:, 3:4] + rec[:, 4:5] + rec[:, 5:6]

            def expert(w13_ref, w2_ref):
                h13 = _dot(xb, w13_ref[...])
                act = _silu(h13[:, 0:D_EXPERT]) * h13[:, D_EXPERT:]
                return _dot(act.astype(BF16), w2_ref[...])

            y = w_lo * expert(w13l_ref, w2l_ref) + w_hi * expert(w13h_ref, w2h_ref)
            y = y.astype(BF16).astype(F32)
            for r in range(CHUNK_ROWS):
                rows = slice(r * CHUNKS_PER_TILE, (r + 1) * CHUNKS_PER_TILE)
                for half in range(2):
                    for c in range(n_blk):
                        col = half * HALF_W + c * LANES
                        obuf[slot, c, pl.ds(2 * r + half, CHUNKS_PER_TILE, stride=SLAB_ROWS), :] = (
                            y[rows, col:col + LANES])

        @pl.when(live_ref[i] == 0)
        def _():
            obuf[slot] = jnp.zeros(obuf.shape[1:], F32)

        for j in range(CHUNKS_PER_TILE):
            scatter_copy(i, slot, j).start(priority=j % 2)

        @pl.when(i == n_used - 1)
        def _():
            wait_scatter(slot)

            @pl.when(i >= 1)
            def _():
                wait_scatter(1 - slot)


def _moe(hxs, src, n_used, tile_lo, tile_hi, tile_live, w13_b, w2_b):
    n_chunks = hxs.shape[1] // SLAB_ROWS
    tile_rows = CHUNKS_PER_TILE * SLAB_ROWS
    n_steps = src.shape[0] // CHUNKS_PER_TILE
    wspec = lambda which, shape: pl.BlockSpec(
        (None,) + shape, (lambda i, nu, lo, hi, lv, s: (lo[i], 0, 0)) if which == 0 else
        (lambda i, nu, lo, hi, lv, s: (hi[i], 0, 0)))
    grid_spec = pltpu.PrefetchScalarGridSpec(
        num_scalar_prefetch=5,
        grid=(n_steps,),
        in_specs=[
            pl.BlockSpec(memory_space=pl.ANY),
            wspec(0, (D_MODEL, 2 * D_EXPERT)), wspec(0, (D_EXPERT, D_MODEL)),
            wspec(1, (D_MODEL, 2 * D_EXPERT)), wspec(1, (D_EXPERT, D_MODEL)),
        ],
        out_specs=pl.BlockSpec(memory_space=pl.ANY),
        scratch_shapes=[
            pltpu.VMEM((2, SLAB_IN_W // LANES, tile_rows, LANES), F32),
            pltpu.VMEM((2, SLAB_OUT_W // LANES, tile_rows, LANES), F32),
            pltpu.SemaphoreType.DMA((2,)),
            pltpu.SemaphoreType.DMA((2,)),
        ],
    )
    return pl.pallas_call(
        functools.partial(_moe_kernel, n_chunks),
        grid_spec=grid_spec,
        out_shape=jax.ShapeDtypeStruct((SLAB_OUT_W // LANES, (n_chunks + 2 * CHUNKS_PER_TILE) * SLAB_ROWS, LANES), F32),
        compiler_params=_params(1),
        name="moe",
    )(n_used, tile_lo, tile_hi, tile_live, src, hxs, w13_b, w2_b, w13_b, w2_b)


def _final_kernel(tm, x1_ref, moe_ref, pos_ref, mod_ref, lnp_ref, o_ref):
    moe_b = jnp.concatenate([moe_ref[c] for c in range(SLAB_OUT_W // LANES)], axis=1).astype(BF16)
    moe = jnp.concatenate([_dot(sort_t, moe_b) for sort_t in _slab_sort_matrices(pos_ref[:, 0:1], tm, 1)], axis=1)
    o_ref[...] = _layer_norm(DEEPNORM_ALPHA * x1_ref[...] + mod_ref[...] * moe, lnp_ref[0:1, :], lnp_ref[1:2, :])


def _final(x1, moe, pos, g2, lnp):
    bsz, t, _ = x1.shape
    tm = SORT_TILE
    n_t = t // tm
    flat = lambda rows, w: pl.BlockSpec((rows, w), lambda b, i: (b * n_t + i, 0))
    return pl.pallas_call(
        functools.partial(_final_kernel, tm),
        grid=(bsz, n_t),
        in_specs=[
            pl.BlockSpec((None, tm, D_MODEL), lambda b, i: (b, i, 0)),
            pl.BlockSpec((SLAB_OUT_W // LANES, LOCAL_SLAB_ROWS, LANES), lambda b, i: (0, b * n_t + i, 0)),
            flat(tm, ROUTE_W),
            pl.BlockSpec((None, 1, D_MODEL), lambda b, i: (b, 0, 0)),
            pl.BlockSpec(lnp.shape, lambda b, i: (0, 0)),
        ],
        out_specs=pl.BlockSpec((None, tm, D_MODEL), lambda b, i: (b, i, 0)),
        out_shape=jax.ShapeDtypeStruct((bsz, t, D_MODEL), F32),
        compiler_params=_params(2),
        name="final",
    )(x1, moe, pos, g2, lnp)


def _pair_tables():
    lo, hi = [], []
    for g in range(N_GROUPS):
        for a in range(EXPERTS_PER_GROUP):
            for b in range(a + 1, EXPERTS_PER_GROUP):
                lo.append(g * EXPERTS_PER_GROUP + a)
                hi.append(g * EXPERTS_PER_GROUP + b)
    return jnp.array(lo, jnp.int32), jnp.array(hi, jnp.int32)


def _moe_plan(chunks, n_sort_tiles):
    n_cls = N_CLASSES + 1
    hp = lax.Precision.HIGHEST
    m = chunks.reshape(n_sort_tiles, 8, ROUTE_W)[:, 0, :n_cls].astype(jnp.int32)
    a_end = jnp.cumsum(m, axis=0)
    a_start = a_end - m
    per_cls = a_end[-1]
    padded = (per_cls + CHUNKS_PER_TILE - 1) // CHUNKS_PER_TILE * CHUNKS_PER_TILE
    g_end = jnp.cumsum(padded)
    g_start = g_end - padded
    local_off = jnp.cumsum(m, axis=1) - m
    seg = jnp.arange(n_sort_tiles, dtype=jnp.int32)[:, None] * LOCAL_CHUNKS + local_off - a_start
    n_steps = -(-(n_sort_tiles * LOCAL_CHUNKS) // CHUNKS_PER_TILE) + n_cls
    p = jnp.arange(n_steps * CHUNKS_PER_TILE, dtype=jnp.int32)
    cls_p = jnp.minimum(jnp.sum((g_end[None, :] <= p[:, None]).astype(jnp.int32), axis=1), n_cls - 1)
    onehot = (cls_p[:, None] == jnp.arange(n_cls, dtype=jnp.int32)[None, :]).astype(F32)
    pick = lambda tab: jnp.dot(onehot, tab.astype(F32), precision=hp)
    u = p - pick(g_start[:, None])[:, 0].astype(jnp.int32)
    valid = u < pick(per_cls[:, None])[:, 0].astype(jnp.int32)
    a_end_p = pick(a_end.T).astype(jnp.int32)
    seg_p = pick(seg.T).astype(jnp.int32)
    tile_p = jnp.sum((a_end_p <= u[:, None]).astype(jnp.int32), axis=1)
    hit = jnp.arange(n_sort_tiles, dtype=jnp.int32)[None, :] == tile_p[:, None]
    src = jnp.sum(jnp.where(hit, seg_p, 0), axis=1) + u
    src = jnp.where(valid, src, -1).astype(jnp.int32)
    n_used = g_end[-1:] // CHUNKS_PER_TILE
    step = jnp.arange(n_steps, dtype=jnp.int32)
    tile_cls = jnp.sum((g_end[None, :] // CHUNKS_PER_TILE <= step[:, None]).astype(jnp.int32), axis=1)
    live = ((tile_cls < N_CLASSES) & (step < n_used[0])).astype(jnp.int32)
    pair_lo, pair_hi = _pair_tables()
    pair_oh = (jnp.minimum(tile_cls, N_CLASSES - 1)[:, None] == jnp.arange(N_CLASSES)[None, :]).astype(jnp.int32)
    tile_lo = jnp.sum(pair_oh * pair_lo[None, :], axis=1).astype(jnp.int32)
    tile_hi = jnp.sum(pair_oh * pair_hi[None, :], axis=1).astype(jnp.int32)
    return src, n_used.astype(jnp.int32), tile_lo, tile_hi, live


def kernel(x, c, ctx, c_ctx, ln_in_g, ln_in_b, w_ada, b_ada, w_in, conv_w, conv_b, gate_w2_fwd, gate_b_fwd,
           gate_w2_bwd, gate_b_bwd, gla_norm_g, w_out, ln1_g, ln1_b, router_group_w, router_group_b,
           router_expert_w, router_expert_b, expert_w1, expert_w3, expert_w2, ln2_g, ln2_b):
    bsz, t, _ = x.shape
    n_tok = bsz * t
    l = 0
    rows = -(-(bsz + 1) // 8) * 8
    cond = jnp.zeros((rows, D_MODEL), F32).at[:bsz].set(c).at[bsz].set(c_ctx)
    ada = _ada(cond, w_ada[l], b_ada[l][None, :])
    sh1, sc1, g1, sh2, sc2, g2 = [ada[:, i * D_MODEL:(i + 1) * D_MODEL] for i in range(6)]

    w_in_b = w_in[l].astype(BF16)
    lnp_in = jnp.stack([ln_in_g, ln_in_b])
    zero = jnp.zeros((GLA_GATE_RANK, GLA_KEY), F32)
    w2cat = jnp.concatenate([jnp.concatenate([gate_w2_fwd[l], zero], axis=1),
                             jnp.concatenate([zero, gate_w2_bwd[l]], axis=1)], axis=0).astype(BF16)
    gbias = jnp.concatenate([gate_b_fwd[l], gate_b_bwd[l]])[None, :]

    mod_ctx = jnp.broadcast_to(jnp.stack([1.0 + sc1[bsz], sh1[bsz]])[None], (bsz, 2, D_MODEL))
    k_c, v_c, g_c = _proj(ctx, mod_ctx, lnp_in, w_in_b, conv_w[l], conv_b[l][None, :], w2cat, gbias, False)
    zero_state = jnp.zeros((bsz, GLA_KEY, GLA_VAL), F32)
    s_f, s_b = _gla(None, k_c, v_c, g_c, zero_state, zero_state)

    mod1 = jnp.stack([1.0 + sc1[:bsz], sh1[:bsz]], axis=1)
    ya, q, k, v, sr, g = _proj(x, mod1, lnp_in, w_in_b, conv_w[l], conv_b[l][None, :], w2cat, gbias, True)
    o_f, o_b, _, _ = _gla(q, k, v, g, s_f, s_b)

    mod2 = jnp.stack([g1[:bsz], 1.0 + sc2[:bsz], sh2[:bsz]], axis=1)
    lnp1 = jnp.stack([ln_in_g, ln_in_b, ln1_g[l], ln1_b[l]])
    wr = jnp.zeros((D_MODEL, ROUTE_W), F32)
    wr = wr.at[:, :N_GROUPS].set(router_group_w[l]).at[:, N_GROUPS:N_GROUPS + N_EXPERTS].set(router_expert_w[l])
    br = jnp.zeros((1, ROUTE_W), F32)
    br = br.at[0, :N_GROUPS].set(router_group_b[l]).at[0, N_GROUPS:N_GROUPS + N_EXPERTS].set(router_expert_b[l])
    x1, hxs, pos, chunks = _mix_out(x, o_f, o_b, sr, ya, mod2, lnp1, gla_norm_g[l][None, :],
                                    w_out[l].astype(BF16), wr, br)

    src, n_used, tile_lo, tile_hi, live = _moe_plan(chunks, n_tok // SORT_TILE)
    w13_b = jnp.concatenate([expert_w1[l], expert_w3[l]], axis=-1).astype(BF16)
    moe = _moe(hxs, src, n_used, tile_lo, tile_hi, live, w13_b, expert_w2[l].astype(BF16))

    return _final(x1, moe, pos, g2[:bsz][:, None, :], jnp.stack([ln2_g[l], ln2_b[l]]))
```

```python
import functools

import jax
import jax.numpy as jnp
from jax import lax
from jax.experimental import pallas as pl
from jax.experimental.pallas import tpu as pltpu

F32 = jnp.float32
BF16 = jnp.bfloat16

D_MODEL = 1024
GRID_W = 64
CONV_CH = 512
GLA_HEADS = 4
GLA_DK = 64
GLA_DV = 128
GLA_KEY = GLA_HEADS * GLA_DK
GLA_VAL = GLA_HEADS * GLA_DV
PAIR_KEY = 2 * GLA_DK
PAIR_VAL = 2 * GLA_DV
GLA_GATE_RANK = 16
GLA_TAU = 16.0
OFF_AB = 0
OFF_AC = OFF_AB + CONV_CH
OFF_AX = OFF_AC + CONV_CH
OFF_Q = OFF_AX + CONV_CH
OFF_K = OFF_Q + GLA_KEY
OFF_V = OFF_K + GLA_KEY
OFF_R = OFF_V + GLA_VAL
OFF_GF = OFF_R + GLA_VAL
D_PROJ = OFF_GF + 2 * GLA_GATE_RANK
N_GROUPS = 4
EXPERTS_PER_GROUP = 4
N_EXPERTS = N_GROUPS * EXPERTS_PER_GROUP
D_EXPERT = 512
PAIRS_PER_GROUP = 6
N_CLASSES = N_GROUPS * PAIRS_PER_GROUP
LN_EPS = 1e-5
RMS_EPS = 1e-6
DEPTH = 1
DEEPNORM_ALPHA = (2.0 * DEPTH) ** 0.25

LANES = 128
GLA_CHUNK = 64
GLA_SUB = 16
N_SUB = GLA_CHUNK // GLA_SUB
ROUTE_W = LANES
HALF_W = D_MODEL // 2
SLAB_IN_W = HALF_W + ROUTE_W
SLAB_OUT_W = HALF_W
SORT_TILE = 256
MOE_TILE = 256
CHUNK_ROWS = 4
SLAB_ROWS = 2 * CHUNK_ROWS
LOCAL_CHUNKS = -(-(SORT_TILE + N_CLASSES * (CHUNK_ROWS - 1)) // (8 * CHUNK_ROWS)) * 8
LOCAL_SLAB_ROWS = LOCAL_CHUNKS * SLAB_ROWS
CHUNKS_PER_TILE = MOE_TILE // CHUNK_ROWS
VMEM_LIMIT = 56 * 1024 * 1024


def _params(n_axes, vmem=VMEM_LIMIT):
    return pltpu.CompilerParams(dimension_semantics=("arbitrary",) * n_axes, vmem_limit_bytes=vmem)


def _dot(a, b):
    return jnp.dot(a, b, preferred_element_type=F32)


def _div_pow2(x, d):
    assert d & (d - 1) == 0
    return lax.shift_right_logical(x, jnp.int32(d.bit_length() - 1))


def _mod_pow2(x, d):
    assert d & (d - 1) == 0
    return lax.bitwise_and(x, jnp.int32(d - 1))


def _split2(x):
    hi = x.astype(BF16)
    lo = (x - hi.astype(F32)).astype(BF16)
    return hi, lo


def _dot3(a, b):
    ah, al = _split2(a)
    bh, bl = _split2(b)
    return _dot(ah, bh) + _dot(ah, bl) + _dot(al, bh)


def _silu(x):
    return x * (0.5 * jnp.tanh(0.5 * x) + 0.5)


def _layer_norm(x, g, b):
    mu = jnp.mean(x, axis=-1, keepdims=True)
    xc = x - mu
    var = jnp.mean(xc * xc, axis=-1, keepdims=True)
    return xc * lax.rsqrt(var + LN_EPS) * g + b


def _ada_kernel(c_ref, w_ref, b_ref, o_ref):
    o_ref[...] = _dot3(_silu(c_ref[...]), w_ref[...]) + b_ref[...]


def _ada(cond, w_ada, b_ada):
    rows = cond.shape[0]
    n_out = w_ada.shape[1]
    tn = 1024
    return pl.pallas_call(
        _ada_kernel,
        grid=(n_out // tn,),
        in_specs=[
            pl.BlockSpec((rows, D_MODEL), lambda j: (0, 0)),
            pl.BlockSpec((D_MODEL, tn), lambda j: (0, j)),
            pl.BlockSpec((1, tn), lambda j: (0, j)),
        ],
        out_specs=pl.BlockSpec((rows, tn), lambda j: (0, j)),
        out_shape=jax.ShapeDtypeStruct((rows, n_out), F32),
        compiler_params=_params(1),
        name="ada",
    )(cond, w_ada, b_ada)


def _log_sigmoid(z):
    return jnp.minimum(z, 0.0) - jnp.log(1.0 + jnp.exp(-jnp.abs(z)))


def _proj_kernel(latent, tm, x_ref, mod_ref, lnp_ref, w_ref, cw_ref, cb_ref, w2_ref, gbias_ref, *out_refs):
    x = x_ref[...]
    xn = _layer_norm(x, lnp_ref[0:1, :], lnp_ref[1:2, :])
    h = xn * mod_ref[0:1, :] + mod_ref[1:2, :]
    hb = h.astype(BF16)
    if latent:
        ya_ref, q_ref, k_ref, v_ref, sr_ref, g_ref = out_refs
        p = _dot(hb, w_ref[:, OFF_AB:OFF_Q])
        a_b = p[:, 0:CONV_CH]
        u = p[:, CONV_CH:2 * CONV_CH] * p[:, 2 * CONV_CH:3 * CONV_CH]
        pos = _mod_pow2(lax.broadcasted_iota(jnp.int32, (tm, 1), 0), GRID_W)
        u_prev = jnp.where(pos == 0, 0.0, pltpu.roll(u, 1, 0))
        u_next = jnp.where(pos == GRID_W - 1, 0.0, pltpu.roll(u, tm - 1, 0))
        conv = u_prev * cw_ref[0:1, :] + u * cw_ref[1:2, :] + u_next * cw_ref[2:3, :] + cb_ref[...]
        ya_ref[...] = (a_b * conv).astype(BF16)
        qk = _dot(hb, w_ref[:, OFF_Q:OFF_V])
        q_ref[...] = (qk[:, 0:GLA_KEY] * (GLA_DK ** -0.5)).astype(BF16)
        k_ref[...] = qk[:, GLA_KEY:].astype(BF16)
        r = _dot(hb, w_ref[:, OFF_R:OFF_GF])
        sr_ref[...] = _silu(r).astype(BF16)
    else:
        k_ref, v_ref, g_ref = out_refs
        k_ref[...] = _dot(hb, w_ref[:, OFF_K:OFF_V]).astype(BF16)
    v_ref[...] = _dot(hb, w_ref[:, OFF_V:OFF_R]).astype(BF16)
    low = _dot(hb, w_ref[:, OFF_GF:D_PROJ])
    z = _dot(low.astype(BF16), w2_ref[...]) + gbias_ref[...]
    g_ref[...] = _log_sigmoid(z) * (1.0 / GLA_TAU)


def _proj(x, mod, lnp, w_in_b, conv_w, conv_b, w2cat, gbias, latent):
    bsz, t, _ = x.shape
    tm = min(512, t)
    assert t % tm == 0 and tm % GRID_W == 0
    tok = lambda w: pl.BlockSpec((None, tm, w), lambda b, i: (b, i, 0))
    full = lambda a: pl.BlockSpec(a.shape, lambda b, i: (0,) * a.ndim)
    widths = ([(CONV_CH, BF16), (GLA_KEY, BF16)] if latent else []) + [(GLA_KEY, BF16), (GLA_VAL, BF16)]
    widths += ([(GLA_VAL, BF16)] if latent else []) + [(2 * GLA_KEY, F32)]
    return pl.pallas_call(
        functools.partial(_proj_kernel, latent, tm),
        grid=(bsz, t // tm),
        in_specs=[
            tok(D_MODEL),
            pl.BlockSpec((None, 2, D_MODEL), lambda b, i: (b, 0, 0)),
            full(lnp), full(w_in_b), full(conv_w), full(conv_b), full(w2cat), full(gbias),
        ],
        out_specs=[tok(w) for w, _ in widths],
        out_shape=[jax.ShapeDtypeStruct((bsz, t, w), dt) for w, dt in widths],
        compiler_params=_params(2),
        name="proj_latent" if latent else "proj_ctx",
    )(x, mod, lnp, w_in_b, conv_w, conv_b, w2cat, gbias)


def _tri(n, reverse, strict=False):
    i = lax.broadcasted_iota(jnp.int32, (n, n), 0)
    j = lax.broadcasted_iota(jnp.int32, (n, n), 1)
    if strict:
        m = (j > i) if reverse else (j < i)
    else:
        m = (j >= i) if reverse else (j <= i)
    return jnp.where(m, 1.0, 0.0).astype(BF16)


def _chunk_cumsum(g, reverse):
    tri = _tri(GLA_CHUNK, reverse)
    g_hi, g_lo = _split2(g)
    return _dot(tri, g_hi) + _dot(tri, g_lo)


def _as_column(row):
    return jnp.broadcast_to(row, (LANES, row.shape[1])).T


def _sub_anchors(gc, reverse):
    zero = jnp.zeros((1, GLA_KEY), F32)
    if reverse:
        return [gc[GLA_SUB * (a + 1):GLA_SUB * (a + 1) + 1] for a in range(N_SUB - 1)] + [zero]
    return [zero] + [gc[GLA_SUB * a - 1:GLA_SUB * a] for a in range(1, N_SUB)]


def _intra_scores(q, k, gc, reverse):
    r = _sub_anchors(gc, reverse)
    anchor = jnp.concatenate([jnp.broadcast_to(ra, (GLA_SUB, GLA_KEY)) for ra in r], axis=0)
    gcb = gc - anchor
    qt = q * jnp.exp(gcb)
    kt = k * jnp.exp(-gcb)
    pairs = [(a, b) for a in range(N_SUB) for b in range(N_SUB) if (b >= a if reverse else b <= a)]
    rows = []
    for a, b in pairs:
        qa = qt[GLA_SUB * a:GLA_SUB * (a + 1)]
        if a != b:
            qa = qa * jnp.exp(r[a] - r[b])
        rows.append(qa)
    qp = jnp.concatenate(rows, axis=0).astype(BF16)
    width = GLA_HEADS * GLA_CHUNK
    rr = lax.broadcasted_iota(jnp.int32, (width, GLA_KEY), 0)
    cc = lax.broadcasted_iota(jnp.int32, (width, GLA_KEY), 1)
    kbd = jnp.where(_div_pow2(rr, GLA_CHUNK) == _div_pow2(cc, GLA_DK),
                    jnp.concatenate([kt] * GLA_HEADS, axis=0), 0.0)
    res = lax.dot_general(qp, kbd.astype(BF16), (((1,), (1,)), ((), ())), preferred_element_type=F32)
    col = _mod_pow2(lax.broadcasted_iota(jnp.int32, (GLA_SUB, width), 1), GLA_CHUNK)
    col_blk = _div_pow2(col, GLA_SUB)
    col_pos = _mod_pow2(col, GLA_SUB)
    row_pos = lax.broadcasted_iota(jnp.int32, (GLA_SUB, width), 0)
    causal = (col_pos >= row_pos) if reverse else (col_pos <= row_pos)
    blocks = []
    for a in range(N_SUB):
        acc = jnp.zeros((GLA_SUB, width), F32)
        for idx, (pa, pb) in enumerate(pairs):
            if pa != a:
                continue
            keep = col_blk == pb
            if pa == pb:
                keep = keep & causal
            acc = acc + jnp.where(keep, res[GLA_SUB * idx:GLA_SUB * (idx + 1)], 0.0)
        blocks.append(acc)
    return jnp.concatenate(blocks, axis=0)


def _pair_mask(rows_per_head, cols_per_head, n_row_pairs=1):
    shape = (n_row_pairs * 2 * rows_per_head, 2 * cols_per_head)
    rr = _mod_pow2(lax.broadcasted_iota(jnp.int32, shape, 0), 2 * rows_per_head)
    cc = lax.broadcasted_iota(jnp.int32, shape, 1)
    return _div_pow2(rr, rows_per_head) == _div_pow2(cc, cols_per_head)


def _state_step(q, k, v_b, gc, state, reverse):
    total = gc[0:1] if reverse else gc[GLA_CHUNK - 1:GLA_CHUNK]
    q_dec = None if q is None else (q * jnp.exp(gc)).astype(BF16)
    k_end = (k * jnp.exp(total - gc)).astype(BF16)
    state_b = state.astype(BF16)
    tn = (((0,), (0,)), ((), ()))
    o_inter, upd = [], []
    for p in range(GLA_HEADS // 2):
        ks = slice(p * PAIR_KEY, (p + 1) * PAIR_KEY)
        if q is not None:
            o_inter.append(_dot(q_dec[:, ks], state_b[ks, :]))
        upd.append(lax.dot_general(k_end[:, ks], v_b[:, p * PAIR_VAL:(p + 1) * PAIR_VAL], tn,
                                   preferred_element_type=F32))
    decay = jnp.exp(_as_column(total))
    decay = jnp.concatenate([decay] * (PAIR_VAL // LANES), axis=1)
    new_state = state * decay + jnp.where(_pair_mask(GLA_DK, GLA_DV, GLA_HEADS // 2),
                                          jnp.concatenate(upd, axis=0), 0.0)
    return (jnp.concatenate(o_inter, axis=1) if q is not None else None), new_state


def _gla_kernel(tt, nt, with_out, *refs):
    if with_out:
        (qf_ref, kf_ref, vf_ref, gfw_ref, qb_ref, kb_ref, vb_ref, gbw_ref, s0f_ref, s0b_ref,
         of_ref, ob_ref, sf_ref, sb_ref) = refs
    else:
        kf_ref, vf_ref, gfw_ref, kb_ref, vb_ref, gbw_ref, s0f_ref, s0b_ref, sf_ref, sb_ref = refs
    j = pl.program_id(1)

    @pl.when(j == 0)
    def _():
        sf_ref[...] = s0f_ref[...]
        sb_ref[...] = s0b_ref[...]

    n_chunks = tt // GLA_CHUNK
    state = sf_ref[...]
    for c in range(n_chunks):
        sl = slice(c * GLA_CHUNK, (c + 1) * GLA_CHUNK)
        k = kf_ref[sl, :].astype(F32)
        v_b = vf_ref[sl, :]
        g = gfw_ref[sl, :]
        g_f = g[:, 0:GLA_KEY]
        gc_f = _chunk_cumsum(g_f, False)
        q = None
        if with_out:
            q = qf_ref[sl, :].astype(F32)
            gc_b = _chunk_cumsum(g[:, GLA_KEY:], True)
            scores = (_intra_scores(q, k, gc_f, False) + _intra_scores(q, k, gc_b, True)).astype(BF16)
            o_intra = []
            for p in range(GLA_HEADS // 2):
                v_p = v_b[:, p * PAIR_VAL:(p + 1) * PAIR_VAL]
                vbd = jnp.where(_pair_mask(GLA_CHUNK, GLA_DV), jnp.concatenate([v_p, v_p], axis=0),
                                jnp.zeros((), BF16))
                o_intra.append(_dot(scores[:, p * 2 * GLA_CHUNK:(p + 1) * 2 * GLA_CHUNK], vbd))
            o_intra = jnp.concatenate(o_intra, axis=1)
        o_inter, state = _state_step(q, k, v_b, gc_f, state, False)
        if with_out:
            of_ref[sl, :] = (o_intra + o_inter).astype(BF16)
    sf_ref[...] = state

    state = sb_ref[...]
    for c in reversed(range(n_chunks)):
        sl = slice(c * GLA_CHUNK, (c + 1) * GLA_CHUNK)
        k = kb_ref[sl, :].astype(F32)
        v_b = vb_ref[sl, :]
        g_b = gbw_ref[sl, :][:, GLA_KEY:]
        gc_b = _chunk_cumsum(g_b, True)
        q = qb_ref[sl, :].astype(F32) if with_out else None
        o_inter, state = _state_step(q, k, v_b, gc_b, state, True)
        if with_out:
            ob_ref[sl, :] = o_inter.astype(BF16)
    sb_ref[...] = state


def _gla(q, k, v, g, s0f, s0b):
    with_out = q is not None
    bsz, t, _ = k.shape
    tt = min(256, t)
    assert t % tt == 0 and tt % GLA_CHUNK == 0
    nt = t // tt
    fwd = lambda w: pl.BlockSpec((None, tt, w), lambda b, j: (b, j, 0))
    bwd = lambda w: pl.BlockSpec((None, tt, w), lambda b, j: (b, nt - 1 - j, 0))
    st = pl.BlockSpec((None, GLA_KEY, PAIR_VAL), lambda b, j: (b, 0, 0))
    st_shape = jax.ShapeDtypeStruct((bsz, GLA_KEY, PAIR_VAL), F32)
    if with_out:
        ins = [q, k, v, g, q, k, v, g, s0f, s0b]
        in_specs = [fwd(GLA_KEY), fwd(GLA_KEY), fwd(GLA_VAL), fwd(2 * GLA_KEY),
                    bwd(GLA_KEY), bwd(GLA_KEY), bwd(GLA_VAL), bwd(2 * GLA_KEY), st, st]
        out_specs = [fwd(GLA_VAL), bwd(GLA_VAL), st, st]
        o_shape = jax.ShapeDtypeStruct((bsz, t, GLA_VAL), BF16)
        out_shape = [o_shape, o_shape, st_shape, st_shape]
    else:
        ins = [k, v, g, k, v, g, s0f, s0b]
        in_specs = [fwd(GLA_KEY), fwd(GLA_VAL), fwd(2 * GLA_KEY),
                    bwd(GLA_KEY), bwd(GLA_VAL), bwd(2 * GLA_KEY), st, st]
        out_specs = [st, st]
        out_shape = [st_shape, st_shape]
    return pl.pallas_call(
        functools.partial(_gla_kernel, tt, nt, with_out),
        grid=(bsz, nt),
        in_specs=in_specs,
        out_specs=out_specs,
        out_shape=out_shape,
        compiler_params=_params(2),
        name="gla_latent" if with_out else "gla_ctx",
    )(*ins)


def _row_max(x):
    return jnp.max(x, axis=-1, keepdims=True)


def _row_min(x):
    return jnp.min(x, axis=-1, keepdims=True)


def _row_sum(x):
    return jnp.sum(x, axis=-1, keepdims=True)


def _exact_bf16_parts(x):
    hi = x.astype(BF16).astype(F32)
    r = x - hi
    mid = r.astype(BF16).astype(F32)
    lo = (r - mid).astype(BF16).astype(F32)
    return hi, mid, lo


def _route(logit, tm):
    lane = lax.broadcasted_iota(jnp.int32, (tm, ROUTE_W), 1).astype(F32)
    none = float(ROUTE_W - 1)
    gmask = lane < N_GROUPS
    eg = jnp.where(gmask, jnp.exp(logit - _row_max(jnp.where(gmask, logit, -jnp.inf))), 0.0)
    pg = eg / _row_sum(eg)
    p_g = _row_max(pg)
    g_idx = _row_min(jnp.where(gmask & (pg >= p_g), lane, none))
    first = N_GROUPS + EXPERTS_PER_GROUP * g_idx
    emask = (lane >= first) & (lane < first + EXPERTS_PER_GROUP)
    ee = jnp.where(emask, jnp.exp(logit - _row_max(jnp.where(emask, logit, -jnp.inf))), 0.0)
    pe = jnp.where(emask, ee / _row_sum(ee), -1.0)
    p1 = _row_max(pe)
    i1 = _row_min(jnp.where(emask & (pe >= p1), lane, none))
    rest = emask & (lane != i1)
    pe2 = jnp.where(rest, pe, -1.0)
    p2 = _row_max(pe2)
    i2 = _row_min(jnp.where(rest & (pe2 >= p2), lane, none))
    den = p1 + p2
    w1 = p1 / den * p_g
    w2 = p2 / den * p_g
    l1 = i1 - first
    l2 = i2 - first
    lo = jnp.minimum(l1, l2)
    hi = jnp.maximum(l1, l2)
    pair = lo * (7.0 - lo) * 0.5 + (hi - lo - 1.0)
    cls = g_idx * PAIRS_PER_GROUP + pair
    w_lo = jnp.where(l1 < l2, w1, w2)
    w_hi = jnp.where(l1 < l2, w2, w1)
    onehot = jnp.where(lane == cls, 1.0, 0.0)
    before = _dot(_tri(tm, False, strict=True), onehot.astype(BF16))
    count = jnp.sum(onehot, axis=0, keepdims=True)
    chunks = jnp.floor((count + (CHUNK_ROWS - 1.0)) * (1.0 / CHUNK_ROWS))
    chunks = jnp.where(lane[0:1] == float(N_CLASSES), LOCAL_CHUNKS - _row_sum(chunks), chunks)
    first_chunk = _dot(jnp.broadcast_to(chunks, (8, ROUTE_W)).astype(BF16), _tri(ROUTE_W, True, strict=True))[0:1]
    val = onehot * (CHUNK_ROWS * first_chunk + before)
    pos = _row_sum(val)
    digit_hi = jnp.floor(val * (1.0 / 16.0))
    digit_lo = val - 16.0 * digit_hi
    ones = jnp.ones((8, ROUTE_W), BF16)
    nt = (((1,), (1,)), ((), ()))
    pos_row = (16.0 * lax.dot_general(ones, digit_hi.astype(BF16), nt, preferred_element_type=F32)
               + lax.dot_general(ones, digit_lo.astype(BF16), nt, preferred_element_type=F32))[0:1]
    return pos, pos_row, w_lo, w_hi, chunks


def _slab_sort_matrices(pos, tm, slab_axis):
    shape = (LOCAL_SLAB_ROWS, tm) if slab_axis == 0 else (tm, LOCAL_SLAB_ROWS)
    slab_row = lax.broadcasted_iota(jnp.int32, shape, slab_axis)
    sub = _mod_pow2(slab_row, SLAB_ROWS)
    token_row = CHUNK_ROWS * _div_pow2(slab_row, SLAB_ROWS) + _div_pow2(sub, 2)
    hit = token_row.astype(F32) == pos
    half = _mod_pow2(sub, 2)
    return [jnp.where(hit & (half == h), 1.0, 0.0).astype(BF16) for h in range(2)]


def _mix_out_kernel(tm, x_ref, of_ref, ob_ref, sr_ref, ya_ref, mod_ref, lnp_ref, gn_ref, wo_ref, wr_ref, br_ref,
                    x1_ref, hxs_ref, pos_ref, chunks_ref):
    o = of_ref[...].astype(F32) + ob_ref[...].astype(F32)
    sr = sr_ref[...].astype(F32)
    y = _dot(ya_ref[...], wo_ref[0:CONV_CH, :])
    for h in range(GLA_HEADS):
        sl = slice(h * GLA_DV, (h + 1) * GLA_DV)
        oh = o[:, sl]
        ms = jnp.mean(oh * oh, axis=-1, keepdims=True)
        yb = oh * lax.rsqrt(ms + RMS_EPS) * gn_ref[...] * sr[:, sl]
        y = y + _dot(yb.astype(BF16), wo_ref[CONV_CH + h * GLA_DV:CONV_CH + (h + 1) * GLA_DV, :])
    xn = _layer_norm(x_ref[...], lnp_ref[0:1, :], lnp_ref[1:2, :])
    x1 = _layer_norm(DEEPNORM_ALPHA * xn + mod_ref[0:1, :] * y, lnp_ref[2:3, :], lnp_ref[3:4, :])
    x1_ref[...] = x1
    h2 = x1 * mod_ref[1:2, :] + mod_ref[2:3, :]
    logit = _dot3(h2, wr_ref[...]) + br_ref[...]
    pos, pos_row, w_lo, w_hi, chunks = _route(logit, tm)
    lane = lax.broadcasted_iota(jnp.int32, (tm, ROUTE_W), 1)
    rec = jnp.zeros((tm, ROUTE_W), F32)
    for i, part in enumerate(_exact_bf16_parts(w_lo) + _exact_bf16_parts(w_hi)):
        rec = jnp.where(lane == i, part, rec)
    h2_b = h2.astype(BF16)
    rec_b = rec.astype(BF16)
    sort_lo, sort_hi = _slab_sort_matrices(pos_row, tm, 0)
    pay_lo = jnp.concatenate([h2_b[:, 0:HALF_W], rec_b], axis=1)
    pay_hi = jnp.concatenate([h2_b[:, HALF_W:], jnp.zeros((tm, ROUTE_W), BF16)], axis=1)
    slabs = _dot(sort_lo, pay_lo) + _dot(sort_hi, pay_hi)
    for c in range(SLAB_IN_W // LANES):
        hxs_ref[c] = slabs[:, c * LANES:(c + 1) * LANES]
    pos_ref[...] = jnp.broadcast_to(pos, (tm, ROUTE_W))
    chunks_ref[...] = jnp.broadcast_to(chunks, chunks_ref.shape)


def _mix_out(x, o_f, o_b, sr, ya, mod, lnp, gn, w_out_b, wr, br):
    bsz, t, _ = x.shape
    tm = SORT_TILE
    assert t % tm == 0
    n_t = t // tm
    tok = lambda w: pl.BlockSpec((None, tm, w), lambda b, i: (b, i, 0))
    full = lambda a: pl.BlockSpec(a.shape, lambda b, i: (0,) * a.ndim)
    flat = lambda rows, w: pl.BlockSpec((rows, w), lambda b, i: (b * n_t + i, 0))
    return pl.pallas_call(
        functools.partial(_mix_out_kernel, tm),
        grid=(bsz, n_t),
        in_specs=[
            tok(D_MODEL), tok(GLA_VAL), tok(GLA_VAL), tok(GLA_VAL), tok(CONV_CH),
            pl.BlockSpec((None, 3, D_MODEL), lambda b, i: (b, 0, 0)),
            full(lnp), full(gn), full(w_out_b), full(wr), full(br),
        ],
        out_specs=[tok(D_MODEL),
                   pl.BlockSpec((SLAB_IN_W // LANES, LOCAL_SLAB_ROWS, LANES), lambda b, i: (0, b * n_t + i, 0)),
                   flat(tm, ROUTE_W), flat(8, ROUTE_W)],
        out_shape=[
            jax.ShapeDtypeStruct((bsz, t, D_MODEL), F32),
            jax.ShapeDtypeStruct((SLAB_IN_W // LANES, bsz * n_t * LOCAL_SLAB_ROWS, LANES), F32),
            jax.ShapeDtypeStruct((bsz * t, ROUTE_W), F32),
            jax.ShapeDtypeStruct((bsz * n_t * 8, ROUTE_W), F32),
        ],
        compiler_params=_params(2),
        name="mix_out",
    )(x, o_f, o_b, sr, ya, mod, lnp, gn, w_out_b, wr, br)


def _moe_kernel(n_chunks, nused_ref, lo_ref, hi_ref, live_ref, src_ref,
                hxs_hbm, w13l_ref, w2l_ref, w13h_ref, w2h_ref, out_hbm, gbuf, obuf, gsem, ssem):
    tile_rows = CHUNKS_PER_TILE * SLAB_ROWS
    i = pl.program_id(0)
    n_used = nused_ref[0]
    slot = lax.bitwise_and(i, 1)

    def slab(chunk):
        return pl.ds(pl.multiple_of(chunk * SLAB_ROWS, SLAB_ROWS), SLAB_ROWS)

    def gather_copy(tile, buf_slot, j):
        chunk = jnp.maximum(src_ref[tile * CHUNKS_PER_TILE + j], 0)
        return pltpu.make_async_copy(hxs_hbm.at[:, slab(chunk), :], gbuf.at[buf_slot, :, slab(j), :],
                                     gsem.at[buf_slot])

    def scatter_copy(tile, buf_slot, j):
        chunk = src_ref[tile * CHUNKS_PER_TILE + j]
        chunk = jnp.where(chunk < 0, n_chunks + buf_slot * CHUNKS_PER_TILE + j, chunk)
        return pltpu.make_async_copy(obuf.at[buf_slot, :, slab(j), :], out_hbm.at[:, slab(chunk), :],
                                     ssem.at[buf_slot])

    def start_gather(tile, buf_slot):
        for j in range(CHUNKS_PER_TILE):
            gather_copy(tile, buf_slot, j).start(priority=j % 2)

    def wait_gather(buf_slot):
        pltpu.make_async_copy(hxs_hbm.at[:, pl.ds(0, tile_rows), :], gbuf.at[buf_slot], gsem.at[buf_slot]).wait()

    def wait_scatter(buf_slot):
        pltpu.make_async_copy(obuf.at[buf_slot], out_hbm.at[:, pl.ds(0, tile_rows), :], ssem.at[buf_slot]).wait()

    @pl.when(i == 0)
    def _():
        start_gather(0, 0)
        obuf[...] = jnp.zeros(obuf.shape, F32)
        for s in range(2):
            fill = pltpu.make_async_copy(
                obuf.at[s], out_hbm.at[:, pl.ds((n_chunks + s * CHUNKS_PER_TILE) * SLAB_ROWS, tile_rows), :],
                ssem.at[s])
            fill.start()
            fill.wait()

    @pl.when(i + 1 < n_used)
    def _():
        start_gather(i + 1, 1 - slot)

    @pl.when(i < n_used)
    def _():
        wait_gather(slot)

        @pl.when(i >= 2)
        def _():
            wait_scatter(slot)

        @pl.when(live_ref[i] != 0)
        def _():
            def lane_block(c, half):
                return jnp.concatenate(
                    [gbuf[slot, c, pl.ds(2 * r + half, CHUNKS_PER_TILE, stride=SLAB_ROWS), :]
                     for r in range(CHUNK_ROWS)], axis=0)

            n_blk = HALF_W // LANES
            xb = jnp.concatenate([lane_block(c, 0) for c in range(n_blk)]
                                 + [lane_block(c, 1) for c in range(n_blk)], axis=1).astype(BF16)
            rec = lane_block(n_blk, 0)
            w_lo = rec[:, 0:1] + rec[:, 1:2] + rec[:, 2:3]
            w_hi = rec[:, 3:4] + rec[:, 4:5] + rec[:, 5:6]

            def expert(w13_ref, w2_ref):
                h13 = _dot(xb, w13_ref[...])
                act = _silu(h13[:, 0:D_EXPERT]) * h13[:, D_EXPERT:]
                return _dot(act.astype(BF16), w2_ref[...])

            y = w_lo * expert(w13l_ref, w2l_ref) + w_hi * expert(w13h_ref, w2h_ref)
            y = y.astype(BF16).astype(F32)
            for r in range(CHUNK_ROWS):
                rows = slice(r * CHUNKS_PER_TILE, (r + 1) * CHUNKS_PER_TILE)
                for half in range(2):
                    for c in range(n_blk):
                        col = half * HALF_W + c * LANES
                        obuf[slot, c, pl.ds(2 * r + half, CHUNKS_PER_TILE, stride=SLAB_ROWS), :] = (
                            y[rows, col:col + LANES])

        @pl.when(live_ref[i] == 0)
        def _():
            obuf[slot] = jnp.zeros(obuf.shape[1:], F32)

        for j in range(CHUNKS_PER_TILE):
            scatter_copy(i, slot, j).start(priority=j % 2)

        @pl.when(i == n_used - 1)
        def _():
            wait_scatter(slot)

            @pl.when(i >= 1)
            def _():
                wait_scatter(1 - slot)


def _moe(hxs, src, n_used, tile_lo, tile_hi, tile_live, w13_b, w2_b):
    n_chunks = hxs.shape[1] // SLAB_ROWS
    tile_rows = CHUNKS_PER_TILE * SLAB_ROWS
    n_steps = src.shape[0] // CHUNKS_PER_TILE
    wspec = lambda which, shape: pl.BlockSpec(
        (None,) + shape, (lambda i, nu, lo, hi, lv, s: (lo[i], 0, 0)) if which == 0 else
        (lambda i, nu, lo, hi, lv, s: (hi[i], 0, 0)))
    grid_spec = pltpu.PrefetchScalarGridSpec(
        num_scalar_prefetch=5,
        grid=(n_steps,),
        in_specs=[
            pl.BlockSpec(memory_space=pl.ANY),
            wspec(0, (D_MODEL, 2 * D_EXPERT)), wspec(0, (D_EXPERT, D_MODEL)),
            wspec(1, (D_MODEL, 2 * D_EXPERT)), wspec(1, (D_EXPERT, D_MODEL)),
        ],
        out_specs=pl.BlockSpec(memory_space=pl.ANY),
        scratch_shapes=[
            pltpu.VMEM((2, SLAB_IN_W // LANES, tile_rows, LANES), F32),
            pltpu.VMEM((2, SLAB_OUT_W // LANES, tile_rows, LANES), F32),
            pltpu.SemaphoreType.DMA((2,)),
            pltpu.SemaphoreType.DMA((2,)),
        ],
    )
    return pl.pallas_call(
        functools.partial(_moe_kernel, n_chunks),
        grid_spec=grid_spec,
        out_shape=jax.ShapeDtypeStruct((SLAB_OUT_W // LANES, (n_chunks + 2 * CHUNKS_PER_TILE) * SLAB_ROWS, LANES), F32),
        compiler_params=_params(1),
        name="moe",
    )(n_used, tile_lo, tile_hi, tile_live, src, hxs, w13_b, w2_b, w13_b, w2_b)


def _final_kernel(tm, x1_ref, moe_ref, pos_ref, mod_ref, lnp_ref, o_ref):
    moe_b = jnp.concatenate([moe_ref[c] for c in range(SLAB_OUT_W // LANES)], axis=1).astype(BF16)
    moe = jnp.concatenate([_dot(sort_t, moe_b) for sort_t in _slab_sort_matrices(pos_ref[:, 0:1], tm, 1)], axis=1)
    o_ref[...] = _layer_norm(DEEPNORM_ALPHA * x1_ref[...] + mod_ref[...] * moe, lnp_ref[0:1, :], lnp_ref[1:2, :])


def _final(x1, moe, pos, g2, lnp):
    bsz, t, _ = x1.shape
    tm = SORT_TILE
    n_t = t // tm
    flat = lambda rows, w: pl.BlockSpec((rows, w), lambda b, i: (b * n_t + i, 0))
    return pl.pallas_call(
        functools.partial(_final_kernel, tm),
        grid=(bsz, n_t),
        in_specs=[
            pl.BlockSpec((None, tm, D_MODEL), lambda b, i: (b, i, 0)),
            pl.BlockSpec((SLAB_OUT_W // LANES, LOCAL_SLAB_ROWS, LANES), lambda b, i: (0, b * n_t + i, 0)),
            flat(tm, ROUTE_W),
            pl.BlockSpec((None, 1, D_MODEL), lambda b, i: (b, 0, 0)),
            pl.BlockSpec(lnp.shape, lambda b, i: (0, 0)),
        ],
        out_specs=pl.BlockSpec((None, tm, D_MODEL), lambda b, i: (b, i, 0)),
        out_shape=jax.ShapeDtypeStruct((bsz, t, D_MODEL), F32),
        compiler_params=_params(2),
        name="final",
    )(x1, moe, pos, g2, lnp)


def _pair_tables():
    lo, hi = [], []
    for g in range(N_GROUPS):
        for a in range(EXPERTS_PER_GROUP):
            for b in range(a + 1, EXPERTS_PER_GROUP):
                lo.append(g * EXPERTS_PER_GROUP + a)
                hi.append(g * EXPERTS_PER_GROUP + b)
    return jnp.array(lo, jnp.int32), jnp.array(hi, jnp.int32)


def _moe_plan(chunks, n_sort_tiles):
    n_cls = N_CLASSES + 1
    hp = lax.Precision.HIGHEST
    m = chunks.reshape(n_sort_tiles, 8, ROUTE_W)[:, 0, :n_cls].astype(jnp.int32)
    a_end = jnp.cumsum(m, axis=0)
    a_start = a_end - m
    per_cls = a_end[-1]
    padded = (per_cls + CHUNKS_PER_TILE - 1) // CHUNKS_PER_TILE * CHUNKS_PER_TILE
    g_end = jnp.cumsum(padded)
    g_start = g_end - padded
    local_off = jnp.cumsum(m, axis=1) - m
    seg = jnp.arange(n_sort_tiles, dtype=jnp.int32)[:, None] * LOCAL_CHUNKS + local_off - a_start
    n_steps = -(-(n_sort_tiles * LOCAL_CHUNKS) // CHUNKS_PER_TILE) + n_cls
    p = jnp.arange(n_steps * CHUNKS_PER_TILE, dtype=jnp.int32)
    cls_p = jnp.minimum(jnp.sum((g_end[None, :] <= p[:, None]).astype(jnp.int32), axis=1), n_cls - 1)
    onehot = (cls_p[:, None] == jnp.arange(n_cls, dtype=jnp.int32)[None, :]).astype(F32)
    pick = lambda tab: jnp.dot(onehot, tab.astype(F32), precision=hp)
    u = p - pick(g_start[:, None])[:, 0].astype(jnp.int32)
    valid = u < pick(per_cls[:, None])[:, 0].astype(jnp.int32)
    a_end_p = pick(a_end.T).astype(jnp.int32)
    seg_p = pick(seg.T).astype(jnp.int32)
    tile_p = jnp.sum((a_end_p <= u[:, None]).astype(jnp.int32), axis=1)
    hit = jnp.arange(n_sort_tiles, dtype=jnp.int32)[None, :] == tile_p[:, None]
    src = jnp.sum(jnp.where(hit, seg_p, 0), axis=1) + u
    src = jnp.where(valid, src, -1).astype(jnp.int32)
    n_used = g_end[-1:] // CHUNKS_PER_TILE
    step = jnp.arange(n_steps, dtype=jnp.int32)
    tile_cls = jnp.sum((g_end[None, :] // CHUNKS_PER_TILE <= step[:, None]).astype(jnp.int32), axis=1)
    live = ((tile_cls < N_CLASSES) & (step < n_used[0])).astype(jnp.int32)
    pair_lo, pair_hi = _pair_tables()
    pair_oh = (jnp.minimum(tile_cls, N_CLASSES - 1)[:, None] == jnp.arange(N_CLASSES)[None, :]).astype(jnp.int32)
    tile_lo = jnp.sum(pair_oh * pair_lo[None, :], axis=1).astype(jnp.int32)
    tile_hi = jnp.sum(pair_oh * pair_hi[None, :], axis=1).astype(jnp.int32)
    return src, n_used.astype(jnp.int32), tile_lo, tile_hi, live


def kernel(x, c, ctx, c_ctx, ln_in_g, ln_in_b, w_ada, b_ada, w_in, conv_w, conv_b, gate_w2_fwd, gate_b_fwd,
           gate_w2_bwd, gate_b_bwd, gla_norm_g, w_out, ln1_g, ln1_b, router_group_w, router_group_b,
           router_expert_w, router_expert_b, expert_w1, expert_w3, expert_w2, ln2_g, ln2_b):
    bsz, t, _ = x.shape
    n_tok = bsz * t
    l = 0
    rows = -(-(bsz + 1) // 8) * 8
    cond = jnp.zeros((rows, D_MODEL), F32).at[:bsz].set(c).at[bsz].set(c_ctx)
    ada = _ada(cond, w_ada[l], b_ada[l][None, :])
    sh1, sc1, g1, sh2, sc2, g2 = [ada[:, i * D_MODEL:(i + 1) * D_MODEL] for i in range(6)]

    w_in_b = w_in[l].astype(BF16)
    lnp_in = jnp.stack([ln_in_g, ln_in_b])
    zero = jnp.zeros((GLA_GATE_RANK, GLA_KEY), F32)
    w2cat = jnp.concatenate([jnp.concatenate([gate_w2_fwd[l], zero], axis=1),
                             jnp.concatenate([zero, gate_w2_bwd[l]], axis=1)], axis=0).astype(BF16)
    gbias = jnp.concatenate([gate_b_fwd[l], gate_b_bwd[l]])[None, :]

    mod_ctx = jnp.broadcast_to(jnp.stack([1.0 + sc1[bsz], sh1[bsz]])[None], (bsz, 2, D_MODEL))
    k_c, v_c, g_c = _proj(ctx, mod_ctx, lnp_in, w_in_b, conv_w[l], conv_b[l][None, :], w2cat, gbias, False)
    zero_state = jnp.zeros((bsz, GLA_KEY, PAIR_VAL), F32)
    s_f, s_b = _gla(None, k_c, v_c, g_c, zero_state, zero_state)

    mod1 = jnp.stack([1.0 + sc1[:bsz], sh1[:bsz]], axis=1)
    ya, q, k, v, sr, g = _proj(x, mod1, lnp_in, w_in_b, conv_w[l], conv_b[l][None, :], w2cat, gbias, True)
    o_f, o_b, _, _ = _gla(q, k, v, g, s_f, s_b)

    mod2 = jnp.stack([g1[:bsz], 1.0 + sc2[:bsz], sh2[:bsz]], axis=1)
    lnp1 = jnp.stack([ln_in_g, ln_in_b, ln1_g[l], ln1_b[l]])
    wr = jnp.zeros((D_MODEL, ROUTE_W), F32)
    wr = wr.at[:, :N_GROUPS].set(router_group_w[l]).at[:, N_GROUPS:N_GROUPS + N_EXPERTS].set(router_expert_w[l])
    br = jnp.zeros((1, ROUTE_W), F32)
    br = br.at[0, :N_GROUPS].set(router_group_b[l]).at[0, N_GROUPS:N_GROUPS + N_EXPERTS].set(router_expert_b[l])
    x1, hxs, pos, chunks = _mix_out(x, o_f, o_b, sr, ya, mod2, lnp1, gla_norm_g[l][None, :],
                                    w_out[l].astype(BF16), wr, br)

    src, n_used, tile_lo, tile_hi, live = _moe_plan(chunks, n_tok // SORT_TILE)
    w13_b = jnp.concatenate([expert_w1[l], expert_w3[l]], axis=-1).astype(BF16)
    moe = _moe(hxs, src, n_used, tile_lo, tile_hi, live, w13_b, expert_w2[l].astype(BF16))

    return _final(x1, moe, pos, g2[:bsz][:, None, :], jnp.stack([ln2_g[l], ln2_b[l]]))
```

```python
import functools

import jax
import jax.numpy as jnp
from jax import lax
from jax.experimental import pallas as pl
from jax.experimental.pallas import tpu as pltpu

F32 = jnp.float32
BF16 = jnp.bfloat16

D_MODEL = 1024
GRID_W = 64
CONV_CH = 512
GLA_HEADS = 4
GLA_DK = 64
GLA_DV = 128
GLA_KEY = GLA_HEADS * GLA_DK
GLA_VAL = GLA_HEADS * GLA_DV
PAIR_KEY = 2 * GLA_DK
PAIR_VAL = 2 * GLA_DV
GLA_GATE_RANK = 16
GLA_TAU = 16.0
OFF_AB = 0
OFF_AC = OFF_AB + CONV_CH
OFF_AX = OFF_AC + CONV_CH
OFF_Q = OFF_AX + CONV_CH
OFF_K = OFF_Q + GLA_KEY
OFF_V = OFF_K + GLA_KEY
OFF_R = OFF_V + GLA_VAL
OFF_GF = OFF_R + GLA_VAL
D_PROJ = OFF_GF + 2 * GLA_GATE_RANK
N_GROUPS = 4
EXPERTS_PER_GROUP = 4
N_EXPERTS = N_GROUPS * EXPERTS_PER_GROUP
D_EXPERT = 512
PAIRS_PER_GROUP = 6
N_CLASSES = N_GROUPS * PAIRS_PER_GROUP
LN_EPS = 1e-5
RMS_EPS = 1e-6
DEPTH = 1
DEEPNORM_ALPHA = (2.0 * DEPTH) ** 0.25

LANES = 128
GLA_CHUNK = 64
GLA_SUB = 16
N_SUB = GLA_CHUNK // GLA_SUB
ROUTE_W = LANES
HALF_W = D_MODEL // 2
SLAB_IN_W = HALF_W + ROUTE_W
SLAB_OUT_W = HALF_W
SORT_TILE = 256
MOE_TILE = 256
CHUNK_ROWS = 4
SLAB_ROWS = 2 * CHUNK_ROWS
LOCAL_CHUNKS = -(-(SORT_TILE + N_CLASSES * (CHUNK_ROWS - 1)) // (8 * CHUNK_ROWS)) * 8
LOCAL_SLAB_ROWS = LOCAL_CHUNKS * SLAB_ROWS
CHUNKS_PER_TILE = MOE_TILE // CHUNK_ROWS
VMEM_LIMIT = 56 * 1024 * 1024


def _params(n_axes, vmem=VMEM_LIMIT):
    return pltpu.CompilerParams(dimension_semantics=("arbitrary",) * n_axes, vmem_limit_bytes=vmem)


def _dot(a, b):
    return jnp.dot(a, b, preferred_element_type=F32)


def _div_pow2(x, d):
    assert d & (d - 1) == 0
    return lax.shift_right_logical(x, jnp.int32(d.bit_length() - 1))


def _mod_pow2(x, d):
    assert d & (d - 1) == 0
    return lax.bitwise_and(x, jnp.int32(d - 1))


def _split2(x):
    hi = x.astype(BF16)
    lo = (x - hi.astype(F32)).astype(BF16)
    return hi, lo


def _dot3(a, b):
    ah, al = _split2(a)
    bh, bl = _split2(b)
    return _dot(ah, bh) + _dot(ah, bl) + _dot(al, bh)


def _silu(x):
    return x * (0.5 * jnp.tanh(0.5 * x) + 0.5)


def _layer_norm(x, g, b):
    mu = jnp.mean(x, axis=-1, keepdims=True)
    xc = x - mu
    var = jnp.mean(xc * xc, axis=-1, keepdims=True)
    return xc * lax.rsqrt(var + LN_EPS) * g + b


def _ada_kernel(c_ref, w_ref, b_ref, o_ref):
    o_ref[...] = _dot3(_silu(c_ref[...]), w_ref[...]) + b_ref[...]


def _ada(cond, w_ada, b_ada):
    rows = cond.shape[0]
    n_out = w_ada.shape[1]
    tn = 1024
    return pl.pallas_call(
        _ada_kernel,
        grid=(n_out // tn,),
        in_specs=[
            pl.BlockSpec((rows, D_MODEL), lambda j: (0, 0)),
            pl.BlockSpec((D_MODEL, tn), lambda j: (0, j)),
            pl.BlockSpec((1, tn), lambda j: (0, j)),
        ],
        out_specs=pl.BlockSpec((rows, tn), lambda j: (0, j)),
        out_shape=jax.ShapeDtypeStruct((rows, n_out), F32),
        compiler_params=_params(1),
        name="ada",
    )(cond, w_ada, b_ada)


def _log_sigmoid(z):
    return jnp.minimum(z, 0.0) - jnp.log(1.0 + jnp.exp(-jnp.abs(z)))


def _proj_kernel(latent, tm, x_ref, mod_ref, lnp_ref, w_ref, cw_ref, cb_ref, w2_ref, gbias_ref, *out_refs):
    x = x_ref[...]
    xn = _layer_norm(x, lnp_ref[0:1, :], lnp_ref[1:2, :])
    h = xn * mod_ref[0:1, :] + mod_ref[1:2, :]
    hb = h.astype(BF16)
    if latent:
        ya_ref, q_ref, k_ref, v_ref, sr_ref, g_ref = out_refs
        p = _dot(hb, w_ref[:, OFF_AB:OFF_Q])
        a_b = p[:, 0:CONV_CH]
        u = p[:, CONV_CH:2 * CONV_CH] * p[:, 2 * CONV_CH:3 * CONV_CH]
        pos = _mod_pow2(lax.broadcasted_iota(jnp.int32, (tm, 1), 0), GRID_W)
        u_prev = jnp.where(pos == 0, 0.0, pltpu.roll(u, 1, 0))
        u_next = jnp.where(pos == GRID_W - 1, 0.0, pltpu.roll(u, tm - 1, 0))
        conv = u_prev * cw_ref[0:1, :] + u * cw_ref[1:2, :] + u_next * cw_ref[2:3, :] + cb_ref[...]
        ya_ref[...] = (a_b * conv).astype(BF16)
        qk = _dot(hb, w_ref[:, OFF_Q:OFF_V])
        q_ref[...] = (qk[:, 0:GLA_KEY] * (GLA_DK ** -0.5)).astype(BF16)
        k_ref[...] = qk[:, GLA_KEY:].astype(BF16)
        r = _dot(hb, w_ref[:, OFF_R:OFF_GF])
        sr_ref[...] = _silu(r).astype(BF16)
    else:
        k_ref, v_ref, g_ref = out_refs
        k_ref[...] = _dot(hb, w_ref[:, OFF_K:OFF_V]).astype(BF16)
    v_ref[...] = _dot(hb, w_ref[:, OFF_V:OFF_R]).astype(BF16)
    low = _dot(hb, w_ref[:, OFF_GF:D_PROJ])
    z = _dot(low.astype(BF16), w2_ref[...]) + gbias_ref[...]
    g_ref[...] = _log_sigmoid(z) * (1.0 / GLA_TAU)


def _proj(x, mod, lnp, w_in_b, conv_w, conv_b, w2cat, gbias, latent):
    bsz, t, _ = x.shape
    tm = min(512, t)
    assert t % tm == 0 and tm % GRID_W == 0
    tok = lambda w: pl.BlockSpec((None, tm, w), lambda b, i: (b, i, 0))
    full = lambda a: pl.BlockSpec(a.shape, lambda b, i: (0,) * a.ndim)
    widths = ([(CONV_CH, BF16), (GLA_KEY, BF16)] if latent else []) + [(GLA_KEY, BF16), (GLA_VAL, BF16)]
    widths += ([(GLA_VAL, BF16)] if latent else []) + [(2 * GLA_KEY, F32)]
    return pl.pallas_call(
        functools.partial(_proj_kernel, latent, tm),
        grid=(bsz, t // tm),
        in_specs=[
            tok(D_MODEL),
            pl.BlockSpec((None, 2, D_MODEL), lambda b, i: (b, 0, 0)),
            full(lnp), full(w_in_b), full(conv_w), full(conv_b), full(w2cat), full(gbias),
        ],
        out_specs=[tok(w) for w, _ in widths],
        out_shape=[jax.ShapeDtypeStruct((bsz, t, w), dt) for w, dt in widths],
        compiler_params=_params(2),
        name="proj_latent" if latent else "proj_ctx",
    )(x, mod, lnp, w_in_b, conv_w, conv_b, w2cat, gbias)


def _tri(n, reverse, strict=False):
    i = lax.broadcasted_iota(jnp.int32, (n, n), 0)
    j = lax.broadcasted_iota(jnp.int32, (n, n), 1)
    if strict:
        m = (j > i) if reverse else (j < i)
    else:
        m = (j >= i) if reverse else (j <= i)
    return jnp.where(m, 1.0, 0.0).astype(BF16)


def _chunk_cumsum(g, reverse):
    tri = _tri(GLA_CHUNK, reverse)
    g_hi, g_lo = _split2(g)
    return _dot(tri, g_hi) + _dot(tri, g_lo)


def _as_column(row):
    return jnp.broadcast_to(row, (LANES, row.shape[1])).T


def _sub_anchors(gc, reverse):
    zero = jnp.zeros((1, GLA_KEY), F32)
    if reverse:
        return [gc[GLA_SUB * (a + 1):GLA_SUB * (a + 1) + 1] for a in range(N_SUB - 1)] + [zero]
    return [zero] + [gc[GLA_SUB * a - 1:GLA_SUB * a] for a in range(1, N_SUB)]


def _score_pairs(reverse):
    return [(a, b) for a in range(N_SUB) for b in range(N_SUB) if (b >= a if reverse else b <= a)]


def _intra_products(q, k, gc, reverse):
    r = _sub_anchors(gc, reverse)
    anchor = jnp.concatenate([jnp.broadcast_to(ra, (GLA_SUB, GLA_KEY)) for ra in r], axis=0)
    gcb = gc - anchor
    qt = q * jnp.exp(gcb)
    kt = k * jnp.exp(-gcb)
    rows = []
    for a, b in _score_pairs(reverse):
        qa = qt[GLA_SUB * a:GLA_SUB * (a + 1)]
        if a != b:
            qa = qa * jnp.exp(r[a] - r[b])
        rows.append(qa)
    qp = jnp.concatenate(rows, axis=0).astype(BF16)
    width = GLA_HEADS * GLA_CHUNK
    rr = lax.broadcasted_iota(jnp.int32, (width, GLA_KEY), 0)
    cc = lax.broadcasted_iota(jnp.int32, (width, GLA_KEY), 1)
    kbd = jnp.where(_div_pow2(rr, GLA_CHUNK) == _div_pow2(cc, GLA_DK),
                    jnp.concatenate([kt] * GLA_HEADS, axis=0), 0.0)
    return lax.dot_general(qp, kbd.astype(BF16), (((1,), (1,)), ((), ())), preferred_element_type=F32)


def _assemble_scores(res, reverse):
    pairs = _score_pairs(reverse)
    width = GLA_HEADS * GLA_CHUNK
    col = _mod_pow2(lax.broadcasted_iota(jnp.int32, (GLA_SUB, width), 1), GLA_CHUNK)
    col_blk = _div_pow2(col, GLA_SUB)
    col_pos = _mod_pow2(col, GLA_SUB)
    row_pos = lax.broadcasted_iota(jnp.int32, (GLA_SUB, width), 0)
    causal = (col_pos >= row_pos) if reverse else (col_pos <= row_pos)
    blocks = []
    for a in range(N_SUB):
        acc = jnp.zeros((GLA_SUB, width), F32)
        for idx, (pa, pb) in enumerate(pairs):
            if pa != a:
                continue
            keep = col_blk == pb
            if pa == pb:
                keep = keep & causal
            acc = acc + jnp.where(keep, res[GLA_SUB * idx:GLA_SUB * (idx + 1)], 0.0)
        blocks.append(acc)
    return jnp.concatenate(blocks, axis=0)


def _pair_mask(rows_per_head, cols_per_head, n_row_pairs=1):
    shape = (n_row_pairs * 2 * rows_per_head, 2 * cols_per_head)
    rr = _mod_pow2(lax.broadcasted_iota(jnp.int32, shape, 0), 2 * rows_per_head)
    cc = lax.broadcasted_iota(jnp.int32, shape, 1)
    return _div_pow2(rr, rows_per_head) == _div_pow2(cc, cols_per_head)


def _state_terms(q, k, v_b, gc, reverse):
    total = gc[0:1] if reverse else gc[GLA_CHUNK - 1:GLA_CHUNK]
    q_dec = None if q is None else (q * jnp.exp(gc)).astype(BF16)
    k_end = (k * jnp.exp(total - gc)).astype(BF16)
    tn = (((0,), (0,)), ((), ()))
    upd = [lax.dot_general(k_end[:, p * PAIR_KEY:(p + 1) * PAIR_KEY], v_b[:, p * PAIR_VAL:(p + 1) * PAIR_VAL], tn,
                           preferred_element_type=F32) for p in range(GLA_HEADS // 2)]
    upd = jnp.where(_pair_mask(GLA_DK, GLA_DV, GLA_HEADS // 2), jnp.concatenate(upd, axis=0), 0.0)
    decay = jnp.exp(_as_column(total))
    decay = jnp.concatenate([decay] * (PAIR_VAL // LANES), axis=1)
    return q_dec, decay, upd


def _advance_state(q_dec, decay, upd, state):
    o_inter = None
    if q_dec is not None:
        state_b = state.astype(BF16)
        o_inter = jnp.concatenate(
            [_dot(q_dec[:, p * PAIR_KEY:(p + 1) * PAIR_KEY], state_b[p * PAIR_KEY:(p + 1) * PAIR_KEY, :])
             for p in range(GLA_HEADS // 2)], axis=1)
    return o_inter, state * decay + upd


def _gla_kernel(tt, nt, with_out, *refs):
    if with_out:
        (qf_ref, kf_ref, vf_ref, gfw_ref, qb_ref, kb_ref, vb_ref, gbw_ref, s0f_ref, s0b_ref,
         of_ref, ob_ref, sf_ref, sb_ref) = refs
    else:
        kf_ref, vf_ref, gfw_ref, kb_ref, vb_ref, gbw_ref, s0f_ref, s0b_ref, sf_ref, sb_ref = refs
    j = pl.program_id(1)

    @pl.when(j == 0)
    def _():
        sf_ref[...] = s0f_ref[...]
        sb_ref[...] = s0b_ref[...]

    chunk_slices = [slice(c * GLA_CHUNK, (c + 1) * GLA_CHUNK) for c in range(tt // GLA_CHUNK)]
    fwd, bwd = [], []
    for sl in chunk_slices:
        g = gfw_ref[sl, :]
        fwd.append(dict(sl=sl, k=kf_ref[sl, :].astype(F32), v=vf_ref[sl, :], g_f=g[:, 0:GLA_KEY], g_b=g[:, GLA_KEY:],
                        q=qf_ref[sl, :].astype(F32) if with_out else None))
        bwd.append(dict(sl=sl, k=kb_ref[sl, :].astype(F32), v=vb_ref[sl, :], g_b=gbw_ref[sl, :][:, GLA_KEY:],
                        q=qb_ref[sl, :].astype(F32) if with_out else None))
    for d in fwd:
        d["gc_f"] = _chunk_cumsum(d["g_f"], False)
        if with_out:
            d["gc_b"] = _chunk_cumsum(d["g_b"], True)
    for d in bwd:
        d["gc_b"] = _chunk_cumsum(d["g_b"], True)
    if with_out:
        for d in fwd:
            d["res_f"] = _intra_products(d["q"], d["k"], d["gc_f"], False)
            d["res_b"] = _intra_products(d["q"], d["k"], d["gc_b"], True)
    for d in fwd:
        d["terms"] = _state_terms(d["q"], d["k"], d["v"], d["gc_f"], False)
    for d in bwd:
        d["terms"] = _state_terms(d["q"], d["k"], d["v"], d["gc_b"], True)
    if with_out:
        for d in fwd:
            scores = (_assemble_scores(d["res_f"], False) + _assemble_scores(d["res_b"], True)).astype(BF16)
            o_intra = []
            for p in range(GLA_HEADS // 2):
                v_p = d["v"][:, p * PAIR_VAL:(p + 1) * PAIR_VAL]
                vbd = jnp.where(_pair_mask(GLA_CHUNK, GLA_DV), jnp.concatenate([v_p, v_p], axis=0),
                                jnp.zeros((), BF16))
                o_intra.append(_dot(scores[:, p * 2 * GLA_CHUNK:(p + 1) * 2 * GLA_CHUNK], vbd))
            d["o_intra"] = jnp.concatenate(o_intra, axis=1)

    state = sf_ref[...]
    for d in fwd:
        o_inter, state = _advance_state(*d["terms"], state)
        if with_out:
            of_ref[d["sl"], :] = (d["o_intra"] + o_inter).astype(BF16)
    sf_ref[...] = state

    state = sb_ref[...]
    for d in reversed(bwd):
        o_inter, state = _advance_state(*d["terms"], state)
        if with_out:
            ob_ref[d["sl"], :] = o_inter.astype(BF16)
    sb_ref[...] = state


def _gla(q, k, v, g, s0f, s0b):
    with_out = q is not None
    bsz, t, _ = k.shape
    tt = min(256, t)
    assert t % tt == 0 and tt % GLA_CHUNK == 0
    nt = t // tt
    fwd = lambda w: pl.BlockSpec((None, tt, w), lambda b, j: (b, j, 0))
    bwd = lambda w: pl.BlockSpec((None, tt, w), lambda b, j: (b, nt - 1 - j, 0))
    st = pl.BlockSpec((None, GLA_KEY, PAIR_VAL), lambda b, j: (b, 0, 0))
    st_shape = jax.ShapeDtypeStruct((bsz, GLA_KEY, PAIR_VAL), F32)
    if with_out:
        ins = [q, k, v, g, q, k, v, g, s0f, s0b]
        in_specs = [fwd(GLA_KEY), fwd(GLA_KEY), fwd(GLA_VAL), fwd(2 * GLA_KEY),
                    bwd(GLA_KEY), bwd(GLA_KEY), bwd(GLA_VAL), bwd(2 * GLA_KEY), st, st]
        out_specs = [fwd(GLA_VAL), bwd(GLA_VAL), st, st]
        o_shape = jax.ShapeDtypeStruct((bsz, t, GLA_VAL), BF16)
        out_shape = [o_shape, o_shape, st_shape, st_shape]
    else:
        ins = [k, v, g, k, v, g, s0f, s0b]
        in_specs = [fwd(GLA_KEY), fwd(GLA_VAL), fwd(2 * GLA_KEY),
                    bwd(GLA_KEY), bwd(GLA_VAL), bwd(2 * GLA_KEY), st, st]
        out_specs = [st, st]
        out_shape = [st_shape, st_shape]
    return pl.pallas_call(
        functools.partial(_gla_kernel, tt, nt, with_out),
        grid=(bsz, nt),
        in_specs=in_specs,
        out_specs=out_specs,
        out_shape=out_shape,
        compiler_params=_params(2),
        name="gla_latent" if with_out else "gla_ctx",
    )(*ins)


def _row_max(x):
    return jnp.max(x, axis=-1, keepdims=True)


def _row_min(x):
    return jnp.min(x, axis=-1, keepdims=True)


def _row_sum(x):
    return jnp.sum(x, axis=-1, keepdims=True)


def _exact_bf16_parts(x):
    hi = x.astype(BF16).astype(F32)
    r = x - hi
    mid = r.astype(BF16).astype(F32)
    lo = (r - mid).astype(BF16).astype(F32)
    return hi, mid, lo


def _route(logit, tm):
    lane = lax.broadcasted_iota(jnp.int32, (tm, ROUTE_W), 1).astype(F32)
    none = float(ROUTE_W - 1)
    gmask = lane < N_GROUPS
    eg = jnp.where(gmask, jnp.exp(logit - _row_max(jnp.where(gmask, logit, -jnp.inf))), 0.0)
    pg = eg / _row_sum(eg)
    p_g = _row_max(pg)
    g_idx = _row_min(jnp.where(gmask & (pg >= p_g), lane, none))
    first = N_GROUPS + EXPERTS_PER_GROUP * g_idx
    emask = (lane >= first) & (lane < first + EXPERTS_PER_GROUP)
    ee = jnp.where(emask, jnp.exp(logit - _row_max(jnp.where(emask, logit, -jnp.inf))), 0.0)
    pe = jnp.where(emask, ee / _row_sum(ee), -1.0)
    p1 = _row_max(pe)
    i1 = _row_min(jnp.where(emask & (pe >= p1), lane, none))
    rest = emask & (lane != i1)
    pe2 = jnp.where(rest, pe, -1.0)
    p2 = _row_max(pe2)
    i2 = _row_min(jnp.where(rest & (pe2 >= p2), lane, none))
    den = p1 + p2
    w1 = p1 / den * p_g
    w2 = p2 / den * p_g
    l1 = i1 - first
    l2 = i2 - first
    lo = jnp.minimum(l1, l2)
    hi = jnp.maximum(l1, l2)
    pair = lo * (7.0 - lo) * 0.5 + (hi - lo - 1.0)
    cls = g_idx * PAIRS_PER_GROUP + pair
    w_lo = jnp.where(l1 < l2, w1, w2)
    w_hi = jnp.where(l1 < l2, w2, w1)
    onehot = jnp.where(lane == cls, 1.0, 0.0)
    before = _dot(_tri(tm, False, strict=True), onehot.astype(BF16))
    count = jnp.sum(onehot, axis=0, keepdims=True)
    chunks = jnp.floor((count + (CHUNK_ROWS - 1.0)) * (1.0 / CHUNK_ROWS))
    chunks = jnp.where(lane[0:1] == float(N_CLASSES), LOCAL_CHUNKS - _row_sum(chunks), chunks)
    first_chunk = _dot(jnp.broadcast_to(chunks, (8, ROUTE_W)).astype(BF16), _tri(ROUTE_W, True, strict=True))[0:1]
    val = onehot * (CHUNK_ROWS * first_chunk + before)
    pos = _row_sum(val)
    digit_hi = jnp.floor(val * (1.0 / 16.0))
    digit_lo = val - 16.0 * digit_hi
    ones = jnp.ones((8, ROUTE_W), BF16)
    nt = (((1,), (1,)), ((), ()))
    pos_row = (16.0 * lax.dot_general(ones, digit_hi.astype(BF16), nt, preferred_element_type=F32)
               + lax.dot_general(ones, digit_lo.astype(BF16), nt, preferred_element_type=F32))[0:1]
    return pos, pos_row, w_lo, w_hi, chunks


def _slab_sort_matrices(pos, tm, slab_axis):
    shape = (LOCAL_SLAB_ROWS, tm) if slab_axis == 0 else (tm, LOCAL_SLAB_ROWS)
    slab_row = lax.broadcasted_iota(jnp.int32, shape, slab_axis)
    sub = _mod_pow2(slab_row, SLAB_ROWS)
    token_row = CHUNK_ROWS * _div_pow2(slab_row, SLAB_ROWS) + _div_pow2(sub, 2)
    hit = token_row.astype(F32) == pos
    half = _mod_pow2(sub, 2)
    return [jnp.where(hit & (half == h), 1.0, 0.0).astype(BF16) for h in range(2)]


def _mix_out_kernel(tm, x_ref, of_ref, ob_ref, sr_ref, ya_ref, mod_ref, lnp_ref, gn_ref, wo_ref, wr_ref, br_ref,
                    x1_ref, hxs_ref, pos_ref, chunks_ref):
    o = of_ref[...].astype(F32) + ob_ref[...].astype(F32)
    sr = sr_ref[...].astype(F32)
    y = _dot(ya_ref[...], wo_ref[0:CONV_CH, :])
    for h in range(GLA_HEADS):
        sl = slice(h * GLA_DV, (h + 1) * GLA_DV)
        oh = o[:, sl]
        ms = jnp.mean(oh * oh, axis=-1, keepdims=True)
        yb = oh * lax.rsqrt(ms + RMS_EPS) * gn_ref[...] * sr[:, sl]
        y = y + _dot(yb.astype(BF16), wo_ref[CONV_CH + h * GLA_DV:CONV_CH + (h + 1) * GLA_DV, :])
    xn = _layer_norm(x_ref[...], lnp_ref[0:1, :], lnp_ref[1:2, :])
    x1 = _layer_norm(DEEPNORM_ALPHA * xn + mod_ref[0:1, :] * y, lnp_ref[2:3, :], lnp_ref[3:4, :])
    x1_ref[...] = x1
    h2 = x1 * mod_ref[1:2, :] + mod_ref[2:3, :]
    logit = _dot3(h2, wr_ref[...]) + br_ref[...]
    pos, pos_row, w_lo, w_hi, chunks = _route(logit, tm)
    lane = lax.broadcasted_iota(jnp.int32, (tm, ROUTE_W), 1)
    rec = jnp.zeros((tm, ROUTE_W), F32)
    for i, part in enumerate(_exact_bf16_parts(w_lo) + _exact_bf16_parts(w_hi)):
        rec = jnp.where(lane == i, part, rec)
    h2_b = h2.astype(BF16)
    rec_b = rec.astype(BF16)
    sort_lo, sort_hi = _slab_sort_matrices(pos_row, tm, 0)
    pay_lo = jnp.concatenate([h2_b[:, 0:HALF_W], rec_b], axis=1)
    pay_hi = jnp.concatenate([h2_b[:, HALF_W:], jnp.zeros((tm, ROUTE_W), BF16)], axis=1)
    slabs = _dot(sort_lo, pay_lo) + _dot(sort_hi, pay_hi)
    for c in range(SLAB_IN_W // LANES):
        hxs_ref[c] = slabs[:, c * LANES:(c + 1) * LANES]
    pos_ref[...] = jnp.broadcast_to(pos, (tm, ROUTE_W))
    chunks_ref[...] = jnp.broadcast_to(chunks, chunks_ref.shape)


def _mix_out(x, o_f, o_b, sr, ya, mod, lnp, gn, w_out_b, wr, br):
    bsz, t, _ = x.shape
    tm = SORT_TILE
    assert t % tm == 0
    n_t = t // tm
    tok = lambda w: pl.BlockSpec((None, tm, w), lambda b, i: (b, i, 0))
    full = lambda a: pl.BlockSpec(a.shape, lambda b, i: (0,) * a.ndim)
    flat = lambda rows, w: pl.BlockSpec((rows, w), lambda b, i: (b * n_t + i, 0))
    return pl.pallas_call(
        functools.partial(_mix_out_kernel, tm),
        grid=(bsz, n_t),
        in_specs=[
            tok(D_MODEL), tok(GLA_VAL), tok(GLA_VAL), tok(GLA_VAL), tok(CONV_CH),
            pl.BlockSpec((None, 3, D_MODEL), lambda b, i: (b, 0, 0)),
            full(lnp), full(gn), full(w_out_b), full(wr), full(br),
        ],
        out_specs=[tok(D_MODEL),
                   pl.BlockSpec((SLAB_IN_W // LANES, LOCAL_SLAB_ROWS, LANES), lambda b, i: (0, b * n_t + i, 0)),
                   flat(tm, ROUTE_W), flat(8, ROUTE_W)],
        out_shape=[
            jax.ShapeDtypeStruct((bsz, t, D_MODEL), F32),
            jax.ShapeDtypeStruct((SLAB_IN_W // LANES, bsz * n_t * LOCAL_SLAB_ROWS, LANES), F32),
            jax.ShapeDtypeStruct((bsz * t, ROUTE_W), F32),
            jax.ShapeDtypeStruct((bsz * n_t * 8, ROUTE_W), F32),
        ],
        compiler_params=_params(2),
        name="mix_out",
    )(x, o_f, o_b, sr, ya, mod, lnp, gn, w_out_b, wr, br)


def _moe_kernel(n_chunks, nused_ref, lo_ref, hi_ref, live_ref, src_ref,
                hxs_hbm, w13l_ref, w2l_ref, w13h_ref, w2h_ref, out_hbm, gbuf, obuf, gsem, ssem):
    tile_rows = CHUNKS_PER_TILE * SLAB_ROWS
    i = pl.program_id(0)
    n_used = nused_ref[0]
    slot = lax.bitwise_and(i, 1)

    def slab(chunk):
        return pl.ds(pl.multiple_of(chunk * SLAB_ROWS, SLAB_ROWS), SLAB_ROWS)

    def gather_copy(tile, buf_slot, j):
        chunk = jnp.maximum(src_ref[tile * CHUNKS_PER_TILE + j], 0)
        return pltpu.make_async_copy(hxs_hbm.at[:, slab(chunk), :], gbuf.at[buf_slot, :, slab(j), :],
                                     gsem.at[buf_slot])

    def scatter_copy(tile, buf_slot, j):
        chunk = src_ref[tile * CHUNKS_PER_TILE + j]
        chunk = jnp.where(chunk < 0, n_chunks + buf_slot * CHUNKS_PER_TILE + j, chunk)
        return pltpu.make_async_copy(obuf.at[buf_slot, :, slab(j), :], out_hbm.at[:, slab(chunk), :],
                                     ssem.at[buf_slot])

    def start_gather(tile, buf_slot):
        for j in range(CHUNKS_PER_TILE):
            gather_copy(tile, buf_slot, j).start(priority=j % 2)

    def wait_gather(buf_slot):
        pltpu.make_async_copy(hxs_hbm.at[:, pl.ds(0, tile_rows), :], gbuf.at[buf_slot], gsem.at[buf_slot]).wait()

    def wait_scatter(buf_slot):
        pltpu.make_async_copy(obuf.at[buf_slot], out_hbm.at[:, pl.ds(0, tile_rows), :], ssem.at[buf_slot]).wait()

    @pl.when(i == 0)
    def _():
        start_gather(0, 0)
        obuf[...] = jnp.zeros(obuf.shape, F32)
        for s in range(2):
            fill = pltpu.make_async_copy(
                obuf.at[s], out_hbm.at[:, pl.ds((n_chunks + s * CHUNKS_PER_TILE) * SLAB_ROWS, tile_rows), :],
                ssem.at[s])
            fill.start()
            fill.wait()

    @pl.when(i + 1 < n_used)
    def _():
        start_gather(i + 1, 1 - slot)

    @pl.when(i < n_used)
    def _():
        wait_gather(slot)

        @pl.when(i >= 2)
        def _():
            wait_scatter(slot)

        @pl.when(live_ref[i] != 0)
        def _():
            def lane_block(c, half):
                return jnp.concatenate(
                    [gbuf[slot, c, pl.ds(2 * r + half, CHUNKS_PER_TILE, stride=SLAB_ROWS), :]
                     for r in range(CHUNK_ROWS)], axis=0)

            n_blk = HALF_W // LANES
            xb = jnp.concatenate([lane_block(c, 0) for c in range(n_blk)]
                                 + [lane_block(c, 1) for c in range(n_blk)], axis=1).astype(BF16)
            rec = lane_block(n_blk, 0)
            w_lo = rec[:, 0:1] + rec[:, 1:2] + rec[:, 2:3]
            w_hi = rec[:, 3:4] + rec[:, 4:5] + rec[:, 5:6]

            def expert(w13_ref, w2_ref):
                h13 = _dot(xb, w13_ref[...])
                act = _silu(h13[:, 0:D_EXPERT]) * h13[:, D_EXPERT:]
                return _dot(act.astype(BF16), w2_ref[...])

            y = w_lo * expert(w13l_ref, w2l_ref) + w_hi * expert(w13h_ref, w2h_ref)
            y = y.astype(BF16).astype(F32)
            for r in range(CHUNK_ROWS):
                rows = slice(r * CHUNKS_PER_TILE, (r + 1) * CHUNKS_PER_TILE)
                for half in range(2):
                    for c in range(n_blk):
                        col = half * HALF_W + c * LANES
                        obuf[slot, c, pl.ds(2 * r + half, CHUNKS_PER_TILE, stride=SLAB_ROWS), :] = (
                            y[rows, col:col + LANES])

        @pl.when(live_ref[i] == 0)
        def _():
            obuf[slot] = jnp.zeros(obuf.shape[1:], F32)

        for j in range(CHUNKS_PER_TILE):
            scatter_copy(i, slot, j).start(priority=j % 2)

        @pl.when(i == n_used - 1)
        def _():
            wait_scatter(slot)

            @pl.when(i >= 1)
            def _():
                wait_scatter(1 - slot)


def _moe(hxs, src, n_used, tile_lo, tile_hi, tile_live, w13_b, w2_b):
    n_chunks = hxs.shape[1] // SLAB_ROWS
    tile_rows = CHUNKS_PER_TILE * SLAB_ROWS
    n_steps = src.shape[0] // CHUNKS_PER_TILE
    wspec = lambda which, shape: pl.BlockSpec(
        (None,) + shape, (lambda i, nu, lo, hi, lv, s: (lo[i], 0, 0)) if which == 0 else
        (lambda i, nu, lo, hi, lv, s: (hi[i], 0, 0)))
    grid_spec = pltpu.PrefetchScalarGridSpec(
        num_scalar_prefetch=5,
        grid=(n_steps,),
        in_specs=[
            pl.BlockSpec(memory_space=pl.ANY),
            wspec(0, (D_MODEL, 2 * D_EXPERT)), wspec(0, (D_EXPERT, D_MODEL)),
            wspec(1, (D_MODEL, 2 * D_EXPERT)), wspec(1, (D_EXPERT, D_MODEL)),
        ],
        out_specs=pl.BlockSpec(memory_space=pl.ANY),
        scratch_shapes=[
            pltpu.VMEM((2, SLAB_IN_W // LANES, tile_rows, LANES), F32),
            pltpu.VMEM((2, SLAB_OUT_W // LANES, tile_rows, LANES), F32),
            pltpu.SemaphoreType.DMA((2,)),
            pltpu.SemaphoreType.DMA((2,)),
        ],
    )
    return pl.pallas_call(
        functools.partial(_moe_kernel, n_chunks),
        grid_spec=grid_spec,
        out_shape=jax.ShapeDtypeStruct((SLAB_OUT_W // LANES, (n_chunks + 2 * CHUNKS_PER_TILE) * SLAB_ROWS, LANES), F32),
        compiler_params=_params(1),
        name="moe",
    )(n_used, tile_lo, tile_hi, tile_live, src, hxs, w13_b, w2_b, w13_b, w2_b)


def _final_kernel(tm, x1_ref, moe_ref, pos_ref, mod_ref, lnp_ref, o_ref):
    moe_b = jnp.concatenate([moe_ref[c] for c in range(SLAB_OUT_W // LANES)], axis=1).astype(BF16)
    moe = jnp.concatenate([_dot(sort_t, moe_b) for sort_t in _slab_sort_matrices(pos_ref[:, 0:1], tm, 1)], axis=1)
    o_ref[...] = _layer_norm(DEEPNORM_ALPHA * x1_ref[...] + mod_ref[...] * moe, lnp_ref[0:1, :], lnp_ref[1:2, :])


def _final(x1, moe, pos, g2, lnp):
    bsz, t, _ = x1.shape
    tm = SORT_TILE
    n_t = t // tm
    flat = lambda rows, w: pl.BlockSpec((rows, w), lambda b, i: (b * n_t + i, 0))
    return pl.pallas_call(
        functools.partial(_final_kernel, tm),
        grid=(bsz, n_t),
        in_specs=[
            pl.BlockSpec((None, tm, D_MODEL), lambda b, i: (b, i, 0)),
            pl.BlockSpec((SLAB_OUT_W // LANES, LOCAL_SLAB_ROWS, LANES), lambda b, i: (0, b * n_t + i, 0)),
            flat(tm, ROUTE_W),
            pl.BlockSpec((None, 1, D_MODEL), lambda b, i: (b, 0, 0)),
            pl.BlockSpec(lnp.shape, lambda b, i: (0, 0)),
        ],
        out_specs=pl.BlockSpec((None, tm, D_MODEL), lambda b, i: (b, i, 0)),
        out_shape=jax.ShapeDtypeStruct((bsz, t, D_MODEL), F32),
        compiler_params=_params(2),
        name="final",
    )(x1, moe, pos, g2, lnp)


def _pair_tables():
    lo, hi = [], []
    for g in range(N_GROUPS):
        for a in range(EXPERTS_PER_GROUP):
            for b in range(a + 1, EXPERTS_PER_GROUP):
                lo.append(g * EXPERTS_PER_GROUP + a)
                hi.append(g * EXPERTS_PER_GROUP + b)
    return jnp.array(lo, jnp.int32), jnp.array(hi, jnp.int32)


def _moe_plan(chunks, n_sort_tiles):
    n_cls = N_CLASSES + 1
    hp = lax.Precision.HIGHEST
    m = chunks.reshape(n_sort_tiles, 8, ROUTE_W)[:, 0, :n_cls].astype(jnp.int32)
    a_end = jnp.cumsum(m, axis=0)
    a_start = a_end - m
    per_cls = a_end[-1]
    padded = (per_cls + CHUNKS_PER_TILE - 1) // CHUNKS_PER_TILE * CHUNKS_PER_TILE
    g_end = jnp.cumsum(padded)
    g_start = g_end - padded
    local_off = jnp.cumsum(m, axis=1) - m
    seg = jnp.arange(n_sort_tiles, dtype=jnp.int32)[:, None] * LOCAL_CHUNKS + local_off - a_start
    n_steps = -(-(n_sort_tiles * LOCAL_CHUNKS) // CHUNKS_PER_TILE) + n_cls
    p = jnp.arange(n_steps * CHUNKS_PER_TILE, dtype=jnp.int32)
    cls_p = jnp.minimum(jnp.sum((g_end[None, :] <= p[:, None]).astype(jnp.int32), axis=1), n_cls - 1)
    onehot = (cls_p[:, None] == jnp.arange(n_cls, dtype=jnp.int32)[None, :]).astype(F32)
    pick = lambda tab: jnp.dot(onehot, tab.astype(F32), precision=hp)
    u = p - pick(g_start[:, None])[:, 0].astype(jnp.int32)
    valid = u < pick(per_cls[:, None])[:, 0].astype(jnp.int32)
    a_end_p = pick(a_end.T).astype(jnp.int32)
    seg_p = pick(seg.T).astype(jnp.int32)
    tile_p = jnp.sum((a_end_p <= u[:, None]).astype(jnp.int32), axis=1)
    hit = jnp.arange(n_sort_tiles, dtype=jnp.int32)[None, :] == tile_p[:, None]
    src = jnp.sum(jnp.where(hit, seg_p, 0), axis=1) + u
    src = jnp.where(valid, src, -1).astype(jnp.int32)
    n_used = g_end[-1:] // CHUNKS_PER_TILE
    step = jnp.arange(n_steps, dtype=jnp.int32)
    tile_cls = jnp.sum((g_end[None, :] // CHUNKS_PER_TILE <= step[:, None]).astype(jnp.int32), axis=1)
    live = ((tile_cls < N_CLASSES) & (step < n_used[0])).astype(jnp.int32)
    pair_lo, pair_hi = _pair_tables()
    pair_oh = (jnp.minimum(tile_cls, N_CLASSES - 1)[:, None] == jnp.arange(N_CLASSES)[None, :]).astype(jnp.int32)
    tile_lo = jnp.sum(pair_oh * pair_lo[None, :], axis=1).astype(jnp.int32)
    tile_hi = jnp.sum(pair_oh * pair_hi[None, :], axis=1).astype(jnp.int32)
    return src, n_used.astype(jnp.int32), tile_lo, tile_hi, live


def kernel(x, c, ctx, c_ctx, ln_in_g, ln_in_b, w_ada, b_ada, w_in, conv_w, conv_b, gate_w2_fwd, gate_b_fwd,
           gate_w2_bwd, gate_b_bwd, gla_norm_g, w_out, ln1_g, ln1_b, router_group_w, router_group_b,
           router_expert_w, router_expert_b, expert_w1, expert_w3, expert_w2, ln2_g, ln2_b):
    bsz, t, _ = x.shape
    n_tok = bsz * t
    l = 0
    rows = -(-(bsz + 1) // 8) * 8
    cond = jnp.zeros((rows, D_MODEL), F32).at[:bsz].set(c).at[bsz].set(c_ctx)
    ada = _ada(cond, w_ada[l], b_ada[l][None, :])
    sh1, sc1, g1, sh2, sc2, g2 = [ada[:, i * D_MODEL:(i + 1) * D_MODEL] for i in range(6)]

    w_in_b = w_in[l].astype(BF16)
    lnp_in = jnp.stack([ln_in_g, ln_in_b])
    zero = jnp.zeros((GLA_GATE_RANK, GLA_KEY), F32)
    w2cat = jnp.concatenate([jnp.concatenate([gate_w2_fwd[l], zero], axis=1),
                             jnp.concatenate([zero, gate_w2_bwd[l]], axis=1)], axis=0).astype(BF16)
    gbias = jnp.concatenate([gate_b_fwd[l], gate_b_bwd[l]])[None, :]

    mod_ctx = jnp.broadcast_to(jnp.stack([1.0 + sc1[bsz], sh1[bsz]])[None], (bsz, 2, D_MODEL))
    k_c, v_c, g_c = _proj(ctx, mod_ctx, lnp_in, w_in_b, conv_w[l], conv_b[l][None, :], w2cat, gbias, False)
    zero_state = jnp.zeros((bsz, GLA_KEY, PAIR_VAL), F32)
    s_f, s_b = _gla(None, k_c, v_c, g_c, zero_state, zero_state)

    mod1 = jnp.stack([1.0 + sc1[:bsz], sh1[:bsz]], axis=1)
    ya, q, k, v, sr, g = _proj(x, mod1, lnp_in, w_in_b, conv_w[l], conv_b[l][None, :], w2cat, gbias, True)
    o_f, o_b, _, _ = _gla(q, k, v, g, s_f, s_b)

    mod2 = jnp.stack([g1[:bsz], 1.0 + sc2[:bsz], sh2[:bsz]], axis=1)
    lnp1 = jnp.stack([ln_in_g, ln_in_b, ln1_g[l], ln1_b[l]])
    wr = jnp.zeros((D_MODEL, ROUTE_W), F32)
    wr = wr.at[:, :N_GROUPS].set(router_group_w[l]).at[:, N_GROUPS:N_GROUPS + N_EXPERTS].set(router_expert_w[l])
    br = jnp.zeros((1, ROUTE_W), F32)
    br = br.at[0, :N_GROUPS].set(router_group_b[l]).at[0, N_GROUPS:N_GROUPS + N_EXPERTS].set(router_expert_b[l])
    x1, hxs, pos, chunks = _mix_out(x, o_f, o_b, sr, ya, mod2, lnp1, gla_norm_g[l][None, :],
                                    w_out[l].astype(BF16), wr, br)

    src, n_used, tile_lo, tile_hi, live = _moe_plan(chunks, n_tok // SORT_TILE)
    w13_b = jnp.concatenate([expert_w1[l], expert_w3[l]], axis=-1).astype(BF16)
    moe = _moe(hxs, src, n_used, tile_lo, tile_hi, live, w13_b, expert_w2[l].astype(BF16))

    return _final(x1, moe, pos, g2[:bsz][:, None, :], jnp.stack([ln2_g[l], ln2_b[l]]))
```

```python
import functools

import jax
import jax.numpy as jnp
from jax import lax
from jax.experimental import pallas as pl
from jax.experimental.pallas import tpu as pltpu

F32 = jnp.float32
BF16 = jnp.bfloat16

D_MODEL = 1024
GRID_W = 64
CONV_CH = 512
GLA_HEADS = 4
GLA_DK = 64
GLA_DV = 128
GLA_KEY = GLA_HEADS * GLA_DK
GLA_VAL = GLA_HEADS * GLA_DV
PAIR_KEY = 2 * GLA_DK
PAIR_VAL = 2 * GLA_DV
GLA_GATE_RANK = 16
GLA_TAU = 16.0
OFF_AB = 0
OFF_AC = OFF_AB + CONV_CH
OFF_AX = OFF_AC + CONV_CH
OFF_Q = OFF_AX + CONV_CH
OFF_K = OFF_Q + GLA_KEY
OFF_V = OFF_K + GLA_KEY
OFF_R = OFF_V + GLA_VAL
OFF_GF = OFF_R + GLA_VAL
D_PROJ = OFF_GF + 2 * GLA_GATE_RANK
N_GROUPS = 4
EXPERTS_PER_GROUP = 4
N_EXPERTS = N_GROUPS * EXPERTS_PER_GROUP
D_EXPERT = 512
PAIRS_PER_GROUP = 6
N_CLASSES = N_GROUPS * PAIRS_PER_GROUP
LN_EPS = 1e-5
RMS_EPS = 1e-6
DEPTH = 1
DEEPNORM_ALPHA = (2.0 * DEPTH) ** 0.25

LANES = 128
GLA_CHUNK = 64
GLA_SUB = 16
N_SUB = GLA_CHUNK // GLA_SUB
ROUTE_W = LANES
HALF_W = D_MODEL // 2
SLAB_IN_W = HALF_W + ROUTE_W
SLAB_OUT_W = HALF_W
PROJ_SUB_TILE = 256
SORT_TILE = 256
MIX_SUB_TILE = 128
MOE_TILE = 256
CHUNK_ROWS = 4
SLAB_ROWS = 2 * CHUNK_ROWS
LOCAL_CHUNKS = -(-(SORT_TILE + N_CLASSES * (CHUNK_ROWS - 1)) // (8 * CHUNK_ROWS)) * 8
LOCAL_SLAB_ROWS = LOCAL_CHUNKS * SLAB_ROWS
CHUNKS_PER_TILE = MOE_TILE // CHUNK_ROWS
VMEM_LIMIT = 56 * 1024 * 1024


def _params(n_axes, vmem=VMEM_LIMIT):
    return pltpu.CompilerParams(dimension_semantics=("arbitrary",) * n_axes, vmem_limit_bytes=vmem)


def _dot(a, b):
    return jnp.dot(a, b, preferred_element_type=F32)


def _div_pow2(x, d):
    assert d & (d - 1) == 0
    return lax.shift_right_logical(x, jnp.int32(d.bit_length() - 1))


def _mod_pow2(x, d):
    assert d & (d - 1) == 0
    return lax.bitwise_and(x, jnp.int32(d - 1))


def _split2(x):
    hi = x.astype(BF16)
    lo = (x - hi.astype(F32)).astype(BF16)
    return hi, lo


def _dot3(a, b):
    ah, al = _split2(a)
    bh, bl = _split2(b)
    return _dot(ah, bh) + _dot(ah, bl) + _dot(al, bh)


def _dot3_nt(a, b):
    ah, al = _split2(a)
    bh, bl = _split2(b)
    nt = (((1,), (1,)), ((), ()))
    d = lambda p, q: lax.dot_general(p, q, nt, preferred_element_type=F32)
    return d(ah, bh) + d(ah, bl) + d(al, bh)


def _silu(x):
    return x * (0.5 * jnp.tanh(0.5 * x) + 0.5)


def _layer_norm(x, g, b):
    mu = jnp.mean(x, axis=-1, keepdims=True)
    xc = x - mu
    var = jnp.mean(xc * xc, axis=-1, keepdims=True)
    return xc * lax.rsqrt(var + LN_EPS) * g + b


def _ada_kernel(c_ref, w_ref, b_ref, o_ref):
    o_ref[...] = _dot3(_silu(c_ref[...]), w_ref[...]) + b_ref[...]


def _ada(cond, w_ada, b_ada):
    rows = cond.shape[0]
    n_out = w_ada.shape[1]
    tn = 1024
    return pl.pallas_call(
        _ada_kernel,
        grid=(n_out // tn,),
        in_specs=[
            pl.BlockSpec((rows, D_MODEL), lambda j: (0, 0)),
            pl.BlockSpec((D_MODEL, tn), lambda j: (0, j)),
            pl.BlockSpec((1, tn), lambda j: (0, j)),
        ],
        out_specs=pl.BlockSpec((rows, tn), lambda j: (0, j)),
        out_shape=jax.ShapeDtypeStruct((rows, n_out), F32),
        compiler_params=_params(1),
        name="ada",
    )(cond, w_ada, b_ada)


def _log_sigmoid(z):
    return jnp.minimum(z, 0.0) - jnp.log(1.0 + jnp.exp(-jnp.abs(z)))


def _proj_kernel(latent, tm, x_ref, mod_ref, lnp_ref, w_ref, cw_ref, cb_ref, w2_ref, gbias_ref, *out_refs):
    sub = min(PROJ_SUB_TILE, tm)
    subs = [slice(s, s + sub) for s in range(0, tm, sub)]
    hb = []
    for rows in subs:
        xn = _layer_norm(x_ref[rows, :], lnp_ref[0:1, :], lnp_ref[1:2, :])
        hb.append((xn * mod_ref[0:1, :] + mod_ref[1:2, :]).astype(BF16))
    if latent:
        ya_ref, q_ref, k_ref, v_ref, sr_ref, g_ref = out_refs
        p = [_dot(h, w_ref[:, OFF_AB:OFF_Q]) for h in hb]
        qk = [_dot(h, w_ref[:, OFF_Q:OFF_V]) for h in hb]
        pos = _mod_pow2(lax.broadcasted_iota(jnp.int32, (sub, 1), 0), GRID_W)
        for rows, p_s in zip(subs, p):
            a_b = p_s[:, 0:CONV_CH]
            u = p_s[:, CONV_CH:2 * CONV_CH] * p_s[:, 2 * CONV_CH:3 * CONV_CH]
            u_prev = jnp.where(pos == 0, 0.0, pltpu.roll(u, 1, 0))
            u_next = jnp.where(pos == GRID_W - 1, 0.0, pltpu.roll(u, sub - 1, 0))
            conv = u_prev * cw_ref[0:1, :] + u * cw_ref[1:2, :] + u_next * cw_ref[2:3, :] + cb_ref[...]
            ya_ref[rows, :] = (a_b * conv).astype(BF16)
        r = [_dot(h, w_ref[:, OFF_R:OFF_GF]) for h in hb]
        for rows, qk_s in zip(subs, qk):
            q_ref[rows, :] = (qk_s[:, 0:GLA_KEY] * (GLA_DK ** -0.5)).astype(BF16)
            k_ref[rows, :] = qk_s[:, GLA_KEY:].astype(BF16)
        v = [_dot(h, w_ref[:, OFF_V:OFF_R]) for h in hb]
        for rows, r_s in zip(subs, r):
            sr_ref[rows, :] = _silu(r_s).astype(BF16)
    else:
        k_ref, v_ref, g_ref = out_refs
        for rows, h in zip(subs, hb):
            k_ref[rows, :] = _dot(h, w_ref[:, OFF_K:OFF_V]).astype(BF16)
        v = [_dot(h, w_ref[:, OFF_V:OFF_R]) for h in hb]
    low = [_dot(h, w_ref[:, OFF_GF:D_PROJ]) for h in hb]
    for rows, v_s in zip(subs, v):
        v_ref[rows, :] = v_s.astype(BF16)
    for rows, low_s in zip(subs, low):
        z = _dot(low_s.astype(BF16), w2_ref[...]) + gbias_ref[...]
        g_ref[rows, :] = _log_sigmoid(z) * (1.0 / GLA_TAU)


def _proj(x, mod, lnp, w_in_b, conv_w, conv_b, w2cat, gbias, latent):
    bsz, t, _ = x.shape
    tm = min(512, t)
    assert t % tm == 0 and tm % GRID_W == 0
    tok = lambda w: pl.BlockSpec((None, tm, w), lambda b, i: (b, i, 0))
    full = lambda a: pl.BlockSpec(a.shape, lambda b, i: (0,) * a.ndim)
    widths = ([(CONV_CH, BF16), (GLA_KEY, BF16)] if latent else []) + [(GLA_KEY, BF16), (GLA_VAL, BF16)]
    widths += ([(GLA_VAL, BF16)] if latent else []) + [(2 * GLA_KEY, F32)]
    return pl.pallas_call(
        functools.partial(_proj_kernel, latent, tm),
        grid=(bsz, t // tm),
        in_specs=[
            tok(D_MODEL),
            pl.BlockSpec((None, 2, D_MODEL), lambda b, i: (b, 0, 0)),
            full(lnp), full(w_in_b), full(conv_w), full(conv_b), full(w2cat), full(gbias),
        ],
        out_specs=[tok(w) for w, _ in widths],
        out_shape=[jax.ShapeDtypeStruct((bsz, t, w), dt) for w, dt in widths],
        compiler_params=_params(2),
        name="proj_latent" if latent else "proj_ctx",
    )(x, mod, lnp, w_in_b, conv_w, conv_b, w2cat, gbias)


def _tri(n, reverse, strict=False):
    i = lax.broadcasted_iota(jnp.int32, (n, n), 0)
    j = lax.broadcasted_iota(jnp.int32, (n, n), 1)
    if strict:
        m = (j > i) if reverse else (j < i)
    else:
        m = (j >= i) if reverse else (j <= i)
    return jnp.where(m, 1.0, 0.0).astype(BF16)


def _chunk_cumsum(g, reverse):
    tri = _tri(GLA_CHUNK, reverse)
    g_hi, g_lo = _split2(g)
    return _dot(tri, g_hi) + _dot(tri, g_lo)


def _as_column(row):
    return jnp.broadcast_to(row, (LANES, row.shape[1])).T


def _sub_anchors(gc, reverse):
    zero = jnp.zeros((1, GLA_KEY), F32)
    if reverse:
        return [gc[GLA_SUB * (a + 1):GLA_SUB * (a + 1) + 1] for a in range(N_SUB - 1)] + [zero]
    return [zero] + [gc[GLA_SUB * a - 1:GLA_SUB * a] for a in range(1, N_SUB)]


def _score_pairs(reverse):
    return [(a, b) for a in range(N_SUB) for b in range(N_SUB) if (b >= a if reverse else b <= a)]


def _intra_products(q, k, gc, reverse):
    r = _sub_anchors(gc, reverse)
    anchor = jnp.concatenate([jnp.broadcast_to(ra, (GLA_SUB, GLA_KEY)) for ra in r], axis=0)
    gcb = gc - anchor
    qt = q * jnp.exp(gcb)
    kt = k * jnp.exp(-gcb)
    rows = []
    for a, b in _score_pairs(reverse):
        qa = qt[GLA_SUB * a:GLA_SUB * (a + 1)]
        if a != b:
            qa = qa * jnp.exp(r[a] - r[b])
        rows.append(qa)
    qp = jnp.concatenate(rows, axis=0).astype(BF16)
    width = GLA_HEADS * GLA_CHUNK
    rr = lax.broadcasted_iota(jnp.int32, (width, GLA_KEY), 0)
    cc = lax.broadcasted_iota(jnp.int32, (width, GLA_KEY), 1)
    kbd = jnp.where(_div_pow2(rr, GLA_CHUNK) == _div_pow2(cc, GLA_DK),
                    jnp.concatenate([kt] * GLA_HEADS, axis=0), 0.0)
    return lax.dot_general(qp, kbd.astype(BF16), (((1,), (1,)), ((), ())), preferred_element_type=F32)


def _assemble_scores(res, reverse):
    pairs = _score_pairs(reverse)
    width = GLA_HEADS * GLA_CHUNK
    col = _mod_pow2(lax.broadcasted_iota(jnp.int32, (GLA_SUB, width), 1), GLA_CHUNK)
    col_blk = _div_pow2(col, GLA_SUB)
    col_pos = _mod_pow2(col, GLA_SUB)
    row_pos = lax.broadcasted_iota(jnp.int32, (GLA_SUB, width), 0)
    causal = (col_pos >= row_pos) if reverse else (col_pos <= row_pos)
    blocks = []
    for a in range(N_SUB):
        acc = jnp.zeros((GLA_SUB, width), F32)
        for idx, (pa, pb) in enumerate(pairs):
            if pa != a:
                continue
            keep = col_blk == pb
            if pa == pb:
                keep = keep & causal
            acc = acc + jnp.where(keep, res[GLA_SUB * idx:GLA_SUB * (idx + 1)], 0.0)
        blocks.append(acc)
    return jnp.concatenate(blocks, axis=0)


def _pair_mask(rows_per_head, cols_per_head, n_row_pairs=1):
    shape = (n_row_pairs * 2 * rows_per_head, 2 * cols_per_head)
    rr = _mod_pow2(lax.broadcasted_iota(jnp.int32, shape, 0), 2 * rows_per_head)
    cc = lax.broadcasted_iota(jnp.int32, shape, 1)
    return _div_pow2(rr, rows_per_head) == _div_pow2(cc, cols_per_head)


def _state_terms(q, k, v_b, gc, reverse):
    total = gc[0:1] if reverse else gc[GLA_CHUNK - 1:GLA_CHUNK]
    q_dec = None if q is None else (q * jnp.exp(gc)).astype(BF16)
    k_end = (k * jnp.exp(total - gc)).astype(BF16)
    tn = (((0,), (0,)), ((), ()))
    upd = [lax.dot_general(k_end[:, p * PAIR_KEY:(p + 1) * PAIR_KEY], v_b[:, p * PAIR_VAL:(p + 1) * PAIR_VAL], tn,
                           preferred_element_type=F32) for p in range(GLA_HEADS // 2)]
    upd = jnp.where(_pair_mask(GLA_DK, GLA_DV, GLA_HEADS // 2), jnp.concatenate(upd, axis=0), 0.0)
    decay = jnp.exp(_as_column(total))
    decay = jnp.concatenate([decay] * (PAIR_VAL // LANES), axis=1)
    return q_dec, decay, upd


def _advance_state(q_dec, decay, upd, state):
    o_inter = None
    if q_dec is not None:
        state_b = state.astype(BF16)
        o_inter = jnp.concatenate(
            [_dot(q_dec[:, p * PAIR_KEY:(p + 1) * PAIR_KEY], state_b[p * PAIR_KEY:(p + 1) * PAIR_KEY, :])
             for p in range(GLA_HEADS // 2)], axis=1)
    return o_inter, state * decay + upd


def _gla_kernel(tt, nt, with_out, *refs):
    if with_out:
        (qf_ref, kf_ref, vf_ref, gfw_ref, qb_ref, kb_ref, vb_ref, gbw_ref, s0f_ref, s0b_ref,
         of_ref, ob_ref, sf_ref, sb_ref) = refs
    else:
        kf_ref, vf_ref, gfw_ref, kb_ref, vb_ref, gbw_ref, s0f_ref, s0b_ref, sf_ref, sb_ref = refs
    j = pl.program_id(1)

    @pl.when(j == 0)
    def _():
        sf_ref[...] = s0f_ref[...]
        sb_ref[...] = s0b_ref[...]

    chunk_slices = [slice(c * GLA_CHUNK, (c + 1) * GLA_CHUNK) for c in range(tt // GLA_CHUNK)]
    fwd, bwd = [], []
    for sl in chunk_slices:
        g = gfw_ref[sl, :]
        fwd.append(dict(sl=sl, k=kf_ref[sl, :].astype(F32), v=vf_ref[sl, :], g_f=g[:, 0:GLA_KEY], g_b=g[:, GLA_KEY:],
                        q=qf_ref[sl, :].astype(F32) if with_out else None))
        bwd.append(dict(sl=sl, k=kb_ref[sl, :].astype(F32), v=vb_ref[sl, :], g_b=gbw_ref[sl, :][:, GLA_KEY:],
                        q=qb_ref[sl, :].astype(F32) if with_out else None))
    for d in fwd:
        d["gc_f"] = _chunk_cumsum(d["g_f"], False)
        if with_out:
            d["gc_b"] = _chunk_cumsum(d["g_b"], True)
    for d in bwd:
        d["gc_b"] = _chunk_cumsum(d["g_b"], True)
    if with_out:
        for d in fwd:
            d["res_f"] = _intra_products(d["q"], d["k"], d["gc_f"], False)
            d["res_b"] = _intra_products(d["q"], d["k"], d["gc_b"], True)
    for d in fwd:
        d["terms"] = _state_terms(d["q"], d["k"], d["v"], d["gc_f"], False)
    for d in bwd:
        d["terms"] = _state_terms(d["q"], d["k"], d["v"], d["gc_b"], True)
    if with_out:
        for d in fwd:
            scores = (_assemble_scores(d["res_f"], False) + _assemble_scores(d["res_b"], True)).astype(BF16)
            o_intra = []
            for p in range(GLA_HEADS // 2):
                v_p = d["v"][:, p * PAIR_VAL:(p + 1) * PAIR_VAL]
                vbd = jnp.where(_pair_mask(GLA_CHUNK, GLA_DV), jnp.concatenate([v_p, v_p], axis=0),
                                jnp.zeros((), BF16))
                o_intra.append(_dot(scores[:, p * 2 * GLA_CHUNK:(p + 1) * 2 * GLA_CHUNK], vbd))
            d["o_intra"] = jnp.concatenate(o_intra, axis=1)

    state = sf_ref[...]
    for d in fwd:
        o_inter, state = _advance_state(*d["terms"], state)
        if with_out:
            of_ref[d["sl"], :] = (d["o_intra"] + o_inter).astype(BF16)
    sf_ref[...] = state

    state = sb_ref[...]
    for d in reversed(bwd):
        o_inter, state = _advance_state(*d["terms"], state)
        if with_out:
            ob_ref[d["sl"], :] = o_inter.astype(BF16)
    sb_ref[...] = state


def _gla(q, k, v, g, s0f, s0b):
    with_out = q is not None
    bsz, t, _ = k.shape
    tt = min(256, t)
    assert t % tt == 0 and tt % GLA_CHUNK == 0
    nt = t // tt
    fwd = lambda w: pl.BlockSpec((None, tt, w), lambda b, j: (b, j, 0))
    bwd = lambda w: pl.BlockSpec((None, tt, w), lambda b, j: (b, nt - 1 - j, 0))
    st = pl.BlockSpec((None, GLA_KEY, PAIR_VAL), lambda b, j: (b, 0, 0))
    st_shape = jax.ShapeDtypeStruct((bsz, GLA_KEY, PAIR_VAL), F32)
    if with_out:
        ins = [q, k, v, g, q, k, v, g, s0f, s0b]
        in_specs = [fwd(GLA_KEY), fwd(GLA_KEY), fwd(GLA_VAL), fwd(2 * GLA_KEY),
                    bwd(GLA_KEY), bwd(GLA_KEY), bwd(GLA_VAL), bwd(2 * GLA_KEY), st, st]
        out_specs = [fwd(GLA_VAL), bwd(GLA_VAL), st, st]
        o_shape = jax.ShapeDtypeStruct((bsz, t, GLA_VAL), BF16)
        out_shape = [o_shape, o_shape, st_shape, st_shape]
    else:
        ins = [k, v, g, k, v, g, s0f, s0b]
        in_specs = [fwd(GLA_KEY), fwd(GLA_VAL), fwd(2 * GLA_KEY),
                    bwd(GLA_KEY), bwd(GLA_VAL), bwd(2 * GLA_KEY), st, st]
        out_specs = [st, st]
        out_shape = [st_shape, st_shape]
    return pl.pallas_call(
        functools.partial(_gla_kernel, tt, nt, with_out),
        grid=(bsz, nt),
        in_specs=in_specs,
        out_specs=out_specs,
        out_shape=out_shape,
        compiler_params=_params(2),
        name="gla_latent" if with_out else "gla_ctx",
    )(*ins)


def _exact_bf16_parts(x):
    hi = x.astype(BF16).astype(F32)
    r = x - hi
    mid = r.astype(BF16).astype(F32)
    lo = (r - mid).astype(BF16).astype(F32)
    return hi, mid, lo


def _first_index(values, best):
    idx = jnp.full_like(best, float(len(values) - 1))
    for i in reversed(range(len(values) - 1)):
        idx = jnp.where(values[i] >= best, float(i), idx)
    return idx


def _pick(rows, idx):
    out = rows[-1]
    for i in reversed(range(len(rows) - 1)):
        out = jnp.where(idx == float(i), rows[i], out)
    return out


def _route(logit_t, tm):
    row = lambda r: logit_t[r:r + 1, :]
    groups = [row(i) for i in range(N_GROUPS)]
    top = functools.reduce(jnp.maximum, groups)
    eg = [jnp.exp(x - top) for x in groups]
    total = functools.reduce(lambda a, b: a + b, eg)
    pg = [e / total for e in eg]
    p_g = functools.reduce(jnp.maximum, pg)
    g_idx = _first_index(pg, p_g)
    sel = [_pick([row(N_GROUPS + EXPERTS_PER_GROUP * g + j) for g in range(N_GROUPS)], g_idx)
           for j in range(EXPERTS_PER_GROUP)]
    top = functools.reduce(jnp.maximum, sel)
    ee = [jnp.exp(x - top) for x in sel]
    total = functools.reduce(lambda a, b: a + b, ee)
    pe = [e / total for e in ee]
    p1 = functools.reduce(jnp.maximum, pe)
    l1 = _first_index(pe, p1)
    pe2 = [jnp.where(l1 == float(j), -1.0, pe[j]) for j in range(EXPERTS_PER_GROUP)]
    p2 = functools.reduce(jnp.maximum, pe2)
    l2 = _first_index(pe2, p2)
    den = p1 + p2
    w1 = p1 / den * p_g
    w2 = p2 / den * p_g
    lo = jnp.minimum(l1, l2)
    hi = jnp.maximum(l1, l2)
    pair = lo * (7.0 - lo) * 0.5 + (hi - lo - 1.0)
    cls = g_idx * PAIRS_PER_GROUP + pair
    w_lo = jnp.where(l1 < l2, w1, w2)
    w_hi = jnp.where(l1 < l2, w2, w1)
    cls_id = lax.broadcasted_iota(jnp.int32, (ROUTE_W, tm), 0).astype(F32)
    onehot = jnp.where(cls_id == cls, 1.0, 0.0)
    before = _dot(onehot.astype(BF16), _tri(tm, True, strict=True))
    count = jnp.sum(onehot, axis=1, keepdims=True)
    chunks = jnp.floor((count + (CHUNK_ROWS - 1.0)) * (1.0 / CHUNK_ROWS))
    chunks = jnp.where(cls_id[:, 0:1] == float(N_CLASSES),
                       LOCAL_CHUNKS - jnp.sum(chunks, axis=0, keepdims=True), chunks)
    first_chunk = _dot(_tri(ROUTE_W, False, strict=True),
                       jnp.broadcast_to(chunks, (ROUTE_W, LANES)).astype(BF16))[:, 0:1]
    pos_row = jnp.sum(onehot * (CHUNK_ROWS * first_chunk + before), axis=0, keepdims=True)
    return pos_row, w_lo, w_hi, chunks


def _slab_sort_matrices(pos, tm, slab_axis):
    shape = (LOCAL_SLAB_ROWS, tm) if slab_axis == 0 else (tm, LOCAL_SLAB_ROWS)
    slab_row = lax.broadcasted_iota(jnp.int32, shape, slab_axis)
    sub = _mod_pow2(slab_row, SLAB_ROWS)
    token_row = CHUNK_ROWS * _div_pow2(slab_row, SLAB_ROWS) + _div_pow2(sub, 2)
    hit = token_row.astype(F32) == pos
    half = _mod_pow2(sub, 2)
    return [jnp.where(hit & (half == h), 1.0, 0.0).astype(BF16) for h in range(2)]


def _mix_out_kernel(tm, x_ref, of_ref, ob_ref, sr_ref, ya_ref, mod_ref, lnp_ref, gn_ref, wo_ref, wr_ref, br_ref,
                    x1_ref, hxs_ref, pos_ref, chunks_ref):
    subs = [slice(s, s + MIX_SUB_TILE) for s in range(0, tm, MIX_SUB_TILE)]
    yb = []
    for rows in subs:
        o = of_ref[rows, :].astype(F32) + ob_ref[rows, :].astype(F32)
        sr = sr_ref[rows, :].astype(F32)
        heads = []
        for h in range(GLA_HEADS):
            sl = slice(h * GLA_DV, (h + 1) * GLA_DV)
            oh = o[:, sl]
            ms = jnp.mean(oh * oh, axis=-1, keepdims=True)
            heads.append((oh * lax.rsqrt(ms + RMS_EPS) * gn_ref[...] * sr[:, sl]).astype(BF16))
        yb.append(heads)
    xn = [_layer_norm(x_ref[rows, :], lnp_ref[0:1, :], lnp_ref[1:2, :]) for rows in subs]
    y = []
    for rows, heads in zip(subs, yb):
        acc = _dot(ya_ref[rows, :], wo_ref[0:CONV_CH, :])
        for h in range(GLA_HEADS):
            acc = acc + _dot(heads[h], wo_ref[CONV_CH + h * GLA_DV:CONV_CH + (h + 1) * GLA_DV, :])
        y.append(acc)
    h2 = []
    for rows, xn_s, y_s in zip(subs, xn, y):
        x1 = _layer_norm(DEEPNORM_ALPHA * xn_s + mod_ref[0:1, :] * y_s, lnp_ref[2:3, :], lnp_ref[3:4, :])
        x1_ref[rows, :] = x1
        h2.append(x1 * mod_ref[1:2, :] + mod_ref[2:3, :])
    logit_t = jnp.concatenate([_dot3_nt(wr_ref[...], h2_s) for h2_s in h2], axis=1) + br_ref[...]
    h2 = jnp.concatenate(h2, axis=0)
    pos_row, w_lo, w_hi, chunks = _route(logit_t, tm)
    rec_id = lax.broadcasted_iota(jnp.int32, (ROUTE_W, tm), 0)
    rec_t = jnp.zeros((ROUTE_W, tm), F32)
    for i, part in enumerate(_exact_bf16_parts(w_lo) + _exact_bf16_parts(w_hi)):
        rec_t = jnp.where(rec_id == i, part, rec_t)
    h2_b = h2.astype(BF16)
    rec_b = rec_t.T.astype(BF16)
    sort_lo, sort_hi = _slab_sort_matrices(pos_row, tm, 0)
    pay_lo = jnp.concatenate([h2_b[:, 0:HALF_W], rec_b], axis=1)
    pay_hi = jnp.concatenate([h2_b[:, HALF_W:], jnp.zeros((tm, ROUTE_W), BF16)], axis=1)
    slabs = _dot(sort_lo, pay_lo) + _dot(sort_hi, pay_hi)
    for c in range(SLAB_IN_W // LANES):
        hxs_ref[c] = slabs[:, c * LANES:(c + 1) * LANES]
    pos_ref[...] = jnp.broadcast_to(pos_row, (ROUTE_W, tm)).T
    chunks_ref[...] = jnp.broadcast_to(chunks, (ROUTE_W, LANES)).T[0:chunks_ref.shape[0], :]


def _mix_out(x, o_f, o_b, sr, ya, mod, lnp, gn, w_out_b, wr, br):
    bsz, t, _ = x.shape
    tm = SORT_TILE
    assert t % tm == 0
    n_t = t // tm
    tok = lambda w: pl.BlockSpec((None, tm, w), lambda b, i: (b, i, 0))
    full = lambda a: pl.BlockSpec(a.shape, lambda b, i: (0,) * a.ndim)
    flat = lambda rows, w: pl.BlockSpec((rows, w), lambda b, i: (b * n_t + i, 0))
    return pl.pallas_call(
        functools.partial(_mix_out_kernel, tm),
        grid=(bsz, n_t),
        in_specs=[
            tok(D_MODEL), tok(GLA_VAL), tok(GLA_VAL), tok(GLA_VAL), tok(CONV_CH),
            pl.BlockSpec((None, 3, D_MODEL), lambda b, i: (b, 0, 0)),
            full(lnp), full(gn), full(w_out_b), full(wr), full(br),
        ],
        out_specs=[tok(D_MODEL),
                   pl.BlockSpec((SLAB_IN_W // LANES, LOCAL_SLAB_ROWS, LANES), lambda b, i: (0, b * n_t + i, 0)),
                   flat(tm, ROUTE_W), flat(8, ROUTE_W)],
        out_shape=[
            jax.ShapeDtypeStruct((bsz, t, D_MODEL), F32),
            jax.ShapeDtypeStruct((SLAB_IN_W // LANES, bsz * n_t * LOCAL_SLAB_ROWS, LANES), F32),
            jax.ShapeDtypeStruct((bsz * t, ROUTE_W), F32),
            jax.ShapeDtypeStruct((bsz * n_t * 8, ROUTE_W), F32),
        ],
        compiler_params=_params(2),
        name="mix_out",
    )(x, o_f, o_b, sr, ya, mod, lnp, gn, w_out_b, wr, br)


def _moe_kernel(n_chunks, nused_ref, lo_ref, hi_ref, live_ref, src_ref,
                hxs_hbm, w13l_ref, w2l_ref, w13h_ref, w2h_ref, out_hbm, gbuf, obuf, gsem, ssem):
    tile_rows = CHUNKS_PER_TILE * SLAB_ROWS
    i = pl.program_id(0)
    n_used = nused_ref[0]
    slot = lax.bitwise_and(i, 1)

    def slab(chunk):
        return pl.ds(pl.multiple_of(chunk * SLAB_ROWS, SLAB_ROWS), SLAB_ROWS)

    def gather_copy(tile, buf_slot, j):
        chunk = jnp.maximum(src_ref[tile * CHUNKS_PER_TILE + j], 0)
        return pltpu.make_async_copy(hxs_hbm.at[:, slab(chunk), :], gbuf.at[buf_slot, :, slab(j), :],
                                     gsem.at[buf_slot])

    def scatter_copy(tile, buf_slot, j):
        chunk = src_ref[tile * CHUNKS_PER_TILE + j]
        chunk = jnp.where(chunk < 0, n_chunks + buf_slot * CHUNKS_PER_TILE + j, chunk)
        return pltpu.make_async_copy(obuf.at[buf_slot, :, slab(j), :], out_hbm.at[:, slab(chunk), :],
                                     ssem.at[buf_slot])

    def start_gather(tile, buf_slot):
        for j in range(CHUNKS_PER_TILE):
            gather_copy(tile, buf_slot, j).start(priority=j % 2)

    def wait_gather(buf_slot):
        pltpu.make_async_copy(hxs_hbm.at[:, pl.ds(0, tile_rows), :], gbuf.at[buf_slot], gsem.at[buf_slot]).wait()

    def wait_scatter(buf_slot):
        pltpu.make_async_copy(obuf.at[buf_slot], out_hbm.at[:, pl.ds(0, tile_rows), :], ssem.at[buf_slot]).wait()

    @pl.when(i == 0)
    def _():
        start_gather(0, 0)
        obuf[...] = jnp.zeros(obuf.shape, F32)
        for s in range(2):
            fill = pltpu.make_async_copy(
                obuf.at[s], out_hbm.at[:, pl.ds((n_chunks + s * CHUNKS_PER_TILE) * SLAB_ROWS, tile_rows), :],
                ssem.at[s])
            fill.start()
            fill.wait()

    @pl.when(i + 1 < n_used)
    def _():
        start_gather(i + 1, 1 - slot)

    @pl.when(i < n_used)
    def _():
        wait_gather(slot)

        @pl.when(i >= 2)
        def _():
            wait_scatter(slot)

        @pl.when(live_ref[i] != 0)
        def _():
            def lane_block(c, half):
                return jnp.concatenate(
                    [gbuf[slot, c, pl.ds(2 * r + half, CHUNKS_PER_TILE, stride=SLAB_ROWS), :]
                     for r in range(CHUNK_ROWS)], axis=0)

            n_blk = HALF_W // LANES
            xb = jnp.concatenate([lane_block(c, 0) for c in range(n_blk)]
                                 + [lane_block(c, 1) for c in range(n_blk)], axis=1).astype(BF16)
            rec = lane_block(n_blk, 0)
            w_lo = rec[:, 0:1] + rec[:, 1:2] + rec[:, 2:3]
            w_hi = rec[:, 3:4] + rec[:, 4:5] + rec[:, 5:6]

            h13 = [_dot(xb, w13_ref[...]) for w13_ref in (w13l_ref, w13h_ref)]
            act = [(_silu(h[:, 0:D_EXPERT]) * h[:, D_EXPERT:]).astype(BF16) for h in h13]
            e_lo, e_hi = [_dot(a, w2_ref[...]) for a, w2_ref in zip(act, (w2l_ref, w2h_ref))]
            y = w_lo * e_lo + w_hi * e_hi
            y = y.astype(BF16).astype(F32)
            for r in range(CHUNK_ROWS):
                rows = slice(r * CHUNKS_PER_TILE, (r + 1) * CHUNKS_PER_TILE)
                for half in range(2):
                    for c in range(n_blk):
                        col = half * HALF_W + c * LANES
                        obuf[slot, c, pl.ds(2 * r + half, CHUNKS_PER_TILE, stride=SLAB_ROWS), :] = (
                            y[rows, col:col + LANES])

        @pl.when(live_ref[i] == 0)
        def _():
            obuf[slot] = jnp.zeros(obuf.shape[1:], F32)

        for j in range(CHUNKS_PER_TILE):
            scatter_copy(i, slot, j).start(priority=j % 2)

        @pl.when(i == n_used - 1)
        def _():
            wait_scatter(slot)

            @pl.when(i >= 1)
            def _():
                wait_scatter(1 - slot)


def _moe(hxs, src, n_used, tile_lo, tile_hi, tile_live, w13_b, w2_b):
    n_chunks = hxs.shape[1] // SLAB_ROWS
    tile_rows = CHUNKS_PER_TILE * SLAB_ROWS
    n_steps = src.shape[0] // CHUNKS_PER_TILE
    wspec = lambda which, shape: pl.BlockSpec(
        (None,) + shape, (lambda i, nu, lo, hi, lv, s: (lo[i], 0, 0)) if which == 0 else
        (lambda i, nu, lo, hi, lv, s: (hi[i], 0, 0)))
    grid_spec = pltpu.PrefetchScalarGridSpec(
        num_scalar_prefetch=5,
        grid=(n_steps,),
        in_specs=[
            pl.BlockSpec(memory_space=pl.ANY),
            wspec(0, (D_MODEL, 2 * D_EXPERT)), wspec(0, (D_EXPERT, D_MODEL)),
            wspec(1, (D_MODEL, 2 * D_EXPERT)), wspec(1, (D_EXPERT, D_MODEL)),
        ],
        out_specs=pl.BlockSpec(memory_space=pl.ANY),
        scratch_shapes=[
            pltpu.VMEM((2, SLAB_IN_W // LANES, tile_rows, LANES), F32),
            pltpu.VMEM((2, SLAB_OUT_W // LANES, tile_rows, LANES), F32),
            pltpu.SemaphoreType.DMA((2,)),
            pltpu.SemaphoreType.DMA((2,)),
        ],
    )
    return pl.pallas_call(
        functools.partial(_moe_kernel, n_chunks),
        grid_spec=grid_spec,
        out_shape=jax.ShapeDtypeStruct((SLAB_OUT_W // LANES, (n_chunks + 2 * CHUNKS_PER_TILE) * SLAB_ROWS, LANES), F32),
        compiler_params=_params(1),
        name="moe",
    )(n_used, tile_lo, tile_hi, tile_live, src, hxs, w13_b, w2_b, w13_b, w2_b)


def _final_kernel(tm, x1_ref, moe_ref, pos_ref, mod_ref, lnp_ref, o_ref):
    moe_b = jnp.concatenate([moe_ref[c] for c in range(SLAB_OUT_W // LANES)], axis=1).astype(BF16)
    sort_lo, sort_hi = _slab_sort_matrices(pos_ref[:, 0:1], tm, 1)
    subs = [slice(s, s + MIX_SUB_TILE) for s in range(0, tm, MIX_SUB_TILE)]
    moe = [jnp.concatenate([_dot(sort_lo[rows, :], moe_b), _dot(sort_hi[rows, :], moe_b)], axis=1) for rows in subs]
    for rows, moe_s in zip(subs, moe):
        o_ref[rows, :] = _layer_norm(DEEPNORM_ALPHA * x1_ref[rows, :] + mod_ref[...] * moe_s,
                                     lnp_ref[0:1, :], lnp_ref[1:2, :])


def _final(x1, moe, pos, g2, lnp):
    bsz, t, _ = x1.shape
    tm = SORT_TILE
    n_t = t // tm
    flat = lambda rows, w: pl.BlockSpec((rows, w), lambda b, i: (b * n_t + i, 0))
    return pl.pallas_call(
        functools.partial(_final_kernel, tm),
        grid=(bsz, n_t),
        in_specs=[
            pl.BlockSpec((None, tm, D_MODEL), lambda b, i: (b, i, 0)),
            pl.BlockSpec((SLAB_OUT_W // LANES, LOCAL_SLAB_ROWS, LANES), lambda b, i: (0, b * n_t + i, 0)),
            flat(tm, ROUTE_W),
            pl.BlockSpec((None, 1, D_MODEL), lambda b, i: (b, 0, 0)),
            pl.BlockSpec(lnp.shape, lambda b, i: (0, 0)),
        ],
        out_specs=pl.BlockSpec((None, tm, D_MODEL), lambda b, i: (b, i, 0)),
        out_shape=jax.ShapeDtypeStruct((bsz, t, D_MODEL), F32),
        compiler_params=_params(2),
        name="final",
    )(x1, moe, pos, g2, lnp)


def _pair_tables():
    lo, hi = [], []
    for g in range(N_GROUPS):
        for a in range(EXPERTS_PER_GROUP):
            for b in range(a + 1, EXPERTS_PER_GROUP):
                lo.append(g * EXPERTS_PER_GROUP + a)
                hi.append(g * EXPERTS_PER_GROUP + b)
    return jnp.array(lo, jnp.int32), jnp.array(hi, jnp.int32)


def _moe_plan(chunks, n_sort_tiles):
    n_cls = N_CLASSES + 1
    hp = lax.Precision.HIGHEST
    m = chunks.reshape(n_sort_tiles, 8, ROUTE_W)[:, 0, :n_cls].astype(jnp.int32)
    a_end = jnp.cumsum(m, axis=0)
    a_start = a_end - m
    per_cls = a_end[-1]
    padded = (per_cls + CHUNKS_PER_TILE - 1) // CHUNKS_PER_TILE * CHUNKS_PER_TILE
    g_end = jnp.cumsum(padded)
    g_start = g_end - padded
    local_off = jnp.cumsum(m, axis=1) - m
    seg = jnp.arange(n_sort_tiles, dtype=jnp.int32)[:, None] * LOCAL_CHUNKS + local_off - a_start
    n_steps = -(-(n_sort_tiles * LOCAL_CHUNKS) // CHUNKS_PER_TILE) + n_cls
    p = jnp.arange(n_steps * CHUNKS_PER_TILE, dtype=jnp.int32)
    cls_p = jnp.minimum(jnp.sum((g_end[None, :] <= p[:, None]).astype(jnp.int32), axis=1), n_cls - 1)
    onehot = (cls_p[:, None] == jnp.arange(n_cls, dtype=jnp.int32)[None, :]).astype(F32)
    pick = lambda tab: jnp.dot(onehot, tab.astype(F32), precision=hp)
    u = p - pick(g_start[:, None])[:, 0].astype(jnp.int32)
    valid = u < pick(per_cls[:, None])[:, 0].astype(jnp.int32)
    a_end_p = pick(a_end.T).astype(jnp.int32)
    seg_p = pick(seg.T).astype(jnp.int32)
    tile_p = jnp.sum((a_end_p <= u[:, None]).astype(jnp.int32), axis=1)
    hit = jnp.arange(n_sort_tiles, dtype=jnp.int32)[None, :] == tile_p[:, None]
    src = jnp.sum(jnp.where(hit, seg_p, 0), axis=1) + u
    src = jnp.where(valid, src, -1).astype(jnp.int32)
    n_used = g_end[-1:] // CHUNKS_PER_TILE
    step = jnp.arange(n_steps, dtype=jnp.int32)
    tile_cls = jnp.sum((g_end[None, :] // CHUNKS_PER_TILE <= step[:, None]).astype(jnp.int32), axis=1)
    live = ((tile_cls < N_CLASSES) & (step < n_used[0])).astype(jnp.int32)
    pair_lo, pair_hi = _pair_tables()
    pair_oh = (jnp.minimum(tile_cls, N_CLASSES - 1)[:, None] == jnp.arange(N_CLASSES)[None, :]).astype(jnp.int32)
    tile_lo = jnp.sum(pair_oh * pair_lo[None, :], axis=1).astype(jnp.int32)
    tile_hi = jnp.sum(pair_oh * pair_hi[None, :], axis=1).astype(jnp.int32)
    return src, n_used.astype(jnp.int32), tile_lo, tile_hi, live


def kernel(x, c, ctx, c_ctx, ln_in_g, ln_in_b, w_ada, b_ada, w_in, conv_w, conv_b, gate_w2_fwd, gate_b_fwd,
           gate_w2_bwd, gate_b_bwd, gla_norm_g, w_out, ln1_g, ln1_b, router_group_w, router_group_b,
           router_expert_w, router_expert_b, expert_w1, expert_w3, expert_w2, ln2_g, ln2_b):
    bsz, t, _ = x.shape
    n_tok = bsz * t
    l = 0
    rows = -(-(bsz + 1) // 8) * 8
    cond = jnp.zeros((rows, D_MODEL), F32).at[:bsz].set(c).at[bsz].set(c_ctx)
    ada = _ada(cond, w_ada[l], b_ada[l][None, :])
    sh1, sc1, g1, sh2, sc2, g2 = [ada[:, i * D_MODEL:(i + 1) * D_MODEL] for i in range(6)]

    w_in_b = w_in[l].astype(BF16)
    lnp_in = jnp.stack([ln_in_g, ln_in_b])
    zero = jnp.zeros((GLA_GATE_RANK, GLA_KEY), F32)
    w2cat = jnp.concatenate([jnp.concatenate([gate_w2_fwd[l], zero], axis=1),
                             jnp.concatenate([zero, gate_w2_bwd[l]], axis=1)], axis=0).astype(BF16)
    gbias = jnp.concatenate([gate_b_fwd[l], gate_b_bwd[l]])[None, :]

    mod_ctx = jnp.broadcast_to(jnp.stack([1.0 + sc1[bsz], sh1[bsz]])[None], (bsz, 2, D_MODEL))
    k_c, v_c, g_c = _proj(ctx, mod_ctx, lnp_in, w_in_b, conv_w[l], conv_b[l][None, :], w2cat, gbias, False)
    zero_state = jnp.zeros((bsz, GLA_KEY, PAIR_VAL), F32)
    s_f, s_b = _gla(None, k_c, v_c, g_c, zero_state, zero_state)

    mod1 = jnp.stack([1.0 + sc1[:bsz], sh1[:bsz]], axis=1)
    ya, q, k, v, sr, g = _proj(x, mod1, lnp_in, w_in_b, conv_w[l], conv_b[l][None, :], w2cat, gbias, True)
    o_f, o_b, _, _ = _gla(q, k, v, g, s_f, s_b)

    mod2 = jnp.stack([g1[:bsz], 1.0 + sc2[:bsz], sh2[:bsz]], axis=1)
    lnp1 = jnp.stack([ln_in_g, ln_in_b, ln1_g[l], ln1_b[l]])
    wr = jnp.zeros((ROUTE_W, D_MODEL), F32)
    wr = wr.at[:N_GROUPS].set(router_group_w[l].T).at[N_GROUPS:N_GROUPS + N_EXPERTS].set(router_expert_w[l].T)
    br = jnp.zeros((ROUTE_W, 1), F32)
    br = br.at[:N_GROUPS, 0].set(router_group_b[l]).at[N_GROUPS:N_GROUPS + N_EXPERTS, 0].set(router_expert_b[l])
    x1, hxs, pos, chunks = _mix_out(x, o_f, o_b, sr, ya, mod2, lnp1, gla_norm_g[l][None, :],
                                    w_out[l].astype(BF16), wr, br)

    src, n_used, tile_lo, tile_hi, live = _moe_plan(chunks, n_tok // SORT_TILE)
    w13_b = jnp.concatenate([expert_w1[l], expert_w3[l]], axis=-1).astype(BF16)
    moe = _moe(hxs, src, n_used, tile_lo, tile_hi, live, w13_b, expert_w2[l].astype(BF16))

    return _final(x1, moe, pos, g2[:bsz][:, None, :], jnp.stack([ln2_g[l], ln2_b[l]]))
```

```python
import functools

import jax
import jax.numpy as jnp
from jax import lax
from jax.experimental import pallas as pl
from jax.experimental.pallas import tpu as pltpu

F32 = jnp.float32
BF16 = jnp.bfloat16

D_MODEL = 1024
GRID_W = 64
CONV_CH = 512
GLA_HEADS = 4
GLA_DK = 64
GLA_DV = 128
GLA_KEY = GLA_HEADS * GLA_DK
GLA_VAL = GLA_HEADS * GLA_DV
PAIR_KEY = 2 * GLA_DK
PAIR_VAL = 2 * GLA_DV
GLA_GATE_RANK = 16
GLA_TAU = 16.0
OFF_AB = 0
OFF_AC = OFF_AB + CONV_CH
OFF_AX = OFF_AC + CONV_CH
OFF_Q = OFF_AX + CONV_CH
OFF_K = OFF_Q + GLA_KEY
OFF_V = OFF_K + GLA_KEY
OFF_R = OFF_V + GLA_VAL
OFF_GF = OFF_R + GLA_VAL
D_PROJ = OFF_GF + 2 * GLA_GATE_RANK
N_GROUPS = 4
EXPERTS_PER_GROUP = 4
N_EXPERTS = N_GROUPS * EXPERTS_PER_GROUP
D_EXPERT = 512
PAIRS_PER_GROUP = 6
N_CLASSES = N_GROUPS * PAIRS_PER_GROUP
LN_EPS = 1e-5
RMS_EPS = 1e-6
DEPTH = 1
DEEPNORM_ALPHA = (2.0 * DEPTH) ** 0.25

LANES = 128
GLA_CHUNK = 64
GLA_SUB = 16
N_SUB = GLA_CHUNK // GLA_SUB
ROUTE_W = LANES
HALF_W = D_MODEL // 2
SLAB_IN_W = HALF_W + ROUTE_W
SLAB_OUT_W = HALF_W
SORT_TILE = 256
MIX_SUB_TILE = 128
MOE_TILE = 256
CHUNK_ROWS = 4
SLAB_ROWS = 2 * CHUNK_ROWS
LOCAL_CHUNKS = -(-(SORT_TILE + N_CLASSES * (CHUNK_ROWS - 1)) // (8 * CHUNK_ROWS)) * 8
LOCAL_SLAB_ROWS = LOCAL_CHUNKS * SLAB_ROWS
CHUNKS_PER_TILE = MOE_TILE // CHUNK_ROWS
VMEM_LIMIT = 56 * 1024 * 1024


def _params(n_axes, vmem=VMEM_LIMIT):
    return pltpu.CompilerParams(dimension_semantics=("arbitrary",) * n_axes, vmem_limit_bytes=vmem)


def _dot(a, b):
    return jnp.dot(a, b, preferred_element_type=F32)


def _div_pow2(x, d):
    assert d & (d - 1) == 0
    return lax.shift_right_logical(x, jnp.int32(d.bit_length() - 1))


def _mod_pow2(x, d):
    assert d & (d - 1) == 0
    return lax.bitwise_and(x, jnp.int32(d - 1))


def _split2(x):
    hi = x.astype(BF16)
    lo = (x - hi.astype(F32)).astype(BF16)
    return hi, lo


def _dot3(a, b):
    ah, al = _split2(a)
    bh, bl = _split2(b)
    return _dot(ah, bh) + _dot(ah, bl) + _dot(al, bh)


def _silu(x):
    return x * (0.5 * jnp.tanh(0.5 * x) + 0.5)


def _layer_norm(x, g, b):
    mu = jnp.mean(x, axis=-1, keepdims=True)
    xc = x - mu
    var = jnp.mean(xc * xc, axis=-1, keepdims=True)
    return xc * lax.rsqrt(var + LN_EPS) * g + b


def _ada_kernel(c_ref, w_ref, b_ref, o_ref):
    o_ref[...] = _dot3(_silu(c_ref[...]), w_ref[...]) + b_ref[...]


def _ada(cond, w_ada, b_ada):
    rows = cond.shape[0]
    n_out = w_ada.shape[1]
    tn = 1024
    return pl.pallas_call(
        _ada_kernel,
        grid=(n_out // tn,),
        in_specs=[
            pl.BlockSpec((rows, D_MODEL), lambda j: (0, 0)),
            pl.BlockSpec((D_MODEL, tn), lambda j: (0, j)),
            pl.BlockSpec((1, tn), lambda j: (0, j)),
        ],
        out_specs=pl.BlockSpec((rows, tn), lambda j: (0, j)),
        out_shape=jax.ShapeDtypeStruct((rows, n_out), F32),
        compiler_params=_params(1),
        name="ada",
    )(cond, w_ada, b_ada)


def _log_sigmoid(z):
    return jnp.minimum(z, 0.0) - jnp.log(1.0 + jnp.exp(-jnp.abs(z)))


def _proj_kernel(latent, tm, x_ref, mod_ref, lnp_ref, w_ref, cw_ref, cb_ref, w2_ref, gbias_ref, *out_refs):
    x = x_ref[...]
    xn = _layer_norm(x, lnp_ref[0:1, :], lnp_ref[1:2, :])
    h = xn * mod_ref[0:1, :] + mod_ref[1:2, :]
    hb = h.astype(BF16)
    if latent:
        ya_ref, q_ref, k_ref, v_ref, sr_ref, g_ref = out_refs
        p = _dot(hb, w_ref[:, OFF_AB:OFF_Q])
        a_b = p[:, 0:CONV_CH]
        u = p[:, CONV_CH:2 * CONV_CH] * p[:, 2 * CONV_CH:3 * CONV_CH]
        pos = _mod_pow2(lax.broadcasted_iota(jnp.int32, (tm, 1), 0), GRID_W)
        u_prev = jnp.where(pos == 0, 0.0, pltpu.roll(u, 1, 0))
        u_next = jnp.where(pos == GRID_W - 1, 0.0, pltpu.roll(u, tm - 1, 0))
        conv = u_prev * cw_ref[0:1, :] + u * cw_ref[1:2, :] + u_next * cw_ref[2:3, :] + cb_ref[...]
        ya_ref[...] = (a_b * conv).astype(BF16)
        qk = _dot(hb, w_ref[:, OFF_Q:OFF_V])
        q_ref[...] = (qk[:, 0:GLA_KEY] * (GLA_DK ** -0.5)).astype(BF16)
        k_ref[...] = qk[:, GLA_KEY:].astype(BF16)
        r = _dot(hb, w_ref[:, OFF_R:OFF_GF])
        sr_ref[...] = _silu(r).astype(BF16)
    else:
        k_ref, v_ref, g_ref = out_refs
        k_ref[...] = _dot(hb, w_ref[:, OFF_K:OFF_V]).astype(BF16)
    v_ref[...] = _dot(hb, w_ref[:, OFF_V:OFF_R]).astype(BF16)
    low = _dot(hb, w_ref[:, OFF_GF:D_PROJ])
    z = _dot(low.astype(BF16), w2_ref[...]) + gbias_ref[...]
    g_ref[...] = _log_sigmoid(z) * (1.0 / GLA_TAU)


def _proj(x, mod, lnp, w_in_b, conv_w, conv_b, w2cat, gbias, latent):
    bsz, t, _ = x.shape
    tm = min(512, t)
    assert t % tm == 0 and tm % GRID_W == 0
    tok = lambda w: pl.BlockSpec((None, tm, w), lambda b, i: (b, i, 0))
    full = lambda a: pl.BlockSpec(a.shape, lambda b, i: (0,) * a.ndim)
    widths = ([(CONV_CH, BF16), (GLA_KEY, BF16)] if latent else []) + [(GLA_KEY, BF16), (GLA_VAL, BF16)]
    widths += ([(GLA_VAL, BF16)] if latent else []) + [(2 * GLA_KEY, F32)]
    return pl.pallas_call(
        functools.partial(_proj_kernel, latent, tm),
        grid=(bsz, t // tm),
        in_specs=[
            tok(D_MODEL),
            pl.BlockSpec((None, 2, D_MODEL), lambda b, i: (b, 0, 0)),
            full(lnp), full(w_in_b), full(conv_w), full(conv_b), full(w2cat), full(gbias),
        ],
        out_specs=[tok(w) for w, _ in widths],
        out_shape=[jax.ShapeDtypeStruct((bsz, t, w), dt) for w, dt in widths],
        compiler_params=_params(2),
        name="proj_latent" if latent else "proj_ctx",
    )(x, mod, lnp, w_in_b, conv_w, conv_b, w2cat, gbias)


def _tri(n, reverse, strict=False):
    i = lax.broadcasted_iota(jnp.int32, (n, n), 0)
    j = lax.broadcasted_iota(jnp.int32, (n, n), 1)
    if strict:
        m = (j > i) if reverse else (j < i)
    else:
        m = (j >= i) if reverse else (j <= i)
    return jnp.where(m, 1.0, 0.0).astype(BF16)


def _chunk_cumsum(g, reverse):
    tri = _tri(GLA_CHUNK, reverse)
    g_hi, g_lo = _split2(g)
    return _dot(tri, g_hi) + _dot(tri, g_lo)


def _as_column(row):
    return jnp.broadcast_to(row, (LANES, row.shape[1])).T


def _sub_anchors(gc, reverse):
    zero = jnp.zeros((1, GLA_KEY), F32)
    if reverse:
        return [gc[GLA_SUB * (a + 1):GLA_SUB * (a + 1) + 1] for a in range(N_SUB - 1)] + [zero]
    return [zero] + [gc[GLA_SUB * a - 1:GLA_SUB * a] for a in range(1, N_SUB)]


def _score_pairs(reverse):
    return [(a, b) for a in range(N_SUB) for b in range(N_SUB) if (b >= a if reverse else b <= a)]


def _intra_products(q, k, gc, reverse):
    r = _sub_anchors(gc, reverse)
    anchor = jnp.concatenate([jnp.broadcast_to(ra, (GLA_SUB, GLA_KEY)) for ra in r], axis=0)
    gcb = gc - anchor
    qt = q * jnp.exp(gcb)
    kt = k * jnp.exp(-gcb)
    rows = []
    for a, b in _score_pairs(reverse):
        qa = qt[GLA_SUB * a:GLA_SUB * (a + 1)]
        if a != b:
            qa = qa * jnp.exp(r[a] - r[b])
        rows.append(qa)
    qp = jnp.concatenate(rows, axis=0).astype(BF16)
    width = GLA_HEADS * GLA_CHUNK
    rr = lax.broadcasted_iota(jnp.int32, (width, GLA_KEY), 0)
    cc = lax.broadcasted_iota(jnp.int32, (width, GLA_KEY), 1)
    kbd = jnp.where(_div_pow2(rr, GLA_CHUNK) == _div_pow2(cc, GLA_DK),
                    jnp.concatenate([kt] * GLA_HEADS, axis=0), 0.0)
    return lax.dot_general(qp, kbd.astype(BF16), (((1,), (1,)), ((), ())), preferred_element_type=F32)


def _assemble_scores(res, reverse):
    pairs = _score_pairs(reverse)
    width = GLA_HEADS * GLA_CHUNK
    col = _mod_pow2(lax.broadcasted_iota(jnp.int32, (GLA_SUB, width), 1), GLA_CHUNK)
    col_blk = _div_pow2(col, GLA_SUB)
    col_pos = _mod_pow2(col, GLA_SUB)
    row_pos = lax.broadcasted_iota(jnp.int32, (GLA_SUB, width), 0)
    causal = (col_pos >= row_pos) if reverse else (col_pos <= row_pos)
    blocks = []
    for a in range(N_SUB):
        acc = jnp.zeros((GLA_SUB, width), F32)
        for idx, (pa, pb) in enumerate(pairs):
            if pa != a:
                continue
            keep = col_blk == pb
            if pa == pb:
                keep = keep & causal
            acc = acc + jnp.where(keep, res[GLA_SUB * idx:GLA_SUB * (idx + 1)], 0.0)
        blocks.append(acc)
    return jnp.concatenate(blocks, axis=0)


def _pair_mask(rows_per_head, cols_per_head, n_row_pairs=1):
    shape = (n_row_pairs * 2 * rows_per_head, 2 * cols_per_head)
    rr = _mod_pow2(lax.broadcasted_iota(jnp.int32, shape, 0), 2 * rows_per_head)
    cc = lax.broadcasted_iota(jnp.int32, shape, 1)
    return _div_pow2(rr, rows_per_head) == _div_pow2(cc, cols_per_head)


def _state_terms(q, k, v_b, gc, reverse):
    total = gc[0:1] if reverse else gc[GLA_CHUNK - 1:GLA_CHUNK]
    q_dec = None if q is None else (q * jnp.exp(gc)).astype(BF16)
    k_end = (k * jnp.exp(total - gc)).astype(BF16)
    tn = (((0,), (0,)), ((), ()))
    upd = [lax.dot_general(k_end[:, p * PAIR_KEY:(p + 1) * PAIR_KEY], v_b[:, p * PAIR_VAL:(p + 1) * PAIR_VAL], tn,
                           preferred_element_type=F32) for p in range(GLA_HEADS // 2)]
    upd = jnp.where(_pair_mask(GLA_DK, GLA_DV, GLA_HEADS // 2), jnp.concatenate(upd, axis=0), 0.0)
    decay = jnp.exp(_as_column(total))
    decay = jnp.concatenate([decay] * (PAIR_VAL // LANES), axis=1)
    return q_dec, decay, upd


def _advance_state(q_dec, decay, upd, state):
    o_inter = None
    if q_dec is not None:
        state_b = state.astype(BF16)
        o_inter = jnp.concatenate(
            [_dot(q_dec[:, p * PAIR_KEY:(p + 1) * PAIR_KEY], state_b[p * PAIR_KEY:(p + 1) * PAIR_KEY, :])
             for p in range(GLA_HEADS // 2)], axis=1)
    return o_inter, state * decay + upd


def _gla_kernel(tt, nt, with_out, *refs):
    if with_out:
        (qf_ref, kf_ref, vf_ref, gfw_ref, qb_ref, kb_ref, vb_ref, gbw_ref, s0f_ref, s0b_ref,
         of_ref, ob_ref, sf_ref, sb_ref) = refs
    else:
        kf_ref, vf_ref, gfw_ref, kb_ref, vb_ref, gbw_ref, s0f_ref, s0b_ref, sf_ref, sb_ref = refs
    j = pl.program_id(1)

    @pl.when(j == 0)
    def _():
        sf_ref[...] = s0f_ref[...]
        sb_ref[...] = s0b_ref[...]

    chunk_slices = [slice(c * GLA_CHUNK, (c + 1) * GLA_CHUNK) for c in range(tt // GLA_CHUNK)]
    fwd, bwd = [], []
    for sl in chunk_slices:
        g = gfw_ref[sl, :]
        fwd.append(dict(sl=sl, k=kf_ref[sl, :].astype(F32), v=vf_ref[sl, :], g_f=g[:, 0:GLA_KEY], g_b=g[:, GLA_KEY:],
                        q=qf_ref[sl, :].astype(F32) if with_out else None))
        bwd.append(dict(sl=sl, k=kb_ref[sl, :].astype(F32), v=vb_ref[sl, :], g_b=gbw_ref[sl, :][:, GLA_KEY:],
                        q=qb_ref[sl, :].astype(F32) if with_out else None))
    for d in fwd:
        d["gc_f"] = _chunk_cumsum(d["g_f"], False)
        if with_out:
            d["gc_b"] = _chunk_cumsum(d["g_b"], True)
    for d in bwd:
        d["gc_b"] = _chunk_cumsum(d["g_b"], True)
    if with_out:
        for d in fwd:
            d["res_f"] = _intra_products(d["q"], d["k"], d["gc_f"], False)
            d["res_b"] = _intra_products(d["q"], d["k"], d["gc_b"], True)
    for d in fwd:
        d["terms"] = _state_terms(d["q"], d["k"], d["v"], d["gc_f"], False)
    for d in bwd:
        d["terms"] = _state_terms(d["q"], d["k"], d["v"], d["gc_b"], True)
    if with_out:
        for d in fwd:
            scores = (_assemble_scores(d["res_f"], False) + _assemble_scores(d["res_b"], True)).astype(BF16)
            o_intra = []
            for p in range(GLA_HEADS // 2):
                v_p = d["v"][:, p * PAIR_VAL:(p + 1) * PAIR_VAL]
                vbd = jnp.where(_pair_mask(GLA_CHUNK, GLA_DV), jnp.concatenate([v_p, v_p], axis=0),
                                jnp.zeros((), BF16))
                o_intra.append(_dot(scores[:, p * 2 * GLA_CHUNK:(p + 1) * 2 * GLA_CHUNK], vbd))
            d["o_intra"] = jnp.concatenate(o_intra, axis=1)

    state = sf_ref[...]
    for d in fwd:
        o_inter, state = _advance_state(*d["terms"], state)
        if with_out:
            of_ref[d["sl"], :] = (d["o_intra"] + o_inter).astype(BF16)
    sf_ref[...] = state

    state = sb_ref[...]
    for d in reversed(bwd):
        o_inter, state = _advance_state(*d["terms"], state)
        if with_out:
            ob_ref[d["sl"], :] = o_inter.astype(BF16)
    sb_ref[...] = state


def _gla(q, k, v, g, s0f, s0b):
    with_out = q is not None
    bsz, t, _ = k.shape
    tt = min(256, t)
    assert t % tt == 0 and tt % GLA_CHUNK == 0
    nt = t // tt
    fwd = lambda w: pl.BlockSpec((None, tt, w), lambda b, j: (b, j, 0))
    bwd = lambda w: pl.BlockSpec((None, tt, w), lambda b, j: (b, nt - 1 - j, 0))
    st = pl.BlockSpec((None, GLA_KEY, PAIR_VAL), lambda b, j: (b, 0, 0))
    st_shape = jax.ShapeDtypeStruct((bsz, GLA_KEY, PAIR_VAL), F32)
    if with_out:
        ins = [q, k, v, g, q, k, v, g, s0f, s0b]
        in_specs = [fwd(GLA_KEY), fwd(GLA_KEY), fwd(GLA_VAL), fwd(2 * GLA_KEY),
                    bwd(GLA_KEY), bwd(GLA_KEY), bwd(GLA_VAL), bwd(2 * GLA_KEY), st, st]
        out_specs = [fwd(GLA_VAL), bwd(GLA_VAL), st, st]
        o_shape = jax.ShapeDtypeStruct((bsz, t, GLA_VAL), BF16)
        out_shape = [o_shape, o_shape, st_shape, st_shape]
    else:
        ins = [k, v, g, k, v, g, s0f, s0b]
        in_specs = [fwd(GLA_KEY), fwd(GLA_VAL), fwd(2 * GLA_KEY),
                    bwd(GLA_KEY), bwd(GLA_VAL), bwd(2 * GLA_KEY), st, st]
        out_specs = [st, st]
        out_shape = [st_shape, st_shape]
    return pl.pallas_call(
        functools.partial(_gla_kernel, tt, nt, with_out),
        grid=(bsz, nt),
        in_specs=in_specs,
        out_specs=out_specs,
        out_shape=out_shape,
        compiler_params=_params(2),
        name="gla_latent" if with_out else "gla_ctx",
    )(*ins)


def _exact_bf16_parts(x):
    hi = x.astype(BF16).astype(F32)
    r = x - hi
    mid = r.astype(BF16).astype(F32)
    lo = (r - mid).astype(BF16).astype(F32)
    return hi, mid, lo


def _first_index(values, best):
    idx = jnp.full_like(best, float(len(values) - 1))
    for i in reversed(range(len(values) - 1)):
        idx = jnp.where(values[i] >= best, float(i), idx)
    return idx


def _pick(rows, idx):
    out = rows[-1]
    for i in reversed(range(len(rows) - 1)):
        out = jnp.where(idx == float(i), rows[i], out)
    return out


def _route(logit_t, tm):
    row = lambda r: logit_t[r:r + 1, :]
    groups = [row(i) for i in range(N_GROUPS)]
    top = functools.reduce(jnp.maximum, groups)
    eg = [jnp.exp(x - top) for x in groups]
    total = functools.reduce(lambda a, b: a + b, eg)
    pg = [e / total for e in eg]
    p_g = functools.reduce(jnp.maximum, pg)
    g_idx = _first_index(pg, p_g)
    sel = [_pick([row(N_GROUPS + EXPERTS_PER_GROUP * g + j) for g in range(N_GROUPS)], g_idx)
           for j in range(EXPERTS_PER_GROUP)]
    top = functools.reduce(jnp.maximum, sel)
    ee = [jnp.exp(x - top) for x in sel]
    total = functools.reduce(lambda a, b: a + b, ee)
    pe = [e / total for e in ee]
    p1 = functools.reduce(jnp.maximum, pe)
    l1 = _first_index(pe, p1)
    pe2 = [jnp.where(l1 == float(j), -1.0, pe[j]) for j in range(EXPERTS_PER_GROUP)]
    p2 = functools.reduce(jnp.maximum, pe2)
    l2 = _first_index(pe2, p2)
    den = p1 + p2
    w1 = p1 / den * p_g
    w2 = p2 / den * p_g
    lo = jnp.minimum(l1, l2)
    hi = jnp.maximum(l1, l2)
    pair = lo * (7.0 - lo) * 0.5 + (hi - lo - 1.0)
    cls = g_idx * PAIRS_PER_GROUP + pair
    w_lo = jnp.where(l1 < l2, w1, w2)
    w_hi = jnp.where(l1 < l2, w2, w1)
    cls_id = lax.broadcasted_iota(jnp.int32, (ROUTE_W, tm), 0).astype(F32)
    onehot = jnp.where(cls_id == cls, 1.0, 0.0)
    before = _dot(onehot.astype(BF16), _tri(tm, True, strict=True))
    count = jnp.sum(onehot, axis=1, keepdims=True)
    chunks = jnp.floor((count + (CHUNK_ROWS - 1.0)) * (1.0 / CHUNK_ROWS))
    chunks = jnp.where(cls_id[:, 0:1] == float(N_CLASSES),
                       LOCAL_CHUNKS - jnp.sum(chunks, axis=0, keepdims=True), chunks)
    first_chunk = _dot(_tri(ROUTE_W, False, strict=True),
                       jnp.broadcast_to(chunks, (ROUTE_W, LANES)).astype(BF16))[:, 0:1]
    pos_row = jnp.sum(onehot * (CHUNK_ROWS * first_chunk + before), axis=0, keepdims=True)
    return pos_row, w_lo, w_hi, chunks


def _slab_sort_matrices(pos, tm, slab_axis):
    shape = (LOCAL_SLAB_ROWS, tm) if slab_axis == 0 else (tm, LOCAL_SLAB_ROWS)
    slab_row = lax.broadcasted_iota(jnp.int32, shape, slab_axis)
    sub = _mod_pow2(slab_row, SLAB_ROWS)
    token_row = CHUNK_ROWS * _div_pow2(slab_row, SLAB_ROWS) + _div_pow2(sub, 2)
    hit = token_row.astype(F32) == pos
    half = _mod_pow2(sub, 2)
    return [jnp.where(hit & (half == h), 1.0, 0.0).astype(BF16) for h in range(2)]


def _mix_out_kernel(tm, x_ref, of_ref, ob_ref, sr_ref, ya_ref, mod_ref, lnp_ref, gn_ref, wo_ref, wr_ref, br_ref,
                    x1_ref, hxs_ref, pos_ref, chunks_ref):
    subs = [slice(s, s + MIX_SUB_TILE) for s in range(0, tm, MIX_SUB_TILE)]
    yb = []
    for rows in subs:
        o = of_ref[rows, :].astype(F32) + ob_ref[rows, :].astype(F32)
        sr = sr_ref[rows, :].astype(F32)
        heads = []
        for h in range(GLA_HEADS):
            sl = slice(h * GLA_DV, (h + 1) * GLA_DV)
            oh = o[:, sl]
            ms = jnp.mean(oh * oh, axis=-1, keepdims=True)
            heads.append((oh * lax.rsqrt(ms + RMS_EPS) * gn_ref[...] * sr[:, sl]).astype(BF16))
        yb.append(heads)
    xn = [_layer_norm(x_ref[rows, :], lnp_ref[0:1, :], lnp_ref[1:2, :]) for rows in subs]
    y = []
    for rows, heads in zip(subs, yb):
        acc = _dot(ya_ref[rows, :], wo_ref[0:CONV_CH, :])
        for h in range(GLA_HEADS):
            acc = acc + _dot(heads[h], wo_ref[CONV_CH + h * GLA_DV:CONV_CH + (h + 1) * GLA_DV, :])
        y.append(acc)
    h2 = []
    for rows, xn_s, y_s in zip(subs, xn, y):
        x1 = _layer_norm(DEEPNORM_ALPHA * xn_s + mod_ref[0:1, :] * y_s, lnp_ref[2:3, :], lnp_ref[3:4, :])
        x1_ref[rows, :] = x1
        h2.append(x1 * mod_ref[1:2, :] + mod_ref[2:3, :])
    logit_t = jnp.concatenate([(_dot3(h2_s, wr_ref[...]) + br_ref[...]).T for h2_s in h2], axis=1)
    h2 = jnp.concatenate(h2, axis=0)
    pos_row, w_lo, w_hi, chunks = _route(logit_t, tm)
    rec_id = lax.broadcasted_iota(jnp.int32, (ROUTE_W, tm), 0)
    rec_t = jnp.zeros((ROUTE_W, tm), F32)
    for i, part in enumerate(_exact_bf16_parts(w_lo) + _exact_bf16_parts(w_hi)):
        rec_t = jnp.where(rec_id == i, part, rec_t)
    h2_b = h2.astype(BF16)
    rec_b = rec_t.T.astype(BF16)
    sort_lo, sort_hi = _slab_sort_matrices(pos_row, tm, 0)
    pay_lo = jnp.concatenate([h2_b[:, 0:HALF_W], rec_b], axis=1)
    pay_hi = jnp.concatenate([h2_b[:, HALF_W:], jnp.zeros((tm, ROUTE_W), BF16)], axis=1)
    slabs = _dot(sort_lo, pay_lo) + _dot(sort_hi, pay_hi)
    for c in range(SLAB_IN_W // LANES):
        hxs_ref[c] = slabs[:, c * LANES:(c + 1) * LANES]
    pos_ref[...] = jnp.broadcast_to(pos_row, (ROUTE_W, tm)).T
    chunks_ref[...] = jnp.broadcast_to(chunks, (ROUTE_W, LANES)).T[0:chunks_ref.shape[0], :]


def _mix_out(x, o_f, o_b, sr, ya, mod, lnp, gn, w_out_b, wr, br):
    bsz, t, _ = x.shape
    tm = SORT_TILE
    assert t % tm == 0
    n_t = t // tm
    tok = lambda w: pl.BlockSpec((None, tm, w), lambda b, i: (b, i, 0))
    full = lambda a: pl.BlockSpec(a.shape, lambda b, i: (0,) * a.ndim)
    flat = lambda rows, w: pl.BlockSpec((rows, w), lambda b, i: (b * n_t + i, 0))
    return pl.pallas_call(
        functools.partial(_mix_out_kernel, tm),
        grid=(bsz, n_t),
        in_specs=[
            tok(D_MODEL), tok(GLA_VAL), tok(GLA_VAL), tok(GLA_VAL), tok(CONV_CH),
            pl.BlockSpec((None, 3, D_MODEL), lambda b, i: (b, 0, 0)),
            full(lnp), full(gn), full(w_out_b), full(wr), full(br),
        ],
        out_specs=[tok(D_MODEL),
                   pl.BlockSpec((SLAB_IN_W // LANES, LOCAL_SLAB_ROWS, LANES), lambda b, i: (0, b * n_t + i, 0)),
                   flat(tm, ROUTE_W), flat(8, ROUTE_W)],
        out_shape=[
            jax.ShapeDtypeStruct((bsz, t, D_MODEL), F32),
            jax.ShapeDtypeStruct((SLAB_IN_W // LANES, bsz * n_t * LOCAL_SLAB_ROWS, LANES), F32),
            jax.ShapeDtypeStruct((bsz * t, ROUTE_W), F32),
            jax.ShapeDtypeStruct((bsz * n_t * 8, ROUTE_W), F32),
        ],
        compiler_params=_params(2),
        name="mix_out",
    )(x, o_f, o_b, sr, ya, mod, lnp, gn, w_out_b, wr, br)


def _moe_kernel(n_chunks, nused_ref, lo_ref, hi_ref, live_ref, src_ref, dst_ref,
                hxs_hbm, w1l_ref, w3l_ref, w2l_ref, w1h_ref, w3h_ref, w2h_ref, out_hbm, gbuf, obuf, gsem, ssem):
    tile_rows = CHUNKS_PER_TILE * SLAB_ROWS
    i = pl.program_id(0)
    n_used = nused_ref[0]
    slot = lax.bitwise_and(i, 1)

    def slab(chunk):
        return pl.ds(pl.multiple_of(chunk * SLAB_ROWS, SLAB_ROWS), SLAB_ROWS)

    def gather_copy(tile, buf_slot, j):
        chunk = src_ref[tile * CHUNKS_PER_TILE + j]
        return pltpu.make_async_copy(hxs_hbm.at[:, slab(chunk), :], gbuf.at[buf_slot, :, slab(j), :],
                                     gsem.at[buf_slot])

    def scatter_copy(tile, buf_slot, j):
        chunk = dst_ref[tile * CHUNKS_PER_TILE + j]
        return pltpu.make_async_copy(obuf.at[buf_slot, :, slab(j), :], out_hbm.at[:, slab(chunk), :],
                                     ssem.at[buf_slot])

    def start_gather(tile, buf_slot):
        for j in range(CHUNKS_PER_TILE):
            gather_copy(tile, buf_slot, j).start(priority=j % 2)

    def wait_gather(buf_slot):
        pltpu.make_async_copy(hxs_hbm.at[:, pl.ds(0, tile_rows), :], gbuf.at[buf_slot], gsem.at[buf_slot]).wait()

    def wait_scatter(buf_slot):
        pltpu.make_async_copy(obuf.at[buf_slot], out_hbm.at[:, pl.ds(0, tile_rows), :], ssem.at[buf_slot]).wait()

    @pl.when(i == 0)
    def _():
        start_gather(0, 0)
        obuf[...] = jnp.zeros(obuf.shape, F32)
        for s in range(2):
            fill = pltpu.make_async_copy(
                obuf.at[s], out_hbm.at[:, pl.ds((n_chunks + s * CHUNKS_PER_TILE) * SLAB_ROWS, tile_rows), :],
                ssem.at[s])
            fill.start()
            fill.wait()

    @pl.when(i + 1 < n_used)
    def _():
        start_gather(i + 1, 1 - slot)

    @pl.when(i < n_used)
    def _():
        wait_gather(slot)

        @pl.when(i >= 2)
        def _():
            wait_scatter(slot)

        @pl.when(live_ref[i] != 0)
        def _():
            def lane_block(c, half):
                return jnp.concatenate(
                    [gbuf[slot, c, pl.ds(2 * r + half, CHUNKS_PER_TILE, stride=SLAB_ROWS), :]
                     for r in range(CHUNK_ROWS)], axis=0)

            n_blk = HALF_W // LANES
            xb = jnp.concatenate([lane_block(c, 0) for c in range(n_blk)]
                                 + [lane_block(c, 1) for c in range(n_blk)], axis=1).astype(BF16)
            rec = lane_block(n_blk, 0)
            w_lo = rec[:, 0:1] + rec[:, 1:2] + rec[:, 2:3]
            w_hi = rec[:, 3:4] + rec[:, 4:5] + rec[:, 5:6]

            gate = [_dot(xb, w1_ref[...]) for w1_ref in (w1l_ref, w1h_ref)]
            up = [_dot(xb, w3_ref[...]) for w3_ref in (w3l_ref, w3h_ref)]
            act = [(_silu(g) * u).astype(BF16) for g, u in zip(gate, up)]
            e_lo, e_hi = [_dot(a, w2_ref[...]) for a, w2_ref in zip(act, (w2l_ref, w2h_ref))]
            y = w_lo * e_lo + w_hi * e_hi
            y = y.astype(BF16).astype(F32)
            for r in range(CHUNK_ROWS):
                rows = slice(r * CHUNKS_PER_TILE, (r + 1) * CHUNKS_PER_TILE)
                for half in range(2):
                    for c in range(n_blk):
                        col = half * HALF_W + c * LANES
                        obuf[slot, c, pl.ds(2 * r + half, CHUNKS_PER_TILE, stride=SLAB_ROWS), :] = (
                            y[rows, col:col + LANES])

        @pl.when(live_ref[i] == 0)
        def _():
            obuf[slot] = jnp.zeros(obuf.shape[1:], F32)

        for j in range(CHUNKS_PER_TILE):
            scatter_copy(i, slot, j).start(priority=j % 2)

        @pl.when(i == n_used - 1)
        def _():
            wait_scatter(slot)

            @pl.when(i >= 1)
            def _():
                wait_scatter(1 - slot)


def _moe(hxs, src, dst, n_used, tile_lo, tile_hi, tile_live, w1_b, w3_b, w2_b):
    n_chunks = hxs.shape[1] // SLAB_ROWS
    tile_rows = CHUNKS_PER_TILE * SLAB_ROWS
    n_steps = src.shape[0] // CHUNKS_PER_TILE
    wspec = lambda which, shape: pl.BlockSpec(
        (None,) + shape, (lambda i, nu, lo, hi, lv, s, d: (lo[i], 0, 0)) if which == 0 else
        (lambda i, nu, lo, hi, lv, s, d: (hi[i], 0, 0)))
    grid_spec = pltpu.PrefetchScalarGridSpec(
        num_scalar_prefetch=6,
        grid=(n_steps,),
        in_specs=[
            pl.BlockSpec(memory_space=pl.ANY),
            wspec(0, (D_MODEL, D_EXPERT)), wspec(0, (D_MODEL, D_EXPERT)), wspec(0, (D_EXPERT, D_MODEL)),
            wspec(1, (D_MODEL, D_EXPERT)), wspec(1, (D_MODEL, D_EXPERT)), wspec(1, (D_EXPERT, D_MODEL)),
        ],
        out_specs=pl.BlockSpec(memory_space=pl.ANY),
        scratch_shapes=[
            pltpu.VMEM((2, SLAB_IN_W // LANES, tile_rows, LANES), F32),
            pltpu.VMEM((2, SLAB_OUT_W // LANES, tile_rows, LANES), F32),
            pltpu.SemaphoreType.DMA((2,)),
            pltpu.SemaphoreType.DMA((2,)),
        ],
    )
    return pl.pallas_call(
        functools.partial(_moe_kernel, n_chunks),
        grid_spec=grid_spec,
        out_shape=jax.ShapeDtypeStruct((SLAB_OUT_W // LANES, (n_chunks + 2 * CHUNKS_PER_TILE) * SLAB_ROWS, LANES), F32),
        compiler_params=_params(1),
        name="moe",
    )(n_used, tile_lo, tile_hi, tile_live, src, dst, hxs, w1_b, w3_b, w2_b, w1_b, w3_b, w2_b)


def _final_kernel(tm, x1_ref, moe_ref, pos_ref, mod_ref, lnp_ref, o_ref):
    moe_b = jnp.concatenate([moe_ref[c] for c in range(SLAB_OUT_W // LANES)], axis=1).astype(BF16)
    sort_lo, sort_hi = _slab_sort_matrices(pos_ref[:, 0:1], tm, 1)
    subs = [slice(s, s + MIX_SUB_TILE) for s in range(0, tm, MIX_SUB_TILE)]
    moe = [jnp.concatenate([_dot(sort_lo[rows, :], moe_b), _dot(sort_hi[rows, :], moe_b)], axis=1) for rows in subs]
    for rows, moe_s in zip(subs, moe):
        o_ref[rows, :] = _layer_norm(DEEPNORM_ALPHA * x1_ref[rows, :] + mod_ref[...] * moe_s,
                                     lnp_ref[0:1, :], lnp_ref[1:2, :])


def _final(x1, moe, pos, g2, lnp):
    bsz, t, _ = x1.shape
    tm = SORT_TILE
    n_t = t // tm
    flat = lambda rows, w: pl.BlockSpec((rows, w), lambda b, i: (b * n_t + i, 0))
    return pl.pallas_call(
        functools.partial(_final_kernel, tm),
        grid=(bsz, n_t),
        in_specs=[
            pl.BlockSpec((None, tm, D_MODEL), lambda b, i: (b, i, 0)),
            pl.BlockSpec((SLAB_OUT_W // LANES, LOCAL_SLAB_ROWS, LANES), lambda b, i: (0, b * n_t + i, 0)),
            flat(tm, ROUTE_W),
            pl.BlockSpec((None, 1, D_MODEL), lambda b, i: (b, 0, 0)),
            pl.BlockSpec(lnp.shape, lambda b, i: (0, 0)),
        ],
        out_specs=pl.BlockSpec((None, tm, D_MODEL), lambda b, i: (b, i, 0)),
        out_shape=jax.ShapeDtypeStruct((bsz, t, D_MODEL), F32),
        compiler_params=_params(2),
        name="final",
    )(x1, moe, pos, g2, lnp)


def _pair_tables():
    lo, hi = [], []
    for g in range(N_GROUPS):
        for a in range(EXPERTS_PER_GROUP):
            for b in range(a + 1, EXPERTS_PER_GROUP):
                lo.append(g * EXPERTS_PER_GROUP + a)
                hi.append(g * EXPERTS_PER_GROUP + b)
    return jnp.array(lo, jnp.int32), jnp.array(hi, jnp.int32)


def _moe_plan(chunks, n_sort_tiles):
    n_cls = N_CLASSES + 1
    hp = lax.Precision.HIGHEST
    m = chunks.reshape(n_sort_tiles, 8, ROUTE_W)[:, 0, :n_cls].astype(jnp.int32)
    a_end = jnp.cumsum(m, axis=0)
    a_start = a_end - m
    per_cls = a_end[-1]
    padded = (per_cls + CHUNKS_PER_TILE - 1) // CHUNKS_PER_TILE * CHUNKS_PER_TILE
    g_end = jnp.cumsum(padded)
    g_start = g_end - padded
    local_off = jnp.cumsum(m, axis=1) - m
    seg = jnp.arange(n_sort_tiles, dtype=jnp.int32)[:, None] * LOCAL_CHUNKS + local_off - a_start
    n_steps = -(-(n_sort_tiles * LOCAL_CHUNKS) // CHUNKS_PER_TILE) + n_cls
    p = jnp.arange(n_steps * CHUNKS_PER_TILE, dtype=jnp.int32)
    cls_p = jnp.minimum(jnp.sum((g_end[None, :] <= p[:, None]).astype(jnp.int32), axis=1), n_cls - 1)
    onehot = (cls_p[:, None] == jnp.arange(n_cls, dtype=jnp.int32)[None, :]).astype(F32)
    pick = lambda tab: jnp.dot(onehot, tab.astype(F32), precision=hp)
    u = p - pick(g_start[:, None])[:, 0].astype(jnp.int32)
    valid = u < pick(per_cls[:, None])[:, 0].astype(jnp.int32)
    a_end_p = pick(a_end.T).astype(jnp.int32)
    seg_p = pick(seg.T).astype(jnp.int32)
    tile_p = jnp.sum((a_end_p <= u[:, None]).astype(jnp.int32), axis=1)
    hit = jnp.arange(n_sort_tiles, dtype=jnp.int32)[None, :] == tile_p[:, None]
    src = jnp.sum(jnp.where(hit, seg_p, 0), axis=1) + u
    pad_dst = n_sort_tiles * LOCAL_CHUNKS + (p // CHUNKS_PER_TILE) % 2 * CHUNKS_PER_TILE + p % CHUNKS_PER_TILE
    dst = jnp.where(valid, src, pad_dst).astype(jnp.int32)
    src = jnp.where(valid, src, 0).astype(jnp.int32)
    n_used = g_end[-1:] // CHUNKS_PER_TILE
    step = jnp.arange(n_steps, dtype=jnp.int32)
    tile_cls = jnp.sum((g_end[None, :] // CHUNKS_PER_TILE <= step[:, None]).astype(jnp.int32), axis=1)
    live = ((tile_cls < N_CLASSES) & (step < n_used[0])).astype(jnp.int32)
    pair_lo, pair_hi = _pair_tables()
    pair_oh = (jnp.minimum(tile_cls, N_CLASSES - 1)[:, None] == jnp.arange(N_CLASSES)[None, :]).astype(jnp.int32)
    tile_lo = jnp.sum(pair_oh * pair_lo[None, :], axis=1).astype(jnp.int32)
    tile_hi = jnp.sum(pair_oh * pair_hi[None, :], axis=1).astype(jnp.int32)
    return src, dst, n_used.astype(jnp.int32), tile_lo, tile_hi, live


def kernel(x, c, ctx, c_ctx, ln_in_g, ln_in_b, w_ada, b_ada, w_in, conv_w, conv_b, gate_w2_fwd, gate_b_fwd,
           gate_w2_bwd, gate_b_bwd, gla_norm_g, w_out, ln1_g, ln1_b, router_group_w, router_group_b,
           router_expert_w, router_expert_b, expert_w1, expert_w3, expert_w2, ln2_g, ln2_b):
    bsz, t, _ = x.shape
    n_tok = bsz * t
    l = 0
    rows = -(-(bsz + 1) // 8) * 8
    cond = jnp.zeros((rows, D_MODEL), F32).at[:bsz].set(c).at[bsz].set(c_ctx)
    ada = _ada(cond, w_ada[l], b_ada[l][None, :])
    sh1, sc1, g1, sh2, sc2, g2 = [ada[:, i * D_MODEL:(i + 1) * D_MODEL] for i in range(6)]

    w_in_b = w_in[l].astype(BF16)
    lnp_in = jnp.stack([ln_in_g, ln_in_b])
    zero = jnp.zeros((GLA_GATE_RANK, GLA_KEY), F32)
    w2cat = jnp.concatenate([jnp.concatenate([gate_w2_fwd[l], zero], axis=1),
                             jnp.concatenate([zero, gate_w2_bwd[l]], axis=1)], axis=0).astype(BF16)
    gbias = jnp.concatenate([gate_b_fwd[l], gate_b_bwd[l]])[None, :]

    mod_ctx = jnp.broadcast_to(jnp.stack([1.0 + sc1[bsz], sh1[bsz]])[None], (bsz, 2, D_MODEL))
    k_c, v_c, g_c = _proj(ctx, mod_ctx, lnp_in, w_in_b, conv_w[l], conv_b[l][None, :], w2cat, gbias, False)
    zero_state = jnp.zeros((bsz, GLA_KEY, PAIR_VAL), F32)
    s_f, s_b = _gla(None, k_c, v_c, g_c, zero_state, zero_state)

    mod1 = jnp.stack([1.0 + sc1[:bsz], sh1[:bsz]], axis=1)
    ya, q, k, v, sr, g = _proj(x, mod1, lnp_in, w_in_b, conv_w[l], conv_b[l][None, :], w2cat, gbias, True)
    o_f, o_b, _, _ = _gla(q, k, v, g, s_f, s_b)

    mod2 = jnp.stack([g1[:bsz], 1.0 + sc2[:bsz], sh2[:bsz]], axis=1)
    lnp1 = jnp.stack([ln_in_g, ln_in_b, ln1_g[l], ln1_b[l]])
    wr = jnp.zeros((D_MODEL, ROUTE_W), F32)
    wr = wr.at[:, :N_GROUPS].set(router_group_w[l]).at[:, N_GROUPS:N_GROUPS + N_EXPERTS].set(router_expert_w[l])
    br = jnp.zeros((1, ROUTE_W), F32)
    br = br.at[0, :N_GROUPS].set(router_group_b[l]).at[0, N_GROUPS:N_GROUPS + N_EXPERTS].set(router_expert_b[l])
    x1, hxs, pos, chunks = _mix_out(x, o_f, o_b, sr, ya, mod2, lnp1, gla_norm_g[l][None, :],
                                    w_out[l].astype(BF16), wr, br)

    src, dst, n_used, tile_lo, tile_hi, live = _moe_plan(chunks, n_tok // SORT_TILE)
    moe = _moe(hxs, src, dst, n_used, tile_lo, tile_hi, live,
               expert_w1[l].astype(BF16), expert_w3[l].astype(BF16), expert_w2[l].astype(BF16))

    return _final(x1, moe, pos, g2[:bsz][:, None, :], jnp.stack([ln2_g[l], ln2_b[l]]))
```

```python
import functools

import jax
import jax.numpy as jnp
from jax import lax
from jax.experimental import pallas as pl
from jax.experimental.pallas import tpu as pltpu

F32 = jnp.float32
BF16 = jnp.bfloat16

D_MODEL = 1024
GRID_W = 64
CONV_CH = 512
GLA_HEADS = 4
GLA_DK = 64
GLA_DV = 128
GLA_KEY = GLA_HEADS * GLA_DK
GLA_VAL = GLA_HEADS * GLA_DV
PAIR_KEY = 2 * GLA_DK
PAIR_VAL = 2 * GLA_DV
GLA_GATE_RANK = 16
GLA_TAU = 16.0
OFF_AB = 0
OFF_AC = OFF_AB + CONV_CH
OFF_AX = OFF_AC + CONV_CH
OFF_Q = OFF_AX + CONV_CH
OFF_K = OFF_Q + GLA_KEY
OFF_V = OFF_K + GLA_KEY
OFF_R = OFF_V + GLA_VAL
OFF_GF = OFF_R + GLA_VAL
D_PROJ = OFF_GF + 2 * GLA_GATE_RANK
N_GROUPS = 4
EXPERTS_PER_GROUP = 4
N_EXPERTS = N_GROUPS * EXPERTS_PER_GROUP
D_EXPERT = 512
PAIRS_PER_GROUP = 6
N_CLASSES = N_GROUPS * PAIRS_PER_GROUP
LN_EPS = 1e-5
RMS_EPS = 1e-6
DEPTH = 1
DEEPNORM_ALPHA = (2.0 * DEPTH) ** 0.25

LANES = 128
GLA_CHUNK = 64
GLA_SUB = 16
N_SUB = GLA_CHUNK // GLA_SUB
ROUTE_W = LANES
HALF_W = D_MODEL // 2
SLAB_IN_W = HALF_W + ROUTE_W
SLAB_OUT_W = HALF_W
SORT_TILE = 256
MIX_SUB_TILE = 128
MOE_TILE = 256
CHUNK_ROWS = 4
SLAB_ROWS = 2 * CHUNK_ROWS
LOCAL_CHUNKS = -(-(SORT_TILE + N_CLASSES * (CHUNK_ROWS - 1)) // CHUNK_ROWS)
LOCAL_SLAB_ROWS = LOCAL_CHUNKS * SLAB_ROWS
CHUNKS_PER_TILE = MOE_TILE // CHUNK_ROWS
VMEM_LIMIT = 56 * 1024 * 1024


def _params(n_axes, vmem=VMEM_LIMIT):
    return pltpu.CompilerParams(dimension_semantics=("arbitrary",) * n_axes, vmem_limit_bytes=vmem)


def _dot(a, b):
    return jnp.dot(a, b, preferred_element_type=F32)


def _div_pow2(x, d):
    assert d & (d - 1) == 0
    return lax.shift_right_logical(x, jnp.int32(d.bit_length() - 1))


def _mod_pow2(x, d):
    assert d & (d - 1) == 0
    return lax.bitwise_and(x, jnp.int32(d - 1))


def _split2(x):
    hi = x.astype(BF16)
    lo = (x - hi.astype(F32)).astype(BF16)
    return hi, lo


def _dot3(a, b):
    ah, al = _split2(a)
    bh, bl = _split2(b)
    return _dot(ah, bh) + _dot(ah, bl) + _dot(al, bh)


def _silu(x):
    return x * (0.5 * jnp.tanh(0.5 * x) + 0.5)


def _layer_norm(x, g, b):
    mu = jnp.mean(x, axis=-1, keepdims=True)
    xc = x - mu
    var = jnp.mean(xc * xc, axis=-1, keepdims=True)
    return xc * lax.rsqrt(var + LN_EPS) * g + b


def _ada_kernel(c_ref, w_ref, b_ref, o_ref):
    o_ref[...] = _dot3(_silu(c_ref[...]), w_ref[...]) + b_ref[...]


def _ada(cond, w_ada, b_ada):
    rows = cond.shape[0]
    n_out = w_ada.shape[1]
    tn = 1024
    return pl.pallas_call(
        _ada_kernel,
        grid=(n_out // tn,),
        in_specs=[
            pl.BlockSpec((rows, D_MODEL), lambda j: (0, 0)),
            pl.BlockSpec((D_MODEL, tn), lambda j: (0, j)),
            pl.BlockSpec((1, tn), lambda j: (0, j)),
        ],
        out_specs=pl.BlockSpec((rows, tn), lambda j: (0, j)),
        out_shape=jax.ShapeDtypeStruct((rows, n_out), F32),
        compiler_params=_params(1),
        name="ada",
    )(cond, w_ada, b_ada)


def _log_sigmoid(z):
    return jnp.minimum(z, 0.0) - jnp.log(1.0 + jnp.exp(-jnp.abs(z)))


def _proj_kernel(latent, tm, x_ref, mod_ref, lnp_ref, w_ref, cw_ref, cb_ref, w2_ref, gbias_ref, *out_refs):
    x = x_ref[...]
    xn = _layer_norm(x, lnp_ref[0:1, :], lnp_ref[1:2, :])
    h = xn * mod_ref[0:1, :] + mod_ref[1:2, :]
    hb = h.astype(BF16)
    if latent:
        ya_ref, q_ref, k_ref, v_ref, sr_ref, g_ref = out_refs
        p = _dot(hb, w_ref[:, OFF_AB:OFF_Q])
        a_b = p[:, 0:CONV_CH]
        u = p[:, CONV_CH:2 * CONV_CH] * p[:, 2 * CONV_CH:3 * CONV_CH]
        pos = _mod_pow2(lax.broadcasted_iota(jnp.int32, (tm, 1), 0), GRID_W)
        u_prev = jnp.where(pos == 0, 0.0, pltpu.roll(u, 1, 0))
        u_next = jnp.where(pos == GRID_W - 1, 0.0, pltpu.roll(u, tm - 1, 0))
        conv = u_prev * cw_ref[0:1, :] + u * cw_ref[1:2, :] + u_next * cw_ref[2:3, :] + cb_ref[...]
        ya_ref[...] = (a_b * conv).astype(BF16)
        qk = _dot(hb, w_ref[:, OFF_Q:OFF_V])
        q_ref[...] = (qk[:, 0:GLA_KEY] * (GLA_DK ** -0.5)).astype(BF16)
        k_ref[...] = qk[:, GLA_KEY:].astype(BF16)
        r = _dot(hb, w_ref[:, OFF_R:OFF_GF])
        sr_ref[...] = _silu(r).astype(BF16)
    else:
        k_ref, v_ref, g_ref = out_refs
        k_ref[...] = _dot(hb, w_ref[:, OFF_K:OFF_V]).astype(BF16)
    v_ref[...] = _dot(hb, w_ref[:, OFF_V:OFF_R]).astype(BF16)
    low = _dot(hb, w_ref[:, OFF_GF:D_PROJ])
    z = _dot(low.astype(BF16), w2_ref[...]) + gbias_ref[...]
    g_ref[...] = _log_sigmoid(z) * (1.0 / GLA_TAU)


def _proj(x, mod, lnp, w_in_b, conv_w, conv_b, w2cat, gbias, latent):
    bsz, t, _ = x.shape
    tm = min(512, t)
    assert t % tm == 0 and tm % GRID_W == 0
    tok = lambda w: pl.BlockSpec((None, tm, w), lambda b, i: (b, i, 0))
    full = lambda a: pl.BlockSpec(a.shape, lambda b, i: (0,) * a.ndim)
    widths = ([(CONV_CH, BF16), (GLA_KEY, BF16)] if latent else []) + [(GLA_KEY, BF16), (GLA_VAL, BF16)]
    widths += ([(GLA_VAL, BF16)] if latent else []) + [(2 * GLA_KEY, F32)]
    return pl.pallas_call(
        functools.partial(_proj_kernel, latent, tm),
        grid=(bsz, t // tm),
        in_specs=[
            tok(D_MODEL),
            pl.BlockSpec((None, 2, D_MODEL), lambda b, i: (b, 0, 0)),
            full(lnp), full(w_in_b), full(conv_w), full(conv_b), full(w2cat), full(gbias),
        ],
        out_specs=[tok(w) for w, _ in widths],
        out_shape=[jax.ShapeDtypeStruct((bsz, t, w), dt) for w, dt in widths],
        compiler_params=_params(2),
        name="proj_latent" if latent else "proj_ctx",
    )(x, mod, lnp, w_in_b, conv_w, conv_b, w2cat, gbias)


def _tri(n, reverse, strict=False):
    i = lax.broadcasted_iota(jnp.int32, (n, n), 0)
    j = lax.broadcasted_iota(jnp.int32, (n, n), 1)
    if strict:
        m = (j > i) if reverse else (j < i)
    else:
        m = (j >= i) if reverse else (j <= i)
    return jnp.where(m, 1.0, 0.0).astype(BF16)


def _chunk_cumsum(g, reverse):
    tri = _tri(GLA_CHUNK, reverse)
    g_hi, g_lo = _split2(g)
    return _dot(tri, g_hi) + _dot(tri, g_lo)


def _as_column(row):
    return jnp.broadcast_to(row, (LANES, row.shape[1])).T


def _sub_anchors(gc, reverse):
    zero = jnp.zeros((1, GLA_KEY), F32)
    if reverse:
        return [gc[GLA_SUB * (a + 1):GLA_SUB * (a + 1) + 1] for a in range(N_SUB - 1)] + [zero]
    return [zero] + [gc[GLA_SUB * a - 1:GLA_SUB * a] for a in range(1, N_SUB)]


def _score_pairs(reverse):
    return [(a, b) for a in range(N_SUB) for b in range(N_SUB) if (b >= a if reverse else b <= a)]


def _intra_products(q, k, gc, reverse):
    r = _sub_anchors(gc, reverse)
    anchor = jnp.concatenate([jnp.broadcast_to(ra, (GLA_SUB, GLA_KEY)) for ra in r], axis=0)
    gcb = gc - anchor
    qt = q * jnp.exp(gcb)
    kt = k * jnp.exp(-gcb)
    rows = []
    for a, b in _score_pairs(reverse):
        qa = qt[GLA_SUB * a:GLA_SUB * (a + 1)]
        if a != b:
            qa = qa * jnp.exp(r[a] - r[b])
        rows.append(qa)
    qp = jnp.concatenate(rows, axis=0).astype(BF16)
    width = GLA_HEADS * GLA_CHUNK
    rr = lax.broadcasted_iota(jnp.int32, (width, GLA_KEY), 0)
    cc = lax.broadcasted_iota(jnp.int32, (width, GLA_KEY), 1)
    kbd = jnp.where(_div_pow2(rr, GLA_CHUNK) == _div_pow2(cc, GLA_DK),
                    jnp.concatenate([kt] * GLA_HEADS, axis=0), 0.0)
    return lax.dot_general(qp, kbd.astype(BF16), (((1,), (1,)), ((), ())), preferred_element_type=F32)


def _assemble_scores(res, reverse):
    pairs = _score_pairs(reverse)
    width = GLA_HEADS * GLA_CHUNK
    col = _mod_pow2(lax.broadcasted_iota(jnp.int32, (GLA_SUB, width), 1), GLA_CHUNK)
    col_blk = _div_pow2(col, GLA_SUB)
    col_pos = _mod_pow2(col, GLA_SUB)
    row_pos = lax.broadcasted_iota(jnp.int32, (GLA_SUB, width), 0)
    causal = (col_pos >= row_pos) if reverse else (col_pos <= row_pos)
    blocks = []
    for a in range(N_SUB):
        acc = jnp.zeros((GLA_SUB, width), F32)
        for idx, (pa, pb) in enumerate(pairs):
            if pa != a:
                continue
            keep = col_blk == pb
            if pa == pb:
                keep = keep & causal
            acc = acc + jnp.where(keep, res[GLA_SUB * idx:GLA_SUB * (idx + 1)], 0.0)
        blocks.append(acc)
    return jnp.concatenate(blocks, axis=0)


def _pair_mask(rows_per_head, cols_per_head, n_row_pairs=1):
    shape = (n_row_pairs * 2 * rows_per_head, 2 * cols_per_head)
    rr = _mod_pow2(lax.broadcasted_iota(jnp.int32, shape, 0), 2 * rows_per_head)
    cc = lax.broadcasted_iota(jnp.int32, shape, 1)
    return _div_pow2(rr, rows_per_head) == _div_pow2(cc, cols_per_head)


def _state_terms(q, k, v_b, gc, reverse):
    total = gc[0:1] if reverse else gc[GLA_CHUNK - 1:GLA_CHUNK]
    q_dec = None if q is None else (q * jnp.exp(gc)).astype(BF16)
    k_end = (k * jnp.exp(total - gc)).astype(BF16)
    tn = (((0,), (0,)), ((), ()))
    upd = [lax.dot_general(k_end[:, p * PAIR_KEY:(p + 1) * PAIR_KEY], v_b[:, p * PAIR_VAL:(p + 1) * PAIR_VAL], tn,
                           preferred_element_type=F32) for p in range(GLA_HEADS // 2)]
    upd = jnp.where(_pair_mask(GLA_DK, GLA_DV, GLA_HEADS // 2), jnp.concatenate(upd, axis=0), 0.0)
    decay = jnp.exp(_as_column(total))
    decay = jnp.concatenate([decay] * (PAIR_VAL // LANES), axis=1)
    return q_dec, decay, upd


def _advance_state(q_dec, decay, upd, state):
    o_inter = None
    if q_dec is not None:
        state_b = state.astype(BF16)
        o_inter = jnp.concatenate(
            [_dot(q_dec[:, p * PAIR_KEY:(p + 1) * PAIR_KEY], state_b[p * PAIR_KEY:(p + 1) * PAIR_KEY, :])
             for p in range(GLA_HEADS // 2)], axis=1)
    return o_inter, state * decay + upd


def _gla_kernel(tt, nt, with_out, *refs):
    if with_out:
        (qf_ref, kf_ref, vf_ref, gfw_ref, qb_ref, kb_ref, vb_ref, gbw_ref, s0f_ref, s0b_ref,
         of_ref, ob_ref, sf_ref, sb_ref) = refs
    else:
        kf_ref, vf_ref, gfw_ref, kb_ref, vb_ref, gbw_ref, s0f_ref, s0b_ref, sf_ref, sb_ref = refs
    j = pl.program_id(1)

    @pl.when(j == 0)
    def _():
        sf_ref[...] = s0f_ref[...]
        sb_ref[...] = s0b_ref[...]

    chunk_slices = [slice(c * GLA_CHUNK, (c + 1) * GLA_CHUNK) for c in range(tt // GLA_CHUNK)]
    fwd, bwd = [], []
    for sl in chunk_slices:
        g = gfw_ref[sl, :]
        fwd.append(dict(sl=sl, k=kf_ref[sl, :].astype(F32), v=vf_ref[sl, :], g_f=g[:, 0:GLA_KEY], g_b=g[:, GLA_KEY:],
                        q=qf_ref[sl, :].astype(F32) if with_out else None))
        bwd.append(dict(sl=sl, k=kb_ref[sl, :].astype(F32), v=vb_ref[sl, :], g_b=gbw_ref[sl, :][:, GLA_KEY:],
                        q=qb_ref[sl, :].astype(F32) if with_out else None))
    for d in fwd:
        d["gc_f"] = _chunk_cumsum(d["g_f"], False)
        if with_out:
            d["gc_b"] = _chunk_cumsum(d["g_b"], True)
    for d in bwd:
        d["gc_b"] = _chunk_cumsum(d["g_b"], True)
    if with_out:
        for d in fwd:
            d["res_f"] = _intra_products(d["q"], d["k"], d["gc_f"], False)
            d["res_b"] = _intra_products(d["q"], d["k"], d["gc_b"], True)
    for d in fwd:
        d["terms"] = _state_terms(d["q"], d["k"], d["v"], d["gc_f"], False)
    for d in bwd:
        d["terms"] = _state_terms(d["q"], d["k"], d["v"], d["gc_b"], True)
    if with_out:
        for d in fwd:
            scores = (_assemble_scores(d["res_f"], False) + _assemble_scores(d["res_b"], True)).astype(BF16)
            o_intra = []
            for p in range(GLA_HEADS // 2):
                v_p = d["v"][:, p * PAIR_VAL:(p + 1) * PAIR_VAL]
                vbd = jnp.where(_pair_mask(GLA_CHUNK, GLA_DV), jnp.concatenate([v_p, v_p], axis=0),
                                jnp.zeros((), BF16))
                o_intra.append(_dot(scores[:, p * 2 * GLA_CHUNK:(p + 1) * 2 * GLA_CHUNK], vbd))
            d["o_intra"] = jnp.concatenate(o_intra, axis=1)

    state = sf_ref[...]
    for d in fwd:
        o_inter, state = _advance_state(*d["terms"], state)
        if with_out:
            of_ref[d["sl"], :] = (d["o_intra"] + o_inter).astype(BF16)
    sf_ref[...] = state

    state = sb_ref[...]
    for d in reversed(bwd):
        o_inter, state = _advance_state(*d["terms"], state)
        if with_out:
            ob_ref[d["sl"], :] = o_inter.astype(BF16)
    sb_ref[...] = state


def _gla(q, k, v, g, s0f, s0b):
    with_out = q is not None
    bsz, t, _ = k.shape
    tt = min(256, t)
    assert t % tt == 0 and tt % GLA_CHUNK == 0
    nt = t // tt
    fwd = lambda w: pl.BlockSpec((None, tt, w), lambda b, j: (b, j, 0))
    bwd = lambda w: pl.BlockSpec((None, tt, w), lambda b, j: (b, nt - 1 - j, 0))
    st = pl.BlockSpec((None, GLA_KEY, PAIR_VAL), lambda b, j: (b, 0, 0))
    st_shape = jax.ShapeDtypeStruct((bsz, GLA_KEY, PAIR_VAL), F32)
    if with_out:
        ins = [q, k, v, g, q, k, v, g, s0f, s0b]
        in_specs = [fwd(GLA_KEY), fwd(GLA_KEY), fwd(GLA_VAL), fwd(2 * GLA_KEY),
                    bwd(GLA_KEY), bwd(GLA_KEY), bwd(GLA_VAL), bwd(2 * GLA_KEY), st, st]
        out_specs = [fwd(GLA_VAL), bwd(GLA_VAL), st, st]
        o_shape = jax.ShapeDtypeStruct((bsz, t, GLA_VAL), BF16)
        out_shape = [o_shape, o_shape, st_shape, st_shape]
    else:
        ins = [k, v, g, k, v, g, s0f, s0b]
        in_specs = [fwd(GLA_KEY), fwd(GLA_VAL), fwd(2 * GLA_KEY),
                    bwd(GLA_KEY), bwd(GLA_VAL), bwd(2 * GLA_KEY), st, st]
        out_specs = [st, st]
        out_shape = [st_shape, st_shape]
    return pl.pallas_call(
        functools.partial(_gla_kernel, tt, nt, with_out),
        grid=(bsz, nt),
        in_specs=in_specs,
        out_specs=out_specs,
        out_shape=out_shape,
        compiler_params=_params(2),
        name="gla_latent" if with_out else "gla_ctx",
    )(*ins)


def _exact_bf16_parts(x):
    hi = x.astype(BF16).astype(F32)
    r = x - hi
    mid = r.astype(BF16).astype(F32)
    lo = (r - mid).astype(BF16).astype(F32)
    return hi, mid, lo


def _first_index(values, best):
    idx = jnp.full_like(best, float(len(values) - 1))
    for i in reversed(range(len(values) - 1)):
        idx = jnp.where(values[i] >= best, float(i), idx)
    return idx


def _pick(rows, idx):
    out = rows[-1]
    for i in reversed(range(len(rows) - 1)):
        out = jnp.where(idx == float(i), rows[i], out)
    return out


def _route(logit_t, tm):
    row = lambda r: logit_t[r:r + 1, :]
    groups = [row(i) for i in range(N_GROUPS)]
    top = functools.reduce(jnp.maximum, groups)
    eg = [jnp.exp(x - top) for x in groups]
    total = functools.reduce(lambda a, b: a + b, eg)
    pg = [e / total for e in eg]
    p_g = functools.reduce(jnp.maximum, pg)
    g_idx = _first_index(pg, p_g)
    sel = [_pick([row(N_GROUPS + EXPERTS_PER_GROUP * g + j) for g in range(N_GROUPS)], g_idx)
           for j in range(EXPERTS_PER_GROUP)]
    top = functools.reduce(jnp.maximum, sel)
    ee = [jnp.exp(x - top) for x in sel]
    total = functools.reduce(lambda a, b: a + b, ee)
    pe = [e / total for e in ee]
    p1 = functools.reduce(jnp.maximum, pe)
    l1 = _first_index(pe, p1)
    pe2 = [jnp.where(l1 == float(j), -1.0, pe[j]) for j in range(EXPERTS_PER_GROUP)]
    p2 = functools.reduce(jnp.maximum, pe2)
    l2 = _first_index(pe2, p2)
    den = p1 + p2
    w1 = p1 / den * p_g
    w2 = p2 / den * p_g
    lo = jnp.minimum(l1, l2)
    hi = jnp.maximum(l1, l2)
    pair = lo * (7.0 - lo) * 0.5 + (hi - lo - 1.0)
    cls = g_idx * PAIRS_PER_GROUP + pair
    w_lo = jnp.where(l1 < l2, w1, w2)
    w_hi = jnp.where(l1 < l2, w2, w1)
    cls_id = lax.broadcasted_iota(jnp.int32, (ROUTE_W, tm), 0).astype(F32)
    onehot = jnp.where(cls_id == cls, 1.0, 0.0)
    before = _dot(onehot.astype(BF16), _tri(tm, True, strict=True))
    count = jnp.sum(onehot, axis=1, keepdims=True)
    chunks = jnp.floor((count + (CHUNK_ROWS - 1.0)) * (1.0 / CHUNK_ROWS))
    chunks = jnp.where(cls_id[:, 0:1] == float(N_CLASSES),
                       LOCAL_CHUNKS - jnp.sum(chunks, axis=0, keepdims=True), chunks)
    first_chunk = _dot(_tri(ROUTE_W, False, strict=True),
                       jnp.broadcast_to(chunks, (ROUTE_W, LANES)).astype(BF16))[:, 0:1]
    pos_row = jnp.sum(onehot * (CHUNK_ROWS * first_chunk + before), axis=0, keepdims=True)
    return pos_row, w_lo, w_hi, chunks


def _slab_sort_matrices(pos, tm, slab_axis):
    shape = (LOCAL_SLAB_ROWS, tm) if slab_axis == 0 else (tm, LOCAL_SLAB_ROWS)
    slab_row = lax.broadcasted_iota(jnp.int32, shape, slab_axis)
    sub = _mod_pow2(slab_row, SLAB_ROWS)
    token_row = CHUNK_ROWS * _div_pow2(slab_row, SLAB_ROWS) + _div_pow2(sub, 2)
    hit = token_row.astype(F32) == pos
    half = _mod_pow2(sub, 2)
    return [jnp.where(hit & (half == h), 1.0, 0.0).astype(BF16) for h in range(2)]


def _mix_out_kernel(tm, x_ref, of_ref, ob_ref, sr_ref, ya_ref, mod_ref, lnp_ref, gn_ref, wo_ref, wr_ref, br_ref,
                    x1_ref, hxs_ref, pos_ref, chunks_ref):
    subs = [slice(s, s + MIX_SUB_TILE) for s in range(0, tm, MIX_SUB_TILE)]
    yb = []
    for rows in subs:
        o = of_ref[rows, :].astype(F32) + ob_ref[rows, :].astype(F32)
        sr = sr_ref[rows, :].astype(F32)
        heads = []
        for h in range(GLA_HEADS):
            sl = slice(h * GLA_DV, (h + 1) * GLA_DV)
            oh = o[:, sl]
            ms = jnp.mean(oh * oh, axis=-1, keepdims=True)
            heads.append((oh * lax.rsqrt(ms + RMS_EPS) * gn_ref[...] * sr[:, sl]).astype(BF16))
        yb.append(heads)
    xn = [_layer_norm(x_ref[rows, :], lnp_ref[0:1, :], lnp_ref[1:2, :]) for rows in subs]
    y = []
    for rows, heads in zip(subs, yb):
        acc = _dot(ya_ref[rows, :], wo_ref[0:CONV_CH, :])
        for h in range(GLA_HEADS):
            acc = acc + _dot(heads[h], wo_ref[CONV_CH + h * GLA_DV:CONV_CH + (h + 1) * GLA_DV, :])
        y.append(acc)
    h2 = []
    for rows, xn_s, y_s in zip(subs, xn, y):
        x1 = _layer_norm(DEEPNORM_ALPHA * xn_s + mod_ref[0:1, :] * y_s, lnp_ref[2:3, :], lnp_ref[3:4, :])
        x1_ref[rows, :] = x1
        h2.append(x1 * mod_ref[1:2, :] + mod_ref[2:3, :])
    logit_t = jnp.concatenate([(_dot3(h2_s, wr_ref[...]) + br_ref[...]).T for h2_s in h2], axis=1)
    h2 = jnp.concatenate(h2, axis=0)
    pos_row, w_lo, w_hi, chunks = _route(logit_t, tm)
    rec_id = lax.broadcasted_iota(jnp.int32, (ROUTE_W, tm), 0)
    rec_t = jnp.zeros((ROUTE_W, tm), F32)
    for i, part in enumerate(_exact_bf16_parts(w_lo) + _exact_bf16_parts(w_hi)):
        rec_t = jnp.where(rec_id == i, part, rec_t)
    h2_b = h2.astype(BF16)
    rec_b = rec_t.T.astype(BF16)
    sort_lo, sort_hi = _slab_sort_matrices(pos_row, tm, 0)
    pay_lo = jnp.concatenate([h2_b[:, 0:HALF_W], rec_b], axis=1)
    pay_hi = jnp.concatenate([h2_b[:, HALF_W:], jnp.zeros((tm, ROUTE_W), BF16)], axis=1)
    slabs = _dot(sort_lo, pay_lo) + _dot(sort_hi, pay_hi)
    for c in range(SLAB_IN_W // LANES):
        hxs_ref[c] = slabs[:, c * LANES:(c + 1) * LANES]
    pos_ref[...] = jnp.broadcast_to(pos_row, (ROUTE_W, tm)).T
    chunks_ref[...] = jnp.broadcast_to(chunks, (ROUTE_W, LANES)).T[0:chunks_ref.shape[0], :]


def _mix_out(x, o_f, o_b, sr, ya, mod, lnp, gn, w_out_b, wr, br):
    bsz, t, _ = x.shape
    tm = SORT_TILE
    assert t % tm == 0
    n_t = t // tm
    tok = lambda w: pl.BlockSpec((None, tm, w), lambda b, i: (b, i, 0))
    full = lambda a: pl.BlockSpec(a.shape, lambda b, i: (0,) * a.ndim)
    flat = lambda rows, w: pl.BlockSpec((rows, w), lambda b, i: (b * n_t + i, 0))
    return pl.pallas_call(
        functools.partial(_mix_out_kernel, tm),
        grid=(bsz, n_t),
        in_specs=[
            tok(D_MODEL), tok(GLA_VAL), tok(GLA_VAL), tok(GLA_VAL), tok(CONV_CH),
            pl.BlockSpec((None, 3, D_MODEL), lambda b, i: (b, 0, 0)),
            full(lnp), full(gn), full(w_out_b), full(wr), full(br),
        ],
        out_specs=[tok(D_MODEL),
                   pl.BlockSpec((SLAB_IN_W // LANES, LOCAL_SLAB_ROWS, LANES), lambda b, i: (0, b * n_t + i, 0)),
                   flat(tm, ROUTE_W), flat(8, ROUTE_W)],
        out_shape=[
            jax.ShapeDtypeStruct((bsz, t, D_MODEL), F32),
            jax.ShapeDtypeStruct((SLAB_IN_W // LANES, bsz * n_t * LOCAL_SLAB_ROWS, LANES), F32),
            jax.ShapeDtypeStruct((bsz * t, ROUTE_W), F32),
            jax.ShapeDtypeStruct((bsz * n_t * 8, ROUTE_W), F32),
        ],
        compiler_params=_params(2),
        name="mix_out",
    )(x, o_f, o_b, sr, ya, mod, lnp, gn, w_out_b, wr, br)


def _moe_kernel(n_chunks, nused_ref, lo_ref, hi_ref, live_ref, src_ref, dst_ref,
                hxs_hbm, w1l_ref, w3l_ref, w2l_ref, w1h_ref, w3h_ref, w2h_ref, out_hbm, gbuf, obuf, gsem, ssem):
    tile_rows = CHUNKS_PER_TILE * SLAB_ROWS
    i = pl.program_id(0)
    n_used = nused_ref[0]
    slot = lax.bitwise_and(i, 1)

    def slab(chunk):
        return pl.ds(pl.multiple_of(chunk * SLAB_ROWS, SLAB_ROWS), SLAB_ROWS)

    def gather_copy(tile, buf_slot, j):
        chunk = src_ref[tile * CHUNKS_PER_TILE + j]
        return pltpu.make_async_copy(hxs_hbm.at[:, slab(chunk), :], gbuf.at[buf_slot, :, slab(j), :],
                                     gsem.at[buf_slot])

    def scatter_copy(tile, buf_slot, j):
        chunk = dst_ref[tile * CHUNKS_PER_TILE + j]
        return pltpu.make_async_copy(obuf.at[buf_slot, :, slab(j), :], out_hbm.at[:, slab(chunk), :],
                                     ssem.at[buf_slot])

    def start_gather(tile, buf_slot):
        for j in range(CHUNKS_PER_TILE):
            gather_copy(tile, buf_slot, j).start(priority=j % 2)

    def wait_gather(buf_slot):
        pltpu.make_async_copy(hxs_hbm.at[:, pl.ds(0, tile_rows), :], gbuf.at[buf_slot], gsem.at[buf_slot]).wait()

    def wait_scatter(buf_slot):
        pltpu.make_async_copy(obuf.at[buf_slot], out_hbm.at[:, pl.ds(0, tile_rows), :], ssem.at[buf_slot]).wait()

    @pl.when(i == 0)
    def _():
        start_gather(0, 0)
        obuf[...] = jnp.zeros(obuf.shape, F32)
        for s in range(2):
            fill = pltpu.make_async_copy(
                obuf.at[s], out_hbm.at[:, pl.ds((n_chunks + s * CHUNKS_PER_TILE) * SLAB_ROWS, tile_rows), :],
                ssem.at[s])
            fill.start()
            fill.wait()

    @pl.when(i + 1 < n_used)
    def _():
        start_gather(i + 1, 1 - slot)

    @pl.when(i < n_used)
    def _():
        wait_gather(slot)

        @pl.when(i >= 2)
        def _():
            wait_scatter(slot)

        @pl.when(live_ref[i] != 0)
        def _():
            def lane_block(c, half):
                return jnp.concatenate(
                    [gbuf[slot, c, pl.ds(2 * r + half, CHUNKS_PER_TILE, stride=SLAB_ROWS), :]
                     for r in range(CHUNK_ROWS)], axis=0)

            n_blk = HALF_W // LANES
            xb = jnp.concatenate([lane_block(c, 0) for c in range(n_blk)]
                                 + [lane_block(c, 1) for c in range(n_blk)], axis=1).astype(BF16)
            rec = lane_block(n_blk, 0)
            w_lo = rec[:, 0:1] + rec[:, 1:2] + rec[:, 2:3]
            w_hi = rec[:, 3:4] + rec[:, 4:5] + rec[:, 5:6]

            gate = [_dot(xb, w1_ref[...]) for w1_ref in (w1l_ref, w1h_ref)]
            up = [_dot(xb, w3_ref[...]) for w3_ref in (w3l_ref, w3h_ref)]
            act = [(_silu(g) * u).astype(BF16) for g, u in zip(gate, up)]
            e_lo, e_hi = [_dot(a, w2_ref[...]) for a, w2_ref in zip(act, (w2l_ref, w2h_ref))]
            y = w_lo * e_lo + w_hi * e_hi
            y = y.astype(BF16).astype(F32)
            for r in range(CHUNK_ROWS):
                rows = slice(r * CHUNKS_PER_TILE, (r + 1) * CHUNKS_PER_TILE)
                for half in range(2):
                    for c in range(n_blk):
                        col = half * HALF_W + c * LANES
                        obuf[slot, c, pl.ds(2 * r + half, CHUNKS_PER_TILE, stride=SLAB_ROWS), :] = (
                            y[rows, col:col + LANES])

        @pl.when(live_ref[i] == 0)
        def _():
            obuf[slot] = jnp.zeros(obuf.shape[1:], F32)

        for j in range(CHUNKS_PER_TILE):
            scatter_copy(i, slot, j).start(priority=j % 2)

        @pl.when(i == n_used - 1)
        def _():
            wait_scatter(slot)

            @pl.when(i >= 1)
            def _():
                wait_scatter(1 - slot)


def _moe(hxs, src, dst, n_used, tile_lo, tile_hi, tile_live, w1_b, w3_b, w2_b):
    n_chunks = hxs.shape[1] // SLAB_ROWS
    tile_rows = CHUNKS_PER_TILE * SLAB_ROWS
    n_steps = src.shape[0] // CHUNKS_PER_TILE
    wspec = lambda which, shape: pl.BlockSpec(
        (None,) + shape, (lambda i, nu, lo, hi, lv, s, d: (lo[i], 0, 0)) if which == 0 else
        (lambda i, nu, lo, hi, lv, s, d: (hi[i], 0, 0)))
    grid_spec = pltpu.PrefetchScalarGridSpec(
        num_scalar_prefetch=6,
        grid=(n_steps,),
        in_specs=[
            pl.BlockSpec(memory_space=pl.ANY),
            wspec(0, (D_MODEL, D_EXPERT)), wspec(0, (D_MODEL, D_EXPERT)), wspec(0, (D_EXPERT, D_MODEL)),
            wspec(1, (D_MODEL, D_EXPERT)), wspec(1, (D_MODEL, D_EXPERT)), wspec(1, (D_EXPERT, D_MODEL)),
        ],
        out_specs=pl.BlockSpec(memory_space=pl.ANY),
        scratch_shapes=[
            pltpu.VMEM((2, SLAB_IN_W // LANES, tile_rows, LANES), F32),
            pltpu.VMEM((2, SLAB_OUT_W // LANES, tile_rows, LANES), F32),
            pltpu.SemaphoreType.DMA((2,)),
            pltpu.SemaphoreType.DMA((2,)),
        ],
    )
    return pl.pallas_call(
        functools.partial(_moe_kernel, n_chunks),
        grid_spec=grid_spec,
        out_shape=jax.ShapeDtypeStruct((SLAB_OUT_W // LANES, (n_chunks + 2 * CHUNKS_PER_TILE) * SLAB_ROWS, LANES), F32),
        compiler_params=_params(1),
        name="moe",
    )(n_used, tile_lo, tile_hi, tile_live, src, dst, hxs, w1_b, w3_b, w2_b, w1_b, w3_b, w2_b)


def _final_kernel(tm, x1_ref, moe_ref, pos_ref, mod_ref, lnp_ref, o_ref):
    moe_b = jnp.concatenate([moe_ref[c] for c in range(SLAB_OUT_W // LANES)], axis=1).astype(BF16)
    sort_lo, sort_hi = _slab_sort_matrices(pos_ref[:, 0:1], tm, 1)
    subs = [slice(s, s + MIX_SUB_TILE) for s in range(0, tm, MIX_SUB_TILE)]
    moe = [jnp.concatenate([_dot(sort_lo[rows, :], moe_b), _dot(sort_hi[rows, :], moe_b)], axis=1) for rows in subs]
    for rows, moe_s in zip(subs, moe):
        o_ref[rows, :] = _layer_norm(DEEPNORM_ALPHA * x1_ref[rows, :] + mod_ref[...] * moe_s,
                                     lnp_ref[0:1, :], lnp_ref[1:2, :])


def _final(x1, moe, pos, g2, lnp):
    bsz, t, _ = x1.shape
    tm = SORT_TILE
    n_t = t // tm
    flat = lambda rows, w: pl.BlockSpec((rows, w), lambda b, i: (b * n_t + i, 0))
    return pl.pallas_call(
        functools.partial(_final_kernel, tm),
        grid=(bsz, n_t),
        in_specs=[
            pl.BlockSpec((None, tm, D_MODEL), lambda b, i: (b, i, 0)),
            pl.BlockSpec((SLAB_OUT_W // LANES, LOCAL_SLAB_ROWS, LANES), lambda b, i: (0, b * n_t + i, 0)),
            flat(tm, ROUTE_W),
            pl.BlockSpec((None, 1, D_MODEL), lambda b, i: (b, 0, 0)),
            pl.BlockSpec(lnp.shape, lambda b, i: (0, 0)),
        ],
        out_specs=pl.BlockSpec((None, tm, D_MODEL), lambda b, i: (b, i, 0)),
        out_shape=jax.ShapeDtypeStruct((bsz, t, D_MODEL), F32),
        compiler_params=_params(2),
        name="final",
    )(x1, moe, pos, g2, lnp)


def _pair_tables():
    lo, hi = [], []
    for g in range(N_GROUPS):
        for a in range(EXPERTS_PER_GROUP):
            for b in range(a + 1, EXPERTS_PER_GROUP):
                lo.append(g * EXPERTS_PER_GROUP + a)
                hi.append(g * EXPERTS_PER_GROUP + b)
    return jnp.array(lo, jnp.int32), jnp.array(hi, jnp.int32)


def _moe_plan(chunks, n_sort_tiles):
    n_cls = N_CLASSES + 1
    hp = lax.Precision.HIGHEST
    m = chunks.reshape(n_sort_tiles, 8, ROUTE_W)[:, 0, :n_cls].astype(jnp.int32)
    a_end = jnp.cumsum(m, axis=0)
    a_start = a_end - m
    per_cls = a_end[-1]
    padded = (per_cls + CHUNKS_PER_TILE - 1) // CHUNKS_PER_TILE * CHUNKS_PER_TILE
    g_end = jnp.cumsum(padded)
    g_start = g_end - padded
    local_off = jnp.cumsum(m, axis=1) - m
    seg = jnp.arange(n_sort_tiles, dtype=jnp.int32)[:, None] * LOCAL_CHUNKS + local_off - a_start
    n_steps = -(-(n_sort_tiles * LOCAL_CHUNKS) // CHUNKS_PER_TILE) + n_cls
    p = jnp.arange(n_steps * CHUNKS_PER_TILE, dtype=jnp.int32)
    cls_p = jnp.minimum(jnp.sum((g_end[None, :] <= p[:, None]).astype(jnp.int32), axis=1), n_cls - 1)
    onehot = (cls_p[:, None] == jnp.arange(n_cls, dtype=jnp.int32)[None, :]).astype(F32)
    pick = lambda tab: jnp.dot(onehot, tab.astype(F32), precision=hp)
    u = p - pick(g_start[:, None])[:, 0].astype(jnp.int32)
    valid = u < pick(per_cls[:, None])[:, 0].astype(jnp.int32)
    a_end_p = pick(a_end.T).astype(jnp.int32)
    seg_p = pick(seg.T).astype(jnp.int32)
    tile_p = jnp.sum((a_end_p <= u[:, None]).astype(jnp.int32), axis=1)
    hit = jnp.arange(n_sort_tiles, dtype=jnp.int32)[None, :] == tile_p[:, None]
    src = jnp.sum(jnp.where(hit, seg_p, 0), axis=1) + u
    pad_dst = n_sort_tiles * LOCAL_CHUNKS + (p // CHUNKS_PER_TILE) % 2 * CHUNKS_PER_TILE + p % CHUNKS_PER_TILE
    dst = jnp.where(valid, src, pad_dst).astype(jnp.int32)
    src = jnp.where(valid, src, 0).astype(jnp.int32)
    n_used = g_end[-1:] // CHUNKS_PER_TILE
    step = jnp.arange(n_steps, dtype=jnp.int32)
    tile_cls = jnp.sum((g_end[None, :] // CHUNKS_PER_TILE <= step[:, None]).astype(jnp.int32), axis=1)
    live = ((tile_cls < N_CLASSES) & (step < n_used[0])).astype(jnp.int32)
    pair_lo, pair_hi = _pair_tables()
    pair_oh = (jnp.minimum(tile_cls, N_CLASSES - 1)[:, None] == jnp.arange(N_CLASSES)[None, :]).astype(jnp.int32)
    tile_lo = jnp.sum(pair_oh * pair_lo[None, :], axis=1).astype(jnp.int32)
    tile_hi = jnp.sum(pair_oh * pair_hi[None, :], axis=1).astype(jnp.int32)
    return src, dst, n_used.astype(jnp.int32), tile_lo, tile_hi, live


def kernel(x, c, ctx, c_ctx, ln_in_g, ln_in_b, w_ada, b_ada, w_in, conv_w, conv_b, gate_w2_fwd, gate_b_fwd,
           gate_w2_bwd, gate_b_bwd, gla_norm_g, w_out, ln1_g, ln1_b, router_group_w, router_group_b,
           router_expert_w, router_expert_b, expert_w1, expert_w3, expert_w2, ln2_g, ln2_b):
    bsz, t, _ = x.shape
    n_tok = bsz * t
    l = 0
    rows = -(-(bsz + 1) // 8) * 8
    cond = jnp.zeros((rows, D_MODEL), F32).at[:bsz].set(c).at[bsz].set(c_ctx)
    ada = _ada(cond, w_ada[l], b_ada[l][None, :])
    sh1, sc1, g1, sh2, sc2, g2 = [ada[:, i * D_MODEL:(i + 1) * D_MODEL] for i in range(6)]

    w_in_b = w_in[l].astype(BF16)
    lnp_in = jnp.stack([ln_in_g, ln_in_b])
    zero = jnp.zeros((GLA_GATE_RANK, GLA_KEY), F32)
    w2cat = jnp.concatenate([jnp.concatenate([gate_w2_fwd[l], zero], axis=1),
                             jnp.concatenate([zero, gate_w2_bwd[l]], axis=1)], axis=0).astype(BF16)
    gbias = jnp.concatenate([gate_b_fwd[l], gate_b_bwd[l]])[None, :]

    mod_ctx = jnp.broadcast_to(jnp.stack([1.0 + sc1[bsz], sh1[bsz]])[None], (bsz, 2, D_MODEL))
    k_c, v_c, g_c = _proj(ctx, mod_ctx, lnp_in, w_in_b, conv_w[l], conv_b[l][None, :], w2cat, gbias, False)
    zero_state = jnp.zeros((bsz, GLA_KEY, PAIR_VAL), F32)
    s_f, s_b = _gla(None, k_c, v_c, g_c, zero_state, zero_state)

    mod1 = jnp.stack([1.0 + sc1[:bsz], sh1[:bsz]], axis=1)
    ya, q, k, v, sr, g = _proj(x, mod1, lnp_in, w_in_b, conv_w[l], conv_b[l][None, :], w2cat, gbias, True)
    o_f, o_b, _, _ = _gla(q, k, v, g, s_f, s_b)

    mod2 = jnp.stack([g1[:bsz], 1.0 + sc2[:bsz], sh2[:bsz]], axis=1)
    lnp1 = jnp.stack([ln_in_g, ln_in_b, ln1_g[l], ln1_b[l]])
    wr = jnp.zeros((D_MODEL, ROUTE_W), F32)
    wr = wr.at[:, :N_GROUPS].set(router_group_w[l]).at[:, N_GROUPS:N_GROUPS + N_EXPERTS].set(router_expert_w[l])
    br = jnp.zeros((1, ROUTE_W), F32)
    br = br.at[0, :N_GROUPS].set(router_group_b[l]).at[0, N_GROUPS:N_GROUPS + N_EXPERTS].set(router_expert_b[l])
    x1, hxs, pos, chunks = _mix_out(x, o_f, o_b, sr, ya, mod2, lnp1, gla_norm_g[l][None, :],
                                    w_out[l].astype(BF16), wr, br)

    src, dst, n_used, tile_lo, tile_hi, live = _moe_plan(chunks, n_tok // SORT_TILE)
    moe = _moe(hxs, src, dst, n_used, tile_lo, tile_hi, live,
               expert_w1[l].astype(BF16), expert_w3[l].astype(BF16), expert_w2[l].astype(BF16))

    return _final(x1, moe, pos, g2[:bsz][:, None, :], jnp.stack([ln2_g[l], ln2_b[l]]))
```

```python
import functools

import jax
import jax.numpy as jnp
from jax import lax
from jax.experimental import pallas as pl
from jax.experimental.pallas import tpu as pltpu

F32 = jnp.float32
BF16 = jnp.bfloat16

D_MODEL = 1024
GRID_W = 64
CONV_CH = 512
GLA_HEADS = 4
GLA_DK = 64
GLA_DV = 128
GLA_KEY = GLA_HEADS * GLA_DK
GLA_VAL = GLA_HEADS * GLA_DV
PAIR_KEY = 2 * GLA_DK
PAIR_VAL = 2 * GLA_DV
GLA_GATE_RANK = 16
GLA_TAU = 16.0
OFF_AB = 0
OFF_AC = OFF_AB + CONV_CH
OFF_AX = OFF_AC + CONV_CH
OFF_Q = OFF_AX + CONV_CH
OFF_K = OFF_Q + GLA_KEY
OFF_V = OFF_K + GLA_KEY
OFF_R = OFF_V + GLA_VAL
OFF_GF = OFF_R + GLA_VAL
D_PROJ = OFF_GF + 2 * GLA_GATE_RANK
N_GROUPS = 4
EXPERTS_PER_GROUP = 4
N_EXPERTS = N_GROUPS * EXPERTS_PER_GROUP
D_EXPERT = 512
PAIRS_PER_GROUP = 6
N_CLASSES = N_GROUPS * PAIRS_PER_GROUP
LN_EPS = 1e-5
RMS_EPS = 1e-6
DEPTH = 1
DEEPNORM_ALPHA = (2.0 * DEPTH) ** 0.25

LANES = 128
GLA_CHUNK = 64
GLA_SUB = 16
N_SUB = GLA_CHUNK // GLA_SUB
ROUTE_W = LANES
HALF_W = D_MODEL // 2
SLAB_IN_W = HALF_W + ROUTE_W
SLAB_OUT_W = HALF_W
SORT_TILE = 256
MIX_SUB_TILE = 128
MOE_TILE = 256
CHUNK_ROWS = 4
SLAB_ROWS = 2 * CHUNK_ROWS
LOCAL_CHUNKS = -(-(SORT_TILE + N_CLASSES * (CHUNK_ROWS - 1)) // CHUNK_ROWS)
LOCAL_SLAB_ROWS = LOCAL_CHUNKS * SLAB_ROWS
CHUNKS_PER_TILE = MOE_TILE // CHUNK_ROWS
VMEM_LIMIT = 56 * 1024 * 1024


def _params(n_axes, vmem=VMEM_LIMIT):
    return pltpu.CompilerParams(dimension_semantics=("arbitrary",) * n_axes, vmem_limit_bytes=vmem)


def _dot(a, b):
    return jnp.dot(a, b, preferred_element_type=F32)


def _div_pow2(x, d):
    assert d & (d - 1) == 0
    return lax.shift_right_logical(x, jnp.int32(d.bit_length() - 1))


def _mod_pow2(x, d):
    assert d & (d - 1) == 0
    return lax.bitwise_and(x, jnp.int32(d - 1))


def _split2(x):
    hi = x.astype(BF16)
    lo = (x - hi.astype(F32)).astype(BF16)
    return hi, lo


def _dot3(a, b):
    ah, al = _split2(a)
    bh, bl = _split2(b)
    return _dot(ah, bh) + _dot(ah, bl) + _dot(al, bh)


def _silu(x):
    return x * (0.5 * jnp.tanh(0.5 * x) + 0.5)


def _layer_norm(x, g, b):
    mu = jnp.mean(x, axis=-1, keepdims=True)
    xc = x - mu
    var = jnp.mean(xc * xc, axis=-1, keepdims=True)
    return xc * lax.rsqrt(var + LN_EPS) * g + b


def _ada_kernel(c_ref, w_ref, b_ref, o_ref):
    o_ref[...] = _dot3(_silu(c_ref[...]), w_ref[...]) + b_ref[...]


def _ada(cond, w_ada, b_ada):
    rows = cond.shape[0]
    n_out = w_ada.shape[1]
    tn = 1024
    return pl.pallas_call(
        _ada_kernel,
        grid=(n_out // tn,),
        in_specs=[
            pl.BlockSpec((rows, D_MODEL), lambda j: (0, 0)),
            pl.BlockSpec((D_MODEL, tn), lambda j: (0, j)),
            pl.BlockSpec((1, tn), lambda j: (0, j)),
        ],
        out_specs=pl.BlockSpec((rows, tn), lambda j: (0, j)),
        out_shape=jax.ShapeDtypeStruct((rows, n_out), F32),
        compiler_params=_params(1),
        name="ada",
    )(cond, w_ada, b_ada)


def _log_sigmoid(z):
    return jnp.minimum(z, 0.0) - jnp.log(1.0 + jnp.exp(-jnp.abs(z)))


def _proj_kernel(latent, tm, x_ref, mod_ref, lnp_ref, w_ref, cw_ref, cb_ref, w2_ref, gbias_ref, *out_refs):
    x = x_ref[...]
    xn = _layer_norm(x, lnp_ref[0:1, :], lnp_ref[1:2, :])
    h = xn * mod_ref[0:1, :] + mod_ref[1:2, :]
    hb = h.astype(BF16)
    if latent:
        ya_ref, q_ref, k_ref, v_ref, sr_ref, g_ref = out_refs
        p = _dot(hb, w_ref[:, OFF_AB:OFF_Q])
        a_b = p[:, 0:CONV_CH]
        u = p[:, CONV_CH:2 * CONV_CH] * p[:, 2 * CONV_CH:3 * CONV_CH]
        pos = _mod_pow2(lax.broadcasted_iota(jnp.int32, (tm, 1), 0), GRID_W)
        u_prev = jnp.where(pos == 0, 0.0, pltpu.roll(u, 1, 0))
        u_next = jnp.where(pos == GRID_W - 1, 0.0, pltpu.roll(u, tm - 1, 0))
        conv = u_prev * cw_ref[0:1, :] + u * cw_ref[1:2, :] + u_next * cw_ref[2:3, :] + cb_ref[...]
        ya_ref[...] = (a_b * conv).astype(BF16)
        qk = _dot(hb, w_ref[:, OFF_Q:OFF_V])
        q_ref[...] = (qk[:, 0:GLA_KEY] * (GLA_DK ** -0.5)).astype(BF16)
        k_ref[...] = qk[:, GLA_KEY:].astype(BF16)
        r = _dot(hb, w_ref[:, OFF_R:OFF_GF])
        sr_ref[...] = _silu(r).astype(BF16)
    else:
        k_ref, v_ref, g_ref = out_refs
        k_ref[...] = _dot(hb, w_ref[:, OFF_K:OFF_V]).astype(BF16)
    v_ref[...] = _dot(hb, w_ref[:, OFF_V:OFF_R]).astype(BF16)
    low = _dot(hb, w_ref[:, OFF_GF:D_PROJ])
    z = _dot(low.astype(BF16), w2_ref[...]) + gbias_ref[...]
    g_ref[...] = _log_sigmoid(z) * (1.0 / GLA_TAU)


def _proj(x, mod, lnp, w_in_b, conv_w, conv_b, w2cat, gbias, latent):
    bsz, t, _ = x.shape
    tm = min(512, t)
    assert t % tm == 0 and tm % GRID_W == 0
    tok = lambda w: pl.BlockSpec((None, tm, w), lambda b, i: (b, i, 0))
    full = lambda a: pl.BlockSpec(a.shape, lambda b, i: (0,) * a.ndim)
    widths = ([(CONV_CH, BF16), (GLA_KEY, BF16)] if latent else []) + [(GLA_KEY, BF16), (GLA_VAL, BF16)]
    widths += ([(GLA_VAL, BF16)] if latent else []) + [(2 * GLA_KEY, F32)]
    return pl.pallas_call(
        functools.partial(_proj_kernel, latent, tm),
        grid=(bsz, t // tm),
        in_specs=[
            tok(D_MODEL),
            pl.BlockSpec((None, 2, D_MODEL), lambda b, i: (b, 0, 0)),
            full(lnp), full(w_in_b), full(conv_w), full(conv_b), full(w2cat), full(gbias),
        ],
        out_specs=[tok(w) for w, _ in widths],
        out_shape=[jax.ShapeDtypeStruct((bsz, t, w), dt) for w, dt in widths],
        compiler_params=_params(2),
        name="proj_latent" if latent else "proj_ctx",
    )(x, mod, lnp, w_in_b, conv_w, conv_b, w2cat, gbias)


def _tri(n, reverse, strict=False):
    i = lax.broadcasted_iota(jnp.int32, (n, n), 0)
    j = lax.broadcasted_iota(jnp.int32, (n, n), 1)
    if strict:
        m = (j > i) if reverse else (j < i)
    else:
        m = (j >= i) if reverse else (j <= i)
    return jnp.where(m, 1.0, 0.0).astype(BF16)


def _chunk_cumsum(g, reverse):
    tri = _tri(GLA_CHUNK, reverse)
    g_hi, g_lo = _split2(g)
    return _dot(tri, g_hi) + _dot(tri, g_lo)


def _as_column(row):
    return jnp.broadcast_to(row, (LANES, row.shape[1])).T


def _sub_anchors(gc, reverse):
    zero = jnp.zeros((1, GLA_KEY), F32)
    if reverse:
        return [gc[GLA_SUB * (a + 1):GLA_SUB * (a + 1) + 1] for a in range(N_SUB - 1)] + [zero]
    return [zero] + [gc[GLA_SUB * a - 1:GLA_SUB * a] for a in range(1, N_SUB)]


def _score_pairs(reverse):
    return [(a, b) for a in range(N_SUB) for b in range(N_SUB) if (b >= a if reverse else b <= a)]


def _intra_products(q, k, gc, reverse):
    r = _sub_anchors(gc, reverse)
    anchor = jnp.concatenate([jnp.broadcast_to(ra, (GLA_SUB, GLA_KEY)) for ra in r], axis=0)
    gcb = gc - anchor
    qt = q * jnp.exp(gcb)
    kt = k * jnp.exp(-gcb)
    rows = []
    for a, b in _score_pairs(reverse):
        qa = qt[GLA_SUB * a:GLA_SUB * (a + 1)]
        if a != b:
            qa = qa * jnp.exp(r[a] - r[b])
        rows.append(qa)
    qp = jnp.concatenate(rows, axis=0).astype(BF16)
    width = GLA_HEADS * GLA_CHUNK
    rr = lax.broadcasted_iota(jnp.int32, (width, GLA_KEY), 0)
    cc = lax.broadcasted_iota(jnp.int32, (width, GLA_KEY), 1)
    kbd = jnp.where(_div_pow2(rr, GLA_CHUNK) == _div_pow2(cc, GLA_DK),
                    jnp.concatenate([kt] * GLA_HEADS, axis=0), 0.0)
    return lax.dot_general(qp, kbd.astype(BF16), (((1,), (1,)), ((), ())), preferred_element_type=F32)


def _assemble_scores(res, reverse):
    pairs = _score_pairs(reverse)
    width = GLA_HEADS * GLA_CHUNK
    col = _mod_pow2(lax.broadcasted_iota(jnp.int32, (GLA_SUB, width), 1), GLA_CHUNK)
    col_blk = _div_pow2(col, GLA_SUB)
    col_pos = _mod_pow2(col, GLA_SUB)
    row_pos = lax.broadcasted_iota(jnp.int32, (GLA_SUB, width), 0)
    causal = (col_pos >= row_pos) if reverse else (col_pos <= row_pos)
    blocks = []
    for a in range(N_SUB):
        acc = jnp.zeros((GLA_SUB, width), F32)
        for idx, (pa, pb) in enumerate(pairs):
            if pa != a:
                continue
            keep = col_blk == pb
            if pa == pb:
                keep = keep & causal
            acc = acc + jnp.where(keep, res[GLA_SUB * idx:GLA_SUB * (idx + 1)], 0.0)
        blocks.append(acc)
    return jnp.concatenate(blocks, axis=0)


def _pair_mask(rows_per_head, cols_per_head, n_row_pairs=1):
    shape = (n_row_pairs * 2 * rows_per_head, 2 * cols_per_head)
    rr = _mod_pow2(lax.broadcasted_iota(jnp.int32, shape, 0), 2 * rows_per_head)
    cc = lax.broadcasted_iota(jnp.int32, shape, 1)
    return _div_pow2(rr, rows_per_head) == _div_pow2(cc, cols_per_head)


def _state_terms(q, k, v_b, gc, reverse):
    total = gc[0:1] if reverse else gc[GLA_CHUNK - 1:GLA_CHUNK]
    q_dec = None if q is None else (q * jnp.exp(gc)).astype(BF16)
    k_end = (k * jnp.exp(total - gc)).astype(BF16)
    tn = (((0,), (0,)), ((), ()))
    upd = [lax.dot_general(k_end[:, p * PAIR_KEY:(p + 1) * PAIR_KEY], v_b[:, p * PAIR_VAL:(p + 1) * PAIR_VAL], tn,
                           preferred_element_type=F32) for p in range(GLA_HEADS // 2)]
    upd = jnp.where(_pair_mask(GLA_DK, GLA_DV, GLA_HEADS // 2), jnp.concatenate(upd, axis=0), 0.0)
    decay = jnp.exp(_as_column(total))
    decay = jnp.concatenate([decay] * (PAIR_VAL // LANES), axis=1)
    return q_dec, decay, upd


def _advance_state(q_dec, decay, upd, state):
    o_inter = None
    if q_dec is not None:
        state_b = state.astype(BF16)
        o_inter = jnp.concatenate(
            [_dot(q_dec[:, p * PAIR_KEY:(p + 1) * PAIR_KEY], state_b[p * PAIR_KEY:(p + 1) * PAIR_KEY, :])
             for p in range(GLA_HEADS // 2)], axis=1)
    return o_inter, state * decay + upd


def _gla_kernel(tt, nt, with_out, *refs):
    if with_out:
        (qf_ref, kf_ref, vf_ref, gfw_ref, qb_ref, kb_ref, vb_ref, gbw_ref, s0f_ref, s0b_ref,
         of_ref, ob_ref, sf_ref, sb_ref) = refs
    else:
        kf_ref, vf_ref, gfw_ref, kb_ref, vb_ref, gbw_ref, s0f_ref, s0b_ref, sf_ref, sb_ref = refs
    j = pl.program_id(1)

    @pl.when(j == 0)
    def _():
        sf_ref[...] = s0f_ref[...]
        sb_ref[...] = s0b_ref[...]

    chunk_slices = [slice(c * GLA_CHUNK, (c + 1) * GLA_CHUNK) for c in range(tt // GLA_CHUNK)]
    fwd, bwd = [], []
    for sl in chunk_slices:
        g = gfw_ref[sl, :]
        fwd.append(dict(sl=sl, k=kf_ref[sl, :].astype(F32), v=vf_ref[sl, :], g_f=g[:, 0:GLA_KEY], g_b=g[:, GLA_KEY:],
                        q=qf_ref[sl, :].astype(F32) if with_out else None))
        bwd.append(dict(sl=sl, k=kb_ref[sl, :].astype(F32), v=vb_ref[sl, :], g_b=gbw_ref[sl, :][:, GLA_KEY:],
                        q=qb_ref[sl, :].astype(F32) if with_out else None))
    for d in fwd:
        d["gc_f"] = _chunk_cumsum(d["g_f"], False)
        if with_out:
            d["gc_b"] = _chunk_cumsum(d["g_b"], True)
    for d in bwd:
        d["gc_b"] = _chunk_cumsum(d["g_b"], True)
    if with_out:
        for d in fwd:
            d["res_f"] = _intra_products(d["q"], d["k"], d["gc_f"], False)
            d["res_b"] = _intra_products(d["q"], d["k"], d["gc_b"], True)
    for d in fwd:
        d["terms"] = _state_terms(d["q"], d["k"], d["v"], d["gc_f"], False)
    for d in bwd:
        d["terms"] = _state_terms(d["q"], d["k"], d["v"], d["gc_b"], True)
    if with_out:
        for d in fwd:
            scores = (_assemble_scores(d["res_f"], False) + _assemble_scores(d["res_b"], True)).astype(BF16)
            o_intra = []
            for p in range(GLA_HEADS // 2):
                v_p = d["v"][:, p * PAIR_VAL:(p + 1) * PAIR_VAL]
                vbd = jnp.where(_pair_mask(GLA_CHUNK, GLA_DV), jnp.concatenate([v_p, v_p], axis=0),
                                jnp.zeros((), BF16))
                o_intra.append(_dot(scores[:, p * 2 * GLA_CHUNK:(p + 1) * 2 * GLA_CHUNK], vbd))
            d["o_intra"] = jnp.concatenate(o_intra, axis=1)

    state = sf_ref[...]
    for d in fwd:
        o_inter, state = _advance_state(*d["terms"], state)
        if with_out:
            of_ref[d["sl"], :] = (d["o_intra"] + o_inter).astype(BF16)
    sf_ref[...] = state

    state = sb_ref[...]
    for d in reversed(bwd):
        o_inter, state = _advance_state(*d["terms"], state)
        if with_out:
            ob_ref[d["sl"], :] = o_inter.astype(BF16)
    sb_ref[...] = state


def _gla(q, k, v, g, s0f, s0b):
    with_out = q is not None
    bsz, t, _ = k.shape
    tt = min(256, t)
    assert t % tt == 0 and tt % GLA_CHUNK == 0
    nt = t // tt
    fwd = lambda w: pl.BlockSpec((None, tt, w), lambda b, j: (b, j, 0))
    bwd = lambda w: pl.BlockSpec((None, tt, w), lambda b, j: (b, nt - 1 - j, 0))
    st = pl.BlockSpec((None, GLA_KEY, PAIR_VAL), lambda b, j: (b, 0, 0))
    st_shape = jax.ShapeDtypeStruct((bsz, GLA_KEY, PAIR_VAL), F32)
    if with_out:
        ins = [q, k, v, g, q, k, v, g, s0f, s0b]
        in_specs = [fwd(GLA_KEY), fwd(GLA_KEY), fwd(GLA_VAL), fwd(2 * GLA_KEY),
                    bwd(GLA_KEY), bwd(GLA_KEY), bwd(GLA_VAL), bwd(2 * GLA_KEY), st, st]
        out_specs = [fwd(GLA_VAL), bwd(GLA_VAL), st, st]
        o_shape = jax.ShapeDtypeStruct((bsz, t, GLA_VAL), BF16)
        out_shape = [o_shape, o_shape, st_shape, st_shape]
    else:
        ins = [k, v, g, k, v, g, s0f, s0b]
        in_specs = [fwd(GLA_KEY), fwd(GLA_VAL), fwd(2 * GLA_KEY),
                    bwd(GLA_KEY), bwd(GLA_VAL), bwd(2 * GLA_KEY), st, st]
        out_specs = [st, st]
        out_shape = [st_shape, st_shape]
    return pl.pallas_call(
        functools.partial(_gla_kernel, tt, nt, with_out),
        grid=(bsz, nt),
        in_specs=in_specs,
        out_specs=out_specs,
        out_shape=out_shape,
        compiler_params=_params(2),
        name="gla_latent" if with_out else "gla_ctx",
    )(*ins)


def _exact_bf16_parts(x):
    hi = x.astype(BF16).astype(F32)
    r = x - hi
    mid = r.astype(BF16).astype(F32)
    lo = (r - mid).astype(BF16).astype(F32)
    return hi, mid, lo


def _first_index(values, best):
    idx = jnp.full_like(best, float(len(values) - 1))
    for i in reversed(range(len(values) - 1)):
        idx = jnp.where(values[i] >= best, float(i), idx)
    return idx


def _pick(rows, idx):
    out = rows[-1]
    for i in reversed(range(len(rows) - 1)):
        out = jnp.where(idx == float(i), rows[i], out)
    return out


def _route(logit_t, tm):
    row = lambda r: logit_t[r:r + 1, :]
    groups = [row(i) for i in range(N_GROUPS)]
    top = functools.reduce(jnp.maximum, groups)
    eg = [jnp.exp(x - top) for x in groups]
    total = functools.reduce(lambda a, b: a + b, eg)
    pg = [e / total for e in eg]
    p_g = functools.reduce(jnp.maximum, pg)
    g_idx = _first_index(pg, p_g)
    sel = [_pick([row(N_GROUPS + EXPERTS_PER_GROUP * g + j) for g in range(N_GROUPS)], g_idx)
           for j in range(EXPERTS_PER_GROUP)]
    top = functools.reduce(jnp.maximum, sel)
    ee = [jnp.exp(x - top) for x in sel]
    total = functools.reduce(lambda a, b: a + b, ee)
    pe = [e / total for e in ee]
    p1 = functools.reduce(jnp.maximum, pe)
    l1 = _first_index(pe, p1)
    pe2 = [jnp.where(l1 == float(j), -1.0, pe[j]) for j in range(EXPERTS_PER_GROUP)]
    p2 = functools.reduce(jnp.maximum, pe2)
    l2 = _first_index(pe2, p2)
    den = p1 + p2
    w1 = p1 / den * p_g
    w2 = p2 / den * p_g
    lo = jnp.minimum(l1, l2)
    hi = jnp.maximum(l1, l2)
    pair = lo * (7.0 - lo) * 0.5 + (hi - lo - 1.0)
    cls = g_idx * PAIRS_PER_GROUP + pair
    w_lo = jnp.where(l1 < l2, w1, w2)
    w_hi = jnp.where(l1 < l2, w2, w1)
    cls_id = lax.broadcasted_iota(jnp.int32, (ROUTE_W, tm), 0).astype(F32)
    onehot = jnp.where(cls_id == cls, 1.0, 0.0)
    before = _dot(onehot.astype(BF16), _tri(tm, True, strict=True))
    count = jnp.sum(onehot, axis=1, keepdims=True)
    chunks = jnp.floor((count + (CHUNK_ROWS - 1.0)) * (1.0 / CHUNK_ROWS))
    chunks = jnp.where(cls_id[:, 0:1] == float(N_CLASSES),
                       LOCAL_CHUNKS - jnp.sum(chunks, axis=0, keepdims=True), chunks)
    first_chunk = _dot(_tri(ROUTE_W, False, strict=True),
                       jnp.broadcast_to(chunks, (ROUTE_W, LANES)).astype(BF16))[:, 0:1]
    pos_row = jnp.sum(onehot * (CHUNK_ROWS * first_chunk + before), axis=0, keepdims=True)
    return pos_row, w_lo, w_hi, chunks


def _slab_sort_matrices(pos, tm, slab_axis):
    shape = (LOCAL_SLAB_ROWS, tm) if slab_axis == 0 else (tm, LOCAL_SLAB_ROWS)
    slab_row = lax.broadcasted_iota(jnp.int32, shape, slab_axis)
    sub = _mod_pow2(slab_row, SLAB_ROWS)
    token_row = CHUNK_ROWS * _div_pow2(slab_row, SLAB_ROWS) + _div_pow2(sub, 2)
    hit = token_row.astype(F32) == pos
    half = _mod_pow2(sub, 2)
    return [jnp.where(hit & (half == h), 1.0, 0.0).astype(BF16) for h in range(2)]


def _mix_out_kernel(tm, x_ref, of_ref, ob_ref, sr_ref, ya_ref, mod_ref, lnp_ref, gn_ref, wo_ref, wr_ref, br_ref,
                    x1_ref, hxs_ref, pos_ref, chunks_ref):
    subs = [slice(s, s + MIX_SUB_TILE) for s in range(0, tm, MIX_SUB_TILE)]
    yb = []
    for rows in subs:
        o = of_ref[rows, :].astype(F32) + ob_ref[rows, :].astype(F32)
        sr = sr_ref[rows, :].astype(F32)
        heads = []
        for h in range(GLA_HEADS):
            sl = slice(h * GLA_DV, (h + 1) * GLA_DV)
            oh = o[:, sl]
            ms = jnp.mean(oh * oh, axis=-1, keepdims=True)
            heads.append((oh * lax.rsqrt(ms + RMS_EPS) * gn_ref[...] * sr[:, sl]).astype(BF16))
        yb.append(heads)
    xn = [_layer_norm(x_ref[rows, :], lnp_ref[0:1, :], lnp_ref[1:2, :]) for rows in subs]
    y = []
    for rows, heads in zip(subs, yb):
        acc = _dot(ya_ref[rows, :], wo_ref[0:CONV_CH, :])
        for h in range(GLA_HEADS):
            acc = acc + _dot(heads[h], wo_ref[CONV_CH + h * GLA_DV:CONV_CH + (h + 1) * GLA_DV, :])
        y.append(acc)
    h2_b = []
    for rows, xn_s, y_s in zip(subs, xn, y):
        x1 = _layer_norm(DEEPNORM_ALPHA * xn_s + mod_ref[0:1, :] * y_s, lnp_ref[2:3, :], lnp_ref[3:4, :])
        x1_ref[rows, :] = x1
        h2_b.append((x1 * mod_ref[1:2, :] + mod_ref[2:3, :]).astype(BF16))
    logit_t = jnp.concatenate([(_dot(h2_s, wr_ref[...]) + br_ref[...]).T for h2_s in h2_b], axis=1)
    h2_b = jnp.concatenate(h2_b, axis=0)
    pos_row, w_lo, w_hi, chunks = _route(logit_t, tm)
    rec_id = lax.broadcasted_iota(jnp.int32, (ROUTE_W, tm), 0)
    rec_t = jnp.zeros((ROUTE_W, tm), F32)
    for i, part in enumerate(_exact_bf16_parts(w_lo) + _exact_bf16_parts(w_hi)):
        rec_t = jnp.where(rec_id == i, part, rec_t)
    rec_b = rec_t.T.astype(BF16)
    sort_lo, sort_hi = _slab_sort_matrices(pos_row, tm, 0)
    pay_lo = jnp.concatenate([h2_b[:, 0:HALF_W], rec_b], axis=1)
    pay_hi = jnp.concatenate([h2_b[:, HALF_W:], jnp.zeros((tm, ROUTE_W), BF16)], axis=1)
    slabs = _dot(sort_lo, pay_lo) + _dot(sort_hi, pay_hi)
    for c in range(SLAB_IN_W // LANES):
        hxs_ref[c] = slabs[:, c * LANES:(c + 1) * LANES]
    pos_ref[...] = jnp.broadcast_to(pos_row, (ROUTE_W, tm)).T
    chunks_ref[...] = jnp.broadcast_to(chunks, (ROUTE_W, LANES)).T[0:chunks_ref.shape[0], :]


def _mix_out(x, o_f, o_b, sr, ya, mod, lnp, gn, w_out_b, wr, br):
    bsz, t, _ = x.shape
    tm = SORT_TILE
    assert t % tm == 0
    n_t = t // tm
    tok = lambda w: pl.BlockSpec((None, tm, w), lambda b, i: (b, i, 0))
    full = lambda a: pl.BlockSpec(a.shape, lambda b, i: (0,) * a.ndim)
    flat = lambda rows, w: pl.BlockSpec((rows, w), lambda b, i: (b * n_t + i, 0))
    return pl.pallas_call(
        functools.partial(_mix_out_kernel, tm),
        grid=(bsz, n_t),
        in_specs=[
            tok(D_MODEL), tok(GLA_VAL), tok(GLA_VAL), tok(GLA_VAL), tok(CONV_CH),
            pl.BlockSpec((None, 3, D_MODEL), lambda b, i: (b, 0, 0)),
            full(lnp), full(gn), full(w_out_b), full(wr), full(br),
        ],
        out_specs=[tok(D_MODEL),
                   pl.BlockSpec((SLAB_IN_W // LANES, LOCAL_SLAB_ROWS, LANES), lambda b, i: (0, b * n_t + i, 0)),
                   flat(tm, ROUTE_W), flat(8, ROUTE_W)],
        out_shape=[
            jax.ShapeDtypeStruct((bsz, t, D_MODEL), F32),
            jax.ShapeDtypeStruct((SLAB_IN_W // LANES, bsz * n_t * LOCAL_SLAB_ROWS, LANES), F32),
            jax.ShapeDtypeStruct((bsz * t, ROUTE_W), F32),
            jax.ShapeDtypeStruct((bsz * n_t * 8, ROUTE_W), F32),
        ],
        compiler_params=_params(2),
        name="mix_out",
    )(x, o_f, o_b, sr, ya, mod, lnp, gn, w_out_b, wr, br)


def _moe_kernel(n_chunks, nused_ref, lo_ref, hi_ref, live_ref, src_ref, dst_ref,
                hxs_hbm, w1l_ref, w3l_ref, w2l_ref, w1h_ref, w3h_ref, w2h_ref, out_hbm, gbuf, obuf, gsem, ssem):
    tile_rows = CHUNKS_PER_TILE * SLAB_ROWS
    i = pl.program_id(0)
    n_used = nused_ref[0]
    slot = lax.bitwise_and(i, 1)

    def slab(chunk):
        return pl.ds(pl.multiple_of(chunk * SLAB_ROWS, SLAB_ROWS), SLAB_ROWS)

    def gather_copy(tile, buf_slot, j):
        chunk = src_ref[tile * CHUNKS_PER_TILE + j]
        return pltpu.make_async_copy(hxs_hbm.at[:, slab(chunk), :], gbuf.at[buf_slot, :, slab(j), :],
                                     gsem.at[buf_slot])

    def scatter_copy(tile, buf_slot, j):
        chunk = dst_ref[tile * CHUNKS_PER_TILE + j]
        return pltpu.make_async_copy(obuf.at[buf_slot, :, slab(j), :], out_hbm.at[:, slab(chunk), :],
                                     ssem.at[buf_slot])

    def start_gather(tile, buf_slot):
        for j in range(CHUNKS_PER_TILE):
            gather_copy(tile, buf_slot, j).start(priority=j % 2)

    def wait_gather(buf_slot):
        pltpu.make_async_copy(hxs_hbm.at[:, pl.ds(0, tile_rows), :], gbuf.at[buf_slot], gsem.at[buf_slot]).wait()

    def wait_scatter(buf_slot):
        pltpu.make_async_copy(obuf.at[buf_slot], out_hbm.at[:, pl.ds(0, tile_rows), :], ssem.at[buf_slot]).wait()

    @pl.when(i == 0)
    def _():
        start_gather(0, 0)
        obuf[...] = jnp.zeros(obuf.shape, F32)
        for s in range(2):
            fill = pltpu.make_async_copy(
                obuf.at[s], out_hbm.at[:, pl.ds((n_chunks + s * CHUNKS_PER_TILE) * SLAB_ROWS, tile_rows), :],
                ssem.at[s])
            fill.start()
            fill.wait()

    @pl.when(i + 1 < n_used)
    def _():
        start_gather(i + 1, 1 - slot)

    @pl.when(i < n_used)
    def _():
        wait_gather(slot)

        @pl.when(i >= 2)
        def _():
            wait_scatter(slot)

        @pl.when(live_ref[i] != 0)
        def _():
            def lane_block(c, half):
                return jnp.concatenate(
                    [gbuf[slot, c, pl.ds(2 * r + half, CHUNKS_PER_TILE, stride=SLAB_ROWS), :]
                     for r in range(CHUNK_ROWS)], axis=0)

            n_blk = HALF_W // LANES
            xb = jnp.concatenate([lane_block(c, 0) for c in range(n_blk)]
                                 + [lane_block(c, 1) for c in range(n_blk)], axis=1).astype(BF16)
            rec = lane_block(n_blk, 0)
            w_lo = rec[:, 0:1] + rec[:, 1:2] + rec[:, 2:3]
            w_hi = rec[:, 3:4] + rec[:, 4:5] + rec[:, 5:6]

            gate = [_dot(xb, w1_ref[...]) for w1_ref in (w1l_ref, w1h_ref)]
            up = [_dot(xb, w3_ref[...]) for w3_ref in (w3l_ref, w3h_ref)]
            act = [(_silu(g) * u).astype(BF16) for g, u in zip(gate, up)]
            e_lo, e_hi = [_dot(a, w2_ref[...]) for a, w2_ref in zip(act, (w2l_ref, w2h_ref))]
            y = w_lo * e_lo + w_hi * e_hi
            y = y.astype(BF16).astype(F32)
            for r in range(CHUNK_ROWS):
                rows = slice(r * CHUNKS_PER_TILE, (r + 1) * CHUNKS_PER_TILE)
                for half in range(2):
                    for c in range(n_blk):
                        col = half * HALF_W + c * LANES
                        obuf[slot, c, pl.ds(2 * r + half, CHUNKS_PER_TILE, stride=SLAB_ROWS), :] = (
                            y[rows, col:col + LANES])

        @pl.when(live_ref[i] == 0)
        def _():
            obuf[slot] = jnp.zeros(obuf.shape[1:], F32)

        for j in range(CHUNKS_PER_TILE):
            scatter_copy(i, slot, j).start(priority=j % 2)

        @pl.when(i == n_used - 1)
        def _():
            wait_scatter(slot)

            @pl.when(i >= 1)
            def _():
                wait_scatter(1 - slot)


def _moe(hxs, src, dst, n_used, tile_lo, tile_hi, tile_live, w1_b, w3_b, w2_b):
    n_chunks = hxs.shape[1] // SLAB_ROWS
    tile_rows = CHUNKS_PER_TILE * SLAB_ROWS
    n_steps = src.shape[0] // CHUNKS_PER_TILE
    wspec = lambda which, shape: pl.BlockSpec(
        (None,) + shape, (lambda i, nu, lo, hi, lv, s, d: (lo[i], 0, 0)) if which == 0 else
        (lambda i, nu, lo, hi, lv, s, d: (hi[i], 0, 0)))
    grid_spec = pltpu.PrefetchScalarGridSpec(
        num_scalar_prefetch=6,
        grid=(n_steps,),
        in_specs=[
            pl.BlockSpec(memory_space=pl.ANY),
            wspec(0, (D_MODEL, D_EXPERT)), wspec(0, (D_MODEL, D_EXPERT)), wspec(0, (D_EXPERT, D_MODEL)),
            wspec(1, (D_MODEL, D_EXPERT)), wspec(1, (D_MODEL, D_EXPERT)), wspec(1, (D_EXPERT, D_MODEL)),
        ],
        out_specs=pl.BlockSpec(memory_space=pl.ANY),
        scratch_shapes=[
            pltpu.VMEM((2, SLAB_IN_W // LANES, tile_rows, LANES), F32),
            pltpu.VMEM((2, SLAB_OUT_W // LANES, tile_rows, LANES), F32),
            pltpu.SemaphoreType.DMA((2,)),
            pltpu.SemaphoreType.DMA((2,)),
        ],
    )
    return pl.pallas_call(
        functools.partial(_moe_kernel, n_chunks),
        grid_spec=grid_spec,
        out_shape=jax.ShapeDtypeStruct((SLAB_OUT_W // LANES, (n_chunks + 2 * CHUNKS_PER_TILE) * SLAB_ROWS, LANES), F32),
        compiler_params=_params(1),
        name="moe",
    )(n_used, tile_lo, tile_hi, tile_live, src, dst, hxs, w1_b, w3_b, w2_b, w1_b, w3_b, w2_b)


def _final_kernel(tm, x1_ref, moe_ref, pos_ref, mod_ref, lnp_ref, o_ref):
    moe_b = jnp.concatenate([moe_ref[c] for c in range(SLAB_OUT_W // LANES)], axis=1).astype(BF16)
    sort_lo, sort_hi = _slab_sort_matrices(pos_ref[:, 0:1], tm, 1)
    subs = [slice(s, s + MIX_SUB_TILE) for s in range(0, tm, MIX_SUB_TILE)]
    moe = [jnp.concatenate([_dot(sort_lo[rows, :], moe_b), _dot(sort_hi[rows, :], moe_b)], axis=1) for rows in subs]
    for rows, moe_s in zip(subs, moe):
        o_ref[rows, :] = _layer_norm(DEEPNORM_ALPHA * x1_ref[rows, :] + mod_ref[...] * moe_s,
                                     lnp_ref[0:1, :], lnp_ref[1:2, :])


def _final(x1, moe, pos, g2, lnp):
    bsz, t, _ = x1.shape
    tm = SORT_TILE
    n_t = t // tm
    flat = lambda rows, w: pl.BlockSpec((rows, w), lambda b, i: (b * n_t + i, 0))
    return pl.pallas_call(
        functools.partial(_final_kernel, tm),
        grid=(bsz, n_t),
        in_specs=[
            pl.BlockSpec((None, tm, D_MODEL), lambda b, i: (b, i, 0)),
            pl.BlockSpec((SLAB_OUT_W // LANES, LOCAL_SLAB_ROWS, LANES), lambda b, i: (0, b * n_t + i, 0)),
            flat(tm, ROUTE_W),
            pl.BlockSpec((None, 1, D_MODEL), lambda b, i: (b, 0, 0)),
            pl.BlockSpec(lnp.shape, lambda b, i: (0, 0)),
        ],
        out_specs=pl.BlockSpec((None, tm, D_MODEL), lambda b, i: (b, i, 0)),
        out_shape=jax.ShapeDtypeStruct((bsz, t, D_MODEL), F32),
        compiler_params=_params(2),
        name="final",
    )(x1, moe, pos, g2, lnp)


def _pair_tables():
    lo, hi = [], []
    for g in range(N_GROUPS):
        for a in range(EXPERTS_PER_GROUP):
            for b in range(a + 1, EXPERTS_PER_GROUP):
                lo.append(g * EXPERTS_PER_GROUP + a)
                hi.append(g * EXPERTS_PER_GROUP + b)
    return jnp.array(lo, jnp.int32), jnp.array(hi, jnp.int32)


def _moe_plan(chunks, n_sort_tiles):
    n_cls = N_CLASSES + 1
    hp = lax.Precision.HIGHEST
    m = chunks.reshape(n_sort_tiles, 8, ROUTE_W)[:, 0, :n_cls].astype(jnp.int32)
    a_end = jnp.cumsum(m, axis=0)
    a_start = a_end - m
    per_cls = a_end[-1]
    padded = (per_cls + CHUNKS_PER_TILE - 1) // CHUNKS_PER_TILE * CHUNKS_PER_TILE
    g_end = jnp.cumsum(padded)
    g_start = g_end - padded
    local_off = jnp.cumsum(m, axis=1) - m
    seg = jnp.arange(n_sort_tiles, dtype=jnp.int32)[:, None] * LOCAL_CHUNKS + local_off - a_start
    n_steps = -(-(n_sort_tiles * LOCAL_CHUNKS) // CHUNKS_PER_TILE) + n_cls
    p = jnp.arange(n_steps * CHUNKS_PER_TILE, dtype=jnp.int32)
    cls_p = jnp.minimum(jnp.sum((g_end[None, :] <= p[:, None]).astype(jnp.int32), axis=1), n_cls - 1)
    onehot = (cls_p[:, None] == jnp.arange(n_cls, dtype=jnp.int32)[None, :]).astype(F32)
    pick = lambda tab: jnp.dot(onehot, tab.astype(F32), precision=hp)
    u = p - pick(g_start[:, None])[:, 0].astype(jnp.int32)
    valid = u < pick(per_cls[:, None])[:, 0].astype(jnp.int32)
    a_end_p = pick(a_end.T).astype(jnp.int32)
    seg_p = pick(seg.T).astype(jnp.int32)
    tile_p = jnp.sum((a_end_p <= u[:, None]).astype(jnp.int32), axis=1)
    hit = jnp.arange(n_sort_tiles, dtype=jnp.int32)[None, :] == tile_p[:, None]
    src = jnp.sum(jnp.where(hit, seg_p, 0), axis=1) + u
    pad_dst = n_sort_tiles * LOCAL_CHUNKS + (p // CHUNKS_PER_TILE) % 2 * CHUNKS_PER_TILE + p % CHUNKS_PER_TILE
    dst = jnp.where(valid, src, pad_dst).astype(jnp.int32)
    src = jnp.where(valid, src, 0).astype(jnp.int32)
    n_used = g_end[-1:] // CHUNKS_PER_TILE
    step = jnp.arange(n_steps, dtype=jnp.int32)
    tile_cls = jnp.sum((g_end[None, :] // CHUNKS_PER_TILE <= step[:, None]).astype(jnp.int32), axis=1)
    live = ((tile_cls < N_CLASSES) & (step < n_used[0])).astype(jnp.int32)
    pair_lo, pair_hi = _pair_tables()
    pair_oh = (jnp.minimum(tile_cls, N_CLASSES - 1)[:, None] == jnp.arange(N_CLASSES)[None, :]).astype(jnp.int32)
    tile_lo = jnp.sum(pair_oh * pair_lo[None, :], axis=1).astype(jnp.int32)
    tile_hi = jnp.sum(pair_oh * pair_hi[None, :], axis=1).astype(jnp.int32)
    return src, dst, n_used.astype(jnp.int32), tile_lo, tile_hi, live


def kernel(x, c, ctx, c_ctx, ln_in_g, ln_in_b, w_ada, b_ada, w_in, conv_w, conv_b, gate_w2_fwd, gate_b_fwd,
           gate_w2_bwd, gate_b_bwd, gla_norm_g, w_out, ln1_g, ln1_b, router_group_w, router_group_b,
           router_expert_w, router_expert_b, expert_w1, expert_w3, expert_w2, ln2_g, ln2_b):
    bsz, t, _ = x.shape
    n_tok = bsz * t
    l = 0
    rows = -(-(bsz + 1) // 8) * 8
    cond = jnp.zeros((rows, D_MODEL), F32).at[:bsz].set(c).at[bsz].set(c_ctx)
    ada = _ada(cond, w_ada[l], b_ada[l][None, :])
    sh1, sc1, g1, sh2, sc2, g2 = [ada[:, i * D_MODEL:(i + 1) * D_MODEL] for i in range(6)]

    w_in_b = w_in[l].astype(BF16)
    lnp_in = jnp.stack([ln_in_g, ln_in_b])
    zero = jnp.zeros((GLA_GATE_RANK, GLA_KEY), F32)
    w2cat = jnp.concatenate([jnp.concatenate([gate_w2_fwd[l], zero], axis=1),
                             jnp.concatenate([zero, gate_w2_bwd[l]], axis=1)], axis=0).astype(BF16)
    gbias = jnp.concatenate([gate_b_fwd[l], gate_b_bwd[l]])[None, :]

    mod_ctx = jnp.broadcast_to(jnp.stack([1.0 + sc1[bsz], sh1[bsz]])[None], (bsz, 2, D_MODEL))
    k_c, v_c, g_c = _proj(ctx, mod_ctx, lnp_in, w_in_b, conv_w[l], conv_b[l][None, :], w2cat, gbias, False)
    zero_state = jnp.zeros((bsz, GLA_KEY, PAIR_VAL), F32)
    s_f, s_b = _gla(None, k_c, v_c, g_c, zero_state, zero_state)

    mod1 = jnp.stack([1.0 + sc1[:bsz], sh1[:bsz]], axis=1)
    ya, q, k, v, sr, g = _proj(x, mod1, lnp_in, w_in_b, conv_w[l], conv_b[l][None, :], w2cat, gbias, True)
    o_f, o_b, _, _ = _gla(q, k, v, g, s_f, s_b)

    mod2 = jnp.stack([g1[:bsz], 1.0 + sc2[:bsz], sh2[:bsz]], axis=1)
    lnp1 = jnp.stack([ln_in_g, ln_in_b, ln1_g[l], ln1_b[l]])
    wr = jnp.zeros((D_MODEL, ROUTE_W), F32)
    wr = wr.at[:, :N_GROUPS].set(router_group_w[l]).at[:, N_GROUPS:N_GROUPS + N_EXPERTS].set(router_expert_w[l])
    br = jnp.zeros((1, ROUTE_W), F32)
    br = br.at[0, :N_GROUPS].set(router_group_b[l]).at[0, N_GROUPS:N_GROUPS + N_EXPERTS].set(router_expert_b[l])
    x1, hxs, pos, chunks = _mix_out(x, o_f, o_b, sr, ya, mod2, lnp1, gla_norm_g[l][None, :],
                                    w_out[l].astype(BF16), wr.astype(BF16), br)

    src, dst, n_used, tile_lo, tile_hi, live = _moe_plan(chunks, n_tok // SORT_TILE)
    moe = _moe(hxs, src, dst, n_used, tile_lo, tile_hi, live,
               expert_w1[l].astype(BF16), expert_w3[l].astype(BF16), expert_w2[l].astype(BF16))

    return _final(x1, moe, pos, g2[:bsz][:, None, :], jnp.stack([ln2_g[l], ln2_b[l]]))
```

```python
import functools

import jax
import jax.numpy as jnp
from jax import lax
from jax.experimental import pallas as pl
from jax.experimental.pallas import tpu as pltpu

F32 = jnp.float32
BF16 = jnp.bfloat16

D_MODEL = 1024
GRID_W = 64
CONV_CH = 512
GLA_HEADS = 4
GLA_DK = 64
GLA_DV = 128
GLA_KEY = GLA_HEADS * GLA_DK
GLA_VAL = GLA_HEADS * GLA_DV
PAIR_KEY = 2 * GLA_DK
PAIR_VAL = 2 * GLA_DV
GLA_GATE_RANK = 16
GLA_TAU = 16.0
OFF_AB = 0
OFF_AC = OFF_AB + CONV_CH
OFF_AX = OFF_AC + CONV_CH
OFF_Q = OFF_AX + CONV_CH
OFF_K = OFF_Q + GLA_KEY
OFF_V = OFF_K + GLA_KEY
OFF_R = OFF_V + GLA_VAL
OFF_GF = OFF_R + GLA_VAL
D_PROJ = OFF_GF + 2 * GLA_GATE_RANK
N_GROUPS = 4
EXPERTS_PER_GROUP = 4
N_EXPERTS = N_GROUPS * EXPERTS_PER_GROUP
D_EXPERT = 512
PAIRS_PER_GROUP = 6
N_CLASSES = N_GROUPS * PAIRS_PER_GROUP
LN_EPS = 1e-5
RMS_EPS = 1e-6
DEPTH = 1
DEEPNORM_ALPHA = (2.0 * DEPTH) ** 0.25

LANES = 128
GLA_CHUNK = 64
GLA_SUB = 16
N_SUB = GLA_CHUNK // GLA_SUB
ROUTE_W = LANES
HALF_W = D_MODEL // 2
SLAB_IN_W = HALF_W + ROUTE_W
SLAB_OUT_W = HALF_W
SORT_TILE = 256
MIX_SUB_TILE = 128
MOE_TILE = 256
CHUNK_ROWS = 4
SLAB_ROWS = 2 * CHUNK_ROWS
LOCAL_CHUNKS = -(-(SORT_TILE + N_CLASSES * (CHUNK_ROWS - 1)) // CHUNK_ROWS)
LOCAL_SLAB_ROWS = LOCAL_CHUNKS * SLAB_ROWS
CHUNKS_PER_TILE = MOE_TILE // CHUNK_ROWS
VMEM_LIMIT = 56 * 1024 * 1024


def _params(n_axes, vmem=VMEM_LIMIT):
    return pltpu.CompilerParams(dimension_semantics=("arbitrary",) * n_axes, vmem_limit_bytes=vmem)


def _dot(a, b):
    return jnp.dot(a, b, preferred_element_type=F32)


def _div_pow2(x, d):
    assert d & (d - 1) == 0
    return lax.shift_right_logical(x, jnp.int32(d.bit_length() - 1))


def _mod_pow2(x, d):
    assert d & (d - 1) == 0
    return lax.bitwise_and(x, jnp.int32(d - 1))


def _split2(x):
    hi = x.astype(BF16)
    lo = (x - hi.astype(F32)).astype(BF16)
    return hi, lo


def _dot3(a, b):
    ah, al = _split2(a)
    bh, bl = _split2(b)
    return _dot(ah, bh) + _dot(ah, bl) + _dot(al, bh)


def _silu(x):
    return x * (0.5 * jnp.tanh(0.5 * x) + 0.5)


def _layer_norm(x, g, b):
    mu = jnp.mean(x, axis=-1, keepdims=True)
    xc = x - mu
    var = jnp.mean(xc * xc, axis=-1, keepdims=True)
    return xc * lax.rsqrt(var + LN_EPS) * g + b


def _ada_kernel(c_ref, w_ref, b_ref, o_ref):
    o_ref[...] = _dot3(_silu(c_ref[...]), w_ref[...]) + b_ref[...]


def _ada(cond, w_ada, b_ada):
    rows = cond.shape[0]
    n_out = w_ada.shape[1]
    tn = 1024
    return pl.pallas_call(
        _ada_kernel,
        grid=(n_out // tn,),
        in_specs=[
            pl.BlockSpec((rows, D_MODEL), lambda j: (0, 0)),
            pl.BlockSpec((D_MODEL, tn), lambda j: (0, j)),
            pl.BlockSpec((1, tn), lambda j: (0, j)),
        ],
        out_specs=pl.BlockSpec((rows, tn), lambda j: (0, j)),
        out_shape=jax.ShapeDtypeStruct((rows, n_out), F32),
        compiler_params=_params(1),
        name="ada",
    )(cond, w_ada, b_ada)


def _log_sigmoid(z):
    return jnp.minimum(z, 0.0) - jnp.log(1.0 + jnp.exp(-jnp.abs(z)))


def _proj_kernel(latent, tm, x_ref, mod_ref, lnp_ref, w_ref, cw_ref, cb_ref, w2_ref, gbias_ref, *out_refs):
    x = x_ref[...]
    xn = _layer_norm(x, lnp_ref[0:1, :], lnp_ref[1:2, :])
    h = xn * mod_ref[0:1, :] + mod_ref[1:2, :]
    hb = h.astype(BF16)
    if latent:
        ya_ref, q_ref, k_ref, v_ref, sr_ref, g_ref = out_refs
        p = _dot(hb, w_ref[:, OFF_AB:OFF_Q])
        a_b = p[:, 0:CONV_CH]
        u = p[:, CONV_CH:2 * CONV_CH] * p[:, 2 * CONV_CH:3 * CONV_CH]
        pos = _mod_pow2(lax.broadcasted_iota(jnp.int32, (tm, 1), 0), GRID_W)
        u_prev = jnp.where(pos == 0, 0.0, pltpu.roll(u, 1, 0))
        u_next = jnp.where(pos == GRID_W - 1, 0.0, pltpu.roll(u, tm - 1, 0))
        conv = u_prev * cw_ref[0:1, :] + u * cw_ref[1:2, :] + u_next * cw_ref[2:3, :] + cb_ref[...]
        ya_ref[...] = (a_b * conv).astype(BF16)
        qk = _dot(hb, w_ref[:, OFF_Q:OFF_V])
        q_ref[...] = (qk[:, 0:GLA_KEY] * (GLA_DK ** -0.5)).astype(BF16)
        k_ref[...] = qk[:, GLA_KEY:].astype(BF16)
        r = _dot(hb, w_ref[:, OFF_R:OFF_GF])
        sr_ref[...] = _silu(r).astype(BF16)
    else:
        k_ref, v_ref, g_ref = out_refs
        k_ref[...] = _dot(hb, w_ref[:, OFF_K:OFF_V]).astype(BF16)
    v_ref[...] = _dot(hb, w_ref[:, OFF_V:OFF_R]).astype(BF16)
    low = _dot(hb, w_ref[:, OFF_GF:D_PROJ])
    z = _dot(low.astype(BF16), w2_ref[...]) + gbias_ref[...]
    g_ref[...] = _log_sigmoid(z) * (1.0 / GLA_TAU)


def _proj(x, mod, lnp, w_in_b, conv_w, conv_b, w2cat, gbias, latent):
    bsz, t, _ = x.shape
    tm = min(512, t)
    assert t % tm == 0 and tm % GRID_W == 0
    tok = lambda w: pl.BlockSpec((None, tm, w), lambda b, i: (b, i, 0))
    full = lambda a: pl.BlockSpec(a.shape, lambda b, i: (0,) * a.ndim)
    widths = ([(CONV_CH, BF16), (GLA_KEY, BF16)] if latent else []) + [(GLA_KEY, BF16), (GLA_VAL, BF16)]
    widths += ([(GLA_VAL, BF16)] if latent else []) + [(2 * GLA_KEY, F32)]
    return pl.pallas_call(
        functools.partial(_proj_kernel, latent, tm),
        grid=(bsz, t // tm),
        in_specs=[
            tok(D_MODEL),
            pl.BlockSpec((None, 2, D_MODEL), lambda b, i: (b, 0, 0)),
            full(lnp), full(w_in_b), full(conv_w), full(conv_b), full(w2cat), full(gbias),
        ],
        out_specs=[tok(w) for w, _ in widths],
        out_shape=[jax.ShapeDtypeStruct((bsz, t, w), dt) for w, dt in widths],
        compiler_params=_params(2),
        name="proj_latent" if latent else "proj_ctx",
    )(x, mod, lnp, w_in_b, conv_w, conv_b, w2cat, gbias)


def _tri(n, reverse, strict=False):
    i = lax.broadcasted_iota(jnp.int32, (n, n), 0)
    j = lax.broadcasted_iota(jnp.int32, (n, n), 1)
    if strict:
        m = (j > i) if reverse else (j < i)
    else:
        m = (j >= i) if reverse else (j <= i)
    return jnp.where(m, 1.0, 0.0).astype(BF16)


def _chunk_cumsum(g, reverse):
    tri = _tri(GLA_CHUNK, reverse)
    g_hi, g_lo = _split2(g)
    return _dot(tri, g_hi) + _dot(tri, g_lo)


def _as_column(row):
    return jnp.broadcast_to(row, (LANES, row.shape[1])).T


def _sub_anchors(gc, reverse):
    zero = jnp.zeros((1, GLA_KEY), F32)
    if reverse:
        return [gc[GLA_SUB * (a + 1):GLA_SUB * (a + 1) + 1] for a in range(N_SUB - 1)] + [zero]
    return [zero] + [gc[GLA_SUB * a - 1:GLA_SUB * a] for a in range(1, N_SUB)]


def _score_pairs(reverse):
    return [(a, b) for a in range(N_SUB) for b in range(N_SUB) if (b >= a if reverse else b <= a)]


def _intra_products(q, k, gc, reverse):
    r = _sub_anchors(gc, reverse)
    anchor = jnp.concatenate([jnp.broadcast_to(ra, (GLA_SUB, GLA_KEY)) for ra in r], axis=0)
    gcb = gc - anchor
    qt = q * jnp.exp(gcb)
    kt = k * jnp.exp(-gcb)
    rows = []
    for a, b in _score_pairs(reverse):
        qa = qt[GLA_SUB * a:GLA_SUB * (a + 1)]
        if a != b:
            qa = qa * jnp.exp(r[a] - r[b])
        rows.append(qa)
    qp = jnp.concatenate(rows, axis=0).astype(BF16)
    width = GLA_HEADS * GLA_CHUNK
    rr = lax.broadcasted_iota(jnp.int32, (width, GLA_KEY), 0)
    cc = lax.broadcasted_iota(jnp.int32, (width, GLA_KEY), 1)
    kbd = jnp.where(_div_pow2(rr, GLA_CHUNK) == _div_pow2(cc, GLA_DK),
                    jnp.concatenate([kt] * GLA_HEADS, axis=0), 0.0)
    return lax.dot_general(qp, kbd.astype(BF16), (((1,), (1,)), ((), ())), preferred_element_type=F32)


def _assemble_scores(res, reverse):
    pairs = _score_pairs(reverse)
    width = GLA_HEADS * GLA_CHUNK
    col = _mod_pow2(lax.broadcasted_iota(jnp.int32, (GLA_SUB, width), 1), GLA_CHUNK)
    col_blk = _div_pow2(col, GLA_SUB)
    col_pos = _mod_pow2(col, GLA_SUB)
    row_pos = lax.broadcasted_iota(jnp.int32, (GLA_SUB, width), 0)
    causal = (col_pos >= row_pos) if reverse else (col_pos <= row_pos)
    blocks = []
    for a in range(N_SUB):
        acc = jnp.zeros((GLA_SUB, width), F32)
        for idx, (pa, pb) in enumerate(pairs):
            if pa != a:
                continue
            keep = col_blk == pb
            if pa == pb:
                keep = keep & causal
            acc = acc + jnp.where(keep, res[GLA_SUB * idx:GLA_SUB * (idx + 1)], 0.0)
        blocks.append(acc)
    return jnp.concatenate(blocks, axis=0)


def _pair_mask(rows_per_head, cols_per_head, n_row_pairs=1):
    shape = (n_row_pairs * 2 * rows_per_head, 2 * cols_per_head)
    rr = _mod_pow2(lax.broadcasted_iota(jnp.int32, shape, 0), 2 * rows_per_head)
    cc = lax.broadcasted_iota(jnp.int32, shape, 1)
    return _div_pow2(rr, rows_per_head) == _div_pow2(cc, cols_per_head)


def _state_terms(q, k, v_b, gc, reverse):
    total = gc[0:1] if reverse else gc[GLA_CHUNK - 1:GLA_CHUNK]
    q_dec = None if q is None else (q * jnp.exp(gc)).astype(BF16)
    k_end = (k * jnp.exp(total - gc)).astype(BF16)
    tn = (((0,), (0,)), ((), ()))
    upd = [lax.dot_general(k_end[:, p * PAIR_KEY:(p + 1) * PAIR_KEY], v_b[:, p * PAIR_VAL:(p + 1) * PAIR_VAL], tn,
                           preferred_element_type=F32) for p in range(GLA_HEADS // 2)]
    upd = jnp.where(_pair_mask(GLA_DK, GLA_DV, GLA_HEADS // 2), jnp.concatenate(upd, axis=0), 0.0)
    decay = jnp.exp(_as_column(total))
    decay = jnp.concatenate([decay] * (PAIR_VAL // LANES), axis=1)
    return q_dec, decay, upd


def _advance_state(q_dec, decay, upd, state):
    o_inter = None
    if q_dec is not None:
        state_b = state.astype(BF16)
        o_inter = jnp.concatenate(
            [_dot(q_dec[:, p * PAIR_KEY:(p + 1) * PAIR_KEY], state_b[p * PAIR_KEY:(p + 1) * PAIR_KEY, :])
             for p in range(GLA_HEADS // 2)], axis=1)
    return o_inter, state * decay + upd


def _gla_kernel(tt, nt, with_out, *refs):
    if with_out:
        (qf_ref, kf_ref, vf_ref, gfw_ref, qb_ref, kb_ref, vb_ref, gbw_ref, s0f_ref, s0b_ref,
         of_ref, ob_ref, sf_ref, sb_ref) = refs
    else:
        kf_ref, vf_ref, gfw_ref, kb_ref, vb_ref, gbw_ref, s0f_ref, s0b_ref, sf_ref, sb_ref = refs
    j = pl.program_id(1)

    @pl.when(j == 0)
    def _():
        sf_ref[...] = s0f_ref[...]
        sb_ref[...] = s0b_ref[...]

    chunk_slices = [slice(c * GLA_CHUNK, (c + 1) * GLA_CHUNK) for c in range(tt // GLA_CHUNK)]
    fwd, bwd = [], []
    for sl in chunk_slices:
        g = gfw_ref[sl, :]
        fwd.append(dict(sl=sl, k=kf_ref[sl, :].astype(F32), v=vf_ref[sl, :], g_f=g[:, 0:GLA_KEY], g_b=g[:, GLA_KEY:],
                        q=qf_ref[sl, :].astype(F32) if with_out else None))
        bwd.append(dict(sl=sl, k=kb_ref[sl, :].astype(F32), v=vb_ref[sl, :], g_b=gbw_ref[sl, :][:, GLA_KEY:],
                        q=qb_ref[sl, :].astype(F32) if with_out else None))
    for d in fwd:
        d["gc_f"] = _chunk_cumsum(d["g_f"], False)
        if with_out:
            d["gc_b"] = _chunk_cumsum(d["g_b"], True)
    for d in bwd:
        d["gc_b"] = _chunk_cumsum(d["g_b"], True)
    if with_out:
        for d in fwd:
            d["res_f"] = _intra_products(d["q"], d["k"], d["gc_f"], False)
            d["res_b"] = _intra_products(d["q"], d["k"], d["gc_b"], True)
    for d in fwd:
        d["terms"] = _state_terms(d["q"], d["k"], d["v"], d["gc_f"], False)
    for d in bwd:
        d["terms"] = _state_terms(d["q"], d["k"], d["v"], d["gc_b"], True)
    if with_out:
        for d in fwd:
            scores = (_assemble_scores(d["res_f"], False) + _assemble_scores(d["res_b"], True)).astype(BF16)
            o_intra = []
            for p in range(GLA_HEADS // 2):
                v_p = d["v"][:, p * PAIR_VAL:(p + 1) * PAIR_VAL]
                vbd = jnp.where(_pair_mask(GLA_CHUNK, GLA_DV), jnp.concatenate([v_p, v_p], axis=0),
                                jnp.zeros((), BF16))
                o_intra.append(_dot(scores[:, p * 2 * GLA_CHUNK:(p + 1) * 2 * GLA_CHUNK], vbd))
            d["o_intra"] = jnp.concatenate(o_intra, axis=1)

    state = sf_ref[...]
    for d in fwd:
        o_inter, state = _advance_state(*d["terms"], state)
        if with_out:
            of_ref[d["sl"], :] = (d["o_intra"] + o_inter).astype(BF16)
    sf_ref[...] = state

    state = sb_ref[...]
    for d in reversed(bwd):
        o_inter, state = _advance_state(*d["terms"], state)
        if with_out:
            ob_ref[d["sl"], :] = o_inter.astype(BF16)
    sb_ref[...] = state


def _gla(q, k, v, g, s0f, s0b):
    with_out = q is not None
    bsz, t, _ = k.shape
    tt = min(512, t)
    assert t % tt == 0 and tt % GLA_CHUNK == 0
    nt = t // tt
    fwd = lambda w: pl.BlockSpec((None, tt, w), lambda b, j: (b, j, 0))
    bwd = lambda w: pl.BlockSpec((None, tt, w), lambda b, j: (b, nt - 1 - j, 0))
    st = pl.BlockSpec((None, GLA_KEY, PAIR_VAL), lambda b, j: (b, 0, 0))
    st_shape = jax.ShapeDtypeStruct((bsz, GLA_KEY, PAIR_VAL), F32)
    if with_out:
        ins = [q, k, v, g, q, k, v, g, s0f, s0b]
        in_specs = [fwd(GLA_KEY), fwd(GLA_KEY), fwd(GLA_VAL), fwd(2 * GLA_KEY),
                    bwd(GLA_KEY), bwd(GLA_KEY), bwd(GLA_VAL), bwd(2 * GLA_KEY), st, st]
        out_specs = [fwd(GLA_VAL), bwd(GLA_VAL), st, st]
        o_shape = jax.ShapeDtypeStruct((bsz, t, GLA_VAL), BF16)
        out_shape = [o_shape, o_shape, st_shape, st_shape]
    else:
        ins = [k, v, g, k, v, g, s0f, s0b]
        in_specs = [fwd(GLA_KEY), fwd(GLA_VAL), fwd(2 * GLA_KEY),
                    bwd(GLA_KEY), bwd(GLA_VAL), bwd(2 * GLA_KEY), st, st]
        out_specs = [st, st]
        out_shape = [st_shape, st_shape]
    return pl.pallas_call(
        functools.partial(_gla_kernel, tt, nt, with_out),
        grid=(bsz, nt),
        in_specs=in_specs,
        out_specs=out_specs,
        out_shape=out_shape,
        compiler_params=_params(2),
        name="gla_latent" if with_out else "gla_ctx",
    )(*ins)


def _exact_bf16_parts(x):
    hi = x.astype(BF16).astype(F32)
    r = x - hi
    mid = r.astype(BF16).astype(F32)
    lo = (r - mid).astype(BF16).astype(F32)
    return hi, mid, lo


def _first_index(values, best):
    idx = jnp.full_like(best, float(len(values) - 1))
    for i in reversed(range(len(values) - 1)):
        idx = jnp.where(values[i] >= best, float(i), idx)
    return idx


def _pick(rows, idx):
    out = rows[-1]
    for i in reversed(range(len(rows) - 1)):
        out = jnp.where(idx == float(i), rows[i], out)
    return out


def _route(logit_t, tm):
    row = lambda r: logit_t[r:r + 1, :]
    groups = [row(i) for i in range(N_GROUPS)]
    top = functools.reduce(jnp.maximum, groups)
    eg = [jnp.exp(x - top) for x in groups]
    total = functools.reduce(lambda a, b: a + b, eg)
    pg = [e / total for e in eg]
    p_g = functools.reduce(jnp.maximum, pg)
    g_idx = _first_index(pg, p_g)
    sel = [_pick([row(N_GROUPS + EXPERTS_PER_GROUP * g + j) for g in range(N_GROUPS)], g_idx)
           for j in range(EXPERTS_PER_GROUP)]
    top = functools.reduce(jnp.maximum, sel)
    ee = [jnp.exp(x - top) for x in sel]
    total = functools.reduce(lambda a, b: a + b, ee)
    pe = [e / total for e in ee]
    p1 = functools.reduce(jnp.maximum, pe)
    l1 = _first_index(pe, p1)
    pe2 = [jnp.where(l1 == float(j), -1.0, pe[j]) for j in range(EXPERTS_PER_GROUP)]
    p2 = functools.reduce(jnp.maximum, pe2)
    l2 = _first_index(pe2, p2)
    den = p1 + p2
    w1 = p1 / den * p_g
    w2 = p2 / den * p_g
    lo = jnp.minimum(l1, l2)
    hi = jnp.maximum(l1, l2)
    pair = lo * (7.0 - lo) * 0.5 + (hi - lo - 1.0)
    cls = g_idx * PAIRS_PER_GROUP + pair
    w_lo = jnp.where(l1 < l2, w1, w2)
    w_hi = jnp.where(l1 < l2, w2, w1)
    cls_id = lax.broadcasted_iota(jnp.int32, (ROUTE_W, tm), 0).astype(F32)
    onehot = jnp.where(cls_id == cls, 1.0, 0.0)
    before = _dot(onehot.astype(BF16), _tri(tm, True, strict=True))
    count = jnp.sum(onehot, axis=1, keepdims=True)
    chunks = jnp.floor((count + (CHUNK_ROWS - 1.0)) * (1.0 / CHUNK_ROWS))
    chunks = jnp.where(cls_id[:, 0:1] == float(N_CLASSES),
                       LOCAL_CHUNKS - jnp.sum(chunks, axis=0, keepdims=True), chunks)
    first_chunk = _dot(_tri(ROUTE_W, False, strict=True),
                       jnp.broadcast_to(chunks, (ROUTE_W, LANES)).astype(BF16))[:, 0:1]
    pos_row = jnp.sum(onehot * (CHUNK_ROWS * first_chunk + before), axis=0, keepdims=True)
    return pos_row, w_lo, w_hi, chunks


def _slab_sort_matrices(pos, tm, slab_axis):
    shape = (LOCAL_SLAB_ROWS, tm) if slab_axis == 0 else (tm, LOCAL_SLAB_ROWS)
    slab_row = lax.broadcasted_iota(jnp.int32, shape, slab_axis)
    sub = _mod_pow2(slab_row, SLAB_ROWS)
    token_row = CHUNK_ROWS * _div_pow2(slab_row, SLAB_ROWS) + _div_pow2(sub, 2)
    hit = token_row.astype(F32) == pos
    half = _mod_pow2(sub, 2)
    return [jnp.where(hit & (half == h), 1.0, 0.0).astype(BF16) for h in range(2)]


def _mix_out_kernel(tm, x_ref, of_ref, ob_ref, sr_ref, ya_ref, mod_ref, lnp_ref, gn_ref, wo_ref, wr_ref, br_ref,
                    x1_ref, hxs_ref, pos_ref, chunks_ref):
    subs = [slice(s, s + MIX_SUB_TILE) for s in range(0, tm, MIX_SUB_TILE)]
    yb = []
    for rows in subs:
        o = of_ref[rows, :].astype(F32) + ob_ref[rows, :].astype(F32)
        sr = sr_ref[rows, :].astype(F32)
        heads = []
        for h in range(GLA_HEADS):
            sl = slice(h * GLA_DV, (h + 1) * GLA_DV)
            oh = o[:, sl]
            ms = jnp.mean(oh * oh, axis=-1, keepdims=True)
            heads.append((oh * lax.rsqrt(ms + RMS_EPS) * gn_ref[...] * sr[:, sl]).astype(BF16))
        yb.append(heads)
    xn = [_layer_norm(x_ref[rows, :], lnp_ref[0:1, :], lnp_ref[1:2, :]) for rows in subs]
    y = []
    for rows, heads in zip(subs, yb):
        acc = _dot(ya_ref[rows, :], wo_ref[0:CONV_CH, :])
        for h in range(GLA_HEADS):
            acc = acc + _dot(heads[h], wo_ref[CONV_CH + h * GLA_DV:CONV_CH + (h + 1) * GLA_DV, :])
        y.append(acc)
    h2_b = []
    for rows, xn_s, y_s in zip(subs, xn, y):
        x1 = _layer_norm(DEEPNORM_ALPHA * xn_s + mod_ref[0:1, :] * y_s, lnp_ref[2:3, :], lnp_ref[3:4, :])
        x1_ref[rows, :] = x1
        h2_b.append((x1 * mod_ref[1:2, :] + mod_ref[2:3, :]).astype(BF16))
    logit_t = jnp.concatenate([(_dot(h2_s, wr_ref[...]) + br_ref[...]).T for h2_s in h2_b], axis=1)
    h2_b = jnp.concatenate(h2_b, axis=0)
    pos_row, w_lo, w_hi, chunks = _route(logit_t, tm)
    rec_id = lax.broadcasted_iota(jnp.int32, (ROUTE_W, tm), 0)
    rec_t = jnp.zeros((ROUTE_W, tm), F32)
    for i, part in enumerate(_exact_bf16_parts(w_lo) + _exact_bf16_parts(w_hi)):
        rec_t = jnp.where(rec_id == i, part, rec_t)
    rec_b = rec_t.T.astype(BF16)
    sort_lo, sort_hi = _slab_sort_matrices(pos_row, tm, 0)
    pay_lo = jnp.concatenate([h2_b[:, 0:HALF_W], rec_b], axis=1)
    pay_hi = jnp.concatenate([h2_b[:, HALF_W:], jnp.zeros((tm, ROUTE_W), BF16)], axis=1)
    slabs = _dot(sort_lo, pay_lo) + _dot(sort_hi, pay_hi)
    for c in range(SLAB_IN_W // LANES):
        hxs_ref[c] = slabs[:, c * LANES:(c + 1) * LANES]
    pos_ref[...] = jnp.broadcast_to(pos_row, (ROUTE_W, tm)).T
    chunks_ref[...] = jnp.broadcast_to(chunks, (ROUTE_W, LANES)).T[0:chunks_ref.shape[0], :]


def _mix_out(x, o_f, o_b, sr, ya, mod, lnp, gn, w_out_b, wr, br):
    bsz, t, _ = x.shape
    tm = SORT_TILE
    assert t % tm == 0
    n_t = t // tm
    tok = lambda w: pl.BlockSpec((None, tm, w), lambda b, i: (b, i, 0))
    full = lambda a: pl.BlockSpec(a.shape, lambda b, i: (0,) * a.ndim)
    flat = lambda rows, w: pl.BlockSpec((rows, w), lambda b, i: (b * n_t + i, 0))
    return pl.pallas_call(
        functools.partial(_mix_out_kernel, tm),
        grid=(bsz, n_t),
        in_specs=[
            tok(D_MODEL), tok(GLA_VAL), tok(GLA_VAL), tok(GLA_VAL), tok(CONV_CH),
            pl.BlockSpec((None, 3, D_MODEL), lambda b, i: (b, 0, 0)),
            full(lnp), full(gn), full(w_out_b), full(wr), full(br),
        ],
        out_specs=[tok(D_MODEL),
                   pl.BlockSpec((SLAB_IN_W // LANES, LOCAL_SLAB_ROWS, LANES), lambda b, i: (0, b * n_t + i, 0)),
                   flat(tm, ROUTE_W), flat(8, ROUTE_W)],
        out_shape=[
            jax.ShapeDtypeStruct((bsz, t, D_MODEL), F32),
            jax.ShapeDtypeStruct((SLAB_IN_W // LANES, bsz * n_t * LOCAL_SLAB_ROWS, LANES), F32),
            jax.ShapeDtypeStruct((bsz * t, ROUTE_W), F32),
            jax.ShapeDtypeStruct((bsz * n_t * 8, ROUTE_W), F32),
        ],
        compiler_params=_params(2),
        name="mix_out",
    )(x, o_f, o_b, sr, ya, mod, lnp, gn, w_out_b, wr, br)


def _moe_kernel(n_chunks, nused_ref, lo_ref, hi_ref, live_ref, src_ref, dst_ref,
                hxs_hbm, w1l_ref, w3l_ref, w2l_ref, w1h_ref, w3h_ref, w2h_ref, out_hbm, gbuf, obuf, gsem, ssem):
    tile_rows = CHUNKS_PER_TILE * SLAB_ROWS
    i = pl.program_id(0)
    n_used = nused_ref[0]
    slot = lax.bitwise_and(i, 1)

    def slab(chunk):
        return pl.ds(pl.multiple_of(chunk * SLAB_ROWS, SLAB_ROWS), SLAB_ROWS)

    def gather_copy(tile, buf_slot, j):
        chunk = src_ref[tile * CHUNKS_PER_TILE + j]
        return pltpu.make_async_copy(hxs_hbm.at[:, slab(chunk), :], gbuf.at[buf_slot, :, slab(j), :],
                                     gsem.at[buf_slot])

    def scatter_copy(tile, buf_slot, j):
        chunk = dst_ref[tile * CHUNKS_PER_TILE + j]
        return pltpu.make_async_copy(obuf.at[buf_slot, :, slab(j), :], out_hbm.at[:, slab(chunk), :],
                                     ssem.at[buf_slot])

    def start_gather(tile, buf_slot):
        for j in range(CHUNKS_PER_TILE):
            gather_copy(tile, buf_slot, j).start(priority=j % 2)

    def wait_gather(buf_slot):
        pltpu.make_async_copy(hxs_hbm.at[:, pl.ds(0, tile_rows), :], gbuf.at[buf_slot], gsem.at[buf_slot]).wait()

    def wait_scatter(buf_slot):
        pltpu.make_async_copy(obuf.at[buf_slot], out_hbm.at[:, pl.ds(0, tile_rows), :], ssem.at[buf_slot]).wait()

    @pl.when(i == 0)
    def _():
        start_gather(0, 0)
        obuf[...] = jnp.zeros(obuf.shape, F32)
        for s in range(2):
            fill = pltpu.make_async_copy(
                obuf.at[s], out_hbm.at[:, pl.ds((n_chunks + s * CHUNKS_PER_TILE) * SLAB_ROWS, tile_rows), :],
                ssem.at[s])
            fill.start()
            fill.wait()

    @pl.when(i + 1 < n_used)
    def _():
        start_gather(i + 1, 1 - slot)

    @pl.when(i < n_used)
    def _():
        wait_gather(slot)

        @pl.when(i >= 2)
        def _():
            wait_scatter(slot)

        @pl.when(live_ref[i] != 0)
        def _():
            def lane_block(c, half):
                return jnp.concatenate(
                    [gbuf[slot, c, pl.ds(2 * r + half, CHUNKS_PER_TILE, stride=SLAB_ROWS), :]
                     for r in range(CHUNK_ROWS)], axis=0)

            n_blk = HALF_W // LANES
            xb = jnp.concatenate([lane_block(c, 0) for c in range(n_blk)]
                                 + [lane_block(c, 1) for c in range(n_blk)], axis=1).astype(BF16)
            rec = lane_block(n_blk, 0)
            w_lo = rec[:, 0:1] + rec[:, 1:2] + rec[:, 2:3]
            w_hi = rec[:, 3:4] + rec[:, 4:5] + rec[:, 5:6]

            gate = [_dot(xb, w1_ref[...]) for w1_ref in (w1l_ref, w1h_ref)]
            up = [_dot(xb, w3_ref[...]) for w3_ref in (w3l_ref, w3h_ref)]
            act = [(_silu(g) * u).astype(BF16) for g, u in zip(gate, up)]
            e_lo, e_hi = [_dot(a, w2_ref[...]) for a, w2_ref in zip(act, (w2l_ref, w2h_ref))]
            y = w_lo * e_lo + w_hi * e_hi
            for r in range(CHUNK_ROWS):
                rows = slice(r * CHUNKS_PER_TILE, (r + 1) * CHUNKS_PER_TILE)
                for half in range(2):
                    for c in range(n_blk):
                        col = half * HALF_W + c * LANES
                        obuf[slot, c, pl.ds(2 * r + half, CHUNKS_PER_TILE, stride=SLAB_ROWS), :] = (
                            y[rows, col:col + LANES])

        @pl.when(live_ref[i] == 0)
        def _():
            obuf[slot] = jnp.zeros(obuf.shape[1:], F32)

        for j in range(CHUNKS_PER_TILE):
            scatter_copy(i, slot, j).start(priority=j % 2)

        @pl.when(i == n_used - 1)
        def _():
            wait_scatter(slot)

            @pl.when(i >= 1)
            def _():
                wait_scatter(1 - slot)


def _moe(hxs, src, dst, n_used, tile_lo, tile_hi, tile_live, w1_b, w3_b, w2_b):
    n_chunks = hxs.shape[1] // SLAB_ROWS
    tile_rows = CHUNKS_PER_TILE * SLAB_ROWS
    n_steps = src.shape[0] // CHUNKS_PER_TILE
    wspec = lambda which, shape: pl.BlockSpec(
        (None,) + shape, (lambda i, nu, lo, hi, lv, s, d: (lo[i], 0, 0)) if which == 0 else
        (lambda i, nu, lo, hi, lv, s, d: (hi[i], 0, 0)))
    grid_spec = pltpu.PrefetchScalarGridSpec(
        num_scalar_prefetch=6,
        grid=(n_steps,),
        in_specs=[
            pl.BlockSpec(memory_space=pl.ANY),
            wspec(0, (D_MODEL, D_EXPERT)), wspec(0, (D_MODEL, D_EXPERT)), wspec(0, (D_EXPERT, D_MODEL)),
            wspec(1, (D_MODEL, D_EXPERT)), wspec(1, (D_MODEL, D_EXPERT)), wspec(1, (D_EXPERT, D_MODEL)),
        ],
        out_specs=pl.BlockSpec(memory_space=pl.ANY),
        scratch_shapes=[
            pltpu.VMEM((2, SLAB_IN_W // LANES, tile_rows, LANES), F32),
            pltpu.VMEM((2, SLAB_OUT_W // LANES, tile_rows, LANES), F32),
            pltpu.SemaphoreType.DMA((2,)),
            pltpu.SemaphoreType.DMA((2,)),
        ],
    )
    return pl.pallas_call(
        functools.partial(_moe_kernel, n_chunks),
        grid_spec=grid_spec,
        out_shape=jax.ShapeDtypeStruct((SLAB_OUT_W // LANES, (n_chunks + 2 * CHUNKS_PER_TILE) * SLAB_ROWS, LANES), F32),
        compiler_params=_params(1),
        name="moe",
    )(n_used, tile_lo, tile_hi, tile_live, src, dst, hxs, w1_b, w3_b, w2_b, w1_b, w3_b, w2_b)


def _final_kernel(n_sort, x1_ref, moe_ref, pos_ref, mod_ref, lnp_ref, o_ref):
    moe = []
    for s in range(n_sort):
        slab_rows = slice(s * LOCAL_SLAB_ROWS, (s + 1) * LOCAL_SLAB_ROWS)
        moe_b = jnp.concatenate([moe_ref[c, slab_rows, :] for c in range(SLAB_OUT_W // LANES)], axis=1).astype(BF16)
        sort_lo, sort_hi = _slab_sort_matrices(pos_ref[s * SORT_TILE:(s + 1) * SORT_TILE, 0:1], SORT_TILE, 1)
        for r in range(0, SORT_TILE, MIX_SUB_TILE):
            rows = slice(r, r + MIX_SUB_TILE)
            moe.append((s * SORT_TILE + r,
                        jnp.concatenate([_dot(sort_lo[rows, :], moe_b), _dot(sort_hi[rows, :], moe_b)], axis=1)))
    for start, moe_s in moe:
        rows = slice(start, start + MIX_SUB_TILE)
        o_ref[rows, :] = _layer_norm(DEEPNORM_ALPHA * x1_ref[rows, :] + mod_ref[...] * moe_s,
                                     lnp_ref[0:1, :], lnp_ref[1:2, :])


def _final(x1, moe, pos, g2, lnp):
    bsz, t, _ = x1.shape
    n_sort = 2 if t % (2 * SORT_TILE) == 0 else 1
    tm = n_sort * SORT_TILE
    n_t = t // tm
    flat = lambda rows, w: pl.BlockSpec((rows, w), lambda b, i: (b * n_t + i, 0))
    return pl.pallas_call(
        functools.partial(_final_kernel, n_sort),
        grid=(bsz, n_t),
        in_specs=[
            pl.BlockSpec((None, tm, D_MODEL), lambda b, i: (b, i, 0)),
            pl.BlockSpec((SLAB_OUT_W // LANES, n_sort * LOCAL_SLAB_ROWS, LANES), lambda b, i: (0, b * n_t + i, 0)),
            flat(tm, ROUTE_W),
            pl.BlockSpec((None, 1, D_MODEL), lambda b, i: (b, 0, 0)),
            pl.BlockSpec(lnp.shape, lambda b, i: (0, 0)),
        ],
        out_specs=pl.BlockSpec((None, tm, D_MODEL), lambda b, i: (b, i, 0)),
        out_shape=jax.ShapeDtypeStruct((bsz, t, D_MODEL), F32),
        compiler_params=_params(2),
        name="final",
    )(x1, moe, pos, g2, lnp)


def _pair_tables():
    lo, hi = [], []
    for g in range(N_GROUPS):
        for a in range(EXPERTS_PER_GROUP):
            for b in range(a + 1, EXPERTS_PER_GROUP):
                lo.append(g * EXPERTS_PER_GROUP + a)
                hi.append(g * EXPERTS_PER_GROUP + b)
    return jnp.array(lo, jnp.int32), jnp.array(hi, jnp.int32)


def _moe_plan(chunks, n_sort_tiles):
    n_cls = N_CLASSES + 1
    hp = lax.Precision.HIGHEST
    m = chunks.reshape(n_sort_tiles, 8, ROUTE_W)[:, 0, :n_cls].astype(jnp.int32)
    a_end = jnp.cumsum(m, axis=0)
    a_start = a_end - m
    per_cls = a_end[-1]
    padded = (per_cls + CHUNKS_PER_TILE - 1) // CHUNKS_PER_TILE * CHUNKS_PER_TILE
    g_end = jnp.cumsum(padded)
    g_start = g_end - padded
    local_off = jnp.cumsum(m, axis=1) - m
    seg = jnp.arange(n_sort_tiles, dtype=jnp.int32)[:, None] * LOCAL_CHUNKS + local_off - a_start
    n_steps = -(-(n_sort_tiles * LOCAL_CHUNKS) // CHUNKS_PER_TILE) + n_cls
    p = jnp.arange(n_steps * CHUNKS_PER_TILE, dtype=jnp.int32)
    cls_p = jnp.minimum(jnp.sum((g_end[None, :] <= p[:, None]).astype(jnp.int32), axis=1), n_cls - 1)
    onehot = (cls_p[:, None] == jnp.arange(n_cls, dtype=jnp.int32)[None, :]).astype(F32)
    pick = lambda tab: jnp.dot(onehot, tab.astype(F32), precision=hp)
    u = p - pick(g_start[:, None])[:, 0].astype(jnp.int32)
    valid = u < pick(per_cls[:, None])[:, 0].astype(jnp.int32)
    a_end_p = pick(a_end.T).astype(jnp.int32)
    seg_p = pick(seg.T).astype(jnp.int32)
    tile_p = jnp.sum((a_end_p <= u[:, None]).astype(jnp.int32), axis=1)
    hit = jnp.arange(n_sort_tiles, dtype=jnp.int32)[None, :] == tile_p[:, None]
    src = jnp.sum(jnp.where(hit, seg_p, 0), axis=1) + u
    pad_dst = n_sort_tiles * LOCAL_CHUNKS + (p // CHUNKS_PER_TILE) % 2 * CHUNKS_PER_TILE + p % CHUNKS_PER_TILE
    dst = jnp.where(valid, src, pad_dst).astype(jnp.int32)
    src = jnp.where(valid, src, 0).astype(jnp.int32)
    n_used = g_end[-1:] // CHUNKS_PER_TILE
    step = jnp.arange(n_steps, dtype=jnp.int32)
    tile_cls = jnp.sum((g_end[None, :] // CHUNKS_PER_TILE <= step[:, None]).astype(jnp.int32), axis=1)
    live = ((tile_cls < N_CLASSES) & (step < n_used[0])).astype(jnp.int32)
    pair_lo, pair_hi = _pair_tables()
    pair_oh = (jnp.minimum(tile_cls, N_CLASSES - 1)[:, None] == jnp.arange(N_CLASSES)[None, :]).astype(jnp.int32)
    tile_lo = jnp.sum(pair_oh * pair_lo[None, :], axis=1).astype(jnp.int32)
    tile_hi = jnp.sum(pair_oh * pair_hi[None, :], axis=1).astype(jnp.int32)
    return src, dst, n_used.astype(jnp.int32), tile_lo, tile_hi, live


def kernel(x, c, ctx, c_ctx, ln_in_g, ln_in_b, w_ada, b_ada, w_in, conv_w, conv_b, gate_w2_fwd, gate_b_fwd,
           gate_w2_bwd, gate_b_bwd, gla_norm_g, w_out, ln1_g, ln1_b, router_group_w, router_group_b,
           router_expert_w, router_expert_b, expert_w1, expert_w3, expert_w2, ln2_g, ln2_b):
    bsz, t, _ = x.shape
    n_tok = bsz * t
    l = 0
    rows = -(-(bsz + 1) // 8) * 8
    cond = jnp.zeros((rows, D_MODEL), F32).at[:bsz].set(c).at[bsz].set(c_ctx)
    ada = _ada(cond, w_ada[l], b_ada[l][None, :])
    sh1, sc1, g1, sh2, sc2, g2 = [ada[:, i * D_MODEL:(i + 1) * D_MODEL] for i in range(6)]

    w_in_b = w_in[l].astype(BF16)
    lnp_in = jnp.stack([ln_in_g, ln_in_b])
    zero = jnp.zeros((GLA_GATE_RANK, GLA_KEY), F32)
    w2cat = jnp.concatenate([jnp.concatenate([gate_w2_fwd[l], zero], axis=1),
                             jnp.concatenate([zero, gate_w2_bwd[l]], axis=1)], axis=0).astype(BF16)
    gbias = jnp.concatenate([gate_b_fwd[l], gate_b_bwd[l]])[None, :]

    mod_ctx = jnp.broadcast_to(jnp.stack([1.0 + sc1[bsz], sh1[bsz]])[None], (bsz, 2, D_MODEL))
    k_c, v_c, g_c = _proj(ctx, mod_ctx, lnp_in, w_in_b, conv_w[l], conv_b[l][None, :], w2cat, gbias, False)
    zero_state = jnp.zeros((bsz, GLA_KEY, PAIR_VAL), F32)
    s_f, s_b = _gla(None, k_c, v_c, g_c, zero_state, zero_state)

    mod1 = jnp.stack([1.0 + sc1[:bsz], sh1[:bsz]], axis=1)
    ya, q, k, v, sr, g = _proj(x, mod1, lnp_in, w_in_b, conv_w[l], conv_b[l][None, :], w2cat, gbias, True)
    o_f, o_b, _, _ = _gla(q, k, v, g, s_f, s_b)

    mod2 = jnp.stack([g1[:bsz], 1.0 + sc2[:bsz], sh2[:bsz]], axis=1)
    lnp1 = jnp.stack([ln_in_g, ln_in_b, ln1_g[l], ln1_b[l]])
    wr = jnp.zeros((D_MODEL, ROUTE_W), F32)
    wr = wr.at[:, :N_GROUPS].set(router_group_w[l]).at[:, N_GROUPS:N_GROUPS + N_EXPERTS].set(router_expert_w[l])
    br = jnp.zeros((1, ROUTE_W), F32)
    br = br.at[0, :N_GROUPS].set(router_group_b[l]).at[0, N_GROUPS:N_GROUPS + N_EXPERTS].set(router_expert_b[l])
    x1, hxs, pos, chunks = _mix_out(x, o_f, o_b, sr, ya, mod2, lnp1, gla_norm_g[l][None, :],
                                    w_out[l].astype(BF16), wr.astype(BF16), br)

    src, dst, n_used, tile_lo, tile_hi, live = _moe_plan(chunks, n_tok // SORT_TILE)
    moe = _moe(hxs, src, dst, n_used, tile_lo, tile_hi, live,
               expert_w1[l].astype(BF16), expert_w3[l].astype(BF16), expert_w2[l].astype(BF16))

    return _final(x1, moe, pos, g2[:bsz][:, None, :], jnp.stack([ln2_g[l], ln2_b[l]]))
```

```python
import functools

import jax
import jax.numpy as jnp
from jax import lax
from jax.experimental import pallas as pl
from jax.experimental.pallas import tpu as pltpu

F32 = jnp.float32
BF16 = jnp.bfloat16

D_MODEL = 1024
GRID_W = 64
CONV_CH = 512
GLA_HEADS = 4
GLA_DK = 64
GLA_DV = 128
GLA_KEY = GLA_HEADS * GLA_DK
GLA_VAL = GLA_HEADS * GLA_DV
PAIR_KEY = 2 * GLA_DK
PAIR_VAL = 2 * GLA_DV
GLA_GATE_RANK = 16
GLA_TAU = 16.0
OFF_AB = 0
OFF_AC = OFF_AB + CONV_CH
OFF_AX = OFF_AC + CONV_CH
OFF_Q = OFF_AX + CONV_CH
OFF_K = OFF_Q + GLA_KEY
OFF_V = OFF_K + GLA_KEY
OFF_R = OFF_V + GLA_VAL
OFF_GF = OFF_R + GLA_VAL
D_PROJ = OFF_GF + 2 * GLA_GATE_RANK
N_GROUPS = 4
EXPERTS_PER_GROUP = 4
N_EXPERTS = N_GROUPS * EXPERTS_PER_GROUP
D_EXPERT = 512
PAIRS_PER_GROUP = 6
N_CLASSES = N_GROUPS * PAIRS_PER_GROUP
LN_EPS = 1e-5
RMS_EPS = 1e-6
DEPTH = 1
DEEPNORM_ALPHA = (2.0 * DEPTH) ** 0.25

LANES = 128
GLA_CHUNK = 64
GLA_SUB = 16
N_SUB = GLA_CHUNK // GLA_SUB
ROUTE_W = LANES
HALF_W = D_MODEL // 2
SLAB_IN_W = HALF_W + ROUTE_W
SLAB_OUT_W = HALF_W
SORT_TILE = 256
MIX_SUB_TILE = 128
MOE_TILE = 256
CHUNK_ROWS = 4
SLAB_ROWS = 2 * CHUNK_ROWS
LOCAL_CHUNKS = -(-(SORT_TILE + N_CLASSES * (CHUNK_ROWS - 1)) // CHUNK_ROWS)
LOCAL_SLAB_ROWS = LOCAL_CHUNKS * SLAB_ROWS
CHUNKS_PER_TILE = MOE_TILE // CHUNK_ROWS
VMEM_LIMIT = 56 * 1024 * 1024


def _params(n_axes, vmem=VMEM_LIMIT):
    return pltpu.CompilerParams(dimension_semantics=("arbitrary",) * n_axes, vmem_limit_bytes=vmem)


def _dot(a, b):
    return jnp.dot(a, b, preferred_element_type=F32)


def _div_pow2(x, d):
    assert d & (d - 1) == 0
    return lax.shift_right_logical(x, jnp.int32(d.bit_length() - 1))


def _mod_pow2(x, d):
    assert d & (d - 1) == 0
    return lax.bitwise_and(x, jnp.int32(d - 1))


def _split2(x):
    hi = x.astype(BF16)
    lo = (x - hi.astype(F32)).astype(BF16)
    return hi, lo


def _dot3(a, b):
    ah, al = _split2(a)
    bh, bl = _split2(b)
    return _dot(ah, bh) + _dot(ah, bl) + _dot(al, bh)


def _silu(x):
    return x * (0.5 * jnp.tanh(0.5 * x) + 0.5)


def _layer_norm(x, g, b):
    mu = jnp.mean(x, axis=-1, keepdims=True)
    xc = x - mu
    var = jnp.mean(xc * xc, axis=-1, keepdims=True)
    return xc * lax.rsqrt(var + LN_EPS) * g + b


def _ada_kernel(c_ref, w_ref, b_ref, o_ref):
    o_ref[...] = _dot3(_silu(c_ref[...]), w_ref[...]) + b_ref[...]


def _ada(cond, w_ada, b_ada):
    rows = cond.shape[0]
    n_out = w_ada.shape[1]
    tn = 1024
    return pl.pallas_call(
        _ada_kernel,
        grid=(n_out // tn,),
        in_specs=[
            pl.BlockSpec((rows, D_MODEL), lambda j: (0, 0)),
            pl.BlockSpec((D_MODEL, tn), lambda j: (0, j)),
            pl.BlockSpec((1, tn), lambda j: (0, j)),
        ],
        out_specs=pl.BlockSpec((rows, tn), lambda j: (0, j)),
        out_shape=jax.ShapeDtypeStruct((rows, n_out), F32),
        compiler_params=_params(1),
        name="ada",
    )(cond, w_ada, b_ada)


def _log_sigmoid(z):
    return jnp.minimum(z, 0.0) - jnp.log(1.0 + jnp.exp(-jnp.abs(z)))


def _proj_kernel(latent, tm, x_ref, mod_ref, lnp_ref, w_ref, cw_ref, cb_ref, w2_ref, gbias_ref, *out_refs):
    x = x_ref[...]
    xn = _layer_norm(x, lnp_ref[0:1, :], lnp_ref[1:2, :])
    h = xn * mod_ref[0:1, :] + mod_ref[1:2, :]
    hb = h.astype(BF16)
    if latent:
        ya_ref, q_ref, k_ref, v_ref, sr_ref, g_ref = out_refs
        p = _dot(hb, w_ref[:, OFF_AB:OFF_Q])
        a_b = p[:, 0:CONV_CH]
        u = p[:, CONV_CH:2 * CONV_CH] * p[:, 2 * CONV_CH:3 * CONV_CH]
        pos = _mod_pow2(lax.broadcasted_iota(jnp.int32, (tm, 1), 0), GRID_W)
        u_prev = jnp.where(pos == 0, 0.0, pltpu.roll(u, 1, 0))
        u_next = jnp.where(pos == GRID_W - 1, 0.0, pltpu.roll(u, tm - 1, 0))
        conv = u_prev * cw_ref[0:1, :] + u * cw_ref[1:2, :] + u_next * cw_ref[2:3, :] + cb_ref[...]
        ya_ref[...] = (a_b * conv).astype(BF16)
        qk = _dot(hb, w_ref[:, OFF_Q:OFF_V])
        q_ref[...] = (qk[:, 0:GLA_KEY] * (GLA_DK ** -0.5)).astype(BF16)
        k_ref[...] = qk[:, GLA_KEY:].astype(BF16)
        r = _dot(hb, w_ref[:, OFF_R:OFF_GF])
        sr_ref[...] = _silu(r).astype(BF16)
    else:
        k_ref, v_ref, g_ref = out_refs
        k_ref[...] = _dot(hb, w_ref[:, OFF_K:OFF_V]).astype(BF16)
    v_ref[...] = _dot(hb, w_ref[:, OFF_V:OFF_R]).astype(BF16)
    low = _dot(hb, w_ref[:, OFF_GF:D_PROJ])
    z = _dot(low.astype(BF16), w2_ref[...]) + gbias_ref[...]
    g_ref[...] = _log_sigmoid(z) * (1.0 / GLA_TAU)


def _proj(x, mod, lnp, w_in_b, conv_w, conv_b, w2cat, gbias, latent):
    bsz, t, _ = x.shape
    tm = min(512, t)
    assert t % tm == 0 and tm % GRID_W == 0
    tok = lambda w: pl.BlockSpec((None, tm, w), lambda b, i: (b, i, 0))
    full = lambda a: pl.BlockSpec(a.shape, lambda b, i: (0,) * a.ndim)
    widths = ([(CONV_CH, BF16), (GLA_KEY, BF16)] if latent else []) + [(GLA_KEY, BF16), (GLA_VAL, BF16)]
    widths += ([(GLA_VAL, BF16)] if latent else []) + [(2 * GLA_KEY, F32)]
    return pl.pallas_call(
        functools.partial(_proj_kernel, latent, tm),
        grid=(bsz, t // tm),
        in_specs=[
            tok(D_MODEL),
            pl.BlockSpec((None, 2, D_MODEL), lambda b, i: (b, 0, 0)),
            full(lnp), full(w_in_b), full(conv_w), full(conv_b), full(w2cat), full(gbias),
        ],
        out_specs=[tok(w) for w, _ in widths],
        out_shape=[jax.ShapeDtypeStruct((bsz, t, w), dt) for w, dt in widths],
        compiler_params=_params(2),
        name="proj_latent" if latent else "proj_ctx",
    )(x, mod, lnp, w_in_b, conv_w, conv_b, w2cat, gbias)


def _tri(n, reverse, strict=False):
    i = lax.broadcasted_iota(jnp.int32, (n, n), 0)
    j = lax.broadcasted_iota(jnp.int32, (n, n), 1)
    if strict:
        m = (j > i) if reverse else (j < i)
    else:
        m = (j >= i) if reverse else (j <= i)
    return jnp.where(m, 1.0, 0.0).astype(BF16)


def _chunk_cumsum(g, reverse):
    tri = _tri(GLA_CHUNK, reverse)
    g_hi, g_lo = _split2(g)
    return _dot(tri, g_hi) + _dot(tri, g_lo)


def _as_column(row):
    return jnp.broadcast_to(row, (LANES, row.shape[1])).T


def _sub_anchors(gc, reverse):
    zero = jnp.zeros((1, GLA_KEY), F32)
    if reverse:
        return [gc[GLA_SUB * (a + 1):GLA_SUB * (a + 1) + 1] for a in range(N_SUB - 1)] + [zero]
    return [zero] + [gc[GLA_SUB * a - 1:GLA_SUB * a] for a in range(1, N_SUB)]


def _score_pairs(reverse):
    return [(a, b) for a in range(N_SUB) for b in range(N_SUB) if (b >= a if reverse else b <= a)]


def _intra_products(q, k, gc, reverse):
    r = _sub_anchors(gc, reverse)
    anchor = jnp.concatenate([jnp.broadcast_to(ra, (GLA_SUB, GLA_KEY)) for ra in r], axis=0)
    gcb = gc - anchor
    qt = q * jnp.exp(gcb)
    kt = k * jnp.exp(-gcb)
    rows = []
    for a, b in _score_pairs(reverse):
        qa = qt[GLA_SUB * a:GLA_SUB * (a + 1)]
        if a != b:
            qa = qa * jnp.exp(r[a] - r[b])
        rows.append(qa)
    qp = jnp.concatenate(rows, axis=0).astype(BF16)
    width = GLA_HEADS * GLA_CHUNK
    rr = lax.broadcasted_iota(jnp.int32, (width, GLA_KEY), 0)
    cc = lax.broadcasted_iota(jnp.int32, (width, GLA_KEY), 1)
    kbd = jnp.where(_div_pow2(rr, GLA_CHUNK) == _div_pow2(cc, GLA_DK),
                    jnp.concatenate([kt] * GLA_HEADS, axis=0), 0.0)
    return lax.dot_general(qp, kbd.astype(BF16), (((1,), (1,)), ((), ())), preferred_element_type=F32)


def _assemble_scores(res, reverse):
    pairs = _score_pairs(reverse)
    width = GLA_HEADS * GLA_CHUNK
    col = _mod_pow2(lax.broadcasted_iota(jnp.int32, (GLA_SUB, width), 1), GLA_CHUNK)
    col_blk = _div_pow2(col, GLA_SUB)
    col_pos = _mod_pow2(col, GLA_SUB)
    row_pos = lax.broadcasted_iota(jnp.int32, (GLA_SUB, width), 0)
    causal = (col_pos >= row_pos) if reverse else (col_pos <= row_pos)
    blocks = []
    for a in range(N_SUB):
        acc = jnp.zeros((GLA_SUB, width), F32)
        for idx, (pa, pb) in enumerate(pairs):
            if pa != a:
                continue
            keep = col_blk == pb
            if pa == pb:
                keep = keep & causal
            acc = acc + jnp.where(keep, res[GLA_SUB * idx:GLA_SUB * (idx + 1)], 0.0)
        blocks.append(acc)
    return jnp.concatenate(blocks, axis=0)


def _pair_mask(rows_per_head, cols_per_head, n_row_pairs=1):
    shape = (n_row_pairs * 2 * rows_per_head, 2 * cols_per_head)
    rr = _mod_pow2(lax.broadcasted_iota(jnp.int32, shape, 0), 2 * rows_per_head)
    cc = lax.broadcasted_iota(jnp.int32, shape, 1)
    return _div_pow2(rr, rows_per_head) == _div_pow2(cc, cols_per_head)


def _state_terms(q, k, v_b, gc, reverse):
    total = gc[0:1] if reverse else gc[GLA_CHUNK - 1:GLA_CHUNK]
    q_dec = None if q is None else (q * jnp.exp(gc)).astype(BF16)
    k_end = (k * jnp.exp(total - gc)).astype(BF16)
    tn = (((0,), (0,)), ((), ()))
    upd = [lax.dot_general(k_end[:, p * PAIR_KEY:(p + 1) * PAIR_KEY], v_b[:, p * PAIR_VAL:(p + 1) * PAIR_VAL], tn,
                           preferred_element_type=F32) for p in range(GLA_HEADS // 2)]
    upd = jnp.where(_pair_mask(GLA_DK, GLA_DV, GLA_HEADS // 2), jnp.concatenate(upd, axis=0), 0.0)
    decay = jnp.exp(_as_column(total))
    decay = jnp.concatenate([decay] * (PAIR_VAL // LANES), axis=1)
    return q_dec, decay, upd


def _advance_state(q_dec, decay, upd, state):
    o_inter = None
    if q_dec is not None:
        state_b = state.astype(BF16)
        o_inter = jnp.concatenate(
            [_dot(q_dec[:, p * PAIR_KEY:(p + 1) * PAIR_KEY], state_b[p * PAIR_KEY:(p + 1) * PAIR_KEY, :])
             for p in range(GLA_HEADS // 2)], axis=1)
    return o_inter, state * decay + upd


def _gla_kernel(tt, nt, with_out, *refs):
    if with_out:
        (qf_ref, kf_ref, vf_ref, gfw_ref, qb_ref, kb_ref, vb_ref, gbw_ref, s0f_ref, s0b_ref,
         of_ref, ob_ref, sf_ref, sb_ref) = refs
    else:
        kf_ref, vf_ref, gfw_ref, kb_ref, vb_ref, gbw_ref, s0f_ref, s0b_ref, sf_ref, sb_ref = refs
    j = pl.program_id(1)

    @pl.when(j == 0)
    def _():
        sf_ref[...] = s0f_ref[...]
        sb_ref[...] = s0b_ref[...]

    chunk_slices = [slice(c * GLA_CHUNK, (c + 1) * GLA_CHUNK) for c in range(tt // GLA_CHUNK)]
    fwd, bwd = [], []
    for sl in chunk_slices:
        g = gfw_ref[sl, :]
        fwd.append(dict(sl=sl, k=kf_ref[sl, :].astype(F32), v=vf_ref[sl, :], g_f=g[:, 0:GLA_KEY], g_b=g[:, GLA_KEY:],
                        q=qf_ref[sl, :].astype(F32) if with_out else None))
        bwd.append(dict(sl=sl, k=kb_ref[sl, :].astype(F32), v=vb_ref[sl, :], g_b=gbw_ref[sl, :][:, GLA_KEY:],
                        q=qb_ref[sl, :].astype(F32) if with_out else None))
    for d in fwd:
        d["gc_f"] = _chunk_cumsum(d["g_f"], False)
        if with_out:
            d["gc_b"] = _chunk_cumsum(d["g_b"], True)
    for d in bwd:
        d["gc_b"] = _chunk_cumsum(d["g_b"], True)
    if with_out:
        for d in fwd:
            d["res_f"] = _intra_products(d["q"], d["k"], d["gc_f"], False)
            d["res_b"] = _intra_products(d["q"], d["k"], d["gc_b"], True)
    for d in fwd:
        d["terms"] = _state_terms(d["q"], d["k"], d["v"], d["gc_f"], False)
    for d in bwd:
        d["terms"] = _state_terms(d["q"], d["k"], d["v"], d["gc_b"], True)
    if with_out:
        for d in fwd:
            scores = (_assemble_scores(d["res_f"], False) + _assemble_scores(d["res_b"], True)).astype(BF16)
            o_intra = []
            for p in range(GLA_HEADS // 2):
                v_p = d["v"][:, p * PAIR_VAL:(p + 1) * PAIR_VAL]
                vbd = jnp.where(_pair_mask(GLA_CHUNK, GLA_DV), jnp.concatenate([v_p, v_p], axis=0),
                                jnp.zeros((), BF16))
                o_intra.append(_dot(scores[:, p * 2 * GLA_CHUNK:(p + 1) * 2 * GLA_CHUNK], vbd))
            d["o_intra"] = jnp.concatenate(o_intra, axis=1)

    state = sf_ref[...]
    for d in fwd:
        o_inter, state = _advance_state(*d["terms"], state)
        if with_out:
            of_ref[d["sl"], :] = (d["o_intra"] + o_inter).astype(BF16)
    sf_ref[...] = state

    state = sb_ref[...]
    for d in reversed(bwd):
        o_inter, state = _advance_state(*d["terms"], state)
        if with_out:
            ob_ref[d["sl"], :] = o_inter.astype(BF16)
    sb_ref[...] = state


def _gla(q, k, v, g, s0f, s0b):
    with_out = q is not None
    bsz, t, _ = k.shape
    tt = min(512, t)
    assert t % tt == 0 and tt % GLA_CHUNK == 0
    nt = t // tt
    fwd = lambda w: pl.BlockSpec((None, tt, w), lambda b, j: (b, j, 0))
    bwd = lambda w: pl.BlockSpec((None, tt, w), lambda b, j: (b, nt - 1 - j, 0))
    st = pl.BlockSpec((None, GLA_KEY, PAIR_VAL), lambda b, j: (b, 0, 0))
    st_shape = jax.ShapeDtypeStruct((bsz, GLA_KEY, PAIR_VAL), F32)
    if with_out:
        ins = [q, k, v, g, q, k, v, g, s0f, s0b]
        in_specs = [fwd(GLA_KEY), fwd(GLA_KEY), fwd(GLA_VAL), fwd(2 * GLA_KEY),
                    bwd(GLA_KEY), bwd(GLA_KEY), bwd(GLA_VAL), bwd(2 * GLA_KEY), st, st]
        out_specs = [fwd(GLA_VAL), bwd(GLA_VAL), st, st]
        o_shape = jax.ShapeDtypeStruct((bsz, t, GLA_VAL), BF16)
        out_shape = [o_shape, o_shape, st_shape, st_shape]
    else:
        ins = [k, v, g, k, v, g, s0f, s0b]
        in_specs = [fwd(GLA_KEY), fwd(GLA_VAL), fwd(2 * GLA_KEY),
                    bwd(GLA_KEY), bwd(GLA_VAL), bwd(2 * GLA_KEY), st, st]
        out_specs = [st, st]
        out_shape = [st_shape, st_shape]
    return pl.pallas_call(
        functools.partial(_gla_kernel, tt, nt, with_out),
        grid=(bsz, nt),
        in_specs=in_specs,
        out_specs=out_specs,
        out_shape=out_shape,
        compiler_params=_params(2),
        name="gla_latent" if with_out else "gla_ctx",
    )(*ins)


def _exact_bf16_parts(x):
    hi = x.astype(BF16).astype(F32)
    r = x - hi
    mid = r.astype(BF16).astype(F32)
    lo = (r - mid).astype(BF16).astype(F32)
    return hi, mid, lo


def _first_index(values, best):
    idx = jnp.full_like(best, float(len(values) - 1))
    for i in reversed(range(len(values) - 1)):
        idx = jnp.where(values[i] >= best, float(i), idx)
    return idx


def _pick(rows, idx):
    out = rows[-1]
    for i in reversed(range(len(rows) - 1)):
        out = jnp.where(idx == float(i), rows[i], out)
    return out


def _route(logit_t, tm):
    row = lambda r: logit_t[r:r + 1, :]
    groups = [row(i) for i in range(N_GROUPS)]
    top = functools.reduce(jnp.maximum, groups)
    eg = [jnp.exp(x - top) for x in groups]
    total = functools.reduce(lambda a, b: a + b, eg)
    pg = [e / total for e in eg]
    p_g = functools.reduce(jnp.maximum, pg)
    g_idx = _first_index(pg, p_g)
    sel = [_pick([row(N_GROUPS + EXPERTS_PER_GROUP * g + j) for g in range(N_GROUPS)], g_idx)
           for j in range(EXPERTS_PER_GROUP)]
    top = functools.reduce(jnp.maximum, sel)
    ee = [jnp.exp(x - top) for x in sel]
    total = functools.reduce(lambda a, b: a + b, ee)
    pe = [e / total for e in ee]
    p1 = functools.reduce(jnp.maximum, pe)
    l1 = _first_index(pe, p1)
    pe2 = [jnp.where(l1 == float(j), -1.0, pe[j]) for j in range(EXPERTS_PER_GROUP)]
    p2 = functools.reduce(jnp.maximum, pe2)
    l2 = _first_index(pe2, p2)
    den = p1 + p2
    w1 = p1 / den * p_g
    w2 = p2 / den * p_g
    lo = jnp.minimum(l1, l2)
    hi = jnp.maximum(l1, l2)
    pair = lo * (7.0 - lo) * 0.5 + (hi - lo - 1.0)
    cls = g_idx * PAIRS_PER_GROUP + pair
    w_lo = jnp.where(l1 < l2, w1, w2)
    w_hi = jnp.where(l1 < l2, w2, w1)
    cls_id = lax.broadcasted_iota(jnp.int32, (ROUTE_W, tm), 0).astype(F32)
    onehot = jnp.where(cls_id == cls, 1.0, 0.0)
    before = _dot(onehot.astype(BF16), _tri(tm, True, strict=True))
    count = jnp.sum(onehot, axis=1, keepdims=True)
    chunks = jnp.floor((count + (CHUNK_ROWS - 1.0)) * (1.0 / CHUNK_ROWS))
    chunks = jnp.where(cls_id[:, 0:1] == float(N_CLASSES),
                       LOCAL_CHUNKS - jnp.sum(chunks, axis=0, keepdims=True), chunks)
    first_chunk = _dot(_tri(ROUTE_W, False, strict=True),
                       jnp.broadcast_to(chunks, (ROUTE_W, LANES)).astype(BF16))[:, 0:1]
    pos_row = jnp.sum(onehot * (CHUNK_ROWS * first_chunk + before), axis=0, keepdims=True)
    return pos_row, w_lo, w_hi, chunks


def _slab_sort_matrices(pos, tm, slab_axis):
    shape = (LOCAL_SLAB_ROWS, tm) if slab_axis == 0 else (tm, LOCAL_SLAB_ROWS)
    slab_row = lax.broadcasted_iota(jnp.int32, shape, slab_axis)
    sub = _mod_pow2(slab_row, SLAB_ROWS)
    token_row = CHUNK_ROWS * _div_pow2(slab_row, SLAB_ROWS) + _div_pow2(sub, 2)
    hit = token_row.astype(F32) == pos
    half = _mod_pow2(sub, 2)
    return [jnp.where(hit & (half == h), 1.0, 0.0).astype(BF16) for h in range(2)]


def _mix_out_kernel(tm, x_ref, of_ref, ob_ref, sr_ref, ya_ref, mod_ref, lnp_ref, gn_ref, wo_ref, wr_ref, br_ref,
                    x1_ref, hxs_ref, pos_ref, chunks_ref):
    subs = [slice(s, s + MIX_SUB_TILE) for s in range(0, tm, MIX_SUB_TILE)]
    yb = []
    for rows in subs:
        o = of_ref[rows, :].astype(F32) + ob_ref[rows, :].astype(F32)
        sr = sr_ref[rows, :].astype(F32)
        heads = []
        for h in range(GLA_HEADS):
            sl = slice(h * GLA_DV, (h + 1) * GLA_DV)
            oh = o[:, sl]
            ms = jnp.mean(oh * oh, axis=-1, keepdims=True)
            heads.append((oh * lax.rsqrt(ms + RMS_EPS) * gn_ref[...] * sr[:, sl]).astype(BF16))
        yb.append(heads)
    xn = [_layer_norm(x_ref[rows, :], lnp_ref[0:1, :], lnp_ref[1:2, :]) for rows in subs]
    y = []
    for rows, heads in zip(subs, yb):
        acc = _dot(ya_ref[rows, :], wo_ref[0:CONV_CH, :])
        for h in range(GLA_HEADS):
            acc = acc + _dot(heads[h], wo_ref[CONV_CH + h * GLA_DV:CONV_CH + (h + 1) * GLA_DV, :])
        y.append(acc)
    h2_b = []
    for rows, xn_s, y_s in zip(subs, xn, y):
        x1 = _layer_norm(DEEPNORM_ALPHA * xn_s + mod_ref[0:1, :] * y_s, lnp_ref[2:3, :], lnp_ref[3:4, :])
        x1_ref[rows, :] = x1
        h2_b.append((x1 * mod_ref[1:2, :] + mod_ref[2:3, :]).astype(BF16))
    logit_t = [(_dot(h2_s, wr_ref[...]) + br_ref[...]).T for h2_s in h2_b]
    per_sort = SORT_TILE // MIX_SUB_TILE
    routes = []
    for s in range(tm // SORT_TILE):
        routes.append(_route(jnp.concatenate(logit_t[s * per_sort:(s + 1) * per_sort], axis=1), SORT_TILE))
    rec_id = lax.broadcasted_iota(jnp.int32, (ROUTE_W, SORT_TILE), 0)
    for s, (pos_row, w_lo, w_hi, chunks) in enumerate(routes):
        rec_t = jnp.zeros((ROUTE_W, SORT_TILE), F32)
        for i, part in enumerate(_exact_bf16_parts(w_lo) + _exact_bf16_parts(w_hi)):
            rec_t = jnp.where(rec_id == i, part, rec_t)
        rec_b = rec_t.T.astype(BF16)
        h2_s = jnp.concatenate(h2_b[s * per_sort:(s + 1) * per_sort], axis=0)
        sort_lo, sort_hi = _slab_sort_matrices(pos_row, SORT_TILE, 0)
        pay_lo = jnp.concatenate([h2_s[:, 0:HALF_W], rec_b], axis=1)
        pay_hi = jnp.concatenate([h2_s[:, HALF_W:], jnp.zeros((SORT_TILE, ROUTE_W), BF16)], axis=1)
        slabs = _dot(sort_lo, pay_lo) + _dot(sort_hi, pay_hi)
        for c in range(SLAB_IN_W // LANES):
            hxs_ref[c, s * LOCAL_SLAB_ROWS:(s + 1) * LOCAL_SLAB_ROWS, :] = slabs[:, c * LANES:(c + 1) * LANES]
        pos_ref[s * SORT_TILE:(s + 1) * SORT_TILE, :] = jnp.broadcast_to(pos_row, (ROUTE_W, SORT_TILE)).T
        chunks_ref[8 * s:8 * (s + 1), :] = jnp.broadcast_to(chunks, (ROUTE_W, LANES)).T[0:8, :]


def _mix_out(x, o_f, o_b, sr, ya, mod, lnp, gn, w_out_b, wr, br):
    bsz, t, _ = x.shape
    assert t % SORT_TILE == 0
    n_sort = 2 if t % (2 * SORT_TILE) == 0 else 1
    tm = n_sort * SORT_TILE
    n_t = t // tm
    tok = lambda w: pl.BlockSpec((None, tm, w), lambda b, i: (b, i, 0))
    full = lambda a: pl.BlockSpec(a.shape, lambda b, i: (0,) * a.ndim)
    flat = lambda rows, w: pl.BlockSpec((rows, w), lambda b, i: (b * n_t + i, 0))
    return pl.pallas_call(
        functools.partial(_mix_out_kernel, tm),
        grid=(bsz, n_t),
        in_specs=[
            tok(D_MODEL), tok(GLA_VAL), tok(GLA_VAL), tok(GLA_VAL), tok(CONV_CH),
            pl.BlockSpec((None, 3, D_MODEL), lambda b, i: (b, 0, 0)),
            full(lnp), full(gn), full(w_out_b), full(wr), full(br),
        ],
        out_specs=[tok(D_MODEL),
                   pl.BlockSpec((SLAB_IN_W // LANES, n_sort * LOCAL_SLAB_ROWS, LANES),
                                lambda b, i: (0, b * n_t + i, 0)),
                   flat(tm, ROUTE_W), flat(n_sort * 8, ROUTE_W)],
        out_shape=[
            jax.ShapeDtypeStruct((bsz, t, D_MODEL), F32),
            jax.ShapeDtypeStruct((SLAB_IN_W // LANES, bsz * t // SORT_TILE * LOCAL_SLAB_ROWS, LANES), F32),
            jax.ShapeDtypeStruct((bsz * t, ROUTE_W), F32),
            jax.ShapeDtypeStruct((bsz * t // SORT_TILE * 8, ROUTE_W), F32),
        ],
        compiler_params=_params(2),
        name="mix_out",
    )(x, o_f, o_b, sr, ya, mod, lnp, gn, w_out_b, wr, br)


def _moe_kernel(n_chunks, nused_ref, lo_ref, hi_ref, live_ref, src_ref, dst_ref,
                hxs_hbm, w1l_ref, w3l_ref, w2l_ref, w1h_ref, w3h_ref, w2h_ref, out_hbm, gbuf, obuf, gsem, ssem):
    tile_rows = CHUNKS_PER_TILE * SLAB_ROWS
    i = pl.program_id(0)
    n_used = nused_ref[0]
    slot = lax.bitwise_and(i, 1)

    def slab(chunk):
        return pl.ds(pl.multiple_of(chunk * SLAB_ROWS, SLAB_ROWS), SLAB_ROWS)

    def gather_copy(tile, buf_slot, j):
        chunk = src_ref[tile * CHUNKS_PER_TILE + j]
        return pltpu.make_async_copy(hxs_hbm.at[:, slab(chunk), :], gbuf.at[buf_slot, :, slab(j), :],
                                     gsem.at[buf_slot])

    def scatter_copy(tile, buf_slot, j):
        chunk = dst_ref[tile * CHUNKS_PER_TILE + j]
        return pltpu.make_async_copy(obuf.at[buf_slot, :, slab(j), :], out_hbm.at[:, slab(chunk), :],
                                     ssem.at[buf_slot])

    def start_gather(tile, buf_slot):
        for j in range(CHUNKS_PER_TILE):
            gather_copy(tile, buf_slot, j).start(priority=j % 2)

    def wait_gather(buf_slot):
        pltpu.make_async_copy(hxs_hbm.at[:, pl.ds(0, tile_rows), :], gbuf.at[buf_slot], gsem.at[buf_slot]).wait()

    def wait_scatter(buf_slot):
        pltpu.make_async_copy(obuf.at[buf_slot], out_hbm.at[:, pl.ds(0, tile_rows), :], ssem.at[buf_slot]).wait()

    @pl.when(i == 0)
    def _():
        start_gather(0, 0)
        obuf[...] = jnp.zeros(obuf.shape, F32)
        for s in range(2):
            fill = pltpu.make_async_copy(
                obuf.at[s], out_hbm.at[:, pl.ds((n_chunks + s * CHUNKS_PER_TILE) * SLAB_ROWS, tile_rows), :],
                ssem.at[s])
            fill.start()
            fill.wait()

    @pl.when(i + 1 < n_used)
    def _():
        start_gather(i + 1, 1 - slot)

    @pl.when(i < n_used)
    def _():
        wait_gather(slot)

        @pl.when(i >= 2)
        def _():
            wait_scatter(slot)

        @pl.when(live_ref[i] != 0)
        def _():
            def lane_block(c, half):
                return jnp.concatenate(
                    [gbuf[slot, c, pl.ds(2 * r + half, CHUNKS_PER_TILE, stride=SLAB_ROWS), :]
                     for r in range(CHUNK_ROWS)], axis=0)

            n_blk = HALF_W // LANES
            xb = jnp.concatenate([lane_block(c, 0) for c in range(n_blk)]
                                 + [lane_block(c, 1) for c in range(n_blk)], axis=1).astype(BF16)
            rec = lane_block(n_blk, 0)
            w_lo = rec[:, 0:1] + rec[:, 1:2] + rec[:, 2:3]
            w_hi = rec[:, 3:4] + rec[:, 4:5] + rec[:, 5:6]

            gate = [_dot(xb, w1_ref[...]) for w1_ref in (w1l_ref, w1h_ref)]
            up = [_dot(xb, w3_ref[...]) for w3_ref in (w3l_ref, w3h_ref)]
            act = [(_silu(g) * u).astype(BF16) for g, u in zip(gate, up)]
            e_lo, e_hi = [_dot(a, w2_ref[...]) for a, w2_ref in zip(act, (w2l_ref, w2h_ref))]
            y = w_lo * e_lo + w_hi * e_hi
            for r in range(CHUNK_ROWS):
                rows = slice(r * CHUNKS_PER_TILE, (r + 1) * CHUNKS_PER_TILE)
                for half in range(2):
                    for c in range(n_blk):
                        col = half * HALF_W + c * LANES
                        obuf[slot, c, pl.ds(2 * r + half, CHUNKS_PER_TILE, stride=SLAB_ROWS), :] = (
                            y[rows, col:col + LANES])

        @pl.when(live_ref[i] == 0)
        def _():
            obuf[slot] = jnp.zeros(obuf.shape[1:], F32)

        for j in range(CHUNKS_PER_TILE):
            scatter_copy(i, slot, j).start(priority=j % 2)

        @pl.when(i == n_used - 1)
        def _():
            wait_scatter(slot)

            @pl.when(i >= 1)
            def _():
                wait_scatter(1 - slot)


def _moe(hxs, src, dst, n_used, tile_lo, tile_hi, tile_live, w1_b, w3_b, w2_b):
    n_chunks = hxs.shape[1] // SLAB_ROWS
    tile_rows = CHUNKS_PER_TILE * SLAB_ROWS
    n_steps = src.shape[0] // CHUNKS_PER_TILE
    wspec = lambda which, shape: pl.BlockSpec(
        (None,) + shape, (lambda i, nu, lo, hi, lv, s, d: (lo[i], 0, 0)) if which == 0 else
        (lambda i, nu, lo, hi, lv, s, d: (hi[i], 0, 0)))
    grid_spec = pltpu.PrefetchScalarGridSpec(
        num_scalar_prefetch=6,
        grid=(n_steps,),
        in_specs=[
            pl.BlockSpec(memory_space=pl.ANY),
            wspec(0, (D_MODEL, D_EXPERT)), wspec(0, (D_MODEL, D_EXPERT)), wspec(0, (D_EXPERT, D_MODEL)),
            wspec(1, (D_MODEL, D_EXPERT)), wspec(1, (D_MODEL, D_EXPERT)), wspec(1, (D_EXPERT, D_MODEL)),
        ],
        out_specs=pl.BlockSpec(memory_space=pl.ANY),
        scratch_shapes=[
            pltpu.VMEM((2, SLAB_IN_W // LANES, tile_rows, LANES), F32),
            pltpu.VMEM((2, SLAB_OUT_W // LANES, tile_rows, LANES), F32),
            pltpu.SemaphoreType.DMA((2,)),
            pltpu.SemaphoreType.DMA((2,)),
        ],
    )
    return pl.pallas_call(
        functools.partial(_moe_kernel, n_chunks),
        grid_spec=grid_spec,
        out_shape=jax.ShapeDtypeStruct((SLAB_OUT_W // LANES, (n_chunks + 2 * CHUNKS_PER_TILE) * SLAB_ROWS, LANES), F32),
        compiler_params=_params(1),
        name="moe",
    )(n_used, tile_lo, tile_hi, tile_live, src, dst, hxs, w1_b, w3_b, w2_b, w1_b, w3_b, w2_b)


def _final_kernel(n_sort, x1_ref, moe_ref, pos_ref, mod_ref, lnp_ref, o_ref):
    moe = []
    for s in range(n_sort):
        slab_rows = slice(s * LOCAL_SLAB_ROWS, (s + 1) * LOCAL_SLAB_ROWS)
        moe_b = jnp.concatenate([moe_ref[c, slab_rows, :] for c in range(SLAB_OUT_W // LANES)], axis=1).astype(BF16)
        sort_lo, sort_hi = _slab_sort_matrices(pos_ref[s * SORT_TILE:(s + 1) * SORT_TILE, 0:1], SORT_TILE, 1)
        for r in range(0, SORT_TILE, MIX_SUB_TILE):
            rows = slice(r, r + MIX_SUB_TILE)
            moe.append((s * SORT_TILE + r,
                        jnp.concatenate([_dot(sort_lo[rows, :], moe_b), _dot(sort_hi[rows, :], moe_b)], axis=1)))
    for start, moe_s in moe:
        rows = slice(start, start + MIX_SUB_TILE)
        o_ref[rows, :] = _layer_norm(DEEPNORM_ALPHA * x1_ref[rows, :] + mod_ref[...] * moe_s,
                                     lnp_ref[0:1, :], lnp_ref[1:2, :])


def _final(x1, moe, pos, g2, lnp):
    bsz, t, _ = x1.shape
    n_sort = max(n for n in (1, 2, 4) if t % (n * SORT_TILE) == 0)
    tm = n_sort * SORT_TILE
    n_t = t // tm
    flat = lambda rows, w: pl.BlockSpec((rows, w), lambda b, i: (b * n_t + i, 0))
    return pl.pallas_call(
        functools.partial(_final_kernel, n_sort),
        grid=(bsz, n_t),
        in_specs=[
            pl.BlockSpec((None, tm, D_MODEL), lambda b, i: (b, i, 0)),
            pl.BlockSpec((SLAB_OUT_W // LANES, n_sort * LOCAL_SLAB_ROWS, LANES), lambda b, i: (0, b * n_t + i, 0)),
            flat(tm, ROUTE_W),
            pl.BlockSpec((None, 1, D_MODEL), lambda b, i: (b, 0, 0)),
            pl.BlockSpec(lnp.shape, lambda b, i: (0, 0)),
        ],
        out_specs=pl.BlockSpec((None, tm, D_MODEL), lambda b, i: (b, i, 0)),
        out_shape=jax.ShapeDtypeStruct((bsz, t, D_MODEL), F32),
        compiler_params=_params(2),
        name="final",
    )(x1, moe, pos, g2, lnp)


def _pair_tables():
    lo, hi = [], []
    for g in range(N_GROUPS):
        for a in range(EXPERTS_PER_GROUP):
            for b in range(a + 1, EXPERTS_PER_GROUP):
                lo.append(g * EXPERTS_PER_GROUP + a)
                hi.append(g * EXPERTS_PER_GROUP + b)
    return jnp.array(lo, jnp.int32), jnp.array(hi, jnp.int32)


def _moe_plan(chunks, n_sort_tiles):
    n_cls = N_CLASSES + 1
    hp = lax.Precision.HIGHEST
    m = chunks.reshape(n_sort_tiles, 8, ROUTE_W)[:, 0, :n_cls].astype(jnp.int32)
    a_end = jnp.cumsum(m, axis=0)
    a_start = a_end - m
    per_cls = a_end[-1]
    padded = (per_cls + CHUNKS_PER_TILE - 1) // CHUNKS_PER_TILE * CHUNKS_PER_TILE
    g_end = jnp.cumsum(padded)
    g_start = g_end - padded
    local_off = jnp.cumsum(m, axis=1) - m
    seg = jnp.arange(n_sort_tiles, dtype=jnp.int32)[:, None] * LOCAL_CHUNKS + local_off - a_start
    n_steps = -(-(n_sort_tiles * LOCAL_CHUNKS) // CHUNKS_PER_TILE) + n_cls
    p = jnp.arange(n_steps * CHUNKS_PER_TILE, dtype=jnp.int32)
    cls_p = jnp.minimum(jnp.sum((g_end[None, :] <= p[:, None]).astype(jnp.int32), axis=1), n_cls - 1)
    onehot = (cls_p[:, None] == jnp.arange(n_cls, dtype=jnp.int32)[None, :]).astype(F32)
    pick = lambda tab: jnp.dot(onehot, tab.astype(F32), precision=hp)
    u = p - pick(g_start[:, None])[:, 0].astype(jnp.int32)
    valid = u < pick(per_cls[:, None])[:, 0].astype(jnp.int32)
    a_end_p = pick(a_end.T).astype(jnp.int32)
    seg_p = pick(seg.T).astype(jnp.int32)
    tile_p = jnp.sum((a_end_p <= u[:, None]).astype(jnp.int32), axis=1)
    hit = jnp.arange(n_sort_tiles, dtype=jnp.int32)[None, :] == tile_p[:, None]
    src = jnp.sum(jnp.where(hit, seg_p, 0), axis=1) + u
    pad_dst = n_sort_tiles * LOCAL_CHUNKS + (p // CHUNKS_PER_TILE) % 2 * CHUNKS_PER_TILE + p % CHUNKS_PER_TILE
    dst = jnp.where(valid, src, pad_dst).astype(jnp.int32)
    src = jnp.where(valid, src, 0).astype(jnp.int32)
    n_used = g_end[-1:] // CHUNKS_PER_TILE
    step = jnp.arange(n_steps, dtype=jnp.int32)
    tile_cls = jnp.sum((g_end[None, :] // CHUNKS_PER_TILE <= step[:, None]).astype(jnp.int32), axis=1)
    live = ((tile_cls < N_CLASSES) & (step < n_used[0])).astype(jnp.int32)
    pair_lo, pair_hi = _pair_tables()
    pair_oh = (jnp.minimum(tile_cls, N_CLASSES - 1)[:, None] == jnp.arange(N_CLASSES)[None, :]).astype(jnp.int32)
    tile_lo = jnp.sum(pair_oh * pair_lo[None, :], axis=1).astype(jnp.int32)
    tile_hi = jnp.sum(pair_oh * pair_hi[None, :], axis=1).astype(jnp.int32)
    return src, dst, n_used.astype(jnp.int32), tile_lo, tile_hi, live


def kernel(x, c, ctx, c_ctx, ln_in_g, ln_in_b, w_ada, b_ada, w_in, conv_w, conv_b, gate_w2_fwd, gate_b_fwd,
           gate_w2_bwd, gate_b_bwd, gla_norm_g, w_out, ln1_g, ln1_b, router_group_w, router_group_b,
           router_expert_w, router_expert_b, expert_w1, expert_w3, expert_w2, ln2_g, ln2_b):
    bsz, t, _ = x.shape
    n_tok = bsz * t
    l = 0
    rows = -(-(bsz + 1) // 8) * 8
    cond = jnp.zeros((rows, D_MODEL), F32).at[:bsz].set(c).at[bsz].set(c_ctx)
    ada = _ada(cond, w_ada[l], b_ada[l][None, :])
    sh1, sc1, g1, sh2, sc2, g2 = [ada[:, i * D_MODEL:(i + 1) * D_MODEL] for i in range(6)]

    w_in_b = w_in[l].astype(BF16)
    lnp_in = jnp.stack([ln_in_g, ln_in_b])
    zero = jnp.zeros((GLA_GATE_RANK, GLA_KEY), F32)
    w2cat = jnp.concatenate([jnp.concatenate([gate_w2_fwd[l], zero], axis=1),
                             jnp.concatenate([zero, gate_w2_bwd[l]], axis=1)], axis=0).astype(BF16)
    gbias = jnp.concatenate([gate_b_fwd[l], gate_b_bwd[l]])[None, :]

    mod_ctx = jnp.broadcast_to(jnp.stack([1.0 + sc1[bsz], sh1[bsz]])[None], (bsz, 2, D_MODEL))
    k_c, v_c, g_c = _proj(ctx, mod_ctx, lnp_in, w_in_b, conv_w[l], conv_b[l][None, :], w2cat, gbias, False)
    zero_state = jnp.zeros((bsz, GLA_KEY, PAIR_VAL), F32)
    s_f, s_b = _gla(None, k_c, v_c, g_c, zero_state, zero_state)

    mod1 = jnp.stack([1.0 + sc1[:bsz], sh1[:bsz]], axis=1)
    ya, q, k, v, sr, g = _proj(x, mod1, lnp_in, w_in_b, conv_w[l], conv_b[l][None, :], w2cat, gbias, True)
    o_f, o_b, _, _ = _gla(q, k, v, g, s_f, s_b)

    mod2 = jnp.stack([g1[:bsz], 1.0 + sc2[:bsz], sh2[:bsz]], axis=1)
    lnp1 = jnp.stack([ln_in_g, ln_in_b, ln1_g[l], ln1_b[l]])
    wr = jnp.zeros((D_MODEL, ROUTE_W), F32)
    wr = wr.at[:, :N_GROUPS].set(router_group_w[l]).at[:, N_GROUPS:N_GROUPS + N_EXPERTS].set(router_expert_w[l])
    br = jnp.zeros((1, ROUTE_W), F32)
    br = br.at[0, :N_GROUPS].set(router_group_b[l]).at[0, N_GROUPS:N_GROUPS + N_EXPERTS].set(router_expert_b[l])
    x1, hxs, pos, chunks = _mix_out(x, o_f, o_b, sr, ya, mod2, lnp1, gla_norm_g[l][None, :],
                                    w_out[l].astype(BF16), wr.astype(BF16), br)

    src, dst, n_used, tile_lo, tile_hi, live = _moe_plan(chunks, n_tok // SORT_TILE)
    moe = _moe(hxs, src, dst, n_used, tile_lo, tile_hi, live,
               expert_w1[l].astype(BF16), expert_w3[l].astype(BF16), expert_w2[l].astype(BF16))

    return _final(x1, moe, pos, g2[:bsz][:, None, :], jnp.stack([ln2_g[l], ln2_b[l]]))
```

```python
import functools

import jax
import jax.numpy as jnp
from jax import lax
from jax.experimental import pallas as pl
from jax.experimental.pallas import tpu as pltpu

F32 = jnp.float32
BF16 = jnp.bfloat16

D_MODEL = 1024
GRID_W = 64
CONV_CH = 512
GLA_HEADS = 4
GLA_DK = 64
GLA_DV = 128
GLA_KEY = GLA_HEADS * GLA_DK
GLA_VAL = GLA_HEADS * GLA_DV
PAIR_KEY = 2 * GLA_DK
PAIR_VAL = 2 * GLA_DV
GLA_GATE_RANK = 16
GLA_TAU = 16.0
OFF_AB = 0
OFF_AC = OFF_AB + CONV_CH
OFF_AX = OFF_AC + CONV_CH
OFF_Q = OFF_AX + CONV_CH
OFF_K = OFF_Q + GLA_KEY
OFF_V = OFF_K + GLA_KEY
OFF_R = OFF_V + GLA_VAL
OFF_GF = OFF_R + GLA_VAL
D_PROJ = OFF_GF + 2 * GLA_GATE_RANK
N_GROUPS = 4
EXPERTS_PER_GROUP = 4
N_EXPERTS = N_GROUPS * EXPERTS_PER_GROUP
D_EXPERT = 512
PAIRS_PER_GROUP = 6
N_CLASSES = N_GROUPS * PAIRS_PER_GROUP
LN_EPS = 1e-5
RMS_EPS = 1e-6
DEPTH = 1
DEEPNORM_ALPHA = (2.0 * DEPTH) ** 0.25

LANES = 128
GLA_CHUNK = 64
GLA_SUB = 16
N_SUB = GLA_CHUNK // GLA_SUB
ROUTE_W = LANES
HALF_W = D_MODEL // 2
SLAB_IN_W = HALF_W + ROUTE_W
SLAB_OUT_W = HALF_W
SORT_TILE = 256
MIX_SUB_TILE = 128
MOE_TILE = 256
CHUNK_ROWS = 4
SLAB_ROWS = 2 * CHUNK_ROWS
LOCAL_CHUNKS = -(-(SORT_TILE + N_CLASSES * (CHUNK_ROWS - 1)) // CHUNK_ROWS)
LOCAL_SLAB_ROWS = LOCAL_CHUNKS * SLAB_ROWS
CHUNKS_PER_TILE = MOE_TILE // CHUNK_ROWS
VMEM_LIMIT = 56 * 1024 * 1024


def _params(n_axes, vmem=VMEM_LIMIT):
    return pltpu.CompilerParams(dimension_semantics=("arbitrary",) * n_axes, vmem_limit_bytes=vmem)


def _dot(a, b):
    return jnp.dot(a, b, preferred_element_type=F32)


def _div_pow2(x, d):
    assert d & (d - 1) == 0
    return lax.shift_right_logical(x, jnp.int32(d.bit_length() - 1))


def _mod_pow2(x, d):
    assert d & (d - 1) == 0
    return lax.bitwise_and(x, jnp.int32(d - 1))


def _split2(x):
    hi = x.astype(BF16)
    lo = (x - hi.astype(F32)).astype(BF16)
    return hi, lo


def _dot3(a, b):
    ah, al = _split2(a)
    bh, bl = _split2(b)
    return _dot(ah, bh) + _dot(ah, bl) + _dot(al, bh)


def _silu(x):
    return x * (0.5 * jnp.tanh(0.5 * x) + 0.5)


def _layer_norm(x, g, b):
    mu = jnp.mean(x, axis=-1, keepdims=True)
    xc = x - mu
    var = jnp.mean(xc * xc, axis=-1, keepdims=True)
    return xc * lax.rsqrt(var + LN_EPS) * g + b


def _ada_kernel(c_ref, w_ref, b_ref, o_ref):
    o_ref[...] = _dot3(_silu(c_ref[...]), w_ref[...]) + b_ref[...]


def _ada(cond, w_ada, b_ada):
    rows = cond.shape[0]
    n_out = w_ada.shape[1]
    tn = 1024
    return pl.pallas_call(
        _ada_kernel,
        grid=(n_out // tn,),
        in_specs=[
            pl.BlockSpec((rows, D_MODEL), lambda j: (0, 0)),
            pl.BlockSpec((D_MODEL, tn), lambda j: (0, j)),
            pl.BlockSpec((1, tn), lambda j: (0, j)),
        ],
        out_specs=pl.BlockSpec((rows, tn), lambda j: (0, j)),
        out_shape=jax.ShapeDtypeStruct((rows, n_out), F32),
        compiler_params=_params(1),
        name="ada",
    )(cond, w_ada, b_ada)


def _log_sigmoid(z):
    return jnp.minimum(z, 0.0) - jnp.log(1.0 + jnp.exp(-jnp.abs(z)))


def _proj_kernel(latent, tm, x_ref, mod_ref, lnp_ref, w_ref, cw_ref, cb_ref, w2_ref, gbias_ref, *out_refs):
    x = x_ref[...]
    xn = _layer_norm(x, lnp_ref[0:1, :], lnp_ref[1:2, :])
    h = xn * mod_ref[0:1, :] + mod_ref[1:2, :]
    hb = h.astype(BF16)
    if latent:
        ya_ref, q_ref, k_ref, v_ref, sr_ref, g_ref = out_refs
        p = _dot(hb, w_ref[:, OFF_AB:OFF_Q])
        a_b = p[:, 0:CONV_CH]
        u = p[:, CONV_CH:2 * CONV_CH] * p[:, 2 * CONV_CH:3 * CONV_CH]
        pos = _mod_pow2(lax.broadcasted_iota(jnp.int32, (tm, 1), 0), GRID_W)
        u_prev = jnp.where(pos == 0, 0.0, pltpu.roll(u, 1, 0))
        u_next = jnp.where(pos == GRID_W - 1, 0.0, pltpu.roll(u, tm - 1, 0))
        conv = u_prev * cw_ref[0:1, :] + u * cw_ref[1:2, :] + u_next * cw_ref[2:3, :] + cb_ref[...]
        ya_ref[...] = (a_b * conv).astype(BF16)
        qk = _dot(hb, w_ref[:, OFF_Q:OFF_V])
        q_ref[...] = (qk[:, 0:GLA_KEY] * (GLA_DK ** -0.5)).astype(BF16)
        k_ref[...] = qk[:, GLA_KEY:].astype(BF16)
        r = _dot(hb, w_ref[:, OFF_R:OFF_GF])
        sr_ref[...] = _silu(r).astype(BF16)
    else:
        k_ref, v_ref, g_ref = out_refs
        k_ref[...] = _dot(hb, w_ref[:, OFF_K:OFF_V]).astype(BF16)
    v_ref[...] = _dot(hb, w_ref[:, OFF_V:OFF_R]).astype(BF16)
    low = _dot(hb, w_ref[:, OFF_GF:D_PROJ])
    z = _dot(low.astype(BF16), w2_ref[...]) + gbias_ref[...]
    g_ref[...] = _log_sigmoid(z) * (1.0 / GLA_TAU)


def _proj(x, mod, lnp, w_in_b, conv_w, conv_b, w2cat, gbias, latent):
    bsz, t, _ = x.shape
    tm = min(1024, t)
    assert t % tm == 0 and tm % GRID_W == 0
    tok = lambda w: pl.BlockSpec((None, tm, w), lambda b, i: (b, i, 0))
    full = lambda a: pl.BlockSpec(a.shape, lambda b, i: (0,) * a.ndim)
    widths = ([(CONV_CH, BF16), (GLA_KEY, BF16)] if latent else []) + [(GLA_KEY, BF16), (GLA_VAL, BF16)]
    widths += ([(GLA_VAL, BF16)] if latent else []) + [(2 * GLA_KEY, F32)]
    return pl.pallas_call(
        functools.partial(_proj_kernel, latent, tm),
        grid=(bsz, t // tm),
        in_specs=[
            tok(D_MODEL),
            pl.BlockSpec((None, 2, D_MODEL), lambda b, i: (b, 0, 0)),
            full(lnp), full(w_in_b), full(conv_w), full(conv_b), full(w2cat), full(gbias),
        ],
        out_specs=[tok(w) for w, _ in widths],
        out_shape=[jax.ShapeDtypeStruct((bsz, t, w), dt) for w, dt in widths],
        compiler_params=_params(2),
        name="proj_latent" if latent else "proj_ctx",
    )(x, mod, lnp, w_in_b, conv_w, conv_b, w2cat, gbias)


def _tri(n, reverse, strict=False):
    i = lax.broadcasted_iota(jnp.int32, (n, n), 0)
    j = lax.broadcasted_iota(jnp.int32, (n, n), 1)
    if strict:
        m = (j > i) if reverse else (j < i)
    else:
        m = (j >= i) if reverse else (j <= i)
    return jnp.where(m, 1.0, 0.0).astype(BF16)


def _chunk_cumsum(g, reverse):
    tri = _tri(GLA_CHUNK, reverse)
    g_hi, g_lo = _split2(g)
    return _dot(tri, g_hi) + _dot(tri, g_lo)


def _as_column(row):
    return jnp.broadcast_to(row, (LANES, row.shape[1])).T


def _sub_anchors(gc, reverse):
    zero = jnp.zeros((1, GLA_KEY), F32)
    if reverse:
        return [gc[GLA_SUB * (a + 1):GLA_SUB * (a + 1) + 1] for a in range(N_SUB - 1)] + [zero]
    return [zero] + [gc[GLA_SUB * a - 1:GLA_SUB * a] for a in range(1, N_SUB)]


def _score_pairs(reverse):
    return [(a, b) for a in range(N_SUB) for b in range(N_SUB) if (b >= a if reverse else b <= a)]


def _intra_products(q, k, gc, reverse):
    r = _sub_anchors(gc, reverse)
    anchor = jnp.concatenate([jnp.broadcast_to(ra, (GLA_SUB, GLA_KEY)) for ra in r], axis=0)
    gcb = gc - anchor
    qt = q * jnp.exp(gcb)
    kt = k * jnp.exp(-gcb)
    rows = []
    for a, b in _score_pairs(reverse):
        qa = qt[GLA_SUB * a:GLA_SUB * (a + 1)]
        if a != b:
            qa = qa * jnp.exp(r[a] - r[b])
        rows.append(qa)
    qp = jnp.concatenate(rows, axis=0).astype(BF16)
    width = GLA_HEADS * GLA_CHUNK
    rr = lax.broadcasted_iota(jnp.int32, (width, GLA_KEY), 0)
    cc = lax.broadcasted_iota(jnp.int32, (width, GLA_KEY), 1)
    kbd = jnp.where(_div_pow2(rr, GLA_CHUNK) == _div_pow2(cc, GLA_DK),
                    jnp.concatenate([kt] * GLA_HEADS, axis=0), 0.0)
    return lax.dot_general(qp, kbd.astype(BF16), (((1,), (1,)), ((), ())), preferred_element_type=F32)


def _assemble_scores(res, reverse):
    pairs = _score_pairs(reverse)
    width = GLA_HEADS * GLA_CHUNK
    col = _mod_pow2(lax.broadcasted_iota(jnp.int32, (GLA_SUB, width), 1), GLA_CHUNK)
    col_blk = _div_pow2(col, GLA_SUB)
    col_pos = _mod_pow2(col, GLA_SUB)
    row_pos = lax.broadcasted_iota(jnp.int32, (GLA_SUB, width), 0)
    causal = (col_pos >= row_pos) if reverse else (col_pos <= row_pos)
    blocks = []
    for a in range(N_SUB):
        acc = jnp.zeros((GLA_SUB, width), F32)
        for idx, (pa, pb) in enumerate(pairs):
            if pa != a:
                continue
            keep = col_blk == pb
            if pa == pb:
                keep = keep & causal
            acc = acc + jnp.where(keep, res[GLA_SUB * idx:GLA_SUB * (idx + 1)], 0.0)
        blocks.append(acc)
    return jnp.concatenate(blocks, axis=0)


def _pair_mask(rows_per_head, cols_per_head, n_row_pairs=1):
    shape = (n_row_pairs * 2 * rows_per_head, 2 * cols_per_head)
    rr = _mod_pow2(lax.broadcasted_iota(jnp.int32, shape, 0), 2 * rows_per_head)
    cc = lax.broadcasted_iota(jnp.int32, shape, 1)
    return _div_pow2(rr, rows_per_head) == _div_pow2(cc, cols_per_head)


def _state_terms(q, k, v_b, gc, reverse):
    total = gc[0:1] if reverse else gc[GLA_CHUNK - 1:GLA_CHUNK]
    q_dec = None if q is None else (q * jnp.exp(gc)).astype(BF16)
    k_end = (k * jnp.exp(total - gc)).astype(BF16)
    tn = (((0,), (0,)), ((), ()))
    upd = [lax.dot_general(k_end[:, p * PAIR_KEY:(p + 1) * PAIR_KEY], v_b[:, p * PAIR_VAL:(p + 1) * PAIR_VAL], tn,
                           preferred_element_type=F32) for p in range(GLA_HEADS // 2)]
    upd = jnp.where(_pair_mask(GLA_DK, GLA_DV, GLA_HEADS // 2), jnp.concatenate(upd, axis=0), 0.0)
    decay = jnp.exp(_as_column(total))
    decay = jnp.concatenate([decay] * (PAIR_VAL // LANES), axis=1)
    return q_dec, decay, upd


def _advance_state(q_dec, decay, upd, state):
    o_inter = None
    if q_dec is not None:
        state_b = state.astype(BF16)
        o_inter = jnp.concatenate(
            [_dot(q_dec[:, p * PAIR_KEY:(p + 1) * PAIR_KEY], state_b[p * PAIR_KEY:(p + 1) * PAIR_KEY, :])
             for p in range(GLA_HEADS // 2)], axis=1)
    return o_inter, state * decay + upd


def _gla_kernel(tt, nt, with_out, *refs):
    if with_out:
        (qf_ref, kf_ref, vf_ref, gfw_ref, qb_ref, kb_ref, vb_ref, gbw_ref, s0f_ref, s0b_ref,
         of_ref, ob_ref, sf_ref, sb_ref) = refs
    else:
        kf_ref, vf_ref, gfw_ref, kb_ref, vb_ref, gbw_ref, s0f_ref, s0b_ref, sf_ref, sb_ref = refs
    j = pl.program_id(1)

    @pl.when(j == 0)
    def _():
        sf_ref[...] = s0f_ref[...]
        sb_ref[...] = s0b_ref[...]

    chunk_slices = [slice(c * GLA_CHUNK, (c + 1) * GLA_CHUNK) for c in range(tt // GLA_CHUNK)]
    fwd, bwd = [], []
    for sl in chunk_slices:
        g = gfw_ref[sl, :]
        fwd.append(dict(sl=sl, k=kf_ref[sl, :].astype(F32), v=vf_ref[sl, :], g_f=g[:, 0:GLA_KEY], g_b=g[:, GLA_KEY:],
                        q=qf_ref[sl, :].astype(F32) if with_out else None))
        bwd.append(dict(sl=sl, k=kb_ref[sl, :].astype(F32), v=vb_ref[sl, :], g_b=gbw_ref[sl, :][:, GLA_KEY:],
                        q=qb_ref[sl, :].astype(F32) if with_out else None))
    for d in fwd:
        d["gc_f"] = _chunk_cumsum(d["g_f"], False)
        if with_out:
            d["gc_b"] = _chunk_cumsum(d["g_b"], True)
    for d in bwd:
        d["gc_b"] = _chunk_cumsum(d["g_b"], True)
    if with_out:
        for d in fwd:
            d["res_f"] = _intra_products(d["q"], d["k"], d["gc_f"], False)
            d["res_b"] = _intra_products(d["q"], d["k"], d["gc_b"], True)
    for d in fwd:
        d["terms"] = _state_terms(d["q"], d["k"], d["v"], d["gc_f"], False)
    for d in bwd:
        d["terms"] = _state_terms(d["q"], d["k"], d["v"], d["gc_b"], True)
    if with_out:
        for d in fwd:
            scores = (_assemble_scores(d["res_f"], False) + _assemble_scores(d["res_b"], True)).astype(BF16)
            o_intra = []
            for p in range(GLA_HEADS // 2):
                v_p = d["v"][:, p * PAIR_VAL:(p + 1) * PAIR_VAL]
                vbd = jnp.where(_pair_mask(GLA_CHUNK, GLA_DV), jnp.concatenate([v_p, v_p], axis=0),
                                jnp.zeros((), BF16))
                o_intra.append(_dot(scores[:, p * 2 * GLA_CHUNK:(p + 1) * 2 * GLA_CHUNK], vbd))
            d["o_intra"] = jnp.concatenate(o_intra, axis=1)

    state = sf_ref[...]
    for d in fwd:
        o_inter, state = _advance_state(*d["terms"], state)
        if with_out:
            of_ref[d["sl"], :] = (d["o_intra"] + o_inter).astype(BF16)
    sf_ref[...] = state

    state = sb_ref[...]
    for d in reversed(bwd):
        o_inter, state = _advance_state(*d["terms"], state)
        if with_out:
            ob_ref[d["sl"], :] = o_inter.astype(BF16)
    sb_ref[...] = state


def _gla(q, k, v, g, s0f, s0b):
    with_out = q is not None
    bsz, t, _ = k.shape
    tt = min(1024, t)
    assert t % tt == 0 and tt % GLA_CHUNK == 0
    nt = t // tt
    fwd = lambda w: pl.BlockSpec((None, tt, w), lambda b, j: (b, j, 0))
    bwd = lambda w: pl.BlockSpec((None, tt, w), lambda b, j: (b, nt - 1 - j, 0))
    st = pl.BlockSpec((None, GLA_KEY, PAIR_VAL), lambda b, j: (b, 0, 0))
    st_shape = jax.ShapeDtypeStruct((bsz, GLA_KEY, PAIR_VAL), F32)
    if with_out:
        ins = [q, k, v, g, q, k, v, g, s0f, s0b]
        in_specs = [fwd(GLA_KEY), fwd(GLA_KEY), fwd(GLA_VAL), fwd(2 * GLA_KEY),
                    bwd(GLA_KEY), bwd(GLA_KEY), bwd(GLA_VAL), bwd(2 * GLA_KEY), st, st]
        out_specs = [fwd(GLA_VAL), bwd(GLA_VAL), st, st]
        o_shape = jax.ShapeDtypeStruct((bsz, t, GLA_VAL), BF16)
        out_shape = [o_shape, o_shape, st_shape, st_shape]
    else:
        ins = [k, v, g, k, v, g, s0f, s0b]
        in_specs = [fwd(GLA_KEY), fwd(GLA_VAL), fwd(2 * GLA_KEY),
                    bwd(GLA_KEY), bwd(GLA_VAL), bwd(2 * GLA_KEY), st, st]
        out_specs = [st, st]
        out_shape = [st_shape, st_shape]
    return pl.pallas_call(
        functools.partial(_gla_kernel, tt, nt, with_out),
        grid=(bsz, nt),
        in_specs=in_specs,
        out_specs=out_specs,
        out_shape=out_shape,
        compiler_params=_params(2),
        name="gla_latent" if with_out else "gla_ctx",
    )(*ins)


def _exact_bf16_parts(x):
    hi = x.astype(BF16).astype(F32)
    r = x - hi
    mid = r.astype(BF16).astype(F32)
    lo = (r - mid).astype(BF16).astype(F32)
    return hi, mid, lo


def _first_index(values, best):
    idx = jnp.full_like(best, float(len(values) - 1))
    for i in reversed(range(len(values) - 1)):
        idx = jnp.where(values[i] >= best, float(i), idx)
    return idx


def _pick(rows, idx):
    out = rows[-1]
    for i in reversed(range(len(rows) - 1)):
        out = jnp.where(idx == float(i), rows[i], out)
    return out


def _route(logit_t, tm):
    row = lambda r: logit_t[r:r + 1, :]
    groups = [row(i) for i in range(N_GROUPS)]
    top = functools.reduce(jnp.maximum, groups)
    eg = [jnp.exp(x - top) for x in groups]
    total = functools.reduce(lambda a, b: a + b, eg)
    pg = [e / total for e in eg]
    p_g = functools.reduce(jnp.maximum, pg)
    g_idx = _first_index(pg, p_g)
    sel = [_pick([row(N_GROUPS + EXPERTS_PER_GROUP * g + j) for g in range(N_GROUPS)], g_idx)
           for j in range(EXPERTS_PER_GROUP)]
    top = functools.reduce(jnp.maximum, sel)
    ee = [jnp.exp(x - top) for x in sel]
    total = functools.reduce(lambda a, b: a + b, ee)
    pe = [e / total for e in ee]
    p1 = functools.reduce(jnp.maximum, pe)
    l1 = _first_index(pe, p1)
    pe2 = [jnp.where(l1 == float(j), -1.0, pe[j]) for j in range(EXPERTS_PER_GROUP)]
    p2 = functools.reduce(jnp.maximum, pe2)
    l2 = _first_index(pe2, p2)
    den = p1 + p2
    w1 = p1 / den * p_g
    w2 = p2 / den * p_g
    lo = jnp.minimum(l1, l2)
    hi = jnp.maximum(l1, l2)
    pair = lo * (7.0 - lo) * 0.5 + (hi - lo - 1.0)
    cls = g_idx * PAIRS_PER_GROUP + pair
    w_lo = jnp.where(l1 < l2, w1, w2)
    w_hi = jnp.where(l1 < l2, w2, w1)
    cls_id = lax.broadcasted_iota(jnp.int32, (ROUTE_W, tm), 0).astype(F32)
    onehot = jnp.where(cls_id == cls, 1.0, 0.0)
    before = _dot(onehot.astype(BF16), _tri(tm, True, strict=True))
    count = jnp.sum(onehot, axis=1, keepdims=True)
    chunks = jnp.floor((count + (CHUNK_ROWS - 1.0)) * (1.0 / CHUNK_ROWS))
    chunks = jnp.where(cls_id[:, 0:1] == float(N_CLASSES),
                       LOCAL_CHUNKS - jnp.sum(chunks, axis=0, keepdims=True), chunks)
    first_chunk = _dot(_tri(ROUTE_W, False, strict=True),
                       jnp.broadcast_to(chunks, (ROUTE_W, LANES)).astype(BF16))[:, 0:1]
    pos_row = jnp.sum(onehot * (CHUNK_ROWS * first_chunk + before), axis=0, keepdims=True)
    return pos_row, w_lo, w_hi, chunks


def _slab_sort_matrices(pos, tm, slab_axis):
    shape = (LOCAL_SLAB_ROWS, tm) if slab_axis == 0 else (tm, LOCAL_SLAB_ROWS)
    slab_row = lax.broadcasted_iota(jnp.int32, shape, slab_axis)
    sub = _mod_pow2(slab_row, SLAB_ROWS)
    token_row = CHUNK_ROWS * _div_pow2(slab_row, SLAB_ROWS) + _div_pow2(sub, 2)
    hit = token_row.astype(F32) == pos
    half = _mod_pow2(sub, 2)
    return [jnp.where(hit & (half == h), 1.0, 0.0).astype(BF16) for h in range(2)]


def _mix_out_kernel(tm, x_ref, of_ref, ob_ref, sr_ref, ya_ref, mod_ref, lnp_ref, gn_ref, wo_ref, wr_ref, br_ref,
                    x1_ref, hxs_ref, pos_ref, chunks_ref):
    subs = [slice(s, s + MIX_SUB_TILE) for s in range(0, tm, MIX_SUB_TILE)]
    yb = []
    for rows in subs:
        o = of_ref[rows, :].astype(F32) + ob_ref[rows, :].astype(F32)
        sr = sr_ref[rows, :].astype(F32)
        heads = []
        for h in range(GLA_HEADS):
            sl = slice(h * GLA_DV, (h + 1) * GLA_DV)
            oh = o[:, sl]
            ms = jnp.mean(oh * oh, axis=-1, keepdims=True)
            heads.append((oh * lax.rsqrt(ms + RMS_EPS) * gn_ref[...] * sr[:, sl]).astype(BF16))
        yb.append(heads)
    xn = [_layer_norm(x_ref[rows, :], lnp_ref[0:1, :], lnp_ref[1:2, :]) for rows in subs]
    y = []
    for rows, heads in zip(subs, yb):
        acc = _dot(ya_ref[rows, :], wo_ref[0:CONV_CH, :])
        for h in range(GLA_HEADS):
            acc = acc + _dot(heads[h], wo_ref[CONV_CH + h * GLA_DV:CONV_CH + (h + 1) * GLA_DV, :])
        y.append(acc)
    h2_b = []
    for rows, xn_s, y_s in zip(subs, xn, y):
        x1 = _layer_norm(DEEPNORM_ALPHA * xn_s + mod_ref[0:1, :] * y_s, lnp_ref[2:3, :], lnp_ref[3:4, :])
        x1_ref[rows, :] = x1
        h2_b.append((x1 * mod_ref[1:2, :] + mod_ref[2:3, :]).astype(BF16))
    logit_t = [(_dot(h2_s, wr_ref[...]) + br_ref[...]).T for h2_s in h2_b]
    per_sort = SORT_TILE // MIX_SUB_TILE
    routes = []
    for s in range(tm // SORT_TILE):
        routes.append(_route(jnp.concatenate(logit_t[s * per_sort:(s + 1) * per_sort], axis=1), SORT_TILE))
    rec_id = lax.broadcasted_iota(jnp.int32, (ROUTE_W, SORT_TILE), 0)
    for s, (pos_row, w_lo, w_hi, chunks) in enumerate(routes):
        rec_t = jnp.zeros((ROUTE_W, SORT_TILE), F32)
        for i, part in enumerate(_exact_bf16_parts(w_lo) + _exact_bf16_parts(w_hi)):
            rec_t = jnp.where(rec_id == i, part, rec_t)
        rec_b = rec_t.T.astype(BF16)
        h2_s = jnp.concatenate(h2_b[s * per_sort:(s + 1) * per_sort], axis=0)
        sort_lo, sort_hi = _slab_sort_matrices(pos_row, SORT_TILE, 0)
        pay_lo = jnp.concatenate([h2_s[:, 0:HALF_W], rec_b], axis=1)
        pay_hi = jnp.concatenate([h2_s[:, HALF_W:], jnp.zeros((SORT_TILE, ROUTE_W), BF16)], axis=1)
        slabs = _dot(sort_lo, pay_lo) + _dot(sort_hi, pay_hi)
        for c in range(SLAB_IN_W // LANES):
            hxs_ref[c, s * LOCAL_SLAB_ROWS:(s + 1) * LOCAL_SLAB_ROWS, :] = slabs[:, c * LANES:(c + 1) * LANES]
        pos_ref[s * SORT_TILE:(s + 1) * SORT_TILE, :] = jnp.broadcast_to(pos_row, (ROUTE_W, SORT_TILE)).T
        chunks_ref[8 * s:8 * (s + 1), :] = jnp.broadcast_to(chunks, (ROUTE_W, LANES)).T[0:8, :]


def _mix_out(x, o_f, o_b, sr, ya, mod, lnp, gn, w_out_b, wr, br):
    bsz, t, _ = x.shape
    assert t % SORT_TILE == 0
    n_sort = max(n for n in (1, 2, 4) if t % (n * SORT_TILE) == 0)
    tm = n_sort * SORT_TILE
    n_t = t // tm
    tok = lambda w: pl.BlockSpec((None, tm, w), lambda b, i: (b, i, 0))
    full = lambda a: pl.BlockSpec(a.shape, lambda b, i: (0,) * a.ndim)
    flat = lambda rows, w: pl.BlockSpec((rows, w), lambda b, i: (b * n_t + i, 0))
    return pl.pallas_call(
        functools.partial(_mix_out_kernel, tm),
        grid=(bsz, n_t),
        in_specs=[
            tok(D_MODEL), tok(GLA_VAL), tok(GLA_VAL), tok(GLA_VAL), tok(CONV_CH),
            pl.BlockSpec((None, 3, D_MODEL), lambda b, i: (b, 0, 0)),
            full(lnp), full(gn), full(w_out_b), full(wr), full(br),
        ],
        out_specs=[tok(D_MODEL),
                   pl.BlockSpec((SLAB_IN_W // LANES, n_sort * LOCAL_SLAB_ROWS, LANES),
                                lambda b, i: (0, b * n_t + i, 0)),
                   flat(tm, ROUTE_W), flat(n_sort * 8, ROUTE_W)],
        out_shape=[
            jax.ShapeDtypeStruct((bsz, t, D_MODEL), F32),
            jax.ShapeDtypeStruct((SLAB_IN_W // LANES, bsz * t // SORT_TILE * LOCAL_SLAB_ROWS, LANES), F32),
            jax.ShapeDtypeStruct((bsz * t, ROUTE_W), F32),
            jax.ShapeDtypeStruct((bsz * t // SORT_TILE * 8, ROUTE_W), F32),
        ],
        compiler_params=_params(2),
        name="mix_out",
    )(x, o_f, o_b, sr, ya, mod, lnp, gn, w_out_b, wr, br)


def _moe_kernel(n_chunks, nused_ref, lo_ref, hi_ref, live_ref, src_ref, dst_ref,
                hxs_hbm, w1l_ref, w3l_ref, w2l_ref, w1h_ref, w3h_ref, w2h_ref, out_hbm, gbuf, obuf, gsem, ssem):
    tile_rows = CHUNKS_PER_TILE * SLAB_ROWS
    i = pl.program_id(0)
    n_used = nused_ref[0]
    slot = lax.bitwise_and(i, 1)

    def slab(chunk):
        return pl.ds(pl.multiple_of(chunk * SLAB_ROWS, SLAB_ROWS), SLAB_ROWS)

    def gather_copy(tile, buf_slot, j):
        chunk = src_ref[tile * CHUNKS_PER_TILE + j]
        return pltpu.make_async_copy(hxs_hbm.at[:, slab(chunk), :], gbuf.at[buf_slot, :, slab(j), :],
                                     gsem.at[buf_slot])

    def scatter_copy(tile, buf_slot, j):
        chunk = dst_ref[tile * CHUNKS_PER_TILE + j]
        return pltpu.make_async_copy(obuf.at[buf_slot, :, slab(j), :], out_hbm.at[:, slab(chunk), :],
                                     ssem.at[buf_slot])

    def start_gather(tile, buf_slot):
        for j in range(CHUNKS_PER_TILE):
            gather_copy(tile, buf_slot, j).start(priority=j % 2)

    def wait_gather(buf_slot):
        pltpu.make_async_copy(hxs_hbm.at[:, pl.ds(0, tile_rows), :], gbuf.at[buf_slot], gsem.at[buf_slot]).wait()

    def wait_scatter(buf_slot):
        pltpu.make_async_copy(obuf.at[buf_slot], out_hbm.at[:, pl.ds(0, tile_rows), :], ssem.at[buf_slot]).wait()

    @pl.when(i == 0)
    def _():
        start_gather(0, 0)
        obuf[...] = jnp.zeros(obuf.shape, F32)
        for s in range(2):
            fill = pltpu.make_async_copy(
                obuf.at[s], out_hbm.at[:, pl.ds((n_chunks + s * CHUNKS_PER_TILE) * SLAB_ROWS, tile_rows), :],
                ssem.at[s])
            fill.start()
            fill.wait()

    @pl.when(i + 1 < n_used)
    def _():
        start_gather(i + 1, 1 - slot)

    @pl.when(i < n_used)
    def _():
        wait_gather(slot)

        @pl.when(i >= 2)
        def _():
            wait_scatter(slot)

        @pl.when(live_ref[i] != 0)
        def _():
            def lane_block(c, half):
                return jnp.concatenate(
                    [gbuf[slot, c, pl.ds(2 * r + half, CHUNKS_PER_TILE, stride=SLAB_ROWS), :]
                     for r in range(CHUNK_ROWS)], axis=0)

            n_blk = HALF_W // LANES
            xb = jnp.concatenate([lane_block(c, 0) for c in range(n_blk)]
                                 + [lane_block(c, 1) for c in range(n_blk)], axis=1).astype(BF16)
            rec = lane_block(n_blk, 0)
            w_lo = rec[:, 0:1] + rec[:, 1:2] + rec[:, 2:3]
            w_hi = rec[:, 3:4] + rec[:, 4:5] + rec[:, 5:6]

            gate = [_dot(xb, w1_ref[...]) for w1_ref in (w1l_ref, w1h_ref)]
            up = [_dot(xb, w3_ref[...]) for w3_ref in (w3l_ref, w3h_ref)]
            act = [(_silu(g) * u).astype(BF16) for g, u in zip(gate, up)]
            e_lo, e_hi = [_dot(a, w2_ref[...]) for a, w2_ref in zip(act, (w2l_ref, w2h_ref))]
            y = w_lo * e_lo + w_hi * e_hi
            for r in range(CHUNK_ROWS):
                rows = slice(r * CHUNKS_PER_TILE, (r + 1) * CHUNKS_PER_TILE)
                for half in range(2):
                    for c in range(n_blk):
                        col = half * HALF_W + c * LANES
                        obuf[slot, c, pl.ds(2 * r + half, CHUNKS_PER_TILE, stride=SLAB_ROWS), :] = (
                            y[rows, col:col + LANES])

        @pl.when(live_ref[i] == 0)
        def _():
            obuf[slot] = jnp.zeros(obuf.shape[1:], F32)

        for j in range(CHUNKS_PER_TILE):
            scatter_copy(i, slot, j).start(priority=j % 2)

        @pl.when(i == n_used - 1)
        def _():
            wait_scatter(slot)

            @pl.when(i >= 1)
            def _():
                wait_scatter(1 - slot)


def _moe(hxs, src, dst, n_used, tile_lo, tile_hi, tile_live, w1_b, w3_b, w2_b):
    n_chunks = hxs.shape[1] // SLAB_ROWS
    tile_rows = CHUNKS_PER_TILE * SLAB_ROWS
    n_steps = src.shape[0] // CHUNKS_PER_TILE
    wspec = lambda which, shape: pl.BlockSpec(
        (None,) + shape, (lambda i, nu, lo, hi, lv, s, d: (lo[i], 0, 0)) if which == 0 else
        (lambda i, nu, lo, hi, lv, s, d: (hi[i], 0, 0)))
    grid_spec = pltpu.PrefetchScalarGridSpec(
        num_scalar_prefetch=6,
        grid=(n_steps,),
        in_specs=[
            pl.BlockSpec(memory_space=pl.ANY),
            wspec(0, (D_MODEL, D_EXPERT)), wspec(0, (D_MODEL, D_EXPERT)), wspec(0, (D_EXPERT, D_MODEL)),
            wspec(1, (D_MODEL, D_EXPERT)), wspec(1, (D_MODEL, D_EXPERT)), wspec(1, (D_EXPERT, D_MODEL)),
        ],
        out_specs=pl.BlockSpec(memory_space=pl.ANY),
        scratch_shapes=[
            pltpu.VMEM((2, SLAB_IN_W // LANES, tile_rows, LANES), F32),
            pltpu.VMEM((2, SLAB_OUT_W // LANES, tile_rows, LANES), F32),
            pltpu.SemaphoreType.DMA((2,)),
            pltpu.SemaphoreType.DMA((2,)),
        ],
    )
    return pl.pallas_call(
        functools.partial(_moe_kernel, n_chunks),
        grid_spec=grid_spec,
        out_shape=jax.ShapeDtypeStruct((SLAB_OUT_W // LANES, (n_chunks + 2 * CHUNKS_PER_TILE) * SLAB_ROWS, LANES), F32),
        compiler_params=_params(1),
        name="moe",
    )(n_used, tile_lo, tile_hi, tile_live, src, dst, hxs, w1_b, w3_b, w2_b, w1_b, w3_b, w2_b)


def _final_kernel(n_sort, x1_ref, moe_ref, pos_ref, mod_ref, lnp_ref, o_ref):
    moe = []
    for s in range(n_sort):
        slab_rows = slice(s * LOCAL_SLAB_ROWS, (s + 1) * LOCAL_SLAB_ROWS)
        moe_b = jnp.concatenate([moe_ref[c, slab_rows, :] for c in range(SLAB_OUT_W // LANES)], axis=1).astype(BF16)
        sort_lo, sort_hi = _slab_sort_matrices(pos_ref[s * SORT_TILE:(s + 1) * SORT_TILE, 0:1], SORT_TILE, 1)
        for r in range(0, SORT_TILE, MIX_SUB_TILE):
            rows = slice(r, r + MIX_SUB_TILE)
            moe.append((s * SORT_TILE + r,
                        jnp.concatenate([_dot(sort_lo[rows, :], moe_b), _dot(sort_hi[rows, :], moe_b)], axis=1)))
    for start, moe_s in moe:
        rows = slice(start, start + MIX_SUB_TILE)
        o_ref[rows, :] = _layer_norm(DEEPNORM_ALPHA * x1_ref[rows, :] + mod_ref[...] * moe_s,
                                     lnp_ref[0:1, :], lnp_ref[1:2, :])


def _final(x1, moe, pos, g2, lnp):
    bsz, t, _ = x1.shape
    n_sort = max(n for n in (1, 2, 4) if t % (n * SORT_TILE) == 0)
    tm = n_sort * SORT_TILE
    n_t = t // tm
    flat = lambda rows, w: pl.BlockSpec((rows, w), lambda b, i: (b * n_t + i, 0))
    return pl.pallas_call(
        functools.partial(_final_kernel, n_sort),
        grid=(bsz, n_t),
        in_specs=[
            pl.BlockSpec((None, tm, D_MODEL), lambda b, i: (b, i, 0)),
            pl.BlockSpec((SLAB_OUT_W // LANES, n_sort * LOCAL_SLAB_ROWS, LANES), lambda b, i: (0, b * n_t + i, 0)),
            flat(tm, ROUTE_W),
            pl.BlockSpec((None, 1, D_MODEL), lambda b, i: (b, 0, 0)),
            pl.BlockSpec(lnp.shape, lambda b, i: (0, 0)),
        ],
        out_specs=pl.BlockSpec((None, tm, D_MODEL), lambda b, i: (b, i, 0)),
        out_shape=jax.ShapeDtypeStruct((bsz, t, D_MODEL), F32),
        compiler_params=_params(2),
        name="final",
    )(x1, moe, pos, g2, lnp)


def _pair_tables():
    lo, hi = [], []
    for g in range(N_GROUPS):
        for a in range(EXPERTS_PER_GROUP):
            for b in range(a + 1, EXPERTS_PER_GROUP):
                lo.append(g * EXPERTS_PER_GROUP + a)
                hi.append(g * EXPERTS_PER_GROUP + b)
    return jnp.array(lo, jnp.int32), jnp.array(hi, jnp.int32)


def _moe_plan(chunks, n_sort_tiles):
    n_cls = N_CLASSES + 1
    hp = lax.Precision.HIGHEST
    m = chunks.reshape(n_sort_tiles, 8, ROUTE_W)[:, 0, :n_cls].astype(jnp.int32)
    a_end = jnp.cumsum(m, axis=0)
    a_start = a_end - m
    per_cls = a_end[-1]
    padded = (per_cls + CHUNKS_PER_TILE - 1) // CHUNKS_PER_TILE * CHUNKS_PER_TILE
    g_end = jnp.cumsum(padded)
    g_start = g_end - padded
    local_off = jnp.cumsum(m, axis=1) - m
    seg = jnp.arange(n_sort_tiles, dtype=jnp.int32)[:, None] * LOCAL_CHUNKS + local_off - a_start
    n_steps = -(-(n_sort_tiles * LOCAL_CHUNKS) // CHUNKS_PER_TILE) + n_cls
    p = jnp.arange(n_steps * CHUNKS_PER_TILE, dtype=jnp.int32)
    cls_p = jnp.minimum(jnp.sum((g_end[None, :] <= p[:, None]).astype(jnp.int32), axis=1), n_cls - 1)
    onehot = (cls_p[:, None] == jnp.arange(n_cls, dtype=jnp.int32)[None, :]).astype(F32)
    pick = lambda tab: jnp.dot(onehot, tab.astype(F32), precision=hp)
    u = p - pick(g_start[:, None])[:, 0].astype(jnp.int32)
    valid = u < pick(per_cls[:, None])[:, 0].astype(jnp.int32)
    a_end_p = pick(a_end.T).astype(jnp.int32)
    seg_p = pick(seg.T).astype(jnp.int32)
    tile_p = jnp.sum((a_end_p <= u[:, None]).astype(jnp.int32), axis=1)
    hit = jnp.arange(n_sort_tiles, dtype=jnp.int32)[None, :] == tile_p[:, None]
    src = jnp.sum(jnp.where(hit, seg_p, 0), axis=1) + u
    pad_dst = n_sort_tiles * LOCAL_CHUNKS + (p // CHUNKS_PER_TILE) % 2 * CHUNKS_PER_TILE + p % CHUNKS_PER_TILE
    dst = jnp.where(valid, src, pad_dst).astype(jnp.int32)
    src = jnp.where(valid, src, 0).astype(jnp.int32)
    n_used = g_end[-1:] // CHUNKS_PER_TILE
    step = jnp.arange(n_steps, dtype=jnp.int32)
    tile_cls = jnp.sum((g_end[None, :] // CHUNKS_PER_TILE <= step[:, None]).astype(jnp.int32), axis=1)
    live = ((tile_cls < N_CLASSES) & (step < n_used[0])).astype(jnp.int32)
    pair_lo, pair_hi = _pair_tables()
    pair_oh = (jnp.minimum(tile_cls, N_CLASSES - 1)[:, None] == jnp.arange(N_CLASSES)[None, :]).astype(jnp.int32)
    tile_lo = jnp.sum(pair_oh * pair_lo[None, :], axis=1).astype(jnp.int32)
    tile_hi = jnp.sum(pair_oh * pair_hi[None, :], axis=1).astype(jnp.int32)
    return src, dst, n_used.astype(jnp.int32), tile_lo, tile_hi, live


def kernel(x, c, ctx, c_ctx, ln_in_g, ln_in_b, w_ada, b_ada, w_in, conv_w, conv_b, gate_w2_fwd, gate_b_fwd,
           gate_w2_bwd, gate_b_bwd, gla_norm_g, w_out, ln1_g, ln1_b, router_group_w, router_group_b,
           router_expert_w, router_expert_b, expert_w1, expert_w3, expert_w2, ln2_g, ln2_b):
    bsz, t, _ = x.shape
    n_tok = bsz * t
    l = 0
    rows = -(-(bsz + 1) // 8) * 8
    cond = jnp.zeros((rows, D_MODEL), F32).at[:bsz].set(c).at[bsz].set(c_ctx)
    ada = _ada(cond, w_ada[l], b_ada[l][None, :])
    sh1, sc1, g1, sh2, sc2, g2 = [ada[:, i * D_MODEL:(i + 1) * D_MODEL] for i in range(6)]

    w_in_b = w_in[l].astype(BF16)
    lnp_in = jnp.stack([ln_in_g, ln_in_b])
    zero = jnp.zeros((GLA_GATE_RANK, GLA_KEY), F32)
    w2cat = jnp.concatenate([jnp.concatenate([gate_w2_fwd[l], zero], axis=1),
                             jnp.concatenate([zero, gate_w2_bwd[l]], axis=1)], axis=0).astype(BF16)
    gbias = jnp.concatenate([gate_b_fwd[l], gate_b_bwd[l]])[None, :]

    mod_ctx = jnp.broadcast_to(jnp.stack([1.0 + sc1[bsz], sh1[bsz]])[None], (bsz, 2, D_MODEL))
    k_c, v_c, g_c = _proj(ctx, mod_ctx, lnp_in, w_in_b, conv_w[l], conv_b[l][None, :], w2cat, gbias, False)
    zero_state = jnp.zeros((bsz, GLA_KEY, PAIR_VAL), F32)
    s_f, s_b = _gla(None, k_c, v_c, g_c, zero_state, zero_state)

    mod1 = jnp.stack([1.0 + sc1[:bsz], sh1[:bsz]], axis=1)
    ya, q, k, v, sr, g = _proj(x, mod1, lnp_in, w_in_b, conv_w[l], conv_b[l][None, :], w2cat, gbias, True)
    o_f, o_b, _, _ = _gla(q, k, v, g, s_f, s_b)

    mod2 = jnp.stack([g1[:bsz], 1.0 + sc2[:bsz], sh2[:bsz]], axis=1)
    lnp1 = jnp.stack([ln_in_g, ln_in_b, ln1_g[l], ln1_b[l]])
    wr = jnp.zeros((D_MODEL, ROUTE_W), F32)
    wr = wr.at[:, :N_GROUPS].set(router_group_w[l]).at[:, N_GROUPS:N_GROUPS + N_EXPERTS].set(router_expert_w[l])
    br = jnp.zeros((1, ROUTE_W), F32)
    br = br.at[0, :N_GROUPS].set(router_group_b[l]).at[0, N_GROUPS:N_GROUPS + N_EXPERTS].set(router_expert_b[l])
    x1, hxs, pos, chunks = _mix_out(x, o_f, o_b, sr, ya, mod2, lnp1, gla_norm_g[l][None, :],
                                    w_out[l].astype(BF16), wr.astype(BF16), br)

    src, dst, n_used, tile_lo, tile_hi, live = _moe_plan(chunks, n_tok // SORT_TILE)
    moe = _moe(hxs, src, dst, n_used, tile_lo, tile_hi, live,
               expert_w1[l].astype(BF16), expert_w3[l].astype(BF16), expert_w2[l].astype(BF16))

    return _final(x1, moe, pos, g2[:bsz][:, None, :], jnp.stack([ln2_g[l], ln2_b[l]]))
```

```python
import functools

import jax
import jax.numpy as jnp
from jax import lax
from jax.experimental import pallas as pl
from jax.experimental.pallas import tpu as pltpu

F32 = jnp.float32
BF16 = jnp.bfloat16

D_MODEL = 1024
GRID_W = 64
CONV_CH = 512
GLA_HEADS = 4
GLA_DK = 64
GLA_DV = 128
GLA_KEY = GLA_HEADS * GLA_DK
GLA_VAL = GLA_HEADS * GLA_DV
PAIR_KEY = 2 * GLA_DK
PAIR_VAL = 2 * GLA_DV
GLA_GATE_RANK = 16
GLA_TAU = 16.0
OFF_AB = 0
OFF_AC = OFF_AB + CONV_CH
OFF_AX = OFF_AC + CONV_CH
OFF_Q = OFF_AX + CONV_CH
OFF_K = OFF_Q + GLA_KEY
OFF_V = OFF_K + GLA_KEY
OFF_R = OFF_V + GLA_VAL
OFF_GF = OFF_R + GLA_VAL
D_PROJ = OFF_GF + 2 * GLA_GATE_RANK
N_GROUPS = 4
EXPERTS_PER_GROUP = 4
N_EXPERTS = N_GROUPS * EXPERTS_PER_GROUP
D_EXPERT = 512
PAIRS_PER_GROUP = 6
N_CLASSES = N_GROUPS * PAIRS_PER_GROUP
LN_EPS = 1e-5
RMS_EPS = 1e-6
DEPTH = 1
DEEPNORM_ALPHA = (2.0 * DEPTH) ** 0.25

LANES = 128
GLA_CHUNK = 64
GLA_SUB = 16
N_SUB = GLA_CHUNK // GLA_SUB
ROUTE_W = LANES
HALF_W = D_MODEL // 2
SLAB_IN_W = HALF_W + ROUTE_W
SLAB_OUT_W = HALF_W
SORT_TILE = 256
MIX_SUB_TILE = 128
MOE_TILE = 256
CHUNK_ROWS = 4
SLAB_ROWS = 2 * CHUNK_ROWS
LOCAL_CHUNKS = -(-(SORT_TILE + N_CLASSES * (CHUNK_ROWS - 1)) // CHUNK_ROWS)
LOCAL_SLAB_ROWS = LOCAL_CHUNKS * SLAB_ROWS
CHUNKS_PER_TILE = MOE_TILE // CHUNK_ROWS
VMEM_LIMIT = 56 * 1024 * 1024


def _params(n_axes, vmem=VMEM_LIMIT):
    return pltpu.CompilerParams(dimension_semantics=("arbitrary",) * n_axes, vmem_limit_bytes=vmem)


def _dot(a, b):
    return jnp.dot(a, b, preferred_element_type=F32)


def _div_pow2(x, d):
    assert d & (d - 1) == 0
    return lax.shift_right_logical(x, jnp.int32(d.bit_length() - 1))


def _mod_pow2(x, d):
    assert d & (d - 1) == 0
    return lax.bitwise_and(x, jnp.int32(d - 1))


def _split2(x):
    hi = x.astype(BF16)
    lo = (x - hi.astype(F32)).astype(BF16)
    return hi, lo


def _dot3(a, b):
    ah, al = _split2(a)
    bh, bl = _split2(b)
    return _dot(ah, bh) + _dot(ah, bl) + _dot(al, bh)


def _silu(x):
    return x * (0.5 * jnp.tanh(0.5 * x) + 0.5)


def _layer_norm(x, g, b):
    mu = jnp.mean(x, axis=-1, keepdims=True)
    xc = x - mu
    var = jnp.mean(xc * xc, axis=-1, keepdims=True)
    return xc * lax.rsqrt(var + LN_EPS) * g + b


def _ada_kernel(c_ref, w_ref, b_ref, o_ref):
    o_ref[...] = _dot3(_silu(c_ref[...]), w_ref[...]) + b_ref[...]


def _ada(cond, w_ada, b_ada):
    rows = cond.shape[0]
    n_out = w_ada.shape[1]
    tn = 1024
    return pl.pallas_call(
        _ada_kernel,
        grid=(n_out // tn,),
        in_specs=[
            pl.BlockSpec((rows, D_MODEL), lambda j: (0, 0)),
            pl.BlockSpec((D_MODEL, tn), lambda j: (0, j)),
            pl.BlockSpec((1, tn), lambda j: (0, j)),
        ],
        out_specs=pl.BlockSpec((rows, tn), lambda j: (0, j)),
        out_shape=jax.ShapeDtypeStruct((rows, n_out), F32),
        compiler_params=_params(1),
        name="ada",
    )(cond, w_ada, b_ada)


def _log_sigmoid(z):
    return jnp.minimum(z, 0.0) - jnp.log(1.0 + jnp.exp(-jnp.abs(z)))


def _proj_kernel(latent, tm, x_ref, mod_ref, lnp_ref, w_ref, cw_ref, cb_ref, w2_ref, gbias_ref, *out_refs):
    x = x_ref[...]
    xn = _layer_norm(x, lnp_ref[0:1, :], lnp_ref[1:2, :])
    h = xn * mod_ref[0:1, :] + mod_ref[1:2, :]
    hb = h.astype(BF16)
    if latent:
        ya_ref, q_ref, k_ref, v_ref, sr_ref, g_ref = out_refs
        p = _dot(hb, w_ref[:, OFF_AB:OFF_Q])
        a_b = p[:, 0:CONV_CH]
        u = p[:, CONV_CH:2 * CONV_CH] * p[:, 2 * CONV_CH:3 * CONV_CH]
        pos = _mod_pow2(lax.broadcasted_iota(jnp.int32, (tm, 1), 0), GRID_W)
        u_prev = jnp.where(pos == 0, 0.0, pltpu.roll(u, 1, 0))
        u_next = jnp.where(pos == GRID_W - 1, 0.0, pltpu.roll(u, tm - 1, 0))
        conv = u_prev * cw_ref[0:1, :] + u * cw_ref[1:2, :] + u_next * cw_ref[2:3, :] + cb_ref[...]
        ya_ref[...] = (a_b * conv).astype(BF16)
        qk = _dot(hb, w_ref[:, OFF_Q:OFF_V])
        q_ref[...] = (qk[:, 0:GLA_KEY] * (GLA_DK ** -0.5)).astype(BF16)
        k_ref[...] = qk[:, GLA_KEY:].astype(BF16)
        r = _dot(hb, w_ref[:, OFF_R:OFF_GF])
        sr_ref[...] = _silu(r).astype(BF16)
    else:
        k_ref, v_ref, g_ref = out_refs
        k_ref[...] = _dot(hb, w_ref[:, OFF_K:OFF_V]).astype(BF16)
    v_ref[...] = _dot(hb, w_ref[:, OFF_V:OFF_R]).astype(BF16)
    low = _dot(hb, w_ref[:, OFF_GF:D_PROJ])
    z = _dot(low.astype(BF16), w2_ref[...]) + gbias_ref[...]
    g_ref[...] = _log_sigmoid(z) * (1.0 / GLA_TAU)


def _proj(x, mod, lnp, w_in_b, conv_w, conv_b, w2cat, gbias, latent):
    bsz, t, _ = x.shape
    tm = min(1024, t)
    assert t % tm == 0 and tm % GRID_W == 0
    tok = lambda w: pl.BlockSpec((None, tm, w), lambda b, i: (b, i, 0))
    full = lambda a: pl.BlockSpec(a.shape, lambda b, i: (0,) * a.ndim)
    widths = ([(CONV_CH, BF16), (GLA_KEY, BF16)] if latent else []) + [(GLA_KEY, BF16), (GLA_VAL, BF16)]
    widths += ([(GLA_VAL, BF16)] if latent else []) + [(2 * GLA_KEY, F32)]
    return pl.pallas_call(
        functools.partial(_proj_kernel, latent, tm),
        grid=(bsz, t // tm),
        in_specs=[
            tok(D_MODEL),
            pl.BlockSpec((None, 2, D_MODEL), lambda b, i: (b, 0, 0)),
            full(lnp), full(w_in_b), full(conv_w), full(conv_b), full(w2cat), full(gbias),
        ],
        out_specs=[tok(w) for w, _ in widths],
        out_shape=[jax.ShapeDtypeStruct((bsz, t, w), dt) for w, dt in widths],
        compiler_params=_params(2),
        name="proj_latent" if latent else "proj_ctx",
    )(x, mod, lnp, w_in_b, conv_w, conv_b, w2cat, gbias)


def _tri(n, reverse, strict=False):
    i = lax.broadcasted_iota(jnp.int32, (n, n), 0)
    j = lax.broadcasted_iota(jnp.int32, (n, n), 1)
    if strict:
        m = (j > i) if reverse else (j < i)
    else:
        m = (j >= i) if reverse else (j <= i)
    return jnp.where(m, 1.0, 0.0).astype(BF16)


def _chunk_cumsum(g, reverse):
    tri = _tri(GLA_CHUNK, reverse)
    g_hi, g_lo = _split2(g)
    return _dot(tri, g_hi) + _dot(tri, g_lo)


def _as_column(row):
    return jnp.broadcast_to(row, (LANES, row.shape[1])).T


def _sub_anchors(gc, reverse):
    zero = jnp.zeros((1, GLA_KEY), F32)
    if reverse:
        return [gc[GLA_SUB * (a + 1):GLA_SUB * (a + 1) + 1] for a in range(N_SUB - 1)] + [zero]
    return [zero] + [gc[GLA_SUB * a - 1:GLA_SUB * a] for a in range(1, N_SUB)]


def _score_pairs(reverse):
    return [(a, b) for a in range(N_SUB) for b in range(N_SUB) if (b >= a if reverse else b <= a)]


def _intra_products(q, k, gc, reverse):
    r = _sub_anchors(gc, reverse)
    anchor = jnp.concatenate([jnp.broadcast_to(ra, (GLA_SUB, GLA_KEY)) for ra in r], axis=0)
    gcb = gc - anchor
    qt = q * jnp.exp(gcb)
    kt = k * jnp.exp(-gcb)
    rows = []
    for a, b in _score_pairs(reverse):
        qa = qt[GLA_SUB * a:GLA_SUB * (a + 1)]
        if a != b:
            qa = qa * jnp.exp(r[a] - r[b])
        rows.append(qa)
    qp = jnp.concatenate(rows, axis=0).astype(BF16)
    width = GLA_HEADS * GLA_CHUNK
    rr = lax.broadcasted_iota(jnp.int32, (width, GLA_KEY), 0)
    cc = lax.broadcasted_iota(jnp.int32, (width, GLA_KEY), 1)
    kbd = jnp.where(_div_pow2(rr, GLA_CHUNK) == _div_pow2(cc, GLA_DK),
                    jnp.concatenate([kt] * GLA_HEADS, axis=0), 0.0)
    return lax.dot_general(qp, kbd.astype(BF16), (((1,), (1,)), ((), ())), preferred_element_type=F32)


def _assemble_scores(res, reverse):
    pairs = _score_pairs(reverse)
    width = GLA_HEADS * GLA_CHUNK
    col = _mod_pow2(lax.broadcasted_iota(jnp.int32, (GLA_SUB, width), 1), GLA_CHUNK)
    col_blk = _div_pow2(col, GLA_SUB)
    col_pos = _mod_pow2(col, GLA_SUB)
    row_pos = lax.broadcasted_iota(jnp.int32, (GLA_SUB, width), 0)
    causal = (col_pos >= row_pos) if reverse else (col_pos <= row_pos)
    blocks = []
    for a in range(N_SUB):
        acc = jnp.zeros((GLA_SUB, width), F32)
        for idx, (pa, pb) in enumerate(pairs):
            if pa != a:
                continue
            keep = col_blk == pb
            if pa == pb:
                keep = keep & causal
            acc = acc + jnp.where(keep, res[GLA_SUB * idx:GLA_SUB * (idx + 1)], 0.0)
        blocks.append(acc)
    return jnp.concatenate(blocks, axis=0)


def _pair_mask(rows_per_head, cols_per_head, n_row_pairs=1):
    shape = (n_row_pairs * 2 * rows_per_head, 2 * cols_per_head)
    rr = _mod_pow2(lax.broadcasted_iota(jnp.int32, shape, 0), 2 * rows_per_head)
    cc = lax.broadcasted_iota(jnp.int32, shape, 1)
    return _div_pow2(rr, rows_per_head) == _div_pow2(cc, cols_per_head)


def _state_terms(q, k, v_b, gc, reverse):
    total = gc[0:1] if reverse else gc[GLA_CHUNK - 1:GLA_CHUNK]
    q_dec = None if q is None else (q * jnp.exp(gc)).astype(BF16)
    k_end = (k * jnp.exp(total - gc)).astype(BF16)
    tn = (((0,), (0,)), ((), ()))
    upd = [lax.dot_general(k_end[:, p * PAIR_KEY:(p + 1) * PAIR_KEY], v_b[:, p * PAIR_VAL:(p + 1) * PAIR_VAL], tn,
                           preferred_element_type=F32) for p in range(GLA_HEADS // 2)]
    upd = jnp.where(_pair_mask(GLA_DK, GLA_DV, GLA_HEADS // 2), jnp.concatenate(upd, axis=0), 0.0)
    decay = jnp.exp(_as_column(total))
    decay = jnp.concatenate([decay] * (PAIR_VAL // LANES), axis=1)
    return q_dec, decay, upd


def _advance_state(q_dec, decay, upd, state):
    o_inter = None
    if q_dec is not None:
        state_b = state.astype(BF16)
        o_inter = jnp.concatenate(
            [_dot(q_dec[:, p * PAIR_KEY:(p + 1) * PAIR_KEY], state_b[p * PAIR_KEY:(p + 1) * PAIR_KEY, :])
             for p in range(GLA_HEADS // 2)], axis=1)
    return o_inter, state * decay + upd


def _gla_kernel(tt, nt, with_out, *refs):
    if with_out:
        (qf_ref, kf_ref, vf_ref, gfw_ref, qb_ref, kb_ref, vb_ref, gbw_ref, s0f_ref, s0b_ref,
         of_ref, ob_ref, sf_ref, sb_ref) = refs
    else:
        kf_ref, vf_ref, gfw_ref, kb_ref, vb_ref, gbw_ref, s0f_ref, s0b_ref, sf_ref, sb_ref = refs
    j = pl.program_id(1)

    @pl.when(j == 0)
    def _():
        sf_ref[...] = s0f_ref[...]
        sb_ref[...] = s0b_ref[...]

    chunk_slices = [slice(c * GLA_CHUNK, (c + 1) * GLA_CHUNK) for c in range(tt // GLA_CHUNK)]
    fwd, bwd = [], []
    for sl in chunk_slices:
        g = gfw_ref[sl, :]
        fwd.append(dict(sl=sl, k=kf_ref[sl, :].astype(F32), v=vf_ref[sl, :], g_f=g[:, 0:GLA_KEY], g_b=g[:, GLA_KEY:],
                        q=qf_ref[sl, :].astype(F32) if with_out else None))
        bwd.append(dict(sl=sl, k=kb_ref[sl, :].astype(F32), v=vb_ref[sl, :], g_b=gbw_ref[sl, :][:, GLA_KEY:],
                        q=qb_ref[sl, :].astype(F32) if with_out else None))
    for d in fwd:
        d["gc_f"] = _chunk_cumsum(d["g_f"], False)
        if with_out:
            d["gc_b"] = _chunk_cumsum(d["g_b"], True)
    for d in bwd:
        d["gc_b"] = _chunk_cumsum(d["g_b"], True)
    if with_out:
        for d in fwd:
            d["res_f"] = _intra_products(d["q"], d["k"], d["gc_f"], False)
            d["res_b"] = _intra_products(d["q"], d["k"], d["gc_b"], True)
    for d in fwd:
        d["terms"] = _state_terms(d["q"], d["k"], d["v"], d["gc_f"], False)
    for d in bwd:
        d["terms"] = _state_terms(d["q"], d["k"], d["v"], d["gc_b"], True)
    if with_out:
        for d in fwd:
            scores = (_assemble_scores(d["res_f"], False) + _assemble_scores(d["res_b"], True)).astype(BF16)
            o_intra = []
            for p in range(GLA_HEADS // 2):
                v_p = d["v"][:, p * PAIR_VAL:(p + 1) * PAIR_VAL]
                vbd = jnp.where(_pair_mask(GLA_CHUNK, GLA_DV), jnp.concatenate([v_p, v_p], axis=0),
                                jnp.zeros((), BF16))
                o_intra.append(_dot(scores[:, p * 2 * GLA_CHUNK:(p + 1) * 2 * GLA_CHUNK], vbd))
            d["o_intra"] = jnp.concatenate(o_intra, axis=1)

    state = sf_ref[...]
    for d in fwd:
        o_inter, state = _advance_state(*d["terms"], state)
        if with_out:
            of_ref[d["sl"], :] = (d["o_intra"] + o_inter).astype(BF16)
    sf_ref[...] = state

    state = sb_ref[...]
    for d in reversed(bwd):
        o_inter, state = _advance_state(*d["terms"], state)
        if with_out:
            ob_ref[d["sl"], :] = o_inter.astype(BF16)
    sb_ref[...] = state


def _gla(q, k, v, g, s0f, s0b):
    with_out = q is not None
    bsz, t, _ = k.shape
    tt = min(1024, t)
    assert t % tt == 0 and tt % GLA_CHUNK == 0
    nt = t // tt
    fwd = lambda w: pl.BlockSpec((None, tt, w), lambda b, j: (b, j, 0))
    bwd = lambda w: pl.BlockSpec((None, tt, w), lambda b, j: (b, nt - 1 - j, 0))
    st = pl.BlockSpec((None, GLA_KEY, PAIR_VAL), lambda b, j: (b, 0, 0))
    st_shape = jax.ShapeDtypeStruct((bsz, GLA_KEY, PAIR_VAL), F32)
    if with_out:
        ins = [q, k, v, g, q, k, v, g, s0f, s0b]
        in_specs = [fwd(GLA_KEY), fwd(GLA_KEY), fwd(GLA_VAL), fwd(2 * GLA_KEY),
                    bwd(GLA_KEY), bwd(GLA_KEY), bwd(GLA_VAL), bwd(2 * GLA_KEY), st, st]
        out_specs = [fwd(GLA_VAL), bwd(GLA_VAL), st, st]
        o_shape = jax.ShapeDtypeStruct((bsz, t, GLA_VAL), BF16)
        out_shape = [o_shape, o_shape, st_shape, st_shape]
    else:
        ins = [k, v, g, k, v, g, s0f, s0b]
        in_specs = [fwd(GLA_KEY), fwd(GLA_VAL), fwd(2 * GLA_KEY),
                    bwd(GLA_KEY), bwd(GLA_VAL), bwd(2 * GLA_KEY), st, st]
        out_specs = [st, st]
        out_shape = [st_shape, st_shape]
    return pl.pallas_call(
        functools.partial(_gla_kernel, tt, nt, with_out),
        grid=(bsz, nt),
        in_specs=in_specs,
        out_specs=out_specs,
        out_shape=out_shape,
        compiler_params=_params(2),
        name="gla_latent" if with_out else "gla_ctx",
    )(*ins)


def _exact_bf16_parts(x):
    hi = x.astype(BF16).astype(F32)
    r = x - hi
    mid = r.astype(BF16).astype(F32)
    lo = (r - mid).astype(BF16).astype(F32)
    return hi, mid, lo


def _first_index(values, best):
    idx = jnp.full_like(best, float(len(values) - 1))
    for i in reversed(range(len(values) - 1)):
        idx = jnp.where(values[i] >= best, float(i), idx)
    return idx


def _pick(rows, idx):
    out = rows[-1]
    for i in reversed(range(len(rows) - 1)):
        out = jnp.where(idx == float(i), rows[i], out)
    return out


def _route(logit_t, tm):
    row = lambda r: logit_t[r:r + 1, :]
    groups = [row(i) for i in range(N_GROUPS)]
    top = functools.reduce(jnp.maximum, groups)
    eg = [jnp.exp(x - top) for x in groups]
    total = functools.reduce(lambda a, b: a + b, eg)
    pg = [e / total for e in eg]
    p_g = functools.reduce(jnp.maximum, pg)
    g_idx = _first_index(pg, p_g)
    sel = [_pick([row(N_GROUPS + EXPERTS_PER_GROUP * g + j) for g in range(N_GROUPS)], g_idx)
           for j in range(EXPERTS_PER_GROUP)]
    top = functools.reduce(jnp.maximum, sel)
    ee = [jnp.exp(x - top) for x in sel]
    total = functools.reduce(lambda a, b: a + b, ee)
    pe = [e / total for e in ee]
    p1 = functools.reduce(jnp.maximum, pe)
    l1 = _first_index(pe, p1)
    pe2 = [jnp.where(l1 == float(j), -1.0, pe[j]) for j in range(EXPERTS_PER_GROUP)]
    p2 = functools.reduce(jnp.maximum, pe2)
    l2 = _first_index(pe2, p2)
    den = p1 + p2
    w1 = p1 / den * p_g
    w2 = p2 / den * p_g
    lo = jnp.minimum(l1, l2)
    hi = jnp.maximum(l1, l2)
    pair = lo * (7.0 - lo) * 0.5 + (hi - lo - 1.0)
    cls = g_idx * PAIRS_PER_GROUP + pair
    w_lo = jnp.where(l1 < l2, w1, w2)
    w_hi = jnp.where(l1 < l2, w2, w1)
    cls_id = lax.broadcasted_iota(jnp.int32, (ROUTE_W, tm), 0).astype(F32)
    onehot = jnp.where(cls_id == cls, 1.0, 0.0)
    before = _dot(onehot.astype(BF16), _tri(tm, True, strict=True))
    count = jnp.sum(onehot, axis=1, keepdims=True)
    chunks = jnp.floor((count + (CHUNK_ROWS - 1.0)) * (1.0 / CHUNK_ROWS))
    chunks = jnp.where(cls_id[:, 0:1] == float(N_CLASSES),
                       LOCAL_CHUNKS - jnp.sum(chunks, axis=0, keepdims=True), chunks)
    first_chunk = _dot(_tri(ROUTE_W, False, strict=True),
                       jnp.broadcast_to(chunks, (ROUTE_W, LANES)).astype(BF16))[:, 0:1]
    pos_row = jnp.sum(onehot * (CHUNK_ROWS * first_chunk + before), axis=0, keepdims=True)
    return pos_row, w_lo, w_hi, chunks


def _slab_targets(slab_axis):
    shape = (LOCAL_SLAB_ROWS, 1) if slab_axis == 0 else (1, LOCAL_SLAB_ROWS)
    slab_row = lax.broadcasted_iota(jnp.int32, shape, slab_axis)
    sub = _mod_pow2(slab_row, SLAB_ROWS)
    token_row = (CHUNK_ROWS * _div_pow2(slab_row, SLAB_ROWS) + _div_pow2(sub, 2)).astype(F32)
    half = _mod_pow2(sub, 2)
    return [jnp.where(half == h, token_row, -1.0) for h in range(2)]


def _slab_sort_matrices(pos, targets):
    return [jnp.where(t == pos, 1.0, 0.0).astype(BF16) for t in targets]


def _mix_out_kernel(tm, x_ref, of_ref, ob_ref, sr_ref, ya_ref, mod_ref, lnp_ref, gn_ref, wo_ref, wr_ref, br_ref,
                    x1_ref, hxs_ref, pos_ref, chunks_ref):
    subs = [slice(s, s + MIX_SUB_TILE) for s in range(0, tm, MIX_SUB_TILE)]
    yb = []
    for rows in subs:
        o = of_ref[rows, :].astype(F32) + ob_ref[rows, :].astype(F32)
        sr = sr_ref[rows, :].astype(F32)
        heads = []
        for h in range(GLA_HEADS):
            sl = slice(h * GLA_DV, (h + 1) * GLA_DV)
            oh = o[:, sl]
            ms = jnp.mean(oh * oh, axis=-1, keepdims=True)
            heads.append((oh * lax.rsqrt(ms + RMS_EPS) * gn_ref[...] * sr[:, sl]).astype(BF16))
        yb.append(jnp.concatenate([ya_ref[rows, :]] + heads, axis=1))
    xn = [_layer_norm(x_ref[rows, :], lnp_ref[0:1, :], lnp_ref[1:2, :]) for rows in subs]
    y = [_dot(y_in, wo_ref[...]) for y_in in yb]
    h2_b = []
    for rows, xn_s, y_s in zip(subs, xn, y):
        x1 = _layer_norm(DEEPNORM_ALPHA * xn_s + mod_ref[0:1, :] * y_s, lnp_ref[2:3, :], lnp_ref[3:4, :])
        x1_ref[rows, :] = x1
        h2_b.append((x1 * mod_ref[1:2, :] + mod_ref[2:3, :]).astype(BF16))
    logit_t = [(_dot(h2_s, wr_ref[...]) + br_ref[...]).T for h2_s in h2_b]
    per_sort = SORT_TILE // MIX_SUB_TILE
    routes = []
    for s in range(tm // SORT_TILE):
        routes.append(_route(jnp.concatenate(logit_t[s * per_sort:(s + 1) * per_sort], axis=1), SORT_TILE))
    rec_id = lax.broadcasted_iota(jnp.int32, (ROUTE_W, SORT_TILE), 0)
    targets = _slab_targets(0)
    for s, (pos_row, w_lo, w_hi, chunks) in enumerate(routes):
        rec_t = jnp.zeros((ROUTE_W, SORT_TILE), F32)
        for i, part in enumerate(_exact_bf16_parts(w_lo) + _exact_bf16_parts(w_hi)):
            rec_t = jnp.where(rec_id == i, part, rec_t)
        rec_b = rec_t.T.astype(BF16)
        h2_s = jnp.concatenate(h2_b[s * per_sort:(s + 1) * per_sort], axis=0)
        sort_lo, sort_hi = _slab_sort_matrices(pos_row, targets)
        pay_lo = jnp.concatenate([h2_s[:, 0:HALF_W], rec_b], axis=1)
        pay_hi = jnp.concatenate([h2_s[:, HALF_W:], jnp.zeros((SORT_TILE, ROUTE_W), BF16)], axis=1)
        slabs = _dot(sort_lo, pay_lo) + _dot(sort_hi, pay_hi)
        for c in range(SLAB_IN_W // LANES):
            hxs_ref[c, s * LOCAL_SLAB_ROWS:(s + 1) * LOCAL_SLAB_ROWS, :] = slabs[:, c * LANES:(c + 1) * LANES]
        pos_ref[s * SORT_TILE:(s + 1) * SORT_TILE, :] = jnp.broadcast_to(pos_row, (ROUTE_W, SORT_TILE)).T
        chunks_ref[8 * s:8 * (s + 1), :] = jnp.broadcast_to(chunks, (ROUTE_W, LANES)).T[0:8, :]


def _mix_out(x, o_f, o_b, sr, ya, mod, lnp, gn, w_out_b, wr, br):
    bsz, t, _ = x.shape
    assert t % SORT_TILE == 0
    n_sort = max(n for n in (1, 2, 4) if t % (n * SORT_TILE) == 0)
    tm = n_sort * SORT_TILE
    n_t = t // tm
    tok = lambda w: pl.BlockSpec((None, tm, w), lambda b, i: (b, i, 0))
    full = lambda a: pl.BlockSpec(a.shape, lambda b, i: (0,) * a.ndim)
    flat = lambda rows, w: pl.BlockSpec((rows, w), lambda b, i: (b * n_t + i, 0))
    return pl.pallas_call(
        functools.partial(_mix_out_kernel, tm),
        grid=(bsz, n_t),
        in_specs=[
            tok(D_MODEL), tok(GLA_VAL), tok(GLA_VAL), tok(GLA_VAL), tok(CONV_CH),
            pl.BlockSpec((None, 3, D_MODEL), lambda b, i: (b, 0, 0)),
            full(lnp), full(gn), full(w_out_b), full(wr), full(br),
        ],
        out_specs=[tok(D_MODEL),
                   pl.BlockSpec((SLAB_IN_W // LANES, n_sort * LOCAL_SLAB_ROWS, LANES),
                                lambda b, i: (0, b * n_t + i, 0)),
                   flat(tm, ROUTE_W), flat(n_sort * 8, ROUTE_W)],
        out_shape=[
            jax.ShapeDtypeStruct((bsz, t, D_MODEL), F32),
            jax.ShapeDtypeStruct((SLAB_IN_W // LANES, bsz * t // SORT_TILE * LOCAL_SLAB_ROWS, LANES), F32),
            jax.ShapeDtypeStruct((bsz * t, ROUTE_W), F32),
            jax.ShapeDtypeStruct((bsz * t // SORT_TILE * 8, ROUTE_W), F32),
        ],
        compiler_params=_params(2),
        name="mix_out",
    )(x, o_f, o_b, sr, ya, mod, lnp, gn, w_out_b, wr, br)


def _moe_kernel(n_chunks, nused_ref, lo_ref, hi_ref, live_ref, src_ref, dst_ref,
                hxs_hbm, w1l_ref, w3l_ref, w2l_ref, w1h_ref, w3h_ref, w2h_ref, out_hbm, gbuf, obuf, gsem, ssem):
    tile_rows = CHUNKS_PER_TILE * SLAB_ROWS
    i = pl.program_id(0)
    n_used = nused_ref[0]
    slot = lax.bitwise_and(i, 1)

    def slab(chunk):
        return pl.ds(pl.multiple_of(chunk * SLAB_ROWS, SLAB_ROWS), SLAB_ROWS)

    def gather_copy(tile, buf_slot, j):
        chunk = src_ref[tile * CHUNKS_PER_TILE + j]
        return pltpu.make_async_copy(hxs_hbm.at[:, slab(chunk), :], gbuf.at[buf_slot, :, slab(j), :],
                                     gsem.at[buf_slot])

    def scatter_copy(tile, buf_slot, j):
        chunk = dst_ref[tile * CHUNKS_PER_TILE + j]
        return pltpu.make_async_copy(obuf.at[buf_slot, :, slab(j), :], out_hbm.at[:, slab(chunk), :],
                                     ssem.at[buf_slot])

    def start_gather(tile, buf_slot):
        for j in range(CHUNKS_PER_TILE):
            gather_copy(tile, buf_slot, j).start(priority=j % 2)

    def wait_gather(buf_slot):
        pltpu.make_async_copy(hxs_hbm.at[:, pl.ds(0, tile_rows), :], gbuf.at[buf_slot], gsem.at[buf_slot]).wait()

    def wait_scatter(buf_slot):
        pltpu.make_async_copy(obuf.at[buf_slot], out_hbm.at[:, pl.ds(0, tile_rows), :], ssem.at[buf_slot]).wait()

    @pl.when(i == 0)
    def _():
        start_gather(0, 0)
        obuf[...] = jnp.zeros(obuf.shape, F32)
        for s in range(2):
            fill = pltpu.make_async_copy(
                obuf.at[s], out_hbm.at[:, pl.ds((n_chunks + s * CHUNKS_PER_TILE) * SLAB_ROWS, tile_rows), :],
                ssem.at[s])
            fill.start()
            fill.wait()

    @pl.when(i + 1 < n_used)
    def _():
        start_gather(i + 1, 1 - slot)

    @pl.when(i < n_used)
    def _():
        wait_gather(slot)

        @pl.when(i >= 2)
        def _():
            wait_scatter(slot)

        @pl.when(live_ref[i] != 0)
        def _():
            def lane_block(c, half):
                return jnp.concatenate(
                    [gbuf[slot, c, pl.ds(2 * r + half, CHUNKS_PER_TILE, stride=SLAB_ROWS), :]
                     for r in range(CHUNK_ROWS)], axis=0)

            n_blk = HALF_W // LANES
            xb = jnp.concatenate([lane_block(c, 0) for c in range(n_blk)]
                                 + [lane_block(c, 1) for c in range(n_blk)], axis=1).astype(BF16)
            rec = lane_block(n_blk, 0)
            w_lo = rec[:, 0:1] + rec[:, 1:2] + rec[:, 2:3]
            w_hi = rec[:, 3:4] + rec[:, 4:5] + rec[:, 5:6]

            gate = [_dot(xb, w1_ref[...]) for w1_ref in (w1l_ref, w1h_ref)]
            up = [_dot(xb, w3_ref[...]) for w3_ref in (w3l_ref, w3h_ref)]
            act = [(_silu(g) * u).astype(BF16) for g, u in zip(gate, up)]
            e_lo, e_hi = [_dot(a, w2_ref[...]) for a, w2_ref in zip(act, (w2l_ref, w2h_ref))]
            y = w_lo * e_lo + w_hi * e_hi
            for r in range(CHUNK_ROWS):
                rows = slice(r * CHUNKS_PER_TILE, (r + 1) * CHUNKS_PER_TILE)
                for half in range(2):
                    for c in range(n_blk):
                        col = half * HALF_W + c * LANES
                        obuf[slot, c, pl.ds(2 * r + half, CHUNKS_PER_TILE, stride=SLAB_ROWS), :] = (
                            y[rows, col:col + LANES])

        @pl.when(live_ref[i] == 0)
        def _():
            obuf[slot] = jnp.zeros(obuf.shape[1:], F32)

        for j in range(CHUNKS_PER_TILE):
            scatter_copy(i, slot, j).start(priority=j % 2)

        @pl.when(i == n_used - 1)
        def _():
            wait_scatter(slot)

            @pl.when(i >= 1)
            def _():
                wait_scatter(1 - slot)


def _moe(hxs, src, dst, n_used, tile_lo, tile_hi, tile_live, w1_b, w3_b, w2_b):
    n_chunks = hxs.shape[1] // SLAB_ROWS
    tile_rows = CHUNKS_PER_TILE * SLAB_ROWS
    n_steps = src.shape[0] // CHUNKS_PER_TILE
    wspec = lambda which, shape: pl.BlockSpec(
        (None,) + shape, (lambda i, nu, lo, hi, lv, s, d: (lo[i], 0, 0)) if which == 0 else
        (lambda i, nu, lo, hi, lv, s, d: (hi[i], 0, 0)))
    grid_spec = pltpu.PrefetchScalarGridSpec(
        num_scalar_prefetch=6,
        grid=(n_steps,),
        in_specs=[
            pl.BlockSpec(memory_space=pl.ANY),
            wspec(0, (D_MODEL, D_EXPERT)), wspec(0, (D_MODEL, D_EXPERT)), wspec(0, (D_EXPERT, D_MODEL)),
            wspec(1, (D_MODEL, D_EXPERT)), wspec(1, (D_MODEL, D_EXPERT)), wspec(1, (D_EXPERT, D_MODEL)),
        ],
        out_specs=pl.BlockSpec(memory_space=pl.ANY),
        scratch_shapes=[
            pltpu.VMEM((2, SLAB_IN_W // LANES, tile_rows, LANES), F32),
            pltpu.VMEM((2, SLAB_OUT_W // LANES, tile_rows, LANES), F32),
            pltpu.SemaphoreType.DMA((2,)),
            pltpu.SemaphoreType.DMA((2,)),
        ],
    )
    return pl.pallas_call(
        functools.partial(_moe_kernel, n_chunks),
        grid_spec=grid_spec,
        out_shape=jax.ShapeDtypeStruct((SLAB_OUT_W // LANES, (n_chunks + 2 * CHUNKS_PER_TILE) * SLAB_ROWS, LANES), F32),
        compiler_params=_params(1),
        name="moe",
    )(n_used, tile_lo, tile_hi, tile_live, src, dst, hxs, w1_b, w3_b, w2_b, w1_b, w3_b, w2_b)


def _final_kernel(n_sort, x1_ref, moe_ref, pos_ref, mod_ref, lnp_ref, o_ref):
    moe = []
    targets = _slab_targets(1)
    for s in range(n_sort):
        slab_rows = slice(s * LOCAL_SLAB_ROWS, (s + 1) * LOCAL_SLAB_ROWS)
        moe_b = jnp.concatenate([moe_ref[c, slab_rows, :] for c in range(SLAB_OUT_W // LANES)], axis=1).astype(BF16)
        sort_lo, sort_hi = _slab_sort_matrices(pos_ref[s * SORT_TILE:(s + 1) * SORT_TILE, 0:1], targets)
        for r in range(0, SORT_TILE, MIX_SUB_TILE):
            rows = slice(r, r + MIX_SUB_TILE)
            moe.append((s * SORT_TILE + r,
                        jnp.concatenate([_dot(sort_lo[rows, :], moe_b), _dot(sort_hi[rows, :], moe_b)], axis=1)))
    for start, moe_s in moe:
        rows = slice(start, start + MIX_SUB_TILE)
        o_ref[rows, :] = _layer_norm(DEEPNORM_ALPHA * x1_ref[rows, :] + mod_ref[...] * moe_s,
                                     lnp_ref[0:1, :], lnp_ref[1:2, :])


def _final(x1, moe, pos, g2, lnp):
    bsz, t, _ = x1.shape
    n_sort = max(n for n in (1, 2, 4) if t % (n * SORT_TILE) == 0)
    tm = n_sort * SORT_TILE
    n_t = t // tm
    flat = lambda rows, w: pl.BlockSpec((rows, w), lambda b, i: (b * n_t + i, 0))
    return pl.pallas_call(
        functools.partial(_final_kernel, n_sort),
        grid=(bsz, n_t),
        in_specs=[
            pl.BlockSpec((None, tm, D_MODEL), lambda b, i: (b, i, 0)),
            pl.BlockSpec((SLAB_OUT_W // LANES, n_sort * LOCAL_SLAB_ROWS, LANES), lambda b, i: (0, b * n_t + i, 0)),
            flat(tm, ROUTE_W),
            pl.BlockSpec((None, 1, D_MODEL), lambda b, i: (b, 0, 0)),
            pl.BlockSpec(lnp.shape, lambda b, i: (0, 0)),
        ],
        out_specs=pl.BlockSpec((None, tm, D_MODEL), lambda b, i: (b, i, 0)),
        out_shape=jax.ShapeDtypeStruct((bsz, t, D_MODEL), F32),
        compiler_params=_params(2),
        name="final",
    )(x1, moe, pos, g2, lnp)


def _pair_tables():
    lo, hi = [], []
    for g in range(N_GROUPS):
        for a in range(EXPERTS_PER_GROUP):
            for b in range(a + 1, EXPERTS_PER_GROUP):
                lo.append(g * EXPERTS_PER_GROUP + a)
                hi.append(g * EXPERTS_PER_GROUP + b)
    return jnp.array(lo, jnp.int32), jnp.array(hi, jnp.int32)


def _moe_plan(chunks, n_sort_tiles):
    n_cls = N_CLASSES + 1
    hp = lax.Precision.HIGHEST
    m = chunks.reshape(n_sort_tiles, 8, ROUTE_W)[:, 0, :n_cls].astype(jnp.int32)
    a_end = jnp.cumsum(m, axis=0)
    a_start = a_end - m
    per_cls = a_end[-1]
    padded = (per_cls + CHUNKS_PER_TILE - 1) // CHUNKS_PER_TILE * CHUNKS_PER_TILE
    g_end = jnp.cumsum(padded)
    g_start = g_end - padded
    local_off = jnp.cumsum(m, axis=1) - m
    seg = jnp.arange(n_sort_tiles, dtype=jnp.int32)[:, None] * LOCAL_CHUNKS + local_off - a_start
    n_steps = -(-(n_sort_tiles * LOCAL_CHUNKS) // CHUNKS_PER_TILE) + n_cls
    p = jnp.arange(n_steps * CHUNKS_PER_TILE, dtype=jnp.int32)
    cls_p = jnp.minimum(jnp.sum((g_end[None, :] <= p[:, None]).astype(jnp.int32), axis=1), n_cls - 1)
    onehot = (cls_p[:, None] == jnp.arange(n_cls, dtype=jnp.int32)[None, :]).astype(F32)
    pick = lambda tab: jnp.dot(onehot, tab.astype(F32), precision=hp)
    u = p - pick(g_start[:, None])[:, 0].astype(jnp.int32)
    valid = u < pick(per_cls[:, None])[:, 0].astype(jnp.int32)
    a_end_p = pick(a_end.T).astype(jnp.int32)
    seg_p = pick(seg.T).astype(jnp.int32)
    tile_p = jnp.sum((a_end_p <= u[:, None]).astype(jnp.int32), axis=1)
    hit = jnp.arange(n_sort_tiles, dtype=jnp.int32)[None, :] == tile_p[:, None]
    src = jnp.sum(jnp.where(hit, seg_p, 0), axis=1) + u
    pad_dst = n_sort_tiles * LOCAL_CHUNKS + (p // CHUNKS_PER_TILE) % 2 * CHUNKS_PER_TILE + p % CHUNKS_PER_TILE
    dst = jnp.where(valid, src, pad_dst).astype(jnp.int32)
    src = jnp.where(valid, src, 0).astype(jnp.int32)
    n_used = g_end[-1:] // CHUNKS_PER_TILE
    step = jnp.arange(n_steps, dtype=jnp.int32)
    tile_cls = jnp.sum((g_end[None, :] // CHUNKS_PER_TILE <= step[:, None]).astype(jnp.int32), axis=1)
    live = ((tile_cls < N_CLASSES) & (step < n_used[0])).astype(jnp.int32)
    pair_lo, pair_hi = _pair_tables()
    pair_oh = (jnp.minimum(tile_cls, N_CLASSES - 1)[:, None] == jnp.arange(N_CLASSES)[None, :]).astype(jnp.int32)
    tile_lo = jnp.sum(pair_oh * pair_lo[None, :], axis=1).astype(jnp.int32)
    tile_hi = jnp.sum(pair_oh * pair_hi[None, :], axis=1).astype(jnp.int32)
    return src, dst, n_used.astype(jnp.int32), tile_lo, tile_hi, live


def kernel(x, c, ctx, c_ctx, ln_in_g, ln_in_b, w_ada, b_ada, w_in, conv_w, conv_b, gate_w2_fwd, gate_b_fwd,
           gate_w2_bwd, gate_b_bwd, gla_norm_g, w_out, ln1_g, ln1_b, router_group_w, router_group_b,
           router_expert_w, router_expert_b, expert_w1, expert_w3, expert_w2, ln2_g, ln2_b):
    bsz, t, _ = x.shape
    n_tok = bsz * t
    l = 0
    rows = -(-(bsz + 1) // 8) * 8
    cond = jnp.zeros((rows, D_MODEL), F32).at[:bsz].set(c).at[bsz].set(c_ctx)
    ada = _ada(cond, w_ada[l], b_ada[l][None, :])
    sh1, sc1, g1, sh2, sc2, g2 = [ada[:, i * D_MODEL:(i + 1) * D_MODEL] for i in range(6)]

    w_in_b = w_in[l].astype(BF16)
    lnp_in = jnp.stack([ln_in_g, ln_in_b])
    zero = jnp.zeros((GLA_GATE_RANK, GLA_KEY), F32)
    w2cat = jnp.concatenate([jnp.concatenate([gate_w2_fwd[l], zero], axis=1),
                             jnp.concatenate([zero, gate_w2_bwd[l]], axis=1)], axis=0).astype(BF16)
    gbias = jnp.concatenate([gate_b_fwd[l], gate_b_bwd[l]])[None, :]

    mod_ctx = jnp.broadcast_to(jnp.stack([1.0 + sc1[bsz], sh1[bsz]])[None], (bsz, 2, D_MODEL))
    k_c, v_c, g_c = _proj(ctx, mod_ctx, lnp_in, w_in_b, conv_w[l], conv_b[l][None, :], w2cat, gbias, False)
    zero_state = jnp.zeros((bsz, GLA_KEY, PAIR_VAL), F32)
    s_f, s_b = _gla(None, k_c, v_c, g_c, zero_state, zero_state)

    mod1 = jnp.stack([1.0 + sc1[:bsz], sh1[:bsz]], axis=1)
    ya, q, k, v, sr, g = _proj(x, mod1, lnp_in, w_in_b, conv_w[l], conv_b[l][None, :], w2cat, gbias, True)
    o_f, o_b, _, _ = _gla(q, k, v, g, s_f, s_b)

    mod2 = jnp.stack([g1[:bsz], 1.0 + sc2[:bsz], sh2[:bsz]], axis=1)
    lnp1 = jnp.stack([ln_in_g, ln_in_b, ln1_g[l], ln1_b[l]])
    wr = jnp.zeros((D_MODEL, ROUTE_W), F32)
    wr = wr.at[:, :N_GROUPS].set(router_group_w[l]).at[:, N_GROUPS:N_GROUPS + N_EXPERTS].set(router_expert_w[l])
    br = jnp.zeros((1, ROUTE_W), F32)
    br = br.at[0, :N_GROUPS].set(router_group_b[l]).at[0, N_GROUPS:N_GROUPS + N_EXPERTS].set(router_expert_b[l])
    x1, hxs, pos, chunks = _mix_out(x, o_f, o_b, sr, ya, mod2, lnp1, gla_norm_g[l][None, :],
                                    w_out[l].astype(BF16), wr.astype(BF16), br)

    src, dst, n_used, tile_lo, tile_hi, live = _moe_plan(chunks, n_tok // SORT_TILE)
    moe = _moe(hxs, src, dst, n_used, tile_lo, tile_hi, live,
               expert_w1[l].astype(BF16), expert_w3[l].astype(BF16), expert_w2[l].astype(BF16))

    return _final(x1, moe, pos, g2[:bsz][:, None, :], jnp.stack([ln2_g[l], ln2_b[l]]))
```

```python
import functools

import jax
import jax.numpy as jnp
from jax import lax
from jax.experimental import pallas as pl
from jax.experimental.pallas import tpu as pltpu

F32 = jnp.float32
BF16 = jnp.bfloat16

D_MODEL = 1024
GRID_W = 64
CONV_CH = 512
GLA_HEADS = 4
GLA_DK = 64
GLA_DV = 128
GLA_KEY = GLA_HEADS * GLA_DK
GLA_VAL = GLA_HEADS * GLA_DV
PAIR_KEY = 2 * GLA_DK
PAIR_VAL = 2 * GLA_DV
GLA_GATE_RANK = 16
GLA_TAU = 16.0
OFF_AB = 0
OFF_AC = OFF_AB + CONV_CH
OFF_AX = OFF_AC + CONV_CH
OFF_Q = OFF_AX + CONV_CH
OFF_K = OFF_Q + GLA_KEY
OFF_V = OFF_K + GLA_KEY
OFF_R = OFF_V + GLA_VAL
OFF_GF = OFF_R + GLA_VAL
D_PROJ = OFF_GF + 2 * GLA_GATE_RANK
N_GROUPS = 4
EXPERTS_PER_GROUP = 4
N_EXPERTS = N_GROUPS * EXPERTS_PER_GROUP
D_EXPERT = 512
PAIRS_PER_GROUP = 6
N_CLASSES = N_GROUPS * PAIRS_PER_GROUP
LN_EPS = 1e-5
RMS_EPS = 1e-6
DEPTH = 1
DEEPNORM_ALPHA = (2.0 * DEPTH) ** 0.25

LANES = 128
GLA_CHUNK = 64
GLA_SUB = 16
N_SUB = GLA_CHUNK // GLA_SUB
ROUTE_W = LANES
HALF_W = D_MODEL // 2
SLAB_IN_W = HALF_W + ROUTE_W
SLAB_OUT_W = HALF_W
SORT_TILE = 256
MIX_SUB_TILE = 128
MOE_TILE = 256
CHUNK_ROWS = 4
SLAB_ROWS = 2 * CHUNK_ROWS
LOCAL_CHUNKS = -(-(SORT_TILE + N_CLASSES * (CHUNK_ROWS - 1)) // CHUNK_ROWS)
LOCAL_SLAB_ROWS = LOCAL_CHUNKS * SLAB_ROWS
CHUNKS_PER_TILE = MOE_TILE // CHUNK_ROWS
VMEM_LIMIT = 56 * 1024 * 1024


def _params(n_axes, vmem=VMEM_LIMIT):
    return pltpu.CompilerParams(dimension_semantics=("arbitrary",) * n_axes, vmem_limit_bytes=vmem)


def _dot(a, b):
    return jnp.dot(a, b, preferred_element_type=F32)


def _div_pow2(x, d):
    assert d & (d - 1) == 0
    return lax.shift_right_logical(x, jnp.int32(d.bit_length() - 1))


def _mod_pow2(x, d):
    assert d & (d - 1) == 0
    return lax.bitwise_and(x, jnp.int32(d - 1))


def _split2(x):
    hi = x.astype(BF16)
    lo = (x - hi.astype(F32)).astype(BF16)
    return hi, lo


def _dot3(a, b):
    ah, al = _split2(a)
    bh, bl = _split2(b)
    return _dot(ah, bh) + _dot(ah, bl) + _dot(al, bh)


def _silu(x):
    return x * (0.5 * jnp.tanh(0.5 * x) + 0.5)


def _layer_norm(x, g, b):
    mu = jnp.mean(x, axis=-1, keepdims=True)
    xc = x - mu
    var = jnp.mean(xc * xc, axis=-1, keepdims=True)
    return xc * lax.rsqrt(var + LN_EPS) * g + b


def _ada_kernel(c_ref, w_ref, b_ref, o_ref):
    o_ref[...] = _dot3(_silu(c_ref[...]), w_ref[...]) + b_ref[...]


def _ada(cond, w_ada, b_ada):
    rows = cond.shape[0]
    n_out = w_ada.shape[1]
    tn = 1024
    return pl.pallas_call(
        _ada_kernel,
        grid=(n_out // tn,),
        in_specs=[
            pl.BlockSpec((rows, D_MODEL), lambda j: (0, 0)),
            pl.BlockSpec((D_MODEL, tn), lambda j: (0, j)),
            pl.BlockSpec((1, tn), lambda j: (0, j)),
        ],
        out_specs=pl.BlockSpec((rows, tn), lambda j: (0, j)),
        out_shape=jax.ShapeDtypeStruct((rows, n_out), F32),
        compiler_params=_params(1),
        name="ada",
    )(cond, w_ada, b_ada)


def _log_sigmoid(z):
    return jnp.minimum(z, 0.0) - jnp.log(1.0 + jnp.exp(-jnp.abs(z)))


def _proj_kernel(latent, tm, x_ref, mod_ref, lnp_ref, w_ref, cw_ref, cb_ref, w2_ref, gbias_ref, *out_refs):
    x = x_ref[...]
    xn = _layer_norm(x, lnp_ref[0:1, :], lnp_ref[1:2, :])
    h = xn * mod_ref[0:1, :] + mod_ref[1:2, :]
    hb = h.astype(BF16)
    if latent:
        ya_ref, q_ref, k_ref, v_ref, sr_ref, g_ref = out_refs
        p = _dot(hb, w_ref[:, OFF_AB:OFF_Q])
        a_b = p[:, 0:CONV_CH]
        u = p[:, CONV_CH:2 * CONV_CH] * p[:, 2 * CONV_CH:3 * CONV_CH]
        pos = _mod_pow2(lax.broadcasted_iota(jnp.int32, (tm, 1), 0), GRID_W)
        u_prev = jnp.where(pos == 0, 0.0, pltpu.roll(u, 1, 0))
        u_next = jnp.where(pos == GRID_W - 1, 0.0, pltpu.roll(u, tm - 1, 0))
        conv = u_prev * cw_ref[0:1, :] + u * cw_ref[1:2, :] + u_next * cw_ref[2:3, :] + cb_ref[...]
        ya_ref[...] = (a_b * conv).astype(BF16)
        qk = _dot(hb, w_ref[:, OFF_Q:OFF_V])
        q_ref[...] = (qk[:, 0:GLA_KEY] * (GLA_DK ** -0.5)).astype(BF16)
        k_ref[...] = qk[:, GLA_KEY:].astype(BF16)
        r = _dot(hb, w_ref[:, OFF_R:OFF_GF])
        sr_ref[...] = _silu(r).astype(BF16)
    else:
        k_ref, v_ref, g_ref = out_refs
        k_ref[...] = _dot(hb, w_ref[:, OFF_K:OFF_V]).astype(BF16)
    v_ref[...] = _dot(hb, w_ref[:, OFF_V:OFF_R]).astype(BF16)
    low = _dot(hb, w_ref[:, OFF_GF:D_PROJ])
    z = _dot(low.astype(BF16), w2_ref[...]) + gbias_ref[...]
    g_ref[...] = _log_sigmoid(z) * (1.0 / GLA_TAU)


def _proj(x, mod, lnp, w_in_b, conv_w, conv_b, w2cat, gbias, latent):
    bsz, t, _ = x.shape
    tm = min(1024, t)
    assert t % tm == 0 and tm % GRID_W == 0
    tok = lambda w: pl.BlockSpec((None, tm, w), lambda b, i: (b, i, 0))
    full = lambda a: pl.BlockSpec(a.shape, lambda b, i: (0,) * a.ndim)
    widths = ([(CONV_CH, BF16), (GLA_KEY, BF16)] if latent else []) + [(GLA_KEY, BF16), (GLA_VAL, BF16)]
    widths += ([(GLA_VAL, BF16)] if latent else []) + [(2 * GLA_KEY, F32)]
    return pl.pallas_call(
        functools.partial(_proj_kernel, latent, tm),
        grid=(bsz, t // tm),
        in_specs=[
            tok(D_MODEL),
            pl.BlockSpec((None, 2, D_MODEL), lambda b, i: (b, 0, 0)),
            full(lnp), full(w_in_b), full(conv_w), full(conv_b), full(w2cat), full(gbias),
        ],
        out_specs=[tok(w) for w, _ in widths],
        out_shape=[jax.ShapeDtypeStruct((bsz, t, w), dt) for w, dt in widths],
        compiler_params=_params(2),
        name="proj_latent" if latent else "proj_ctx",
    )(x, mod, lnp, w_in_b, conv_w, conv_b, w2cat, gbias)


def _tri(n, reverse, strict=False):
    i = lax.broadcasted_iota(jnp.int32, (n, n), 0)
    j = lax.broadcasted_iota(jnp.int32, (n, n), 1)
    if strict:
        m = (j > i) if reverse else (j < i)
    else:
        m = (j >= i) if reverse else (j <= i)
    return jnp.where(m, 1.0, 0.0).astype(BF16)


def _chunk_cumsum(g, reverse):
    tri = _tri(GLA_CHUNK, reverse)
    g_hi, g_lo = _split2(g)
    return _dot(tri, g_hi) + _dot(tri, g_lo)


def _as_column(row):
    return jnp.broadcast_to(row, (LANES, row.shape[1])).T


def _sub_anchors(gc, reverse):
    zero = jnp.zeros((1, GLA_KEY), F32)
    if reverse:
        return [gc[GLA_SUB * (a + 1):GLA_SUB * (a + 1) + 1] for a in range(N_SUB - 1)] + [zero]
    return [zero] + [gc[GLA_SUB * a - 1:GLA_SUB * a] for a in range(1, N_SUB)]


def _score_pairs(reverse):
    return [(a, b) for a in range(N_SUB) for b in range(N_SUB) if (b >= a if reverse else b <= a)]


def _intra_products(q, k, gc, reverse):
    r = _sub_anchors(gc, reverse)
    anchor = jnp.concatenate([jnp.broadcast_to(ra, (GLA_SUB, GLA_KEY)) for ra in r], axis=0)
    gcb = gc - anchor
    qt = q * jnp.exp(gcb)
    kt = k * jnp.exp(-gcb)
    rows = []
    for a, b in _score_pairs(reverse):
        qa = qt[GLA_SUB * a:GLA_SUB * (a + 1)]
        if a != b:
            qa = qa * jnp.exp(r[a] - r[b])
        rows.append(qa)
    qp = jnp.concatenate(rows, axis=0).astype(BF16)
    width = GLA_HEADS * GLA_CHUNK
    rr = lax.broadcasted_iota(jnp.int32, (width, GLA_KEY), 0)
    cc = lax.broadcasted_iota(jnp.int32, (width, GLA_KEY), 1)
    kbd = jnp.where(_div_pow2(rr, GLA_CHUNK) == _div_pow2(cc, GLA_DK),
                    jnp.concatenate([kt] * GLA_HEADS, axis=0), 0.0)
    return lax.dot_general(qp, kbd.astype(BF16), (((1,), (1,)), ((), ())), preferred_element_type=F32)


def _assemble_scores(res, reverse):
    pairs = _score_pairs(reverse)
    width = GLA_HEADS * GLA_CHUNK
    col = _mod_pow2(lax.broadcasted_iota(jnp.int32, (GLA_SUB, width), 1), GLA_CHUNK)
    col_blk = _div_pow2(col, GLA_SUB)
    col_pos = _mod_pow2(col, GLA_SUB)
    row_pos = lax.broadcasted_iota(jnp.int32, (GLA_SUB, width), 0)
    causal = (col_pos >= row_pos) if reverse else (col_pos <= row_pos)
    blocks = []
    for a in range(N_SUB):
        acc = jnp.zeros((GLA_SUB, width), F32)
        for idx, (pa, pb) in enumerate(pairs):
            if pa != a:
                continue
            keep = col_blk == pb
            if pa == pb:
                keep = keep & causal
            acc = acc + jnp.where(keep, res[GLA_SUB * idx:GLA_SUB * (idx + 1)], 0.0)
        blocks.append(acc)
    return jnp.concatenate(blocks, axis=0)


def _pair_mask(rows_per_head, cols_per_head, n_row_pairs=1):
    shape = (n_row_pairs * 2 * rows_per_head, 2 * cols_per_head)
    rr = _mod_pow2(lax.broadcasted_iota(jnp.int32, shape, 0), 2 * rows_per_head)
    cc = lax.broadcasted_iota(jnp.int32, shape, 1)
    return _div_pow2(rr, rows_per_head) == _div_pow2(cc, cols_per_head)


def _state_terms(q, k, v_b, gc, reverse):
    total = gc[0:1] if reverse else gc[GLA_CHUNK - 1:GLA_CHUNK]
    q_dec = None if q is None else (q * jnp.exp(gc)).astype(BF16)
    k_end = (k * jnp.exp(total - gc)).astype(BF16)
    tn = (((0,), (0,)), ((), ()))
    upd = [lax.dot_general(k_end[:, p * PAIR_KEY:(p + 1) * PAIR_KEY], v_b[:, p * PAIR_VAL:(p + 1) * PAIR_VAL], tn,
                           preferred_element_type=F32) for p in range(GLA_HEADS // 2)]
    upd = jnp.where(_pair_mask(GLA_DK, GLA_DV, GLA_HEADS // 2), jnp.concatenate(upd, axis=0), 0.0)
    decay = jnp.exp(_as_column(total))
    decay = jnp.concatenate([decay] * (PAIR_VAL // LANES), axis=1)
    return q_dec, decay, upd


def _advance_state(q_dec, decay, upd, state):
    o_inter = None
    if q_dec is not None:
        state_b = state.astype(BF16)
        o_inter = jnp.concatenate(
            [_dot(q_dec[:, p * PAIR_KEY:(p + 1) * PAIR_KEY], state_b[p * PAIR_KEY:(p + 1) * PAIR_KEY, :])
             for p in range(GLA_HEADS // 2)], axis=1)
    return o_inter, state * decay + upd


def _gla_kernel(tt, nt, with_out, *refs):
    if with_out:
        (qf_ref, kf_ref, vf_ref, gfw_ref, qb_ref, kb_ref, vb_ref, gbw_ref, s0f_ref, s0b_ref,
         of_ref, ob_ref, sf_ref, sb_ref) = refs
    else:
        kf_ref, vf_ref, gfw_ref, kb_ref, vb_ref, gbw_ref, s0f_ref, s0b_ref, sf_ref, sb_ref = refs
    j = pl.program_id(1)

    @pl.when(j == 0)
    def _():
        sf_ref[...] = s0f_ref[...]
        sb_ref[...] = s0b_ref[...]

    chunk_slices = [slice(c * GLA_CHUNK, (c + 1) * GLA_CHUNK) for c in range(tt // GLA_CHUNK)]
    fwd, bwd = [], []
    for sl in chunk_slices:
        g = gfw_ref[sl, :]
        fwd.append(dict(sl=sl, k=kf_ref[sl, :].astype(F32), v=vf_ref[sl, :], g_f=g[:, 0:GLA_KEY], g_b=g[:, GLA_KEY:],
                        q=qf_ref[sl, :].astype(F32) if with_out else None))
        bwd.append(dict(sl=sl, k=kb_ref[sl, :].astype(F32), v=vb_ref[sl, :], g_b=gbw_ref[sl, :][:, GLA_KEY:],
                        q=qb_ref[sl, :].astype(F32) if with_out else None))
    for d in fwd:
        d["gc_f"] = _chunk_cumsum(d["g_f"], False)
        if with_out:
            d["gc_b"] = _chunk_cumsum(d["g_b"], True)
    for d in bwd:
        d["gc_b"] = _chunk_cumsum(d["g_b"], True)
    if with_out:
        for d in fwd:
            d["res_f"] = _intra_products(d["q"], d["k"], d["gc_f"], False)
            d["res_b"] = _intra_products(d["q"], d["k"], d["gc_b"], True)
    for d in fwd:
        d["terms"] = _state_terms(d["q"], d["k"], d["v"], d["gc_f"], False)
    for d in bwd:
        d["terms"] = _state_terms(d["q"], d["k"], d["v"], d["gc_b"], True)
    if with_out:
        for d in fwd:
            scores = (_assemble_scores(d["res_f"], False) + _assemble_scores(d["res_b"], True)).astype(BF16)
            o_intra = []
            for p in range(GLA_HEADS // 2):
                v_p = d["v"][:, p * PAIR_VAL:(p + 1) * PAIR_VAL]
                vbd = jnp.where(_pair_mask(GLA_CHUNK, GLA_DV), jnp.concatenate([v_p, v_p], axis=0),
                                jnp.zeros((), BF16))
                o_intra.append(_dot(scores[:, p * 2 * GLA_CHUNK:(p + 1) * 2 * GLA_CHUNK], vbd))
            d["o_intra"] = jnp.concatenate(o_intra, axis=1)

    state = sf_ref[...]
    for d in fwd:
        o_inter, state = _advance_state(*d["terms"], state)
        if with_out:
            of_ref[d["sl"], :] = (d["o_intra"] + o_inter).astype(BF16)
    sf_ref[...] = state

    state = sb_ref[...]
    for d in reversed(bwd):
        o_inter, state = _advance_state(*d["terms"], state)
        if with_out:
            ob_ref[d["sl"], :] = o_inter.astype(BF16)
    sb_ref[...] = state


def _gla(q, k, v, g, s0f, s0b):
    with_out = q is not None
    bsz, t, _ = k.shape
    tt = min(1024, t)
    assert t % tt == 0 and tt % GLA_CHUNK == 0
    nt = t // tt
    fwd = lambda w: pl.BlockSpec((None, tt, w), lambda b, j: (b, j, 0))
    bwd = lambda w: pl.BlockSpec((None, tt, w), lambda b, j: (b, nt - 1 - j, 0))
    st = pl.BlockSpec((None, GLA_KEY, PAIR_VAL), lambda b, j: (b, 0, 0))
    st_shape = jax.ShapeDtypeStruct((bsz, GLA_KEY, PAIR_VAL), F32)
    if with_out:
        ins = [q, k, v, g, q, k, v, g, s0f, s0b]
        in_specs = [fwd(GLA_KEY), fwd(GLA_KEY), fwd(GLA_VAL), fwd(2 * GLA_KEY),
                    bwd(GLA_KEY), bwd(GLA_KEY), bwd(GLA_VAL), bwd(2 * GLA_KEY), st, st]
        out_specs = [fwd(GLA_VAL), bwd(GLA_VAL), st, st]
        o_shape = jax.ShapeDtypeStruct((bsz, t, GLA_VAL), BF16)
        out_shape = [o_shape, o_shape, st_shape, st_shape]
    else:
        ins = [k, v, g, k, v, g, s0f, s0b]
        in_specs = [fwd(GLA_KEY), fwd(GLA_VAL), fwd(2 * GLA_KEY),
                    bwd(GLA_KEY), bwd(GLA_VAL), bwd(2 * GLA_KEY), st, st]
        out_specs = [st, st]
        out_shape = [st_shape, st_shape]
    return pl.pallas_call(
        functools.partial(_gla_kernel, tt, nt, with_out),
        grid=(bsz, nt),
        in_specs=in_specs,
        out_specs=out_specs,
        out_shape=out_shape,
        compiler_params=_params(2),
        name="gla_latent" if with_out else "gla_ctx",
    )(*ins)


def _exact_bf16_parts(x):
    hi = x.astype(BF16).astype(F32)
    r = x - hi
    mid = r.astype(BF16).astype(F32)
    lo = (r - mid).astype(BF16).astype(F32)
    return hi, mid, lo


def _first_index(values, best):
    idx = jnp.full_like(best, float(len(values) - 1))
    for i in reversed(range(len(values) - 1)):
        idx = jnp.where(values[i] >= best, float(i), idx)
    return idx


def _pick(rows, idx):
    out = rows[-1]
    for i in reversed(range(len(rows) - 1)):
        out = jnp.where(idx == float(i), rows[i], out)
    return out


def _route(logit_t, tm):
    row = lambda r: logit_t[r:r + 1, :]
    groups = [row(i) for i in range(N_GROUPS)]
    top = functools.reduce(jnp.maximum, groups)
    eg = [jnp.exp(x - top) for x in groups]
    total = functools.reduce(lambda a, b: a + b, eg)
    pg = [e / total for e in eg]
    p_g = functools.reduce(jnp.maximum, pg)
    g_idx = _first_index(pg, p_g)
    sel = [_pick([row(N_GROUPS + EXPERTS_PER_GROUP * g + j) for g in range(N_GROUPS)], g_idx)
           for j in range(EXPERTS_PER_GROUP)]
    top = functools.reduce(jnp.maximum, sel)
    ee = [jnp.exp(x - top) for x in sel]
    total = functools.reduce(lambda a, b: a + b, ee)
    pe = [e / total for e in ee]
    p1 = functools.reduce(jnp.maximum, pe)
    l1 = _first_index(pe, p1)
    pe2 = [jnp.where(l1 == float(j), -1.0, pe[j]) for j in range(EXPERTS_PER_GROUP)]
    p2 = functools.reduce(jnp.maximum, pe2)
    l2 = _first_index(pe2, p2)
    den = p1 + p2
    w1 = p1 / den * p_g
    w2 = p2 / den * p_g
    lo = jnp.minimum(l1, l2)
    hi = jnp.maximum(l1, l2)
    pair = lo * (7.0 - lo) * 0.5 + (hi - lo - 1.0)
    cls = g_idx * PAIRS_PER_GROUP + pair
    w_lo = jnp.where(l1 < l2, w1, w2)
    w_hi = jnp.where(l1 < l2, w2, w1)
    cls_id = lax.broadcasted_iota(jnp.int32, (ROUTE_W, tm), 0).astype(F32)
    onehot = jnp.where(cls_id == cls, 1.0, 0.0)
    before = _dot(onehot.astype(BF16), _tri(tm, True, strict=True))
    count = jnp.sum(onehot, axis=1, keepdims=True)
    chunks = jnp.floor((count + (CHUNK_ROWS - 1.0)) * (1.0 / CHUNK_ROWS))
    first_chunk = _dot(_tri(ROUTE_W, False, strict=True),
                       jnp.broadcast_to(chunks, (ROUTE_W, LANES)).astype(BF16))[:, 0:1]
    pos_row = jnp.sum(onehot * (CHUNK_ROWS * first_chunk + before), axis=0, keepdims=True)
    return pos_row, w_lo, w_hi, chunks


def _slab_targets(slab_axis):
    shape = (LOCAL_SLAB_ROWS, 1) if slab_axis == 0 else (1, LOCAL_SLAB_ROWS)
    slab_row = lax.broadcasted_iota(jnp.int32, shape, slab_axis)
    sub = _mod_pow2(slab_row, SLAB_ROWS)
    token_row = (CHUNK_ROWS * _div_pow2(slab_row, SLAB_ROWS) + _div_pow2(sub, 2)).astype(F32)
    half = _mod_pow2(sub, 2)
    return [jnp.where(half == h, token_row, -1.0) for h in range(2)]


def _slab_sort_matrices(pos, targets):
    return [jnp.where(t == pos, 1.0, 0.0).astype(BF16) for t in targets]


def _mix_out_kernel(tm, x_ref, of_ref, ob_ref, sr_ref, ya_ref, mod_ref, lnp_ref, gn_ref, wo_ref, wr_ref, br_ref,
                    x1_ref, hxs_ref, pos_ref, chunks_ref):
    subs = [slice(s, s + MIX_SUB_TILE) for s in range(0, tm, MIX_SUB_TILE)]
    yb = []
    for rows in subs:
        o = of_ref[rows, :].astype(F32) + ob_ref[rows, :].astype(F32)
        sr = sr_ref[rows, :].astype(F32)
        heads = []
        for h in range(GLA_HEADS):
            sl = slice(h * GLA_DV, (h + 1) * GLA_DV)
            oh = o[:, sl]
            ms = jnp.mean(oh * oh, axis=-1, keepdims=True)
            heads.append((oh * lax.rsqrt(ms + RMS_EPS) * gn_ref[...] * sr[:, sl]).astype(BF16))
        yb.append(jnp.concatenate([ya_ref[rows, :]] + heads, axis=1))
    xn = [_layer_norm(x_ref[rows, :], lnp_ref[0:1, :], lnp_ref[1:2, :]) for rows in subs]
    y = [_dot(y_in, wo_ref[...]) for y_in in yb]
    h2_b = []
    for rows, xn_s, y_s in zip(subs, xn, y):
        x1 = _layer_norm(DEEPNORM_ALPHA * xn_s + mod_ref[0:1, :] * y_s, lnp_ref[2:3, :], lnp_ref[3:4, :])
        x1_ref[rows, :] = x1
        h2_b.append((x1 * mod_ref[1:2, :] + mod_ref[2:3, :]).astype(BF16))
    logit_t = [(_dot(h2_s, wr_ref[...]) + br_ref[...]).T for h2_s in h2_b]
    per_sort = SORT_TILE // MIX_SUB_TILE
    routes = []
    for s in range(tm // SORT_TILE):
        routes.append(_route(jnp.concatenate(logit_t[s * per_sort:(s + 1) * per_sort], axis=1), SORT_TILE))
    rec_id = lax.broadcasted_iota(jnp.int32, (ROUTE_W, SORT_TILE), 0)
    targets = _slab_targets(0)
    for s, (pos_row, w_lo, w_hi, chunks) in enumerate(routes):
        rec_t = jnp.zeros((ROUTE_W, SORT_TILE), F32)
        for i, part in enumerate(_exact_bf16_parts(w_lo) + _exact_bf16_parts(w_hi)):
            rec_t = jnp.where(rec_id == i, part, rec_t)
        rec_b = rec_t.T.astype(BF16)
        h2_s = jnp.concatenate(h2_b[s * per_sort:(s + 1) * per_sort], axis=0)
        sort_lo, sort_hi = _slab_sort_matrices(pos_row, targets)
        pay_lo = jnp.concatenate([h2_s[:, 0:HALF_W], rec_b], axis=1)
        pay_hi = jnp.concatenate([h2_s[:, HALF_W:], jnp.zeros((SORT_TILE, ROUTE_W), BF16)], axis=1)
        slabs = _dot(sort_lo, pay_lo) + _dot(sort_hi, pay_hi)
        for c in range(SLAB_IN_W // LANES):
            hxs_ref[c, s * LOCAL_SLAB_ROWS:(s + 1) * LOCAL_SLAB_ROWS, :] = slabs[:, c * LANES:(c + 1) * LANES]
        pos_ref[s * SORT_TILE:(s + 1) * SORT_TILE, :] = jnp.broadcast_to(pos_row, (ROUTE_W, SORT_TILE)).T
        chunks_ref[8 * s:8 * (s + 1), :] = jnp.broadcast_to(chunks, (ROUTE_W, LANES)).T[0:8, :]


def _mix_out(x, o_f, o_b, sr, ya, mod, lnp, gn, w_out_b, wr, br):
    bsz, t, _ = x.shape
    assert t % SORT_TILE == 0
    n_sort = max(n for n in (1, 2, 4) if t % (n * SORT_TILE) == 0)
    tm = n_sort * SORT_TILE
    n_t = t // tm
    tok = lambda w: pl.BlockSpec((None, tm, w), lambda b, i: (b, i, 0))
    full = lambda a: pl.BlockSpec(a.shape, lambda b, i: (0,) * a.ndim)
    flat = lambda rows, w: pl.BlockSpec((rows, w), lambda b, i: (b * n_t + i, 0))
    return pl.pallas_call(
        functools.partial(_mix_out_kernel, tm),
        grid=(bsz, n_t),
        in_specs=[
            tok(D_MODEL), tok(GLA_VAL), tok(GLA_VAL), tok(GLA_VAL), tok(CONV_CH),
            pl.BlockSpec((None, 3, D_MODEL), lambda b, i: (b, 0, 0)),
            full(lnp), full(gn), full(w_out_b), full(wr), full(br),
        ],
        out_specs=[tok(D_MODEL),
                   pl.BlockSpec((SLAB_IN_W // LANES, n_sort * LOCAL_SLAB_ROWS, LANES),
                                lambda b, i: (0, b * n_t + i, 0)),
                   flat(tm, ROUTE_W), flat(n_sort * 8, ROUTE_W)],
        out_shape=[
            jax.ShapeDtypeStruct((bsz, t, D_MODEL), F32),
            jax.ShapeDtypeStruct((SLAB_IN_W // LANES, bsz * t // SORT_TILE * LOCAL_SLAB_ROWS, LANES), F32),
            jax.ShapeDtypeStruct((bsz * t, ROUTE_W), F32),
            jax.ShapeDtypeStruct((bsz * t // SORT_TILE * 8, ROUTE_W), F32),
        ],
        compiler_params=_params(2),
        name="mix_out",
    )(x, o_f, o_b, sr, ya, mod, lnp, gn, w_out_b, wr, br)


def _moe_kernel(nused_ref, lo_ref, hi_ref, src_ref,
                hxs_hbm, w1l_ref, w3l_ref, w2l_ref, w1h_ref, w3h_ref, w2h_ref, out_ref, gbuf, gsem):
    tile_rows = CHUNKS_PER_TILE * SLAB_ROWS
    i = pl.program_id(0)
    n_used = nused_ref[0]
    slot = lax.bitwise_and(i, 1)

    def slab(chunk):
        return pl.ds(pl.multiple_of(chunk * SLAB_ROWS, SLAB_ROWS), SLAB_ROWS)

    def start_gather(tile, buf_slot):
        for j in range(CHUNKS_PER_TILE):
            chunk = src_ref[tile * CHUNKS_PER_TILE + j]
            pltpu.make_async_copy(hxs_hbm.at[:, slab(chunk), :], gbuf.at[buf_slot, :, slab(j), :],
                                  gsem.at[buf_slot]).start(priority=j % 2)

    @pl.when(i == 0)
    def _():
        start_gather(0, 0)

    @pl.when(i + 1 < n_used)
    def _():
        start_gather(i + 1, 1 - slot)

    @pl.when(i >= n_used)
    def _():
        out_ref[...] = jnp.zeros(out_ref.shape, F32)

    @pl.when(i < n_used)
    def _():
        pltpu.make_async_copy(hxs_hbm.at[:, pl.ds(0, tile_rows), :], gbuf.at[slot], gsem.at[slot]).wait()

        def lane_block(c, half):
            return jnp.concatenate(
                [gbuf[slot, c, pl.ds(2 * r + half, CHUNKS_PER_TILE, stride=SLAB_ROWS), :]
                 for r in range(CHUNK_ROWS)], axis=0)

        n_blk = HALF_W // LANES
        xb = jnp.concatenate([lane_block(c, 0) for c in range(n_blk)]
                             + [lane_block(c, 1) for c in range(n_blk)], axis=1).astype(BF16)
        rec = lane_block(n_blk, 0)
        w_lo = rec[:, 0:1] + rec[:, 1:2] + rec[:, 2:3]
        w_hi = rec[:, 3:4] + rec[:, 4:5] + rec[:, 5:6]

        gate = [_dot(xb, w1_ref[...]) for w1_ref in (w1l_ref, w1h_ref)]
        up = [_dot(xb, w3_ref[...]) for w3_ref in (w3l_ref, w3h_ref)]
        act = [(_silu(g) * u).astype(BF16) for g, u in zip(gate, up)]
        e_lo, e_hi = [_dot(a, w2_ref[...]) for a, w2_ref in zip(act, (w2l_ref, w2h_ref))]
        y = w_lo * e_lo + w_hi * e_hi
        for r in range(CHUNK_ROWS):
            rows = slice(r * CHUNKS_PER_TILE, (r + 1) * CHUNKS_PER_TILE)
            for half in range(2):
                for c in range(n_blk):
                    col = half * HALF_W + c * LANES
                    out_ref[c, pl.ds(2 * r + half, CHUNKS_PER_TILE, stride=SLAB_ROWS), :] = y[rows, col:col + LANES]


def _moe(hxs, src, n_used, tile_lo, tile_hi, w1_b, w3_b, w2_b):
    tile_rows = CHUNKS_PER_TILE * SLAB_ROWS
    n_steps = src.shape[0] // CHUNKS_PER_TILE
    wspec = lambda which, shape: pl.BlockSpec(
        (None,) + shape, (lambda i, nu, lo, hi, s: (lo[i], 0, 0)) if which == 0 else
        (lambda i, nu, lo, hi, s: (hi[i], 0, 0)))
    grid_spec = pltpu.PrefetchScalarGridSpec(
        num_scalar_prefetch=4,
        grid=(n_steps,),
        in_specs=[
            pl.BlockSpec(memory_space=pl.ANY),
            wspec(0, (D_MODEL, D_EXPERT)), wspec(0, (D_MODEL, D_EXPERT)), wspec(0, (D_EXPERT, D_MODEL)),
            wspec(1, (D_MODEL, D_EXPERT)), wspec(1, (D_MODEL, D_EXPERT)), wspec(1, (D_EXPERT, D_MODEL)),
        ],
        out_specs=pl.BlockSpec((SLAB_OUT_W // LANES, tile_rows, LANES), lambda i, nu, lo, hi, s: (0, i, 0)),
        scratch_shapes=[
            pltpu.VMEM((2, SLAB_IN_W // LANES, tile_rows, LANES), F32),
            pltpu.SemaphoreType.DMA((2,)),
        ],
    )
    return pl.pallas_call(
        _moe_kernel,
        grid_spec=grid_spec,
        out_shape=jax.ShapeDtypeStruct((SLAB_OUT_W // LANES, n_steps * tile_rows, LANES), F32),
        compiler_params=_params(1),
        name="moe",
    )(n_used, tile_lo, tile_hi, src, hxs, w1_b, w3_b, w2_b, w1_b, w3_b, w2_b)


def _final_kernel(n_sort, back_ref, x1_ref, moe_hbm, pos_ref, mod_ref, lnp_ref, o_ref, mbuf, msem):
    step = pl.program_id(0) * pl.num_programs(1) + pl.program_id(1)
    n_steps = pl.num_programs(0) * pl.num_programs(1)
    slot = lax.bitwise_and(step, 1)
    per_step = n_sort * LOCAL_CHUNKS

    def slab(chunk):
        return pl.ds(pl.multiple_of(chunk * SLAB_ROWS, SLAB_ROWS), SLAB_ROWS)

    def start_gather(s, buf_slot):
        for j in range(per_step):
            chunk = back_ref[s * per_step + j]
            pltpu.make_async_copy(moe_hbm.at[:, slab(chunk), :], mbuf.at[buf_slot, :, slab(j), :],
                                  msem.at[buf_slot]).start(priority=j % 2)

    @pl.when(step == 0)
    def _():
        start_gather(0, 0)

    @pl.when(step + 1 < n_steps)
    def _():
        start_gather(step + 1, 1 - slot)

    pltpu.make_async_copy(moe_hbm.at[:, pl.ds(0, per_step * SLAB_ROWS), :], mbuf.at[slot], msem.at[slot]).wait()
    moe = []
    targets = _slab_targets(1)
    for s in range(n_sort):
        slab_rows = pl.ds(s * LOCAL_SLAB_ROWS, LOCAL_SLAB_ROWS)
        moe_b = jnp.concatenate([mbuf[slot, c, slab_rows, :] for c in range(SLAB_OUT_W // LANES)],
                                axis=1).astype(BF16)
        sort_lo, sort_hi = _slab_sort_matrices(pos_ref[s * SORT_TILE:(s + 1) * SORT_TILE, 0:1], targets)
        for r in range(0, SORT_TILE, MIX_SUB_TILE):
            rows = slice(r, r + MIX_SUB_TILE)
            moe.append((s * SORT_TILE + r,
                        jnp.concatenate([_dot(sort_lo[rows, :], moe_b), _dot(sort_hi[rows, :], moe_b)], axis=1)))
    for start, moe_s in moe:
        rows = slice(start, start + MIX_SUB_TILE)
        o_ref[rows, :] = _layer_norm(DEEPNORM_ALPHA * x1_ref[rows, :] + mod_ref[...] * moe_s,
                                     lnp_ref[0:1, :], lnp_ref[1:2, :])


def _final(x1, moe, back, pos, g2, lnp):
    bsz, t, _ = x1.shape
    n_sort = max(n for n in (1, 2, 4) if t % (n * SORT_TILE) == 0)
    tm = n_sort * SORT_TILE
    n_t = t // tm
    grid_spec = pltpu.PrefetchScalarGridSpec(
        num_scalar_prefetch=1,
        grid=(bsz, n_t),
        in_specs=[
            pl.BlockSpec((None, tm, D_MODEL), lambda b, i, bk: (b, i, 0)),
            pl.BlockSpec(memory_space=pl.ANY),
            pl.BlockSpec((tm, ROUTE_W), lambda b, i, bk: (b * n_t + i, 0)),
            pl.BlockSpec((None, 1, D_MODEL), lambda b, i, bk: (b, 0, 0)),
            pl.BlockSpec(lnp.shape, lambda b, i, bk: (0, 0)),
        ],
        out_specs=pl.BlockSpec((None, tm, D_MODEL), lambda b, i, bk: (b, i, 0)),
        scratch_shapes=[
            pltpu.VMEM((2, SLAB_OUT_W // LANES, n_sort * LOCAL_SLAB_ROWS, LANES), F32),
            pltpu.SemaphoreType.DMA((2,)),
        ],
    )
    return pl.pallas_call(
        functools.partial(_final_kernel, n_sort),
        grid_spec=grid_spec,
        out_shape=jax.ShapeDtypeStruct((bsz, t, D_MODEL), F32),
        compiler_params=_params(2),
        name="final",
    )(back, x1, moe, pos, g2, lnp)


def _pair_tables():
    lo, hi = [], []
    for g in range(N_GROUPS):
        for a in range(EXPERTS_PER_GROUP):
            for b in range(a + 1, EXPERTS_PER_GROUP):
                lo.append(g * EXPERTS_PER_GROUP + a)
                hi.append(g * EXPERTS_PER_GROUP + b)
    return jnp.array(lo, jnp.int32), jnp.array(hi, jnp.int32)


def _moe_plan(chunks, n_sort_tiles):
    hp = lax.Precision.HIGHEST
    m = chunks.reshape(n_sort_tiles, 8, ROUTE_W)[:, 0, :N_CLASSES].astype(jnp.int32)
    a_end = jnp.cumsum(m, axis=0)
    a_start = a_end - m
    per_cls = a_end[-1]
    padded = (per_cls + CHUNKS_PER_TILE - 1) // CHUNKS_PER_TILE * CHUNKS_PER_TILE
    g_end = jnp.cumsum(padded)
    g_start = g_end - padded
    local_end = jnp.cumsum(m, axis=1)
    local_off = local_end - m
    seg = jnp.arange(n_sort_tiles, dtype=jnp.int32)[:, None] * LOCAL_CHUNKS + local_off - a_start
    n_steps = -(-(n_sort_tiles * LOCAL_CHUNKS) // CHUNKS_PER_TILE) + N_CLASSES
    p = jnp.arange(n_steps * CHUNKS_PER_TILE, dtype=jnp.int32)
    cls_p = jnp.minimum(jnp.sum((g_end[None, :] <= p[:, None]).astype(jnp.int32), axis=1), N_CLASSES - 1)
    onehot = (cls_p[:, None] == jnp.arange(N_CLASSES, dtype=jnp.int32)[None, :]).astype(F32)
    pick = lambda tab: jnp.dot(onehot, tab.astype(F32), precision=hp)
    u = p - pick(g_start[:, None])[:, 0].astype(jnp.int32)
    valid = u < pick(per_cls[:, None])[:, 0].astype(jnp.int32)
    a_end_p = pick(a_end.T).astype(jnp.int32)
    seg_p = pick(seg.T).astype(jnp.int32)
    tile_p = jnp.sum((a_end_p <= u[:, None]).astype(jnp.int32), axis=1)
    hit = jnp.arange(n_sort_tiles, dtype=jnp.int32)[None, :] == tile_p[:, None]
    src = jnp.sum(jnp.where(hit, seg_p, 0), axis=1) + u
    src = jnp.where(valid, src, 0).astype(jnp.int32)
    n_used = g_end[-1:] // CHUNKS_PER_TILE
    step = jnp.arange(n_steps, dtype=jnp.int32)
    tile_cls = jnp.sum((g_end[None, :] // CHUNKS_PER_TILE <= step[:, None]).astype(jnp.int32), axis=1)
    pair_lo, pair_hi = _pair_tables()
    pair_oh = (jnp.minimum(tile_cls, N_CLASSES - 1)[:, None] == jnp.arange(N_CLASSES)[None, :]).astype(jnp.int32)
    tile_lo = jnp.sum(pair_oh * pair_lo[None, :], axis=1).astype(jnp.int32)
    tile_hi = jnp.sum(pair_oh * pair_hi[None, :], axis=1).astype(jnp.int32)
    k = jnp.arange(LOCAL_CHUNKS, dtype=jnp.int32)
    cls_k = jnp.sum((local_end[:, None, :] <= k[None, :, None]).astype(jnp.int32), axis=2)
    cls_oh = jnp.minimum(cls_k, N_CLASSES - 1)[:, :, None] == jnp.arange(N_CLASSES, dtype=jnp.int32)[None, None, :]
    base = g_start[None, :] + a_start - local_off
    back = jnp.sum(jnp.where(cls_oh, base[:, None, :], 0), axis=2) + k[None, :]
    back = jnp.where(cls_k < N_CLASSES, back, 0).astype(jnp.int32).reshape(-1)
    return src, back, n_used.astype(jnp.int32), tile_lo, tile_hi


def kernel(x, c, ctx, c_ctx, ln_in_g, ln_in_b, w_ada, b_ada, w_in, conv_w, conv_b, gate_w2_fwd, gate_b_fwd,
           gate_w2_bwd, gate_b_bwd, gla_norm_g, w_out, ln1_g, ln1_b, router_group_w, router_group_b,
           router_expert_w, router_expert_b, expert_w1, expert_w3, expert_w2, ln2_g, ln2_b):
    bsz, t, _ = x.shape
    n_tok = bsz * t
    l = 0
    rows = -(-(bsz + 1) // 8) * 8
    cond = jnp.zeros((rows, D_MODEL), F32).at[:bsz].set(c).at[bsz].set(c_ctx)
    ada = _ada(cond, w_ada[l], b_ada[l][None, :])
    sh1, sc1, g1, sh2, sc2, g2 = [ada[:, i * D_MODEL:(i + 1) * D_MODEL] for i in range(6)]

    w_in_b = w_in[l].astype(BF16)
    lnp_in = jnp.stack([ln_in_g, ln_in_b])
    zero = jnp.zeros((GLA_GATE_RANK, GLA_KEY), F32)
    w2cat = jnp.concatenate([jnp.concatenate([gate_w2_fwd[l], zero], axis=1),
                             jnp.concatenate([zero, gate_w2_bwd[l]], axis=1)], axis=0).astype(BF16)
    gbias = jnp.concatenate([gate_b_fwd[l], gate_b_bwd[l]])[None, :]

    mod_ctx = jnp.broadcast_to(jnp.stack([1.0 + sc1[bsz], sh1[bsz]])[None], (bsz, 2, D_MODEL))
    k_c, v_c, g_c = _proj(ctx, mod_ctx, lnp_in, w_in_b, conv_w[l], conv_b[l][None, :], w2cat, gbias, False)
    zero_state = jnp.zeros((bsz, GLA_KEY, PAIR_VAL), F32)
    s_f, s_b = _gla(None, k_c, v_c, g_c, zero_state, zero_state)

    mod1 = jnp.stack([1.0 + sc1[:bsz], sh1[:bsz]], axis=1)
    ya, q, k, v, sr, g = _proj(x, mod1, lnp_in, w_in_b, conv_w[l], conv_b[l][None, :], w2cat, gbias, True)
    o_f, o_b, _, _ = _gla(q, k, v, g, s_f, s_b)

    mod2 = jnp.stack([g1[:bsz], 1.0 + sc2[:bsz], sh2[:bsz]], axis=1)
    lnp1 = jnp.stack([ln_in_g, ln_in_b, ln1_g[l], ln1_b[l]])
    wr = jnp.zeros((D_MODEL, ROUTE_W), F32)
    wr = wr.at[:, :N_GROUPS].set(router_group_w[l]).at[:, N_GROUPS:N_GROUPS + N_EXPERTS].set(router_expert_w[l])
    br = jnp.zeros((1, ROUTE_W), F32)
    br = br.at[0, :N_GROUPS].set(router_group_b[l]).at[0, N_GROUPS:N_GROUPS + N_EXPERTS].set(router_expert_b[l])
    x1, hxs, pos, chunks = _mix_out(x, o_f, o_b, sr, ya, mod2, lnp1, gla_norm_g[l][None, :],
                                    w_out[l].astype(BF16), wr.astype(BF16), br)

    src, back, n_used, tile_lo, tile_hi = _moe_plan(chunks, n_tok // SORT_TILE)
    moe = _moe(hxs, src, n_used, tile_lo, tile_hi,
               expert_w1[l].astype(BF16), expert_w3[l].astype(BF16), expert_w2[l].astype(BF16))

    return _final(x1, moe, back, pos, g2[:bsz][:, None, :], jnp.stack([ln2_g[l], ln2_b[l]]))
```

```python
import functools

import jax
import jax.numpy as jnp
from jax import lax
from jax.experimental import pallas as pl
from jax.experimental.pallas import tpu as pltpu

F32 = jnp.float32
BF16 = jnp.bfloat16

D_MODEL = 1024
GRID_W = 64
CONV_CH = 512
GLA_HEADS = 4
GLA_DK = 64
GLA_DV = 128
GLA_KEY = GLA_HEADS * GLA_DK
GLA_VAL = GLA_HEADS * GLA_DV
PAIR_KEY = 2 * GLA_DK
PAIR_VAL = 2 * GLA_DV
GLA_GATE_RANK = 16
GLA_TAU = 16.0
OFF_AB = 0
OFF_AC = OFF_AB + CONV_CH
OFF_AX = OFF_AC + CONV_CH
OFF_Q = OFF_AX + CONV_CH
OFF_K = OFF_Q + GLA_KEY
OFF_V = OFF_K + GLA_KEY
OFF_R = OFF_V + GLA_VAL
OFF_GF = OFF_R + GLA_VAL
D_PROJ = OFF_GF + 2 * GLA_GATE_RANK
N_GROUPS = 4
EXPERTS_PER_GROUP = 4
N_EXPERTS = N_GROUPS * EXPERTS_PER_GROUP
D_EXPERT = 512
PAIRS_PER_GROUP = 6
N_CLASSES = N_GROUPS * PAIRS_PER_GROUP
LN_EPS = 1e-5
RMS_EPS = 1e-6
DEPTH = 1
DEEPNORM_ALPHA = (2.0 * DEPTH) ** 0.25

LANES = 128
GLA_CHUNK = 64
GLA_SUB = 16
N_SUB = GLA_CHUNK // GLA_SUB
ROUTE_W = LANES
HALF_W = D_MODEL // 2
SLAB_IN_W = HALF_W + ROUTE_W
SLAB_OUT_W = HALF_W
ADA_COL_TILE = 1024
TOKEN_TILE = 1024
PROJ_SUB_TILE = 128
SORT_TILE = 256
MIX_SUB_TILE = 128
MOE_TILE = 256
CHUNK_ROWS = 4
SLAB_ROWS = 2 * CHUNK_ROWS
LOCAL_CHUNKS = -(-(SORT_TILE + N_CLASSES * (CHUNK_ROWS - 1)) // CHUNK_ROWS)
LOCAL_SLAB_ROWS = LOCAL_CHUNKS * SLAB_ROWS
CHUNKS_PER_TILE = MOE_TILE // CHUNK_ROWS
VMEM_LIMIT = 56 * 1024 * 1024


def _params(n_axes, vmem=VMEM_LIMIT):
    return pltpu.CompilerParams(dimension_semantics=("arbitrary",) * n_axes, vmem_limit_bytes=vmem)


def _dot(a, b):
    return jnp.dot(a, b, preferred_element_type=F32)


def _div_pow2(x, d):
    assert d & (d - 1) == 0
    return lax.shift_right_logical(x, jnp.int32(d.bit_length() - 1))


def _mod_pow2(x, d):
    assert d & (d - 1) == 0
    return lax.bitwise_and(x, jnp.int32(d - 1))


def _split2(x):
    hi = x.astype(BF16)
    lo = (x - hi.astype(F32)).astype(BF16)
    return hi, lo


def _dot3(a, b):
    ah, al = _split2(a)
    bh, bl = _split2(b)
    return _dot(ah, bh) + _dot(ah, bl) + _dot(al, bh)


def _silu(x):
    return x * (0.5 * jnp.tanh(0.5 * x) + 0.5)


def _layer_norm(x, g, b):
    mu = jnp.mean(x, axis=-1, keepdims=True)
    xc = x - mu
    var = jnp.mean(xc * xc, axis=-1, keepdims=True)
    return xc * lax.rsqrt(var + LN_EPS) * g + b


def _ada_kernel(c_ref, w_ref, b_ref, o_ref):
    o_ref[...] = _dot3(_silu(c_ref[...]), w_ref[...]) + b_ref[...]


def _ada(cond, w_ada, b_ada):
    rows = cond.shape[0]
    n_out = w_ada.shape[1]
    tn = ADA_COL_TILE
    return pl.pallas_call(
        _ada_kernel,
        grid=(n_out // tn,),
        in_specs=[
            pl.BlockSpec((rows, D_MODEL), lambda j: (0, 0)),
            pl.BlockSpec((D_MODEL, tn), lambda j: (0, j)),
            pl.BlockSpec((1, tn), lambda j: (0, j)),
        ],
        out_specs=pl.BlockSpec((rows, tn), lambda j: (0, j)),
        out_shape=jax.ShapeDtypeStruct((rows, n_out), F32),
        compiler_params=_params(1),
        name="ada",
    )(cond, w_ada, b_ada)


def _log_sigmoid(z):
    return jnp.minimum(z, 0.0) - jnp.log(1.0 + jnp.exp(-jnp.abs(z)))


def _proj_kernel(latent, tm, x_ref, mod_ref, lnp_ref, w_ref, cw_ref, cb_ref, w2_ref, gbias_ref, *out_refs):
    sub = min(PROJ_SUB_TILE, tm)

    def stages(rows):
        xn = _layer_norm(x_ref[rows, :], lnp_ref[0:1, :], lnp_ref[1:2, :])
        hb = (xn * mod_ref[0:1, :] + mod_ref[1:2, :]).astype(BF16)
        yield
        if latent:
            ya_ref, q_ref, k_ref, v_ref, sr_ref, g_ref = out_refs
            p = _dot(hb, w_ref[:, OFF_AB:OFF_Q])
            yield
            a_b = p[:, 0:CONV_CH]
            u = p[:, CONV_CH:2 * CONV_CH] * p[:, 2 * CONV_CH:3 * CONV_CH]
            pos = _mod_pow2(lax.broadcasted_iota(jnp.int32, (sub, 1), 0), GRID_W)
            u_prev = jnp.where(pos == 0, 0.0, pltpu.roll(u, 1, 0))
            u_next = jnp.where(pos == GRID_W - 1, 0.0, pltpu.roll(u, sub - 1, 0))
            conv = u_prev * cw_ref[0:1, :] + u * cw_ref[1:2, :] + u_next * cw_ref[2:3, :] + cb_ref[...]
            ya_ref[rows, :] = (a_b * conv).astype(BF16)
            qk = _dot(hb, w_ref[:, OFF_Q:OFF_V])
            yield
            q_ref[rows, :] = (qk[:, 0:GLA_KEY] * (GLA_DK ** -0.5)).astype(BF16)
            k_ref[rows, :] = qk[:, GLA_KEY:].astype(BF16)
            r = _dot(hb, w_ref[:, OFF_R:OFF_GF])
            yield
            sr_ref[rows, :] = _silu(r).astype(BF16)
        else:
            k_ref, v_ref, g_ref = out_refs
            k_ref[rows, :] = _dot(hb, w_ref[:, OFF_K:OFF_V]).astype(BF16)
            yield
        v_ref[rows, :] = _dot(hb, w_ref[:, OFF_V:OFF_R]).astype(BF16)
        low = _dot(hb, w_ref[:, OFF_GF:D_PROJ])
        yield
        z = _dot(low.astype(BF16), w2_ref[...]) + gbias_ref[...]
        g_ref[rows, :] = _log_sigmoid(z) * (1.0 / GLA_TAU)

    pending = []
    for s in range(0, tm, sub):
        pending.append(stages(slice(s, s + sub)))
        pending = [g for g in pending if next(g, True) is None]
    while pending:
        pending = [g for g in pending if next(g, True) is None]


def _proj(x, mod, lnp, w_in_b, conv_w, conv_b, w2cat, gbias, latent):
    bsz, t, _ = x.shape
    tm = min(TOKEN_TILE, t)
    assert t % tm == 0 and tm % GRID_W == 0
    tok = lambda w: pl.BlockSpec((None, tm, w), lambda b, i: (b, i, 0))
    full = lambda a: pl.BlockSpec(a.shape, lambda b, i: (0,) * a.ndim)
    widths = ([(CONV_CH, BF16), (GLA_KEY, BF16)] if latent else []) + [(GLA_KEY, BF16), (GLA_VAL, BF16)]
    widths += ([(GLA_VAL, BF16)] if latent else []) + [(2 * GLA_KEY, F32)]
    return pl.pallas_call(
        functools.partial(_proj_kernel, latent, tm),
        grid=(bsz, t // tm),
        in_specs=[
            tok(D_MODEL),
            pl.BlockSpec((None, 2, D_MODEL), lambda b, i: (b, 0, 0)),
            full(lnp), full(w_in_b), full(conv_w), full(conv_b), full(w2cat), full(gbias),
        ],
        out_specs=[tok(w) for w, _ in widths],
        out_shape=[jax.ShapeDtypeStruct((bsz, t, w), dt) for w, dt in widths],
        compiler_params=_params(2),
        name="proj_latent" if latent else "proj_ctx",
    )(x, mod, lnp, w_in_b, conv_w, conv_b, w2cat, gbias)


def _tri(n, reverse, strict=False):
    i = lax.broadcasted_iota(jnp.int32, (n, n), 0)
    j = lax.broadcasted_iota(jnp.int32, (n, n), 1)
    if strict:
        m = (j > i) if reverse else (j < i)
    else:
        m = (j >= i) if reverse else (j <= i)
    return jnp.where(m, 1.0, 0.0).astype(BF16)


def _chunk_cumsum(g, reverse):
    tri = _tri(GLA_CHUNK, reverse)
    g_hi, g_lo = _split2(g)
    return _dot(tri, g_hi) + _dot(tri, g_lo)


def _as_column(row):
    return jnp.broadcast_to(row, (LANES, row.shape[1])).T


def _sub_anchors(gc, reverse):
    zero = jnp.zeros((1, GLA_KEY), F32)
    if reverse:
        return [gc[GLA_SUB * (a + 1):GLA_SUB * (a + 1) + 1] for a in range(N_SUB - 1)] + [zero]
    return [zero] + [gc[GLA_SUB * a - 1:GLA_SUB * a] for a in range(1, N_SUB)]


def _score_pairs(reverse):
    return [(a, b) for a in range(N_SUB) for b in range(N_SUB) if (b >= a if reverse else b <= a)]


def _intra_products(q, k, gc, reverse):
    r = _sub_anchors(gc, reverse)
    anchor = jnp.concatenate([jnp.broadcast_to(ra, (GLA_SUB, GLA_KEY)) for ra in r], axis=0)
    gcb = gc - anchor
    qt = q * jnp.exp(gcb)
    kt = k * jnp.exp(-gcb)
    rows = []
    for a, b in _score_pairs(reverse):
        qa = qt[GLA_SUB * a:GLA_SUB * (a + 1)]
        if a != b:
            qa = qa * jnp.exp(r[a] - r[b])
        rows.append(qa)
    qp = jnp.concatenate(rows, axis=0).astype(BF16)
    width = GLA_HEADS * GLA_CHUNK
    rr = lax.broadcasted_iota(jnp.int32, (width, GLA_KEY), 0)
    cc = lax.broadcasted_iota(jnp.int32, (width, GLA_KEY), 1)
    kbd = jnp.where(_div_pow2(rr, GLA_CHUNK) == _div_pow2(cc, GLA_DK),
                    jnp.concatenate([kt] * GLA_HEADS, axis=0), 0.0)
    return lax.dot_general(qp, kbd.astype(BF16), (((1,), (1,)), ((), ())), preferred_element_type=F32)


def _assemble_scores(res, reverse):
    pairs = _score_pairs(reverse)
    width = GLA_HEADS * GLA_CHUNK
    col = _mod_pow2(lax.broadcasted_iota(jnp.int32, (GLA_SUB, width), 1), GLA_CHUNK)
    col_blk = _div_pow2(col, GLA_SUB)
    col_pos = _mod_pow2(col, GLA_SUB)
    row_pos = lax.broadcasted_iota(jnp.int32, (GLA_SUB, width), 0)
    causal = (col_pos >= row_pos) if reverse else (col_pos <= row_pos)
    blocks = []
    for a in range(N_SUB):
        acc = jnp.zeros((GLA_SUB, width), F32)
        for idx, (pa, pb) in enumerate(pairs):
            if pa != a:
                continue
            keep = col_blk == pb
            if pa == pb:
                keep = keep & causal
            acc = acc + jnp.where(keep, res[GLA_SUB * idx:GLA_SUB * (idx + 1)], 0.0)
        blocks.append(acc)
    return jnp.concatenate(blocks, axis=0)


def _pair_mask(rows_per_head, cols_per_head, n_row_pairs=1):
    shape = (n_row_pairs * 2 * rows_per_head, 2 * cols_per_head)
    rr = _mod_pow2(lax.broadcasted_iota(jnp.int32, shape, 0), 2 * rows_per_head)
    cc = lax.broadcasted_iota(jnp.int32, shape, 1)
    return _div_pow2(rr, rows_per_head) == _div_pow2(cc, cols_per_head)


def _state_terms(q, k, v_b, gc, reverse):
    total = gc[0:1] if reverse else gc[GLA_CHUNK - 1:GLA_CHUNK]
    q_dec = None if q is None else (q * jnp.exp(gc)).astype(BF16)
    k_end = (k * jnp.exp(total - gc)).astype(BF16)
    tn = (((0,), (0,)), ((), ()))
    upd = [lax.dot_general(k_end[:, p * PAIR_KEY:(p + 1) * PAIR_KEY], v_b[:, p * PAIR_VAL:(p + 1) * PAIR_VAL], tn,
                           preferred_element_type=F32) for p in range(GLA_HEADS // 2)]
    upd = jnp.where(_pair_mask(GLA_DK, GLA_DV, GLA_HEADS // 2), jnp.concatenate(upd, axis=0), 0.0)
    decay = jnp.exp(_as_column(total))
    decay = jnp.concatenate([decay] * (PAIR_VAL // LANES), axis=1)
    return q_dec, decay, upd


def _advance_state(q_dec, decay, upd, state):
    o_inter = None
    if q_dec is not None:
        state_b = state.astype(BF16)
        o_inter = jnp.concatenate(
            [_dot(q_dec[:, p * PAIR_KEY:(p + 1) * PAIR_KEY], state_b[p * PAIR_KEY:(p + 1) * PAIR_KEY, :])
             for p in range(GLA_HEADS // 2)], axis=1)
    return o_inter, state * decay + upd


def _gla_kernel(tt, nt, with_out, *refs):
    if with_out:
        (qf_ref, kf_ref, vf_ref, gfw_ref, qb_ref, kb_ref, vb_ref, gbw_ref, s0f_ref, s0b_ref,
         of_ref, ob_ref, sf_ref, sb_ref) = refs
    else:
        kf_ref, vf_ref, gfw_ref, kb_ref, vb_ref, gbw_ref, s0f_ref, s0b_ref, sf_ref, sb_ref = refs
    j = pl.program_id(1)

    @pl.when(j == 0)
    def _():
        sf_ref[...] = s0f_ref[...]
        sb_ref[...] = s0b_ref[...]

    chunk_slices = [slice(c * GLA_CHUNK, (c + 1) * GLA_CHUNK) for c in range(tt // GLA_CHUNK)]
    fwd, bwd = [], []
    for sl in chunk_slices:
        g = gfw_ref[sl, :]
        fwd.append(dict(sl=sl, k=kf_ref[sl, :].astype(F32), v=vf_ref[sl, :], g_f=g[:, 0:GLA_KEY], g_b=g[:, GLA_KEY:],
                        q=qf_ref[sl, :].astype(F32) if with_out else None))
        bwd.append(dict(sl=sl, k=kb_ref[sl, :].astype(F32), v=vb_ref[sl, :], g_b=gbw_ref[sl, :][:, GLA_KEY:],
                        q=qb_ref[sl, :].astype(F32) if with_out else None))
    for d in fwd:
        d["gc_f"] = _chunk_cumsum(d["g_f"], False)
        if with_out:
            d["gc_b"] = _chunk_cumsum(d["g_b"], True)
    for d in bwd:
        d["gc_b"] = _chunk_cumsum(d["g_b"], True)
    if with_out:
        for d in fwd:
            d["res_f"] = _intra_products(d["q"], d["k"], d["gc_f"], False)
            d["res_b"] = _intra_products(d["q"], d["k"], d["gc_b"], True)
    for d in fwd:
        d["terms"] = _state_terms(d["q"], d["k"], d["v"], d["gc_f"], False)
    for d in bwd:
        d["terms"] = _state_terms(d["q"], d["k"], d["v"], d["gc_b"], True)
    if with_out:
        for d in fwd:
            scores = (_assemble_scores(d["res_f"], False) + _assemble_scores(d["res_b"], True)).astype(BF16)
            o_intra = []
            for p in range(GLA_HEADS // 2):
                v_p = d["v"][:, p * PAIR_VAL:(p + 1) * PAIR_VAL]
                vbd = jnp.where(_pair_mask(GLA_CHUNK, GLA_DV), jnp.concatenate([v_p, v_p], axis=0),
                                jnp.zeros((), BF16))
                o_intra.append(_dot(scores[:, p * 2 * GLA_CHUNK:(p + 1) * 2 * GLA_CHUNK], vbd))
            d["o_intra"] = jnp.concatenate(o_intra, axis=1)

    state = sf_ref[...]
    for d in fwd:
        o_inter, state = _advance_state(*d["terms"], state)
        if with_out:
            of_ref[d["sl"], :] = (d["o_intra"] + o_inter).astype(BF16)
    sf_ref[...] = state

    state = sb_ref[...]
    for d in reversed(bwd):
        o_inter, state = _advance_state(*d["terms"], state)
        if with_out:
            ob_ref[d["sl"], :] = o_inter.astype(BF16)
    sb_ref[...] = state


def _gla(q, k, v, g, s0f, s0b):
    with_out = q is not None
    bsz, t, _ = k.shape
    tt = min(TOKEN_TILE, t)
    assert t % tt == 0 and tt % GLA_CHUNK == 0
    nt = t // tt
    fwd = lambda w: pl.BlockSpec((None, tt, w), lambda b, j: (b, j, 0))
    bwd = lambda w: pl.BlockSpec((None, tt, w), lambda b, j: (b, nt - 1 - j, 0))
    st = pl.BlockSpec((None, GLA_KEY, PAIR_VAL), lambda b, j: (b, 0, 0))
    st_shape = jax.ShapeDtypeStruct((bsz, GLA_KEY, PAIR_VAL), F32)
    if with_out:
        ins = [q, k, v, g, q, k, v, g, s0f, s0b]
        in_specs = [fwd(GLA_KEY), fwd(GLA_KEY), fwd(GLA_VAL), fwd(2 * GLA_KEY),
                    bwd(GLA_KEY), bwd(GLA_KEY), bwd(GLA_VAL), bwd(2 * GLA_KEY), st, st]
        out_specs = [fwd(GLA_VAL), bwd(GLA_VAL), st, st]
        o_shape = jax.ShapeDtypeStruct((bsz, t, GLA_VAL), BF16)
        out_shape = [o_shape, o_shape, st_shape, st_shape]
    else:
        ins = [k, v, g, k, v, g, s0f, s0b]
        in_specs = [fwd(GLA_KEY), fwd(GLA_VAL), fwd(2 * GLA_KEY),
                    bwd(GLA_KEY), bwd(GLA_VAL), bwd(2 * GLA_KEY), st, st]
        out_specs = [st, st]
        out_shape = [st_shape, st_shape]
    return pl.pallas_call(
        functools.partial(_gla_kernel, tt, nt, with_out),
        grid=(bsz, nt),
        in_specs=in_specs,
        out_specs=out_specs,
        out_shape=out_shape,
        compiler_params=_params(2),
        name="gla_latent" if with_out else "gla_ctx",
    )(*ins)


def _exact_bf16_parts(x):
    hi = x.astype(BF16).astype(F32)
    r = x - hi
    mid = r.astype(BF16).astype(F32)
    lo = (r - mid).astype(BF16).astype(F32)
    return hi, mid, lo


def _first_index(values, best):
    idx = jnp.full_like(best, float(len(values) - 1))
    for i in reversed(range(len(values) - 1)):
        idx = jnp.where(values[i] >= best, float(i), idx)
    return idx


def _pick(rows, idx):
    out = rows[-1]
    for i in reversed(range(len(rows) - 1)):
        out = jnp.where(idx == float(i), rows[i], out)
    return out


def _route(logit_t, tm):
    row = lambda r: logit_t[r:r + 1, :]
    groups = [row(i) for i in range(N_GROUPS)]
    top = functools.reduce(jnp.maximum, groups)
    eg = [jnp.exp(x - top) for x in groups]
    total = functools.reduce(lambda a, b: a + b, eg)
    pg = [e / total for e in eg]
    p_g = functools.reduce(jnp.maximum, pg)
    g_idx = _first_index(pg, p_g)
    sel = [_pick([row(N_GROUPS + EXPERTS_PER_GROUP * g + j) for g in range(N_GROUPS)], g_idx)
           for j in range(EXPERTS_PER_GROUP)]
    top = functools.reduce(jnp.maximum, sel)
    ee = [jnp.exp(x - top) for x in sel]
    total = functools.reduce(lambda a, b: a + b, ee)
    pe = [e / total for e in ee]
    p1 = functools.reduce(jnp.maximum, pe)
    l1 = _first_index(pe, p1)
    pe2 = [jnp.where(l1 == float(j), -1.0, pe[j]) for j in range(EXPERTS_PER_GROUP)]
    p2 = functools.reduce(jnp.maximum, pe2)
    l2 = _first_index(pe2, p2)
    den = p1 + p2
    w1 = p1 / den * p_g
    w2 = p2 / den * p_g
    lo = jnp.minimum(l1, l2)
    hi = jnp.maximum(l1, l2)
    pair = lo * (7.0 - lo) * 0.5 + (hi - lo - 1.0)
    cls = g_idx * PAIRS_PER_GROUP + pair
    w_lo = jnp.where(l1 < l2, w1, w2)
    w_hi = jnp.where(l1 < l2, w2, w1)
    cls_id = lax.broadcasted_iota(jnp.int32, (ROUTE_W, tm), 0).astype(F32)
    onehot = jnp.where(cls_id == cls, 1.0, 0.0)
    before = _dot(onehot.astype(BF16), _tri(tm, True, strict=True))
    count = jnp.sum(onehot, axis=1, keepdims=True)
    chunks = jnp.floor((count + (CHUNK_ROWS - 1.0)) * (1.0 / CHUNK_ROWS))
    chunks = jnp.where(cls_id[:, 0:1] == float(N_CLASSES),
                       LOCAL_CHUNKS - jnp.sum(chunks, axis=0, keepdims=True), chunks)
    first_chunk = _dot(_tri(ROUTE_W, False, strict=True),
                       jnp.broadcast_to(chunks, (ROUTE_W, LANES)).astype(BF16))[:, 0:1]
    pos_row = jnp.sum(onehot * (CHUNK_ROWS * first_chunk + before), axis=0, keepdims=True)
    return pos_row, w_lo, w_hi, chunks


def _slab_targets(slab_axis):
    shape = (LOCAL_SLAB_ROWS, 1) if slab_axis == 0 else (1, LOCAL_SLAB_ROWS)
    slab_row = lax.broadcasted_iota(jnp.int32, shape, slab_axis)
    sub = _mod_pow2(slab_row, SLAB_ROWS)
    token_row = (CHUNK_ROWS * _div_pow2(slab_row, SLAB_ROWS) + _div_pow2(sub, 2)).astype(F32)
    half = _mod_pow2(sub, 2)
    return [jnp.where(half == h, token_row, -1.0) for h in range(2)]


def _slab_sort_matrices(pos, targets):
    return [jnp.where(t == pos, 1.0, 0.0).astype(BF16) for t in targets]


def _mix_out_kernel(tm, x_ref, of_ref, ob_ref, sr_ref, ya_ref, mod_ref, lnp_ref, gn_ref, wo_ref, wr_ref, br_ref,
                    x1_ref, hxs_ref, pos_ref, chunks_ref):
    subs = [slice(s, s + MIX_SUB_TILE) for s in range(0, tm, MIX_SUB_TILE)]
    yb = []
    for rows in subs:
        o = of_ref[rows, :].astype(F32) + ob_ref[rows, :].astype(F32)
        sr = sr_ref[rows, :].astype(F32)
        heads = []
        for h in range(GLA_HEADS):
            sl = slice(h * GLA_DV, (h + 1) * GLA_DV)
            oh = o[:, sl]
            ms = jnp.mean(oh * oh, axis=-1, keepdims=True)
            heads.append((oh * lax.rsqrt(ms + RMS_EPS) * gn_ref[...] * sr[:, sl]).astype(BF16))
        yb.append(jnp.concatenate([ya_ref[rows, :]] + heads, axis=1))
    xn = [_layer_norm(x_ref[rows, :], lnp_ref[0:1, :], lnp_ref[1:2, :]) for rows in subs]
    y = [_dot(y_in, wo_ref[...]) for y_in in yb]
    h2_b = []
    for rows, xn_s, y_s in zip(subs, xn, y):
        x1 = _layer_norm(DEEPNORM_ALPHA * xn_s + mod_ref[0:1, :] * y_s, lnp_ref[2:3, :], lnp_ref[3:4, :])
        x1_ref[rows, :] = x1
        h2_b.append((x1 * mod_ref[1:2, :] + mod_ref[2:3, :]).astype(BF16))
    logit_t = [(_dot(h2_s, wr_ref[...]) + br_ref[...]).T for h2_s in h2_b]
    per_sort = SORT_TILE // MIX_SUB_TILE
    routes = []
    for s in range(tm // SORT_TILE):
        routes.append(_route(jnp.concatenate(logit_t[s * per_sort:(s + 1) * per_sort], axis=1), SORT_TILE))
    rec_id = lax.broadcasted_iota(jnp.int32, (ROUTE_W, SORT_TILE), 0)
    targets = _slab_targets(0)
    for s, (pos_row, w_lo, w_hi, chunks) in enumerate(routes):
        rec_t = jnp.zeros((ROUTE_W, SORT_TILE), F32)
        for i, part in enumerate(_exact_bf16_parts(w_lo) + _exact_bf16_parts(w_hi)):
            rec_t = jnp.where(rec_id == i, part, rec_t)
        rec_b = rec_t.T.astype(BF16)
        h2_s = jnp.concatenate(h2_b[s * per_sort:(s + 1) * per_sort], axis=0)
        sort_lo, sort_hi = _slab_sort_matrices(pos_row, targets)
        pay_lo = jnp.concatenate([h2_s[:, 0:HALF_W], rec_b], axis=1)
        pay_hi = jnp.concatenate([h2_s[:, HALF_W:], jnp.zeros((SORT_TILE, ROUTE_W), BF16)], axis=1)
        slabs = _dot(sort_lo, pay_lo) + _dot(sort_hi, pay_hi)
        for c in range(SLAB_IN_W // LANES):
            hxs_ref[c, s * LOCAL_SLAB_ROWS:(s + 1) * LOCAL_SLAB_ROWS, :] = slabs[:, c * LANES:(c + 1) * LANES]
        pos_ref[s * SORT_TILE:(s + 1) * SORT_TILE, :] = jnp.broadcast_to(pos_row, (ROUTE_W, SORT_TILE)).T
        chunks_ref[8 * s:8 * (s + 1), :] = jnp.broadcast_to(chunks, (ROUTE_W, LANES)).T[0:8, :]


def _mix_out(x, o_f, o_b, sr, ya, mod, lnp, gn, w_out_b, wr, br):
    bsz, t, _ = x.shape
    assert t % SORT_TILE == 0
    n_sort = max(n for n in (1, 2, 4) if n * SORT_TILE <= TOKEN_TILE and t % (n * SORT_TILE) == 0)
    tm = n_sort * SORT_TILE
    n_t = t // tm
    tok = lambda w: pl.BlockSpec((None, tm, w), lambda b, i: (b, i, 0))
    full = lambda a: pl.BlockSpec(a.shape, lambda b, i: (0,) * a.ndim)
    flat = lambda rows, w: pl.BlockSpec((rows, w), lambda b, i: (b * n_t + i, 0))
    return pl.pallas_call(
        functools.partial(_mix_out_kernel, tm),
        grid=(bsz, n_t),
        in_specs=[
            tok(D_MODEL), tok(GLA_VAL), tok(GLA_VAL), tok(GLA_VAL), tok(CONV_CH),
            pl.BlockSpec((None, 3, D_MODEL), lambda b, i: (b, 0, 0)),
            full(lnp), full(gn), full(w_out_b), full(wr), full(br),
        ],
        out_specs=[tok(D_MODEL),
                   pl.BlockSpec((SLAB_IN_W // LANES, n_sort * LOCAL_SLAB_ROWS, LANES),
                                lambda b, i: (0, b * n_t + i, 0)),
                   flat(tm, ROUTE_W), flat(n_sort * 8, ROUTE_W)],
        out_shape=[
            jax.ShapeDtypeStruct((bsz, t, D_MODEL), F32),
            jax.ShapeDtypeStruct((SLAB_IN_W // LANES, bsz * t // SORT_TILE * LOCAL_SLAB_ROWS, LANES), F32),
            jax.ShapeDtypeStruct((bsz * t, ROUTE_W), F32),
            jax.ShapeDtypeStruct((bsz * t // SORT_TILE * 8, ROUTE_W), F32),
        ],
        compiler_params=_params(2),
        name="mix_out",
    )(x, o_f, o_b, sr, ya, mod, lnp, gn, w_out_b, wr, br)


def _moe_kernel(n_chunks, nused_ref, lo_ref, hi_ref, live_ref, src_ref, dst_ref,
                hxs_hbm, w1l_ref, w3l_ref, w2l_ref, w1h_ref, w3h_ref, w2h_ref, out_hbm, gbuf, obuf, gsem, ssem):
    tile_rows = CHUNKS_PER_TILE * SLAB_ROWS
    i = pl.program_id(0)
    n_used = nused_ref[0]
    slot = lax.bitwise_and(i, 1)

    def slab(chunk):
        return pl.ds(pl.multiple_of(chunk * SLAB_ROWS, SLAB_ROWS), SLAB_ROWS)

    def gather_copy(tile, buf_slot, j):
        chunk = src_ref[tile * CHUNKS_PER_TILE + j]
        return pltpu.make_async_copy(hxs_hbm.at[:, slab(chunk), :], gbuf.at[buf_slot, :, slab(j), :],
                                     gsem.at[buf_slot])

    def scatter_copy(tile, buf_slot, j):
        chunk = dst_ref[tile * CHUNKS_PER_TILE + j]
        return pltpu.make_async_copy(obuf.at[buf_slot, :, slab(j), :], out_hbm.at[:, slab(chunk), :],
                                     ssem.at[buf_slot])

    def start_gather(tile, buf_slot):
        for j in range(CHUNKS_PER_TILE):
            gather_copy(tile, buf_slot, j).start(priority=j % 2)

    def wait_gather(buf_slot):
        pltpu.make_async_copy(hxs_hbm.at[:, pl.ds(0, tile_rows), :], gbuf.at[buf_slot], gsem.at[buf_slot]).wait()

    def wait_scatter(buf_slot):
        pltpu.make_async_copy(obuf.at[buf_slot], out_hbm.at[:, pl.ds(0, tile_rows), :], ssem.at[buf_slot]).wait()

    @pl.when(i == 0)
    def _():
        start_gather(0, 0)
        obuf[...] = jnp.zeros(obuf.shape, F32)
        for s in range(2):
            fill = pltpu.make_async_copy(
                obuf.at[s], out_hbm.at[:, pl.ds((n_chunks + s * CHUNKS_PER_TILE) * SLAB_ROWS, tile_rows), :],
                ssem.at[s])
            fill.start()
            fill.wait()

    @pl.when(i + 1 < n_used)
    def _():
        start_gather(i + 1, 1 - slot)

    @pl.when(i < n_used)
    def _():
        wait_gather(slot)

        @pl.when(i >= 2)
        def _():
            wait_scatter(slot)

        @pl.when(live_ref[i] != 0)
        def _():
            def lane_block(c, half):
                return jnp.concatenate(
                    [gbuf[slot, c, pl.ds(2 * r + half, CHUNKS_PER_TILE, stride=SLAB_ROWS), :]
                     for r in range(CHUNK_ROWS)], axis=0)

            n_blk = HALF_W // LANES
            xb = jnp.concatenate([lane_block(c, 0) for c in range(n_blk)]
                                 + [lane_block(c, 1) for c in range(n_blk)], axis=1).astype(BF16)
            rec = lane_block(n_blk, 0)
            w_lo = rec[:, 0:1] + rec[:, 1:2] + rec[:, 2:3]
            w_hi = rec[:, 3:4] + rec[:, 4:5] + rec[:, 5:6]

            gate = [_dot(xb, w1_ref[...]) for w1_ref in (w1l_ref, w1h_ref)]
            up = [_dot(xb, w3_ref[...]) for w3_ref in (w3l_ref, w3h_ref)]
            act = [(_silu(g) * u).astype(BF16) for g, u in zip(gate, up)]
            e_lo, e_hi = [_dot(a, w2_ref[...]) for a, w2_ref in zip(act, (w2l_ref, w2h_ref))]
            y = w_lo * e_lo + w_hi * e_hi
            for r in range(CHUNK_ROWS):
                rows = slice(r * CHUNKS_PER_TILE, (r + 1) * CHUNKS_PER_TILE)
                for half in range(2):
                    for c in range(n_blk):
                        col = half * HALF_W + c * LANES
                        obuf[slot, c, pl.ds(2 * r + half, CHUNKS_PER_TILE, stride=SLAB_ROWS), :] = (
                            y[rows, col:col + LANES])

        @pl.when(live_ref[i] == 0)
        def _():
            obuf[slot] = jnp.zeros(obuf.shape[1:], F32)

        for j in range(CHUNKS_PER_TILE):
            scatter_copy(i, slot, j).start(priority=j % 2)

        @pl.when(i == n_used - 1)
        def _():
            wait_scatter(slot)

            @pl.when(i >= 1)
            def _():
                wait_scatter(1 - slot)


def _moe(hxs, src, dst, n_used, tile_lo, tile_hi, tile_live, w1_b, w3_b, w2_b):
    n_chunks = hxs.shape[1] // SLAB_ROWS
    tile_rows = CHUNKS_PER_TILE * SLAB_ROWS
    n_steps = src.shape[0] // CHUNKS_PER_TILE
    wspec = lambda which, shape: pl.BlockSpec(
        (None,) + shape, (lambda i, nu, lo, hi, lv, s, d: (lo[i], 0, 0)) if which == 0 else
        (lambda i, nu, lo, hi, lv, s, d: (hi[i], 0, 0)))
    grid_spec = pltpu.PrefetchScalarGridSpec(
        num_scalar_prefetch=6,
        grid=(n_steps,),
        in_specs=[
            pl.BlockSpec(memory_space=pl.ANY),
            wspec(0, (D_MODEL, D_EXPERT)), wspec(0, (D_MODEL, D_EXPERT)), wspec(0, (D_EXPERT, D_MODEL)),
            wspec(1, (D_MODEL, D_EXPERT)), wspec(1, (D_MODEL, D_EXPERT)), wspec(1, (D_EXPERT, D_MODEL)),
        ],
        out_specs=pl.BlockSpec(memory_space=pl.ANY),
        scratch_shapes=[
            pltpu.VMEM((2, SLAB_IN_W // LANES, tile_rows, LANES), F32),
            pltpu.VMEM((2, SLAB_OUT_W // LANES, tile_rows, LANES), F32),
            pltpu.SemaphoreType.DMA((2,)),
            pltpu.SemaphoreType.DMA((2,)),
        ],
    )
    return pl.pallas_call(
        functools.partial(_moe_kernel, n_chunks),
        grid_spec=grid_spec,
        out_shape=jax.ShapeDtypeStruct((SLAB_OUT_W // LANES, (n_chunks + 2 * CHUNKS_PER_TILE) * SLAB_ROWS, LANES), F32),
        compiler_params=_params(1),
        name="moe",
    )(n_used, tile_lo, tile_hi, tile_live, src, dst, hxs, w1_b, w3_b, w2_b, w1_b, w3_b, w2_b)


def _final_kernel(n_sort, x1_ref, moe_ref, pos_ref, mod_ref, lnp_ref, o_ref):
    moe = []
    targets = _slab_targets(1)
    for s in range(n_sort):
        slab_rows = slice(s * LOCAL_SLAB_ROWS, (s + 1) * LOCAL_SLAB_ROWS)
        moe_b = jnp.concatenate([moe_ref[c, slab_rows, :] for c in range(SLAB_OUT_W // LANES)], axis=1).astype(BF16)
        sort_lo, sort_hi = _slab_sort_matrices(pos_ref[s * SORT_TILE:(s + 1) * SORT_TILE, 0:1], targets)
        for r in range(0, SORT_TILE, MIX_SUB_TILE):
            rows = slice(r, r + MIX_SUB_TILE)
            moe.append((s * SORT_TILE + r,
                        jnp.concatenate([_dot(sort_lo[rows, :], moe_b), _dot(sort_hi[rows, :], moe_b)], axis=1)))
    for start, moe_s in moe:
        rows = slice(start, start + MIX_SUB_TILE)
        o_ref[rows, :] = _layer_norm(DEEPNORM_ALPHA * x1_ref[rows, :] + mod_ref[...] * moe_s,
                                     lnp_ref[0:1, :], lnp_ref[1:2, :])


def _final(x1, moe, pos, g2, lnp):
    bsz, t, _ = x1.shape
    n_sort = max(n for n in (1, 2, 4) if n * SORT_TILE <= TOKEN_TILE and t % (n * SORT_TILE) == 0)
    tm = n_sort * SORT_TILE
    n_t = t // tm
    flat = lambda rows, w: pl.BlockSpec((rows, w), lambda b, i: (b * n_t + i, 0))
    return pl.pallas_call(
        functools.partial(_final_kernel, n_sort),
        grid=(bsz, n_t),
        in_specs=[
            pl.BlockSpec((None, tm, D_MODEL), lambda b, i: (b, i, 0)),
            pl.BlockSpec((SLAB_OUT_W // LANES, n_sort * LOCAL_SLAB_ROWS, LANES), lambda b, i: (0, b * n_t + i, 0)),
            flat(tm, ROUTE_W),
            pl.BlockSpec((None, 1, D_MODEL), lambda b, i: (b, 0, 0)),
            pl.BlockSpec(lnp.shape, lambda b, i: (0, 0)),
        ],
        out_specs=pl.BlockSpec((None, tm, D_MODEL), lambda b, i: (b, i, 0)),
        out_shape=jax.ShapeDtypeStruct((bsz, t, D_MODEL), F32),
        compiler_params=_params(2),
        name="final",
    )(x1, moe, pos, g2, lnp)


def _pair_tables():
    lo, hi = [], []
    for g in range(N_GROUPS):
        for a in range(EXPERTS_PER_GROUP):
            for b in range(a + 1, EXPERTS_PER_GROUP):
                lo.append(g * EXPERTS_PER_GROUP + a)
                hi.append(g * EXPERTS_PER_GROUP + b)
    return jnp.array(lo, jnp.int32), jnp.array(hi, jnp.int32)


def _moe_plan(chunks, n_sort_tiles):
    n_cls = N_CLASSES + 1
    hp = lax.Precision.HIGHEST
    m = chunks.reshape(n_sort_tiles, 8, ROUTE_W)[:, 0, :n_cls].astype(jnp.int32)
    a_end = jnp.cumsum(m, axis=0)
    a_start = a_end - m
    per_cls = a_end[-1]
    padded = (per_cls + CHUNKS_PER_TILE - 1) // CHUNKS_PER_TILE * CHUNKS_PER_TILE
    g_end = jnp.cumsum(padded)
    g_start = g_end - padded
    local_off = jnp.cumsum(m, axis=1) - m
    seg = jnp.arange(n_sort_tiles, dtype=jnp.int32)[:, None] * LOCAL_CHUNKS + local_off - a_start
    n_steps = -(-(n_sort_tiles * LOCAL_CHUNKS) // CHUNKS_PER_TILE) + n_cls
    p = jnp.arange(n_steps * CHUNKS_PER_TILE, dtype=jnp.int32)
    cls_p = jnp.minimum(jnp.sum((g_end[None, :] <= p[:, None]).astype(jnp.int32), axis=1), n_cls - 1)
    onehot = (cls_p[:, None] == jnp.arange(n_cls, dtype=jnp.int32)[None, :]).astype(F32)
    pick = lambda tab: jnp.dot(onehot, tab.astype(F32), precision=hp)
    u = p - pick(g_start[:, None])[:, 0].astype(jnp.int32)
    valid = u < pick(per_cls[:, None])[:, 0].astype(jnp.int32)
    a_end_p = pick(a_end.T).astype(jnp.int32)
    seg_p = pick(seg.T).astype(jnp.int32)
    tile_p = jnp.sum((a_end_p <= u[:, None]).astype(jnp.int32), axis=1)
    hit = jnp.arange(n_sort_tiles, dtype=jnp.int32)[None, :] == tile_p[:, None]
    src = jnp.sum(jnp.where(hit, seg_p, 0), axis=1) + u
    pad_dst = n_sort_tiles * LOCAL_CHUNKS + (p // CHUNKS_PER_TILE) % 2 * CHUNKS_PER_TILE + p % CHUNKS_PER_TILE
    dst = jnp.where(valid, src, pad_dst).astype(jnp.int32)
    src = jnp.where(valid, src, 0).astype(jnp.int32)
    n_used = g_end[-1:] // CHUNKS_PER_TILE
    step = jnp.arange(n_steps, dtype=jnp.int32)
    tile_cls = jnp.sum((g_end[None, :] // CHUNKS_PER_TILE <= step[:, None]).astype(jnp.int32), axis=1)
    live = ((tile_cls < N_CLASSES) & (step < n_used[0])).astype(jnp.int32)
    pair_lo, pair_hi = _pair_tables()
    pair_oh = (jnp.minimum(tile_cls, N_CLASSES - 1)[:, None] == jnp.arange(N_CLASSES)[None, :]).astype(jnp.int32)
    tile_lo = jnp.sum(pair_oh * pair_lo[None, :], axis=1).astype(jnp.int32)
    tile_hi = jnp.sum(pair_oh * pair_hi[None, :], axis=1).astype(jnp.int32)
    return src, dst, n_used.astype(jnp.int32), tile_lo, tile_hi, live


def kernel(x, c, ctx, c_ctx, ln_in_g, ln_in_b, w_ada, b_ada, w_in, conv_w, conv_b, gate_w2_fwd, gate_b_fwd,
           gate_w2_bwd, gate_b_bwd, gla_norm_g, w_out, ln1_g, ln1_b, router_group_w, router_group_b,
           router_expert_w, router_expert_b, expert_w1, expert_w3, expert_w2, ln2_g, ln2_b):
    bsz, t, _ = x.shape
    n_tok = bsz * t
    l = 0
    rows = -(-(bsz + 1) // 8) * 8
    cond = jnp.zeros((rows, D_MODEL), F32).at[:bsz].set(c).at[bsz].set(c_ctx)
    ada = _ada(cond, w_ada[l], b_ada[l][None, :])
    sh1, sc1, g1, sh2, sc2, g2 = [ada[:, i * D_MODEL:(i + 1) * D_MODEL] for i in range(6)]

    w_in_b = w_in[l].astype(BF16)
    lnp_in = jnp.stack([ln_in_g, ln_in_b])
    zero = jnp.zeros((GLA_GATE_RANK, GLA_KEY), F32)
    w2cat = jnp.concatenate([jnp.concatenate([gate_w2_fwd[l], zero], axis=1),
                             jnp.concatenate([zero, gate_w2_bwd[l]], axis=1)], axis=0).astype(BF16)
    gbias = jnp.concatenate([gate_b_fwd[l], gate_b_bwd[l]])[None, :]

    n_ctx = ctx.shape[1]
    mod_ctx = jnp.stack([1.0 + sc1[bsz], sh1[bsz]])[None]
    k_c, v_c, g_c = [a.reshape(bsz, n_ctx, -1) for a in _proj(
        ctx.reshape(1, bsz * n_ctx, D_MODEL), mod_ctx, lnp_in, w_in_b, conv_w[l], conv_b[l][None, :], w2cat, gbias,
        False)]
    zero_state = jnp.zeros((bsz, GLA_KEY, PAIR_VAL), F32)
    s_f, s_b = _gla(None, k_c, v_c, g_c, zero_state, zero_state)

    mod1 = jnp.stack([1.0 + sc1[:bsz], sh1[:bsz]], axis=1)
    ya, q, k, v, sr, g = _proj(x, mod1, lnp_in, w_in_b, conv_w[l], conv_b[l][None, :], w2cat, gbias, True)
    o_f, o_b, _, _ = _gla(q, k, v, g, s_f, s_b)

    mod2 = jnp.stack([g1[:bsz], 1.0 + sc2[:bsz], sh2[:bsz]], axis=1)
    lnp1 = jnp.stack([ln_in_g, ln_in_b, ln1_g[l], ln1_b[l]])
    wr = jnp.zeros((D_MODEL, ROUTE_W), F32)
    wr = wr.at[:, :N_GROUPS].set(router_group_w[l]).at[:, N_GROUPS:N_GROUPS + N_EXPERTS].set(router_expert_w[l])
    br = jnp.zeros((1, ROUTE_W), F32)
    br = br.at[0, :N_GROUPS].set(router_group_b[l]).at[0, N_GROUPS:N_GROUPS + N_EXPERTS].set(router_expert_b[l])
    x1, hxs, pos, chunks = _mix_out(x, o_f, o_b, sr, ya, mod2, lnp1, gla_norm_g[l][None, :],
                                    w_out[l].astype(BF16), wr.astype(BF16), br)

    src, dst, n_used, tile_lo, tile_hi, live = _moe_plan(chunks, n_tok // SORT_TILE)
    moe = _moe(hxs, src, dst, n_used, tile_lo, tile_hi, live,
               expert_w1[l].astype(BF16), expert_w3[l].astype(BF16), expert_w2[l].astype(BF16))

    return _final(x1, moe, pos, g2[:bsz][:, None, :], jnp.stack([ln2_g[l], ln2_b[l]]))
```

```python
import functools

import jax
import jax.numpy as jnp
from jax import lax
from jax.experimental import pallas as pl
from jax.experimental.pallas import tpu as pltpu

F32 = jnp.float32
BF16 = jnp.bfloat16

D_MODEL = 1024
GRID_W = 64
CONV_CH = 512
GLA_HEADS = 4
GLA_DK = 64
GLA_DV = 128
GLA_KEY = GLA_HEADS * GLA_DK
GLA_VAL = GLA_HEADS * GLA_DV
PAIR_KEY = 2 * GLA_DK
PAIR_VAL = 2 * GLA_DV
GLA_GATE_RANK = 16
GLA_TAU = 16.0
OFF_AB = 0
OFF_AC = OFF_AB + CONV_CH
OFF_AX = OFF_AC + CONV_CH
OFF_Q = OFF_AX + CONV_CH
OFF_K = OFF_Q + GLA_KEY
OFF_V = OFF_K + GLA_KEY
OFF_R = OFF_V + GLA_VAL
OFF_GF = OFF_R + GLA_VAL
D_PROJ = OFF_GF + 2 * GLA_GATE_RANK
N_GROUPS = 4
EXPERTS_PER_GROUP = 4
N_EXPERTS = N_GROUPS * EXPERTS_PER_GROUP
D_EXPERT = 512
PAIRS_PER_GROUP = 6
N_CLASSES = N_GROUPS * PAIRS_PER_GROUP
LN_EPS = 1e-5
RMS_EPS = 1e-6
DEPTH = 1
DEEPNORM_ALPHA = (2.0 * DEPTH) ** 0.25

LANES = 128
GLA_CHUNK = 64
GLA_SUB = 16
N_SUB = GLA_CHUNK // GLA_SUB
ROUTE_W = LANES
HALF_W = D_MODEL // 2
SLAB_IN_W = HALF_W + ROUTE_W
SLAB_OUT_W = HALF_W
ADA_COL_TILE = 1024
TOKEN_TILE = 1024
SORT_TILE = 256
MIX_SUB_TILE = 128
MOE_TILE = 256
CHUNK_ROWS = 4
SLAB_ROWS = 2 * CHUNK_ROWS
LOCAL_CHUNKS = -(-(SORT_TILE + N_CLASSES * (CHUNK_ROWS - 1)) // CHUNK_ROWS)
LOCAL_SLAB_ROWS = LOCAL_CHUNKS * SLAB_ROWS
CHUNKS_PER_TILE = MOE_TILE // CHUNK_ROWS
VMEM_LIMIT = 56 * 1024 * 1024


def _params(n_axes, vmem=VMEM_LIMIT):
    return pltpu.CompilerParams(dimension_semantics=("arbitrary",) * n_axes, vmem_limit_bytes=vmem)


def _dot(a, b):
    return jnp.dot(a, b, preferred_element_type=F32)


def _div_pow2(x, d):
    assert d & (d - 1) == 0
    return lax.shift_right_logical(x, jnp.int32(d.bit_length() - 1))


def _mod_pow2(x, d):
    assert d & (d - 1) == 0
    return lax.bitwise_and(x, jnp.int32(d - 1))


def _split2(x):
    hi = x.astype(BF16)
    lo = (x - hi.astype(F32)).astype(BF16)
    return hi, lo


def _dot3(a, b):
    ah, al = _split2(a)
    bh, bl = _split2(b)
    return _dot(ah, bh) + _dot(ah, bl) + _dot(al, bh)


def _silu(x):
    return x * (0.5 * jnp.tanh(0.5 * x) + 0.5)


def _layer_norm(x, g, b):
    mu = jnp.mean(x, axis=-1, keepdims=True)
    xc = x - mu
    var = jnp.mean(xc * xc, axis=-1, keepdims=True)
    return xc * lax.rsqrt(var + LN_EPS) * g + b


def _ada_kernel(c_ref, w_ref, b_ref, o_ref):
    o_ref[...] = _dot3(_silu(c_ref[...]), w_ref[...]) + b_ref[...]


def _ada(cond, w_ada, b_ada):
    rows = cond.shape[0]
    n_out = w_ada.shape[1]
    tn = ADA_COL_TILE
    return pl.pallas_call(
        _ada_kernel,
        grid=(n_out // tn,),
        in_specs=[
            pl.BlockSpec((rows, D_MODEL), lambda j: (0, 0)),
            pl.BlockSpec((D_MODEL, tn), lambda j: (0, j)),
            pl.BlockSpec((1, tn), lambda j: (0, j)),
        ],
        out_specs=pl.BlockSpec((rows, tn), lambda j: (0, j)),
        out_shape=jax.ShapeDtypeStruct((rows, n_out), F32),
        compiler_params=_params(1),
        name="ada",
    )(cond, w_ada, b_ada)


def _log_sigmoid(z):
    return jnp.minimum(z, 0.0) - jnp.log(1.0 + jnp.exp(-jnp.abs(z)))


def _proj_kernel(latent, tm, x_ref, mod_ref, lnp_ref, w_ref, cw_ref, cb_ref, w2_ref, gbias_ref, *out_refs):
    x = x_ref[...]
    xn = _layer_norm(x, lnp_ref[0:1, :], lnp_ref[1:2, :])
    h = xn * mod_ref[0:1, :] + mod_ref[1:2, :]
    hb = h.astype(BF16)
    if latent:
        ya_ref, q_ref, k_ref, v_ref, sr_ref, g_ref = out_refs
        p = _dot(hb, w_ref[:, OFF_AB:OFF_Q])
        a_b = p[:, 0:CONV_CH]
        u = p[:, CONV_CH:2 * CONV_CH] * p[:, 2 * CONV_CH:3 * CONV_CH]
        pos = _mod_pow2(lax.broadcasted_iota(jnp.int32, (tm, 1), 0), GRID_W)
        u_prev = jnp.where(pos == 0, 0.0, pltpu.roll(u, 1, 0))
        u_next = jnp.where(pos == GRID_W - 1, 0.0, pltpu.roll(u, tm - 1, 0))
        conv = u_prev * cw_ref[0:1, :] + u * cw_ref[1:2, :] + u_next * cw_ref[2:3, :] + cb_ref[...]
        ya_ref[...] = (a_b * conv).astype(BF16)
        qk = _dot(hb, w_ref[:, OFF_Q:OFF_V])
        q_ref[...] = (qk[:, 0:GLA_KEY] * (GLA_DK ** -0.5)).astype(BF16)
        k_ref[...] = qk[:, GLA_KEY:].astype(BF16)
        r = _dot(hb, w_ref[:, OFF_R:OFF_GF])
        sr_ref[...] = _silu(r).astype(BF16)
    else:
        k_ref, v_ref, g_ref = out_refs
        k_ref[...] = _dot(hb, w_ref[:, OFF_K:OFF_V]).astype(BF16)
    v_ref[...] = _dot(hb, w_ref[:, OFF_V:OFF_R]).astype(BF16)
    low = _dot(hb, w_ref[:, OFF_GF:D_PROJ])
    z = _dot(low.astype(BF16), w2_ref[...]) + gbias_ref[...]
    g_ref[...] = _log_sigmoid(z) * (1.0 / GLA_TAU)


def _proj(x, mod, lnp, w_in_b, conv_w, conv_b, w2cat, gbias, latent):
    bsz, t, _ = x.shape
    tm = min(TOKEN_TILE, t)
    assert t % tm == 0 and tm % GRID_W == 0
    tok = lambda w: pl.BlockSpec((None, tm, w), lambda b, i: (b, i, 0))
    full = lambda a: pl.BlockSpec(a.shape, lambda b, i: (0,) * a.ndim)
    widths = ([(CONV_CH, BF16), (GLA_KEY, BF16)] if latent else []) + [(GLA_KEY, BF16), (GLA_VAL, BF16)]
    widths += ([(GLA_VAL, BF16)] if latent else []) + [(2 * GLA_KEY, F32)]
    return pl.pallas_call(
        functools.partial(_proj_kernel, latent, tm),
        grid=(bsz, t // tm),
        in_specs=[
            tok(D_MODEL),
            pl.BlockSpec((None, 2, D_MODEL), lambda b, i: (b, 0, 0)),
            full(lnp), full(w_in_b), full(conv_w), full(conv_b), full(w2cat), full(gbias),
        ],
        out_specs=[tok(w) for w, _ in widths],
        out_shape=[jax.ShapeDtypeStruct((bsz, t, w), dt) for w, dt in widths],
        compiler_params=_params(2),
        name="proj_latent" if latent else "proj_ctx",
    )(x, mod, lnp, w_in_b, conv_w, conv_b, w2cat, gbias)


def _tri(n, reverse, strict=False):
    i = lax.broadcasted_iota(jnp.int32, (n, n), 0)
    j = lax.broadcasted_iota(jnp.int32, (n, n), 1)
    if strict:
        m = (j > i) if reverse else (j < i)
    else:
        m = (j >= i) if reverse else (j <= i)
    return jnp.where(m, 1.0, 0.0).astype(BF16)


def _chunk_cumsum(g, reverse):
    tri = _tri(GLA_CHUNK, reverse)
    g_hi, g_lo = _split2(g)
    return _dot(tri, g_hi) + _dot(tri, g_lo)


def _as_column(row):
    return jnp.broadcast_to(row, (LANES, row.shape[1])).T


def _sub_anchors(gc, reverse):
    zero = jnp.zeros((1, GLA_KEY), F32)
    if reverse:
        return [gc[GLA_SUB * (a + 1):GLA_SUB * (a + 1) + 1] for a in range(N_SUB - 1)] + [zero]
    return [zero] + [gc[GLA_SUB * a - 1:GLA_SUB * a] for a in range(1, N_SUB)]


def _score_pairs(reverse):
    return [(a, b) for a in range(N_SUB) for b in range(N_SUB) if (b >= a if reverse else b <= a)]


def _intra_products(q, k, gc, reverse):
    r = _sub_anchors(gc, reverse)
    anchor = jnp.concatenate([jnp.broadcast_to(ra, (GLA_SUB, GLA_KEY)) for ra in r], axis=0)
    gcb = gc - anchor
    qt = q * jnp.exp(gcb)
    kt = k * jnp.exp(-gcb)
    rows = []
    for a, b in _score_pairs(reverse):
        qa = qt[GLA_SUB * a:GLA_SUB * (a + 1)]
        if a != b:
            qa = qa * jnp.exp(r[a] - r[b])
        rows.append(qa)
    qp = jnp.concatenate(rows, axis=0).astype(BF16)
    width = GLA_HEADS * GLA_CHUNK
    rr = lax.broadcasted_iota(jnp.int32, (width, GLA_KEY), 0)
    cc = lax.broadcasted_iota(jnp.int32, (width, GLA_KEY), 1)
    kbd = jnp.where(_div_pow2(rr, GLA_CHUNK) == _div_pow2(cc, GLA_DK),
                    jnp.concatenate([kt] * GLA_HEADS, axis=0), 0.0)
    return lax.dot_general(qp, kbd.astype(BF16), (((1,), (1,)), ((), ())), preferred_element_type=F32)


def _assemble_scores(res, reverse):
    pairs = _score_pairs(reverse)
    width = GLA_HEADS * GLA_CHUNK
    col = _mod_pow2(lax.broadcasted_iota(jnp.int32, (GLA_SUB, width), 1), GLA_CHUNK)
    col_blk = _div_pow2(col, GLA_SUB)
    col_pos = _mod_pow2(col, GLA_SUB)
    row_pos = lax.broadcasted_iota(jnp.int32, (GLA_SUB, width), 0)
    causal = (col_pos >= row_pos) if reverse else (col_pos <= row_pos)
    blocks = []
    for a in range(N_SUB):
        acc = jnp.zeros((GLA_SUB, width), F32)
        for idx, (pa, pb) in enumerate(pairs):
            if pa != a:
                continue
            keep = col_blk == pb
            if pa == pb:
                keep = keep & causal
            acc = acc + jnp.where(keep, res[GLA_SUB * idx:GLA_SUB * (idx + 1)], 0.0)
        blocks.append(acc)
    return jnp.concatenate(blocks, axis=0)


def _pair_mask(rows_per_head, cols_per_head, n_row_pairs=1):
    shape = (n_row_pairs * 2 * rows_per_head, 2 * cols_per_head)
    rr = _mod_pow2(lax.broadcasted_iota(jnp.int32, shape, 0), 2 * rows_per_head)
    cc = lax.broadcasted_iota(jnp.int32, shape, 1)
    return _div_pow2(rr, rows_per_head) == _div_pow2(cc, cols_per_head)


def _state_terms(q, k, v_b, gc, reverse):
    total = gc[0:1] if reverse else gc[GLA_CHUNK - 1:GLA_CHUNK]
    q_dec = None if q is None else (q * jnp.exp(gc)).astype(BF16)
    k_end = (k * jnp.exp(total - gc)).astype(BF16)
    tn = (((0,), (0,)), ((), ()))
    upd = [lax.dot_general(k_end[:, p * PAIR_KEY:(p + 1) * PAIR_KEY], v_b[:, p * PAIR_VAL:(p + 1) * PAIR_VAL], tn,
                           preferred_element_type=F32) for p in range(GLA_HEADS // 2)]
    upd = jnp.where(_pair_mask(GLA_DK, GLA_DV, GLA_HEADS // 2), jnp.concatenate(upd, axis=0), 0.0)
    decay = jnp.exp(_as_column(total))
    decay = jnp.concatenate([decay] * (PAIR_VAL // LANES), axis=1)
    return q_dec, decay, upd


def _advance_state(q_dec, decay, upd, state):
    o_inter = None
    if q_dec is not None:
        state_b = state.astype(BF16)
        o_inter = jnp.concatenate(
            [_dot(q_dec[:, p * PAIR_KEY:(p + 1) * PAIR_KEY], state_b[p * PAIR_KEY:(p + 1) * PAIR_KEY, :])
             for p in range(GLA_HEADS // 2)], axis=1)
    return o_inter, state * decay + upd


def _gla_kernel(tt, nt, with_out, *refs):
    if with_out:
        (qf_ref, kf_ref, vf_ref, gfw_ref, qb_ref, kb_ref, vb_ref, gbw_ref, s0f_ref, s0b_ref,
         of_ref, ob_ref, sf_ref, sb_ref) = refs
    else:
        kf_ref, vf_ref, gfw_ref, kb_ref, vb_ref, gbw_ref, s0f_ref, s0b_ref, sf_ref, sb_ref = refs
    j = pl.program_id(1)

    @pl.when(j == 0)
    def _():
        sf_ref[...] = s0f_ref[...]
        sb_ref[...] = s0b_ref[...]

    chunk_slices = [slice(c * GLA_CHUNK, (c + 1) * GLA_CHUNK) for c in range(tt // GLA_CHUNK)]
    fwd, bwd = [], []
    for sl in chunk_slices:
        g = gfw_ref[sl, :]
        fwd.append(dict(sl=sl, k=kf_ref[sl, :].astype(F32), v=vf_ref[sl, :], g_f=g[:, 0:GLA_KEY], g_b=g[:, GLA_KEY:],
                        q=qf_ref[sl, :].astype(F32) if with_out else None))
        bwd.append(dict(sl=sl, k=kb_ref[sl, :].astype(F32), v=vb_ref[sl, :], g_b=gbw_ref[sl, :][:, GLA_KEY:],
                        q=qb_ref[sl, :].astype(F32) if with_out else None))
    for d in fwd:
        d["gc_f"] = _chunk_cumsum(d["g_f"], False)
        if with_out:
            d["gc_b"] = _chunk_cumsum(d["g_b"], True)
    for d in bwd:
        d["gc_b"] = _chunk_cumsum(d["g_b"], True)
    if with_out:
        for d in fwd:
            d["res_f"] = _intra_products(d["q"], d["k"], d["gc_f"], False)
            d["res_b"] = _intra_products(d["q"], d["k"], d["gc_b"], True)
    for d in fwd:
        d["terms"] = _state_terms(d["q"], d["k"], d["v"], d["gc_f"], False)
    for d in bwd:
        d["terms"] = _state_terms(d["q"], d["k"], d["v"], d["gc_b"], True)
    if with_out:
        for d in fwd:
            scores = (_assemble_scores(d["res_f"], False) + _assemble_scores(d["res_b"], True)).astype(BF16)
            o_intra = []
            for p in range(GLA_HEADS // 2):
                v_p = d["v"][:, p * PAIR_VAL:(p + 1) * PAIR_VAL]
                vbd = jnp.where(_pair_mask(GLA_CHUNK, GLA_DV), jnp.concatenate([v_p, v_p], axis=0),
                                jnp.zeros((), BF16))
                o_intra.append(_dot(scores[:, p * 2 * GLA_CHUNK:(p + 1) * 2 * GLA_CHUNK], vbd))
            d["o_intra"] = jnp.concatenate(o_intra, axis=1)

    state = sf_ref[...]
    for d in fwd:
        o_inter, state = _advance_state(*d["terms"], state)
        if with_out:
            of_ref[d["sl"], :] = (d["o_intra"] + o_inter).astype(BF16)
    sf_ref[...] = state

    state = sb_ref[...]
    for d in reversed(bwd):
        o_inter, state = _advance_state(*d["terms"], state)
        if with_out:
            ob_ref[d["sl"], :] = o_inter.astype(BF16)
    sb_ref[...] = state


def _gla(q, k, v, g, s0f, s0b):
    with_out = q is not None
    bsz, t, _ = k.shape
    tt = min(TOKEN_TILE, t)
    assert t % tt == 0 and tt % GLA_CHUNK == 0
    nt = t // tt
    fwd = lambda w: pl.BlockSpec((None, tt, w), lambda b, j: (b, j, 0))
    bwd = lambda w: pl.BlockSpec((None, tt, w), lambda b, j: (b, nt - 1 - j, 0))
    st = pl.BlockSpec((None, GLA_KEY, PAIR_VAL), lambda b, j: (b, 0, 0))
    st_shape = jax.ShapeDtypeStruct((bsz, GLA_KEY, PAIR_VAL), F32)
    if with_out:
        ins = [q, k, v, g, q, k, v, g, s0f, s0b]
        in_specs = [fwd(GLA_KEY), fwd(GLA_KEY), fwd(GLA_VAL), fwd(2 * GLA_KEY),
                    bwd(GLA_KEY), bwd(GLA_KEY), bwd(GLA_VAL), bwd(2 * GLA_KEY), st, st]
        out_specs = [fwd(GLA_VAL), bwd(GLA_VAL), st, st]
        o_shape = jax.ShapeDtypeStruct((bsz, t, GLA_VAL), BF16)
        out_shape = [o_shape, o_shape, st_shape, st_shape]
    else:
        ins = [k, v, g, k, v, g, s0f, s0b]
        in_specs = [fwd(GLA_KEY), fwd(GLA_VAL), fwd(2 * GLA_KEY),
                    bwd(GLA_KEY), bwd(GLA_VAL), bwd(2 * GLA_KEY), st, st]
        out_specs = [st, st]
        out_shape = [st_shape, st_shape]
    return pl.pallas_call(
        functools.partial(_gla_kernel, tt, nt, with_out),
        grid=(bsz, nt),
        in_specs=in_specs,
        out_specs=out_specs,
        out_shape=out_shape,
        compiler_params=_params(2),
        name="gla_latent" if with_out else "gla_ctx",
    )(*ins)


def _exact_bf16_parts(x):
    hi = x.astype(BF16).astype(F32)
    r = x - hi
    mid = r.astype(BF16).astype(F32)
    lo = (r - mid).astype(BF16).astype(F32)
    return hi, mid, lo


def _first_index(values, best):
    idx = jnp.full_like(best, float(len(values) - 1))
    for i in reversed(range(len(values) - 1)):
        idx = jnp.where(values[i] >= best, float(i), idx)
    return idx


def _pick(rows, idx):
    out = rows[-1]
    for i in reversed(range(len(rows) - 1)):
        out = jnp.where(idx == float(i), rows[i], out)
    return out


def _route(logit_t, tm):
    row = lambda r: logit_t[r:r + 1, :]
    groups = [row(i) for i in range(N_GROUPS)]
    top = functools.reduce(jnp.maximum, groups)
    eg = [jnp.exp(x - top) for x in groups]
    total = functools.reduce(lambda a, b: a + b, eg)
    pg = [e / total for e in eg]
    p_g = functools.reduce(jnp.maximum, pg)
    g_idx = _first_index(pg, p_g)
    sel = [_pick([row(N_GROUPS + EXPERTS_PER_GROUP * g + j) for g in range(N_GROUPS)], g_idx)
           for j in range(EXPERTS_PER_GROUP)]
    top = functools.reduce(jnp.maximum, sel)
    ee = [jnp.exp(x - top) for x in sel]
    total = functools.reduce(lambda a, b: a + b, ee)
    pe = [e / total for e in ee]
    p1 = functools.reduce(jnp.maximum, pe)
    l1 = _first_index(pe, p1)
    pe2 = [jnp.where(l1 == float(j), -1.0, pe[j]) for j in range(EXPERTS_PER_GROUP)]
    p2 = functools.reduce(jnp.maximum, pe2)
    l2 = _first_index(pe2, p2)
    den = p1 + p2
    w1 = p1 / den * p_g
    w2 = p2 / den * p_g
    lo = jnp.minimum(l1, l2)
    hi = jnp.maximum(l1, l2)
    pair = lo * (7.0 - lo) * 0.5 + (hi - lo - 1.0)
    cls = g_idx * PAIRS_PER_GROUP + pair
    w_lo = jnp.where(l1 < l2, w1, w2)
    w_hi = jnp.where(l1 < l2, w2, w1)
    cls_id = lax.broadcasted_iota(jnp.int32, (ROUTE_W, tm), 0).astype(F32)
    onehot = jnp.where(cls_id == cls, 1.0, 0.0)
    before = _dot(onehot.astype(BF16), _tri(tm, True, strict=True))
    count = jnp.sum(onehot, axis=1, keepdims=True)
    chunks = jnp.floor((count + (CHUNK_ROWS - 1.0)) * (1.0 / CHUNK_ROWS))
    chunks = jnp.where(cls_id[:, 0:1] == float(N_CLASSES),
                       LOCAL_CHUNKS - jnp.sum(chunks, axis=0, keepdims=True), chunks)
    first_chunk = _dot(_tri(ROUTE_W, False, strict=True),
                       jnp.broadcast_to(chunks, (ROUTE_W, LANES)).astype(BF16))[:, 0:1]
    pos_row = jnp.sum(onehot * (CHUNK_ROWS * first_chunk + before), axis=0, keepdims=True)
    return pos_row, w_lo, w_hi, chunks


def _slab_targets(slab_axis):
    shape = (LOCAL_SLAB_ROWS, 1) if slab_axis == 0 else (1, LOCAL_SLAB_ROWS)
    slab_row = lax.broadcasted_iota(jnp.int32, shape, slab_axis)
    sub = _mod_pow2(slab_row, SLAB_ROWS)
    token_row = (CHUNK_ROWS * _div_pow2(slab_row, SLAB_ROWS) + _div_pow2(sub, 2)).astype(F32)
    half = _mod_pow2(sub, 2)
    return [jnp.where(half == h, token_row, -1.0) for h in range(2)]


def _slab_sort_matrices(pos, targets):
    return [jnp.where(t == pos, 1.0, 0.0).astype(BF16) for t in targets]


def _mix_out_kernel(tm, x_ref, of_ref, ob_ref, sr_ref, ya_ref, mod_ref, lnp_ref, gn_ref, wo_ref, wr_ref, br_ref,
                    x1_ref, hxs_ref, pos_ref, chunks_ref):
    subs = [slice(s, s + MIX_SUB_TILE) for s in range(0, tm, MIX_SUB_TILE)]
    yb = []
    for rows in subs:
        o = of_ref[rows, :].astype(F32) + ob_ref[rows, :].astype(F32)
        sr = sr_ref[rows, :].astype(F32)
        heads = []
        for h in range(GLA_HEADS):
            sl = slice(h * GLA_DV, (h + 1) * GLA_DV)
            oh = o[:, sl]
            ms = jnp.mean(oh * oh, axis=-1, keepdims=True)
            heads.append((oh * lax.rsqrt(ms + RMS_EPS) * gn_ref[...] * sr[:, sl]).astype(BF16))
        yb.append(jnp.concatenate([ya_ref[rows, :]] + heads, axis=1))
    xn = [_layer_norm(x_ref[rows, :], lnp_ref[0:1, :], lnp_ref[1:2, :]) for rows in subs]
    y = [_dot(y_in, wo_ref[...]) for y_in in yb]
    h2_b = []
    for rows, xn_s, y_s in zip(subs, xn, y):
        x1 = _layer_norm(DEEPNORM_ALPHA * xn_s + mod_ref[0:1, :] * y_s, lnp_ref[2:3, :], lnp_ref[3:4, :])
        x1_ref[rows, :] = x1
        h2_b.append((x1 * mod_ref[1:2, :] + mod_ref[2:3, :]).astype(BF16))
    logit_t = [(_dot(h2_s, wr_ref[...]) + br_ref[...]).T for h2_s in h2_b]
    per_sort = SORT_TILE // MIX_SUB_TILE
    routes = []
    for s in range(tm // SORT_TILE):
        routes.append(_route(jnp.concatenate(logit_t[s * per_sort:(s + 1) * per_sort], axis=1), SORT_TILE))
    rec_id = lax.broadcasted_iota(jnp.int32, (ROUTE_W, SORT_TILE), 0)
    targets = _slab_targets(0)
    for s, (pos_row, w_lo, w_hi, chunks) in enumerate(routes):
        rec_t = jnp.zeros((ROUTE_W, SORT_TILE), F32)
        for i, part in enumerate(_exact_bf16_parts(w_lo) + _exact_bf16_parts(w_hi)):
            rec_t = jnp.where(rec_id == i, part, rec_t)
        rec_b = rec_t.T.astype(BF16)
        h2_s = jnp.concatenate(h2_b[s * per_sort:(s + 1) * per_sort], axis=0)
        sort_lo, sort_hi = _slab_sort_matrices(pos_row, targets)
        pay_lo = jnp.concatenate([h2_s[:, 0:HALF_W], rec_b], axis=1)
        pay_hi = jnp.concatenate([h2_s[:, HALF_W:], jnp.zeros((SORT_TILE, ROUTE_W), BF16)], axis=1)
        slabs = _dot(sort_lo, pay_lo) + _dot(sort_hi, pay_hi)
        for c in range(SLAB_IN_W // LANES):
            hxs_ref[c, s * LOCAL_SLAB_ROWS:(s + 1) * LOCAL_SLAB_ROWS, :] = slabs[:, c * LANES:(c + 1) * LANES]
        pos_ref[s * SORT_TILE:(s + 1) * SORT_TILE, :] = jnp.broadcast_to(pos_row, (ROUTE_W, SORT_TILE)).T
        chunks_ref[8 * s:8 * (s + 1), :] = jnp.broadcast_to(chunks, (ROUTE_W, LANES)).T[0:8, :]


def _mix_out(x, o_f, o_b, sr, ya, mod, lnp, gn, w_out_b, wr, br):
    bsz, t, _ = x.shape
    assert t % SORT_TILE == 0
    n_sort = max(n for n in (1, 2, 4) if n * SORT_TILE <= TOKEN_TILE and t % (n * SORT_TILE) == 0)
    tm = n_sort * SORT_TILE
    n_t = t // tm
    tok = lambda w: pl.BlockSpec((None, tm, w), lambda b, i: (b, i, 0))
    full = lambda a: pl.BlockSpec(a.shape, lambda b, i: (0,) * a.ndim)
    flat = lambda rows, w: pl.BlockSpec((rows, w), lambda b, i: (b * n_t + i, 0))
    return pl.pallas_call(
        functools.partial(_mix_out_kernel, tm),
        grid=(bsz, n_t),
        in_specs=[
            tok(D_MODEL), tok(GLA_VAL), tok(GLA_VAL), tok(GLA_VAL), tok(CONV_CH),
            pl.BlockSpec((None, 3, D_MODEL), lambda b, i: (b, 0, 0)),
            full(lnp), full(gn), full(w_out_b), full(wr), full(br),
        ],
        out_specs=[tok(D_MODEL),
                   pl.BlockSpec((SLAB_IN_W // LANES, n_sort * LOCAL_SLAB_ROWS, LANES),
                                lambda b, i: (0, b * n_t + i, 0)),
                   flat(tm, ROUTE_W), flat(n_sort * 8, ROUTE_W)],
        out_shape=[
            jax.ShapeDtypeStruct((bsz, t, D_MODEL), F32),
            jax.ShapeDtypeStruct((SLAB_IN_W // LANES, bsz * t // SORT_TILE * LOCAL_SLAB_ROWS, LANES), F32),
            jax.ShapeDtypeStruct((bsz * t, ROUTE_W), F32),
            jax.ShapeDtypeStruct((bsz * t // SORT_TILE * 8, ROUTE_W), F32),
        ],
        compiler_params=_params(2),
        name="mix_out",
    )(x, o_f, o_b, sr, ya, mod, lnp, gn, w_out_b, wr, br)


def _moe_kernel(n_chunks, nused_ref, lo_ref, hi_ref, live_ref, src_ref, dst_ref,
                hxs_hbm, w1l_ref, w3l_ref, w2l_ref, w1h_ref, w3h_ref, w2h_ref, out_hbm, gbuf, obuf, gsem, ssem):
    tile_rows = CHUNKS_PER_TILE * SLAB_ROWS
    i = pl.program_id(0)
    n_used = nused_ref[0]
    slot = lax.bitwise_and(i, 1)

    def slab(chunk):
        return pl.ds(pl.multiple_of(chunk * SLAB_ROWS, SLAB_ROWS), SLAB_ROWS)

    def gather_copy(tile, buf_slot, j):
        chunk = src_ref[tile * CHUNKS_PER_TILE + j]
        return pltpu.make_async_copy(hxs_hbm.at[:, slab(chunk), :], gbuf.at[buf_slot, :, slab(j), :],
                                     gsem.at[buf_slot])

    def scatter_copy(tile, buf_slot, j):
        chunk = dst_ref[tile * CHUNKS_PER_TILE + j]
        return pltpu.make_async_copy(obuf.at[buf_slot, :, slab(j), :], out_hbm.at[:, slab(chunk), :],
                                     ssem.at[buf_slot])

    def start_gather(tile, buf_slot):
        for j in range(CHUNKS_PER_TILE):
            gather_copy(tile, buf_slot, j).start(priority=j % 2)

    def wait_gather(buf_slot):
        pltpu.make_async_copy(hxs_hbm.at[:, pl.ds(0, tile_rows), :], gbuf.at[buf_slot], gsem.at[buf_slot]).wait()

    def wait_scatter(buf_slot):
        pltpu.make_async_copy(obuf.at[buf_slot], out_hbm.at[:, pl.ds(0, tile_rows), :], ssem.at[buf_slot]).wait()

    @pl.when(i == 0)
    def _():
        start_gather(0, 0)
        obuf[...] = jnp.zeros(obuf.shape, F32)
        for s in range(2):
            fill = pltpu.make_async_copy(
                obuf.at[s], out_hbm.at[:, pl.ds((n_chunks + s * CHUNKS_PER_TILE) * SLAB_ROWS, tile_rows), :],
                ssem.at[s])
            fill.start()
            fill.wait()

    @pl.when(i + 1 < n_used)
    def _():
        start_gather(i + 1, 1 - slot)

    @pl.when(i < n_used)
    def _():
        wait_gather(slot)

        @pl.when(i >= 2)
        def _():
            wait_scatter(slot)

        @pl.when(live_ref[i] != 0)
        def _():
            def lane_block(c, half):
                return jnp.concatenate(
                    [gbuf[slot, c, pl.ds(2 * r + half, CHUNKS_PER_TILE, stride=SLAB_ROWS), :]
                     for r in range(CHUNK_ROWS)], axis=0)

            n_blk = HALF_W // LANES
            xb = jnp.concatenate([lane_block(c, 0) for c in range(n_blk)]
                                 + [lane_block(c, 1) for c in range(n_blk)], axis=1).astype(BF16)
            rec = lane_block(n_blk, 0)
            w_lo = rec[:, 0:1] + rec[:, 1:2] + rec[:, 2:3]
            w_hi = rec[:, 3:4] + rec[:, 4:5] + rec[:, 5:6]

            gate = [_dot(xb, w1_ref[...]) for w1_ref in (w1l_ref, w1h_ref)]
            up = [_dot(xb, w3_ref[...]) for w3_ref in (w3l_ref, w3h_ref)]
            act = [(_silu(g) * u).astype(BF16) for g, u in zip(gate, up)]
            e_lo, e_hi = [_dot(a, w2_ref[...]) for a, w2_ref in zip(act, (w2l_ref, w2h_ref))]
            y = w_lo * e_lo + w_hi * e_hi
            for r in range(CHUNK_ROWS):
                rows = slice(r * CHUNKS_PER_TILE, (r + 1) * CHUNKS_PER_TILE)
                for half in range(2):
                    for c in range(n_blk):
                        col = half * HALF_W + c * LANES
                        obuf[slot, c, pl.ds(2 * r + half, CHUNKS_PER_TILE, stride=SLAB_ROWS), :] = (
                            y[rows, col:col + LANES])

        @pl.when(live_ref[i] == 0)
        def _():
            obuf[slot] = jnp.zeros(obuf.shape[1:], F32)

        for j in range(CHUNKS_PER_TILE):
            scatter_copy(i, slot, j).start(priority=j % 2)

        @pl.when(i == n_used - 1)
        def _():
            wait_scatter(slot)

            @pl.when(i >= 1)
            def _():
                wait_scatter(1 - slot)


def _moe(hxs, src, dst, n_used, tile_lo, tile_hi, tile_live, w1_b, w3_b, w2_b):
    n_chunks = hxs.shape[1] // SLAB_ROWS
    tile_rows = CHUNKS_PER_TILE * SLAB_ROWS
    n_steps = src.shape[0] // CHUNKS_PER_TILE
    wspec = lambda which, shape: pl.BlockSpec(
        (None,) + shape, (lambda i, nu, lo, hi, lv, s, d: (lo[i], 0, 0)) if which == 0 else
        (lambda i, nu, lo, hi, lv, s, d: (hi[i], 0, 0)))
    grid_spec = pltpu.PrefetchScalarGridSpec(
        num_scalar_prefetch=6,
        grid=(n_steps,),
        in_specs=[
            pl.BlockSpec(memory_space=pl.ANY),
            wspec(0, (D_MODEL, D_EXPERT)), wspec(0, (D_MODEL, D_EXPERT)), wspec(0, (D_EXPERT, D_MODEL)),
            wspec(1, (D_MODEL, D_EXPERT)), wspec(1, (D_MODEL, D_EXPERT)), wspec(1, (D_EXPERT, D_MODEL)),
        ],
        out_specs=pl.BlockSpec(memory_space=pl.ANY),
        scratch_shapes=[
            pltpu.VMEM((2, SLAB_IN_W // LANES, tile_rows, LANES), F32),
            pltpu.VMEM((2, SLAB_OUT_W // LANES, tile_rows, LANES), F32),
            pltpu.SemaphoreType.DMA((2,)),
            pltpu.SemaphoreType.DMA((2,)),
        ],
    )
    return pl.pallas_call(
        functools.partial(_moe_kernel, n_chunks),
        grid_spec=grid_spec,
        out_shape=jax.ShapeDtypeStruct((SLAB_OUT_W // LANES, (n_chunks + 2 * CHUNKS_PER_TILE) * SLAB_ROWS, LANES), F32),
        compiler_params=_params(1),
        name="moe",
    )(n_used, tile_lo, tile_hi, tile_live, src, dst, hxs, w1_b, w3_b, w2_b, w1_b, w3_b, w2_b)


def _final_kernel(n_sort, x1_ref, moe_ref, pos_ref, mod_ref, lnp_ref, o_ref):
    moe = []
    targets = _slab_targets(1)
    for s in range(n_sort):
        slab_rows = slice(s * LOCAL_SLAB_ROWS, (s + 1) * LOCAL_SLAB_ROWS)
        moe_b = jnp.concatenate([moe_ref[c, slab_rows, :] for c in range(SLAB_OUT_W // LANES)], axis=1).astype(BF16)
        sort_lo, sort_hi = _slab_sort_matrices(pos_ref[s * SORT_TILE:(s + 1) * SORT_TILE, 0:1], targets)
        for r in range(0, SORT_TILE, MIX_SUB_TILE):
            rows = slice(r, r + MIX_SUB_TILE)
            moe.append((s * SORT_TILE + r,
                        jnp.concatenate([_dot(sort_lo[rows, :], moe_b), _dot(sort_hi[rows, :], moe_b)], axis=1)))
    for start, moe_s in moe:
        rows = slice(start, start + MIX_SUB_TILE)
        o_ref[rows, :] = _layer_norm(DEEPNORM_ALPHA * x1_ref[rows, :] + mod_ref[...] * moe_s,
                                     lnp_ref[0:1, :], lnp_ref[1:2, :])


def _final(x1, moe, pos, g2, lnp):
    bsz, t, _ = x1.shape
    n_sort = max(n for n in (1, 2, 4) if n * SORT_TILE <= TOKEN_TILE and t % (n * SORT_TILE) == 0)
    tm = n_sort * SORT_TILE
    n_t = t // tm
    flat = lambda rows, w: pl.BlockSpec((rows, w), lambda b, i: (b * n_t + i, 0))
    return pl.pallas_call(
        functools.partial(_final_kernel, n_sort),
        grid=(bsz, n_t),
        in_specs=[
            pl.BlockSpec((None, tm, D_MODEL), lambda b, i: (b, i, 0)),
            pl.BlockSpec((SLAB_OUT_W // LANES, n_sort * LOCAL_SLAB_ROWS, LANES), lambda b, i: (0, b * n_t + i, 0)),
            flat(tm, ROUTE_W),
            pl.BlockSpec((None, 1, D_MODEL), lambda b, i: (b, 0, 0)),
            pl.BlockSpec(lnp.shape, lambda b, i: (0, 0)),
        ],
        out_specs=pl.BlockSpec((None, tm, D_MODEL), lambda b, i: (b, i, 0)),
        out_shape=jax.ShapeDtypeStruct((bsz, t, D_MODEL), F32),
        compiler_params=_params(2),
        name="final",
    )(x1, moe, pos, g2, lnp)


def _pair_tables():
    lo, hi = [], []
    for g in range(N_GROUPS):
        for a in range(EXPERTS_PER_GROUP):
            for b in range(a + 1, EXPERTS_PER_GROUP):
                lo.append(g * EXPERTS_PER_GROUP + a)
                hi.append(g * EXPERTS_PER_GROUP + b)
    return jnp.array(lo, jnp.int32), jnp.array(hi, jnp.int32)


def _moe_plan(chunks, n_sort_tiles):
    n_cls = N_CLASSES + 1
    hp = lax.Precision.HIGHEST
    m = chunks.reshape(n_sort_tiles, 8, ROUTE_W)[:, 0, :n_cls].astype(jnp.int32)
    a_end = jnp.cumsum(m, axis=0)
    a_start = a_end - m
    per_cls = a_end[-1]
    padded = (per_cls + CHUNKS_PER_TILE - 1) // CHUNKS_PER_TILE * CHUNKS_PER_TILE
    g_end = jnp.cumsum(padded)
    g_start = g_end - padded
    local_off = jnp.cumsum(m, axis=1) - m
    seg = jnp.arange(n_sort_tiles, dtype=jnp.int32)[:, None] * LOCAL_CHUNKS + local_off - a_start
    n_steps = -(-(n_sort_tiles * LOCAL_CHUNKS) // CHUNKS_PER_TILE) + n_cls
    p = jnp.arange(n_steps * CHUNKS_PER_TILE, dtype=jnp.int32)
    cls_p = jnp.minimum(jnp.sum((g_end[None, :] <= p[:, None]).astype(jnp.int32), axis=1), n_cls - 1)
    onehot = (cls_p[:, None] == jnp.arange(n_cls, dtype=jnp.int32)[None, :]).astype(F32)
    pick = lambda tab: jnp.dot(onehot, tab.astype(F32), precision=hp)
    u = p - pick(g_start[:, None])[:, 0].astype(jnp.int32)
    valid = u < pick(per_cls[:, None])[:, 0].astype(jnp.int32)
    a_end_p = pick(a_end.T).astype(jnp.int32)
    seg_p = pick(seg.T).astype(jnp.int32)
    tile_p = jnp.sum((a_end_p <= u[:, None]).astype(jnp.int32), axis=1)
    hit = jnp.arange(n_sort_tiles, dtype=jnp.int32)[None, :] == tile_p[:, None]
    src = jnp.sum(jnp.where(hit, seg_p, 0), axis=1) + u
    pad_dst = n_sort_tiles * LOCAL_CHUNKS + (p // CHUNKS_PER_TILE) % 2 * CHUNKS_PER_TILE + p % CHUNKS_PER_TILE
    dst = jnp.where(valid, src, pad_dst).astype(jnp.int32)
    src = jnp.where(valid, src, 0).astype(jnp.int32)
    n_used = g_end[-1:] // CHUNKS_PER_TILE
    step = jnp.arange(n_steps, dtype=jnp.int32)
    tile_cls = jnp.sum((g_end[None, :] // CHUNKS_PER_TILE <= step[:, None]).astype(jnp.int32), axis=1)
    live = ((tile_cls < N_CLASSES) & (step < n_used[0])).astype(jnp.int32)
    pair_lo, pair_hi = _pair_tables()
    pair_oh = (jnp.minimum(tile_cls, N_CLASSES - 1)[:, None] == jnp.arange(N_CLASSES)[None, :]).astype(jnp.int32)
    tile_lo = jnp.sum(pair_oh * pair_lo[None, :], axis=1).astype(jnp.int32)
    tile_hi = jnp.sum(pair_oh * pair_hi[None, :], axis=1).astype(jnp.int32)
    return src, dst, n_used.astype(jnp.int32), tile_lo, tile_hi, live


def kernel(x, c, ctx, c_ctx, ln_in_g, ln_in_b, w_ada, b_ada, w_in, conv_w, conv_b, gate_w2_fwd, gate_b_fwd,
           gate_w2_bwd, gate_b_bwd, gla_norm_g, w_out, ln1_g, ln1_b, router_group_w, router_group_b,
           router_expert_w, router_expert_b, expert_w1, expert_w3, expert_w2, ln2_g, ln2_b):
    bsz, t, _ = x.shape
    n_tok = bsz * t
    l = 0
    rows = -(-(bsz + 1) // 8) * 8
    cond = jnp.zeros((rows, D_MODEL), F32).at[:bsz].set(c).at[bsz].set(c_ctx)
    ada = _ada(cond, w_ada[l], b_ada[l][None, :])
    sh1, sc1, g1, sh2, sc2, g2 = [ada[:, i * D_MODEL:(i + 1) * D_MODEL] for i in range(6)]

    w_in_b = w_in[l].astype(BF16)
    lnp_in = jnp.stack([ln_in_g, ln_in_b])
    zero = jnp.zeros((GLA_GATE_RANK, GLA_KEY), F32)
    w2cat = jnp.concatenate([jnp.concatenate([gate_w2_fwd[l], zero], axis=1),
                             jnp.concatenate([zero, gate_w2_bwd[l]], axis=1)], axis=0).astype(BF16)
    gbias = jnp.concatenate([gate_b_fwd[l], gate_b_bwd[l]])[None, :]

    n_ctx = ctx.shape[1]
    mod_ctx = jnp.stack([1.0 + sc1[bsz], sh1[bsz]])[None]
    k_c, v_c, g_c = [a.reshape(bsz, n_ctx, -1) for a in _proj(
        ctx.reshape(1, bsz * n_ctx, D_MODEL), mod_ctx, lnp_in, w_in_b, conv_w[l], conv_b[l][None, :], w2cat, gbias,
        False)]
    zero_state = jnp.zeros((bsz, GLA_KEY, PAIR_VAL), F32)
    s_f, s_b = _gla(None, k_c, v_c, g_c, zero_state, zero_state)

    mod1 = jnp.stack([1.0 + sc1[:bsz], sh1[:bsz]], axis=1)
    ya, q, k, v, sr, g = _proj(x, mod1, lnp_in, w_in_b, conv_w[l], conv_b[l][None, :], w2cat, gbias, True)
    o_f, o_b, _, _ = _gla(q, k, v, g, s_f, s_b)

    mod2 = jnp.stack([g1[:bsz], 1.0 + sc2[:bsz], sh2[:bsz]], axis=1)
    lnp1 = jnp.stack([ln_in_g, ln_in_b, ln1_g[l], ln1_b[l]])
    wr = jnp.zeros((D_MODEL, ROUTE_W), F32)
    wr = wr.at[:, :N_GROUPS].set(router_group_w[l]).at[:, N_GROUPS:N_GROUPS + N_EXPERTS].set(router_expert_w[l])
    br = jnp.zeros((1, ROUTE_W), F32)
    br = br.at[0, :N_GROUPS].set(router_group_b[l]).at[0, N_GROUPS:N_GROUPS + N_EXPERTS].set(router_expert_b[l])
    x1, hxs, pos, chunks = _mix_out(x, o_f, o_b, sr, ya, mod2, lnp1, gla_norm_g[l][None, :],
                                    w_out[l].astype(BF16), wr.astype(BF16), br)

    src, dst, n_used, tile_lo, tile_hi, live = _moe_plan(chunks, n_tok // SORT_TILE)
    moe = _moe(hxs, src, dst, n_used, tile_lo, tile_hi, live,
               expert_w1[l].astype(BF16), expert_w3[l].astype(BF16), expert_w2[l].astype(BF16))

    return _final(x1, moe, pos, g2[:bsz][:, None, :], jnp.stack([ln2_g[l], ln2_b[l]]))
```

```python
import functools

import jax
import jax.numpy as jnp
from jax import lax
from jax.experimental import pallas as pl
from jax.experimental.pallas import tpu as pltpu

F32 = jnp.float32
BF16 = jnp.bfloat16

D_MODEL = 1024
GRID_W = 64
CONV_CH = 512
GLA_HEADS = 4
GLA_DK = 64
GLA_DV = 128
GLA_KEY = GLA_HEADS * GLA_DK
GLA_VAL = GLA_HEADS * GLA_DV
PAIR_KEY = 2 * GLA_DK
PAIR_VAL = 2 * GLA_DV
GLA_GATE_RANK = 16
GLA_TAU = 16.0
OFF_AB = 0
OFF_AC = OFF_AB + CONV_CH
OFF_AX = OFF_AC + CONV_CH
OFF_Q = OFF_AX + CONV_CH
OFF_K = OFF_Q + GLA_KEY
OFF_V = OFF_K + GLA_KEY
OFF_R = OFF_V + GLA_VAL
OFF_GF = OFF_R + GLA_VAL
D_PROJ = OFF_GF + 2 * GLA_GATE_RANK
N_GROUPS = 4
EXPERTS_PER_GROUP = 4
N_EXPERTS = N_GROUPS * EXPERTS_PER_GROUP
D_EXPERT = 512
PAIRS_PER_GROUP = 6
N_CLASSES = N_GROUPS * PAIRS_PER_GROUP
LN_EPS = 1e-5
RMS_EPS = 1e-6
DEPTH = 1
DEEPNORM_ALPHA = (2.0 * DEPTH) ** 0.25

LANES = 128
GLA_CHUNK = 64
GLA_SUB = 16
N_SUB = GLA_CHUNK // GLA_SUB
ROUTE_W = LANES
HALF_W = D_MODEL // 2
SLAB_IN_W = HALF_W + ROUTE_W
SLAB_OUT_W = HALF_W
ADA_COL_TILE = 1024
TOKEN_TILE = 1024
SORT_TILE = 256
MIX_SUB_TILE = 128
MOE_TILE = 256
CHUNK_ROWS = 4
SLAB_ROWS = 2 * CHUNK_ROWS
LOCAL_CHUNKS = -(-(SORT_TILE + N_CLASSES * (CHUNK_ROWS - 1)) // CHUNK_ROWS)
LOCAL_SLAB_ROWS = LOCAL_CHUNKS * SLAB_ROWS
CHUNKS_PER_TILE = MOE_TILE // CHUNK_ROWS
VMEM_LIMIT = 56 * 1024 * 1024


def _params(n_axes, vmem=VMEM_LIMIT):
    return pltpu.CompilerParams(dimension_semantics=("arbitrary",) * n_axes, vmem_limit_bytes=vmem)


def _dot(a, b):
    return jnp.dot(a, b, preferred_element_type=F32)


def _div_pow2(x, d):
    assert d & (d - 1) == 0
    return lax.shift_right_logical(x, jnp.int32(d.bit_length() - 1))


def _mod_pow2(x, d):
    assert d & (d - 1) == 0
    return lax.bitwise_and(x, jnp.int32(d - 1))


def _split2(x):
    hi = x.astype(BF16)
    lo = (x - hi.astype(F32)).astype(BF16)
    return hi, lo


def _dot3(a, b):
    ah, al = _split2(a)
    bh, bl = _split2(b)
    return _dot(ah, bh) + _dot(ah, bl) + _dot(al, bh)


def _silu(x):
    return x * (0.5 * jnp.tanh(0.5 * x) + 0.5)


def _layer_norm(x, g, b):
    mu = jnp.mean(x, axis=-1, keepdims=True)
    xc = x - mu
    var = jnp.mean(xc * xc, axis=-1, keepdims=True)
    return xc * lax.rsqrt(var + LN_EPS) * g + b


def _ada_kernel(c_ref, w_ref, b_ref, o_ref):
    o_ref[...] = _dot3(_silu(c_ref[...]), w_ref[...]) + b_ref[...]


def _ada(cond, w_ada, b_ada):
    rows = cond.shape[0]
    n_out = w_ada.shape[1]
    tn = ADA_COL_TILE
    return pl.pallas_call(
        _ada_kernel,
        grid=(n_out // tn,),
        in_specs=[
            pl.BlockSpec((rows, D_MODEL), lambda j: (0, 0)),
            pl.BlockSpec((D_MODEL, tn), lambda j: (0, j)),
            pl.BlockSpec((1, tn), lambda j: (0, j)),
        ],
        out_specs=pl.BlockSpec((rows, tn), lambda j: (0, j)),
        out_shape=jax.ShapeDtypeStruct((rows, n_out), F32),
        compiler_params=_params(1),
        name="ada",
    )(cond, w_ada, b_ada)


def _log_sigmoid(z):
    return jnp.minimum(z, 0.0) - jnp.log(1.0 + jnp.exp(-jnp.abs(z)))


def _proj_kernel(latent, tm, x_ref, mod_ref, lnp_ref, w_ref, cw_ref, cb_ref, w2_ref, gbias_ref, *out_refs):
    x = x_ref[...]
    xn = _layer_norm(x, lnp_ref[0:1, :], lnp_ref[1:2, :])
    h = xn * mod_ref[0:1, :] + mod_ref[1:2, :]
    hb = h.astype(BF16)
    if latent:
        ya_ref, q_ref, k_ref, v_ref, sr_ref, g_ref, xn_ref = out_refs
        xn_ref[...] = xn
        p = _dot(hb, w_ref[:, OFF_AB:OFF_Q])
        a_b = p[:, 0:CONV_CH]
        u = p[:, CONV_CH:2 * CONV_CH] * p[:, 2 * CONV_CH:3 * CONV_CH]
        pos = _mod_pow2(lax.broadcasted_iota(jnp.int32, (tm, 1), 0), GRID_W)
        u_prev = jnp.where(pos == 0, 0.0, pltpu.roll(u, 1, 0))
        u_next = jnp.where(pos == GRID_W - 1, 0.0, pltpu.roll(u, tm - 1, 0))
        conv = u_prev * cw_ref[0:1, :] + u * cw_ref[1:2, :] + u_next * cw_ref[2:3, :] + cb_ref[...]
        ya_ref[...] = (a_b * conv).astype(BF16)
        qk = _dot(hb, w_ref[:, OFF_Q:OFF_V])
        q_ref[...] = (qk[:, 0:GLA_KEY] * (GLA_DK ** -0.5)).astype(BF16)
        k_ref[...] = qk[:, GLA_KEY:].astype(BF16)
        r = _dot(hb, w_ref[:, OFF_R:OFF_GF])
        sr_ref[...] = _silu(r).astype(BF16)
    else:
        k_ref, v_ref, g_ref = out_refs
        k_ref[...] = _dot(hb, w_ref[:, OFF_K:OFF_V]).astype(BF16)
    v_ref[...] = _dot(hb, w_ref[:, OFF_V:OFF_R]).astype(BF16)
    low = _dot(hb, w_ref[:, OFF_GF:D_PROJ])
    z = _dot(low.astype(BF16), w2_ref[...]) + gbias_ref[...]
    g_ref[...] = _log_sigmoid(z) * (1.0 / GLA_TAU)


def _proj(x, mod, lnp, w_in_b, conv_w, conv_b, w2cat, gbias, latent):
    bsz, t, _ = x.shape
    tm = min(TOKEN_TILE, t)
    assert t % tm == 0 and tm % GRID_W == 0
    tok = lambda w: pl.BlockSpec((None, tm, w), lambda b, i: (b, i, 0))
    full = lambda a: pl.BlockSpec(a.shape, lambda b, i: (0,) * a.ndim)
    widths = ([(CONV_CH, BF16), (GLA_KEY, BF16)] if latent else []) + [(GLA_KEY, BF16), (GLA_VAL, BF16)]
    widths += ([(GLA_VAL, BF16)] if latent else []) + [(2 * GLA_KEY, F32)]
    widths += [(D_MODEL, F32)] if latent else []
    return pl.pallas_call(
        functools.partial(_proj_kernel, latent, tm),
        grid=(bsz, t // tm),
        in_specs=[
            tok(D_MODEL),
            pl.BlockSpec((None, 2, D_MODEL), lambda b, i: (b, 0, 0)),
            full(lnp), full(w_in_b), full(conv_w), full(conv_b), full(w2cat), full(gbias),
        ],
        out_specs=[tok(w) for w, _ in widths],
        out_shape=[jax.ShapeDtypeStruct((bsz, t, w), dt) for w, dt in widths],
        compiler_params=_params(2),
        name="proj_latent" if latent else "proj_ctx",
    )(x, mod, lnp, w_in_b, conv_w, conv_b, w2cat, gbias)


def _tri(n, reverse, strict=False):
    i = lax.broadcasted_iota(jnp.int32, (n, n), 0)
    j = lax.broadcasted_iota(jnp.int32, (n, n), 1)
    if strict:
        m = (j > i) if reverse else (j < i)
    else:
        m = (j >= i) if reverse else (j <= i)
    return jnp.where(m, 1.0, 0.0).astype(BF16)


def _chunk_cumsum(g, reverse):
    tri = _tri(GLA_CHUNK, reverse)
    g_hi, g_lo = _split2(g)
    return _dot(tri, g_hi) + _dot(tri, g_lo)


def _as_column(row):
    return jnp.broadcast_to(row, (LANES, row.shape[1])).T


def _sub_anchors(gc, reverse):
    zero = jnp.zeros((1, GLA_KEY), F32)
    if reverse:
        return [gc[GLA_SUB * (a + 1):GLA_SUB * (a + 1) + 1] for a in range(N_SUB - 1)] + [zero]
    return [zero] + [gc[GLA_SUB * a - 1:GLA_SUB * a] for a in range(1, N_SUB)]


def _score_pairs(reverse):
    return [(a, b) for a in range(N_SUB) for b in range(N_SUB) if (b >= a if reverse else b <= a)]


def _intra_products(q, k, gc, reverse):
    r = _sub_anchors(gc, reverse)
    anchor = jnp.concatenate([jnp.broadcast_to(ra, (GLA_SUB, GLA_KEY)) for ra in r], axis=0)
    gcb = gc - anchor
    qt = q * jnp.exp(gcb)
    kt = k * jnp.exp(-gcb)
    rows = []
    for a, b in _score_pairs(reverse):
        qa = qt[GLA_SUB * a:GLA_SUB * (a + 1)]
        if a != b:
            qa = qa * jnp.exp(r[a] - r[b])
        rows.append(qa)
    qp = jnp.concatenate(rows, axis=0).astype(BF16)
    width = GLA_HEADS * GLA_CHUNK
    rr = lax.broadcasted_iota(jnp.int32, (width, GLA_KEY), 0)
    cc = lax.broadcasted_iota(jnp.int32, (width, GLA_KEY), 1)
    kbd = jnp.where(_div_pow2(rr, GLA_CHUNK) == _div_pow2(cc, GLA_DK),
                    jnp.concatenate([kt] * GLA_HEADS, axis=0), 0.0)
    return lax.dot_general(qp, kbd.astype(BF16), (((1,), (1,)), ((), ())), preferred_element_type=F32)


def _assemble_scores(res, reverse):
    pairs = _score_pairs(reverse)
    width = GLA_HEADS * GLA_CHUNK
    col = _mod_pow2(lax.broadcasted_iota(jnp.int32, (GLA_SUB, width), 1), GLA_CHUNK)
    col_blk = _div_pow2(col, GLA_SUB)
    col_pos = _mod_pow2(col, GLA_SUB)
    row_pos = lax.broadcasted_iota(jnp.int32, (GLA_SUB, width), 0)
    causal = (col_pos >= row_pos) if reverse else (col_pos <= row_pos)
    blocks = []
    for a in range(N_SUB):
        acc = jnp.zeros((GLA_SUB, width), F32)
        for idx, (pa, pb) in enumerate(pairs):
            if pa != a:
                continue
            keep = col_blk == pb
            if pa == pb:
                keep = keep & causal
            acc = acc + jnp.where(keep, res[GLA_SUB * idx:GLA_SUB * (idx + 1)], 0.0)
        blocks.append(acc)
    return jnp.concatenate(blocks, axis=0)


def _pair_mask(rows_per_head, cols_per_head, n_row_pairs=1):
    shape = (n_row_pairs * 2 * rows_per_head, 2 * cols_per_head)
    rr = _mod_pow2(lax.broadcasted_iota(jnp.int32, shape, 0), 2 * rows_per_head)
    cc = lax.broadcasted_iota(jnp.int32, shape, 1)
    return _div_pow2(rr, rows_per_head) == _div_pow2(cc, cols_per_head)


def _state_terms(q, k, v_b, gc, reverse):
    total = gc[0:1] if reverse else gc[GLA_CHUNK - 1:GLA_CHUNK]
    q_dec = None if q is None else (q * jnp.exp(gc)).astype(BF16)
    k_end = (k * jnp.exp(total - gc)).astype(BF16)
    tn = (((0,), (0,)), ((), ()))
    upd = [lax.dot_general(k_end[:, p * PAIR_KEY:(p + 1) * PAIR_KEY], v_b[:, p * PAIR_VAL:(p + 1) * PAIR_VAL], tn,
                           preferred_element_type=F32) for p in range(GLA_HEADS // 2)]
    upd = jnp.where(_pair_mask(GLA_DK, GLA_DV, GLA_HEADS // 2), jnp.concatenate(upd, axis=0), 0.0)
    decay = jnp.exp(_as_column(total))
    decay = jnp.concatenate([decay] * (PAIR_VAL // LANES), axis=1)
    return q_dec, decay, upd


def _advance_state(q_dec, decay, upd, state):
    o_inter = None
    if q_dec is not None:
        state_b = state.astype(BF16)
        o_inter = jnp.concatenate(
            [_dot(q_dec[:, p * PAIR_KEY:(p + 1) * PAIR_KEY], state_b[p * PAIR_KEY:(p + 1) * PAIR_KEY, :])
             for p in range(GLA_HEADS // 2)], axis=1)
    return o_inter, state * decay + upd


def _gla_kernel(tt, nt, with_out, *refs):
    if with_out:
        (qf_ref, kf_ref, vf_ref, gfw_ref, qb_ref, kb_ref, vb_ref, gbw_ref, s0f_ref, s0b_ref,
         of_ref, ob_ref, sf_ref, sb_ref) = refs
    else:
        kf_ref, vf_ref, gfw_ref, kb_ref, vb_ref, gbw_ref, s0f_ref, s0b_ref, sf_ref, sb_ref = refs
    j = pl.program_id(1)

    @pl.when(j == 0)
    def _():
        sf_ref[...] = s0f_ref[...]
        sb_ref[...] = s0b_ref[...]

    chunk_slices = [slice(c * GLA_CHUNK, (c + 1) * GLA_CHUNK) for c in range(tt // GLA_CHUNK)]
    fwd, bwd = [], []
    for sl in chunk_slices:
        g = gfw_ref[sl, :]
        fwd.append(dict(sl=sl, k=kf_ref[sl, :].astype(F32), v=vf_ref[sl, :], g_f=g[:, 0:GLA_KEY], g_b=g[:, GLA_KEY:],
                        q=qf_ref[sl, :].astype(F32) if with_out else None))
        bwd.append(dict(sl=sl, k=kb_ref[sl, :].astype(F32), v=vb_ref[sl, :], g_b=gbw_ref[sl, :][:, GLA_KEY:],
                        q=qb_ref[sl, :].astype(F32) if with_out else None))
    for d in fwd:
        d["gc_f"] = _chunk_cumsum(d["g_f"], False)
        if with_out:
            d["gc_b"] = _chunk_cumsum(d["g_b"], True)
    for d in bwd:
        d["gc_b"] = _chunk_cumsum(d["g_b"], True)
    if with_out:
        for d in fwd:
            d["res_f"] = _intra_products(d["q"], d["k"], d["gc_f"], False)
            d["res_b"] = _intra_products(d["q"], d["k"], d["gc_b"], True)
    for d in fwd:
        d["terms"] = _state_terms(d["q"], d["k"], d["v"], d["gc_f"], False)
    for d in bwd:
        d["terms"] = _state_terms(d["q"], d["k"], d["v"], d["gc_b"], True)
    if with_out:
        for d in fwd:
            scores = (_assemble_scores(d["res_f"], False) + _assemble_scores(d["res_b"], True)).astype(BF16)
            o_intra = []
            for p in range(GLA_HEADS // 2):
                v_p = d["v"][:, p * PAIR_VAL:(p + 1) * PAIR_VAL]
                vbd = jnp.where(_pair_mask(GLA_CHUNK, GLA_DV), jnp.concatenate([v_p, v_p], axis=0),
                                jnp.zeros((), BF16))
                o_intra.append(_dot(scores[:, p * 2 * GLA_CHUNK:(p + 1) * 2 * GLA_CHUNK], vbd))
            d["o_intra"] = jnp.concatenate(o_intra, axis=1)

    state = sf_ref[...]
    for d in fwd:
        o_inter, state = _advance_state(*d["terms"], state)
        if with_out:
            of_ref[d["sl"], :] = (d["o_intra"] + o_inter).astype(BF16)
    sf_ref[...] = state

    state = sb_ref[...]
    for d in reversed(bwd):
        o_inter, state = _advance_state(*d["terms"], state)
        if with_out:
            ob_ref[d["sl"], :] = o_inter.astype(BF16)
    sb_ref[...] = state


def _gla(q, k, v, g, s0f, s0b):
    with_out = q is not None
    bsz, t, _ = k.shape
    tt = min(TOKEN_TILE, t)
    assert t % tt == 0 and tt % GLA_CHUNK == 0
    nt = t // tt
    fwd = lambda w: pl.BlockSpec((None, tt, w), lambda b, j: (b, j, 0))
    bwd = lambda w: pl.BlockSpec((None, tt, w), lambda b, j: (b, nt - 1 - j, 0))
    st = pl.BlockSpec((None, GLA_KEY, PAIR_VAL), lambda b, j: (b, 0, 0))
    st_shape = jax.ShapeDtypeStruct((bsz, GLA_KEY, PAIR_VAL), F32)
    if with_out:
        ins = [q, k, v, g, q, k, v, g, s0f, s0b]
        in_specs = [fwd(GLA_KEY), fwd(GLA_KEY), fwd(GLA_VAL), fwd(2 * GLA_KEY),
                    bwd(GLA_KEY), bwd(GLA_KEY), bwd(GLA_VAL), bwd(2 * GLA_KEY), st, st]
        out_specs = [fwd(GLA_VAL), bwd(GLA_VAL), st, st]
        o_shape = jax.ShapeDtypeStruct((bsz, t, GLA_VAL), BF16)
        out_shape = [o_shape, o_shape, st_shape, st_shape]
    else:
        ins = [k, v, g, k, v, g, s0f, s0b]
        in_specs = [fwd(GLA_KEY), fwd(GLA_VAL), fwd(2 * GLA_KEY),
                    bwd(GLA_KEY), bwd(GLA_VAL), bwd(2 * GLA_KEY), st, st]
        out_specs = [st, st]
        out_shape = [st_shape, st_shape]
    return pl.pallas_call(
        functools.partial(_gla_kernel, tt, nt, with_out),
        grid=(bsz, nt),
        in_specs=in_specs,
        out_specs=out_specs,
        out_shape=out_shape,
        compiler_params=_params(2),
        name="gla_latent" if with_out else "gla_ctx",
    )(*ins)


def _exact_bf16_parts(x):
    hi = x.astype(BF16).astype(F32)
    r = x - hi
    mid = r.astype(BF16).astype(F32)
    lo = (r - mid).astype(BF16).astype(F32)
    return hi, mid, lo


def _first_index(values, best):
    idx = jnp.full_like(best, float(len(values) - 1))
    for i in reversed(range(len(values) - 1)):
        idx = jnp.where(values[i] >= best, float(i), idx)
    return idx


def _pick(rows, idx):
    out = rows[-1]
    for i in reversed(range(len(rows) - 1)):
        out = jnp.where(idx == float(i), rows[i], out)
    return out


def _route(logit_t, tm):
    row = lambda r: logit_t[r:r + 1, :]
    groups = [row(i) for i in range(N_GROUPS)]
    top = functools.reduce(jnp.maximum, groups)
    eg = [jnp.exp(x - top) for x in groups]
    total = functools.reduce(lambda a, b: a + b, eg)
    pg = [e / total for e in eg]
    p_g = functools.reduce(jnp.maximum, pg)
    g_idx = _first_index(pg, p_g)
    sel = [_pick([row(N_GROUPS + EXPERTS_PER_GROUP * g + j) for g in range(N_GROUPS)], g_idx)
           for j in range(EXPERTS_PER_GROUP)]
    top = functools.reduce(jnp.maximum, sel)
    ee = [jnp.exp(x - top) for x in sel]
    total = functools.reduce(lambda a, b: a + b, ee)
    pe = [e / total for e in ee]
    p1 = functools.reduce(jnp.maximum, pe)
    l1 = _first_index(pe, p1)
    pe2 = [jnp.where(l1 == float(j), -1.0, pe[j]) for j in range(EXPERTS_PER_GROUP)]
    p2 = functools.reduce(jnp.maximum, pe2)
    l2 = _first_index(pe2, p2)
    den = p1 + p2
    w1 = p1 / den * p_g
    w2 = p2 / den * p_g
    lo = jnp.minimum(l1, l2)
    hi = jnp.maximum(l1, l2)
    pair = lo * (7.0 - lo) * 0.5 + (hi - lo - 1.0)
    cls = g_idx * PAIRS_PER_GROUP + pair
    w_lo = jnp.where(l1 < l2, w1, w2)
    w_hi = jnp.where(l1 < l2, w2, w1)
    cls_id = lax.broadcasted_iota(jnp.int32, (ROUTE_W, tm), 0).astype(F32)
    onehot = jnp.where(cls_id == cls, 1.0, 0.0)
    before = _dot(onehot.astype(BF16), _tri(tm, True, strict=True))
    count = jnp.sum(onehot, axis=1, keepdims=True)
    chunks = jnp.floor((count + (CHUNK_ROWS - 1.0)) * (1.0 / CHUNK_ROWS))
    chunks = jnp.where(cls_id[:, 0:1] == float(N_CLASSES),
                       LOCAL_CHUNKS - jnp.sum(chunks, axis=0, keepdims=True), chunks)
    first_chunk = _dot(_tri(ROUTE_W, False, strict=True),
                       jnp.broadcast_to(chunks, (ROUTE_W, LANES)).astype(BF16))[:, 0:1]
    pos_row = jnp.sum(onehot * (CHUNK_ROWS * first_chunk + before), axis=0, keepdims=True)
    return pos_row, w_lo, w_hi, chunks


def _slab_targets(slab_axis):
    shape = (LOCAL_SLAB_ROWS, 1) if slab_axis == 0 else (1, LOCAL_SLAB_ROWS)
    slab_row = lax.broadcasted_iota(jnp.int32, shape, slab_axis)
    sub = _mod_pow2(slab_row, SLAB_ROWS)
    token_row = (CHUNK_ROWS * _div_pow2(slab_row, SLAB_ROWS) + _div_pow2(sub, 2)).astype(F32)
    half = _mod_pow2(sub, 2)
    return [jnp.where(half == h, token_row, -1.0) for h in range(2)]


def _slab_sort_matrices(pos, targets):
    return [jnp.where(t == pos, 1.0, 0.0).astype(BF16) for t in targets]


def _mix_out_kernel(tm, xn_ref, of_ref, ob_ref, sr_ref, ya_ref, mod_ref, lnp_ref, gn_ref, wo_ref, wr_ref, br_ref,
                    x1_ref, hxs_ref, pos_ref, chunks_ref):
    subs = [slice(s, s + MIX_SUB_TILE) for s in range(0, tm, MIX_SUB_TILE)]
    yb = []
    for rows in subs:
        o = of_ref[rows, :].astype(F32) + ob_ref[rows, :].astype(F32)
        sr = sr_ref[rows, :].astype(F32)
        heads = []
        for h in range(GLA_HEADS):
            sl = slice(h * GLA_DV, (h + 1) * GLA_DV)
            oh = o[:, sl]
            ms = jnp.mean(oh * oh, axis=-1, keepdims=True)
            heads.append((oh * lax.rsqrt(ms + RMS_EPS) * gn_ref[...] * sr[:, sl]).astype(BF16))
        yb.append(jnp.concatenate([ya_ref[rows, :]] + heads, axis=1))
    xn = [xn_ref[rows, :] for rows in subs]
    y = [_dot(y_in, wo_ref[...]) for y_in in yb]
    h2_b = []
    for rows, xn_s, y_s in zip(subs, xn, y):
        x1 = _layer_norm(DEEPNORM_ALPHA * xn_s + mod_ref[0:1, :] * y_s, lnp_ref[0:1, :], lnp_ref[1:2, :])
        x1_ref[rows, :] = x1
        h2_b.append((x1 * mod_ref[1:2, :] + mod_ref[2:3, :]).astype(BF16))
    logit_t = [(_dot(h2_s, wr_ref[...]) + br_ref[...]).T for h2_s in h2_b]
    per_sort = SORT_TILE // MIX_SUB_TILE
    routes = []
    for s in range(tm // SORT_TILE):
        routes.append(_route(jnp.concatenate(logit_t[s * per_sort:(s + 1) * per_sort], axis=1), SORT_TILE))
    rec_id = lax.broadcasted_iota(jnp.int32, (ROUTE_W, SORT_TILE), 0)
    targets = _slab_targets(0)
    for s, (pos_row, w_lo, w_hi, chunks) in enumerate(routes):
        rec_t = jnp.zeros((ROUTE_W, SORT_TILE), F32)
        for i, part in enumerate(_exact_bf16_parts(w_lo) + _exact_bf16_parts(w_hi)):
            rec_t = jnp.where(rec_id == i, part, rec_t)
        rec_b = rec_t.T.astype(BF16)
        h2_s = jnp.concatenate(h2_b[s * per_sort:(s + 1) * per_sort], axis=0)
        sort_lo, sort_hi = _slab_sort_matrices(pos_row, targets)
        pay_lo = jnp.concatenate([h2_s[:, 0:HALF_W], rec_b], axis=1)
        pay_hi = jnp.concatenate([h2_s[:, HALF_W:], jnp.zeros((SORT_TILE, ROUTE_W), BF16)], axis=1)
        slabs = _dot(sort_lo, pay_lo) + _dot(sort_hi, pay_hi)
        for c in range(SLAB_IN_W // LANES):
            hxs_ref[c, s * LOCAL_SLAB_ROWS:(s + 1) * LOCAL_SLAB_ROWS, :] = slabs[:, c * LANES:(c + 1) * LANES]
        pos_ref[s * SORT_TILE:(s + 1) * SORT_TILE, :] = jnp.broadcast_to(pos_row, (ROUTE_W, SORT_TILE)).T
        chunks_ref[8 * s:8 * (s + 1), :] = jnp.broadcast_to(chunks, (ROUTE_W, LANES)).T[0:8, :]


def _mix_out(xn, o_f, o_b, sr, ya, mod, lnp, gn, w_out_b, wr, br):
    bsz, t, _ = xn.shape
    assert t % SORT_TILE == 0
    n_sort = max(n for n in (1, 2, 4) if n * SORT_TILE <= TOKEN_TILE and t % (n * SORT_TILE) == 0)
    tm = n_sort * SORT_TILE
    n_t = t // tm
    tok = lambda w: pl.BlockSpec((None, tm, w), lambda b, i: (b, i, 0))
    full = lambda a: pl.BlockSpec(a.shape, lambda b, i: (0,) * a.ndim)
    flat = lambda rows, w: pl.BlockSpec((rows, w), lambda b, i: (b * n_t + i, 0))
    return pl.pallas_call(
        functools.partial(_mix_out_kernel, tm),
        grid=(bsz, n_t),
        in_specs=[
            tok(D_MODEL), tok(GLA_VAL), tok(GLA_VAL), tok(GLA_VAL), tok(CONV_CH),
            pl.BlockSpec((None, 3, D_MODEL), lambda b, i: (b, 0, 0)),
            full(lnp), full(gn), full(w_out_b), full(wr), full(br),
        ],
        out_specs=[tok(D_MODEL),
                   pl.BlockSpec((SLAB_IN_W // LANES, n_sort * LOCAL_SLAB_ROWS, LANES),
                                lambda b, i: (0, b * n_t + i, 0)),
                   flat(tm, ROUTE_W), flat(n_sort * 8, ROUTE_W)],
        out_shape=[
            jax.ShapeDtypeStruct((bsz, t, D_MODEL), F32),
            jax.ShapeDtypeStruct((SLAB_IN_W // LANES, bsz * t // SORT_TILE * LOCAL_SLAB_ROWS, LANES), F32),
            jax.ShapeDtypeStruct((bsz * t, ROUTE_W), F32),
            jax.ShapeDtypeStruct((bsz * t // SORT_TILE * 8, ROUTE_W), F32),
        ],
        compiler_params=_params(2),
        name="mix_out",
    )(xn, o_f, o_b, sr, ya, mod, lnp, gn, w_out_b, wr, br)


def _moe_kernel(n_chunks, nused_ref, lo_ref, hi_ref, live_ref, src_ref, dst_ref,
                hxs_hbm, w1l_ref, w3l_ref, w2l_ref, w1h_ref, w3h_ref, w2h_ref, out_hbm, gbuf, obuf, gsem, ssem):
    tile_rows = CHUNKS_PER_TILE * SLAB_ROWS
    i = pl.program_id(0)
    n_used = nused_ref[0]
    slot = lax.bitwise_and(i, 1)

    def slab(chunk):
        return pl.ds(pl.multiple_of(chunk * SLAB_ROWS, SLAB_ROWS), SLAB_ROWS)

    def gather_copy(tile, buf_slot, j):
        chunk = src_ref[tile * CHUNKS_PER_TILE + j]
        return pltpu.make_async_copy(hxs_hbm.at[:, slab(chunk), :], gbuf.at[buf_slot, :, slab(j), :],
                                     gsem.at[buf_slot])

    def scatter_copy(tile, buf_slot, j):
        chunk = dst_ref[tile * CHUNKS_PER_TILE + j]
        return pltpu.make_async_copy(obuf.at[buf_slot, :, slab(j), :], out_hbm.at[:, slab(chunk), :],
                                     ssem.at[buf_slot])

    def start_gather(tile, buf_slot):
        for j in range(CHUNKS_PER_TILE):
            gather_copy(tile, buf_slot, j).start(priority=j % 2)

    def wait_gather(buf_slot):
        pltpu.make_async_copy(hxs_hbm.at[:, pl.ds(0, tile_rows), :], gbuf.at[buf_slot], gsem.at[buf_slot]).wait()

    def wait_scatter(buf_slot):
        pltpu.make_async_copy(obuf.at[buf_slot], out_hbm.at[:, pl.ds(0, tile_rows), :], ssem.at[buf_slot]).wait()

    @pl.when(i == 0)
    def _():
        start_gather(0, 0)
        obuf[...] = jnp.zeros(obuf.shape, F32)
        for s in range(2):
            fill = pltpu.make_async_copy(
                obuf.at[s], out_hbm.at[:, pl.ds((n_chunks + s * CHUNKS_PER_TILE) * SLAB_ROWS, tile_rows), :],
                ssem.at[s])
            fill.start()
            fill.wait()

    @pl.when(i + 1 < n_used)
    def _():
        start_gather(i + 1, 1 - slot)

    @pl.when(i < n_used)
    def _():
        wait_gather(slot)

        @pl.when(i >= 2)
        def _():
            wait_scatter(slot)

        @pl.when(live_ref[i] != 0)
        def _():
            def lane_block(c, half):
                return jnp.concatenate(
                    [gbuf[slot, c, pl.ds(2 * r + half, CHUNKS_PER_TILE, stride=SLAB_ROWS), :]
                     for r in range(CHUNK_ROWS)], axis=0)

            n_blk = HALF_W // LANES
            xb = jnp.concatenate([lane_block(c, 0) for c in range(n_blk)]
                                 + [lane_block(c, 1) for c in range(n_blk)], axis=1).astype(BF16)
            rec = lane_block(n_blk, 0)
            w_lo = rec[:, 0:1] + rec[:, 1:2] + rec[:, 2:3]
            w_hi = rec[:, 3:4] + rec[:, 4:5] + rec[:, 5:6]

            gate = [_dot(xb, w1_ref[...]) for w1_ref in (w1l_ref, w1h_ref)]
            up = [_dot(xb, w3_ref[...]) for w3_ref in (w3l_ref, w3h_ref)]
            act = [(_silu(g) * u).astype(BF16) for g, u in zip(gate, up)]
            e_lo, e_hi = [_dot(a, w2_ref[...]) for a, w2_ref in zip(act, (w2l_ref, w2h_ref))]
            y = w_lo * e_lo + w_hi * e_hi
            for r in range(CHUNK_ROWS):
                rows = slice(r * CHUNKS_PER_TILE, (r + 1) * CHUNKS_PER_TILE)
                for half in range(2):
                    for c in range(n_blk):
                        col = half * HALF_W + c * LANES
                        obuf[slot, c, pl.ds(2 * r + half, CHUNKS_PER_TILE, stride=SLAB_ROWS), :] = (
                            y[rows, col:col + LANES])

        @pl.when(live_ref[i] == 0)
        def _():
            obuf[slot] = jnp.zeros(obuf.shape[1:], F32)

        for j in range(CHUNKS_PER_TILE):
            scatter_copy(i, slot, j).start(priority=j % 2)

        @pl.when(i == n_used - 1)
        def _():
            wait_scatter(slot)

            @pl.when(i >= 1)
            def _():
                wait_scatter(1 - slot)


def _moe(hxs, src, dst, n_used, tile_lo, tile_hi, tile_live, w1_b, w3_b, w2_b):
    n_chunks = hxs.shape[1] // SLAB_ROWS
    tile_rows = CHUNKS_PER_TILE * SLAB_ROWS
    n_steps = src.shape[0] // CHUNKS_PER_TILE
    wspec = lambda which, shape: pl.BlockSpec(
        (None,) + shape, (lambda i, nu, lo, hi, lv, s, d: (lo[i], 0, 0)) if which == 0 else
        (lambda i, nu, lo, hi, lv, s, d: (hi[i], 0, 0)))
    grid_spec = pltpu.PrefetchScalarGridSpec(
        num_scalar_prefetch=6,
        grid=(n_steps,),
        in_specs=[
            pl.BlockSpec(memory_space=pl.ANY),
            wspec(0, (D_MODEL, D_EXPERT)), wspec(0, (D_MODEL, D_EXPERT)), wspec(0, (D_EXPERT, D_MODEL)),
            wspec(1, (D_MODEL, D_EXPERT)), wspec(1, (D_MODEL, D_EXPERT)), wspec(1, (D_EXPERT, D_MODEL)),
        ],
        out_specs=pl.BlockSpec(memory_space=pl.ANY),
        scratch_shapes=[
            pltpu.VMEM((2, SLAB_IN_W // LANES, tile_rows, LANES), F32),
            pltpu.VMEM((2, SLAB_OUT_W // LANES, tile_rows, LANES), F32),
            pltpu.SemaphoreType.DMA((2,)),
            pltpu.SemaphoreType.DMA((2,)),
        ],
    )
    return pl.pallas_call(
        functools.partial(_moe_kernel, n_chunks),
        grid_spec=grid_spec,
        out_shape=jax.ShapeDtypeStruct((SLAB_OUT_W // LANES, (n_chunks + 2 * CHUNKS_PER_TILE) * SLAB_ROWS, LANES), F32),
        compiler_params=_params(1),
        name="moe",
    )(n_used, tile_lo, tile_hi, tile_live, src, dst, hxs, w1_b, w3_b, w2_b, w1_b, w3_b, w2_b)


def _final_kernel(n_sort, x1_ref, moe_ref, pos_ref, mod_ref, lnp_ref, o_ref):
    moe = []
    targets = _slab_targets(1)
    for s in range(n_sort):
        slab_rows = slice(s * LOCAL_SLAB_ROWS, (s + 1) * LOCAL_SLAB_ROWS)
        moe_b = jnp.concatenate([moe_ref[c, slab_rows, :] for c in range(SLAB_OUT_W // LANES)], axis=1).astype(BF16)
        sort_lo, sort_hi = _slab_sort_matrices(pos_ref[s * SORT_TILE:(s + 1) * SORT_TILE, 0:1], targets)
        for r in range(0, SORT_TILE, MIX_SUB_TILE):
            rows = slice(r, r + MIX_SUB_TILE)
            moe.append((s * SORT_TILE + r,
                        jnp.concatenate([_dot(sort_lo[rows, :], moe_b), _dot(sort_hi[rows, :], moe_b)], axis=1)))
    for start, moe_s in moe:
        rows = slice(start, start + MIX_SUB_TILE)
        o_ref[rows, :] = _layer_norm(DEEPNORM_ALPHA * x1_ref[rows, :] + mod_ref[...] * moe_s,
                                     lnp_ref[0:1, :], lnp_ref[1:2, :])


def _final(x1, moe, pos, g2, lnp):
    bsz, t, _ = x1.shape
    n_sort = max(n for n in (1, 2, 4) if n * SORT_TILE <= TOKEN_TILE and t % (n * SORT_TILE) == 0)
    tm = n_sort * SORT_TILE
    n_t = t // tm
    flat = lambda rows, w: pl.BlockSpec((rows, w), lambda b, i: (b * n_t + i, 0))
    return pl.pallas_call(
        functools.partial(_final_kernel, n_sort),
        grid=(bsz, n_t),
        in_specs=[
            pl.BlockSpec((None, tm, D_MODEL), lambda b, i: (b, i, 0)),
            pl.BlockSpec((SLAB_OUT_W // LANES, n_sort * LOCAL_SLAB_ROWS, LANES), lambda b, i: (0, b * n_t + i, 0)),
            flat(tm, ROUTE_W),
            pl.BlockSpec((None, 1, D_MODEL), lambda b, i: (b, 0, 0)),
            pl.BlockSpec(lnp.shape, lambda b, i: (0, 0)),
        ],
        out_specs=pl.BlockSpec((None, tm, D_MODEL), lambda b, i: (b, i, 0)),
        out_shape=jax.ShapeDtypeStruct((bsz, t, D_MODEL), F32),
        compiler_params=_params(2),
        name="final",
    )(x1, moe, pos, g2, lnp)


def _pair_tables():
    lo, hi = [], []
    for g in range(N_GROUPS):
        for a in range(EXPERTS_PER_GROUP):
            for b in range(a + 1, EXPERTS_PER_GROUP):
                lo.append(g * EXPERTS_PER_GROUP + a)
                hi.append(g * EXPERTS_PER_GROUP + b)
    return jnp.array(lo, jnp.int32), jnp.array(hi, jnp.int32)


def _moe_plan(chunks, n_sort_tiles):
    n_cls = N_CLASSES + 1
    hp = lax.Precision.HIGHEST
    m = chunks.reshape(n_sort_tiles, 8, ROUTE_W)[:, 0, :n_cls].astype(jnp.int32)
    a_end = jnp.cumsum(m, axis=0)
    a_start = a_end - m
    per_cls = a_end[-1]
    padded = (per_cls + CHUNKS_PER_TILE - 1) // CHUNKS_PER_TILE * CHUNKS_PER_TILE
    g_end = jnp.cumsum(padded)
    g_start = g_end - padded
    local_off = jnp.cumsum(m, axis=1) - m
    seg = jnp.arange(n_sort_tiles, dtype=jnp.int32)[:, None] * LOCAL_CHUNKS + local_off - a_start
    n_steps = -(-(n_sort_tiles * LOCAL_CHUNKS) // CHUNKS_PER_TILE) + n_cls
    p = jnp.arange(n_steps * CHUNKS_PER_TILE, dtype=jnp.int32)
    cls_p = jnp.minimum(jnp.sum((g_end[None, :] <= p[:, None]).astype(jnp.int32), axis=1), n_cls - 1)
    onehot = (cls_p[:, None] == jnp.arange(n_cls, dtype=jnp.int32)[None, :]).astype(F32)
    pick = lambda tab: jnp.dot(onehot, tab.astype(F32), precision=hp)
    u = p - pick(g_start[:, None])[:, 0].astype(jnp.int32)
    valid = u < pick(per_cls[:, None])[:, 0].astype(jnp.int32)
    a_end_p = pick(a_end.T).astype(jnp.int32)
    seg_p = pick(seg.T).astype(jnp.int32)
    tile_p = jnp.sum((a_end_p <= u[:, None]).astype(jnp.int32), axis=1)
    hit = jnp.arange(n_sort_tiles, dtype=jnp.int32)[None, :] == tile_p[:, None]
    src = jnp.sum(jnp.where(hit, seg_p, 0), axis=1) + u
    pad_dst = n_sort_tiles * LOCAL_CHUNKS + (p // CHUNKS_PER_TILE) % 2 * CHUNKS_PER_TILE + p % CHUNKS_PER_TILE
    dst = jnp.where(valid, src, pad_dst).astype(jnp.int32)
    src = jnp.where(valid, src, 0).astype(jnp.int32)
    n_used = g_end[-1:] // CHUNKS_PER_TILE
    step = jnp.arange(n_steps, dtype=jnp.int32)
    tile_cls = jnp.sum((g_end[None, :] // CHUNKS_PER_TILE <= step[:, None]).astype(jnp.int32), axis=1)
    live = ((tile_cls < N_CLASSES) & (step < n_used[0])).astype(jnp.int32)
    pair_lo, pair_hi = _pair_tables()
    pair_oh = (jnp.minimum(tile_cls, N_CLASSES - 1)[:, None] == jnp.arange(N_CLASSES)[None, :]).astype(jnp.int32)
    tile_lo = jnp.sum(pair_oh * pair_lo[None, :], axis=1).astype(jnp.int32)
    tile_hi = jnp.sum(pair_oh * pair_hi[None, :], axis=1).astype(jnp.int32)
    return src, dst, n_used.astype(jnp.int32), tile_lo, tile_hi, live


def kernel(x, c, ctx, c_ctx, ln_in_g, ln_in_b, w_ada, b_ada, w_in, conv_w, conv_b, gate_w2_fwd, gate_b_fwd,
           gate_w2_bwd, gate_b_bwd, gla_norm_g, w_out, ln1_g, ln1_b, router_group_w, router_group_b,
           router_expert_w, router_expert_b, expert_w1, expert_w3, expert_w2, ln2_g, ln2_b):
    bsz, t, _ = x.shape
    n_tok = bsz * t
    l = 0
    rows = -(-(bsz + 1) // 8) * 8
    cond = jnp.zeros((rows, D_MODEL), F32).at[:bsz].set(c).at[bsz].set(c_ctx)
    ada = _ada(cond, w_ada[l], b_ada[l][None, :])
    sh1, sc1, g1, sh2, sc2, g2 = [ada[:, i * D_MODEL:(i + 1) * D_MODEL] for i in range(6)]

    w_in_b = w_in[l].astype(BF16)
    lnp_in = jnp.stack([ln_in_g, ln_in_b])
    zero = jnp.zeros((GLA_GATE_RANK, GLA_KEY), F32)
    w2cat = jnp.concatenate([jnp.concatenate([gate_w2_fwd[l], zero], axis=1),
                             jnp.concatenate([zero, gate_w2_bwd[l]], axis=1)], axis=0).astype(BF16)
    gbias = jnp.concatenate([gate_b_fwd[l], gate_b_bwd[l]])[None, :]

    n_ctx = ctx.shape[1]
    mod_ctx = jnp.stack([1.0 + sc1[bsz], sh1[bsz]])[None]
    k_c, v_c, g_c = [a.reshape(bsz, n_ctx, -1) for a in _proj(
        ctx.reshape(1, bsz * n_ctx, D_MODEL), mod_ctx, lnp_in, w_in_b, conv_w[l], conv_b[l][None, :], w2cat, gbias,
        False)]
    zero_state = jnp.zeros((bsz, GLA_KEY, PAIR_VAL), F32)
    s_f, s_b = _gla(None, k_c, v_c, g_c, zero_state, zero_state)

    mod1 = jnp.stack([1.0 + sc1[:bsz], sh1[:bsz]], axis=1)
    ya, q, k, v, sr, g, xn = _proj(x, mod1, lnp_in, w_in_b, conv_w[l], conv_b[l][None, :], w2cat, gbias, True)
    o_f, o_b, _, _ = _gla(q, k, v, g, s_f, s_b)

    mod2 = jnp.stack([g1[:bsz], 1.0 + sc2[:bsz], sh2[:bsz]], axis=1)
    lnp1 = jnp.stack([ln1_g[l], ln1_b[l]])
    wr = jnp.zeros((D_MODEL, ROUTE_W), F32)
    wr = wr.at[:, :N_GROUPS].set(router_group_w[l]).at[:, N_GROUPS:N_GROUPS + N_EXPERTS].set(router_expert_w[l])
    br = jnp.zeros((1, ROUTE_W), F32)
    br = br.at[0, :N_GROUPS].set(router_group_b[l]).at[0, N_GROUPS:N_GROUPS + N_EXPERTS].set(router_expert_b[l])
    x1, hxs, pos, chunks = _mix_out(xn, o_f, o_b, sr, ya, mod2, lnp1, gla_norm_g[l][None, :],
                                    w_out[l].astype(BF16), wr.astype(BF16), br)

    src, dst, n_used, tile_lo, tile_hi, live = _moe_plan(chunks, n_tok // SORT_TILE)
    moe = _moe(hxs, src, dst, n_used, tile_lo, tile_hi, live,
               expert_w1[l].astype(BF16), expert_w3[l].astype(BF16), expert_w2[l].astype(BF16))

    return _final(x1, moe, pos, g2[:bsz][:, None, :], jnp.stack([ln2_g[l], ln2_b[l]]))
```

```python
import functools

import jax
import jax.numpy as jnp
from jax import lax
from jax.experimental import pallas as pl
from jax.experimental.pallas import tpu as pltpu

F32 = jnp.float32
BF16 = jnp.bfloat16

D_MODEL = 1024
GRID_W = 64
CONV_CH = 512
GLA_HEADS = 4
GLA_DK = 64
GLA_DV = 128
GLA_KEY = GLA_HEADS * GLA_DK
GLA_VAL = GLA_HEADS * GLA_DV
PAIR_KEY = 2 * GLA_DK
PAIR_VAL = 2 * GLA_DV
GLA_GATE_RANK = 16
GLA_TAU = 16.0
OFF_AB = 0
OFF_AC = OFF_AB + CONV_CH
OFF_AX = OFF_AC + CONV_CH
OFF_Q = OFF_AX + CONV_CH
OFF_K = OFF_Q + GLA_KEY
OFF_V = OFF_K + GLA_KEY
OFF_R = OFF_V + GLA_VAL
OFF_GF = OFF_R + GLA_VAL
D_PROJ = OFF_GF + 2 * GLA_GATE_RANK
N_GROUPS = 4
EXPERTS_PER_GROUP = 4
N_EXPERTS = N_GROUPS * EXPERTS_PER_GROUP
D_EXPERT = 512
PAIRS_PER_GROUP = 6
N_CLASSES = N_GROUPS * PAIRS_PER_GROUP
LN_EPS = 1e-5
RMS_EPS = 1e-6
DEPTH = 1
DEEPNORM_ALPHA = (2.0 * DEPTH) ** 0.25

LANES = 128
GLA_CHUNK = 64
GLA_SUB = 16
N_SUB = GLA_CHUNK // GLA_SUB
ROUTE_W = LANES
HALF_W = D_MODEL // 2
SLAB_IN_W = HALF_W + ROUTE_W
SLAB_OUT_W = HALF_W
ADA_COL_TILE = 1024
TOKEN_TILE = 1024
SORT_TILE = 256
MIX_SUB_TILE = 128
MOE_TILE = 256
CHUNK_ROWS = 4
SLAB_ROWS = 2 * CHUNK_ROWS
LOCAL_CHUNKS = -(-(SORT_TILE + N_CLASSES * (CHUNK_ROWS - 1)) // CHUNK_ROWS)
LOCAL_SLAB_ROWS = LOCAL_CHUNKS * SLAB_ROWS
CHUNKS_PER_TILE = MOE_TILE // CHUNK_ROWS
VMEM_LIMIT = 56 * 1024 * 1024


def _params(n_axes, vmem=VMEM_LIMIT):
    return pltpu.CompilerParams(dimension_semantics=("arbitrary",) * n_axes, vmem_limit_bytes=vmem)


def _dot(a, b):
    return jnp.dot(a, b, preferred_element_type=F32)


def _div_pow2(x, d):
    assert d & (d - 1) == 0
    return lax.shift_right_logical(x, jnp.int32(d.bit_length() - 1))


def _mod_pow2(x, d):
    assert d & (d - 1) == 0
    return lax.bitwise_and(x, jnp.int32(d - 1))


def _split2(x):
    hi = x.astype(BF16)
    lo = (x - hi.astype(F32)).astype(BF16)
    return hi, lo


def _dot3(a, b):
    ah, al = _split2(a)
    bh, bl = _split2(b)
    return _dot(ah, bh) + _dot(ah, bl) + _dot(al, bh)


def _silu(x):
    return x * (0.5 * jnp.tanh(0.5 * x) + 0.5)


def _layer_norm(x, g, b):
    mu = jnp.mean(x, axis=-1, keepdims=True)
    xc = x - mu
    var = jnp.mean(xc * xc, axis=-1, keepdims=True)
    return xc * lax.rsqrt(var + LN_EPS) * g + b


def _ada_kernel(c_ref, w_ref, b_ref, o_ref):
    o_ref[...] = _dot3(_silu(c_ref[...]), w_ref[...]) + b_ref[...]


def _ada(cond, w_ada, b_ada):
    rows = cond.shape[0]
    n_out = w_ada.shape[1]
    tn = ADA_COL_TILE
    return pl.pallas_call(
        _ada_kernel,
        grid=(n_out // tn,),
        in_specs=[
            pl.BlockSpec((rows, D_MODEL), lambda j: (0, 0)),
            pl.BlockSpec((D_MODEL, tn), lambda j: (0, j)),
            pl.BlockSpec((1, tn), lambda j: (0, j)),
        ],
        out_specs=pl.BlockSpec((rows, tn), lambda j: (0, j)),
        out_shape=jax.ShapeDtypeStruct((rows, n_out), F32),
        compiler_params=_params(1),
        name="ada",
    )(cond, w_ada, b_ada)


def _log_sigmoid(z):
    return jnp.minimum(z, 0.0) - jnp.log(1.0 + jnp.exp(-jnp.abs(z)))


def _proj_kernel(latent, tm, x_ref, mod_ref, lnp_ref, w_ref, cw_ref, cb_ref, w2_ref, gbias_ref, *out_refs):
    x = x_ref[...]
    xn = _layer_norm(x, lnp_ref[0:1, :], lnp_ref[1:2, :])
    h = xn * mod_ref[0:1, :] + mod_ref[1:2, :]
    hb = h.astype(BF16)
    if latent:
        ya_ref, q_ref, k_ref, v_ref, sr_ref, g_ref, xn_ref = out_refs
        xn_ref[...] = DEEPNORM_ALPHA * xn
        p = _dot(hb, w_ref[:, OFF_AB:OFF_Q])
        a_b = p[:, 0:CONV_CH]
        u = p[:, CONV_CH:2 * CONV_CH] * p[:, 2 * CONV_CH:3 * CONV_CH]
        pos = _mod_pow2(lax.broadcasted_iota(jnp.int32, (tm, 1), 0), GRID_W)
        u_prev = jnp.where(pos == 0, 0.0, pltpu.roll(u, 1, 0))
        u_next = jnp.where(pos == GRID_W - 1, 0.0, pltpu.roll(u, tm - 1, 0))
        conv = u_prev * cw_ref[0:1, :] + u * cw_ref[1:2, :] + u_next * cw_ref[2:3, :] + cb_ref[...]
        ya_ref[...] = (a_b * conv).astype(BF16)
        qk = _dot(hb, w_ref[:, OFF_Q:OFF_V])
        q_ref[...] = (qk[:, 0:GLA_KEY] * (GLA_DK ** -0.5)).astype(BF16)
        k_ref[...] = qk[:, GLA_KEY:].astype(BF16)
        r = _dot(hb, w_ref[:, OFF_R:OFF_GF])
        sr_ref[...] = _silu(r).astype(BF16)
    else:
        k_ref, v_ref, g_ref = out_refs
        k_ref[...] = _dot(hb, w_ref[:, OFF_K:OFF_V]).astype(BF16)
    v_ref[...] = _dot(hb, w_ref[:, OFF_V:OFF_R]).astype(BF16)
    low = _dot(hb, w_ref[:, OFF_GF:D_PROJ])
    z = _dot(low.astype(BF16), w2_ref[...]) + gbias_ref[...]
    g_ref[...] = _log_sigmoid(z) * (1.0 / GLA_TAU)


def _proj(x, mod, lnp, w_in_b, conv_w, conv_b, w2cat, gbias, latent):
    bsz, t, _ = x.shape
    tm = min(TOKEN_TILE, t)
    assert t % tm == 0 and tm % GRID_W == 0
    tok = lambda w: pl.BlockSpec((None, tm, w), lambda b, i: (b, i, 0))
    full = lambda a: pl.BlockSpec(a.shape, lambda b, i: (0,) * a.ndim)
    widths = ([(CONV_CH, BF16), (GLA_KEY, BF16)] if latent else []) + [(GLA_KEY, BF16), (GLA_VAL, BF16)]
    widths += ([(GLA_VAL, BF16)] if latent else []) + [(2 * GLA_KEY, F32)]
    widths += [(D_MODEL, F32)] if latent else []
    return pl.pallas_call(
        functools.partial(_proj_kernel, latent, tm),
        grid=(bsz, t // tm),
        in_specs=[
            tok(D_MODEL),
            pl.BlockSpec((None, 2, D_MODEL), lambda b, i: (b, 0, 0)),
            full(lnp), full(w_in_b), full(conv_w), full(conv_b), full(w2cat), full(gbias),
        ],
        out_specs=[tok(w) for w, _ in widths],
        out_shape=[jax.ShapeDtypeStruct((bsz, t, w), dt) for w, dt in widths],
        compiler_params=_params(2),
        name="proj_latent" if latent else "proj_ctx",
    )(x, mod, lnp, w_in_b, conv_w, conv_b, w2cat, gbias)


def _tri(n, reverse, strict=False):
    i = lax.broadcasted_iota(jnp.int32, (n, n), 0)
    j = lax.broadcasted_iota(jnp.int32, (n, n), 1)
    if strict:
        m = (j > i) if reverse else (j < i)
    else:
        m = (j >= i) if reverse else (j <= i)
    return jnp.where(m, 1.0, 0.0).astype(BF16)


def _chunk_cumsum(g, reverse):
    tri = _tri(GLA_CHUNK, reverse)
    g_hi, g_lo = _split2(g)
    return _dot(tri, g_hi) + _dot(tri, g_lo)


def _as_column(row):
    return jnp.broadcast_to(row, (LANES, row.shape[1])).T


def _sub_anchors(gc, reverse):
    zero = jnp.zeros((1, GLA_KEY), F32)
    if reverse:
        return [gc[GLA_SUB * (a + 1):GLA_SUB * (a + 1) + 1] for a in range(N_SUB - 1)] + [zero]
    return [zero] + [gc[GLA_SUB * a - 1:GLA_SUB * a] for a in range(1, N_SUB)]


def _score_pairs(reverse):
    return [(a, b) for a in range(N_SUB) for b in range(N_SUB) if (b >= a if reverse else b <= a)]


def _intra_products(q, k, gc, reverse):
    r = _sub_anchors(gc, reverse)
    anchor = jnp.concatenate([jnp.broadcast_to(ra, (GLA_SUB, GLA_KEY)) for ra in r], axis=0)
    gcb = gc - anchor
    qt = q * jnp.exp(gcb)
    kt = k * jnp.exp(-gcb)
    rows = []
    for a, b in _score_pairs(reverse):
        qa = qt[GLA_SUB * a:GLA_SUB * (a + 1)]
        if a != b:
            qa = qa * jnp.exp(r[a] - r[b])
        rows.append(qa)
    qp = jnp.concatenate(rows, axis=0).astype(BF16)
    width = GLA_HEADS * GLA_CHUNK
    rr = lax.broadcasted_iota(jnp.int32, (width, GLA_KEY), 0)
    cc = lax.broadcasted_iota(jnp.int32, (width, GLA_KEY), 1)
    kbd = jnp.where(_div_pow2(rr, GLA_CHUNK) == _div_pow2(cc, GLA_DK),
                    jnp.concatenate([kt] * GLA_HEADS, axis=0), 0.0)
    return lax.dot_general(qp, kbd.astype(BF16), (((1,), (1,)), ((), ())), preferred_element_type=F32)


def _assemble_scores(res, reverse):
    pairs = _score_pairs(reverse)
    width = GLA_HEADS * GLA_CHUNK
    col = _mod_pow2(lax.broadcasted_iota(jnp.int32, (GLA_SUB, width), 1), GLA_CHUNK)
    col_blk = _div_pow2(col, GLA_SUB)
    col_pos = _mod_pow2(col, GLA_SUB)
    row_pos = lax.broadcasted_iota(jnp.int32, (GLA_SUB, width), 0)
    causal = (col_pos >= row_pos) if reverse else (col_pos <= row_pos)
    blocks = []
    for a in range(N_SUB):
        acc = jnp.zeros((GLA_SUB, width), F32)
        for idx, (pa, pb) in enumerate(pairs):
            if pa != a:
                continue
            keep = col_blk == pb
            if pa == pb:
                keep = keep & causal
            acc = acc + jnp.where(keep, res[GLA_SUB * idx:GLA_SUB * (idx + 1)], 0.0)
        blocks.append(acc)
    return jnp.concatenate(blocks, axis=0)


def _pair_mask(rows_per_head, cols_per_head, n_row_pairs=1):
    shape = (n_row_pairs * 2 * rows_per_head, 2 * cols_per_head)
    rr = _mod_pow2(lax.broadcasted_iota(jnp.int32, shape, 0), 2 * rows_per_head)
    cc = lax.broadcasted_iota(jnp.int32, shape, 1)
    return _div_pow2(rr, rows_per_head) == _div_pow2(cc, cols_per_head)


def _state_terms(q, k, v_b, gc, reverse):
    total = gc[0:1] if reverse else gc[GLA_CHUNK - 1:GLA_CHUNK]
    q_dec = None if q is None else (q * jnp.exp(gc)).astype(BF16)
    k_end = (k * jnp.exp(total - gc)).astype(BF16)
    tn = (((0,), (0,)), ((), ()))
    upd = [lax.dot_general(k_end[:, p * PAIR_KEY:(p + 1) * PAIR_KEY], v_b[:, p * PAIR_VAL:(p + 1) * PAIR_VAL], tn,
                           preferred_element_type=F32) for p in range(GLA_HEADS // 2)]
    upd = jnp.where(_pair_mask(GLA_DK, GLA_DV, GLA_HEADS // 2), jnp.concatenate(upd, axis=0), 0.0)
    decay = jnp.exp(_as_column(total))
    decay = jnp.concatenate([decay] * (PAIR_VAL // LANES), axis=1)
    return q_dec, decay, upd


def _advance_state(q_dec, decay, upd, state):
    o_inter = None
    if q_dec is not None:
        state_b = state.astype(BF16)
        o_inter = jnp.concatenate(
            [_dot(q_dec[:, p * PAIR_KEY:(p + 1) * PAIR_KEY], state_b[p * PAIR_KEY:(p + 1) * PAIR_KEY, :])
             for p in range(GLA_HEADS // 2)], axis=1)
    return o_inter, state * decay + upd


def _gla_kernel(tt, nt, with_out, *refs):
    if with_out:
        (qf_ref, kf_ref, vf_ref, gfw_ref, qb_ref, kb_ref, vb_ref, gbw_ref, s0f_ref, s0b_ref,
         of_ref, ob_ref, sf_ref, sb_ref) = refs
    else:
        kf_ref, vf_ref, gfw_ref, kb_ref, vb_ref, gbw_ref, s0f_ref, s0b_ref, sf_ref, sb_ref = refs
    j = pl.program_id(1)

    @pl.when(j == 0)
    def _():
        sf_ref[...] = s0f_ref[...]
        sb_ref[...] = s0b_ref[...]

    chunk_slices = [slice(c * GLA_CHUNK, (c + 1) * GLA_CHUNK) for c in range(tt // GLA_CHUNK)]
    fwd, bwd = [], []
    for sl in chunk_slices:
        g = gfw_ref[sl, :]
        fwd.append(dict(sl=sl, k=kf_ref[sl, :].astype(F32), v=vf_ref[sl, :], g_f=g[:, 0:GLA_KEY], g_b=g[:, GLA_KEY:],
                        q=qf_ref[sl, :].astype(F32) if with_out else None))
        bwd.append(dict(sl=sl, k=kb_ref[sl, :].astype(F32), v=vb_ref[sl, :], g_b=gbw_ref[sl, :][:, GLA_KEY:],
                        q=qb_ref[sl, :].astype(F32) if with_out else None))
    for d in fwd:
        d["gc_f"] = _chunk_cumsum(d["g_f"], False)
        if with_out:
            d["gc_b"] = _chunk_cumsum(d["g_b"], True)
    for d in bwd:
        d["gc_b"] = _chunk_cumsum(d["g_b"], True)
    if with_out:
        for d in fwd:
            d["res_f"] = _intra_products(d["q"], d["k"], d["gc_f"], False)
            d["res_b"] = _intra_products(d["q"], d["k"], d["gc_b"], True)
    for d in fwd:
        d["terms"] = _state_terms(d["q"], d["k"], d["v"], d["gc_f"], False)
    for d in bwd:
        d["terms"] = _state_terms(d["q"], d["k"], d["v"], d["gc_b"], True)
    if with_out:
        for d in fwd:
            scores = (_assemble_scores(d["res_f"], False) + _assemble_scores(d["res_b"], True)).astype(BF16)
            o_intra = []
            for p in range(GLA_HEADS // 2):
                v_p = d["v"][:, p * PAIR_VAL:(p + 1) * PAIR_VAL]
                vbd = jnp.where(_pair_mask(GLA_CHUNK, GLA_DV), jnp.concatenate([v_p, v_p], axis=0),
                                jnp.zeros((), BF16))
                o_intra.append(_dot(scores[:, p * 2 * GLA_CHUNK:(p + 1) * 2 * GLA_CHUNK], vbd))
            d["o_intra"] = jnp.concatenate(o_intra, axis=1)

    state = sf_ref[...]
    for d in fwd:
        o_inter, state = _advance_state(*d["terms"], state)
        if with_out:
            of_ref[d["sl"], :] = (d["o_intra"] + o_inter).astype(BF16)
    sf_ref[...] = state

    state = sb_ref[...]
    for d in reversed(bwd):
        o_inter, state = _advance_state(*d["terms"], state)
        if with_out:
            ob_ref[d["sl"], :] = o_inter.astype(BF16)
    sb_ref[...] = state


def _gla(q, k, v, g, s0f, s0b):
    with_out = q is not None
    bsz, t, _ = k.shape
    tt = min(TOKEN_TILE, t)
    assert t % tt == 0 and tt % GLA_CHUNK == 0
    nt = t // tt
    fwd = lambda w: pl.BlockSpec((None, tt, w), lambda b, j: (b, j, 0))
    bwd = lambda w: pl.BlockSpec((None, tt, w), lambda b, j: (b, nt - 1 - j, 0))
    st = pl.BlockSpec((None, GLA_KEY, PAIR_VAL), lambda b, j: (b, 0, 0))
    st_shape = jax.ShapeDtypeStruct((bsz, GLA_KEY, PAIR_VAL), F32)
    if with_out:
        ins = [q, k, v, g, q, k, v, g, s0f, s0b]
        in_specs = [fwd(GLA_KEY), fwd(GLA_KEY), fwd(GLA_VAL), fwd(2 * GLA_KEY),
                    bwd(GLA_KEY), bwd(GLA_KEY), bwd(GLA_VAL), bwd(2 * GLA_KEY), st, st]
        out_specs = [fwd(GLA_VAL), bwd(GLA_VAL), st, st]
        o_shape = jax.ShapeDtypeStruct((bsz, t, GLA_VAL), BF16)
        out_shape = [o_shape, o_shape, st_shape, st_shape]
    else:
        ins = [k, v, g, k, v, g, s0f, s0b]
        in_specs = [fwd(GLA_KEY), fwd(GLA_VAL), fwd(2 * GLA_KEY),
                    bwd(GLA_KEY), bwd(GLA_VAL), bwd(2 * GLA_KEY), st, st]
        out_specs = [st, st]
        out_shape = [st_shape, st_shape]
    return pl.pallas_call(
        functools.partial(_gla_kernel, tt, nt, with_out),
        grid=(bsz, nt),
        in_specs=in_specs,
        out_specs=out_specs,
        out_shape=out_shape,
        compiler_params=_params(2),
        name="gla_latent" if with_out else "gla_ctx",
    )(*ins)


def _exact_bf16_parts(x):
    hi = x.astype(BF16).astype(F32)
    r = x - hi
    mid = r.astype(BF16).astype(F32)
    lo = (r - mid).astype(BF16).astype(F32)
    return hi, mid, lo


def _first_index(values, best):
    idx = jnp.full_like(best, float(len(values) - 1))
    for i in reversed(range(len(values) - 1)):
        idx = jnp.where(values[i] >= best, float(i), idx)
    return idx


def _pick(rows, idx):
    out = rows[-1]
    for i in reversed(range(len(rows) - 1)):
        out = jnp.where(idx == float(i), rows[i], out)
    return out


def _route(logit_t, tm):
    row = lambda r: logit_t[r:r + 1, :]
    groups = [row(i) for i in range(N_GROUPS)]
    top = functools.reduce(jnp.maximum, groups)
    eg = [jnp.exp(x - top) for x in groups]
    total = functools.reduce(lambda a, b: a + b, eg)
    pg = [e / total for e in eg]
    p_g = functools.reduce(jnp.maximum, pg)
    g_idx = _first_index(pg, p_g)
    sel = [_pick([row(N_GROUPS + EXPERTS_PER_GROUP * g + j) for g in range(N_GROUPS)], g_idx)
           for j in range(EXPERTS_PER_GROUP)]
    top = functools.reduce(jnp.maximum, sel)
    ee = [jnp.exp(x - top) for x in sel]
    total = functools.reduce(lambda a, b: a + b, ee)
    pe = [e / total for e in ee]
    p1 = functools.reduce(jnp.maximum, pe)
    l1 = _first_index(pe, p1)
    pe2 = [jnp.where(l1 == float(j), -1.0, pe[j]) for j in range(EXPERTS_PER_GROUP)]
    p2 = functools.reduce(jnp.maximum, pe2)
    l2 = _first_index(pe2, p2)
    den = p1 + p2
    w1 = p1 / den * p_g
    w2 = p2 / den * p_g
    lo = jnp.minimum(l1, l2)
    hi = jnp.maximum(l1, l2)
    pair = lo * (7.0 - lo) * 0.5 + (hi - lo - 1.0)
    cls = g_idx * PAIRS_PER_GROUP + pair
    w_lo = jnp.where(l1 < l2, w1, w2)
    w_hi = jnp.where(l1 < l2, w2, w1)
    cls_id = lax.broadcasted_iota(jnp.int32, (ROUTE_W, tm), 0).astype(F32)
    onehot = jnp.where(cls_id == cls, 1.0, 0.0)
    before = _dot(onehot.astype(BF16), _tri(tm, True, strict=True))
    count = jnp.sum(onehot, axis=1, keepdims=True)
    chunks = jnp.floor((count + (CHUNK_ROWS - 1.0)) * (1.0 / CHUNK_ROWS))
    chunks = jnp.where(cls_id[:, 0:1] == float(N_CLASSES),
                       LOCAL_CHUNKS - jnp.sum(chunks, axis=0, keepdims=True), chunks)
    first_chunk = _dot(_tri(ROUTE_W, False, strict=True),
                       jnp.broadcast_to(chunks, (ROUTE_W, LANES)).astype(BF16))[:, 0:1]
    pos_row = jnp.sum(onehot * (CHUNK_ROWS * first_chunk + before), axis=0, keepdims=True)
    return pos_row, w_lo, w_hi, chunks


def _slab_targets(slab_axis):
    shape = (LOCAL_SLAB_ROWS, 1) if slab_axis == 0 else (1, LOCAL_SLAB_ROWS)
    slab_row = lax.broadcasted_iota(jnp.int32, shape, slab_axis)
    sub = _mod_pow2(slab_row, SLAB_ROWS)
    token_row = (CHUNK_ROWS * _div_pow2(slab_row, SLAB_ROWS) + _div_pow2(sub, 2)).astype(F32)
    half = _mod_pow2(sub, 2)
    return [jnp.where(half == h, token_row, -1.0) for h in range(2)]


def _slab_sort_matrices(pos, targets):
    return [jnp.where(t == pos, 1.0, 0.0).astype(BF16) for t in targets]


def _mix_out_kernel(tm, xn_ref, of_ref, ob_ref, sr_ref, ya_ref, mod_ref, lnp_ref, gn_ref, wo_ref, wr_ref, br_ref,
                    x1_ref, hxs_ref, pos_ref, chunks_ref):
    subs = [slice(s, s + MIX_SUB_TILE) for s in range(0, tm, MIX_SUB_TILE)]
    yb = []
    for rows in subs:
        o = of_ref[rows, :].astype(F32) + ob_ref[rows, :].astype(F32)
        sr = sr_ref[rows, :].astype(F32)
        heads = []
        for h in range(GLA_HEADS):
            sl = slice(h * GLA_DV, (h + 1) * GLA_DV)
            oh = o[:, sl]
            ms = jnp.mean(oh * oh, axis=-1, keepdims=True)
            heads.append((oh * lax.rsqrt(ms + RMS_EPS) * gn_ref[...] * sr[:, sl]).astype(BF16))
        yb.append(jnp.concatenate([ya_ref[rows, :]] + heads, axis=1))
    xn = [xn_ref[rows, :] for rows in subs]
    y = [_dot(y_in, wo_ref[...]) for y_in in yb]
    h2_b = []
    for rows, xn_s, y_s in zip(subs, xn, y):
        x1 = _layer_norm(xn_s + mod_ref[0:1, :] * y_s, lnp_ref[0:1, :], lnp_ref[1:2, :])
        x1_ref[rows, :] = x1
        h2_b.append((x1 * mod_ref[1:2, :] + mod_ref[2:3, :]).astype(BF16))
    logit_t = [(_dot(h2_s, wr_ref[...]) + br_ref[...]).T for h2_s in h2_b]
    per_sort = SORT_TILE // MIX_SUB_TILE
    routes = []
    for s in range(tm // SORT_TILE):
        routes.append(_route(jnp.concatenate(logit_t[s * per_sort:(s + 1) * per_sort], axis=1), SORT_TILE))
    rec_id = lax.broadcasted_iota(jnp.int32, (ROUTE_W, SORT_TILE), 0)
    targets = _slab_targets(0)
    for s, (pos_row, w_lo, w_hi, chunks) in enumerate(routes):
        rec_t = jnp.zeros((ROUTE_W, SORT_TILE), F32)
        for i, part in enumerate(_exact_bf16_parts(w_lo) + _exact_bf16_parts(w_hi)):
            rec_t = jnp.where(rec_id == i, part, rec_t)
        rec_b = rec_t.T.astype(BF16)
        h2_s = jnp.concatenate(h2_b[s * per_sort:(s + 1) * per_sort], axis=0)
        sort_lo, sort_hi = _slab_sort_matrices(pos_row, targets)
        pay_lo = jnp.concatenate([h2_s[:, 0:HALF_W], rec_b], axis=1)
        pay_hi = jnp.concatenate([h2_s[:, HALF_W:], jnp.zeros((SORT_TILE, ROUTE_W), BF16)], axis=1)
        slabs = _dot(sort_lo, pay_lo) + _dot(sort_hi, pay_hi)
        for c in range(SLAB_IN_W // LANES):
            hxs_ref[c, s * LOCAL_SLAB_ROWS:(s + 1) * LOCAL_SLAB_ROWS, :] = slabs[:, c * LANES:(c + 1) * LANES]
        pos_ref[8 * s:8 * (s + 1), :] = jnp.broadcast_to(pos_row, (8, SORT_TILE))
        chunks_ref[8 * s:8 * (s + 1), :] = jnp.broadcast_to(chunks, (ROUTE_W, LANES)).T[0:8, :]


def _mix_out(xn, o_f, o_b, sr, ya, mod, lnp, gn, w_out_b, wr, br):
    bsz, t, _ = xn.shape
    assert t % SORT_TILE == 0
    n_sort = max(n for n in (1, 2, 4) if n * SORT_TILE <= TOKEN_TILE and t % (n * SORT_TILE) == 0)
    tm = n_sort * SORT_TILE
    n_t = t // tm
    tok = lambda w: pl.BlockSpec((None, tm, w), lambda b, i: (b, i, 0))
    full = lambda a: pl.BlockSpec(a.shape, lambda b, i: (0,) * a.ndim)
    flat = lambda rows, w: pl.BlockSpec((rows, w), lambda b, i: (b * n_t + i, 0))
    return pl.pallas_call(
        functools.partial(_mix_out_kernel, tm),
        grid=(bsz, n_t),
        in_specs=[
            tok(D_MODEL), tok(GLA_VAL), tok(GLA_VAL), tok(GLA_VAL), tok(CONV_CH),
            pl.BlockSpec((None, 3, D_MODEL), lambda b, i: (b, 0, 0)),
            full(lnp), full(gn), full(w_out_b), full(wr), full(br),
        ],
        out_specs=[tok(D_MODEL),
                   pl.BlockSpec((SLAB_IN_W // LANES, n_sort * LOCAL_SLAB_ROWS, LANES),
                                lambda b, i: (0, b * n_t + i, 0)),
                   flat(n_sort * 8, SORT_TILE), flat(n_sort * 8, ROUTE_W)],
        out_shape=[
            jax.ShapeDtypeStruct((bsz, t, D_MODEL), F32),
            jax.ShapeDtypeStruct((SLAB_IN_W // LANES, bsz * t // SORT_TILE * LOCAL_SLAB_ROWS, LANES), F32),
            jax.ShapeDtypeStruct((bsz * t // SORT_TILE * 8, SORT_TILE), F32),
            jax.ShapeDtypeStruct((bsz * t // SORT_TILE * 8, ROUTE_W), F32),
        ],
        compiler_params=_params(2),
        name="mix_out",
    )(xn, o_f, o_b, sr, ya, mod, lnp, gn, w_out_b, wr, br)


def _moe_kernel(n_chunks, nused_ref, lo_ref, hi_ref, live_ref, src_ref, dst_ref,
                hxs_hbm, w1l_ref, w3l_ref, w2l_ref, w1h_ref, w3h_ref, w2h_ref, out_hbm, gbuf, obuf, gsem, ssem):
    tile_rows = CHUNKS_PER_TILE * SLAB_ROWS
    i = pl.program_id(0)
    n_used = nused_ref[0]
    slot = lax.bitwise_and(i, 1)

    def slab(chunk):
        return pl.ds(pl.multiple_of(chunk * SLAB_ROWS, SLAB_ROWS), SLAB_ROWS)

    def gather_copy(tile, buf_slot, j):
        chunk = src_ref[tile * CHUNKS_PER_TILE + j]
        return pltpu.make_async_copy(hxs_hbm.at[:, slab(chunk), :], gbuf.at[buf_slot, :, slab(j), :],
                                     gsem.at[buf_slot])

    def scatter_copy(tile, buf_slot, j):
        chunk = dst_ref[tile * CHUNKS_PER_TILE + j]
        return pltpu.make_async_copy(obuf.at[buf_slot, :, slab(j), :], out_hbm.at[:, slab(chunk), :],
                                     ssem.at[buf_slot])

    def start_gather(tile, buf_slot):
        for j in range(CHUNKS_PER_TILE):
            gather_copy(tile, buf_slot, j).start(priority=j % 2)

    def wait_gather(buf_slot):
        pltpu.make_async_copy(hxs_hbm.at[:, pl.ds(0, tile_rows), :], gbuf.at[buf_slot], gsem.at[buf_slot]).wait()

    def wait_scatter(buf_slot):
        pltpu.make_async_copy(obuf.at[buf_slot], out_hbm.at[:, pl.ds(0, tile_rows), :], ssem.at[buf_slot]).wait()

    @pl.when(i == 0)
    def _():
        start_gather(0, 0)
        obuf[...] = jnp.zeros(obuf.shape, F32)
        for s in range(2):
            fill = pltpu.make_async_copy(
                obuf.at[s], out_hbm.at[:, pl.ds((n_chunks + s * CHUNKS_PER_TILE) * SLAB_ROWS, tile_rows), :],
                ssem.at[s])
            fill.start()
            fill.wait()

    @pl.when(i + 1 < n_used)
    def _():
        start_gather(i + 1, 1 - slot)

    @pl.when(i < n_used)
    def _():
        wait_gather(slot)

        @pl.when(i >= 2)
        def _():
            wait_scatter(slot)

        @pl.when(live_ref[i] != 0)
        def _():
            def lane_block(c, half):
                return jnp.concatenate(
                    [gbuf[slot, c, pl.ds(2 * r + half, CHUNKS_PER_TILE, stride=SLAB_ROWS), :]
                     for r in range(CHUNK_ROWS)], axis=0)

            n_blk = HALF_W // LANES
            xb = jnp.concatenate([lane_block(c, 0) for c in range(n_blk)]
                                 + [lane_block(c, 1) for c in range(n_blk)], axis=1).astype(BF16)
            rec = lane_block(n_blk, 0)
            w_lo = rec[:, 0:1] + rec[:, 1:2] + rec[:, 2:3]
            w_hi = rec[:, 3:4] + rec[:, 4:5] + rec[:, 5:6]

            gate = [_dot(xb, w1_ref[...]) for w1_ref in (w1l_ref, w1h_ref)]
            up = [_dot(xb, w3_ref[...]) for w3_ref in (w3l_ref, w3h_ref)]
            act = [(_silu(g) * u).astype(BF16) for g, u in zip(gate, up)]
            e_lo, e_hi = [_dot(a, w2_ref[...]) for a, w2_ref in zip(act, (w2l_ref, w2h_ref))]
            y = w_lo * e_lo + w_hi * e_hi
            for r in range(CHUNK_ROWS):
                rows = slice(r * CHUNKS_PER_TILE, (r + 1) * CHUNKS_PER_TILE)
                for half in range(2):
                    for c in range(n_blk):
                        col = half * HALF_W + c * LANES
                        obuf[slot, c, pl.ds(2 * r + half, CHUNKS_PER_TILE, stride=SLAB_ROWS), :] = (
                            y[rows, col:col + LANES])

        @pl.when(live_ref[i] == 0)
        def _():
            obuf[slot] = jnp.zeros(obuf.shape[1:], F32)

        for j in range(CHUNKS_PER_TILE):
            scatter_copy(i, slot, j).start(priority=j % 2)

        @pl.when(i == n_used - 1)
        def _():
            wait_scatter(slot)

            @pl.when(i >= 1)
            def _():
                wait_scatter(1 - slot)


def _moe(hxs, src, dst, n_used, tile_lo, tile_hi, tile_live, w1_b, w3_b, w2_b):
    n_chunks = hxs.shape[1] // SLAB_ROWS
    tile_rows = CHUNKS_PER_TILE * SLAB_ROWS
    n_steps = src.shape[0] // CHUNKS_PER_TILE
    wspec = lambda which, shape: pl.BlockSpec(
        (None,) + shape, (lambda i, nu, lo, hi, lv, s, d: (lo[i], 0, 0)) if which == 0 else
        (lambda i, nu, lo, hi, lv, s, d: (hi[i], 0, 0)))
    grid_spec = pltpu.PrefetchScalarGridSpec(
        num_scalar_prefetch=6,
        grid=(n_steps,),
        in_specs=[
            pl.BlockSpec(memory_space=pl.ANY),
            wspec(0, (D_MODEL, D_EXPERT)), wspec(0, (D_MODEL, D_EXPERT)), wspec(0, (D_EXPERT, D_MODEL)),
            wspec(1, (D_MODEL, D_EXPERT)), wspec(1, (D_MODEL, D_EXPERT)), wspec(1, (D_EXPERT, D_MODEL)),
        ],
        out_specs=pl.BlockSpec(memory_space=pl.ANY),
        scratch_shapes=[
            pltpu.VMEM((2, SLAB_IN_W // LANES, tile_rows, LANES), F32),
            pltpu.VMEM((2, SLAB_OUT_W // LANES, tile_rows, LANES), F32),
            pltpu.SemaphoreType.DMA((2,)),
            pltpu.SemaphoreType.DMA((2,)),
        ],
    )
    return pl.pallas_call(
        functools.partial(_moe_kernel, n_chunks),
        grid_spec=grid_spec,
        out_shape=jax.ShapeDtypeStruct((SLAB_OUT_W // LANES, (n_chunks + 2 * CHUNKS_PER_TILE) * SLAB_ROWS, LANES), F32),
        compiler_params=_params(1),
        name="moe",
    )(n_used, tile_lo, tile_hi, tile_live, src, dst, hxs, w1_b, w3_b, w2_b, w1_b, w3_b, w2_b)


def _final_kernel(n_sort, x1_ref, moe_ref, pos_ref, mod_ref, lnp_ref, o_ref):
    moe = []
    targets = _slab_targets(1)
    for s in range(n_sort):
        slab_rows = slice(s * LOCAL_SLAB_ROWS, (s + 1) * LOCAL_SLAB_ROWS)
        moe_b = jnp.concatenate([moe_ref[c, slab_rows, :] for c in range(SLAB_OUT_W // LANES)], axis=1).astype(BF16)
        sort_lo, sort_hi = _slab_sort_matrices(_as_column(pos_ref[8 * s:8 * s + 1, :])[:, 0:1], targets)
        for r in range(0, SORT_TILE, MIX_SUB_TILE):
            rows = slice(r, r + MIX_SUB_TILE)
            moe.append((s * SORT_TILE + r,
                        jnp.concatenate([_dot(sort_lo[rows, :], moe_b), _dot(sort_hi[rows, :], moe_b)], axis=1)))
    for start, moe_s in moe:
        rows = slice(start, start + MIX_SUB_TILE)
        o_ref[rows, :] = _layer_norm(DEEPNORM_ALPHA * x1_ref[rows, :] + mod_ref[...] * moe_s,
                                     lnp_ref[0:1, :], lnp_ref[1:2, :])


def _final(x1, moe, pos, g2, lnp):
    bsz, t, _ = x1.shape
    n_sort = max(n for n in (1, 2, 4) if n * SORT_TILE <= TOKEN_TILE and t % (n * SORT_TILE) == 0)
    tm = n_sort * SORT_TILE
    n_t = t // tm
    flat = lambda rows, w: pl.BlockSpec((rows, w), lambda b, i: (b * n_t + i, 0))
    return pl.pallas_call(
        functools.partial(_final_kernel, n_sort),
        grid=(bsz, n_t),
        in_specs=[
            pl.BlockSpec((None, tm, D_MODEL), lambda b, i: (b, i, 0)),
            pl.BlockSpec((SLAB_OUT_W // LANES, n_sort * LOCAL_SLAB_ROWS, LANES), lambda b, i: (0, b * n_t + i, 0)),
            flat(n_sort * 8, SORT_TILE),
            pl.BlockSpec((None, 1, D_MODEL), lambda b, i: (b, 0, 0)),
            pl.BlockSpec(lnp.shape, lambda b, i: (0, 0)),
        ],
        out_specs=pl.BlockSpec((None, tm, D_MODEL), lambda b, i: (b, i, 0)),
        out_shape=jax.ShapeDtypeStruct((bsz, t, D_MODEL), F32),
        compiler_params=_params(2),
        name="final",
    )(x1, moe, pos, g2, lnp)


def _pair_tables():
    lo, hi = [], []
    for g in range(N_GROUPS):
        for a in range(EXPERTS_PER_GROUP):
            for b in range(a + 1, EXPERTS_PER_GROUP):
                lo.append(g * EXPERTS_PER_GROUP + a)
                hi.append(g * EXPERTS_PER_GROUP + b)
    return jnp.array(lo, jnp.int32), jnp.array(hi, jnp.int32)


def _moe_plan(chunks, n_sort_tiles):
    n_cls = N_CLASSES + 1
    hp = lax.Precision.HIGHEST
    m = chunks.reshape(n_sort_tiles, 8, ROUTE_W)[:, 0, :n_cls].astype(jnp.int32)
    a_end = jnp.cumsum(m, axis=0)
    a_start = a_end - m
    per_cls = a_end[-1]
    padded = (per_cls + CHUNKS_PER_TILE - 1) // CHUNKS_PER_TILE * CHUNKS_PER_TILE
    g_end = jnp.cumsum(padded)
    g_start = g_end - padded
    local_off = jnp.cumsum(m, axis=1) - m
    seg = jnp.arange(n_sort_tiles, dtype=jnp.int32)[:, None] * LOCAL_CHUNKS + local_off - a_start
    n_steps = -(-(n_sort_tiles * LOCAL_CHUNKS) // CHUNKS_PER_TILE) + n_cls
    p = jnp.arange(n_steps * CHUNKS_PER_TILE, dtype=jnp.int32)
    cls_p = jnp.minimum(jnp.sum((g_end[None, :] <= p[:, None]).astype(jnp.int32), axis=1), n_cls - 1)
    onehot = (cls_p[:, None] == jnp.arange(n_cls, dtype=jnp.int32)[None, :]).astype(F32)
    pick = lambda tab: jnp.dot(onehot, tab.astype(F32), precision=hp)
    u = p - pick(g_start[:, None])[:, 0].astype(jnp.int32)
    valid = u < pick(per_cls[:, None])[:, 0].astype(jnp.int32)
    a_end_p = pick(a_end.T).astype(jnp.int32)
    seg_p = pick(seg.T).astype(jnp.int32)
    tile_p = jnp.sum((a_end_p <= u[:, None]).astype(jnp.int32), axis=1)
    hit = jnp.arange(n_sort_tiles, dtype=jnp.int32)[None, :] == tile_p[:, None]
    src = jnp.sum(jnp.where(hit, seg_p, 0), axis=1) + u
    pad_dst = n_sort_tiles * LOCAL_CHUNKS + (p // CHUNKS_PER_TILE) % 2 * CHUNKS_PER_TILE + p % CHUNKS_PER_TILE
    dst = jnp.where(valid, src, pad_dst).astype(jnp.int32)
    src = jnp.where(valid, src, 0).astype(jnp.int32)
    n_used = g_end[-1:] // CHUNKS_PER_TILE
    step = jnp.arange(n_steps, dtype=jnp.int32)
    tile_cls = jnp.sum((g_end[None, :] // CHUNKS_PER_TILE <= step[:, None]).astype(jnp.int32), axis=1)
    live = ((tile_cls < N_CLASSES) & (step < n_used[0])).astype(jnp.int32)
    pair_lo, pair_hi = _pair_tables()
    pair_oh = (jnp.minimum(tile_cls, N_CLASSES - 1)[:, None] == jnp.arange(N_CLASSES)[None, :]).astype(jnp.int32)
    tile_lo = jnp.sum(pair_oh * pair_lo[None, :], axis=1).astype(jnp.int32)
    tile_hi = jnp.sum(pair_oh * pair_hi[None, :], axis=1).astype(jnp.int32)
    return src, dst, n_used.astype(jnp.int32), tile_lo, tile_hi, live


def kernel(x, c, ctx, c_ctx, ln_in_g, ln_in_b, w_ada, b_ada, w_in, conv_w, conv_b, gate_w2_fwd, gate_b_fwd,
           gate_w2_bwd, gate_b_bwd, gla_norm_g, w_out, ln1_g, ln1_b, router_group_w, router_group_b,
           router_expert_w, router_expert_b, expert_w1, expert_w3, expert_w2, ln2_g, ln2_b):
    bsz, t, _ = x.shape
    n_tok = bsz * t
    l = 0
    rows = -(-(bsz + 1) // 8) * 8
    cond = jnp.zeros((rows, D_MODEL), F32).at[:bsz].set(c).at[bsz].set(c_ctx)
    ada = _ada(cond, w_ada[l], b_ada[l][None, :])
    sh1, sc1, g1, sh2, sc2, g2 = [ada[:, i * D_MODEL:(i + 1) * D_MODEL] for i in range(6)]

    w_in_b = w_in[l].astype(BF16)
    lnp_in = jnp.stack([ln_in_g, ln_in_b])
    zero = jnp.zeros((GLA_GATE_RANK, GLA_KEY), F32)
    w2cat = jnp.concatenate([jnp.concatenate([gate_w2_fwd[l], zero], axis=1),
                             jnp.concatenate([zero, gate_w2_bwd[l]], axis=1)], axis=0).astype(BF16)
    gbias = jnp.concatenate([gate_b_fwd[l], gate_b_bwd[l]])[None, :]

    n_ctx = ctx.shape[1]
    mod_ctx = jnp.stack([1.0 + sc1[bsz], sh1[bsz]])[None]
    k_c, v_c, g_c = [a.reshape(bsz, n_ctx, -1) for a in _proj(
        ctx.reshape(1, bsz * n_ctx, D_MODEL), mod_ctx, lnp_in, w_in_b, conv_w[l], conv_b[l][None, :], w2cat, gbias,
        False)]
    zero_state = jnp.zeros((bsz, GLA_KEY, PAIR_VAL), F32)
    s_f, s_b = _gla(None, k_c, v_c, g_c, zero_state, zero_state)

    mod1 = jnp.stack([1.0 + sc1[:bsz], sh1[:bsz]], axis=1)
    ya, q, k, v, sr, g, xn = _proj(x, mod1, lnp_in, w_in_b, conv_w[l], conv_b[l][None, :], w2cat, gbias, True)
    o_f, o_b, _, _ = _gla(q, k, v, g, s_f, s_b)

    mod2 = jnp.stack([g1[:bsz], 1.0 + sc2[:bsz], sh2[:bsz]], axis=1)
    lnp1 = jnp.stack([ln1_g[l], ln1_b[l]])
    wr = jnp.zeros((D_MODEL, ROUTE_W), F32)
    wr = wr.at[:, :N_GROUPS].set(router_group_w[l]).at[:, N_GROUPS:N_GROUPS + N_EXPERTS].set(router_expert_w[l])
    br = jnp.zeros((1, ROUTE_W), F32)
    br = br.at[0, :N_GROUPS].set(router_group_b[l]).at[0, N_GROUPS:N_GROUPS + N_EXPERTS].set(router_expert_b[l])
    x1, hxs, pos, chunks = _mix_out(xn, o_f, o_b, sr, ya, mod2, lnp1, gla_norm_g[l][None, :],
                                    w_out[l].astype(BF16), wr.astype(BF16), br)

    src, dst, n_used, tile_lo, tile_hi, live = _moe_plan(chunks, n_tok // SORT_TILE)
    moe = _moe(hxs, src, dst, n_used, tile_lo, tile_hi, live,
               expert_w1[l].astype(BF16), expert_w3[l].astype(BF16), expert_w2[l].astype(BF16))

    return _final(x1, moe, pos, g2[:bsz][:, None, :], jnp.stack([ln2_g[l], ln2_b[l]]))
```

```python
import functools

import jax
import jax.numpy as jnp
from jax import lax
from jax.experimental import pallas as pl
from jax.experimental.pallas import tpu as pltpu

F32 = jnp.float32
BF16 = jnp.bfloat16

D_MODEL = 1024
GRID_W = 64
CONV_CH = 512
GLA_HEADS = 4
GLA_DK = 64
GLA_DV = 128
GLA_KEY = GLA_HEADS * GLA_DK
GLA_VAL = GLA_HEADS * GLA_DV
PAIR_KEY = 2 * GLA_DK
PAIR_VAL = 2 * GLA_DV
GLA_GATE_RANK = 16
GLA_TAU = 16.0
OFF_AB = 0
OFF_AC = OFF_AB + CONV_CH
OFF_AX = OFF_AC + CONV_CH
OFF_Q = OFF_AX + CONV_CH
OFF_K = OFF_Q + GLA_KEY
OFF_V = OFF_K + GLA_KEY
OFF_R = OFF_V + GLA_VAL
OFF_GF = OFF_R + GLA_VAL
D_PROJ = OFF_GF + 2 * GLA_GATE_RANK
N_GROUPS = 4
EXPERTS_PER_GROUP = 4
N_EXPERTS = N_GROUPS * EXPERTS_PER_GROUP
D_EXPERT = 512
PAIRS_PER_GROUP = 6
N_CLASSES = N_GROUPS * PAIRS_PER_GROUP
LN_EPS = 1e-5
RMS_EPS = 1e-6
DEPTH = 1
DEEPNORM_ALPHA = (2.0 * DEPTH) ** 0.25

LANES = 128
GLA_CHUNK = 64
GLA_SUB = 16
N_SUB = GLA_CHUNK // GLA_SUB
ROUTE_W = LANES
HALF_W = D_MODEL // 2
SLAB_IN_W = HALF_W + ROUTE_W
SLAB_OUT_W = HALF_W
ADA_COL_TILE = 1024
TOKEN_TILE = 1024
SORT_TILE = 256
MIX_SUB_TILE = 128
MOE_TILE = 256
CHUNK_ROWS = 4
SLAB_ROWS = 2 * CHUNK_ROWS
LOCAL_CHUNKS = -(-(SORT_TILE + N_CLASSES * (CHUNK_ROWS - 1)) // CHUNK_ROWS)
LOCAL_SLAB_ROWS = LOCAL_CHUNKS * SLAB_ROWS
CHUNKS_PER_TILE = MOE_TILE // CHUNK_ROWS
VMEM_LIMIT = 56 * 1024 * 1024


def _params(n_axes, vmem=VMEM_LIMIT):
    return pltpu.CompilerParams(dimension_semantics=("arbitrary",) * n_axes, vmem_limit_bytes=vmem)


def _dot(a, b):
    return jnp.dot(a, b, preferred_element_type=F32)


def _div_pow2(x, d):
    assert d & (d - 1) == 0
    return lax.shift_right_logical(x, jnp.int32(d.bit_length() - 1))


def _mod_pow2(x, d):
    assert d & (d - 1) == 0
    return lax.bitwise_and(x, jnp.int32(d - 1))


def _split2(x):
    hi = x.astype(BF16)
    lo = (x - hi.astype(F32)).astype(BF16)
    return hi, lo


def _dot3(a, b):
    ah, al = _split2(a)
    bh, bl = _split2(b)
    return _dot(ah, bh) + _dot(ah, bl) + _dot(al, bh)


def _silu(x):
    return x * (0.5 * jnp.tanh(0.5 * x) + 0.5)


def _layer_norm(x, g, b):
    mu = jnp.mean(x, axis=-1, keepdims=True)
    xc = x - mu
    var = jnp.mean(xc * xc, axis=-1, keepdims=True)
    return xc * lax.rsqrt(var + LN_EPS) * g + b


def _ada_kernel(c_ref, w_ref, b_ref, o_ref):
    o_ref[...] = _dot3(_silu(c_ref[...]), w_ref[...]) + b_ref[...]


def _ada(cond, w_ada, b_ada):
    rows = cond.shape[0]
    n_out = w_ada.shape[1]
    tn = ADA_COL_TILE
    return pl.pallas_call(
        _ada_kernel,
        grid=(n_out // tn,),
        in_specs=[
            pl.BlockSpec((rows, D_MODEL), lambda j: (0, 0)),
            pl.BlockSpec((D_MODEL, tn), lambda j: (0, j)),
            pl.BlockSpec((1, tn), lambda j: (0, j)),
        ],
        out_specs=pl.BlockSpec((rows, tn), lambda j: (0, j)),
        out_shape=jax.ShapeDtypeStruct((rows, n_out), F32),
        compiler_params=_params(1),
        name="ada",
    )(cond, w_ada, b_ada)


def _log_sigmoid(z):
    return jnp.minimum(z, 0.0) - jnp.log(1.0 + jnp.exp(-jnp.abs(z)))


def _proj_kernel(latent, tm, x_ref, mod_ref, lnp_ref, w_ref, cw_ref, cb_ref, w2_ref, gbias_ref, *out_refs):
    x = x_ref[...]
    xn = _layer_norm(x, lnp_ref[0:1, :], lnp_ref[1:2, :])
    h = xn * mod_ref[0:1, :] + mod_ref[1:2, :]
    hb = h.astype(BF16)
    if latent:
        ya_ref, q_ref, k_ref, v_ref, sr_ref, g_ref, xn_ref = out_refs
        xn_ref[...] = DEEPNORM_ALPHA * xn
        p = _dot(hb, w_ref[:, OFF_AB:OFF_Q])
        a_b = p[:, 0:CONV_CH]
        u = p[:, CONV_CH:2 * CONV_CH] * p[:, 2 * CONV_CH:3 * CONV_CH]
        pos = _mod_pow2(lax.broadcasted_iota(jnp.int32, (tm, 1), 0), GRID_W)
        u_prev = jnp.where(pos == 0, 0.0, pltpu.roll(u, 1, 0))
        u_next = jnp.where(pos == GRID_W - 1, 0.0, pltpu.roll(u, tm - 1, 0))
        conv = u_prev * cw_ref[0:1, :] + u * cw_ref[1:2, :] + u_next * cw_ref[2:3, :] + cb_ref[...]
        ya_ref[...] = (a_b * conv).astype(BF16)
        qk = _dot(hb, w_ref[:, OFF_Q:OFF_V])
        q_ref[...] = (qk[:, 0:GLA_KEY] * (GLA_DK ** -0.5)).astype(BF16)
        k_ref[...] = qk[:, GLA_KEY:].astype(BF16)
        r = _dot(hb, w_ref[:, OFF_R:OFF_GF])
        sr_ref[...] = _silu(r).astype(BF16)
    else:
        k_ref, v_ref, g_ref = out_refs
        k_ref[...] = _dot(hb, w_ref[:, OFF_K:OFF_V]).astype(BF16)
    v_ref[...] = _dot(hb, w_ref[:, OFF_V:OFF_R]).astype(BF16)
    low = _dot(hb, w_ref[:, OFF_GF:D_PROJ])
    z = _dot(low.astype(BF16), w2_ref[...]) + gbias_ref[...]
    g_ref[...] = _log_sigmoid(z) * (1.0 / GLA_TAU)


def _proj(x, mod, lnp, w_in_b, conv_w, conv_b, w2cat, gbias, latent):
    bsz, t, _ = x.shape
    tm = min(TOKEN_TILE, t)
    assert t % tm == 0 and tm % GRID_W == 0
    tok = lambda w: pl.BlockSpec((None, tm, w), lambda b, i: (b, i, 0))
    full = lambda a: pl.BlockSpec(a.shape, lambda b, i: (0,) * a.ndim)
    widths = ([(CONV_CH, BF16), (GLA_KEY, BF16)] if latent else []) + [(GLA_KEY, BF16), (GLA_VAL, BF16)]
    widths += ([(GLA_VAL, BF16)] if latent else []) + [(2 * GLA_KEY, F32)]
    widths += [(D_MODEL, F32)] if latent else []
    return pl.pallas_call(
        functools.partial(_proj_kernel, latent, tm),
        grid=(bsz, t // tm),
        in_specs=[
            tok(D_MODEL),
            pl.BlockSpec((None, 2, D_MODEL), lambda b, i: (b, 0, 0)),
            full(lnp), full(w_in_b), full(conv_w), full(conv_b), full(w2cat), full(gbias),
        ],
        out_specs=[tok(w) for w, _ in widths],
        out_shape=[jax.ShapeDtypeStruct((bsz, t, w), dt) for w, dt in widths],
        compiler_params=_params(2),
        name="proj_latent" if latent else "proj_ctx",
    )(x, mod, lnp, w_in_b, conv_w, conv_b, w2cat, gbias)


def _tri(n, reverse, strict=False):
    i = lax.broadcasted_iota(jnp.int32, (n, n), 0)
    j = lax.broadcasted_iota(jnp.int32, (n, n), 1)
    if strict:
        m = (j > i) if reverse else (j < i)
    else:
        m = (j >= i) if reverse else (j <= i)
    return jnp.where(m, 1.0, 0.0).astype(BF16)


def _chunk_cumsum(g, reverse):
    tri = _tri(GLA_CHUNK, reverse)
    g_hi, g_lo = _split2(g)
    return _dot(tri, g_hi) + _dot(tri, g_lo)


def _as_column(row):
    return jnp.broadcast_to(row, (LANES, row.shape[1])).T


def _sub_anchors(gc, reverse):
    zero = jnp.zeros((1, GLA_KEY), F32)
    if reverse:
        return [gc[GLA_SUB * (a + 1):GLA_SUB * (a + 1) + 1] for a in range(N_SUB - 1)] + [zero]
    return [zero] + [gc[GLA_SUB * a - 1:GLA_SUB * a] for a in range(1, N_SUB)]


def _score_pairs(reverse):
    return [(a, b) for a in range(N_SUB) for b in range(N_SUB) if (b >= a if reverse else b <= a)]


def _intra_products(q, k, gc, reverse):
    r = _sub_anchors(gc, reverse)
    anchor = jnp.concatenate([jnp.broadcast_to(ra, (GLA_SUB, GLA_KEY)) for ra in r], axis=0)
    gcb = gc - anchor
    qt = q * jnp.exp(gcb)
    kt = k * jnp.exp(-gcb)
    rows = []
    for a, b in _score_pairs(reverse):
        qa = qt[GLA_SUB * a:GLA_SUB * (a + 1)]
        if a != b:
            qa = qa * jnp.exp(r[a] - r[b])
        rows.append(qa)
    qp = jnp.concatenate(rows, axis=0).astype(BF16)
    width = GLA_HEADS * GLA_CHUNK
    rr = lax.broadcasted_iota(jnp.int32, (width, GLA_KEY), 0)
    cc = lax.broadcasted_iota(jnp.int32, (width, GLA_KEY), 1)
    kbd = jnp.where(_div_pow2(rr, GLA_CHUNK) == _div_pow2(cc, GLA_DK),
                    jnp.concatenate([kt] * GLA_HEADS, axis=0), 0.0)
    return lax.dot_general(qp, kbd.astype(BF16), (((1,), (1,)), ((), ())), preferred_element_type=F32)


def _assemble_scores(res, reverse):
    pairs = _score_pairs(reverse)
    width = GLA_HEADS * GLA_CHUNK
    col = _mod_pow2(lax.broadcasted_iota(jnp.int32, (GLA_SUB, width), 1), GLA_CHUNK)
    col_blk = _div_pow2(col, GLA_SUB)
    col_pos = _mod_pow2(col, GLA_SUB)
    row_pos = lax.broadcasted_iota(jnp.int32, (GLA_SUB, width), 0)
    causal = (col_pos >= row_pos) if reverse else (col_pos <= row_pos)
    blocks = []
    for a in range(N_SUB):
        acc = jnp.zeros((GLA_SUB, width), F32)
        for idx, (pa, pb) in enumerate(pairs):
            if pa != a:
                continue
            keep = col_blk == pb
            if pa == pb:
                keep = keep & causal
            acc = acc + jnp.where(keep, res[GLA_SUB * idx:GLA_SUB * (idx + 1)], 0.0)
        blocks.append(acc)
    return jnp.concatenate(blocks, axis=0)


def _pair_mask(rows_per_head, cols_per_head, n_row_pairs=1):
    shape = (n_row_pairs * 2 * rows_per_head, 2 * cols_per_head)
    rr = _mod_pow2(lax.broadcasted_iota(jnp.int32, shape, 0), 2 * rows_per_head)
    cc = lax.broadcasted_iota(jnp.int32, shape, 1)
    return _div_pow2(rr, rows_per_head) == _div_pow2(cc, cols_per_head)


def _state_terms(q, k, v_b, gc, reverse):
    total = gc[0:1] if reverse else gc[GLA_CHUNK - 1:GLA_CHUNK]
    q_dec = None if q is None else (q * jnp.exp(gc)).astype(BF16)
    k_end = (k * jnp.exp(total - gc)).astype(BF16)
    tn = (((0,), (0,)), ((), ()))
    upd = [lax.dot_general(k_end[:, p * PAIR_KEY:(p + 1) * PAIR_KEY], v_b[:, p * PAIR_VAL:(p + 1) * PAIR_VAL], tn,
                           preferred_element_type=F32) for p in range(GLA_HEADS // 2)]
    upd = jnp.where(_pair_mask(GLA_DK, GLA_DV, GLA_HEADS // 2), jnp.concatenate(upd, axis=0), 0.0)
    decay = jnp.exp(_as_column(total))
    decay = jnp.concatenate([decay] * (PAIR_VAL // LANES), axis=1)
    return q_dec, decay, upd


def _advance_state(q_dec, decay, upd, state):
    o_inter = None
    if q_dec is not None:
        state_b = state.astype(BF16)
        o_inter = jnp.concatenate(
            [_dot(q_dec[:, p * PAIR_KEY:(p + 1) * PAIR_KEY], state_b[p * PAIR_KEY:(p + 1) * PAIR_KEY, :])
             for p in range(GLA_HEADS // 2)], axis=1)
    return o_inter, state * decay + upd


def _gla_kernel(tt, nt, with_out, *refs):
    if with_out:
        (qf_ref, kf_ref, vf_ref, gfw_ref, qb_ref, kb_ref, vb_ref, gbw_ref, s0f_ref, s0b_ref,
         of_ref, ob_ref, sf_ref, sb_ref) = refs
    else:
        kf_ref, vf_ref, gfw_ref, kb_ref, vb_ref, gbw_ref, s0f_ref, s0b_ref, sf_ref, sb_ref = refs
    j = pl.program_id(1)

    @pl.when(j == 0)
    def _():
        sf_ref[...] = s0f_ref[...]
        sb_ref[...] = s0b_ref[...]

    chunk_slices = [slice(c * GLA_CHUNK, (c + 1) * GLA_CHUNK) for c in range(tt // GLA_CHUNK)]
    fwd, bwd = [], []
    for sl in chunk_slices:
        g = gfw_ref[sl, :]
        fwd.append(dict(sl=sl, k=kf_ref[sl, :].astype(F32), v=vf_ref[sl, :], g_f=g[:, 0:GLA_KEY], g_b=g[:, GLA_KEY:],
                        q=qf_ref[sl, :].astype(F32) if with_out else None))
        bwd.append(dict(sl=sl, k=kb_ref[sl, :].astype(F32), v=vb_ref[sl, :], g_b=gbw_ref[sl, :][:, GLA_KEY:],
                        q=qb_ref[sl, :].astype(F32) if with_out else None))
    for d in fwd:
        d["gc_f"] = _chunk_cumsum(d["g_f"], False)
        if with_out:
            d["gc_b"] = _chunk_cumsum(d["g_b"], True)
    for d in bwd:
        d["gc_b"] = _chunk_cumsum(d["g_b"], True)
    if with_out:
        for d in fwd:
            d["res_f"] = _intra_products(d["q"], d["k"], d["gc_f"], False)
            d["res_b"] = _intra_products(d["q"], d["k"], d["gc_b"], True)
    for d in fwd:
        d["terms"] = _state_terms(d["q"], d["k"], d["v"], d["gc_f"], False)
    for d in bwd:
        d["terms"] = _state_terms(d["q"], d["k"], d["v"], d["gc_b"], True)
    if with_out:
        for d in fwd:
            scores = (_assemble_scores(d["res_f"], False) + _assemble_scores(d["res_b"], True)).astype(BF16)
            o_intra = []
            for p in range(GLA_HEADS // 2):
                v_p = d["v"][:, p * PAIR_VAL:(p + 1) * PAIR_VAL]
                vbd = jnp.where(_pair_mask(GLA_CHUNK, GLA_DV), jnp.concatenate([v_p, v_p], axis=0),
                                jnp.zeros((), BF16))
                o_intra.append(_dot(scores[:, p * 2 * GLA_CHUNK:(p + 1) * 2 * GLA_CHUNK], vbd))
            d["o_intra"] = jnp.concatenate(o_intra, axis=1)

    state = sf_ref[...]
    for d in fwd:
        o_inter, state = _advance_state(*d["terms"], state)
        if with_out:
            of_ref[d["sl"], :] = (d["o_intra"] + o_inter).astype(BF16)
    sf_ref[...] = state

    state = sb_ref[...]
    for d in reversed(bwd):
        o_inter, state = _advance_state(*d["terms"], state)
        if with_out:
            ob_ref[d["sl"], :] = o_inter.astype(BF16)
    sb_ref[...] = state


def _gla(q, k, v, g, s0f, s0b):
    with_out = q is not None
    bsz, t, _ = k.shape
    tt = min(TOKEN_TILE, t)
    assert t % tt == 0 and tt % GLA_CHUNK == 0
    nt = t // tt
    fwd = lambda w: pl.BlockSpec((None, tt, w), lambda b, j: (b, j, 0))
    bwd = lambda w: pl.BlockSpec((None, tt, w), lambda b, j: (b, nt - 1 - j, 0))
    st = pl.BlockSpec((None, GLA_KEY, PAIR_VAL), lambda b, j: (b, 0, 0))
    st_shape = jax.ShapeDtypeStruct((bsz, GLA_KEY, PAIR_VAL), F32)
    if with_out:
        ins = [q, k, v, g, q, k, v, g, s0f, s0b]
        in_specs = [fwd(GLA_KEY), fwd(GLA_KEY), fwd(GLA_VAL), fwd(2 * GLA_KEY),
                    bwd(GLA_KEY), bwd(GLA_KEY), bwd(GLA_VAL), bwd(2 * GLA_KEY), st, st]
        out_specs = [fwd(GLA_VAL), bwd(GLA_VAL), st, st]
        o_shape = jax.ShapeDtypeStruct((bsz, t, GLA_VAL), BF16)
        out_shape = [o_shape, o_shape, st_shape, st_shape]
    else:
        ins = [k, v, g, k, v, g, s0f, s0b]
        in_specs = [fwd(GLA_KEY), fwd(GLA_VAL), fwd(2 * GLA_KEY),
                    bwd(GLA_KEY), bwd(GLA_VAL), bwd(2 * GLA_KEY), st, st]
        out_specs = [st, st]
        out_shape = [st_shape, st_shape]
    return pl.pallas_call(
        functools.partial(_gla_kernel, tt, nt, with_out),
        grid=(bsz, nt),
        in_specs=in_specs,
        out_specs=out_specs,
        out_shape=out_shape,
        compiler_params=_params(2),
        name="gla_latent" if with_out else "gla_ctx",
    )(*ins)


def _exact_bf16_parts(x):
    hi = x.astype(BF16).astype(F32)
    r = x - hi
    mid = r.astype(BF16).astype(F32)
    lo = (r - mid).astype(BF16).astype(F32)
    return hi, mid, lo


def _first_index(values, best):
    idx = jnp.full_like(best, float(len(values) - 1))
    for i in reversed(range(len(values) - 1)):
        idx = jnp.where(values[i] >= best, float(i), idx)
    return idx


def _pick(rows, idx):
    out = rows[-1]
    for i in reversed(range(len(rows) - 1)):
        out = jnp.where(idx == float(i), rows[i], out)
    return out


def _route(logit_t, tm):
    row = lambda r: logit_t[r:r + 1, :]
    groups = [row(i) for i in range(N_GROUPS)]
    top = functools.reduce(jnp.maximum, groups)
    eg = [jnp.exp(x - top) for x in groups]
    total = functools.reduce(lambda a, b: a + b, eg)
    pg = [e / total for e in eg]
    p_g = functools.reduce(jnp.maximum, pg)
    g_idx = _first_index(pg, p_g)
    sel = [_pick([row(N_GROUPS + EXPERTS_PER_GROUP * g + j) for g in range(N_GROUPS)], g_idx)
           for j in range(EXPERTS_PER_GROUP)]
    top = functools.reduce(jnp.maximum, sel)
    ee = [jnp.exp(x - top) for x in sel]
    total = functools.reduce(lambda a, b: a + b, ee)
    pe = [e / total for e in ee]
    p1 = functools.reduce(jnp.maximum, pe)
    l1 = _first_index(pe, p1)
    pe2 = [jnp.where(l1 == float(j), -1.0, pe[j]) for j in range(EXPERTS_PER_GROUP)]
    p2 = functools.reduce(jnp.maximum, pe2)
    l2 = _first_index(pe2, p2)
    den = p1 + p2
    w1 = p1 / den * p_g
    w2 = p2 / den * p_g
    lo = jnp.minimum(l1, l2)
    hi = jnp.maximum(l1, l2)
    pair = lo * (7.0 - lo) * 0.5 + (hi - lo - 1.0)
    cls = g_idx * PAIRS_PER_GROUP + pair
    w_lo = jnp.where(l1 < l2, w1, w2)
    w_hi = jnp.where(l1 < l2, w2, w1)
    cls_id = lax.broadcasted_iota(jnp.int32, (ROUTE_W, tm), 0).astype(F32)
    onehot = jnp.where(cls_id == cls, 1.0, 0.0)
    before = _dot(onehot.astype(BF16), _tri(tm, True, strict=True))
    count = jnp.sum(onehot, axis=1, keepdims=True)
    chunks = jnp.floor((count + (CHUNK_ROWS - 1.0)) * (1.0 / CHUNK_ROWS))
    first_chunk = _dot(_tri(ROUTE_W, False, strict=True),
                       jnp.broadcast_to(chunks, (ROUTE_W, LANES)).astype(BF16))[:, 0:1]
    pos_row = jnp.sum(onehot * (CHUNK_ROWS * first_chunk + before), axis=0, keepdims=True)
    return pos_row, w_lo, w_hi, chunks


def _slab_targets(slab_axis):
    shape = (LOCAL_SLAB_ROWS, 1) if slab_axis == 0 else (1, LOCAL_SLAB_ROWS)
    slab_row = lax.broadcasted_iota(jnp.int32, shape, slab_axis)
    sub = _mod_pow2(slab_row, SLAB_ROWS)
    token_row = (CHUNK_ROWS * _div_pow2(slab_row, SLAB_ROWS) + _div_pow2(sub, 2)).astype(F32)
    half = _mod_pow2(sub, 2)
    return [jnp.where(half == h, token_row, -1.0) for h in range(2)]


def _slab_sort_matrices(pos, targets):
    return [jnp.where(t == pos, 1.0, 0.0).astype(BF16) for t in targets]


def _mix_out_kernel(tm, xn_ref, of_ref, ob_ref, sr_ref, ya_ref, mod_ref, lnp_ref, gn_ref, wo_ref, wr_ref, br_ref,
                    x1_ref, hxs_ref, pos_ref, chunks_ref):
    subs = [slice(s, s + MIX_SUB_TILE) for s in range(0, tm, MIX_SUB_TILE)]
    yb = []
    for rows in subs:
        o = of_ref[rows, :].astype(F32) + ob_ref[rows, :].astype(F32)
        sr = sr_ref[rows, :].astype(F32)
        heads = []
        for h in range(GLA_HEADS):
            sl = slice(h * GLA_DV, (h + 1) * GLA_DV)
            oh = o[:, sl]
            ms = jnp.mean(oh * oh, axis=-1, keepdims=True)
            heads.append((oh * lax.rsqrt(ms + RMS_EPS) * gn_ref[...] * sr[:, sl]).astype(BF16))
        yb.append(jnp.concatenate([ya_ref[rows, :]] + heads, axis=1))
    xn = [xn_ref[rows, :] for rows in subs]
    y = [_dot(y_in, wo_ref[...]) for y_in in yb]
    h2_b = []
    for rows, xn_s, y_s in zip(subs, xn, y):
        x1 = _layer_norm(xn_s + mod_ref[0:1, :] * y_s, lnp_ref[0:1, :], lnp_ref[1:2, :])
        x1_ref[rows, :] = x1
        h2_b.append((x1 * mod_ref[1:2, :] + mod_ref[2:3, :]).astype(BF16))
    logit_t = [(_dot(h2_s, wr_ref[...]) + br_ref[...]).T for h2_s in h2_b]
    per_sort = SORT_TILE // MIX_SUB_TILE
    routes = []
    for s in range(tm // SORT_TILE):
        routes.append(_route(jnp.concatenate(logit_t[s * per_sort:(s + 1) * per_sort], axis=1), SORT_TILE))
    rec_id = lax.broadcasted_iota(jnp.int32, (ROUTE_W, SORT_TILE), 0)
    targets = _slab_targets(0)
    for s, (pos_row, w_lo, w_hi, chunks) in enumerate(routes):
        rec_t = jnp.zeros((ROUTE_W, SORT_TILE), F32)
        for i, part in enumerate(_exact_bf16_parts(w_lo) + _exact_bf16_parts(w_hi)):
            rec_t = jnp.where(rec_id == i, part, rec_t)
        rec_b = rec_t.T.astype(BF16)
        h2_s = jnp.concatenate(h2_b[s * per_sort:(s + 1) * per_sort], axis=0)
        sort_lo, sort_hi = _slab_sort_matrices(pos_row, targets)
        pay_lo = jnp.concatenate([h2_s[:, 0:HALF_W], rec_b], axis=1)
        pay_hi = jnp.concatenate([h2_s[:, HALF_W:], jnp.zeros((SORT_TILE, ROUTE_W), BF16)], axis=1)
        slabs = _dot(sort_lo, pay_lo) + _dot(sort_hi, pay_hi)
        for c in range(SLAB_IN_W // LANES):
            hxs_ref[c, s * LOCAL_SLAB_ROWS:(s + 1) * LOCAL_SLAB_ROWS, :] = slabs[:, c * LANES:(c + 1) * LANES]
        pos_ref[8 * s:8 * (s + 1), :] = jnp.broadcast_to(pos_row, (8, SORT_TILE))
        chunks_ref[8 * s:8 * (s + 1), :] = jnp.broadcast_to(chunks, (ROUTE_W, LANES)).T[0:8, :]


def _mix_out(xn, o_f, o_b, sr, ya, mod, lnp, gn, w_out_b, wr, br):
    bsz, t, _ = xn.shape
    assert t % SORT_TILE == 0
    n_sort = max(n for n in (1, 2, 4) if n * SORT_TILE <= TOKEN_TILE and t % (n * SORT_TILE) == 0)
    tm = n_sort * SORT_TILE
    n_t = t // tm
    tok = lambda w: pl.BlockSpec((None, tm, w), lambda b, i: (b, i, 0))
    full = lambda a: pl.BlockSpec(a.shape, lambda b, i: (0,) * a.ndim)
    flat = lambda rows, w: pl.BlockSpec((rows, w), lambda b, i: (b * n_t + i, 0))
    return pl.pallas_call(
        functools.partial(_mix_out_kernel, tm),
        grid=(bsz, n_t),
        in_specs=[
            tok(D_MODEL), tok(GLA_VAL), tok(GLA_VAL), tok(GLA_VAL), tok(CONV_CH),
            pl.BlockSpec((None, 3, D_MODEL), lambda b, i: (b, 0, 0)),
            full(lnp), full(gn), full(w_out_b), full(wr), full(br),
        ],
        out_specs=[tok(D_MODEL),
                   pl.BlockSpec((SLAB_IN_W // LANES, n_sort * LOCAL_SLAB_ROWS, LANES),
                                lambda b, i: (0, b * n_t + i, 0)),
                   flat(n_sort * 8, SORT_TILE), flat(n_sort * 8, ROUTE_W)],
        out_shape=[
            jax.ShapeDtypeStruct((bsz, t, D_MODEL), F32),
            jax.ShapeDtypeStruct((SLAB_IN_W // LANES, bsz * t // SORT_TILE * LOCAL_SLAB_ROWS, LANES), F32),
            jax.ShapeDtypeStruct((bsz * t // SORT_TILE * 8, SORT_TILE), F32),
            jax.ShapeDtypeStruct((bsz * t // SORT_TILE * 8, ROUTE_W), F32),
        ],
        compiler_params=_params(2),
        name="mix_out",
    )(xn, o_f, o_b, sr, ya, mod, lnp, gn, w_out_b, wr, br)


def _moe_kernel(n_chunks, nused_ref, lo_ref, hi_ref, src_ref, dst_ref,
                hxs_hbm, w1l_ref, w3l_ref, w2l_ref, w1h_ref, w3h_ref, w2h_ref, out_hbm, gbuf, obuf, gsem, ssem):
    tile_rows = CHUNKS_PER_TILE * SLAB_ROWS
    i = pl.program_id(0)
    n_used = nused_ref[0]
    slot = lax.bitwise_and(i, 1)

    def slab(chunk):
        return pl.ds(pl.multiple_of(chunk * SLAB_ROWS, SLAB_ROWS), SLAB_ROWS)

    def gather_copy(tile, buf_slot, j):
        chunk = src_ref[tile * CHUNKS_PER_TILE + j]
        return pltpu.make_async_copy(hxs_hbm.at[:, slab(chunk), :], gbuf.at[buf_slot, :, slab(j), :],
                                     gsem.at[buf_slot])

    def scatter_copy(tile, buf_slot, j):
        chunk = dst_ref[tile * CHUNKS_PER_TILE + j]
        return pltpu.make_async_copy(obuf.at[buf_slot, :, slab(j), :], out_hbm.at[:, slab(chunk), :],
                                     ssem.at[buf_slot])

    def start_gather(tile, buf_slot):
        for j in range(CHUNKS_PER_TILE):
            gather_copy(tile, buf_slot, j).start(priority=j % 2)

    def wait_gather(buf_slot):
        pltpu.make_async_copy(hxs_hbm.at[:, pl.ds(0, tile_rows), :], gbuf.at[buf_slot], gsem.at[buf_slot]).wait()

    def wait_scatter(buf_slot):
        pltpu.make_async_copy(obuf.at[buf_slot], out_hbm.at[:, pl.ds(0, tile_rows), :], ssem.at[buf_slot]).wait()

    @pl.when(i == 0)
    def _():
        start_gather(0, 0)
        obuf[...] = jnp.zeros(obuf.shape, F32)
        for s in range(2):
            fill = pltpu.make_async_copy(
                obuf.at[s], out_hbm.at[:, pl.ds((n_chunks + s * CHUNKS_PER_TILE) * SLAB_ROWS, tile_rows), :],
                ssem.at[s])
            fill.start()
            fill.wait()
        min_chunks = SORT_TILE // CHUNK_ROWS
        tail_rows = (LOCAL_CHUNKS - min_chunks) * SLAB_ROWS
        tails = [pltpu.make_async_copy(
            obuf.at[0, :, pl.ds(0, tail_rows), :],
            out_hbm.at[:, pl.ds((t * LOCAL_CHUNKS + min_chunks) * SLAB_ROWS, tail_rows), :], ssem.at[0])
            for t in range(n_chunks // LOCAL_CHUNKS)]
        for tail in tails:
            tail.start()
        for tail in tails:
            tail.wait()

    @pl.when(i + 1 < n_used)
    def _():
        start_gather(i + 1, 1 - slot)

    @pl.when(i < n_used)
    def _():
        wait_gather(slot)

        @pl.when(i >= 2)
        def _():
            wait_scatter(slot)

        def lane_block(c, half):
            return jnp.concatenate(
                [gbuf[slot, c, pl.ds(2 * r + half, CHUNKS_PER_TILE, stride=SLAB_ROWS), :]
                 for r in range(CHUNK_ROWS)], axis=0)

        n_blk = HALF_W // LANES
        xb = jnp.concatenate([lane_block(c, 0) for c in range(n_blk)]
                             + [lane_block(c, 1) for c in range(n_blk)], axis=1).astype(BF16)
        rec = lane_block(n_blk, 0)
        w_lo = rec[:, 0:1] + rec[:, 1:2] + rec[:, 2:3]
        w_hi = rec[:, 3:4] + rec[:, 4:5] + rec[:, 5:6]

        gate = [_dot(xb, w1_ref[...]) for w1_ref in (w1l_ref, w1h_ref)]
        up = [_dot(xb, w3_ref[...]) for w3_ref in (w3l_ref, w3h_ref)]
        act = [(_silu(g) * u).astype(BF16) for g, u in zip(gate, up)]
        e_lo, e_hi = [_dot(a, w2_ref[...]) for a, w2_ref in zip(act, (w2l_ref, w2h_ref))]
        y = w_lo * e_lo + w_hi * e_hi
        for r in range(CHUNK_ROWS):
            rows = slice(r * CHUNKS_PER_TILE, (r + 1) * CHUNKS_PER_TILE)
            for half in range(2):
                for c in range(n_blk):
                    col = half * HALF_W + c * LANES
                    obuf[slot, c, pl.ds(2 * r + half, CHUNKS_PER_TILE, stride=SLAB_ROWS), :] = (
                        y[rows, col:col + LANES])

        for j in range(CHUNKS_PER_TILE):
            scatter_copy(i, slot, j).start(priority=j % 2)

        @pl.when(i == n_used - 1)
        def _():
            wait_scatter(slot)

            @pl.when(i >= 1)
            def _():
                wait_scatter(1 - slot)


def _moe(hxs, src, dst, n_used, tile_lo, tile_hi, w1_b, w3_b, w2_b):
    n_chunks = hxs.shape[1] // SLAB_ROWS
    tile_rows = CHUNKS_PER_TILE * SLAB_ROWS
    n_steps = src.shape[0] // CHUNKS_PER_TILE
    wspec = lambda which, shape: pl.BlockSpec(
        (None,) + shape, (lambda i, nu, lo, hi, s, d: (lo[i], 0, 0)) if which == 0 else
        (lambda i, nu, lo, hi, s, d: (hi[i], 0, 0)))
    grid_spec = pltpu.PrefetchScalarGridSpec(
        num_scalar_prefetch=5,
        grid=(n_steps,),
        in_specs=[
            pl.BlockSpec(memory_space=pl.ANY),
            wspec(0, (D_MODEL, D_EXPERT)), wspec(0, (D_MODEL, D_EXPERT)), wspec(0, (D_EXPERT, D_MODEL)),
            wspec(1, (D_MODEL, D_EXPERT)), wspec(1, (D_MODEL, D_EXPERT)), wspec(1, (D_EXPERT, D_MODEL)),
        ],
        out_specs=pl.BlockSpec(memory_space=pl.ANY),
        scratch_shapes=[
            pltpu.VMEM((2, SLAB_IN_W // LANES, tile_rows, LANES), F32),
            pltpu.VMEM((2, SLAB_OUT_W // LANES, tile_rows, LANES), F32),
            pltpu.SemaphoreType.DMA((2,)),
            pltpu.SemaphoreType.DMA((2,)),
        ],
    )
    return pl.pallas_call(
        functools.partial(_moe_kernel, n_chunks),
        grid_spec=grid_spec,
        out_shape=jax.ShapeDtypeStruct((SLAB_OUT_W // LANES, (n_chunks + 2 * CHUNKS_PER_TILE) * SLAB_ROWS, LANES), F32),
        compiler_params=_params(1),
        name="moe",
    )(n_used, tile_lo, tile_hi, src, dst, hxs, w1_b, w3_b, w2_b, w1_b, w3_b, w2_b)


def _final_kernel(n_sort, x1_ref, moe_ref, pos_ref, mod_ref, lnp_ref, o_ref):
    moe = []
    targets = _slab_targets(1)
    for s in range(n_sort):
        slab_rows = slice(s * LOCAL_SLAB_ROWS, (s + 1) * LOCAL_SLAB_ROWS)
        moe_b = jnp.concatenate([moe_ref[c, slab_rows, :] for c in range(SLAB_OUT_W // LANES)], axis=1).astype(BF16)
        sort_lo, sort_hi = _slab_sort_matrices(_as_column(pos_ref[8 * s:8 * s + 1, :])[:, 0:1], targets)
        for r in range(0, SORT_TILE, MIX_SUB_TILE):
            rows = slice(r, r + MIX_SUB_TILE)
            moe.append((s * SORT_TILE + r,
                        jnp.concatenate([_dot(sort_lo[rows, :], moe_b), _dot(sort_hi[rows, :], moe_b)], axis=1)))
    for start, moe_s in moe:
        rows = slice(start, start + MIX_SUB_TILE)
        o_ref[rows, :] = _layer_norm(DEEPNORM_ALPHA * x1_ref[rows, :] + mod_ref[...] * moe_s,
                                     lnp_ref[0:1, :], lnp_ref[1:2, :])


def _final(x1, moe, pos, g2, lnp):
    bsz, t, _ = x1.shape
    n_sort = max(n for n in (1, 2, 4) if n * SORT_TILE <= TOKEN_TILE and t % (n * SORT_TILE) == 0)
    tm = n_sort * SORT_TILE
    n_t = t // tm
    flat = lambda rows, w: pl.BlockSpec((rows, w), lambda b, i: (b * n_t + i, 0))
    return pl.pallas_call(
        functools.partial(_final_kernel, n_sort),
        grid=(bsz, n_t),
        in_specs=[
            pl.BlockSpec((None, tm, D_MODEL), lambda b, i: (b, i, 0)),
            pl.BlockSpec((SLAB_OUT_W // LANES, n_sort * LOCAL_SLAB_ROWS, LANES), lambda b, i: (0, b * n_t + i, 0)),
            flat(n_sort * 8, SORT_TILE),
            pl.BlockSpec((None, 1, D_MODEL), lambda b, i: (b, 0, 0)),
            pl.BlockSpec(lnp.shape, lambda b, i: (0, 0)),
        ],
        out_specs=pl.BlockSpec((None, tm, D_MODEL), lambda b, i: (b, i, 0)),
        out_shape=jax.ShapeDtypeStruct((bsz, t, D_MODEL), F32),
        compiler_params=_params(2),
        name="final",
    )(x1, moe, pos, g2, lnp)


def _pair_tables():
    lo, hi = [], []
    for g in range(N_GROUPS):
        for a in range(EXPERTS_PER_GROUP):
            for b in range(a + 1, EXPERTS_PER_GROUP):
                lo.append(g * EXPERTS_PER_GROUP + a)
                hi.append(g * EXPERTS_PER_GROUP + b)
    return jnp.array(lo, jnp.int32), jnp.array(hi, jnp.int32)


def _moe_plan(chunks, n_sort_tiles):
    n_cls = N_CLASSES
    hp = lax.Precision.HIGHEST
    m = chunks.reshape(n_sort_tiles, 8, ROUTE_W)[:, 0, :n_cls].astype(jnp.int32)
    a_end = jnp.cumsum(m, axis=0)
    a_start = a_end - m
    per_cls = a_end[-1]
    padded = (per_cls + CHUNKS_PER_TILE - 1) // CHUNKS_PER_TILE * CHUNKS_PER_TILE
    g_end = jnp.cumsum(padded)
    g_start = g_end - padded
    local_off = jnp.cumsum(m, axis=1) - m
    seg = jnp.arange(n_sort_tiles, dtype=jnp.int32)[:, None] * LOCAL_CHUNKS + local_off - a_start
    n_steps = -(-(n_sort_tiles * LOCAL_CHUNKS) // CHUNKS_PER_TILE) + n_cls
    p = jnp.arange(n_steps * CHUNKS_PER_TILE, dtype=jnp.int32)
    cls_p = jnp.minimum(jnp.sum((g_end[None, :] <= p[:, None]).astype(jnp.int32), axis=1), n_cls - 1)
    onehot = (cls_p[:, None] == jnp.arange(n_cls, dtype=jnp.int32)[None, :]).astype(F32)
    pick = lambda tab: jnp.dot(onehot, tab.astype(F32), precision=hp)
    u = p - pick(g_start[:, None])[:, 0].astype(jnp.int32)
    valid = u < pick(per_cls[:, None])[:, 0].astype(jnp.int32)
    a_end_p = pick(a_end.T).astype(jnp.int32)
    seg_p = pick(seg.T).astype(jnp.int32)
    tile_p = jnp.sum((a_end_p <= u[:, None]).astype(jnp.int32), axis=1)
    hit = jnp.arange(n_sort_tiles, dtype=jnp.int32)[None, :] == tile_p[:, None]
    src = jnp.sum(jnp.where(hit, seg_p, 0), axis=1) + u
    pad_dst = n_sort_tiles * LOCAL_CHUNKS + (p // CHUNKS_PER_TILE) % 2 * CHUNKS_PER_TILE + p % CHUNKS_PER_TILE
    dst = jnp.where(valid, src, pad_dst).astype(jnp.int32)
    src = jnp.where(valid, src, 0).astype(jnp.int32)
    n_used = g_end[-1:] // CHUNKS_PER_TILE
    step = jnp.arange(n_steps, dtype=jnp.int32)
    tile_cls = jnp.sum((g_end[None, :] // CHUNKS_PER_TILE <= step[:, None]).astype(jnp.int32), axis=1)
    pair_lo, pair_hi = _pair_tables()
    pair_oh = (jnp.minimum(tile_cls, N_CLASSES - 1)[:, None] == jnp.arange(N_CLASSES)[None, :]).astype(jnp.int32)
    tile_lo = jnp.sum(pair_oh * pair_lo[None, :], axis=1).astype(jnp.int32)
    tile_hi = jnp.sum(pair_oh * pair_hi[None, :], axis=1).astype(jnp.int32)
    return src, dst, n_used.astype(jnp.int32), tile_lo, tile_hi


def kernel(x, c, ctx, c_ctx, ln_in_g, ln_in_b, w_ada, b_ada, w_in, conv_w, conv_b, gate_w2_fwd, gate_b_fwd,
           gate_w2_bwd, gate_b_bwd, gla_norm_g, w_out, ln1_g, ln1_b, router_group_w, router_group_b,
           router_expert_w, router_expert_b, expert_w1, expert_w3, expert_w2, ln2_g, ln2_b):
    bsz, t, _ = x.shape
    n_tok = bsz * t
    l = 0
    rows = -(-(bsz + 1) // 8) * 8
    cond = jnp.zeros((rows, D_MODEL), F32).at[:bsz].set(c).at[bsz].set(c_ctx)
    ada = _ada(cond, w_ada[l], b_ada[l][None, :])
    sh1, sc1, g1, sh2, sc2, g2 = [ada[:, i * D_MODEL:(i + 1) * D_MODEL] for i in range(6)]

    w_in_b = w_in[l].astype(BF16)
    lnp_in = jnp.stack([ln_in_g, ln_in_b])
    zero = jnp.zeros((GLA_GATE_RANK, GLA_KEY), F32)
    w2cat = jnp.concatenate([jnp.concatenate([gate_w2_fwd[l], zero], axis=1),
                             jnp.concatenate([zero, gate_w2_bwd[l]], axis=1)], axis=0).astype(BF16)
    gbias = jnp.concatenate([gate_b_fwd[l], gate_b_bwd[l]])[None, :]

    n_ctx = ctx.shape[1]
    mod_ctx = jnp.stack([1.0 + sc1[bsz], sh1[bsz]])[None]
    k_c, v_c, g_c = [a.reshape(bsz, n_ctx, -1) for a in _proj(
        ctx.reshape(1, bsz * n_ctx, D_MODEL), mod_ctx, lnp_in, w_in_b, conv_w[l], conv_b[l][None, :], w2cat, gbias,
        False)]
    zero_state = jnp.zeros((bsz, GLA_KEY, PAIR_VAL), F32)
    s_f, s_b = _gla(None, k_c, v_c, g_c, zero_state, zero_state)

    mod1 = jnp.stack([1.0 + sc1[:bsz], sh1[:bsz]], axis=1)
    ya, q, k, v, sr, g, xn = _proj(x, mod1, lnp_in, w_in_b, conv_w[l], conv_b[l][None, :], w2cat, gbias, True)
    o_f, o_b, _, _ = _gla(q, k, v, g, s_f, s_b)

    mod2 = jnp.stack([g1[:bsz], 1.0 + sc2[:bsz], sh2[:bsz]], axis=1)
    lnp1 = jnp.stack([ln1_g[l], ln1_b[l]])
    wr = jnp.zeros((D_MODEL, ROUTE_W), F32)
    wr = wr.at[:, :N_GROUPS].set(router_group_w[l]).at[:, N_GROUPS:N_GROUPS + N_EXPERTS].set(router_expert_w[l])
    br = jnp.zeros((1, ROUTE_W), F32)
    br = br.at[0, :N_GROUPS].set(router_group_b[l]).at[0, N_GROUPS:N_GROUPS + N_EXPERTS].set(router_expert_b[l])
    x1, hxs, pos, chunks = _mix_out(xn, o_f, o_b, sr, ya, mod2, lnp1, gla_norm_g[l][None, :],
                                    w_out[l].astype(BF16), wr.astype(BF16), br)

    src, dst, n_used, tile_lo, tile_hi = _moe_plan(chunks, n_tok // SORT_TILE)
    moe = _moe(hxs, src, dst, n_used, tile_lo, tile_hi,
               expert_w1[l].astype(BF16), expert_w3[l].astype(BF16), expert_w2[l].astype(BF16))

    return _final(x1, moe, pos, g2[:bsz][:, None, :], jnp.stack([ln2_g[l], ln2_b[l]]))
```

```python
import functools

import jax
import jax.numpy as jnp
from jax import lax
from jax.experimental import pallas as pl
from jax.experimental.pallas import tpu as pltpu

F32 = jnp.float32
BF16 = jnp.bfloat16

D_MODEL = 1024
GRID_W = 64
CONV_CH = 512
GLA_HEADS = 4
GLA_DK = 64
GLA_DV = 128
GLA_KEY = GLA_HEADS * GLA_DK
GLA_VAL = GLA_HEADS * GLA_DV
PAIR_KEY = 2 * GLA_DK
PAIR_VAL = 2 * GLA_DV
GLA_GATE_RANK = 16
GLA_TAU = 16.0
OFF_AB = 0
OFF_AC = OFF_AB + CONV_CH
OFF_AX = OFF_AC + CONV_CH
OFF_Q = OFF_AX + CONV_CH
OFF_K = OFF_Q + GLA_KEY
OFF_V = OFF_K + GLA_KEY
OFF_R = OFF_V + GLA_VAL
OFF_GF = OFF_R + GLA_VAL
D_PROJ = OFF_GF + 2 * GLA_GATE_RANK
N_GROUPS = 4
EXPERTS_PER_GROUP = 4
N_EXPERTS = N_GROUPS * EXPERTS_PER_GROUP
D_EXPERT = 512
PAIRS_PER_GROUP = 6
N_CLASSES = N_GROUPS * PAIRS_PER_GROUP
LN_EPS = 1e-5
RMS_EPS = 1e-6
DEPTH = 1
DEEPNORM_ALPHA = (2.0 * DEPTH) ** 0.25

LANES = 128
GLA_CHUNK = 64
GLA_SUB = 16
N_SUB = GLA_CHUNK // GLA_SUB
ROUTE_W = LANES
HALF_W = D_MODEL // 2
SLAB_IN_W = HALF_W + ROUTE_W
SLAB_OUT_W = HALF_W
ADA_COL_TILE = 1024
TOKEN_TILE = 1024
SORT_TILE = 256
MIX_SUB_TILE = 128
MOE_TILE = 256
CHUNK_ROWS = 4
SLAB_ROWS = 2 * CHUNK_ROWS
LOCAL_CHUNKS = -(-(SORT_TILE + N_CLASSES * (CHUNK_ROWS - 1)) // CHUNK_ROWS)
LOCAL_SLAB_ROWS = LOCAL_CHUNKS * SLAB_ROWS
CHUNKS_PER_TILE = MOE_TILE // CHUNK_ROWS
VMEM_LIMIT = 56 * 1024 * 1024


def _params(n_axes, vmem=VMEM_LIMIT):
    return pltpu.CompilerParams(dimension_semantics=("arbitrary",) * n_axes, vmem_limit_bytes=vmem)


def _dot(a, b):
    return jnp.dot(a, b, preferred_element_type=F32)


def _div_pow2(x, d):
    assert d & (d - 1) == 0
    return lax.shift_right_logical(x, jnp.int32(d.bit_length() - 1))


def _mod_pow2(x, d):
    assert d & (d - 1) == 0
    return lax.bitwise_and(x, jnp.int32(d - 1))


def _split2(x):
    hi = x.astype(BF16)
    lo = (x - hi.astype(F32)).astype(BF16)
    return hi, lo


def _dot3(a, b):
    ah, al = _split2(a)
    bh, bl = _split2(b)
    return _dot(ah, bh) + _dot(ah, bl) + _dot(al, bh)


def _silu(x):
    return x * (0.5 * jnp.tanh(0.5 * x) + 0.5)


def _layer_norm(x, g, b):
    mu = jnp.mean(x, axis=-1, keepdims=True)
    xc = x - mu
    var = jnp.mean(xc * xc, axis=-1, keepdims=True)
    return xc * lax.rsqrt(var + LN_EPS) * g + b


def _ada_kernel(c_ref, w_ref, b_ref, o_ref):
    o_ref[...] = _dot3(_silu(c_ref[...]), w_ref[...]) + b_ref[...]


def _ada(cond, w_ada, b_ada):
    rows = cond.shape[0]
    n_out = w_ada.shape[1]
    tn = ADA_COL_TILE
    return pl.pallas_call(
        _ada_kernel,
        grid=(n_out // tn,),
        in_specs=[
            pl.BlockSpec((rows, D_MODEL), lambda j: (0, 0)),
            pl.BlockSpec((D_MODEL, tn), lambda j: (0, j)),
            pl.BlockSpec((1, tn), lambda j: (0, j)),
        ],
        out_specs=pl.BlockSpec((rows, tn), lambda j: (0, j)),
        out_shape=jax.ShapeDtypeStruct((rows, n_out), F32),
        compiler_params=_params(1),
        name="ada",
    )(cond, w_ada, b_ada)


def _log_sigmoid(z):
    return jnp.minimum(z, 0.0) - jnp.log(1.0 + jnp.exp(-jnp.abs(z)))


def _proj_kernel(latent, tm, x_ref, mod_ref, lnp_ref, w_ref, cw_ref, cb_ref, w2_ref, gbias_ref, *out_refs):
    x = x_ref[...]
    xn = _layer_norm(x, lnp_ref[0:1, :], lnp_ref[1:2, :])
    h = xn * mod_ref[0:1, :] + mod_ref[1:2, :]
    hb = h.astype(BF16)
    if latent:
        ya_ref, q_ref, k_ref, v_ref, sr_ref, g_ref, xn_ref = out_refs
        xn_ref[...] = DEEPNORM_ALPHA * xn
        p = _dot(hb, w_ref[:, OFF_AB:OFF_Q])
        a_b = p[:, 0:CONV_CH]
        u = p[:, CONV_CH:2 * CONV_CH] * p[:, 2 * CONV_CH:3 * CONV_CH]
        pos = _mod_pow2(lax.broadcasted_iota(jnp.int32, (tm, 1), 0), GRID_W)
        u_prev = jnp.where(pos == 0, 0.0, pltpu.roll(u, 1, 0))
        u_next = jnp.where(pos == GRID_W - 1, 0.0, pltpu.roll(u, tm - 1, 0))
        conv = u_prev * cw_ref[0:1, :] + u * cw_ref[1:2, :] + u_next * cw_ref[2:3, :] + cb_ref[...]
        ya_ref[...] = (a_b * conv).astype(BF16)
        qk = _dot(hb, w_ref[:, OFF_Q:OFF_V])
        q_ref[...] = (qk[:, 0:GLA_KEY] * (GLA_DK ** -0.5)).astype(BF16)
        k_ref[...] = qk[:, GLA_KEY:].astype(BF16)
        r = _dot(hb, w_ref[:, OFF_R:OFF_GF])
        sr_ref[...] = _silu(r).astype(BF16)
    else:
        k_ref, v_ref, g_ref = out_refs
        k_ref[...] = _dot(hb, w_ref[:, OFF_K:OFF_V]).astype(BF16)
    v_ref[...] = _dot(hb, w_ref[:, OFF_V:OFF_R]).astype(BF16)
    low = _dot(hb, w_ref[:, OFF_GF:D_PROJ])
    z = _dot(low.astype(BF16), w2_ref[...]) + gbias_ref[...]
    g_ref[...] = _log_sigmoid(z) * (1.0 / GLA_TAU)


def _proj(x, mod, lnp, w_in_b, conv_w, conv_b, w2cat, gbias, latent):
    bsz, t, _ = x.shape
    tm = min(TOKEN_TILE, t)
    assert t % tm == 0 and tm % GRID_W == 0
    tok = lambda w: pl.BlockSpec((None, tm, w), lambda b, i: (b, i, 0))
    full = lambda a: pl.BlockSpec(a.shape, lambda b, i: (0,) * a.ndim)
    widths = ([(CONV_CH, BF16), (GLA_KEY, BF16)] if latent else []) + [(GLA_KEY, BF16), (GLA_VAL, BF16)]
    widths += ([(GLA_VAL, BF16)] if latent else []) + [(2 * GLA_KEY, F32)]
    widths += [(D_MODEL, F32)] if latent else []
    return pl.pallas_call(
        functools.partial(_proj_kernel, latent, tm),
        grid=(bsz, t // tm),
        in_specs=[
            tok(D_MODEL),
            pl.BlockSpec((None, 2, D_MODEL), lambda b, i: (b, 0, 0)),
            full(lnp), full(w_in_b), full(conv_w), full(conv_b), full(w2cat), full(gbias),
        ],
        out_specs=[tok(w) for w, _ in widths],
        out_shape=[jax.ShapeDtypeStruct((bsz, t, w), dt) for w, dt in widths],
        compiler_params=_params(2),
        name="proj_latent" if latent else "proj_ctx",
    )(x, mod, lnp, w_in_b, conv_w, conv_b, w2cat, gbias)


def _tri(n, reverse, strict=False):
    i = lax.broadcasted_iota(jnp.int32, (n, n), 0)
    j = lax.broadcasted_iota(jnp.int32, (n, n), 1)
    if strict:
        m = (j > i) if reverse else (j < i)
    else:
        m = (j >= i) if reverse else (j <= i)
    return jnp.where(m, 1.0, 0.0).astype(BF16)


def _chunk_cumsum(g, reverse):
    tri = _tri(GLA_CHUNK, reverse)
    g_hi, g_lo = _split2(g)
    return _dot(tri, g_hi) + _dot(tri, g_lo)


def _as_column(row):
    return jnp.broadcast_to(row, (LANES, row.shape[1])).T


def _sub_anchors(gc, reverse):
    zero = jnp.zeros((1, GLA_KEY), F32)
    if reverse:
        return [gc[GLA_SUB * (a + 1):GLA_SUB * (a + 1) + 1] for a in range(N_SUB - 1)] + [zero]
    return [zero] + [gc[GLA_SUB * a - 1:GLA_SUB * a] for a in range(1, N_SUB)]


def _score_pairs(reverse):
    return [(a, b) for a in range(N_SUB) for b in range(N_SUB) if (b >= a if reverse else b <= a)]


def _intra_products(q, k, gc, reverse):
    r = _sub_anchors(gc, reverse)
    anchor = jnp.concatenate([jnp.broadcast_to(ra, (GLA_SUB, GLA_KEY)) for ra in r], axis=0)
    gcb = gc - anchor
    qt = q * jnp.exp(gcb)
    kt = k * jnp.exp(-gcb)
    rows = []
    for a, b in _score_pairs(reverse):
        qa = qt[GLA_SUB * a:GLA_SUB * (a + 1)]
        if a != b:
            qa = qa * jnp.exp(r[a] - r[b])
        rows.append(qa)
    qp = jnp.concatenate(rows, axis=0).astype(BF16)
    width = GLA_HEADS * GLA_CHUNK
    rr = lax.broadcasted_iota(jnp.int32, (width, GLA_KEY), 0)
    cc = lax.broadcasted_iota(jnp.int32, (width, GLA_KEY), 1)
    kbd = jnp.where(_div_pow2(rr, GLA_CHUNK) == _div_pow2(cc, GLA_DK),
                    jnp.concatenate([kt] * GLA_HEADS, axis=0), 0.0)
    return lax.dot_general(qp, kbd.astype(BF16), (((1,), (1,)), ((), ())), preferred_element_type=F32)


def _assemble_scores(res, reverse):
    pairs = _score_pairs(reverse)
    width = GLA_HEADS * GLA_CHUNK
    col = _mod_pow2(lax.broadcasted_iota(jnp.int32, (GLA_SUB, width), 1), GLA_CHUNK)
    col_blk = _div_pow2(col, GLA_SUB)
    col_pos = _mod_pow2(col, GLA_SUB)
    row_pos = lax.broadcasted_iota(jnp.int32, (GLA_SUB, width), 0)
    causal = (col_pos >= row_pos) if reverse else (col_pos <= row_pos)
    blocks = []
    for a in range(N_SUB):
        acc = jnp.zeros((GLA_SUB, width), F32)
        for idx, (pa, pb) in enumerate(pairs):
            if pa != a:
                continue
            keep = col_blk == pb
            if pa == pb:
                keep = keep & causal
            acc = acc + jnp.where(keep, res[GLA_SUB * idx:GLA_SUB * (idx + 1)], 0.0)
        blocks.append(acc)
    return jnp.concatenate(blocks, axis=0)


def _pair_mask(rows_per_head, cols_per_head, n_row_pairs=1):
    shape = (n_row_pairs * 2 * rows_per_head, 2 * cols_per_head)
    rr = _mod_pow2(lax.broadcasted_iota(jnp.int32, shape, 0), 2 * rows_per_head)
    cc = lax.broadcasted_iota(jnp.int32, shape, 1)
    return _div_pow2(rr, rows_per_head) == _div_pow2(cc, cols_per_head)


def _state_terms(q, k, v_b, gc, reverse):
    total = gc[0:1] if reverse else gc[GLA_CHUNK - 1:GLA_CHUNK]
    q_dec = None if q is None else (q * jnp.exp(gc)).astype(BF16)
    k_end = (k * jnp.exp(total - gc)).astype(BF16)
    tn = (((0,), (0,)), ((), ()))
    upd = [lax.dot_general(k_end[:, p * PAIR_KEY:(p + 1) * PAIR_KEY], v_b[:, p * PAIR_VAL:(p + 1) * PAIR_VAL], tn,
                           preferred_element_type=F32) for p in range(GLA_HEADS // 2)]
    upd = jnp.where(_pair_mask(GLA_DK, GLA_DV, GLA_HEADS // 2), jnp.concatenate(upd, axis=0), 0.0)
    decay = jnp.exp(_as_column(total))
    decay = jnp.concatenate([decay] * (PAIR_VAL // LANES), axis=1)
    return q_dec, decay, upd


def _advance_state(q_dec, decay, upd, state):
    o_inter = None
    if q_dec is not None:
        state_b = state.astype(BF16)
        o_inter = jnp.concatenate(
            [_dot(q_dec[:, p * PAIR_KEY:(p + 1) * PAIR_KEY], state_b[p * PAIR_KEY:(p + 1) * PAIR_KEY, :])
             for p in range(GLA_HEADS // 2)], axis=1)
    return o_inter, state * decay + upd


def _gla_kernel(tt, nt, with_out, *refs):
    if with_out:
        (qf_ref, kf_ref, vf_ref, gfw_ref, qb_ref, kb_ref, vb_ref, gbw_ref, s0f_ref, s0b_ref,
         of_ref, ob_ref, sf_ref, sb_ref) = refs
    else:
        kf_ref, vf_ref, gfw_ref, kb_ref, vb_ref, gbw_ref, s0f_ref, s0b_ref, sf_ref, sb_ref = refs
    j = pl.program_id(1)

    @pl.when(j == 0)
    def _():
        sf_ref[...] = s0f_ref[...]
        sb_ref[...] = s0b_ref[...]

    chunk_slices = [slice(c * GLA_CHUNK, (c + 1) * GLA_CHUNK) for c in range(tt // GLA_CHUNK)]
    fwd, bwd = [], []
    for sl in chunk_slices:
        g = gfw_ref[sl, :]
        fwd.append(dict(sl=sl, k=kf_ref[sl, :].astype(F32), v=vf_ref[sl, :], g_f=g[:, 0:GLA_KEY], g_b=g[:, GLA_KEY:],
                        q=qf_ref[sl, :].astype(F32) if with_out else None))
        bwd.append(dict(sl=sl, k=kb_ref[sl, :].astype(F32), v=vb_ref[sl, :], g_b=gbw_ref[sl, :][:, GLA_KEY:],
                        q=qb_ref[sl, :].astype(F32) if with_out else None))
    for d in fwd:
        d["gc_f"] = _chunk_cumsum(d["g_f"], False)
        if with_out:
            d["gc_b"] = _chunk_cumsum(d["g_b"], True)
    for d in bwd:
        d["gc_b"] = _chunk_cumsum(d["g_b"], True)
    if with_out:
        for d in fwd:
            d["res_f"] = _intra_products(d["q"], d["k"], d["gc_f"], False)
            d["res_b"] = _intra_products(d["q"], d["k"], d["gc_b"], True)
    for d in fwd:
        d["terms"] = _state_terms(d["q"], d["k"], d["v"], d["gc_f"], False)
    for d in bwd:
        d["terms"] = _state_terms(d["q"], d["k"], d["v"], d["gc_b"], True)
    if with_out:
        for d in fwd:
            scores = (_assemble_scores(d["res_f"], False) + _assemble_scores(d["res_b"], True)).astype(BF16)
            o_intra = []
            for p in range(GLA_HEADS // 2):
                v_p = d["v"][:, p * PAIR_VAL:(p + 1) * PAIR_VAL]
                vbd = jnp.where(_pair_mask(GLA_CHUNK, GLA_DV), jnp.concatenate([v_p, v_p], axis=0),
                                jnp.zeros((), BF16))
                o_intra.append(_dot(scores[:, p * 2 * GLA_CHUNK:(p + 1) * 2 * GLA_CHUNK], vbd))
            d["o_intra"] = jnp.concatenate(o_intra, axis=1)

    state = sf_ref[...]
    for d in fwd:
        o_inter, state = _advance_state(*d["terms"], state)
        if with_out:
            of_ref[d["sl"], :] = (d["o_intra"] + o_inter).astype(BF16)
    sf_ref[...] = state

    state = sb_ref[...]
    for d in reversed(bwd):
        o_inter, state = _advance_state(*d["terms"], state)
        if with_out:
            ob_ref[d["sl"], :] = o_inter.astype(BF16)
    sb_ref[...] = state


def _gla(q, k, v, g, s0f, s0b):
    with_out = q is not None
    bsz, t, _ = k.shape
    tt = min(TOKEN_TILE, t)
    assert t % tt == 0 and tt % GLA_CHUNK == 0
    nt = t // tt
    fwd = lambda w: pl.BlockSpec((None, tt, w), lambda b, j: (b, j, 0))
    bwd = lambda w: pl.BlockSpec((None, tt, w), lambda b, j: (b, nt - 1 - j, 0))
    st = pl.BlockSpec((None, GLA_KEY, PAIR_VAL), lambda b, j: (b, 0, 0))
    st_shape = jax.ShapeDtypeStruct((bsz, GLA_KEY, PAIR_VAL), F32)
    if with_out:
        ins = [q, k, v, g, q, k, v, g, s0f, s0b]
        in_specs = [fwd(GLA_KEY), fwd(GLA_KEY), fwd(GLA_VAL), fwd(2 * GLA_KEY),
                    bwd(GLA_KEY), bwd(GLA_KEY), bwd(GLA_VAL), bwd(2 * GLA_KEY), st, st]
        out_specs = [fwd(GLA_VAL), bwd(GLA_VAL), st, st]
        o_shape = jax.ShapeDtypeStruct((bsz, t, GLA_VAL), BF16)
        out_shape = [o_shape, o_shape, st_shape, st_shape]
    else:
        ins = [k, v, g, k, v, g, s0f, s0b]
        in_specs = [fwd(GLA_KEY), fwd(GLA_VAL), fwd(2 * GLA_KEY),
                    bwd(GLA_KEY), bwd(GLA_VAL), bwd(2 * GLA_KEY), st, st]
        out_specs = [st, st]
        out_shape = [st_shape, st_shape]
    return pl.pallas_call(
        functools.partial(_gla_kernel, tt, nt, with_out),
        grid=(bsz, nt),
        in_specs=in_specs,
        out_specs=out_specs,
        out_shape=out_shape,
        compiler_params=_params(2),
        name="gla_latent" if with_out else "gla_ctx",
    )(*ins)


def _exact_bf16_parts(x):
    hi = x.astype(BF16).astype(F32)
    r = x - hi
    mid = r.astype(BF16).astype(F32)
    lo = (r - mid).astype(BF16).astype(F32)
    return hi, mid, lo


def _first_index(values, best):
    idx = jnp.full_like(best, float(len(values) - 1))
    for i in reversed(range(len(values) - 1)):
        idx = jnp.where(values[i] >= best, float(i), idx)
    return idx


def _pick(rows, idx):
    out = rows[-1]
    for i in reversed(range(len(rows) - 1)):
        out = jnp.where(idx == float(i), rows[i], out)
    return out


def _route(logit_t, tm):
    row = lambda r: logit_t[r:r + 1, :]
    groups = [row(i) for i in range(N_GROUPS)]
    top = functools.reduce(jnp.maximum, groups)
    eg = [jnp.exp(x - top) for x in groups]
    total = functools.reduce(lambda a, b: a + b, eg)
    pg = [e / total for e in eg]
    p_g = functools.reduce(jnp.maximum, pg)
    g_idx = _first_index(pg, p_g)
    sel = [_pick([row(N_GROUPS + EXPERTS_PER_GROUP * g + j) for g in range(N_GROUPS)], g_idx)
           for j in range(EXPERTS_PER_GROUP)]
    top = functools.reduce(jnp.maximum, sel)
    ee = [jnp.exp(x - top) for x in sel]
    total = functools.reduce(lambda a, b: a + b, ee)
    pe = [e / total for e in ee]
    p1 = functools.reduce(jnp.maximum, pe)
    l1 = _first_index(pe, p1)
    pe2 = [jnp.where(l1 == float(j), -1.0, pe[j]) for j in range(EXPERTS_PER_GROUP)]
    p2 = functools.reduce(jnp.maximum, pe2)
    l2 = _first_index(pe2, p2)
    den = p1 + p2
    w1 = p1 / den * p_g
    w2 = p2 / den * p_g
    lo = jnp.minimum(l1, l2)
    hi = jnp.maximum(l1, l2)
    pair = lo * (7.0 - lo) * 0.5 + (hi - lo - 1.0)
    cls = g_idx * PAIRS_PER_GROUP + pair
    w_lo = jnp.where(l1 < l2, w1, w2)
    w_hi = jnp.where(l1 < l2, w2, w1)
    cls_id = lax.broadcasted_iota(jnp.int32, (ROUTE_W, tm), 0).astype(F32)
    onehot = jnp.where(cls_id == cls, 1.0, 0.0)
    before = _dot(onehot.astype(BF16), _tri(tm, True, strict=True))
    count = jnp.sum(onehot, axis=1, keepdims=True)
    chunks = jnp.floor((count + (CHUNK_ROWS - 1.0)) * (1.0 / CHUNK_ROWS))
    first_chunk = _dot(_tri(ROUTE_W, False, strict=True),
                       jnp.broadcast_to(chunks, (ROUTE_W, LANES)).astype(BF16))[:, 0:1]
    pos_row = jnp.sum(onehot * (CHUNK_ROWS * first_chunk + before), axis=0, keepdims=True)
    return pos_row, w_lo, w_hi, chunks


def _slab_targets(slab_axis):
    shape = (LOCAL_SLAB_ROWS, 1) if slab_axis == 0 else (1, LOCAL_SLAB_ROWS)
    slab_row = lax.broadcasted_iota(jnp.int32, shape, slab_axis)
    sub = _mod_pow2(slab_row, SLAB_ROWS)
    token_row = (CHUNK_ROWS * _div_pow2(slab_row, SLAB_ROWS) + _div_pow2(sub, 2)).astype(F32)
    half = _mod_pow2(sub, 2)
    return [jnp.where(half == h, token_row, -1.0) for h in range(2)]


def _slab_sort_matrices(pos, targets):
    return [jnp.where(t == pos, 1.0, 0.0).astype(BF16) for t in targets]


def _mix_out_kernel(tm, xn_ref, of_ref, ob_ref, sr_ref, ya_ref, mod_ref, lnp_ref, gn_ref, wo_ref, wr_ref, br_ref,
                    x1_ref, hxs_ref, pos_ref, chunks_ref):
    subs = [slice(s, s + MIX_SUB_TILE) for s in range(0, tm, MIX_SUB_TILE)]
    yb = []
    for rows in subs:
        o = of_ref[rows, :].astype(F32) + ob_ref[rows, :].astype(F32)
        sr = sr_ref[rows, :].astype(F32)
        heads = []
        for h in range(GLA_HEADS):
            sl = slice(h * GLA_DV, (h + 1) * GLA_DV)
            oh = o[:, sl]
            ms = jnp.mean(oh * oh, axis=-1, keepdims=True)
            heads.append((oh * lax.rsqrt(ms + RMS_EPS) * gn_ref[...] * sr[:, sl]).astype(BF16))
        yb.append(jnp.concatenate([ya_ref[rows, :]] + heads, axis=1))
    xn = [xn_ref[rows, :] for rows in subs]
    y = [_dot(y_in, wo_ref[...]) for y_in in yb]
    h2_b = []
    for rows, xn_s, y_s in zip(subs, xn, y):
        x1 = _layer_norm(xn_s + mod_ref[0:1, :] * y_s, lnp_ref[0:1, :], lnp_ref[1:2, :])
        x1_ref[rows, :] = x1
        h2_b.append((x1 * mod_ref[1:2, :] + mod_ref[2:3, :]).astype(BF16))
    logit_t = [(_dot(h2_s, wr_ref[...]) + br_ref[...]).T for h2_s in h2_b]
    per_sort = SORT_TILE // MIX_SUB_TILE
    routes = []
    for s in range(tm // SORT_TILE):
        routes.append(_route(jnp.concatenate(logit_t[s * per_sort:(s + 1) * per_sort], axis=1), SORT_TILE))
    rec_id = lax.broadcasted_iota(jnp.int32, (ROUTE_W, SORT_TILE), 0)
    targets = _slab_targets(0)
    for s, (pos_row, w_lo, w_hi, chunks) in enumerate(routes):
        rec_t = jnp.zeros((ROUTE_W, SORT_TILE), F32)
        for i, part in enumerate(_exact_bf16_parts(w_lo) + _exact_bf16_parts(w_hi)):
            rec_t = jnp.where(rec_id == i, part, rec_t)
        rec_b = rec_t.T.astype(BF16)
        h2_s = jnp.concatenate(h2_b[s * per_sort:(s + 1) * per_sort], axis=0)
        sort_lo, sort_hi = _slab_sort_matrices(pos_row, targets)
        pay_lo = jnp.concatenate([h2_s[:, 0:HALF_W], rec_b], axis=1)
        pay_hi = jnp.concatenate([h2_s[:, HALF_W:], jnp.zeros((SORT_TILE, ROUTE_W), BF16)], axis=1)
        slabs = _dot(sort_lo, pay_lo) + _dot(sort_hi, pay_hi)
        for c in range(SLAB_IN_W // LANES):
            hxs_ref[c, s * LOCAL_SLAB_ROWS:(s + 1) * LOCAL_SLAB_ROWS, :] = slabs[:, c * LANES:(c + 1) * LANES]
        pos_ref[8 * s:8 * (s + 1), :] = jnp.broadcast_to(pos_row, (8, SORT_TILE))
        chunks_ref[8 * s:8 * (s + 1), :] = jnp.broadcast_to(chunks, (ROUTE_W, LANES)).T[0:8, :]


def _mix_out(xn, o_f, o_b, sr, ya, mod, lnp, gn, w_out_b, wr, br):
    bsz, t, _ = xn.shape
    assert t % SORT_TILE == 0
    n_sort = max(n for n in (1, 2, 4) if n * SORT_TILE <= TOKEN_TILE and t % (n * SORT_TILE) == 0)
    tm = n_sort * SORT_TILE
    n_t = t // tm
    tok = lambda w: pl.BlockSpec((None, tm, w), lambda b, i: (b, i, 0))
    full = lambda a: pl.BlockSpec(a.shape, lambda b, i: (0,) * a.ndim)
    flat = lambda rows, w: pl.BlockSpec((rows, w), lambda b, i: (b * n_t + i, 0))
    return pl.pallas_call(
        functools.partial(_mix_out_kernel, tm),
        grid=(bsz, n_t),
        in_specs=[
            tok(D_MODEL), tok(GLA_VAL), tok(GLA_VAL), tok(GLA_VAL), tok(CONV_CH),
            pl.BlockSpec((None, 3, D_MODEL), lambda b, i: (b, 0, 0)),
            full(lnp), full(gn), full(w_out_b), full(wr), full(br),
        ],
        out_specs=[tok(D_MODEL),
                   pl.BlockSpec((SLAB_IN_W // LANES, n_sort * LOCAL_SLAB_ROWS, LANES),
                                lambda b, i: (0, b * n_t + i, 0)),
                   flat(n_sort * 8, SORT_TILE), flat(n_sort * 8, ROUTE_W)],
        out_shape=[
            jax.ShapeDtypeStruct((bsz, t, D_MODEL), F32),
            jax.ShapeDtypeStruct((SLAB_IN_W // LANES, bsz * t // SORT_TILE * LOCAL_SLAB_ROWS, LANES), F32),
            jax.ShapeDtypeStruct((bsz * t // SORT_TILE * 8, SORT_TILE), F32),
            jax.ShapeDtypeStruct((bsz * t // SORT_TILE * 8, ROUTE_W), F32),
        ],
        compiler_params=_params(2),
        name="mix_out",
    )(xn, o_f, o_b, sr, ya, mod, lnp, gn, w_out_b, wr, br)


def _moe_kernel(n_chunks, nused_ref, lo_ref, hi_ref, live_ref, src_ref, dst_ref,
                hxs_hbm, w1l_ref, w3l_ref, w2l_ref, w1h_ref, w3h_ref, w2h_ref, out_hbm, gbuf, obuf, gsem, ssem):
    tile_rows = CHUNKS_PER_TILE * SLAB_ROWS
    i = pl.program_id(0)
    n_used = nused_ref[0]
    slot = lax.bitwise_and(i, 1)

    def slab(chunk):
        return pl.ds(pl.multiple_of(chunk * SLAB_ROWS, SLAB_ROWS), SLAB_ROWS)

    def gather_copy(tile, buf_slot, j):
        chunk = src_ref[tile * CHUNKS_PER_TILE + j]
        return pltpu.make_async_copy(hxs_hbm.at[:, slab(chunk), :], gbuf.at[buf_slot, :, slab(j), :],
                                     gsem.at[buf_slot])

    def scatter_copy(tile, buf_slot, j):
        chunk = dst_ref[tile * CHUNKS_PER_TILE + j]
        return pltpu.make_async_copy(obuf.at[buf_slot, :, slab(j), :], out_hbm.at[:, slab(chunk), :],
                                     ssem.at[buf_slot])

    def start_gather(tile, buf_slot):
        for j in range(CHUNKS_PER_TILE):
            gather_copy(tile, buf_slot, j).start(priority=j % 2)

    def wait_gather(buf_slot):
        pltpu.make_async_copy(hxs_hbm.at[:, pl.ds(0, tile_rows), :], gbuf.at[buf_slot], gsem.at[buf_slot]).wait()

    def wait_scatter(buf_slot):
        pltpu.make_async_copy(obuf.at[buf_slot], out_hbm.at[:, pl.ds(0, tile_rows), :], ssem.at[buf_slot]).wait()

    @pl.when(i == 0)
    def _():
        start_gather(0, 0)
        obuf[...] = jnp.zeros(obuf.shape, F32)
        for s in range(2):
            fill = pltpu.make_async_copy(
                obuf.at[s], out_hbm.at[:, pl.ds((n_chunks + s * CHUNKS_PER_TILE) * SLAB_ROWS, tile_rows), :],
                ssem.at[s])
            fill.start()
            fill.wait()
        min_chunks = SORT_TILE // CHUNK_ROWS
        tail_rows = (LOCAL_CHUNKS - min_chunks) * SLAB_ROWS
        tails = [pltpu.make_async_copy(
            obuf.at[0, :, pl.ds(0, tail_rows), :],
            out_hbm.at[:, pl.ds((t * LOCAL_CHUNKS + min_chunks) * SLAB_ROWS, tail_rows), :], ssem.at[0])
            for t in range(n_chunks // LOCAL_CHUNKS)]
        for tail in tails:
            tail.start()
        for tail in tails:
            tail.wait()

    @pl.when(i + 1 < n_used)
    def _():
        start_gather(i + 1, 1 - slot)

    @pl.when(i < n_used)
    def _():
        wait_gather(slot)

        @pl.when(i >= 2)
        def _():
            wait_scatter(slot)

        def experts(n_live):
            def lane_block(c, half):
                return jnp.concatenate(
                    [gbuf[slot, c, pl.ds(2 * r + half, n_live, stride=SLAB_ROWS), :]
                     for r in range(CHUNK_ROWS)], axis=0)

            n_blk = HALF_W // LANES
            xb = jnp.concatenate([lane_block(c, 0) for c in range(n_blk)]
                                 + [lane_block(c, 1) for c in range(n_blk)], axis=1).astype(BF16)
            rec = lane_block(n_blk, 0)
            w_lo = rec[:, 0:1] + rec[:, 1:2] + rec[:, 2:3]
            w_hi = rec[:, 3:4] + rec[:, 4:5] + rec[:, 5:6]

            gate = [_dot(xb, w1_ref[...]) for w1_ref in (w1l_ref, w1h_ref)]
            up = [_dot(xb, w3_ref[...]) for w3_ref in (w3l_ref, w3h_ref)]
            act = [(_silu(g) * u).astype(BF16) for g, u in zip(gate, up)]
            e_lo, e_hi = [_dot(a, w2_ref[...]) for a, w2_ref in zip(act, (w2l_ref, w2h_ref))]
            y = w_lo * e_lo + w_hi * e_hi
            for r in range(CHUNK_ROWS):
                rows = slice(r * n_live, (r + 1) * n_live)
                for half in range(2):
                    for c in range(n_blk):
                        col = half * HALF_W + c * LANES
                        obuf[slot, c, pl.ds(2 * r + half, n_live, stride=SLAB_ROWS), :] = y[rows, col:col + LANES]

        few = live_ref[i] <= CHUNKS_PER_TILE // 2
        pl.when(few)(functools.partial(experts, CHUNKS_PER_TILE // 2))
        pl.when(jnp.logical_not(few))(functools.partial(experts, CHUNKS_PER_TILE))

        for j in range(CHUNKS_PER_TILE):
            scatter_copy(i, slot, j).start(priority=j % 2)

        @pl.when(i == n_used - 1)
        def _():
            wait_scatter(slot)

            @pl.when(i >= 1)
            def _():
                wait_scatter(1 - slot)


def _moe(hxs, src, dst, n_used, tile_lo, tile_hi, tile_live, w1_b, w3_b, w2_b):
    n_chunks = hxs.shape[1] // SLAB_ROWS
    tile_rows = CHUNKS_PER_TILE * SLAB_ROWS
    n_steps = src.shape[0] // CHUNKS_PER_TILE
    wspec = lambda which, shape: pl.BlockSpec(
        (None,) + shape, (lambda i, nu, lo, hi, lv, s, d: (lo[i], 0, 0)) if which == 0 else
        (lambda i, nu, lo, hi, lv, s, d: (hi[i], 0, 0)))
    grid_spec = pltpu.PrefetchScalarGridSpec(
        num_scalar_prefetch=6,
        grid=(n_steps,),
        in_specs=[
            pl.BlockSpec(memory_space=pl.ANY),
            wspec(0, (D_MODEL, D_EXPERT)), wspec(0, (D_MODEL, D_EXPERT)), wspec(0, (D_EXPERT, D_MODEL)),
            wspec(1, (D_MODEL, D_EXPERT)), wspec(1, (D_MODEL, D_EXPERT)), wspec(1, (D_EXPERT, D_MODEL)),
        ],
        out_specs=pl.BlockSpec(memory_space=pl.ANY),
        scratch_shapes=[
            pltpu.VMEM((2, SLAB_IN_W // LANES, tile_rows, LANES), F32),
            pltpu.VMEM((2, SLAB_OUT_W // LANES, tile_rows, LANES), F32),
            pltpu.SemaphoreType.DMA((2,)),
            pltpu.SemaphoreType.DMA((2,)),
        ],
    )
    return pl.pallas_call(
        functools.partial(_moe_kernel, n_chunks),
        grid_spec=grid_spec,
        out_shape=jax.ShapeDtypeStruct((SLAB_OUT_W // LANES, (n_chunks + 2 * CHUNKS_PER_TILE) * SLAB_ROWS, LANES), F32),
        compiler_params=_params(1),
        name="moe",
    )(n_used, tile_lo, tile_hi, tile_live, src, dst, hxs, w1_b, w3_b, w2_b, w1_b, w3_b, w2_b)


def _final_kernel(n_sort, x1_ref, moe_ref, pos_ref, mod_ref, lnp_ref, o_ref):
    moe = []
    targets = _slab_targets(1)
    for s in range(n_sort):
        slab_rows = slice(s * LOCAL_SLAB_ROWS, (s + 1) * LOCAL_SLAB_ROWS)
        moe_b = jnp.concatenate([moe_ref[c, slab_rows, :] for c in range(SLAB_OUT_W // LANES)], axis=1).astype(BF16)
        sort_lo, sort_hi = _slab_sort_matrices(_as_column(pos_ref[8 * s:8 * s + 1, :])[:, 0:1], targets)
        for r in range(0, SORT_TILE, MIX_SUB_TILE):
            rows = slice(r, r + MIX_SUB_TILE)
            moe.append((s * SORT_TILE + r,
                        jnp.concatenate([_dot(sort_lo[rows, :], moe_b), _dot(sort_hi[rows, :], moe_b)], axis=1)))
    for start, moe_s in moe:
        rows = slice(start, start + MIX_SUB_TILE)
        o_ref[rows, :] = _layer_norm(DEEPNORM_ALPHA * x1_ref[rows, :] + mod_ref[...] * moe_s,
                                     lnp_ref[0:1, :], lnp_ref[1:2, :])


def _final(x1, moe, pos, g2, lnp):
    bsz, t, _ = x1.shape
    n_sort = max(n for n in (1, 2, 4) if n * SORT_TILE <= TOKEN_TILE and t % (n * SORT_TILE) == 0)
    tm = n_sort * SORT_TILE
    n_t = t // tm
    flat = lambda rows, w: pl.BlockSpec((rows, w), lambda b, i: (b * n_t + i, 0))
    return pl.pallas_call(
        functools.partial(_final_kernel, n_sort),
        grid=(bsz, n_t),
        in_specs=[
            pl.BlockSpec((None, tm, D_MODEL), lambda b, i: (b, i, 0)),
            pl.BlockSpec((SLAB_OUT_W // LANES, n_sort * LOCAL_SLAB_ROWS, LANES), lambda b, i: (0, b * n_t + i, 0)),
            flat(n_sort * 8, SORT_TILE),
            pl.BlockSpec((None, 1, D_MODEL), lambda b, i: (b, 0, 0)),
            pl.BlockSpec(lnp.shape, lambda b, i: (0, 0)),
        ],
        out_specs=pl.BlockSpec((None, tm, D_MODEL), lambda b, i: (b, i, 0)),
        out_shape=jax.ShapeDtypeStruct((bsz, t, D_MODEL), F32),
        compiler_params=_params(2),
        name="final",
    )(x1, moe, pos, g2, lnp)


def _pair_tables():
    lo, hi = [], []
    for g in range(N_GROUPS):
        for a in range(EXPERTS_PER_GROUP):
            for b in range(a + 1, EXPERTS_PER_GROUP):
                lo.append(g * EXPERTS_PER_GROUP + a)
                hi.append(g * EXPERTS_PER_GROUP + b)
    return jnp.array(lo, jnp.int32), jnp.array(hi, jnp.int32)


def _moe_plan(chunks, n_sort_tiles):
    n_cls = N_CLASSES
    hp = lax.Precision.HIGHEST
    m = chunks.reshape(n_sort_tiles, 8, ROUTE_W)[:, 0, :n_cls].astype(jnp.int32)
    a_end = jnp.cumsum(m, axis=0)
    a_start = a_end - m
    per_cls = a_end[-1]
    padded = (per_cls + CHUNKS_PER_TILE - 1) // CHUNKS_PER_TILE * CHUNKS_PER_TILE
    g_end = jnp.cumsum(padded)
    g_start = g_end - padded
    local_off = jnp.cumsum(m, axis=1) - m
    seg = jnp.arange(n_sort_tiles, dtype=jnp.int32)[:, None] * LOCAL_CHUNKS + local_off - a_start
    n_steps = -(-(n_sort_tiles * LOCAL_CHUNKS) // CHUNKS_PER_TILE) + n_cls
    p = jnp.arange(n_steps * CHUNKS_PER_TILE, dtype=jnp.int32)
    cls_p = jnp.minimum(jnp.sum((g_end[None, :] <= p[:, None]).astype(jnp.int32), axis=1), n_cls - 1)
    onehot = (cls_p[:, None] == jnp.arange(n_cls, dtype=jnp.int32)[None, :]).astype(F32)
    pick = lambda tab: jnp.dot(onehot, tab.astype(F32), precision=hp)
    u = p - pick(g_start[:, None])[:, 0].astype(jnp.int32)
    valid = u < pick(per_cls[:, None])[:, 0].astype(jnp.int32)
    a_end_p = pick(a_end.T).astype(jnp.int32)
    seg_p = pick(seg.T).astype(jnp.int32)
    tile_p = jnp.sum((a_end_p <= u[:, None]).astype(jnp.int32), axis=1)
    hit = jnp.arange(n_sort_tiles, dtype=jnp.int32)[None, :] == tile_p[:, None]
    src = jnp.sum(jnp.where(hit, seg_p, 0), axis=1) + u
    pad_dst = n_sort_tiles * LOCAL_CHUNKS + (p // CHUNKS_PER_TILE) % 2 * CHUNKS_PER_TILE + p % CHUNKS_PER_TILE
    dst = jnp.where(valid, src, pad_dst).astype(jnp.int32)
    src = jnp.where(valid, src, 0).astype(jnp.int32)
    n_used = g_end[-1:] // CHUNKS_PER_TILE
    step = jnp.arange(n_steps, dtype=jnp.int32)
    tile_cls = jnp.sum((g_end[None, :] // CHUNKS_PER_TILE <= step[:, None]).astype(jnp.int32), axis=1)
    pair_lo, pair_hi = _pair_tables()
    pair_oh = (jnp.minimum(tile_cls, N_CLASSES - 1)[:, None] == jnp.arange(N_CLASSES)[None, :]).astype(jnp.int32)
    tile_lo = jnp.sum(pair_oh * pair_lo[None, :], axis=1).astype(jnp.int32)
    tile_hi = jnp.sum(pair_oh * pair_hi[None, :], axis=1).astype(jnp.int32)
    tile_live = jnp.sum(valid.reshape(n_steps, CHUNKS_PER_TILE).astype(jnp.int32), axis=1)
    return src, dst, n_used.astype(jnp.int32), tile_lo, tile_hi, tile_live


def kernel(x, c, ctx, c_ctx, ln_in_g, ln_in_b, w_ada, b_ada, w_in, conv_w, conv_b, gate_w2_fwd, gate_b_fwd,
           gate_w2_bwd, gate_b_bwd, gla_norm_g, w_out, ln1_g, ln1_b, router_group_w, router_group_b,
           router_expert_w, router_expert_b, expert_w1, expert_w3, expert_w2, ln2_g, ln2_b):
    bsz, t, _ = x.shape
    n_tok = bsz * t
    l = 0
    rows = -(-(bsz + 1) // 8) * 8
    cond = jnp.zeros((rows, D_MODEL), F32).at[:bsz].set(c).at[bsz].set(c_ctx)
    ada = _ada(cond, w_ada[l], b_ada[l][None, :])
    sh1, sc1, g1, sh2, sc2, g2 = [ada[:, i * D_MODEL:(i + 1) * D_MODEL] for i in range(6)]

    w_in_b = w_in[l].astype(BF16)
    lnp_in = jnp.stack([ln_in_g, ln_in_b])
    zero = jnp.zeros((GLA_GATE_RANK, GLA_KEY), F32)
    w2cat = jnp.concatenate([jnp.concatenate([gate_w2_fwd[l], zero], axis=1),
                             jnp.concatenate([zero, gate_w2_bwd[l]], axis=1)], axis=0).astype(BF16)
    gbias = jnp.concatenate([gate_b_fwd[l], gate_b_bwd[l]])[None, :]

    n_ctx = ctx.shape[1]
    mod_ctx = jnp.stack([1.0 + sc1[bsz], sh1[bsz]])[None]
    k_c, v_c, g_c = [a.reshape(bsz, n_ctx, -1) for a in _proj(
        ctx.reshape(1, bsz * n_ctx, D_MODEL), mod_ctx, lnp_in, w_in_b, conv_w[l], conv_b[l][None, :], w2cat, gbias,
        False)]
    zero_state = jnp.zeros((bsz, GLA_KEY, PAIR_VAL), F32)
    s_f, s_b = _gla(None, k_c, v_c, g_c, zero_state, zero_state)

    mod1 = jnp.stack([1.0 + sc1[:bsz], sh1[:bsz]], axis=1)
    ya, q, k, v, sr, g, xn = _proj(x, mod1, lnp_in, w_in_b, conv_w[l], conv_b[l][None, :], w2cat, gbias, True)
    o_f, o_b, _, _ = _gla(q, k, v, g, s_f, s_b)

    mod2 = jnp.stack([g1[:bsz], 1.0 + sc2[:bsz], sh2[:bsz]], axis=1)
    lnp1 = jnp.stack([ln1_g[l], ln1_b[l]])
    wr = jnp.zeros((D_MODEL, ROUTE_W), F32)
    wr = wr.at[:, :N_GROUPS].set(router_group_w[l]).at[:, N_GROUPS:N_GROUPS + N_EXPERTS].set(router_expert_w[l])
    br = jnp.zeros((1, ROUTE_W), F32)
    br = br.at[0, :N_GROUPS].set(router_group_b[l]).at[0, N_GROUPS:N_GROUPS + N_EXPERTS].set(router_expert_b[l])
    x1, hxs, pos, chunks = _mix_out(xn, o_f, o_b, sr, ya, mod2, lnp1, gla_norm_g[l][None, :],
                                    w_out[l].astype(BF16), wr.astype(BF16), br)

    src, dst, n_used, tile_lo, tile_hi, tile_live = _moe_plan(chunks, n_tok // SORT_TILE)
    moe = _moe(hxs, src, dst, n_used, tile_lo, tile_hi, tile_live,
               expert_w1[l].astype(BF16), expert_w3[l].astype(BF16), expert_w2[l].astype(BF16))

    return _final(x1, moe, pos, g2[:bsz][:, None, :], jnp.stack([ln2_g[l], ln2_b[l]]))
```

```python
import functools

import jax
import jax.numpy as jnp
from jax import lax
from jax.experimental import pallas as pl
from jax.experimental.pallas import tpu as pltpu

F32 = jnp.float32
BF16 = jnp.bfloat16

D_MODEL = 1024
GRID_W = 64
CONV_CH = 512
GLA_HEADS = 4
GLA_DK = 64
GLA_DV = 128
GLA_KEY = GLA_HEADS * GLA_DK
GLA_VAL = GLA_HEADS * GLA_DV
PAIR_KEY = 2 * GLA_DK
PAIR_VAL = 2 * GLA_DV
GLA_GATE_RANK = 16
GLA_TAU = 16.0
OFF_AB = 0
OFF_AC = OFF_AB + CONV_CH
OFF_AX = OFF_AC + CONV_CH
OFF_Q = OFF_AX + CONV_CH
OFF_K = OFF_Q + GLA_KEY
OFF_V = OFF_K + GLA_KEY
OFF_R = OFF_V + GLA_VAL
OFF_GF = OFF_R + GLA_VAL
D_PROJ = OFF_GF + 2 * GLA_GATE_RANK
N_GROUPS = 4
EXPERTS_PER_GROUP = 4
N_EXPERTS = N_GROUPS * EXPERTS_PER_GROUP
D_EXPERT = 512
PAIRS_PER_GROUP = 6
N_CLASSES = N_GROUPS * PAIRS_PER_GROUP
LN_EPS = 1e-5
RMS_EPS = 1e-6
DEPTH = 1
DEEPNORM_ALPHA = (2.0 * DEPTH) ** 0.25

LANES = 128
GLA_CHUNK = 64
GLA_SUB = 16
N_SUB = GLA_CHUNK // GLA_SUB
ROUTE_W = LANES
HALF_W = D_MODEL // 2
SLAB_IN_W = HALF_W + ROUTE_W
SLAB_OUT_W = HALF_W
ADA_COL_TILE = 1024
TOKEN_TILE = 1024
SORT_TILE = 256
MIX_SUB_TILE = 128
MOE_TILE = 256
CHUNK_ROWS = 4
SLAB_ROWS = 2 * CHUNK_ROWS
LOCAL_CHUNKS = -(-(SORT_TILE + N_CLASSES * (CHUNK_ROWS - 1)) // CHUNK_ROWS)
LOCAL_SLAB_ROWS = LOCAL_CHUNKS * SLAB_ROWS
CHUNKS_PER_TILE = MOE_TILE // CHUNK_ROWS
VMEM_LIMIT = 56 * 1024 * 1024


def _params(n_axes, vmem=VMEM_LIMIT):
    return pltpu.CompilerParams(dimension_semantics=("arbitrary",) * n_axes, vmem_limit_bytes=vmem)


def _dot(a, b):
    return jnp.dot(a, b, preferred_element_type=F32)


def _div_pow2(x, d):
    assert d & (d - 1) == 0
    return lax.shift_right_logical(x, jnp.int32(d.bit_length() - 1))


def _mod_pow2(x, d):
    assert d & (d - 1) == 0
    return lax.bitwise_and(x, jnp.int32(d - 1))


def _split2(x):
    hi = x.astype(BF16)
    lo = (x - hi.astype(F32)).astype(BF16)
    return hi, lo


def _dot3(a, b):
    ah, al = _split2(a)
    bh, bl = _split2(b)
    return _dot(ah, bh) + _dot(ah, bl) + _dot(al, bh)


def _silu(x):
    return x * (0.5 * jnp.tanh(0.5 * x) + 0.5)


def _layer_norm(x, g, b):
    mu = jnp.mean(x, axis=-1, keepdims=True)
    xc = x - mu
    var = jnp.mean(xc * xc, axis=-1, keepdims=True)
    return xc * lax.rsqrt(var + LN_EPS) * g + b


def _ada_kernel(c_ref, w_ref, b_ref, o_ref):
    o_ref[...] = _dot3(_silu(c_ref[...]), w_ref[...]) + b_ref[...]


def _ada(cond, w_ada, b_ada):
    rows = cond.shape[0]
    n_out = w_ada.shape[1]
    tn = ADA_COL_TILE
    return pl.pallas_call(
        _ada_kernel,
        grid=(n_out // tn,),
        in_specs=[
            pl.BlockSpec((rows, D_MODEL), lambda j: (0, 0)),
            pl.BlockSpec((D_MODEL, tn), lambda j: (0, j)),
            pl.BlockSpec((1, tn), lambda j: (0, j)),
        ],
        out_specs=pl.BlockSpec((rows, tn), lambda j: (0, j)),
        out_shape=jax.ShapeDtypeStruct((rows, n_out), F32),
        compiler_params=_params(1),
        name="ada",
    )(cond, w_ada, b_ada)


def _log_sigmoid(z):
    return jnp.minimum(z, 0.0) - jnp.log(1.0 + jnp.exp(-jnp.abs(z)))


def _proj_kernel(latent, tm, n_cast, x_ref, mod_ref, lnp_ref, w_ref, cw_ref, cb_ref, w2_ref, gbias_ref, *refs):
    cast_refs, out_refs, cast_out_refs = refs[:n_cast], refs[n_cast:len(refs) - n_cast], refs[len(refs) - n_cast:]
    for src_ref, dst_ref in zip(cast_refs, cast_out_refs):
        dst_ref[...] = src_ref[...].astype(BF16)
    x = x_ref[...]
    xn = _layer_norm(x, lnp_ref[0:1, :], lnp_ref[1:2, :])
    h = xn * mod_ref[0:1, :] + mod_ref[1:2, :]
    hb = h.astype(BF16)
    if latent:
        ya_ref, q_ref, k_ref, v_ref, sr_ref, g_ref, xn_ref = out_refs
        xn_ref[...] = DEEPNORM_ALPHA * xn
        p = _dot(hb, w_ref[:, OFF_AB:OFF_Q])
        a_b = p[:, 0:CONV_CH]
        u = p[:, CONV_CH:2 * CONV_CH] * p[:, 2 * CONV_CH:3 * CONV_CH]
        pos = _mod_pow2(lax.broadcasted_iota(jnp.int32, (tm, 1), 0), GRID_W)
        u_prev = jnp.where(pos == 0, 0.0, pltpu.roll(u, 1, 0))
        u_next = jnp.where(pos == GRID_W - 1, 0.0, pltpu.roll(u, tm - 1, 0))
        conv = u_prev * cw_ref[0:1, :] + u * cw_ref[1:2, :] + u_next * cw_ref[2:3, :] + cb_ref[...]
        ya_ref[...] = (a_b * conv).astype(BF16)
        qk = _dot(hb, w_ref[:, OFF_Q:OFF_V])
        q_ref[...] = (qk[:, 0:GLA_KEY] * (GLA_DK ** -0.5)).astype(BF16)
        k_ref[...] = qk[:, GLA_KEY:].astype(BF16)
        r = _dot(hb, w_ref[:, OFF_R:OFF_GF])
        sr_ref[...] = _silu(r).astype(BF16)
    else:
        k_ref, v_ref, g_ref = out_refs
        k_ref[...] = _dot(hb, w_ref[:, OFF_K:OFF_V]).astype(BF16)
    v_ref[...] = _dot(hb, w_ref[:, OFF_V:OFF_R]).astype(BF16)
    low = _dot(hb, w_ref[:, OFF_GF:D_PROJ])
    z = _dot(low.astype(BF16), w2_ref[...]) + gbias_ref[...]
    g_ref[...] = _log_sigmoid(z) * (1.0 / GLA_TAU)


def _proj(x, mod, lnp, w_in_b, conv_w, conv_b, w2cat, gbias, latent, to_cast=()):
    bsz, t, _ = x.shape
    tm = min(TOKEN_TILE, t)
    assert t % tm == 0 and tm % GRID_W == 0
    n_t = t // tm
    tok = lambda w: pl.BlockSpec((None, tm, w), lambda b, i: (b, i, 0))
    full = lambda a: pl.BlockSpec(a.shape, lambda b, i: (0,) * a.ndim)
    sliced = [a.reshape(bsz * n_t, -1, a.shape[-1]) for a in to_cast]
    step = lambda a: pl.BlockSpec((None,) + a.shape[1:], lambda b, i: (b * n_t + i, 0, 0))
    widths = ([(CONV_CH, BF16), (GLA_KEY, BF16)] if latent else []) + [(GLA_KEY, BF16), (GLA_VAL, BF16)]
    widths += ([(GLA_VAL, BF16)] if latent else []) + [(2 * GLA_KEY, F32)]
    widths += [(D_MODEL, F32)] if latent else []
    outs = pl.pallas_call(
        functools.partial(_proj_kernel, latent, tm, len(sliced)),
        grid=(bsz, n_t),
        in_specs=[
            tok(D_MODEL),
            pl.BlockSpec((None, 2, D_MODEL), lambda b, i: (b, 0, 0)),
            full(lnp), full(w_in_b), full(conv_w), full(conv_b), full(w2cat), full(gbias),
        ] + [step(a) for a in sliced],
        out_specs=[tok(w) for w, _ in widths] + [step(a) for a in sliced],
        out_shape=([jax.ShapeDtypeStruct((bsz, t, w), dt) for w, dt in widths]
                   + [jax.ShapeDtypeStruct(a.shape, BF16) for a in sliced]),
        compiler_params=_params(2),
        name="proj_latent" if latent else "proj_ctx",
    )(x, mod, lnp, w_in_b, conv_w, conv_b, w2cat, gbias, *sliced)
    return list(outs[:len(widths)]) + [o.reshape(a.shape) for o, a in zip(outs[len(widths):], to_cast)]


def _tri(n, reverse, strict=False):
    i = lax.broadcasted_iota(jnp.int32, (n, n), 0)
    j = lax.broadcasted_iota(jnp.int32, (n, n), 1)
    if strict:
        m = (j > i) if reverse else (j < i)
    else:
        m = (j >= i) if reverse else (j <= i)
    return jnp.where(m, 1.0, 0.0).astype(BF16)


def _chunk_cumsum(g, reverse):
    tri = _tri(GLA_CHUNK, reverse)
    g_hi, g_lo = _split2(g)
    return _dot(tri, g_hi) + _dot(tri, g_lo)


def _as_column(row):
    return jnp.broadcast_to(row, (LANES, row.shape[1])).T


def _sub_anchors(gc, reverse):
    zero = jnp.zeros((1, GLA_KEY), F32)
    if reverse:
        return [gc[GLA_SUB * (a + 1):GLA_SUB * (a + 1) + 1] for a in range(N_SUB - 1)] + [zero]
    return [zero] + [gc[GLA_SUB * a - 1:GLA_SUB * a] for a in range(1, N_SUB)]


def _score_pairs(reverse):
    return [(a, b) for a in range(N_SUB) for b in range(N_SUB) if (b >= a if reverse else b <= a)]


def _intra_products(q, k, gc, reverse):
    r = _sub_anchors(gc, reverse)
    anchor = jnp.concatenate([jnp.broadcast_to(ra, (GLA_SUB, GLA_KEY)) for ra in r], axis=0)
    gcb = gc - anchor
    qt = q * jnp.exp(gcb)
    kt = k * jnp.exp(-gcb)
    rows = []
    for a, b in _score_pairs(reverse):
        qa = qt[GLA_SUB * a:GLA_SUB * (a + 1)]
        if a != b:
            qa = qa * jnp.exp(r[a] - r[b])
        rows.append(qa)
    qp = jnp.concatenate(rows, axis=0).astype(BF16)
    width = GLA_HEADS * GLA_CHUNK
    rr = lax.broadcasted_iota(jnp.int32, (width, GLA_KEY), 0)
    cc = lax.broadcasted_iota(jnp.int32, (width, GLA_KEY), 1)
    kbd = jnp.where(_div_pow2(rr, GLA_CHUNK) == _div_pow2(cc, GLA_DK),
                    jnp.concatenate([kt] * GLA_HEADS, axis=0), 0.0)
    return lax.dot_general(qp, kbd.astype(BF16), (((1,), (1,)), ((), ())), preferred_element_type=F32)


def _assemble_scores(res, reverse):
    pairs = _score_pairs(reverse)
    width = GLA_HEADS * GLA_CHUNK
    col = _mod_pow2(lax.broadcasted_iota(jnp.int32, (GLA_SUB, width), 1), GLA_CHUNK)
    col_blk = _div_pow2(col, GLA_SUB)
    col_pos = _mod_pow2(col, GLA_SUB)
    row_pos = lax.broadcasted_iota(jnp.int32, (GLA_SUB, width), 0)
    causal = (col_pos >= row_pos) if reverse else (col_pos <= row_pos)
    blocks = []
    for a in range(N_SUB):
        acc = jnp.zeros((GLA_SUB, width), F32)
        for idx, (pa, pb) in enumerate(pairs):
            if pa != a:
                continue
            keep = col_blk == pb
            if pa == pb:
                keep = keep & causal
            acc = acc + jnp.where(keep, res[GLA_SUB * idx:GLA_SUB * (idx + 1)], 0.0)
        blocks.append(acc)
    return jnp.concatenate(blocks, axis=0)


def _pair_mask(rows_per_head, cols_per_head, n_row_pairs=1):
    shape = (n_row_pairs * 2 * rows_per_head, 2 * cols_per_head)
    rr = _mod_pow2(lax.broadcasted_iota(jnp.int32, shape, 0), 2 * rows_per_head)
    cc = lax.broadcasted_iota(jnp.int32, shape, 1)
    return _div_pow2(rr, rows_per_head) == _div_pow2(cc, cols_per_head)


def _state_terms(q, k, v_b, gc, reverse):
    total = gc[0:1] if reverse else gc[GLA_CHUNK - 1:GLA_CHUNK]
    q_dec = None if q is None else (q * jnp.exp(gc)).astype(BF16)
    k_end = (k * jnp.exp(total - gc)).astype(BF16)
    tn = (((0,), (0,)), ((), ()))
    upd = [lax.dot_general(k_end[:, p * PAIR_KEY:(p + 1) * PAIR_KEY], v_b[:, p * PAIR_VAL:(p + 1) * PAIR_VAL], tn,
                           preferred_element_type=F32) for p in range(GLA_HEADS // 2)]
    upd = jnp.where(_pair_mask(GLA_DK, GLA_DV, GLA_HEADS // 2), jnp.concatenate(upd, axis=0), 0.0)
    decay = jnp.exp(_as_column(total))
    decay = jnp.concatenate([decay] * (PAIR_VAL // LANES), axis=1)
    return q_dec, decay, upd


def _advance_state(q_dec, decay, upd, state):
    o_inter = None
    if q_dec is not None:
        state_b = state.astype(BF16)
        o_inter = jnp.concatenate(
            [_dot(q_dec[:, p * PAIR_KEY:(p + 1) * PAIR_KEY], state_b[p * PAIR_KEY:(p + 1) * PAIR_KEY, :])
             for p in range(GLA_HEADS // 2)], axis=1)
    return o_inter, state * decay + upd


def _gla_kernel(tt, nt, with_out, *refs):
    if with_out:
        (qf_ref, kf_ref, vf_ref, gfw_ref, qb_ref, kb_ref, vb_ref, gbw_ref, s0f_ref, s0b_ref,
         of_ref, ob_ref, sf_ref, sb_ref) = refs
    else:
        kf_ref, vf_ref, gfw_ref, kb_ref, vb_ref, gbw_ref, s0f_ref, s0b_ref, sf_ref, sb_ref = refs
    j = pl.program_id(1)

    @pl.when(j == 0)
    def _():
        sf_ref[...] = s0f_ref[...]
        sb_ref[...] = s0b_ref[...]

    chunk_slices = [slice(c * GLA_CHUNK, (c + 1) * GLA_CHUNK) for c in range(tt // GLA_CHUNK)]
    fwd, bwd = [], []
    for sl in chunk_slices:
        g = gfw_ref[sl, :]
        fwd.append(dict(sl=sl, k=kf_ref[sl, :].astype(F32), v=vf_ref[sl, :], g_f=g[:, 0:GLA_KEY], g_b=g[:, GLA_KEY:],
                        q=qf_ref[sl, :].astype(F32) if with_out else None))
        bwd.append(dict(sl=sl, k=kb_ref[sl, :].astype(F32), v=vb_ref[sl, :], g_b=gbw_ref[sl, :][:, GLA_KEY:],
                        q=qb_ref[sl, :].astype(F32) if with_out else None))
    for d in fwd:
        d["gc_f"] = _chunk_cumsum(d["g_f"], False)
        if with_out:
            d["gc_b"] = _chunk_cumsum(d["g_b"], True)
    for d in bwd:
        d["gc_b"] = _chunk_cumsum(d["g_b"], True)
    if with_out:
        for d in fwd:
            d["res_f"] = _intra_products(d["q"], d["k"], d["gc_f"], False)
            d["res_b"] = _intra_products(d["q"], d["k"], d["gc_b"], True)
    for d in fwd:
        d["terms"] = _state_terms(d["q"], d["k"], d["v"], d["gc_f"], False)
    for d in bwd:
        d["terms"] = _state_terms(d["q"], d["k"], d["v"], d["gc_b"], True)
    if with_out:
        for d in fwd:
            scores = (_assemble_scores(d["res_f"], False) + _assemble_scores(d["res_b"], True)).astype(BF16)
            o_intra = []
            for p in range(GLA_HEADS // 2):
                v_p = d["v"][:, p * PAIR_VAL:(p + 1) * PAIR_VAL]
                vbd = jnp.where(_pair_mask(GLA_CHUNK, GLA_DV), jnp.concatenate([v_p, v_p], axis=0),
                                jnp.zeros((), BF16))
                o_intra.append(_dot(scores[:, p * 2 * GLA_CHUNK:(p + 1) * 2 * GLA_CHUNK], vbd))
            d["o_intra"] = jnp.concatenate(o_intra, axis=1)

    state = sf_ref[...]
    for d in fwd:
        o_inter, state = _advance_state(*d["terms"], state)
        if with_out:
            of_ref[d["sl"], :] = (d["o_intra"] + o_inter).astype(BF16)
    sf_ref[...] = state

    state = sb_ref[...]
    for d in reversed(bwd):
        o_inter, state = _advance_state(*d["terms"], state)
        if with_out:
            ob_ref[d["sl"], :] = o_inter.astype(BF16)
    sb_ref[...] = state


def _gla(q, k, v, g, s0f, s0b):
    with_out = q is not None
    bsz, t, _ = k.shape
    tt = min(TOKEN_TILE, t)
    assert t % tt == 0 and tt % GLA_CHUNK == 0
    nt = t // tt
    fwd = lambda w: pl.BlockSpec((None, tt, w), lambda b, j: (b, j, 0))
    bwd = lambda w: pl.BlockSpec((None, tt, w), lambda b, j: (b, nt - 1 - j, 0))
    st = pl.BlockSpec((None, GLA_KEY, PAIR_VAL), lambda b, j: (b, 0, 0))
    st_shape = jax.ShapeDtypeStruct((bsz, GLA_KEY, PAIR_VAL), F32)
    if with_out:
        ins = [q, k, v, g, q, k, v, g, s0f, s0b]
        in_specs = [fwd(GLA_KEY), fwd(GLA_KEY), fwd(GLA_VAL), fwd(2 * GLA_KEY),
                    bwd(GLA_KEY), bwd(GLA_KEY), bwd(GLA_VAL), bwd(2 * GLA_KEY), st, st]
        out_specs = [fwd(GLA_VAL), bwd(GLA_VAL), st, st]
        o_shape = jax.ShapeDtypeStruct((bsz, t, GLA_VAL), BF16)
        out_shape = [o_shape, o_shape, st_shape, st_shape]
    else:
        ins = [k, v, g, k, v, g, s0f, s0b]
        in_specs = [fwd(GLA_KEY), fwd(GLA_VAL), fwd(2 * GLA_KEY),
                    bwd(GLA_KEY), bwd(GLA_VAL), bwd(2 * GLA_KEY), st, st]
        out_specs = [st, st]
        out_shape = [st_shape, st_shape]
    return pl.pallas_call(
        functools.partial(_gla_kernel, tt, nt, with_out),
        grid=(bsz, nt),
        in_specs=in_specs,
        out_specs=out_specs,
        out_shape=out_shape,
        compiler_params=_params(2),
        name="gla_latent" if with_out else "gla_ctx",
    )(*ins)


def _exact_bf16_parts(x):
    hi = x.astype(BF16).astype(F32)
    r = x - hi
    mid = r.astype(BF16).astype(F32)
    lo = (r - mid).astype(BF16).astype(F32)
    return hi, mid, lo


def _first_index(values, best):
    idx = jnp.full_like(best, float(len(values) - 1))
    for i in reversed(range(len(values) - 1)):
        idx = jnp.where(values[i] >= best, float(i), idx)
    return idx


def _pick(rows, idx):
    out = rows[-1]
    for i in reversed(range(len(rows) - 1)):
        out = jnp.where(idx == float(i), rows[i], out)
    return out


def _route(logit_t, tm):
    row = lambda r: logit_t[r:r + 1, :]
    groups = [row(i) for i in range(N_GROUPS)]
    top = functools.reduce(jnp.maximum, groups)
    eg = [jnp.exp(x - top) for x in groups]
    total = functools.reduce(lambda a, b: a + b, eg)
    pg = [e / total for e in eg]
    p_g = functools.reduce(jnp.maximum, pg)
    g_idx = _first_index(pg, p_g)
    sel = [_pick([row(N_GROUPS + EXPERTS_PER_GROUP * g + j) for g in range(N_GROUPS)], g_idx)
           for j in range(EXPERTS_PER_GROUP)]
    top = functools.reduce(jnp.maximum, sel)
    ee = [jnp.exp(x - top) for x in sel]
    total = functools.reduce(lambda a, b: a + b, ee)
    pe = [e / total for e in ee]
    p1 = functools.reduce(jnp.maximum, pe)
    l1 = _first_index(pe, p1)
    pe2 = [jnp.where(l1 == float(j), -1.0, pe[j]) for j in range(EXPERTS_PER_GROUP)]
    p2 = functools.reduce(jnp.maximum, pe2)
    l2 = _first_index(pe2, p2)
    den = p1 + p2
    w1 = p1 / den * p_g
    w2 = p2 / den * p_g
    lo = jnp.minimum(l1, l2)
    hi = jnp.maximum(l1, l2)
    pair = lo * (7.0 - lo) * 0.5 + (hi - lo - 1.0)
    cls = g_idx * PAIRS_PER_GROUP + pair
    w_lo = jnp.where(l1 < l2, w1, w2)
    w_hi = jnp.where(l1 < l2, w2, w1)
    cls_id = lax.broadcasted_iota(jnp.int32, (ROUTE_W, tm), 0).astype(F32)
    onehot = jnp.where(cls_id == cls, 1.0, 0.0)
    before = _dot(onehot.astype(BF16), _tri(tm, True, strict=True))
    count = jnp.sum(onehot, axis=1, keepdims=True)
    chunks = jnp.floor((count + (CHUNK_ROWS - 1.0)) * (1.0 / CHUNK_ROWS))
    first_chunk = _dot(_tri(ROUTE_W, False, strict=True),
                       jnp.broadcast_to(chunks, (ROUTE_W, LANES)).astype(BF16))[:, 0:1]
    pos_row = jnp.sum(onehot * (CHUNK_ROWS * first_chunk + before), axis=0, keepdims=True)
    return pos_row, w_lo, w_hi, chunks


def _slab_targets(slab_axis):
    shape = (LOCAL_SLAB_ROWS, 1) if slab_axis == 0 else (1, LOCAL_SLAB_ROWS)
    slab_row = lax.broadcasted_iota(jnp.int32, shape, slab_axis)
    sub = _mod_pow2(slab_row, SLAB_ROWS)
    token_row = (CHUNK_ROWS * _div_pow2(slab_row, SLAB_ROWS) + _div_pow2(sub, 2)).astype(F32)
    half = _mod_pow2(sub, 2)
    return [jnp.where(half == h, token_row, -1.0) for h in range(2)]


def _slab_sort_matrices(pos, targets):
    return [jnp.where(t == pos, 1.0, 0.0).astype(BF16) for t in targets]


def _mix_out_kernel(tm, xn_ref, of_ref, ob_ref, sr_ref, ya_ref, mod_ref, lnp_ref, gn_ref, wo_ref, wr_ref, br_ref,
                    x1_ref, hxs_ref, pos_ref, chunks_ref):
    subs = [slice(s, s + MIX_SUB_TILE) for s in range(0, tm, MIX_SUB_TILE)]
    yb = []
    for rows in subs:
        o = of_ref[rows, :].astype(F32) + ob_ref[rows, :].astype(F32)
        sr = sr_ref[rows, :].astype(F32)
        heads = []
        for h in range(GLA_HEADS):
            sl = slice(h * GLA_DV, (h + 1) * GLA_DV)
            oh = o[:, sl]
            ms = jnp.mean(oh * oh, axis=-1, keepdims=True)
            heads.append((oh * lax.rsqrt(ms + RMS_EPS) * gn_ref[...] * sr[:, sl]).astype(BF16))
        yb.append(jnp.concatenate([ya_ref[rows, :]] + heads, axis=1))
    xn = [xn_ref[rows, :] for rows in subs]
    y = [_dot(y_in, wo_ref[...]) for y_in in yb]
    h2_b = []
    for rows, xn_s, y_s in zip(subs, xn, y):
        x1 = _layer_norm(xn_s + mod_ref[0:1, :] * y_s, lnp_ref[0:1, :], lnp_ref[1:2, :])
        x1_ref[rows, :] = x1.astype(BF16)
        h2_b.append((x1 * mod_ref[1:2, :] + mod_ref[2:3, :]).astype(BF16))
    logit_t = [(_dot(h2_s, wr_ref[...]) + br_ref[...]).T for h2_s in h2_b]
    per_sort = SORT_TILE // MIX_SUB_TILE
    routes = []
    for s in range(tm // SORT_TILE):
        routes.append(_route(jnp.concatenate(logit_t[s * per_sort:(s + 1) * per_sort], axis=1), SORT_TILE))
    rec_id = lax.broadcasted_iota(jnp.int32, (ROUTE_W, SORT_TILE), 0)
    targets = _slab_targets(0)
    for s, (pos_row, w_lo, w_hi, chunks) in enumerate(routes):
        rec_t = jnp.zeros((ROUTE_W, SORT_TILE), F32)
        for i, part in enumerate(_exact_bf16_parts(w_lo) + _exact_bf16_parts(w_hi)):
            rec_t = jnp.where(rec_id == i, part, rec_t)
        rec_b = rec_t.T.astype(BF16)
        h2_s = jnp.concatenate(h2_b[s * per_sort:(s + 1) * per_sort], axis=0)
        sort_lo, sort_hi = _slab_sort_matrices(pos_row, targets)
        pay_lo = jnp.concatenate([h2_s[:, 0:HALF_W], rec_b], axis=1)
        pay_hi = jnp.concatenate([h2_s[:, HALF_W:], jnp.zeros((SORT_TILE, ROUTE_W), BF16)], axis=1)
        slabs = _dot(sort_lo, pay_lo) + _dot(sort_hi, pay_hi)
        for c in range(SLAB_IN_W // LANES):
            hxs_ref[c, s * LOCAL_SLAB_ROWS:(s + 1) * LOCAL_SLAB_ROWS, :] = slabs[:, c * LANES:(c + 1) * LANES]
        pos_ref[8 * s:8 * (s + 1), :] = jnp.broadcast_to(pos_row, (8, SORT_TILE))
        chunks_ref[8 * s:8 * (s + 1), :] = jnp.broadcast_to(chunks, (ROUTE_W, LANES)).T[0:8, :]


def _mix_out(xn, o_f, o_b, sr, ya, mod, lnp, gn, w_out_b, wr, br):
    bsz, t, _ = xn.shape
    assert t % SORT_TILE == 0
    n_sort = max(n for n in (1, 2, 4) if n * SORT_TILE <= TOKEN_TILE and t % (n * SORT_TILE) == 0)
    tm = n_sort * SORT_TILE
    n_t = t // tm
    tok = lambda w: pl.BlockSpec((None, tm, w), lambda b, i: (b, i, 0))
    full = lambda a: pl.BlockSpec(a.shape, lambda b, i: (0,) * a.ndim)
    flat = lambda rows, w: pl.BlockSpec((rows, w), lambda b, i: (b * n_t + i, 0))
    return pl.pallas_call(
        functools.partial(_mix_out_kernel, tm),
        grid=(bsz, n_t),
        in_specs=[
            tok(D_MODEL), tok(GLA_VAL), tok(GLA_VAL), tok(GLA_VAL), tok(CONV_CH),
            pl.BlockSpec((None, 3, D_MODEL), lambda b, i: (b, 0, 0)),
            full(lnp), full(gn), full(w_out_b), full(wr), full(br),
        ],
        out_specs=[tok(D_MODEL),
                   pl.BlockSpec((SLAB_IN_W // LANES, n_sort * LOCAL_SLAB_ROWS, LANES),
                                lambda b, i: (0, b * n_t + i, 0)),
                   flat(n_sort * 8, SORT_TILE), flat(n_sort * 8, ROUTE_W)],
        out_shape=[
            jax.ShapeDtypeStruct((bsz, t, D_MODEL), BF16),
            jax.ShapeDtypeStruct((SLAB_IN_W // LANES, bsz * t // SORT_TILE * LOCAL_SLAB_ROWS, LANES), F32),
            jax.ShapeDtypeStruct((bsz * t // SORT_TILE * 8, SORT_TILE), F32),
            jax.ShapeDtypeStruct((bsz * t // SORT_TILE * 8, ROUTE_W), F32),
        ],
        compiler_params=_params(2),
        name="mix_out",
    )(xn, o_f, o_b, sr, ya, mod, lnp, gn, w_out_b, wr, br)


def _moe_kernel(n_chunks, nused_ref, lo_ref, hi_ref, src_ref, dst_ref,
                hxs_hbm, w1l_ref, w3l_ref, w2l_ref, w1h_ref, w3h_ref, w2h_ref, out_hbm, gbuf, obuf, gsem, ssem):
    tile_rows = CHUNKS_PER_TILE * SLAB_ROWS
    i = pl.program_id(0)
    n_used = nused_ref[0]
    slot = lax.bitwise_and(i, 1)

    def slab(chunk):
        return pl.ds(pl.multiple_of(chunk * SLAB_ROWS, SLAB_ROWS), SLAB_ROWS)

    def gather_copy(tile, buf_slot, j):
        chunk = src_ref[tile * CHUNKS_PER_TILE + j]
        return pltpu.make_async_copy(hxs_hbm.at[:, slab(chunk), :], gbuf.at[buf_slot, :, slab(j), :],
                                     gsem.at[buf_slot])

    def scatter_copy(tile, buf_slot, j):
        chunk = dst_ref[tile * CHUNKS_PER_TILE + j]
        return pltpu.make_async_copy(obuf.at[buf_slot, :, slab(j), :], out_hbm.at[:, slab(chunk), :],
                                     ssem.at[buf_slot])

    def start_gather(tile, buf_slot):
        for j in range(CHUNKS_PER_TILE):
            gather_copy(tile, buf_slot, j).start(priority=j % 2)

    def wait_gather(buf_slot):
        pltpu.make_async_copy(hxs_hbm.at[:, pl.ds(0, tile_rows), :], gbuf.at[buf_slot], gsem.at[buf_slot]).wait()

    def wait_scatter(buf_slot):
        pltpu.make_async_copy(obuf.at[buf_slot], out_hbm.at[:, pl.ds(0, tile_rows), :], ssem.at[buf_slot]).wait()

    @pl.when(i == 0)
    def _():
        start_gather(0, 0)
        obuf[...] = jnp.zeros(obuf.shape, F32)
        min_chunks = SORT_TILE // CHUNK_ROWS
        tail_rows = (LOCAL_CHUNKS - min_chunks) * SLAB_ROWS
        fills = [pltpu.make_async_copy(
            obuf.at[s], out_hbm.at[:, pl.ds((n_chunks + s * CHUNKS_PER_TILE) * SLAB_ROWS, tile_rows), :],
            ssem.at[s]) for s in range(2)]
        fills += [pltpu.make_async_copy(
            obuf.at[0, :, pl.ds(0, tail_rows), :],
            out_hbm.at[:, pl.ds((t * LOCAL_CHUNKS + min_chunks) * SLAB_ROWS, tail_rows), :], ssem.at[0])
            for t in range(n_chunks // LOCAL_CHUNKS)]
        for fill in fills:
            fill.start()
        for fill in fills:
            fill.wait()

    @pl.when(i + 1 < n_used)
    def _():
        start_gather(i + 1, 1 - slot)

    @pl.when(i < n_used)
    def _():
        wait_gather(slot)

        @pl.when(i >= 2)
        def _():
            wait_scatter(slot)

        def lane_block(c, half):
            return jnp.concatenate(
                [gbuf[slot, c, pl.ds(2 * r + half, CHUNKS_PER_TILE, stride=SLAB_ROWS), :]
                 for r in range(CHUNK_ROWS)], axis=0)

        n_blk = HALF_W // LANES
        xb = jnp.concatenate([lane_block(c, 0) for c in range(n_blk)]
                             + [lane_block(c, 1) for c in range(n_blk)], axis=1).astype(BF16)
        rec = lane_block(n_blk, 0)
        w_lo = rec[:, 0:1] + rec[:, 1:2] + rec[:, 2:3]
        w_hi = rec[:, 3:4] + rec[:, 4:5] + rec[:, 5:6]

        gate = [_dot(xb, w1_ref[...]) for w1_ref in (w1l_ref, w1h_ref)]
        up = [_dot(xb, w3_ref[...]) for w3_ref in (w3l_ref, w3h_ref)]
        act = [(_silu(g) * u).astype(BF16) for g, u in zip(gate, up)]
        e_lo, e_hi = [_dot(a, w2_ref[...]) for a, w2_ref in zip(act, (w2l_ref, w2h_ref))]
        y = w_lo * e_lo + w_hi * e_hi
        for r in range(CHUNK_ROWS):
            rows = slice(r * CHUNKS_PER_TILE, (r + 1) * CHUNKS_PER_TILE)
            for half in range(2):
                for c in range(n_blk):
                    col = half * HALF_W + c * LANES
                    obuf[slot, c, pl.ds(2 * r + half, CHUNKS_PER_TILE, stride=SLAB_ROWS), :] = (
                        y[rows, col:col + LANES])

        for j in range(CHUNKS_PER_TILE):
            scatter_copy(i, slot, j).start(priority=j % 2)

        @pl.when(i == n_used - 1)
        def _():
            wait_scatter(slot)

            @pl.when(i >= 1)
            def _():
                wait_scatter(1 - slot)


def _moe(hxs, src, dst, n_used, tile_lo, tile_hi, w1_b, w3_b, w2_b):
    n_chunks = hxs.shape[1] // SLAB_ROWS
    tile_rows = CHUNKS_PER_TILE * SLAB_ROWS
    n_steps = src.shape[0] // CHUNKS_PER_TILE
    wspec = lambda which, shape: pl.BlockSpec(
        (None,) + shape, (lambda i, nu, lo, hi, s, d: (lo[i], 0, 0)) if which == 0 else
        (lambda i, nu, lo, hi, s, d: (hi[i], 0, 0)))
    grid_spec = pltpu.PrefetchScalarGridSpec(
        num_scalar_prefetch=5,
        grid=(n_steps,),
        in_specs=[
            pl.BlockSpec(memory_space=pl.ANY),
            wspec(0, (D_MODEL, D_EXPERT)), wspec(0, (D_MODEL, D_EXPERT)), wspec(0, (D_EXPERT, D_MODEL)),
            wspec(1, (D_MODEL, D_EXPERT)), wspec(1, (D_MODEL, D_EXPERT)), wspec(1, (D_EXPERT, D_MODEL)),
        ],
        out_specs=pl.BlockSpec(memory_space=pl.ANY),
        scratch_shapes=[
            pltpu.VMEM((2, SLAB_IN_W // LANES, tile_rows, LANES), F32),
            pltpu.VMEM((2, SLAB_OUT_W // LANES, tile_rows, LANES), F32),
            pltpu.SemaphoreType.DMA((2,)),
            pltpu.SemaphoreType.DMA((2,)),
        ],
    )
    return pl.pallas_call(
        functools.partial(_moe_kernel, n_chunks),
        grid_spec=grid_spec,
        out_shape=jax.ShapeDtypeStruct((SLAB_OUT_W // LANES, (n_chunks + 2 * CHUNKS_PER_TILE) * SLAB_ROWS, LANES), F32),
        compiler_params=_params(1),
        name="moe",
    )(n_used, tile_lo, tile_hi, src, dst, hxs, w1_b, w3_b, w2_b, w1_b, w3_b, w2_b)


def _final_kernel(n_sort, x1_ref, moe_ref, pos_ref, mod_ref, lnp_ref, o_ref):
    moe = []
    targets = _slab_targets(1)
    for s in range(n_sort):
        slab_rows = slice(s * LOCAL_SLAB_ROWS, (s + 1) * LOCAL_SLAB_ROWS)
        moe_b = jnp.concatenate([moe_ref[c, slab_rows, :] for c in range(SLAB_OUT_W // LANES)], axis=1).astype(BF16)
        sort_lo, sort_hi = _slab_sort_matrices(_as_column(pos_ref[8 * s:8 * s + 1, :])[:, 0:1], targets)
        for r in range(0, SORT_TILE, MIX_SUB_TILE):
            rows = slice(r, r + MIX_SUB_TILE)
            moe.append((s * SORT_TILE + r,
                        jnp.concatenate([_dot(sort_lo[rows, :], moe_b), _dot(sort_hi[rows, :], moe_b)], axis=1)))
    for start, moe_s in moe:
        rows = slice(start, start + MIX_SUB_TILE)
        o_ref[rows, :] = _layer_norm(DEEPNORM_ALPHA * x1_ref[rows, :].astype(F32) + mod_ref[...] * moe_s,
                                     lnp_ref[0:1, :], lnp_ref[1:2, :])


def _final(x1, moe, pos, g2, lnp):
    bsz, t, _ = x1.shape
    n_sort = max(n for n in (1, 2, 4) if n * SORT_TILE <= TOKEN_TILE and t % (n * SORT_TILE) == 0)
    tm = n_sort * SORT_TILE
    n_t = t // tm
    flat = lambda rows, w: pl.BlockSpec((rows, w), lambda b, i: (b * n_t + i, 0))
    return pl.pallas_call(
        functools.partial(_final_kernel, n_sort),
        grid=(bsz, n_t),
        in_specs=[
            pl.BlockSpec((None, tm, D_MODEL), lambda b, i: (b, i, 0)),
            pl.BlockSpec((SLAB_OUT_W // LANES, n_sort * LOCAL_SLAB_ROWS, LANES), lambda b, i: (0, b * n_t + i, 0)),
            flat(n_sort * 8, SORT_TILE),
            pl.BlockSpec((None, 1, D_MODEL), lambda b, i: (b, 0, 0)),
            pl.BlockSpec(lnp.shape, lambda b, i: (0, 0)),
        ],
        out_specs=pl.BlockSpec((None, tm, D_MODEL), lambda b, i: (b, i, 0)),
        out_shape=jax.ShapeDtypeStruct((bsz, t, D_MODEL), F32),
        compiler_params=_params(2),
        name="final",
    )(x1, moe, pos, g2, lnp)


def _pair_tables():
    lo, hi = [], []
    for g in range(N_GROUPS):
        for a in range(EXPERTS_PER_GROUP):
            for b in range(a + 1, EXPERTS_PER_GROUP):
                lo.append(g * EXPERTS_PER_GROUP + a)
                hi.append(g * EXPERTS_PER_GROUP + b)
    return jnp.array(lo, jnp.int32), jnp.array(hi, jnp.int32)


def _moe_plan(chunks, n_sort_tiles):
    n_cls = N_CLASSES
    hp = lax.Precision.HIGHEST
    m = chunks.reshape(n_sort_tiles, 8, ROUTE_W)[:, 0, :n_cls].astype(jnp.int32)
    a_end = jnp.cumsum(m, axis=0)
    a_start = a_end - m
    per_cls = a_end[-1]
    padded = (per_cls + CHUNKS_PER_TILE - 1) // CHUNKS_PER_TILE * CHUNKS_PER_TILE
    g_end = jnp.cumsum(padded)
    g_start = g_end - padded
    local_off = jnp.cumsum(m, axis=1) - m
    seg = jnp.arange(n_sort_tiles, dtype=jnp.int32)[:, None] * LOCAL_CHUNKS + local_off - a_start
    n_steps = -(-(n_sort_tiles * LOCAL_CHUNKS) // CHUNKS_PER_TILE) + n_cls
    p = jnp.arange(n_steps * CHUNKS_PER_TILE, dtype=jnp.int32)
    cls_p = jnp.minimum(jnp.sum((g_end[None, :] <= p[:, None]).astype(jnp.int32), axis=1), n_cls - 1)
    onehot = (cls_p[:, None] == jnp.arange(n_cls, dtype=jnp.int32)[None, :]).astype(F32)
    pick = lambda tab: jnp.dot(onehot, tab.astype(F32), precision=hp)
    u = p - pick(g_start[:, None])[:, 0].astype(jnp.int32)
    valid = u < pick(per_cls[:, None])[:, 0].astype(jnp.int32)
    a_end_p = pick(a_end.T).astype(jnp.int32)
    seg_p = pick(seg.T).astype(jnp.int32)
    tile_p = jnp.sum((a_end_p <= u[:, None]).astype(jnp.int32), axis=1)
    hit = jnp.arange(n_sort_tiles, dtype=jnp.int32)[None, :] == tile_p[:, None]
    src = jnp.sum(jnp.where(hit, seg_p, 0), axis=1) + u
    pad_dst = n_sort_tiles * LOCAL_CHUNKS + (p // CHUNKS_PER_TILE) % 2 * CHUNKS_PER_TILE + p % CHUNKS_PER_TILE
    dst = jnp.where(valid, src, pad_dst).astype(jnp.int32)
    src = jnp.where(valid, src, 0).astype(jnp.int32)
    n_used = g_end[-1:] // CHUNKS_PER_TILE
    step = jnp.arange(n_steps, dtype=jnp.int32)
    tile_cls = jnp.sum((g_end[None, :] // CHUNKS_PER_TILE <= step[:, None]).astype(jnp.int32), axis=1)
    pair_lo, pair_hi = _pair_tables()
    pair_oh = (jnp.minimum(tile_cls, N_CLASSES - 1)[:, None] == jnp.arange(N_CLASSES)[None, :]).astype(jnp.int32)
    tile_lo = jnp.sum(pair_oh * pair_lo[None, :], axis=1).astype(jnp.int32)
    tile_hi = jnp.sum(pair_oh * pair_hi[None, :], axis=1).astype(jnp.int32)
    return src, dst, n_used.astype(jnp.int32), tile_lo, tile_hi


def kernel(x, c, ctx, c_ctx, ln_in_g, ln_in_b, w_ada, b_ada, w_in, conv_w, conv_b, gate_w2_fwd, gate_b_fwd,
           gate_w2_bwd, gate_b_bwd, gla_norm_g, w_out, ln1_g, ln1_b, router_group_w, router_group_b,
           router_expert_w, router_expert_b, expert_w1, expert_w3, expert_w2, ln2_g, ln2_b):
    bsz, t, _ = x.shape
    n_tok = bsz * t
    l = 0
    rows = -(-(bsz + 1) // 8) * 8
    cond = jnp.zeros((rows, D_MODEL), F32).at[:bsz].set(c).at[bsz].set(c_ctx)
    ada = _ada(cond, w_ada[l], b_ada[l][None, :])
    sh1, sc1, g1, sh2, sc2, g2 = [ada[:, i * D_MODEL:(i + 1) * D_MODEL] for i in range(6)]

    w_in_b = w_in[l].astype(BF16)
    lnp_in = jnp.stack([ln_in_g, ln_in_b])
    zero = jnp.zeros((GLA_GATE_RANK, GLA_KEY), F32)
    w2cat = jnp.concatenate([jnp.concatenate([gate_w2_fwd[l], zero], axis=1),
                             jnp.concatenate([zero, gate_w2_bwd[l]], axis=1)], axis=0).astype(BF16)
    gbias = jnp.concatenate([gate_b_fwd[l], gate_b_bwd[l]])[None, :]

    n_ctx = ctx.shape[1]
    mod_ctx = jnp.stack([1.0 + sc1[bsz], sh1[bsz]])[None]
    k_c, v_c, g_c = [a.reshape(bsz, n_ctx, -1) for a in _proj(
        ctx.reshape(1, bsz * n_ctx, D_MODEL), mod_ctx, lnp_in, w_in_b, conv_w[l], conv_b[l][None, :], w2cat, gbias,
        False)]
    zero_state = jnp.zeros((bsz, GLA_KEY, PAIR_VAL), F32)
    s_f, s_b = _gla(None, k_c, v_c, g_c, zero_state, zero_state)

    mod1 = jnp.stack([1.0 + sc1[:bsz], sh1[:bsz]], axis=1)
    ya, q, k, v, sr, g, xn, w1_b, w3_b, w2_b = _proj(
        x, mod1, lnp_in, w_in_b, conv_w[l], conv_b[l][None, :], w2cat, gbias, True,
        to_cast=(expert_w1[l], expert_w3[l], expert_w2[l]))
    o_f, o_b, _, _ = _gla(q, k, v, g, s_f, s_b)

    mod2 = jnp.stack([g1[:bsz], 1.0 + sc2[:bsz], sh2[:bsz]], axis=1)
    lnp1 = jnp.stack([ln1_g[l], ln1_b[l]])
    wr = jnp.zeros((D_MODEL, ROUTE_W), F32)
    wr = wr.at[:, :N_GROUPS].set(router_group_w[l]).at[:, N_GROUPS:N_GROUPS + N_EXPERTS].set(router_expert_w[l])
    br = jnp.zeros((1, ROUTE_W), F32)
    br = br.at[0, :N_GROUPS].set(router_group_b[l]).at[0, N_GROUPS:N_GROUPS + N_EXPERTS].set(router_expert_b[l])
    x1, hxs, pos, chunks = _mix_out(xn, o_f, o_b, sr, ya, mod2, lnp1, gla_norm_g[l][None, :],
                                    w_out[l].astype(BF16), wr.astype(BF16), br)

    src, dst, n_used, tile_lo, tile_hi = _moe_plan(chunks, n_tok // SORT_TILE)
    moe = _moe(hxs, src, dst, n_used, tile_lo, tile_hi, w1_b, w3_b, w2_b)

    return _final(x1, moe, pos, g2[:bsz][:, None, :], jnp.stack([ln2_g[l], ln2_b[l]]))
```

```python
import functools

import jax
import jax.numpy as jnp
from jax import lax
from jax.experimental import pallas as pl
from jax.experimental.pallas import tpu as pltpu

F32 = jnp.float32
BF16 = jnp.bfloat16

D_MODEL = 1024
GRID_W = 64
CONV_CH = 512
GLA_HEADS = 4
GLA_DK = 64
GLA_DV = 128
GLA_KEY = GLA_HEADS * GLA_DK
GLA_VAL = GLA_HEADS * GLA_DV
PAIR_KEY = 2 * GLA_DK
PAIR_VAL = 2 * GLA_DV
GLA_GATE_RANK = 16
GLA_TAU = 16.0
OFF_AB = 0
OFF_AC = OFF_AB + CONV_CH
OFF_AX = OFF_AC + CONV_CH
OFF_Q = OFF_AX + CONV_CH
OFF_K = OFF_Q + GLA_KEY
OFF_V = OFF_K + GLA_KEY
OFF_R = OFF_V + GLA_VAL
OFF_GF = OFF_R + GLA_VAL
D_PROJ = OFF_GF + 2 * GLA_GATE_RANK
N_GROUPS = 4
EXPERTS_PER_GROUP = 4
N_EXPERTS = N_GROUPS * EXPERTS_PER_GROUP
D_EXPERT = 512
PAIRS_PER_GROUP = 6
N_CLASSES = N_GROUPS * PAIRS_PER_GROUP
LN_EPS = 1e-5
RMS_EPS = 1e-6
DEPTH = 1
DEEPNORM_ALPHA = (2.0 * DEPTH) ** 0.25

LANES = 128
GLA_CHUNK = 64
GLA_SUB = 16
N_SUB = GLA_CHUNK // GLA_SUB
ROUTE_W = LANES
HALF_W = D_MODEL // 2
SLAB_IN_W = HALF_W + ROUTE_W
SLAB_OUT_W = HALF_W
ADA_COL_TILE = 1024
TOKEN_TILE = 1024
SORT_TILE = 256
MIX_SUB_TILE = 128
MOE_TILE = 256
CHUNK_ROWS = 4
SLAB_ROWS = 2 * CHUNK_ROWS
LOCAL_CHUNKS = -(-(SORT_TILE + N_CLASSES * (CHUNK_ROWS - 1)) // CHUNK_ROWS)
LOCAL_SLAB_ROWS = LOCAL_CHUNKS * SLAB_ROWS
CHUNKS_PER_TILE = MOE_TILE // CHUNK_ROWS
VMEM_LIMIT = 56 * 1024 * 1024


def _params(n_axes, vmem=VMEM_LIMIT):
    return pltpu.CompilerParams(dimension_semantics=("arbitrary",) * n_axes, vmem_limit_bytes=vmem)


def _dot(a, b):
    return jnp.dot(a, b, preferred_element_type=F32)


def _div_pow2(x, d):
    assert d & (d - 1) == 0
    return lax.shift_right_logical(x, jnp.int32(d.bit_length() - 1))


def _mod_pow2(x, d):
    assert d & (d - 1) == 0
    return lax.bitwise_and(x, jnp.int32(d - 1))


def _split2(x):
    hi = x.astype(BF16)
    lo = (x - hi.astype(F32)).astype(BF16)
    return hi, lo


def _dot3(a, b):
    ah, al = _split2(a)
    bh, bl = _split2(b)
    return _dot(ah, bh) + _dot(ah, bl) + _dot(al, bh)


def _silu(x):
    return x * (0.5 * jnp.tanh(0.5 * x) + 0.5)


def _layer_norm(x, g, b):
    mu = jnp.mean(x, axis=-1, keepdims=True)
    xc = x - mu
    var = jnp.mean(xc * xc, axis=-1, keepdims=True)
    return xc * lax.rsqrt(var + LN_EPS) * g + b


def _ada_kernel(c_ref, w_ref, b_ref, o_ref):
    o_ref[...] = _dot3(_silu(c_ref[...]), w_ref[...]) + b_ref[...]


def _ada(cond, w_ada, b_ada):
    rows = cond.shape[0]
    n_out = w_ada.shape[1]
    tn = ADA_COL_TILE
    return pl.pallas_call(
        _ada_kernel,
        grid=(n_out // tn,),
        in_specs=[
            pl.BlockSpec((rows, D_MODEL), lambda j: (0, 0)),
            pl.BlockSpec((D_MODEL, tn), lambda j: (0, j)),
            pl.BlockSpec((1, tn), lambda j: (0, j)),
        ],
        out_specs=pl.BlockSpec((rows, tn), lambda j: (0, j)),
        out_shape=jax.ShapeDtypeStruct((rows, n_out), F32),
        compiler_params=_params(1),
        name="ada",
    )(cond, w_ada, b_ada)


def _log_sigmoid(z):
    return jnp.minimum(z, 0.0) - jnp.log(1.0 + jnp.exp(-jnp.abs(z)))


def _proj_kernel(latent, tm, n_cast, x_ref, mod_ref, lnp_ref, w_ref, cw_ref, cb_ref, w2_ref, gbias_ref, *refs):
    cast_refs, out_refs, cast_out_refs = refs[:n_cast], refs[n_cast:len(refs) - n_cast], refs[len(refs) - n_cast:]
    for src_ref, dst_ref in zip(cast_refs, cast_out_refs):
        dst_ref[...] = src_ref[...].astype(BF16)
    x = x_ref[...]
    xn = _layer_norm(x, lnp_ref[0:1, :], lnp_ref[1:2, :])
    h = xn * mod_ref[0:1, :] + mod_ref[1:2, :]
    hb = h.astype(BF16)
    if latent:
        ya_ref, q_ref, k_ref, v_ref, sr_ref, g_ref, xn_ref = out_refs
        xn_ref[...] = DEEPNORM_ALPHA * xn
        p = _dot(hb, w_ref[:, OFF_AB:OFF_Q])
        a_b = p[:, 0:CONV_CH]
        u = p[:, CONV_CH:2 * CONV_CH] * p[:, 2 * CONV_CH:3 * CONV_CH]
        pos = _mod_pow2(lax.broadcasted_iota(jnp.int32, (tm, 1), 0), GRID_W)
        u_prev = jnp.where(pos == 0, 0.0, pltpu.roll(u, 1, 0))
        u_next = jnp.where(pos == GRID_W - 1, 0.0, pltpu.roll(u, tm - 1, 0))
        conv = u_prev * cw_ref[0:1, :] + u * cw_ref[1:2, :] + u_next * cw_ref[2:3, :] + cb_ref[...]
        ya_ref[...] = (a_b * conv).astype(BF16)
        qk = _dot(hb, w_ref[:, OFF_Q:OFF_V])
        q_ref[...] = (qk[:, 0:GLA_KEY] * (GLA_DK ** -0.5)).astype(BF16)
        k_ref[...] = qk[:, GLA_KEY:].astype(BF16)
        r = _dot(hb, w_ref[:, OFF_R:OFF_GF])
        sr_ref[...] = _silu(r).astype(BF16)
    else:
        k_ref, v_ref, g_ref = out_refs
        k_ref[...] = _dot(hb, w_ref[:, OFF_K:OFF_V]).astype(BF16)
    v_ref[...] = _dot(hb, w_ref[:, OFF_V:OFF_R]).astype(BF16)
    low = _dot(hb, w_ref[:, OFF_GF:D_PROJ])
    z = _dot(low.astype(BF16), w2_ref[...]) + gbias_ref[...]
    g_ref[...] = _log_sigmoid(z) * (1.0 / GLA_TAU)


def _proj(x, mod, lnp, w_in_b, conv_w, conv_b, w2cat, gbias, latent, to_cast=()):
    bsz, t, _ = x.shape
    tm = min(TOKEN_TILE, t)
    assert t % tm == 0 and tm % GRID_W == 0
    n_t = t // tm
    tok = lambda w: pl.BlockSpec((None, tm, w), lambda b, i: (b, i, 0))
    full = lambda a: pl.BlockSpec(a.shape, lambda b, i: (0,) * a.ndim)
    sliced = [a.reshape(bsz * n_t, -1, a.shape[-1]) for a in to_cast]
    step = lambda a: pl.BlockSpec((None,) + a.shape[1:], lambda b, i: (b * n_t + i, 0, 0))
    widths = ([(CONV_CH, BF16), (GLA_KEY, BF16)] if latent else []) + [(GLA_KEY, BF16), (GLA_VAL, BF16)]
    widths += ([(GLA_VAL, BF16)] if latent else []) + [(2 * GLA_KEY, F32)]
    widths += [(D_MODEL, F32)] if latent else []
    outs = pl.pallas_call(
        functools.partial(_proj_kernel, latent, tm, len(sliced)),
        grid=(bsz, n_t),
        in_specs=[
            tok(D_MODEL),
            pl.BlockSpec((None, 2, D_MODEL), lambda b, i: (b, 0, 0)),
            full(lnp), full(w_in_b), full(conv_w), full(conv_b), full(w2cat), full(gbias),
        ] + [step(a) for a in sliced],
        out_specs=[tok(w) for w, _ in widths] + [step(a) for a in sliced],
        out_shape=([jax.ShapeDtypeStruct((bsz, t, w), dt) for w, dt in widths]
                   + [jax.ShapeDtypeStruct(a.shape, BF16) for a in sliced]),
        compiler_params=_params(2),
        name="proj_latent" if latent else "proj_ctx",
    )(x, mod, lnp, w_in_b, conv_w, conv_b, w2cat, gbias, *sliced)
    return list(outs[:len(widths)]) + [o.reshape(a.shape) for o, a in zip(outs[len(widths):], to_cast)]


def _tri(n, reverse, strict=False):
    i = lax.broadcasted_iota(jnp.int32, (n, n), 0)
    j = lax.broadcasted_iota(jnp.int32, (n, n), 1)
    if strict:
        m = (j > i) if reverse else (j < i)
    else:
        m = (j >= i) if reverse else (j <= i)
    return jnp.where(m, 1.0, 0.0).astype(BF16)


def _chunk_cumsum(g, reverse):
    tri = _tri(GLA_CHUNK, reverse)
    g_hi, g_lo = _split2(g)
    return _dot(tri, g_hi) + _dot(tri, g_lo)


def _as_column(row):
    return jnp.broadcast_to(row, (LANES, row.shape[1])).T


def _sub_anchors(gc, reverse):
    zero = jnp.zeros((1, GLA_KEY), F32)
    if reverse:
        return [gc[GLA_SUB * (a + 1):GLA_SUB * (a + 1) + 1] for a in range(N_SUB - 1)] + [zero]
    return [zero] + [gc[GLA_SUB * a - 1:GLA_SUB * a] for a in range(1, N_SUB)]


def _score_pairs(reverse):
    return [(a, b) for a in range(N_SUB) for b in range(N_SUB) if (b >= a if reverse else b <= a)]


def _intra_products(q, k, gc, reverse):
    r = _sub_anchors(gc, reverse)
    anchor = jnp.concatenate([jnp.broadcast_to(ra, (GLA_SUB, GLA_KEY)) for ra in r], axis=0)
    gcb = gc - anchor
    qt = q * jnp.exp(gcb)
    kt = k * jnp.exp(-gcb)
    rows = []
    for a, b in _score_pairs(reverse):
        qa = qt[GLA_SUB * a:GLA_SUB * (a + 1)]
        if a != b:
            qa = qa * jnp.exp(r[a] - r[b])
        rows.append(qa)
    qp = jnp.concatenate(rows, axis=0).astype(BF16)
    width = GLA_HEADS * GLA_CHUNK
    rr = lax.broadcasted_iota(jnp.int32, (width, GLA_KEY), 0)
    cc = lax.broadcasted_iota(jnp.int32, (width, GLA_KEY), 1)
    kbd = jnp.where(_div_pow2(rr, GLA_CHUNK) == _div_pow2(cc, GLA_DK),
                    jnp.concatenate([kt] * GLA_HEADS, axis=0), 0.0)
    return lax.dot_general(qp, kbd.astype(BF16), (((1,), (1,)), ((), ())), preferred_element_type=F32)


def _assemble_scores(res, reverse):
    pairs = _score_pairs(reverse)
    width = GLA_HEADS * GLA_CHUNK
    col = _mod_pow2(lax.broadcasted_iota(jnp.int32, (GLA_SUB, width), 1), GLA_CHUNK)
    col_blk = _div_pow2(col, GLA_SUB)
    col_pos = _mod_pow2(col, GLA_SUB)
    row_pos = lax.broadcasted_iota(jnp.int32, (GLA_SUB, width), 0)
    causal = (col_pos >= row_pos) if reverse else (col_pos <= row_pos)
    blocks = []
    for a in range(N_SUB):
        acc = jnp.zeros((GLA_SUB, width), F32)
        for idx, (pa, pb) in enumerate(pairs):
            if pa != a:
                continue
            keep = col_blk == pb
            if pa == pb:
                keep = keep & causal
            acc = acc + jnp.where(keep, res[GLA_SUB * idx:GLA_SUB * (idx + 1)], 0.0)
        blocks.append(acc)
    return jnp.concatenate(blocks, axis=0)


def _pair_mask(rows_per_head, cols_per_head, n_row_pairs=1):
    shape = (n_row_pairs * 2 * rows_per_head, 2 * cols_per_head)
    rr = _mod_pow2(lax.broadcasted_iota(jnp.int32, shape, 0), 2 * rows_per_head)
    cc = lax.broadcasted_iota(jnp.int32, shape, 1)
    return _div_pow2(rr, rows_per_head) == _div_pow2(cc, cols_per_head)


def _state_terms(q, k, v_b, gc, reverse):
    total = gc[0:1] if reverse else gc[GLA_CHUNK - 1:GLA_CHUNK]
    q_dec = None if q is None else (q * jnp.exp(gc)).astype(BF16)
    k_end = (k * jnp.exp(total - gc)).astype(BF16)
    tn = (((0,), (0,)), ((), ()))
    upd = [lax.dot_general(k_end[:, p * PAIR_KEY:(p + 1) * PAIR_KEY], v_b[:, p * PAIR_VAL:(p + 1) * PAIR_VAL], tn,
                           preferred_element_type=F32) for p in range(GLA_HEADS // 2)]
    upd = jnp.where(_pair_mask(GLA_DK, GLA_DV, GLA_HEADS // 2), jnp.concatenate(upd, axis=0), 0.0)
    decay = jnp.exp(_as_column(total))
    decay = jnp.concatenate([decay] * (PAIR_VAL // LANES), axis=1)
    return q_dec, decay, upd


def _advance_state(q_dec, decay, upd, state):
    o_inter = None
    if q_dec is not None:
        state_b = state.astype(BF16)
        o_inter = jnp.concatenate(
            [_dot(q_dec[:, p * PAIR_KEY:(p + 1) * PAIR_KEY], state_b[p * PAIR_KEY:(p + 1) * PAIR_KEY, :])
             for p in range(GLA_HEADS // 2)], axis=1)
    return o_inter, state * decay + upd


def _gla_kernel(tt, nt, with_out, *refs):
    if with_out:
        (qf_ref, kf_ref, vf_ref, gfw_ref, qb_ref, kb_ref, vb_ref, gbw_ref, s0f_ref, s0b_ref,
         of_ref, ob_ref, sf_ref, sb_ref) = refs
    else:
        kf_ref, vf_ref, gfw_ref, kb_ref, vb_ref, gbw_ref, s0f_ref, s0b_ref, sf_ref, sb_ref = refs
    j = pl.program_id(1)

    @pl.when(j == 0)
    def _():
        sf_ref[...] = s0f_ref[...]
        sb_ref[...] = s0b_ref[...]

    chunk_slices = [slice(c * GLA_CHUNK, (c + 1) * GLA_CHUNK) for c in range(tt // GLA_CHUNK)]
    fwd, bwd = [], []
    for sl in chunk_slices:
        g = gfw_ref[sl, :]
        fwd.append(dict(sl=sl, k=kf_ref[sl, :].astype(F32), v=vf_ref[sl, :], g_f=g[:, 0:GLA_KEY], g_b=g[:, GLA_KEY:],
                        q=qf_ref[sl, :].astype(F32) if with_out else None))
        bwd.append(dict(sl=sl, k=kb_ref[sl, :].astype(F32), v=vb_ref[sl, :], g_b=gbw_ref[sl, :][:, GLA_KEY:],
                        q=qb_ref[sl, :].astype(F32) if with_out else None))
    for d in fwd:
        d["gc_f"] = _chunk_cumsum(d["g_f"], False)
        if with_out:
            d["gc_b"] = _chunk_cumsum(d["g_b"], True)
    for d in bwd:
        d["gc_b"] = _chunk_cumsum(d["g_b"], True)
    if with_out:
        for d in fwd:
            d["res_f"] = _intra_products(d["q"], d["k"], d["gc_f"], False)
            d["res_b"] = _intra_products(d["q"], d["k"], d["gc_b"], True)
    for d in fwd:
        d["terms"] = _state_terms(d["q"], d["k"], d["v"], d["gc_f"], False)
    for d in bwd:
        d["terms"] = _state_terms(d["q"], d["k"], d["v"], d["gc_b"], True)
    if with_out:
        for d in fwd:
            scores = (_assemble_scores(d["res_f"], False) + _assemble_scores(d["res_b"], True)).astype(BF16)
            o_intra = []
            for p in range(GLA_HEADS // 2):
                v_p = d["v"][:, p * PAIR_VAL:(p + 1) * PAIR_VAL]
                vbd = jnp.where(_pair_mask(GLA_CHUNK, GLA_DV), jnp.concatenate([v_p, v_p], axis=0),
                                jnp.zeros((), BF16))
                o_intra.append(_dot(scores[:, p * 2 * GLA_CHUNK:(p + 1) * 2 * GLA_CHUNK], vbd))
            d["o_intra"] = jnp.concatenate(o_intra, axis=1)

    state = sf_ref[...]
    for d in fwd:
        o_inter, state = _advance_state(*d["terms"], state)
        if with_out:
            of_ref[d["sl"], :] = (d["o_intra"] + o_inter).astype(BF16)
    sf_ref[...] = state

    state = sb_ref[...]
    for d in reversed(bwd):
        o_inter, state = _advance_state(*d["terms"], state)
        if with_out:
            ob_ref[d["sl"], :] = o_inter.astype(BF16)
    sb_ref[...] = state


def _gla(q, k, v, g, s0f, s0b):
    with_out = q is not None
    bsz, t, _ = k.shape
    tt = min(TOKEN_TILE, t)
    assert t % tt == 0 and tt % GLA_CHUNK == 0
    nt = t // tt
    fwd = lambda w: pl.BlockSpec((None, tt, w), lambda b, j: (b, j, 0))
    bwd = lambda w: pl.BlockSpec((None, tt, w), lambda b, j: (b, nt - 1 - j, 0))
    st = pl.BlockSpec((None, GLA_KEY, PAIR_VAL), lambda b, j: (b, 0, 0))
    st_shape = jax.ShapeDtypeStruct((bsz, GLA_KEY, PAIR_VAL), F32)
    if with_out:
        ins = [q, k, v, g, q, k, v, g, s0f, s0b]
        in_specs = [fwd(GLA_KEY), fwd(GLA_KEY), fwd(GLA_VAL), fwd(2 * GLA_KEY),
                    bwd(GLA_KEY), bwd(GLA_KEY), bwd(GLA_VAL), bwd(2 * GLA_KEY), st, st]
        out_specs = [fwd(GLA_VAL), bwd(GLA_VAL), st, st]
        o_shape = jax.ShapeDtypeStruct((bsz, t, GLA_VAL), BF16)
        out_shape = [o_shape, o_shape, st_shape, st_shape]
    else:
        ins = [k, v, g, k, v, g, s0f, s0b]
        in_specs = [fwd(GLA_KEY), fwd(GLA_VAL), fwd(2 * GLA_KEY),
                    bwd(GLA_KEY), bwd(GLA_VAL), bwd(2 * GLA_KEY), st, st]
        out_specs = [st, st]
        out_shape = [st_shape, st_shape]
    return pl.pallas_call(
        functools.partial(_gla_kernel, tt, nt, with_out),
        grid=(bsz, nt),
        in_specs=in_specs,
        out_specs=out_specs,
        out_shape=out_shape,
        compiler_params=_params(2),
        name="gla_latent" if with_out else "gla_ctx",
    )(*ins)


def _exact_bf16_parts(x):
    hi = x.astype(BF16).astype(F32)
    r = x - hi
    mid = r.astype(BF16).astype(F32)
    lo = (r - mid).astype(BF16).astype(F32)
    return hi, mid, lo


def _first_index(values, best):
    idx = jnp.full_like(best, float(len(values) - 1))
    for i in reversed(range(len(values) - 1)):
        idx = jnp.where(values[i] >= best, float(i), idx)
    return idx


def _pick(rows, idx):
    out = rows[-1]
    for i in reversed(range(len(rows) - 1)):
        out = jnp.where(idx == float(i), rows[i], out)
    return out


def _route(logit_t, tm):
    row = lambda r: logit_t[r:r + 1, :]
    groups = [row(i) for i in range(N_GROUPS)]
    top = functools.reduce(jnp.maximum, groups)
    eg = [jnp.exp(x - top) for x in groups]
    total = functools.reduce(lambda a, b: a + b, eg)
    pg = [e / total for e in eg]
    p_g = functools.reduce(jnp.maximum, pg)
    g_idx = _first_index(pg, p_g)
    sel = [_pick([row(N_GROUPS + EXPERTS_PER_GROUP * g + j) for g in range(N_GROUPS)], g_idx)
           for j in range(EXPERTS_PER_GROUP)]
    top = functools.reduce(jnp.maximum, sel)
    ee = [jnp.exp(x - top) for x in sel]
    total = functools.reduce(lambda a, b: a + b, ee)
    pe = [e / total for e in ee]
    p1 = functools.reduce(jnp.maximum, pe)
    l1 = _first_index(pe, p1)
    pe2 = [jnp.where(l1 == float(j), -1.0, pe[j]) for j in range(EXPERTS_PER_GROUP)]
    p2 = functools.reduce(jnp.maximum, pe2)
    l2 = _first_index(pe2, p2)
    den = p1 + p2
    w1 = p1 / den * p_g
    w2 = p2 / den * p_g
    lo = jnp.minimum(l1, l2)
    hi = jnp.maximum(l1, l2)
    pair = lo * (7.0 - lo) * 0.5 + (hi - lo - 1.0)
    cls = g_idx * PAIRS_PER_GROUP + pair
    w_lo = jnp.where(l1 < l2, w1, w2)
    w_hi = jnp.where(l1 < l2, w2, w1)
    cls_id = lax.broadcasted_iota(jnp.int32, (ROUTE_W, tm), 0).astype(F32)
    onehot = jnp.where(cls_id == cls, 1.0, 0.0)
    before = _dot(onehot.astype(BF16), _tri(tm, True, strict=True))
    count = jnp.sum(onehot, axis=1, keepdims=True)
    chunks = jnp.floor((count + (CHUNK_ROWS - 1.0)) * (1.0 / CHUNK_ROWS))
    first_chunk = _dot(_tri(ROUTE_W, False, strict=True),
                       jnp.broadcast_to(chunks, (ROUTE_W, LANES)).astype(BF16))[:, 0:1]
    pos_row = jnp.sum(onehot * (CHUNK_ROWS * first_chunk + before), axis=0, keepdims=True)
    return pos_row, w_lo, w_hi, chunks


def _slab_targets(slab_axis):
    shape = (LOCAL_SLAB_ROWS, 1) if slab_axis == 0 else (1, LOCAL_SLAB_ROWS)
    slab_row = lax.broadcasted_iota(jnp.int32, shape, slab_axis)
    sub = _mod_pow2(slab_row, SLAB_ROWS)
    token_row = (CHUNK_ROWS * _div_pow2(slab_row, SLAB_ROWS) + _div_pow2(sub, 2)).astype(F32)
    half = _mod_pow2(sub, 2)
    return [jnp.where(half == h, token_row, -1.0) for h in range(2)]


def _slab_sort_matrices(pos, targets):
    return [jnp.where(t == pos, 1.0, 0.0).astype(BF16) for t in targets]


def _mix_out_kernel(tm, xn_ref, of_ref, ob_ref, sr_ref, ya_ref, mod_ref, lnp_ref, gn_ref, wo_ref, wr_ref, br_ref,
                    x1_ref, hxs_ref, pos_ref, chunks_ref):
    subs = [slice(s, s + MIX_SUB_TILE) for s in range(0, tm, MIX_SUB_TILE)]
    yb = []
    for rows in subs:
        o = of_ref[rows, :].astype(F32) + ob_ref[rows, :].astype(F32)
        sr = sr_ref[rows, :].astype(F32)
        heads = []
        for h in range(GLA_HEADS):
            sl = slice(h * GLA_DV, (h + 1) * GLA_DV)
            oh = o[:, sl]
            ms = jnp.mean(oh * oh, axis=-1, keepdims=True)
            heads.append((oh * lax.rsqrt(ms + RMS_EPS) * gn_ref[...] * sr[:, sl]).astype(BF16))
        yb.append(jnp.concatenate([ya_ref[rows, :]] + heads, axis=1))
    xn = [xn_ref[rows, :] for rows in subs]
    y = [_dot(y_in, wo_ref[...]) for y_in in yb]
    h2_b = []
    for rows, xn_s, y_s in zip(subs, xn, y):
        x1 = _layer_norm(xn_s + mod_ref[0:1, :] * y_s, lnp_ref[0:1, :], lnp_ref[1:2, :])
        x1_ref[rows, :] = x1.astype(BF16)
        h2_b.append((x1 * mod_ref[1:2, :] + mod_ref[2:3, :]).astype(BF16))
    logit_t = [(_dot(h2_s, wr_ref[...]) + br_ref[...]).T for h2_s in h2_b]
    per_sort = SORT_TILE // MIX_SUB_TILE
    routes = []
    for s in range(tm // SORT_TILE):
        routes.append(_route(jnp.concatenate(logit_t[s * per_sort:(s + 1) * per_sort], axis=1), SORT_TILE))
    rec_id = lax.broadcasted_iota(jnp.int32, (ROUTE_W, SORT_TILE), 0)
    targets = _slab_targets(0)
    for s, (pos_row, w_lo, w_hi, chunks) in enumerate(routes):
        rec_t = jnp.zeros((ROUTE_W, SORT_TILE), F32)
        for i, part in enumerate(_exact_bf16_parts(w_lo) + _exact_bf16_parts(w_hi)):
            rec_t = jnp.where(rec_id == i, part, rec_t)
        rec_b = rec_t.T.astype(BF16)
        h2_s = jnp.concatenate(h2_b[s * per_sort:(s + 1) * per_sort], axis=0)
        sort_lo, sort_hi = _slab_sort_matrices(pos_row, targets)
        pay_lo = jnp.concatenate([h2_s[:, 0:HALF_W], rec_b], axis=1)
        pay_hi = jnp.concatenate([h2_s[:, HALF_W:], jnp.zeros((SORT_TILE, ROUTE_W), BF16)], axis=1)
        slabs = _dot(sort_lo, pay_lo) + _dot(sort_hi, pay_hi)
        for c in range(SLAB_IN_W // LANES):
            hxs_ref[c, s * LOCAL_SLAB_ROWS:(s + 1) * LOCAL_SLAB_ROWS, :] = slabs[:, c * LANES:(c + 1) * LANES]
        pos_ref[8 * s:8 * (s + 1), :] = jnp.broadcast_to(pos_row, (8, SORT_TILE))
        chunks_ref[8 * s:8 * (s + 1), :] = jnp.broadcast_to(chunks, (ROUTE_W, LANES)).T[0:8, :]


def _mix_out(xn, o_f, o_b, sr, ya, mod, lnp, gn, w_out_b, wr, br):
    bsz, t, _ = xn.shape
    assert t % SORT_TILE == 0
    n_sort = max(n for n in (1, 2, 4) if n * SORT_TILE <= TOKEN_TILE and t % (n * SORT_TILE) == 0)
    tm = n_sort * SORT_TILE
    n_t = t // tm
    tok = lambda w: pl.BlockSpec((None, tm, w), lambda b, i: (b, i, 0))
    full = lambda a: pl.BlockSpec(a.shape, lambda b, i: (0,) * a.ndim)
    flat = lambda rows, w: pl.BlockSpec((rows, w), lambda b, i: (b * n_t + i, 0))
    return pl.pallas_call(
        functools.partial(_mix_out_kernel, tm),
        grid=(bsz, n_t),
        in_specs=[
            tok(D_MODEL), tok(GLA_VAL), tok(GLA_VAL), tok(GLA_VAL), tok(CONV_CH),
            pl.BlockSpec((None, 3, D_MODEL), lambda b, i: (b, 0, 0)),
            full(lnp), full(gn), full(w_out_b), full(wr), full(br),
        ],
        out_specs=[tok(D_MODEL),
                   pl.BlockSpec((SLAB_IN_W // LANES, n_sort * LOCAL_SLAB_ROWS, LANES),
                                lambda b, i: (0, b * n_t + i, 0)),
                   flat(n_sort * 8, SORT_TILE), flat(n_sort * 8, ROUTE_W)],
        out_shape=[
            jax.ShapeDtypeStruct((bsz, t, D_MODEL), BF16),
            jax.ShapeDtypeStruct((SLAB_IN_W // LANES, bsz * t // SORT_TILE * LOCAL_SLAB_ROWS, LANES), F32),
            jax.ShapeDtypeStruct((bsz * t // SORT_TILE * 8, SORT_TILE), F32),
            jax.ShapeDtypeStruct((bsz * t // SORT_TILE * 8, ROUTE_W), F32),
        ],
        compiler_params=_params(2),
        name="mix_out",
    )(xn, o_f, o_b, sr, ya, mod, lnp, gn, w_out_b, wr, br)


def _moe_kernel(n_chunks, n_steps, nused_ref, lo_ref, hi_ref, src_ref, dst_ref,
                hxs_hbm, w1_hbm, w3_hbm, w2_hbm, out_hbm, gbuf, obuf, w1buf, w3buf, w2buf, gsem, ssem, wsem):
    tile_rows = CHUNKS_PER_TILE * SLAB_ROWS
    n_used = nused_ref[0]
    expert_tabs = (lo_ref, hi_ref)

    def weight_copies(role, expert, wslot):
        return [pltpu.make_async_copy(w_hbm.at[expert], wbuf.at[role, wslot], wsem.at[role, wslot])
                for w_hbm, wbuf in ((w1_hbm, w1buf), (w3_hbm, w3buf), (w2_hbm, w2buf))]

    def slab(chunk):
        return pl.ds(pl.multiple_of(chunk * SLAB_ROWS, SLAB_ROWS), SLAB_ROWS)

    def gather_copy(tile, buf_slot, j):
        chunk = src_ref[tile * CHUNKS_PER_TILE + j]
        return pltpu.make_async_copy(hxs_hbm.at[:, slab(chunk), :], gbuf.at[buf_slot, :, slab(j), :],
                                     gsem.at[buf_slot])

    def scatter_copy(tile, buf_slot, j):
        chunk = dst_ref[tile * CHUNKS_PER_TILE + j]
        return pltpu.make_async_copy(obuf.at[buf_slot, :, slab(j), :], out_hbm.at[:, slab(chunk), :],
                                     ssem.at[buf_slot])

    def start_gather(tile, buf_slot):
        for j in range(CHUNKS_PER_TILE):
            gather_copy(tile, buf_slot, j).start(priority=j % 2)

    def wait_gather(buf_slot):
        pltpu.make_async_copy(hxs_hbm.at[:, pl.ds(0, tile_rows), :], gbuf.at[buf_slot], gsem.at[buf_slot]).wait()

    def wait_scatter(buf_slot):
        pltpu.make_async_copy(obuf.at[buf_slot], out_hbm.at[:, pl.ds(0, tile_rows), :], ssem.at[buf_slot]).wait()

    start_gather(0, 0)
    for role, tab_ref in enumerate(expert_tabs):
        for copy in weight_copies(role, tab_ref[0], 0):
            copy.start()
    obuf[...] = jnp.zeros(obuf.shape, F32)
    min_chunks = SORT_TILE // CHUNK_ROWS
    tail_rows = (LOCAL_CHUNKS - min_chunks) * SLAB_ROWS
    fills = [pltpu.make_async_copy(
        obuf.at[s], out_hbm.at[:, pl.ds((n_chunks + s * CHUNKS_PER_TILE) * SLAB_ROWS, tile_rows), :],
        ssem.at[s]) for s in range(2)]
    fills += [pltpu.make_async_copy(
        obuf.at[0, :, pl.ds(0, tail_rows), :],
        out_hbm.at[:, pl.ds((t * LOCAL_CHUNKS + min_chunks) * SLAB_ROWS, tail_rows), :], ssem.at[0])
        for t in range(n_chunks // LOCAL_CHUNKS)]
    for fill in fills:
        fill.start()
    for fill in fills:
        fill.wait()

    def tile_step(i, wslots):
        slot = lax.bitwise_and(i, 1)

        @pl.when(i + 1 < n_used)
        def _():
            start_gather(i + 1, 1 - slot)

        new_wslots = []
        for role, (tab_ref, prev) in enumerate(zip(expert_tabs, wslots)):
            expert = tab_ref[i]
            changed = jnp.logical_or(i == 0, expert != tab_ref[jnp.maximum(i - 1, 0)])
            cur = jnp.where(changed, 1 - prev, prev)

            @pl.when(changed)
            def _():
                for copy in weight_copies(role, expert, cur):
                    copy.wait()

            nxt = tab_ref[jnp.minimum(i + 1, n_steps - 1)]

            @pl.when(jnp.logical_and(i + 1 < n_used, nxt != expert))
            def _():
                for copy in weight_copies(role, nxt, 1 - cur):
                    copy.start()

            new_wslots.append(cur)

        wait_gather(slot)

        @pl.when(i >= 2)
        def _():
            wait_scatter(slot)

        def lane_block(c, half):
            return jnp.concatenate(
                [gbuf[slot, c, pl.ds(2 * r + half, CHUNKS_PER_TILE, stride=SLAB_ROWS), :]
                 for r in range(CHUNK_ROWS)], axis=0)

        n_blk = HALF_W // LANES
        xb = jnp.concatenate([lane_block(c, 0) for c in range(n_blk)]
                             + [lane_block(c, 1) for c in range(n_blk)], axis=1).astype(BF16)
        rec = lane_block(n_blk, 0)
        w_lo = rec[:, 0:1] + rec[:, 1:2] + rec[:, 2:3]
        w_hi = rec[:, 3:4] + rec[:, 4:5] + rec[:, 5:6]

        gate = [_dot(xb, w1buf[role, ws]) for role, ws in enumerate(new_wslots)]
        up = [_dot(xb, w3buf[role, ws]) for role, ws in enumerate(new_wslots)]
        act = [(_silu(g) * u).astype(BF16) for g, u in zip(gate, up)]
        e_lo, e_hi = [_dot(a, w2buf[role, ws]) for a, (role, ws) in zip(act, enumerate(new_wslots))]
        y = w_lo * e_lo + w_hi * e_hi
        for r in range(CHUNK_ROWS):
            rows = slice(r * CHUNKS_PER_TILE, (r + 1) * CHUNKS_PER_TILE)
            for half in range(2):
                for c in range(n_blk):
                    col = half * HALF_W + c * LANES
                    obuf[slot, c, pl.ds(2 * r + half, CHUNKS_PER_TILE, stride=SLAB_ROWS), :] = (
                        y[rows, col:col + LANES])

        for j in range(CHUNKS_PER_TILE):
            scatter_copy(i, slot, j).start(priority=j % 2)
        return tuple(new_wslots)

    one = jnp.ones((), jnp.int32)
    lax.fori_loop(0, n_used, tile_step, (one, one))

    last_slot = lax.bitwise_and(n_used - 1, 1)
    wait_scatter(last_slot)

    @pl.when(n_used >= 2)
    def _():
        wait_scatter(1 - last_slot)


def _moe(hxs, src, dst, n_used, tile_lo, tile_hi, w1_b, w3_b, w2_b):
    n_chunks = hxs.shape[1] // SLAB_ROWS
    tile_rows = CHUNKS_PER_TILE * SLAB_ROWS
    n_steps = src.shape[0] // CHUNKS_PER_TILE
    grid_spec = pltpu.PrefetchScalarGridSpec(
        num_scalar_prefetch=5,
        grid=(1,),
        in_specs=[pl.BlockSpec(memory_space=pl.ANY)] * 4,
        out_specs=pl.BlockSpec(memory_space=pl.ANY),
        scratch_shapes=[
            pltpu.VMEM((2, SLAB_IN_W // LANES, tile_rows, LANES), F32),
            pltpu.VMEM((2, SLAB_OUT_W // LANES, tile_rows, LANES), F32),
            pltpu.VMEM((2, 2, D_MODEL, D_EXPERT), BF16),
            pltpu.VMEM((2, 2, D_MODEL, D_EXPERT), BF16),
            pltpu.VMEM((2, 2, D_EXPERT, D_MODEL), BF16),
            pltpu.SemaphoreType.DMA((2,)),
            pltpu.SemaphoreType.DMA((2,)),
            pltpu.SemaphoreType.DMA((2, 2)),
        ],
    )
    return pl.pallas_call(
        functools.partial(_moe_kernel, n_chunks, n_steps),
        grid_spec=grid_spec,
        out_shape=jax.ShapeDtypeStruct((SLAB_OUT_W // LANES, (n_chunks + 2 * CHUNKS_PER_TILE) * SLAB_ROWS, LANES), F32),
        compiler_params=_params(1),
        name="moe",
    )(n_used, tile_lo, tile_hi, src, dst, hxs, w1_b, w3_b, w2_b)


def _final_kernel(n_sort, x1_ref, moe_ref, pos_ref, mod_ref, lnp_ref, o_ref):
    moe = []
    targets = _slab_targets(1)
    for s in range(n_sort):
        slab_rows = slice(s * LOCAL_SLAB_ROWS, (s + 1) * LOCAL_SLAB_ROWS)
        moe_b = jnp.concatenate([moe_ref[c, slab_rows, :] for c in range(SLAB_OUT_W // LANES)], axis=1).astype(BF16)
        sort_lo, sort_hi = _slab_sort_matrices(_as_column(pos_ref[8 * s:8 * s + 1, :])[:, 0:1], targets)
        for r in range(0, SORT_TILE, MIX_SUB_TILE):
            rows = slice(r, r + MIX_SUB_TILE)
            moe.append((s * SORT_TILE + r,
                        jnp.concatenate([_dot(sort_lo[rows, :], moe_b), _dot(sort_hi[rows, :], moe_b)], axis=1)))
    for start, moe_s in moe:
        rows = slice(start, start + MIX_SUB_TILE)
        o_ref[rows, :] = _layer_norm(DEEPNORM_ALPHA * x1_ref[rows, :].astype(F32) + mod_ref[...] * moe_s,
                                     lnp_ref[0:1, :], lnp_ref[1:2, :])


def _final(x1, moe, pos, g2, lnp):
    bsz, t, _ = x1.shape
    n_sort = max(n for n in (1, 2, 4) if n * SORT_TILE <= TOKEN_TILE and t % (n * SORT_TILE) == 0)
    tm = n_sort * SORT_TILE
    n_t = t // tm
    flat = lambda rows, w: pl.BlockSpec((rows, w), lambda b, i: (b * n_t + i, 0))
    return pl.pallas_call(
        functools.partial(_final_kernel, n_sort),
        grid=(bsz, n_t),
        in_specs=[
            pl.BlockSpec((None, tm, D_MODEL), lambda b, i: (b, i, 0)),
            pl.BlockSpec((SLAB_OUT_W // LANES, n_sort * LOCAL_SLAB_ROWS, LANES), lambda b, i: (0, b * n_t + i, 0)),
            flat(n_sort * 8, SORT_TILE),
            pl.BlockSpec((None, 1, D_MODEL), lambda b, i: (b, 0, 0)),
            pl.BlockSpec(lnp.shape, lambda b, i: (0, 0)),
        ],
        out_specs=pl.BlockSpec((None, tm, D_MODEL), lambda b, i: (b, i, 0)),
        out_shape=jax.ShapeDtypeStruct((bsz, t, D_MODEL), F32),
        compiler_params=_params(2),
        name="final",
    )(x1, moe, pos, g2, lnp)


def _pair_tables():
    lo, hi = [], []
    for g in range(N_GROUPS):
        for a in range(EXPERTS_PER_GROUP):
            for b in range(a + 1, EXPERTS_PER_GROUP):
                lo.append(g * EXPERTS_PER_GROUP + a)
                hi.append(g * EXPERTS_PER_GROUP + b)
    return jnp.array(lo, jnp.int32), jnp.array(hi, jnp.int32)


def _moe_plan(chunks, n_sort_tiles):
    n_cls = N_CLASSES
    hp = lax.Precision.HIGHEST
    m = chunks.reshape(n_sort_tiles, 8, ROUTE_W)[:, 0, :n_cls].astype(jnp.int32)
    a_end = jnp.cumsum(m, axis=0)
    a_start = a_end - m
    per_cls = a_end[-1]
    padded = (per_cls + CHUNKS_PER_TILE - 1) // CHUNKS_PER_TILE * CHUNKS_PER_TILE
    g_end = jnp.cumsum(padded)
    g_start = g_end - padded
    local_off = jnp.cumsum(m, axis=1) - m
    seg = jnp.arange(n_sort_tiles, dtype=jnp.int32)[:, None] * LOCAL_CHUNKS + local_off - a_start
    n_steps = -(-(n_sort_tiles * LOCAL_CHUNKS) // CHUNKS_PER_TILE) + n_cls
    p = jnp.arange(n_steps * CHUNKS_PER_TILE, dtype=jnp.int32)
    cls_p = jnp.minimum(jnp.sum((g_end[None, :] <= p[:, None]).astype(jnp.int32), axis=1), n_cls - 1)
    onehot = (cls_p[:, None] == jnp.arange(n_cls, dtype=jnp.int32)[None, :]).astype(F32)
    pick = lambda tab: jnp.dot(onehot, tab.astype(F32), precision=hp)
    u = p - pick(g_start[:, None])[:, 0].astype(jnp.int32)
    valid = u < pick(per_cls[:, None])[:, 0].astype(jnp.int32)
    a_end_p = pick(a_end.T).astype(jnp.int32)
    seg_p = pick(seg.T).astype(jnp.int32)
    tile_p = jnp.sum((a_end_p <= u[:, None]).astype(jnp.int32), axis=1)
    hit = jnp.arange(n_sort_tiles, dtype=jnp.int32)[None, :] == tile_p[:, None]
    src = jnp.sum(jnp.where(hit, seg_p, 0), axis=1) + u
    pad_dst = n_sort_tiles * LOCAL_CHUNKS + (p // CHUNKS_PER_TILE) % 2 * CHUNKS_PER_TILE + p % CHUNKS_PER_TILE
    dst = jnp.where(valid, src, pad_dst).astype(jnp.int32)
    src = jnp.where(valid, src, 0).astype(jnp.int32)
    n_used = g_end[-1:] // CHUNKS_PER_TILE
    step = jnp.arange(n_steps, dtype=jnp.int32)
    tile_cls = jnp.sum((g_end[None, :] // CHUNKS_PER_TILE <= step[:, None]).astype(jnp.int32), axis=1)
    pair_lo, pair_hi = _pair_tables()
    pair_oh = (jnp.minimum(tile_cls, N_CLASSES - 1)[:, None] == jnp.arange(N_CLASSES)[None, :]).astype(jnp.int32)
    tile_lo = jnp.sum(pair_oh * pair_lo[None, :], axis=1).astype(jnp.int32)
    tile_hi = jnp.sum(pair_oh * pair_hi[None, :], axis=1).astype(jnp.int32)
    return src, dst, n_used.astype(jnp.int32), tile_lo, tile_hi


def kernel(x, c, ctx, c_ctx, ln_in_g, ln_in_b, w_ada, b_ada, w_in, conv_w, conv_b, gate_w2_fwd, gate_b_fwd,
           gate_w2_bwd, gate_b_bwd, gla_norm_g, w_out, ln1_g, ln1_b, router_group_w, router_group_b,
           router_expert_w, router_expert_b, expert_w1, expert_w3, expert_w2, ln2_g, ln2_b):
    bsz, t, _ = x.shape
    n_tok = bsz * t
    l = 0
    rows = -(-(bsz + 1) // 8) * 8
    cond = jnp.zeros((rows, D_MODEL), F32).at[:bsz].set(c).at[bsz].set(c_ctx)
    ada = _ada(cond, w_ada[l], b_ada[l][None, :])
    sh1, sc1, g1, sh2, sc2, g2 = [ada[:, i * D_MODEL:(i + 1) * D_MODEL] for i in range(6)]

    w_in_b = w_in[l].astype(BF16)
    lnp_in = jnp.stack([ln_in_g, ln_in_b])
    zero = jnp.zeros((GLA_GATE_RANK, GLA_KEY), F32)
    w2cat = jnp.concatenate([jnp.concatenate([gate_w2_fwd[l], zero], axis=1),
                             jnp.concatenate([zero, gate_w2_bwd[l]], axis=1)], axis=0).astype(BF16)
    gbias = jnp.concatenate([gate_b_fwd[l], gate_b_bwd[l]])[None, :]

    n_ctx = ctx.shape[1]
    mod_ctx = jnp.stack([1.0 + sc1[bsz], sh1[bsz]])[None]
    k_c, v_c, g_c = [a.reshape(bsz, n_ctx, -1) for a in _proj(
        ctx.reshape(1, bsz * n_ctx, D_MODEL), mod_ctx, lnp_in, w_in_b, conv_w[l], conv_b[l][None, :], w2cat, gbias,
        False)]
    zero_state = jnp.zeros((bsz, GLA_KEY, PAIR_VAL), F32)
    s_f, s_b = _gla(None, k_c, v_c, g_c, zero_state, zero_state)

    mod1 = jnp.stack([1.0 + sc1[:bsz], sh1[:bsz]], axis=1)
    ya, q, k, v, sr, g, xn, w1_b, w3_b, w2_b = _proj(
        x, mod1, lnp_in, w_in_b, conv_w[l], conv_b[l][None, :], w2cat, gbias, True,
        to_cast=(expert_w1[l], expert_w3[l], expert_w2[l]))
    o_f, o_b, _, _ = _gla(q, k, v, g, s_f, s_b)

    mod2 = jnp.stack([g1[:bsz], 1.0 + sc2[:bsz], sh2[:bsz]], axis=1)
    lnp1 = jnp.stack([ln1_g[l], ln1_b[l]])
    wr = jnp.zeros((D_MODEL, ROUTE_W), F32)
    wr = wr.at[:, :N_GROUPS].set(router_group_w[l]).at[:, N_GROUPS:N_GROUPS + N_EXPERTS].set(router_expert_w[l])
    br = jnp.zeros((1, ROUTE_W), F32)
    br = br.at[0, :N_GROUPS].set(router_group_b[l]).at[0, N_GROUPS:N_GROUPS + N_EXPERTS].set(router_expert_b[l])
    x1, hxs, pos, chunks = _mix_out(xn, o_f, o_b, sr, ya, mod2, lnp1, gla_norm_g[l][None, :],
                                    w_out[l].astype(BF16), wr.astype(BF16), br)

    src, dst, n_used, tile_lo, tile_hi = _moe_plan(chunks, n_tok // SORT_TILE)
    moe = _moe(hxs, src, dst, n_used, tile_lo, tile_hi, w1_b, w3_b, w2_b)

    return _final(x1, moe, pos, g2[:bsz][:, None, :], jnp.stack([ln2_g[l], ln2_b[l]]))
```

```python
import functools

import jax
import jax.numpy as jnp
from jax import lax
from jax.experimental import pallas as pl
from jax.experimental.pallas import tpu as pltpu

F32 = jnp.float32
BF16 = jnp.bfloat16

D_MODEL = 1024
GRID_W = 64
CONV_CH = 512
GLA_HEADS = 4
GLA_DK = 64
GLA_DV = 128
GLA_KEY = GLA_HEADS * GLA_DK
GLA_VAL = GLA_HEADS * GLA_DV
PAIR_KEY = 2 * GLA_DK
PAIR_VAL = 2 * GLA_DV
GLA_GATE_RANK = 16
GLA_TAU = 16.0
OFF_AB = 0
OFF_AC = OFF_AB + CONV_CH
OFF_AX = OFF_AC + CONV_CH
OFF_Q = OFF_AX + CONV_CH
OFF_K = OFF_Q + GLA_KEY
OFF_V = OFF_K + GLA_KEY
OFF_R = OFF_V + GLA_VAL
OFF_GF = OFF_R + GLA_VAL
D_PROJ = OFF_GF + 2 * GLA_GATE_RANK
N_GROUPS = 4
EXPERTS_PER_GROUP = 4
N_EXPERTS = N_GROUPS * EXPERTS_PER_GROUP
D_EXPERT = 512
PAIRS_PER_GROUP = 6
N_CLASSES = N_GROUPS * PAIRS_PER_GROUP
LN_EPS = 1e-5
RMS_EPS = 1e-6
DEPTH = 1
DEEPNORM_ALPHA = (2.0 * DEPTH) ** 0.25

LANES = 128
GLA_CHUNK = 64
GLA_SUB = 16
N_SUB = GLA_CHUNK // GLA_SUB
ROUTE_W = LANES
HALF_W = D_MODEL // 2
SLAB_IN_W = HALF_W + ROUTE_W
SLAB_OUT_W = HALF_W
ADA_COL_TILE = 1024
TOKEN_TILE = 1024
SORT_TILE = 256
MIX_SUB_TILE = 128
MOE_TILE = 256
CHUNK_ROWS = 4
SLAB_ROWS = 2 * CHUNK_ROWS
LOCAL_CHUNKS = -(-(SORT_TILE + N_CLASSES * (CHUNK_ROWS - 1)) // CHUNK_ROWS)
LOCAL_SLAB_ROWS = LOCAL_CHUNKS * SLAB_ROWS
CHUNKS_PER_TILE = MOE_TILE // CHUNK_ROWS
VMEM_LIMIT = 56 * 1024 * 1024


def _params(n_axes, vmem=VMEM_LIMIT):
    return pltpu.CompilerParams(dimension_semantics=("arbitrary",) * n_axes, vmem_limit_bytes=vmem)


def _dot(a, b):
    return jnp.dot(a, b, preferred_element_type=F32)


def _div_pow2(x, d):
    assert d & (d - 1) == 0
    return lax.shift_right_logical(x, jnp.int32(d.bit_length() - 1))


def _mod_pow2(x, d):
    assert d & (d - 1) == 0
    return lax.bitwise_and(x, jnp.int32(d - 1))


def _split2(x):
    hi = x.astype(BF16)
    lo = (x - hi.astype(F32)).astype(BF16)
    return hi, lo


def _dot3(a, b):
    ah, al = _split2(a)
    bh, bl = _split2(b)
    return _dot(ah, bh) + _dot(ah, bl) + _dot(al, bh)


def _silu(x):
    return x * (0.5 * jnp.tanh(0.5 * x) + 0.5)


def _layer_norm(x, g, b):
    mu = jnp.mean(x, axis=-1, keepdims=True)
    xc = x - mu
    var = jnp.mean(xc * xc, axis=-1, keepdims=True)
    return xc * lax.rsqrt(var + LN_EPS) * g + b


def _ada_kernel(c_ref, w_ref, b_ref, o_ref):
    o_ref[...] = _dot3(_silu(c_ref[...]), w_ref[...]) + b_ref[...]


def _ada(cond, w_ada, b_ada):
    rows = cond.shape[0]
    n_out = w_ada.shape[1]
    tn = ADA_COL_TILE
    return pl.pallas_call(
        _ada_kernel,
        grid=(n_out // tn,),
        in_specs=[
            pl.BlockSpec((rows, D_MODEL), lambda j: (0, 0)),
            pl.BlockSpec((D_MODEL, tn), lambda j: (0, j)),
            pl.BlockSpec((1, tn), lambda j: (0, j)),
        ],
        out_specs=pl.BlockSpec((rows, tn), lambda j: (0, j)),
        out_shape=jax.ShapeDtypeStruct((rows, n_out), F32),
        compiler_params=_params(1),
        name="ada",
    )(cond, w_ada, b_ada)


def _log_sigmoid(z):
    return jnp.minimum(z, 0.0) - jnp.log(1.0 + jnp.exp(-jnp.abs(z)))


def _proj_kernel(latent, tm, n_cast, x_ref, mod_ref, lnp_ref, w_ref, cw_ref, cb_ref, w2_ref, gbias_ref, *refs):
    cast_refs, out_refs, cast_out_refs = refs[:n_cast], refs[n_cast:len(refs) - n_cast], refs[len(refs) - n_cast:]
    for src_ref, dst_ref in zip(cast_refs, cast_out_refs):
        dst_ref[...] = src_ref[...].astype(BF16)
    x = x_ref[...]
    xn = _layer_norm(x, lnp_ref[0:1, :], lnp_ref[1:2, :])
    h = xn * mod_ref[0:1, :] + mod_ref[1:2, :]
    hb = h.astype(BF16)
    if latent:
        ya_ref, q_ref, k_ref, v_ref, sr_ref, g_ref, xn_ref = out_refs
        xn_ref[...] = DEEPNORM_ALPHA * xn
        p = _dot(hb, w_ref[:, OFF_AB:OFF_Q])
        a_b = p[:, 0:CONV_CH]
        u = p[:, CONV_CH:2 * CONV_CH] * p[:, 2 * CONV_CH:3 * CONV_CH]
        pos = _mod_pow2(lax.broadcasted_iota(jnp.int32, (tm, 1), 0), GRID_W)
        u_prev = jnp.where(pos == 0, 0.0, pltpu.roll(u, 1, 0))
        u_next = jnp.where(pos == GRID_W - 1, 0.0, pltpu.roll(u, tm - 1, 0))
        conv = u_prev * cw_ref[0:1, :] + u * cw_ref[1:2, :] + u_next * cw_ref[2:3, :] + cb_ref[...]
        ya_ref[...] = (a_b * conv).astype(BF16)
        qk = _dot(hb, w_ref[:, OFF_Q:OFF_V])
        q_ref[...] = (qk[:, 0:GLA_KEY] * (GLA_DK ** -0.5)).astype(BF16)
        k_ref[...] = qk[:, GLA_KEY:].astype(BF16)
        r = _dot(hb, w_ref[:, OFF_R:OFF_GF])
        sr_ref[...] = _silu(r).astype(BF16)
    else:
        k_ref, v_ref, g_ref = out_refs
        k_ref[...] = _dot(hb, w_ref[:, OFF_K:OFF_V]).astype(BF16)
    v_ref[...] = _dot(hb, w_ref[:, OFF_V:OFF_R]).astype(BF16)
    low = _dot(hb, w_ref[:, OFF_GF:D_PROJ])
    z = _dot(low.astype(BF16), w2_ref[...]) + gbias_ref[...]
    g_ref[...] = _log_sigmoid(z) * (1.0 / GLA_TAU)


def _proj(x, mod, lnp, w_in_b, conv_w, conv_b, w2cat, gbias, latent, to_cast=()):
    bsz, t, _ = x.shape
    tm = min(TOKEN_TILE, t)
    assert t % tm == 0 and tm % GRID_W == 0
    n_t = t // tm
    tok = lambda w: pl.BlockSpec((None, tm, w), lambda b, i: (b, i, 0))
    full = lambda a: pl.BlockSpec(a.shape, lambda b, i: (0,) * a.ndim)
    sliced = [a.reshape(bsz * n_t, -1, a.shape[-1]) for a in to_cast]
    step = lambda a: pl.BlockSpec((None,) + a.shape[1:], lambda b, i: (b * n_t + i, 0, 0))
    widths = ([(CONV_CH, BF16), (GLA_KEY, BF16)] if latent else []) + [(GLA_KEY, BF16), (GLA_VAL, BF16)]
    widths += ([(GLA_VAL, BF16)] if latent else []) + [(2 * GLA_KEY, F32)]
    widths += [(D_MODEL, F32)] if latent else []
    outs = pl.pallas_call(
        functools.partial(_proj_kernel, latent, tm, len(sliced)),
        grid=(bsz, n_t),
        in_specs=[
            tok(D_MODEL),
            pl.BlockSpec((None, 2, D_MODEL), lambda b, i: (b, 0, 0)),
            full(lnp), full(w_in_b), full(conv_w), full(conv_b), full(w2cat), full(gbias),
        ] + [step(a) for a in sliced],
        out_specs=[tok(w) for w, _ in widths] + [step(a) for a in sliced],
        out_shape=([jax.ShapeDtypeStruct((bsz, t, w), dt) for w, dt in widths]
                   + [jax.ShapeDtypeStruct(a.shape, BF16) for a in sliced]),
        compiler_params=_params(2),
        name="proj_latent" if latent else "proj_ctx",
    )(x, mod, lnp, w_in_b, conv_w, conv_b, w2cat, gbias, *sliced)
    return list(outs[:len(widths)]) + [o.reshape(a.shape) for o, a in zip(outs[len(widths):], to_cast)]


def _tri(n, reverse, strict=False):
    i = lax.broadcasted_iota(jnp.int32, (n, n), 0)
    j = lax.broadcasted_iota(jnp.int32, (n, n), 1)
    if strict:
        m = (j > i) if reverse else (j < i)
    else:
        m = (j >= i) if reverse else (j <= i)
    return jnp.where(m, 1.0, 0.0).astype(BF16)


def _chunk_cumsum(g, reverse):
    tri = _tri(GLA_CHUNK, reverse)
    g_hi, g_lo = _split2(g)
    return _dot(tri, g_hi) + _dot(tri, g_lo)


def _as_column(row):
    return jnp.broadcast_to(row, (LANES, row.shape[1])).T


def _sub_anchors(gc, reverse):
    zero = jnp.zeros((1, GLA_KEY), F32)
    if reverse:
        return [gc[GLA_SUB * (a + 1):GLA_SUB * (a + 1) + 1] for a in range(N_SUB - 1)] + [zero]
    return [zero] + [gc[GLA_SUB * a - 1:GLA_SUB * a] for a in range(1, N_SUB)]


def _score_pairs(reverse):
    return [(a, b) for a in range(N_SUB) for b in range(N_SUB) if (b >= a if reverse else b <= a)]


def _intra_products(q, k, gc, reverse):
    r = _sub_anchors(gc, reverse)
    anchor = jnp.concatenate([jnp.broadcast_to(ra, (GLA_SUB, GLA_KEY)) for ra in r], axis=0)
    gcb = gc - anchor
    qt = q * jnp.exp(gcb)
    kt = k * jnp.exp(-gcb)
    rows = []
    for a, b in _score_pairs(reverse):
        qa = qt[GLA_SUB * a:GLA_SUB * (a + 1)]
        if a != b:
            qa = qa * jnp.exp(r[a] - r[b])
        rows.append(qa)
    qp = jnp.concatenate(rows, axis=0).astype(BF16)
    width = GLA_HEADS * GLA_CHUNK
    rr = lax.broadcasted_iota(jnp.int32, (width, GLA_KEY), 0)
    cc = lax.broadcasted_iota(jnp.int32, (width, GLA_KEY), 1)
    kbd = jnp.where(_div_pow2(rr, GLA_CHUNK) == _div_pow2(cc, GLA_DK),
                    jnp.concatenate([kt] * GLA_HEADS, axis=0), 0.0)
    return lax.dot_general(qp, kbd.astype(BF16), (((1,), (1,)), ((), ())), preferred_element_type=F32)


def _assemble_scores(res, reverse):
    pairs = _score_pairs(reverse)
    width = GLA_HEADS * GLA_CHUNK
    col = _mod_pow2(lax.broadcasted_iota(jnp.int32, (GLA_SUB, width), 1), GLA_CHUNK)
    col_blk = _div_pow2(col, GLA_SUB)
    col_pos = _mod_pow2(col, GLA_SUB)
    row_pos = lax.broadcasted_iota(jnp.int32, (GLA_SUB, width), 0)
    causal = (col_pos >= row_pos) if reverse else (col_pos <= row_pos)
    blocks = []
    for a in range(N_SUB):
        acc = jnp.zeros((GLA_SUB, width), F32)
        for idx, (pa, pb) in enumerate(pairs):
            if pa != a:
                continue
            keep = col_blk == pb
            if pa == pb:
                keep = keep & causal
            acc = acc + jnp.where(keep, res[GLA_SUB * idx:GLA_SUB * (idx + 1)], 0.0)
        blocks.append(acc)
    return jnp.concatenate(blocks, axis=0)


def _pair_mask(rows_per_head, cols_per_head, n_row_pairs=1):
    shape = (n_row_pairs * 2 * rows_per_head, 2 * cols_per_head)
    rr = _mod_pow2(lax.broadcasted_iota(jnp.int32, shape, 0), 2 * rows_per_head)
    cc = lax.broadcasted_iota(jnp.int32, shape, 1)
    return _div_pow2(rr, rows_per_head) == _div_pow2(cc, cols_per_head)


def _state_terms(q, k, v_b, gc, reverse):
    total = gc[0:1] if reverse else gc[GLA_CHUNK - 1:GLA_CHUNK]
    q_dec = None if q is None else (q * jnp.exp(gc)).astype(BF16)
    k_end = (k * jnp.exp(total - gc)).astype(BF16)
    tn = (((0,), (0,)), ((), ()))
    upd = [lax.dot_general(k_end[:, p * PAIR_KEY:(p + 1) * PAIR_KEY], v_b[:, p * PAIR_VAL:(p + 1) * PAIR_VAL], tn,
                           preferred_element_type=F32) for p in range(GLA_HEADS // 2)]
    upd = jnp.where(_pair_mask(GLA_DK, GLA_DV, GLA_HEADS // 2), jnp.concatenate(upd, axis=0), 0.0)
    decay = jnp.exp(_as_column(total))
    decay = jnp.concatenate([decay] * (PAIR_VAL // LANES), axis=1)
    return q_dec, decay, upd


def _advance_state(q_dec, decay, upd, state):
    o_inter = None
    if q_dec is not None:
        state_b = state.astype(BF16)
        o_inter = jnp.concatenate(
            [_dot(q_dec[:, p * PAIR_KEY:(p + 1) * PAIR_KEY], state_b[p * PAIR_KEY:(p + 1) * PAIR_KEY, :])
             for p in range(GLA_HEADS // 2)], axis=1)
    return o_inter, state * decay + upd


def _gla_kernel(tt, nt, with_out, *refs):
    if with_out:
        (qf_ref, kf_ref, vf_ref, gfw_ref, qb_ref, kb_ref, vb_ref, gbw_ref, s0f_ref, s0b_ref,
         of_ref, ob_ref, sf_ref, sb_ref) = refs
    else:
        kf_ref, vf_ref, gfw_ref, kb_ref, vb_ref, gbw_ref, s0f_ref, s0b_ref, sf_ref, sb_ref = refs
    j = pl.program_id(1)

    @pl.when(j == 0)
    def _():
        sf_ref[...] = s0f_ref[...]
        sb_ref[...] = s0b_ref[...]

    chunk_slices = [slice(c * GLA_CHUNK, (c + 1) * GLA_CHUNK) for c in range(tt // GLA_CHUNK)]
    fwd, bwd = [], []
    for sl in chunk_slices:
        g = gfw_ref[sl, :]
        fwd.append(dict(sl=sl, k=kf_ref[sl, :].astype(F32), v=vf_ref[sl, :], g_f=g[:, 0:GLA_KEY], g_b=g[:, GLA_KEY:],
                        q=qf_ref[sl, :].astype(F32) if with_out else None))
        bwd.append(dict(sl=sl, k=kb_ref[sl, :].astype(F32), v=vb_ref[sl, :], g_b=gbw_ref[sl, :][:, GLA_KEY:],
                        q=qb_ref[sl, :].astype(F32) if with_out else None))
    for d in fwd:
        d["gc_f"] = _chunk_cumsum(d["g_f"], False)
        if with_out:
            d["gc_b"] = _chunk_cumsum(d["g_b"], True)
    for d in bwd:
        d["gc_b"] = _chunk_cumsum(d["g_b"], True)
    if with_out:
        for d in fwd:
            d["res_f"] = _intra_products(d["q"], d["k"], d["gc_f"], False)
            d["res_b"] = _intra_products(d["q"], d["k"], d["gc_b"], True)
    for d in fwd:
        d["terms"] = _state_terms(d["q"], d["k"], d["v"], d["gc_f"], False)
    for d in bwd:
        d["terms"] = _state_terms(d["q"], d["k"], d["v"], d["gc_b"], True)
    if with_out:
        for d in fwd:
            scores = (_assemble_scores(d["res_f"], False) + _assemble_scores(d["res_b"], True)).astype(BF16)
            o_intra = []
            for p in range(GLA_HEADS // 2):
                v_p = d["v"][:, p * PAIR_VAL:(p + 1) * PAIR_VAL]
                vbd = jnp.where(_pair_mask(GLA_CHUNK, GLA_DV), jnp.concatenate([v_p, v_p], axis=0),
                                jnp.zeros((), BF16))
                o_intra.append(_dot(scores[:, p * 2 * GLA_CHUNK:(p + 1) * 2 * GLA_CHUNK], vbd))
            d["o_intra"] = jnp.concatenate(o_intra, axis=1)

    state = sf_ref[...]
    for d in fwd:
        o_inter, state = _advance_state(*d["terms"], state)
        if with_out:
            of_ref[d["sl"], :] = (d["o_intra"] + o_inter).astype(BF16)
    sf_ref[...] = state

    state = sb_ref[...]
    for d in reversed(bwd):
        o_inter, state = _advance_state(*d["terms"], state)
        if with_out:
            ob_ref[d["sl"], :] = o_inter.astype(BF16)
    sb_ref[...] = state


def _gla(q, k, v, g, s0f, s0b):
    with_out = q is not None
    bsz, t, _ = k.shape
    tt = min(TOKEN_TILE, t)
    assert t % tt == 0 and tt % GLA_CHUNK == 0
    nt = t // tt
    fwd = lambda w: pl.BlockSpec((None, tt, w), lambda b, j: (b, j, 0))
    bwd = lambda w: pl.BlockSpec((None, tt, w), lambda b, j: (b, nt - 1 - j, 0))
    st = pl.BlockSpec((None, GLA_KEY, PAIR_VAL), lambda b, j: (b, 0, 0))
    st_shape = jax.ShapeDtypeStruct((bsz, GLA_KEY, PAIR_VAL), F32)
    if with_out:
        ins = [q, k, v, g, q, k, v, g, s0f, s0b]
        in_specs = [fwd(GLA_KEY), fwd(GLA_KEY), fwd(GLA_VAL), fwd(2 * GLA_KEY),
                    bwd(GLA_KEY), bwd(GLA_KEY), bwd(GLA_VAL), bwd(2 * GLA_KEY), st, st]
        out_specs = [fwd(GLA_VAL), bwd(GLA_VAL), st, st]
        o_shape = jax.ShapeDtypeStruct((bsz, t, GLA_VAL), BF16)
        out_shape = [o_shape, o_shape, st_shape, st_shape]
    else:
        ins = [k, v, g, k, v, g, s0f, s0b]
        in_specs = [fwd(GLA_KEY), fwd(GLA_VAL), fwd(2 * GLA_KEY),
                    bwd(GLA_KEY), bwd(GLA_VAL), bwd(2 * GLA_KEY), st, st]
        out_specs = [st, st]
        out_shape = [st_shape, st_shape]
    return pl.pallas_call(
        functools.partial(_gla_kernel, tt, nt, with_out),
        grid=(bsz, nt),
        in_specs=in_specs,
        out_specs=out_specs,
        out_shape=out_shape,
        compiler_params=_params(2),
        name="gla_latent" if with_out else "gla_ctx",
    )(*ins)


def _exact_bf16_parts(x):
    hi = x.astype(BF16).astype(F32)
    r = x - hi
    mid = r.astype(BF16).astype(F32)
    lo = (r - mid).astype(BF16).astype(F32)
    return hi, mid, lo


def _first_index(values, best):
    idx = jnp.full_like(best, float(len(values) - 1))
    for i in reversed(range(len(values) - 1)):
        idx = jnp.where(values[i] >= best, float(i), idx)
    return idx


def _pick(rows, idx):
    out = rows[-1]
    for i in reversed(range(len(rows) - 1)):
        out = jnp.where(idx == float(i), rows[i], out)
    return out


def _route(logit_t, tm):
    row = lambda r: logit_t[r:r + 1, :]
    groups = [row(i) for i in range(N_GROUPS)]
    top = functools.reduce(jnp.maximum, groups)
    eg = [jnp.exp(x - top) for x in groups]
    total = functools.reduce(lambda a, b: a + b, eg)
    pg = [e / total for e in eg]
    p_g = functools.reduce(jnp.maximum, pg)
    g_idx = _first_index(pg, p_g)
    sel = [_pick([row(N_GROUPS + EXPERTS_PER_GROUP * g + j) for g in range(N_GROUPS)], g_idx)
           for j in range(EXPERTS_PER_GROUP)]
    top = functools.reduce(jnp.maximum, sel)
    ee = [jnp.exp(x - top) for x in sel]
    total = functools.reduce(lambda a, b: a + b, ee)
    pe = [e / total for e in ee]
    p1 = functools.reduce(jnp.maximum, pe)
    l1 = _first_index(pe, p1)
    pe2 = [jnp.where(l1 == float(j), -1.0, pe[j]) for j in range(EXPERTS_PER_GROUP)]
    p2 = functools.reduce(jnp.maximum, pe2)
    l2 = _first_index(pe2, p2)
    den = p1 + p2
    w1 = p1 / den * p_g
    w2 = p2 / den * p_g
    lo = jnp.minimum(l1, l2)
    hi = jnp.maximum(l1, l2)
    pair = lo * (7.0 - lo) * 0.5 + (hi - lo - 1.0)
    cls = g_idx * PAIRS_PER_GROUP + pair
    w_lo = jnp.where(l1 < l2, w1, w2)
    w_hi = jnp.where(l1 < l2, w2, w1)
    cls_id = lax.broadcasted_iota(jnp.int32, (ROUTE_W, tm), 0).astype(F32)
    onehot = jnp.where(cls_id == cls, 1.0, 0.0)
    before = _dot(onehot.astype(BF16), _tri(tm, True, strict=True))
    count = jnp.sum(onehot, axis=1, keepdims=True)
    chunks = jnp.floor((count + (CHUNK_ROWS - 1.0)) * (1.0 / CHUNK_ROWS))
    first_chunk = _dot(_tri(ROUTE_W, False, strict=True),
                       jnp.broadcast_to(chunks, (ROUTE_W, LANES)).astype(BF16))[:, 0:1]
    pos_row = jnp.sum(onehot * (CHUNK_ROWS * first_chunk + before), axis=0, keepdims=True)
    return pos_row, w_lo, w_hi, chunks


def _slab_targets(slab_axis):
    shape = (LOCAL_SLAB_ROWS, 1) if slab_axis == 0 else (1, LOCAL_SLAB_ROWS)
    slab_row = lax.broadcasted_iota(jnp.int32, shape, slab_axis)
    sub = _mod_pow2(slab_row, SLAB_ROWS)
    token_row = (CHUNK_ROWS * _div_pow2(slab_row, SLAB_ROWS) + _div_pow2(sub, 2)).astype(F32)
    half = _mod_pow2(sub, 2)
    return [jnp.where(half == h, token_row, -1.0) for h in range(2)]


def _slab_sort_matrices(pos, targets):
    return [jnp.where(t == pos, 1.0, 0.0).astype(BF16) for t in targets]


def _mix_out_kernel(tm, xn_ref, of_ref, ob_ref, sr_ref, ya_ref, mod_ref, lnp_ref, gn_ref, wo_ref, wr_ref, br_ref,
                    x1_ref, hxs_ref, pos_ref, chunks_ref):
    subs = [slice(s, s + MIX_SUB_TILE) for s in range(0, tm, MIX_SUB_TILE)]
    yb = []
    for rows in subs:
        o = of_ref[rows, :].astype(F32) + ob_ref[rows, :].astype(F32)
        sr = sr_ref[rows, :].astype(F32)
        heads = []
        for h in range(GLA_HEADS):
            sl = slice(h * GLA_DV, (h + 1) * GLA_DV)
            oh = o[:, sl]
            ms = jnp.mean(oh * oh, axis=-1, keepdims=True)
            heads.append((oh * lax.rsqrt(ms + RMS_EPS) * gn_ref[...] * sr[:, sl]).astype(BF16))
        yb.append(jnp.concatenate([ya_ref[rows, :]] + heads, axis=1))
    xn = [xn_ref[rows, :] for rows in subs]
    y = [_dot(y_in, wo_ref[...]) for y_in in yb]
    h2_b = []
    for rows, xn_s, y_s in zip(subs, xn, y):
        x1 = _layer_norm(xn_s + mod_ref[0:1, :] * y_s, lnp_ref[0:1, :], lnp_ref[1:2, :])
        x1_ref[rows, :] = x1.astype(BF16)
        h2_b.append((x1 * mod_ref[1:2, :] + mod_ref[2:3, :]).astype(BF16))
    logit_t = [(_dot(h2_s, wr_ref[...]) + br_ref[...]).T for h2_s in h2_b]
    per_sort = SORT_TILE // MIX_SUB_TILE
    routes = []
    for s in range(tm // SORT_TILE):
        routes.append(_route(jnp.concatenate(logit_t[s * per_sort:(s + 1) * per_sort], axis=1), SORT_TILE))
    rec_id = lax.broadcasted_iota(jnp.int32, (ROUTE_W, SORT_TILE), 0)
    targets = _slab_targets(0)
    for s, (pos_row, w_lo, w_hi, chunks) in enumerate(routes):
        rec_t = jnp.zeros((ROUTE_W, SORT_TILE), F32)
        for i, part in enumerate(_exact_bf16_parts(w_lo) + _exact_bf16_parts(w_hi)):
            rec_t = jnp.where(rec_id == i, part, rec_t)
        rec_b = rec_t.T.astype(BF16)
        h2_s = jnp.concatenate(h2_b[s * per_sort:(s + 1) * per_sort], axis=0)
        sort_lo, sort_hi = _slab_sort_matrices(pos_row, targets)
        pay_lo = jnp.concatenate([h2_s[:, 0:HALF_W], rec_b], axis=1)
        pay_hi = jnp.concatenate([h2_s[:, HALF_W:], jnp.zeros((SORT_TILE, ROUTE_W), BF16)], axis=1)
        slabs = _dot(sort_lo, pay_lo) + _dot(sort_hi, pay_hi)
        for c in range(SLAB_IN_W // LANES):
            hxs_ref[c, s * LOCAL_SLAB_ROWS:(s + 1) * LOCAL_SLAB_ROWS, :] = slabs[:, c * LANES:(c + 1) * LANES]
        pos_ref[8 * s:8 * (s + 1), :] = jnp.broadcast_to(pos_row, (8, SORT_TILE))
        chunks_ref[8 * s:8 * (s + 1), :] = jnp.broadcast_to(chunks, (ROUTE_W, LANES)).T[0:8, :]


def _mix_out(xn, o_f, o_b, sr, ya, mod, lnp, gn, w_out_b, wr, br):
    bsz, t, _ = xn.shape
    assert t % SORT_TILE == 0
    n_sort = max(n for n in (1, 2, 4) if n * SORT_TILE <= TOKEN_TILE and t % (n * SORT_TILE) == 0)
    tm = n_sort * SORT_TILE
    n_t = t // tm
    tok = lambda w: pl.BlockSpec((None, tm, w), lambda b, i: (b, i, 0))
    full = lambda a: pl.BlockSpec(a.shape, lambda b, i: (0,) * a.ndim)
    flat = lambda rows, w: pl.BlockSpec((rows, w), lambda b, i: (b * n_t + i, 0))
    return pl.pallas_call(
        functools.partial(_mix_out_kernel, tm),
        grid=(bsz, n_t),
        in_specs=[
            tok(D_MODEL), tok(GLA_VAL), tok(GLA_VAL), tok(GLA_VAL), tok(CONV_CH),
            pl.BlockSpec((None, 3, D_MODEL), lambda b, i: (b, 0, 0)),
            full(lnp), full(gn), full(w_out_b), full(wr), full(br),
        ],
        out_specs=[tok(D_MODEL),
                   pl.BlockSpec((SLAB_IN_W // LANES, n_sort * LOCAL_SLAB_ROWS, LANES),
                                lambda b, i: (0, b * n_t + i, 0)),
                   flat(n_sort * 8, SORT_TILE), flat(n_sort * 8, ROUTE_W)],
        out_shape=[
            jax.ShapeDtypeStruct((bsz, t, D_MODEL), BF16),
            jax.ShapeDtypeStruct((SLAB_IN_W // LANES, bsz * t // SORT_TILE * LOCAL_SLAB_ROWS, LANES), F32),
            jax.ShapeDtypeStruct((bsz * t // SORT_TILE * 8, SORT_TILE), F32),
            jax.ShapeDtypeStruct((bsz * t // SORT_TILE * 8, ROUTE_W), F32),
        ],
        compiler_params=_params(2),
        name="mix_out",
    )(xn, o_f, o_b, sr, ya, mod, lnp, gn, w_out_b, wr, br)


def _moe_kernel(n_chunks, n_steps, nused_ref, lo_ref, hi_ref, src_ref, dst_ref,
                hxs_hbm, w1_hbm, w3_hbm, w2_hbm, out_hbm, gbuf, obuf, w1buf, w3buf, w2buf, gsem, ssem, wsem):
    tile_rows = CHUNKS_PER_TILE * SLAB_ROWS
    n_used = nused_ref[0]
    expert_tabs = (lo_ref, hi_ref)

    def weight_copies(role, expert, wslot):
        return [pltpu.make_async_copy(w_hbm.at[expert], wbuf.at[role, wslot], wsem.at[role, wslot])
                for w_hbm, wbuf in ((w1_hbm, w1buf), (w3_hbm, w3buf), (w2_hbm, w2buf))]

    def slab(chunk):
        return pl.ds(pl.multiple_of(chunk * SLAB_ROWS, SLAB_ROWS), SLAB_ROWS)

    def gather_copy(tile, buf_slot, j):
        chunk = src_ref[tile * CHUNKS_PER_TILE + j]
        return pltpu.make_async_copy(hxs_hbm.at[:, slab(chunk), :], gbuf.at[buf_slot, :, slab(j), :],
                                     gsem.at[buf_slot])

    def scatter_copy(tile, buf_slot, j):
        chunk = dst_ref[tile * CHUNKS_PER_TILE + j]
        return pltpu.make_async_copy(obuf.at[buf_slot, :, slab(j), :], out_hbm.at[:, slab(chunk), :],
                                     ssem.at[buf_slot])

    def start_gather(tile, buf_slot):
        for j in range(CHUNKS_PER_TILE):
            gather_copy(tile, buf_slot, j).start(priority=j % 2)

    def wait_gather(buf_slot):
        pltpu.make_async_copy(hxs_hbm.at[:, pl.ds(0, tile_rows), :], gbuf.at[buf_slot], gsem.at[buf_slot]).wait()

    def wait_scatter(buf_slot):
        pltpu.make_async_copy(obuf.at[buf_slot], out_hbm.at[:, pl.ds(0, tile_rows), :], ssem.at[buf_slot]).wait()

    start_gather(0, 0)
    for role, tab_ref in enumerate(expert_tabs):
        for copy in weight_copies(role, tab_ref[0], 0):
            copy.start()
    obuf[...] = jnp.zeros(obuf.shape, F32)
    min_chunks = SORT_TILE // CHUNK_ROWS
    tail_rows = (LOCAL_CHUNKS - min_chunks) * SLAB_ROWS
    fills = [pltpu.make_async_copy(
        obuf.at[1], out_hbm.at[:, pl.ds((n_chunks + s * CHUNKS_PER_TILE) * SLAB_ROWS, tile_rows), :],
        ssem.at[s]) for s in range(2)]
    fills += [pltpu.make_async_copy(
        obuf.at[1, :, pl.ds(0, tail_rows), :],
        out_hbm.at[:, pl.ds((t * LOCAL_CHUNKS + min_chunks) * SLAB_ROWS, tail_rows), :], ssem.at[0])
        for t in range(n_chunks // LOCAL_CHUNKS)]
    for fill in fills:
        fill.start()

    def tile_step(i, wslots):
        slot = lax.bitwise_and(i, 1)

        @pl.when(i + 1 < n_used)
        def _():
            start_gather(i + 1, 1 - slot)

        new_wslots = []
        for role, (tab_ref, prev) in enumerate(zip(expert_tabs, wslots)):
            expert = tab_ref[i]
            changed = jnp.logical_or(i == 0, expert != tab_ref[jnp.maximum(i - 1, 0)])
            cur = jnp.where(changed, 1 - prev, prev)

            @pl.when(changed)
            def _():
                for copy in weight_copies(role, expert, cur):
                    copy.wait()

            nxt = tab_ref[jnp.minimum(i + 1, n_steps - 1)]

            @pl.when(jnp.logical_and(i + 1 < n_used, nxt != expert))
            def _():
                for copy in weight_copies(role, nxt, 1 - cur):
                    copy.start()

            new_wslots.append(cur)

        wait_gather(slot)

        @pl.when(i >= 2)
        def _():
            wait_scatter(slot)

        def lane_block(c, half):
            return jnp.concatenate(
                [gbuf[slot, c, pl.ds(2 * r + half, CHUNKS_PER_TILE, stride=SLAB_ROWS), :]
                 for r in range(CHUNK_ROWS)], axis=0)

        n_blk = HALF_W // LANES
        xb = jnp.concatenate([lane_block(c, 0) for c in range(n_blk)]
                             + [lane_block(c, 1) for c in range(n_blk)], axis=1).astype(BF16)
        rec = lane_block(n_blk, 0)
        w_lo = rec[:, 0:1] + rec[:, 1:2] + rec[:, 2:3]
        w_hi = rec[:, 3:4] + rec[:, 4:5] + rec[:, 5:6]

        gate = [_dot(xb, w1buf[role, ws]) for role, ws in enumerate(new_wslots)]
        up = [_dot(xb, w3buf[role, ws]) for role, ws in enumerate(new_wslots)]
        act = [(_silu(g) * u).astype(BF16) for g, u in zip(gate, up)]
        e_lo, e_hi = [_dot(a, w2buf[role, ws]) for a, (role, ws) in zip(act, enumerate(new_wslots))]
        y = w_lo * e_lo + w_hi * e_hi
        for r in range(CHUNK_ROWS):
            rows = slice(r * CHUNKS_PER_TILE, (r + 1) * CHUNKS_PER_TILE)
            for half in range(2):
                for c in range(n_blk):
                    col = half * HALF_W + c * LANES
                    obuf[slot, c, pl.ds(2 * r + half, CHUNKS_PER_TILE, stride=SLAB_ROWS), :] = (
                        y[rows, col:col + LANES])

        @pl.when(i == 0)
        def _():
            for fill in fills:
                fill.wait()

        for j in range(CHUNKS_PER_TILE):
            scatter_copy(i, slot, j).start(priority=j % 2)
        return tuple(new_wslots)

    one = jnp.ones((), jnp.int32)
    lax.fori_loop(0, n_used, tile_step, (one, one))

    last_slot = lax.bitwise_and(n_used - 1, 1)
    wait_scatter(last_slot)

    @pl.when(n_used >= 2)
    def _():
        wait_scatter(1 - last_slot)


def _moe(hxs, src, dst, n_used, tile_lo, tile_hi, w1_b, w3_b, w2_b):
    n_chunks = hxs.shape[1] // SLAB_ROWS
    tile_rows = CHUNKS_PER_TILE * SLAB_ROWS
    n_steps = src.shape[0] // CHUNKS_PER_TILE
    grid_spec = pltpu.PrefetchScalarGridSpec(
        num_scalar_prefetch=5,
        grid=(1,),
        in_specs=[pl.BlockSpec(memory_space=pl.ANY)] * 4,
        out_specs=pl.BlockSpec(memory_space=pl.ANY),
        scratch_shapes=[
            pltpu.VMEM((2, SLAB_IN_W // LANES, tile_rows, LANES), F32),
            pltpu.VMEM((2, SLAB_OUT_W // LANES, tile_rows, LANES), F32),
            pltpu.VMEM((2, 2, D_MODEL, D_EXPERT), BF16),
            pltpu.VMEM((2, 2, D_MODEL, D_EXPERT), BF16),
            pltpu.VMEM((2, 2, D_EXPERT, D_MODEL), BF16),
            pltpu.SemaphoreType.DMA((2,)),
            pltpu.SemaphoreType.DMA((2,)),
            pltpu.SemaphoreType.DMA((2, 2)),
        ],
    )
    return pl.pallas_call(
        functools.partial(_moe_kernel, n_chunks, n_steps),
        grid_spec=grid_spec,
        out_shape=jax.ShapeDtypeStruct((SLAB_OUT_W // LANES, (n_chunks + 2 * CHUNKS_PER_TILE) * SLAB_ROWS, LANES), F32),
        compiler_params=_params(1),
        name="moe",
    )(n_used, tile_lo, tile_hi, src, dst, hxs, w1_b, w3_b, w2_b)


def _final_kernel(n_sort, x1_ref, moe_ref, pos_ref, mod_ref, lnp_ref, o_ref):
    moe = []
    targets = _slab_targets(1)
    for s in range(n_sort):
        slab_rows = slice(s * LOCAL_SLAB_ROWS, (s + 1) * LOCAL_SLAB_ROWS)
        moe_b = jnp.concatenate([moe_ref[c, slab_rows, :] for c in range(SLAB_OUT_W // LANES)], axis=1).astype(BF16)
        sort_lo, sort_hi = _slab_sort_matrices(_as_column(pos_ref[8 * s:8 * s + 1, :])[:, 0:1], targets)
        for r in range(0, SORT_TILE, MIX_SUB_TILE):
            rows = slice(r, r + MIX_SUB_TILE)
            moe.append((s * SORT_TILE + r,
                        jnp.concatenate([_dot(sort_lo[rows, :], moe_b), _dot(sort_hi[rows, :], moe_b)], axis=1)))
    for start, moe_s in moe:
        rows = slice(start, start + MIX_SUB_TILE)
        o_ref[rows, :] = _layer_norm(DEEPNORM_ALPHA * x1_ref[rows, :].astype(F32) + mod_ref[...] * moe_s,
                                     lnp_ref[0:1, :], lnp_ref[1:2, :])


def _final(x1, moe, pos, g2, lnp):
    bsz, t, _ = x1.shape
    n_sort = max(n for n in (1, 2, 4) if n * SORT_TILE <= TOKEN_TILE and t % (n * SORT_TILE) == 0)
    tm = n_sort * SORT_TILE
    n_t = t // tm
    flat = lambda rows, w: pl.BlockSpec((rows, w), lambda b, i: (b * n_t + i, 0))
    return pl.pallas_call(
        functools.partial(_final_kernel, n_sort),
        grid=(bsz, n_t),
        in_specs=[
            pl.BlockSpec((None, tm, D_MODEL), lambda b, i: (b, i, 0)),
            pl.BlockSpec((SLAB_OUT_W // LANES, n_sort * LOCAL_SLAB_ROWS, LANES), lambda b, i: (0, b * n_t + i, 0)),
            flat(n_sort * 8, SORT_TILE),
            pl.BlockSpec((None, 1, D_MODEL), lambda b, i: (b, 0, 0)),
            pl.BlockSpec(lnp.shape, lambda b, i: (0, 0)),
        ],
        out_specs=pl.BlockSpec((None, tm, D_MODEL), lambda b, i: (b, i, 0)),
        out_shape=jax.ShapeDtypeStruct((bsz, t, D_MODEL), F32),
        compiler_params=_params(2),
        name="final",
    )(x1, moe, pos, g2, lnp)


def _pair_tables():
    lo, hi = [], []
    for g in range(N_GROUPS):
        for a in range(EXPERTS_PER_GROUP):
            for b in range(a + 1, EXPERTS_PER_GROUP):
                lo.append(g * EXPERTS_PER_GROUP + a)
                hi.append(g * EXPERTS_PER_GROUP + b)
    return jnp.array(lo, jnp.int32), jnp.array(hi, jnp.int32)


def _moe_plan(chunks, n_sort_tiles):
    n_cls = N_CLASSES
    hp = lax.Precision.HIGHEST
    m = chunks.reshape(n_sort_tiles, 8, ROUTE_W)[:, 0, :n_cls].astype(jnp.int32)
    a_end = jnp.cumsum(m, axis=0)
    a_start = a_end - m
    per_cls = a_end[-1]
    padded = (per_cls + CHUNKS_PER_TILE - 1) // CHUNKS_PER_TILE * CHUNKS_PER_TILE
    g_end = jnp.cumsum(padded)
    g_start = g_end - padded
    local_off = jnp.cumsum(m, axis=1) - m
    seg = jnp.arange(n_sort_tiles, dtype=jnp.int32)[:, None] * LOCAL_CHUNKS + local_off - a_start
    n_steps = -(-(n_sort_tiles * LOCAL_CHUNKS) // CHUNKS_PER_TILE) + n_cls
    p = jnp.arange(n_steps * CHUNKS_PER_TILE, dtype=jnp.int32)
    cls_p = jnp.minimum(jnp.sum((g_end[None, :] <= p[:, None]).astype(jnp.int32), axis=1), n_cls - 1)
    onehot = (cls_p[:, None] == jnp.arange(n_cls, dtype=jnp.int32)[None, :]).astype(F32)
    pick = lambda tab: jnp.dot(onehot, tab.astype(F32), precision=hp)
    u = p - pick(g_start[:, None])[:, 0].astype(jnp.int32)
    valid = u < pick(per_cls[:, None])[:, 0].astype(jnp.int32)
    a_end_p = pick(a_end.T).astype(jnp.int32)
    seg_p = pick(seg.T).astype(jnp.int32)
    tile_p = jnp.sum((a_end_p <= u[:, None]).astype(jnp.int32), axis=1)
    hit = jnp.arange(n_sort_tiles, dtype=jnp.int32)[None, :] == tile_p[:, None]
    src = jnp.sum(jnp.where(hit, seg_p, 0), axis=1) + u
    pad_dst = n_sort_tiles * LOCAL_CHUNKS + (p // CHUNKS_PER_TILE) % 2 * CHUNKS_PER_TILE + p % CHUNKS_PER_TILE
    dst = jnp.where(valid, src, pad_dst).astype(jnp.int32)
    src = jnp.where(valid, src, 0).astype(jnp.int32)
    n_used = g_end[-1:] // CHUNKS_PER_TILE
    step = jnp.arange(n_steps, dtype=jnp.int32)
    tile_cls = jnp.sum((g_end[None, :] // CHUNKS_PER_TILE <= step[:, None]).astype(jnp.int32), axis=1)
    pair_lo, pair_hi = _pair_tables()
    pair_oh = (jnp.minimum(tile_cls, N_CLASSES - 1)[:, None] == jnp.arange(N_CLASSES)[None, :]).astype(jnp.int32)
    tile_lo = jnp.sum(pair_oh * pair_lo[None, :], axis=1).astype(jnp.int32)
    tile_hi = jnp.sum(pair_oh * pair_hi[None, :], axis=1).astype(jnp.int32)
    return src, dst, n_used.astype(jnp.int32), tile_lo, tile_hi


def kernel(x, c, ctx, c_ctx, ln_in_g, ln_in_b, w_ada, b_ada, w_in, conv_w, conv_b, gate_w2_fwd, gate_b_fwd,
           gate_w2_bwd, gate_b_bwd, gla_norm_g, w_out, ln1_g, ln1_b, router_group_w, router_group_b,
           router_expert_w, router_expert_b, expert_w1, expert_w3, expert_w2, ln2_g, ln2_b):
    bsz, t, _ = x.shape
    n_tok = bsz * t
    l = 0
    rows = -(-(bsz + 1) // 8) * 8
    cond = jnp.zeros((rows, D_MODEL), F32).at[:bsz].set(c).at[bsz].set(c_ctx)
    ada = _ada(cond, w_ada[l], b_ada[l][None, :])
    sh1, sc1, g1, sh2, sc2, g2 = [ada[:, i * D_MODEL:(i + 1) * D_MODEL] for i in range(6)]

    w_in_b = w_in[l].astype(BF16)
    lnp_in = jnp.stack([ln_in_g, ln_in_b])
    zero = jnp.zeros((GLA_GATE_RANK, GLA_KEY), F32)
    w2cat = jnp.concatenate([jnp.concatenate([gate_w2_fwd[l], zero], axis=1),
                             jnp.concatenate([zero, gate_w2_bwd[l]], axis=1)], axis=0).astype(BF16)
    gbias = jnp.concatenate([gate_b_fwd[l], gate_b_bwd[l]])[None, :]

    n_ctx = ctx.shape[1]
    mod_ctx = jnp.stack([1.0 + sc1[bsz], sh1[bsz]])[None]
    k_c, v_c, g_c = [a.reshape(bsz, n_ctx, -1) for a in _proj(
        ctx.reshape(1, bsz * n_ctx, D_MODEL), mod_ctx, lnp_in, w_in_b, conv_w[l], conv_b[l][None, :], w2cat, gbias,
        False)]
    zero_state = jnp.zeros((bsz, GLA_KEY, PAIR_VAL), F32)
    s_f, s_b = _gla(None, k_c, v_c, g_c, zero_state, zero_state)

    mod1 = jnp.stack([1.0 + sc1[:bsz], sh1[:bsz]], axis=1)
    ya, q, k, v, sr, g, xn, w1_b, w3_b, w2_b = _proj(
        x, mod1, lnp_in, w_in_b, conv_w[l], conv_b[l][None, :], w2cat, gbias, True,
        to_cast=(expert_w1[l], expert_w3[l], expert_w2[l]))
    o_f, o_b, _, _ = _gla(q, k, v, g, s_f, s_b)

    mod2 = jnp.stack([g1[:bsz], 1.0 + sc2[:bsz], sh2[:bsz]], axis=1)
    lnp1 = jnp.stack([ln1_g[l], ln1_b[l]])
    wr = jnp.zeros((D_MODEL, ROUTE_W), F32)
    wr = wr.at[:, :N_GROUPS].set(router_group_w[l]).at[:, N_GROUPS:N_GROUPS + N_EXPERTS].set(router_expert_w[l])
    br = jnp.zeros((1, ROUTE_W), F32)
    br = br.at[0, :N_GROUPS].set(router_group_b[l]).at[0, N_GROUPS:N_GROUPS + N_EXPERTS].set(router_expert_b[l])
    x1, hxs, pos, chunks = _mix_out(xn, o_f, o_b, sr, ya, mod2, lnp1, gla_norm_g[l][None, :],
                                    w_out[l].astype(BF16), wr.astype(BF16), br)

    src, dst, n_used, tile_lo, tile_hi = _moe_plan(chunks, n_tok // SORT_TILE)
    moe = _moe(hxs, src, dst, n_used, tile_lo, tile_hi, w1_b, w3_b, w2_b)

    return _final(x1, moe, pos, g2[:bsz][:, None, :], jnp.stack([ln2_g[l], ln2_b[l]]))
```
